```python
import math
import jax
import jax.numpy as jnp
from jax import lax
import numpy as np


D_MODEL = 1024
BATCH = 8
SEQ = 8192
DEPTH = 1

GRID_W = 64
CTX_LEN = 256
S5_WIDTH = 512
S5_GROUP = 16
S5_GROUPS = S5_WIDTH // S5_GROUP
S5_STATE = 64
SGU_WIDTH = 512
SGU_GROUPS = 8
SGU_GROUP_DIM = SGU_WIDTH // SGU_GROUPS
CHUNK = 128
FFN_HIDDEN = 2816
CONV_K = 3
N_BRANCH = 2
IN_WIDTH = S5_WIDTH + 2 * SGU_WIDTH + N_BRANCH * D_MODEL
N_MOD = 6
EPS = 1e-6
DT_MIN = 1e-3
DT_MAX = 1e-1

kernel_name = 'hybrid_s5_sgu_convffn_block'


def rms_norm(x, g):
    x32 = x.astype(jnp.float32)
    y = x32 * lax.rsqrt(jnp.mean(x32 * x32, axis=-1, keepdims=True) + EPS)
    return y.astype(x.dtype) * g


def layer_norm(x, g, b):
    x32 = x.astype(jnp.float32)
    xc = x32 - jnp.mean(x32, axis=-1, keepdims=True)
    y = xc * lax.rsqrt(jnp.mean(xc * xc, axis=-1, keepdims=True) + EPS)
    return y.astype(x.dtype) * g + b


def modulate(x, g, shift, scale):
    return rms_norm(x, g) * (1 + scale) + shift


def s5_discretize(a_re, a_im, log_step, b_re, b_im):
    f32 = jnp.float32
    a_re = a_re.astype(f32)
    a_im = a_im.astype(f32)
    dt = jnp.exp(log_step.astype(f32))[:, None]
    mag = jnp.exp(a_re * dt)
    ab_re = mag * jnp.cos(a_im * dt)
    ab_im = mag * jnp.sin(a_im * dt)
    p = ab_re - 1.0
    q = ab_im
    den = a_re * a_re + a_im * a_im
    k_re = ((p * a_re + q * a_im) / den)[..., None]
    k_im = ((q * a_re - p * a_im) / den)[..., None]
    b_re = b_re.astype(f32)
    b_im = b_im.astype(f32)
    bb_re = k_re * b_re - k_im * b_im
    bb_im = k_re * b_im + k_im * b_re
    return ab_re, ab_im, bb_re, bb_im


def _linear_recurrence_combine(e1, e2):
    a1r, a1i, b1r, b1i = e1
    a2r, a2i, b2r, b2i = e2
    return (a2r * a1r - a2i * a1i,
            a2r * a1i + a2i * a1r,
            a2r * b1r - a2i * b1i + b2r,
            a2r * b1i + a2i * b1r + b2i)


def s5_scan(u, ab_re, ab_im, bb_re, bb_im, s0=None):
    bsz, length, _ = u.shape
    ug = u.astype(jnp.float32).reshape(bsz, length, S5_GROUPS, S5_GROUP)
    bu_re = jnp.einsum('blgh,gnh->blgn', ug, bb_re)
    bu_im = jnp.einsum('blgh,gnh->blgn', ug, bb_im)
    if s0 is not None:
        s0_re, s0_im = s0
        bu_re = bu_re.at[:, 0].add(ab_re * s0_re - ab_im * s0_im)
        bu_im = bu_im.at[:, 0].add(ab_re * s0_im + ab_im * s0_re)
    a_re = jnp.broadcast_to(ab_re, (1, length) + ab_re.shape)
    a_im = jnp.broadcast_to(ab_im, (1, length) + ab_im.shape)
    _, _, h_re, h_im = lax.associative_scan(
        _linear_recurrence_combine, (a_re, a_im, bu_re, bu_im), axis=1)
    return h_re, h_im


def s5_readout(h_re, h_im, c_re, c_im):
    y = (jnp.einsum('blgn,ghn->blgh', h_re, c_re.astype(jnp.float32))
         - jnp.einsum('blgn,ghn->blgh', h_im, c_im.astype(jnp.float32)))
    return y.reshape(y.shape[0], y.shape[1], S5_WIDTH)


def s5_glu(y, w, b):
    y = jax.nn.gelu(y)
    return y * jax.nn.sigmoid(y @ w + b)


def s5_mixer(u, u_ctx, a_re, a_im, log_step, b_re, b_im, c_re, c_im, d_skip, w_glu, b_glu, with_ctx_out):
    f32 = jnp.float32
    y = u.astype(f32) * d_skip.astype(f32)
    y_ctx = u_ctx.astype(f32) * d_skip.astype(f32) if with_ctx_out else None
    for direction in range(2):
        ab_re, ab_im, bb_re, bb_im = s5_discretize(
            a_re[direction], a_im[direction], log_step[direction], b_re[direction], b_im[direction])
        orient = (lambda t: jnp.flip(t, axis=1)) if direction == 1 else (lambda t: t)
        hc_re, hc_im = s5_scan(orient(u_ctx), ab_re, ab_im, bb_re, bb_im)
        h_re, h_im = s5_scan(orient(u), ab_re, ab_im, bb_re, bb_im,
                             s0=(hc_re[:, -1], hc_im[:, -1]))
        y = y + orient(s5_readout(h_re, h_im, c_re[direction], c_im[direction]))
        if with_ctx_out:
            y_ctx = y_ctx + orient(s5_readout(hc_re, hc_im, c_re[direction], c_im[direction]))
    y = s5_glu(y.astype(u.dtype), w_glu, b_glu)
    if with_ctx_out:
        y_ctx = s5_glu(y_ctx.astype(u.dtype), w_glu, b_glu)
    return y, y_ctx


def sgu_mixer(z, ln_g, ln_b, w_sp, b_sp):
    u, v = jnp.split(z, 2, axis=-1)
    v = layer_norm(v, ln_g, ln_b)
    bsz, length, _ = v.shape
    v = v.reshape(bsz, length // CHUNK, CHUNK, SGU_GROUPS, SGU_GROUP_DIM)
    s = jnp.einsum('bcpgd,gqp->bcqgd', v, w_sp) + jnp.transpose(b_sp)[None, None, :, :, None]
    return u * s.reshape(bsz, length, SGU_WIDTH)


def merge_branches(y_a, y_b, gate_logits, w_proj_a, w_proj_b, b_gate, w_out):
    g_a, g_b = jnp.split(jax.nn.sigmoid(gate_logits + b_gate), N_BRANCH, axis=-1)
    return (g_a * (y_a @ w_proj_a) + g_b * (y_b @ w_proj_b)) @ w_out


def depthwise_conv_grid(u, w, b, rows, cols):
    bsz, _, ch = u.shape
    y = lax.conv_general_dilated(
        u.reshape(bsz, rows, cols, ch), w[:, :, None, :], (1, 1), 'SAME',
        dimension_numbers=('NHWC', 'HWIO', 'NHWC'), feature_group_count=ch)
    return y.reshape(bsz, rows * cols, ch) + b


def conv_ffn(h, w_up, conv_w, conv_b, w_down, rows, cols):
    up = depthwise_conv_grid(h @ w_up, conv_w, conv_b, rows, cols)
    gate, val = jnp.split(up, 2, axis=-1)
    return (jax.nn.silu(gate) * val) @ w_down


def trunk_layer(x, xc, c_silu, cc_silu, rows, w_ada, b_ada, g_mix, w_in,
                s5_a_re, s5_a_im, s5_log_step, s5_b_re, s5_b_im, s5_c_re, s5_c_im,
                s5_d, s5_w_glu, s5_b_glu, sgu_ln_g, sgu_ln_b, sgu_w, sgu_b,
                w_proj_a, w_proj_b, b_gate, w_out, g_ffn, w_up, conv_w, conv_b, w_down,
                update_ctx):
    mod = c_silu @ w_ada + b_ada
    sh1, sc1, ga1, sh2, sc2, ga2 = jnp.split(mod[:, None, :], N_MOD, axis=-1)
    mod_c = cc_silu @ w_ada + b_ada
    sh1c, sc1c, ga1c, sh2c, sc2c, ga2c = jnp.split(mod_c, N_MOD, axis=-1)

    h = modulate(x, g_mix, sh1, sc1)
    hc = modulate(xc, g_mix, sh1c, sc1c)
    proj = h @ w_in
    u_a = proj[..., :S5_WIDTH]
    z_b = jax.nn.gelu(proj[..., S5_WIDTH:S5_WIDTH + 2 * SGU_WIDTH])
    gate_logits = proj[..., S5_WIDTH + 2 * SGU_WIDTH:]
    proj_c = hc @ (w_in if update_ctx else w_in[:, :S5_WIDTH])
    u_ac = proj_c[..., :S5_WIDTH]
    y_a, y_ac = s5_mixer(u_a, u_ac, s5_a_re, s5_a_im, s5_log_step, s5_b_re, s5_b_im,
                         s5_c_re, s5_c_im, s5_d, s5_w_glu, s5_b_glu, update_ctx)
    y_b = sgu_mixer(z_b, sgu_ln_g, sgu_ln_b, sgu_w, sgu_b)
    x = x + ga1 * merge_branches(y_a, y_b, gate_logits, w_proj_a, w_proj_b, b_gate, w_out)

    h2 = modulate(x, g_ffn, sh2, sc2)
    x = x + ga2 * conv_ffn(h2, w_up, conv_w, conv_b, w_down, rows, GRID_W)

    if update_ctx:
        z_bc = jax.nn.gelu(proj_c[..., S5_WIDTH:S5_WIDTH + 2 * SGU_WIDTH])
        gate_c = proj_c[..., S5_WIDTH + 2 * SGU_WIDTH:]
        y_bc = sgu_mixer(z_bc, sgu_ln_g, sgu_ln_b, sgu_w, sgu_b)
        xc = xc + ga1c * merge_branches(y_ac, y_bc, gate_c, w_proj_a, w_proj_b, b_gate, w_out)
        h2c = modulate(xc, g_ffn, sh2c, sc2c)
        mid = CONV_K // 2
        xc = xc + ga2c * conv_ffn(h2c, w_up, conv_w[mid:mid + 1], conv_b, w_down, 1, xc.shape[1])
    else:
        xc = None
    return x, xc


def _fwd_setup_inputs(seed: int = 0) -> dict:
    key = jax.random.key(seed)
    ks = jax.random.split(key, 40)
    f32 = jnp.float32
    D, L = D_MODEL, DEPTH
    G, N, H = S5_GROUPS, S5_STATE, S5_GROUP
    F2 = 2 * FFN_HIDDEN

    def nrm(k, shape, scale):
        return jax.random.normal(k, shape, f32) * scale

    n_idx = jnp.arange(N, dtype=f32)
    return {
        'x': nrm(ks[0], (BATCH, SEQ, D), 1.0),
        'c': nrm(ks[1], (BATCH, D), 1.0),
        'ctx': nrm(ks[2], (BATCH, CTX_LEN, D), 1.0),
        'c_ctx': nrm(ks[3], (D,), 1.0),
        'w_ada': nrm(ks[4], (L, D, N_MOD * D), 0.5 * D ** -0.5),
        'b_ada': nrm(ks[5], (L, N_MOD * D), 0.02),
        'g_mix': 1.0 + nrm(ks[6], (L, D), 0.02),
        'w_in': nrm(ks[7], (L, D, IN_WIDTH), D ** -0.5),
        's5_a_re': -0.5 + nrm(ks[8], (L, 2, G, N), 0.01),
        's5_a_im': math.pi * n_idx + nrm(ks[9], (L, 2, G, N), 0.01),
        's5_log_step': jax.random.uniform(ks[10], (L, 2, G), f32, math.log(DT_MIN), math.log(DT_MAX)),
        's5_b_re': nrm(ks[11], (L, 2, G, N, H), (2 * H) ** -0.5),
        's5_b_im': nrm(ks[12], (L, 2, G, N, H), (2 * H) ** -0.5),
        's5_c_re': nrm(ks[13], (L, 2, G, H, N), N ** -0.5),
        's5_c_im': nrm(ks[14], (L, 2, G, H, N), N ** -0.5),
        's5_d': nrm(ks[15], (L, S5_WIDTH), 1.0),
        's5_w_glu': nrm(ks[16], (L, S5_WIDTH, S5_WIDTH), S5_WIDTH ** -0.5),
        's5_b_glu': nrm(ks[17], (L, S5_WIDTH), 0.02),
        'sgu_ln_g': 1.0 + nrm(ks[18], (L, SGU_WIDTH), 0.02),
        'sgu_ln_b': nrm(ks[19], (L, SGU_WIDTH), 0.02),
        'sgu_w': nrm(ks[20], (L, SGU_GROUPS, CHUNK, CHUNK), CHUNK ** -0.5),
        'sgu_b': 1.0 + nrm(ks[21], (L, SGU_GROUPS, CHUNK), 0.1),
        'w_proj_a': nrm(ks[22], (L, S5_WIDTH, D), S5_WIDTH ** -0.5),
        'w_proj_b': nrm(ks[23], (L, SGU_WIDTH, D), SGU_WIDTH ** -0.5),
        'b_gate': nrm(ks[24], (L, N_BRANCH * D), 0.02),
        'w_out': nrm(ks[25], (L, D, D), D ** -0.5),
        'g_ffn': 1.0 + nrm(ks[26], (L, D), 0.02),
        'w_up': nrm(ks[27], (L, D, F2), D ** -0.5),
        'conv_w': nrm(ks[28], (L, CONV_K, CONV_K, F2), 1.0 / CONV_K),
        'conv_b': nrm(ks[29], (L, F2), 0.02),
        'w_down': nrm(ks[30], (L, FFN_HIDDEN, D), FFN_HIDDEN ** -0.5),
        'g_final': 1.0 + nrm(ks[31], (D,), 0.02),
    }


def _fwd_reference(x, c, ctx, c_ctx, w_ada, b_ada, g_mix, w_in, s5_a_re, s5_a_im, s5_log_step,
              s5_b_re, s5_b_im, s5_c_re, s5_c_im, s5_d, s5_w_glu, s5_b_glu,
              sgu_ln_g, sgu_ln_b, sgu_w, sgu_b, w_proj_a, w_proj_b, b_gate, w_out,
              g_ffn, w_up, conv_w, conv_b, w_down, g_final):
    rows = x.shape[1] // GRID_W
    c_silu = jax.nn.silu(c)
    cc_silu = jax.nn.silu(c_ctx)
    xc = ctx
    for i in range(DEPTH):
        x, xc = trunk_layer(
            x, xc, c_silu, cc_silu, rows, w_ada[i], b_ada[i], g_mix[i], w_in[i],
            s5_a_re[i], s5_a_im[i], s5_log_step[i], s5_b_re[i], s5_b_im[i], s5_c_re[i], s5_c_im[i],
            s5_d[i], s5_w_glu[i], s5_b_glu[i], sgu_ln_g[i], sgu_ln_b[i], sgu_w[i], sgu_b[i],
            w_proj_a[i], w_proj_b[i], b_gate[i], w_out[i], g_ffn[i], w_up[i], conv_w[i], conv_b[i],
            w_down[i], i + 1 < DEPTH)
    return rms_norm(x, g_final)


import jax as _jax
import jax.numpy as _jnp

TWIN_FORMAT = 'train_step'
FWD_PARAMS = ['x', 'c', 'ctx', 'c_ctx', 'w_ada', 'b_ada', 'g_mix', 'w_in', 's5_a_re', 's5_a_im', 's5_log_step', 's5_b_re', 's5_b_im', 's5_c_re', 's5_c_im', 's5_d', 's5_w_glu', 's5_b_glu', 'sgu_ln_g', 'sgu_ln_b', 'sgu_w', 'sgu_b', 'w_proj_a', 'w_proj_b', 'b_gate', 'w_out', 'g_ffn', 'w_up', 'conv_w', 'conv_b', 'w_down', 'g_final']
TWIN_WEIGHTS = ['c_ctx', 'w_ada', 'b_ada', 'g_mix', 'w_in', 's5_a_re', 's5_a_im', 's5_log_step', 's5_b_re', 's5_b_im', 's5_c_re', 's5_c_im', 's5_d', 's5_w_glu', 's5_b_glu', 'sgu_ln_g', 'sgu_ln_b', 'sgu_w', 'sgu_b', 'w_proj_a', 'w_proj_b', 'b_gate', 'w_out', 'g_ffn', 'w_up', 'conv_w', 'conv_b', 'w_down', 'g_final']
TWIN_DIFF_INPUT = 'x'
TWIN_INPUTS = ['x', 'c', 'ctx', 'c_ctx', 'w_ada', 'b_ada', 'g_mix', 'w_in', 's5_a_re', 's5_a_im', 's5_log_step', 's5_b_re', 's5_b_im', 's5_c_re', 's5_c_im', 's5_d', 's5_w_glu', 's5_b_glu', 'sgu_ln_g', 'sgu_ln_b', 'sgu_w', 'sgu_b', 'w_proj_a', 'w_proj_b', 'b_gate', 'w_out', 'g_ffn', 'w_up', 'conv_w', 'conv_b', 'w_down', 'g_final', 'loss_target', 'm_c_ctx', 'm_w_ada', 'm_b_ada', 'm_g_mix', 'm_w_in', 'm_s5_a_re', 'm_s5_a_im', 'm_s5_log_step', 'm_s5_b_re', 'm_s5_b_im', 'm_s5_c_re', 'm_s5_c_im', 'm_s5_d', 'm_s5_w_glu', 'm_s5_b_glu', 'm_sgu_ln_g', 'm_sgu_ln_b', 'm_sgu_w', 'm_sgu_b', 'm_w_proj_a', 'm_w_proj_b', 'm_b_gate', 'm_w_out', 'm_g_ffn', 'm_w_up', 'm_conv_w', 'm_conv_b', 'm_w_down', 'm_g_final', 'v_c_ctx', 'v_w_ada', 'v_b_ada', 'v_g_mix', 'v_w_in', 'v_s5_a_re', 'v_s5_a_im', 'v_s5_log_step', 'v_s5_b_re', 'v_s5_b_im', 'v_s5_c_re', 'v_s5_c_im', 'v_s5_d', 'v_s5_w_glu', 'v_s5_b_glu', 'v_sgu_ln_g', 'v_sgu_ln_b', 'v_sgu_w', 'v_sgu_b', 'v_w_proj_a', 'v_w_proj_b', 'v_b_gate', 'v_w_out', 'v_g_ffn', 'v_w_up', 'v_conv_w', 'v_conv_b', 'v_w_down', 'v_g_final']
TWIN_OUTPUTS = ['loss', 'grad_x', 'grad_c_ctx', 'grad_w_ada', 'grad_b_ada', 'grad_g_mix', 'grad_w_in', 'grad_s5_a_re', 'grad_s5_a_im', 'grad_s5_log_step', 'grad_s5_b_re', 'grad_s5_b_im', 'grad_s5_c_re', 'grad_s5_c_im', 'grad_s5_d', 'grad_s5_w_glu', 'grad_s5_b_glu', 'grad_sgu_ln_g', 'grad_sgu_ln_b', 'grad_sgu_w', 'grad_sgu_b', 'grad_w_proj_a', 'grad_w_proj_b', 'grad_b_gate', 'grad_w_out', 'grad_g_ffn', 'grad_w_up', 'grad_conv_w', 'grad_conv_b', 'grad_w_down', 'grad_g_final', 'delta_c_ctx', 'delta_w_ada', 'delta_b_ada', 'delta_g_mix', 'delta_w_in', 'delta_s5_a_re', 'delta_s5_a_im', 'delta_s5_log_step', 'delta_s5_b_re', 'delta_s5_b_im', 'delta_s5_c_re', 'delta_s5_c_im', 'delta_s5_d', 'delta_s5_w_glu', 'delta_s5_b_glu', 'delta_sgu_ln_g', 'delta_sgu_ln_b', 'delta_sgu_w', 'delta_sgu_b', 'delta_w_proj_a', 'delta_w_proj_b', 'delta_b_gate', 'delta_w_out', 'delta_g_ffn', 'delta_w_up', 'delta_conv_w', 'delta_conv_b', 'delta_w_down', 'delta_g_final', 'new_m_c_ctx', 'new_m_w_ada', 'new_m_b_ada', 'new_m_g_mix', 'new_m_w_in', 'new_m_s5_a_re', 'new_m_s5_a_im', 'new_m_s5_log_step', 'new_m_s5_b_re', 'new_m_s5_b_im', 'new_m_s5_c_re', 'new_m_s5_c_im', 'new_m_s5_d', 'new_m_s5_w_glu', 'new_m_s5_b_glu', 'new_m_sgu_ln_g', 'new_m_sgu_ln_b', 'new_m_sgu_w', 'new_m_sgu_b', 'new_m_w_proj_a', 'new_m_w_proj_b', 'new_m_b_gate', 'new_m_w_out', 'new_m_g_ffn', 'new_m_w_up', 'new_m_conv_w', 'new_m_conv_b', 'new_m_w_down', 'new_m_g_final', 'new_v_c_ctx', 'new_v_w_ada', 'new_v_b_ada', 'new_v_g_mix', 'new_v_w_in', 'new_v_s5_a_re', 'new_v_s5_a_im', 'new_v_s5_log_step', 'new_v_s5_b_re', 'new_v_s5_b_im', 'new_v_s5_c_re', 'new_v_s5_c_im', 'new_v_s5_d', 'new_v_s5_w_glu', 'new_v_s5_b_glu', 'new_v_sgu_ln_g', 'new_v_sgu_ln_b', 'new_v_sgu_w', 'new_v_sgu_b', 'new_v_w_proj_a', 'new_v_w_proj_b', 'new_v_b_gate', 'new_v_w_out', 'new_v_g_ffn', 'new_v_w_up', 'new_v_conv_w', 'new_v_conv_b', 'new_v_w_down', 'new_v_g_final']
TWIN_LEAF_KINDS = {'loss': 'loss', 'grad_x': 'grad_x', 'grad_c_ctx': 'grad_w', 'grad_w_ada': 'grad_w', 'grad_b_ada': 'grad_w', 'grad_g_mix': 'grad_w', 'grad_w_in': 'grad_w', 'grad_s5_a_re': 'grad_w', 'grad_s5_a_im': 'grad_w', 'grad_s5_log_step': 'grad_w', 'grad_s5_b_re': 'grad_w', 'grad_s5_b_im': 'grad_w', 'grad_s5_c_re': 'grad_w', 'grad_s5_c_im': 'grad_w', 'grad_s5_d': 'grad_w', 'grad_s5_w_glu': 'grad_w', 'grad_s5_b_glu': 'grad_w', 'grad_sgu_ln_g': 'grad_w', 'grad_sgu_ln_b': 'grad_w', 'grad_sgu_w': 'grad_w', 'grad_sgu_b': 'grad_w', 'grad_w_proj_a': 'grad_w', 'grad_w_proj_b': 'grad_w', 'grad_b_gate': 'grad_w', 'grad_w_out': 'grad_w', 'grad_g_ffn': 'grad_w', 'grad_w_up': 'grad_w', 'grad_conv_w': 'grad_w', 'grad_conv_b': 'grad_w', 'grad_w_down': 'grad_w', 'grad_g_final': 'grad_w', 'delta_c_ctx': 'delta_w', 'delta_w_ada': 'delta_w', 'delta_b_ada': 'delta_w', 'delta_g_mix': 'delta_w', 'delta_w_in': 'delta_w', 'delta_s5_a_re': 'delta_w', 'delta_s5_a_im': 'delta_w', 'delta_s5_log_step': 'delta_w', 'delta_s5_b_re': 'delta_w', 'delta_s5_b_im': 'delta_w', 'delta_s5_c_re': 'delta_w', 'delta_s5_c_im': 'delta_w', 'delta_s5_d': 'delta_w', 'delta_s5_w_glu': 'delta_w', 'delta_s5_b_glu': 'delta_w', 'delta_sgu_ln_g': 'delta_w', 'delta_sgu_ln_b': 'delta_w', 'delta_sgu_w': 'delta_w', 'delta_sgu_b': 'delta_w', 'delta_w_proj_a': 'delta_w', 'delta_w_proj_b': 'delta_w', 'delta_b_gate': 'delta_w', 'delta_w_out': 'delta_w', 'delta_g_ffn': 'delta_w', 'delta_w_up': 'delta_w', 'delta_conv_w': 'delta_w', 'delta_conv_b': 'delta_w', 'delta_w_down': 'delta_w', 'delta_g_final': 'delta_w', 'new_m_c_ctx': 'new_m', 'new_m_w_ada': 'new_m', 'new_m_b_ada': 'new_m', 'new_m_g_mix': 'new_m', 'new_m_w_in': 'new_m', 'new_m_s5_a_re': 'new_m', 'new_m_s5_a_im': 'new_m', 'new_m_s5_log_step': 'new_m', 'new_m_s5_b_re': 'new_m', 'new_m_s5_b_im': 'new_m', 'new_m_s5_c_re': 'new_m', 'new_m_s5_c_im': 'new_m', 'new_m_s5_d': 'new_m', 'new_m_s5_w_glu': 'new_m', 'new_m_s5_b_glu': 'new_m', 'new_m_sgu_ln_g': 'new_m', 'new_m_sgu_ln_b': 'new_m', 'new_m_sgu_w': 'new_m', 'new_m_sgu_b': 'new_m', 'new_m_w_proj_a': 'new_m', 'new_m_w_proj_b': 'new_m', 'new_m_b_gate': 'new_m', 'new_m_w_out': 'new_m', 'new_m_g_ffn': 'new_m', 'new_m_w_up': 'new_m', 'new_m_conv_w': 'new_m', 'new_m_conv_b': 'new_m', 'new_m_w_down': 'new_m', 'new_m_g_final': 'new_m', 'new_v_c_ctx': 'new_v', 'new_v_w_ada': 'new_v', 'new_v_b_ada': 'new_v', 'new_v_g_mix': 'new_v', 'new_v_w_in': 'new_v', 'new_v_s5_a_re': 'new_v', 'new_v_s5_a_im': 'new_v', 'new_v_s5_log_step': 'new_v', 'new_v_s5_b_re': 'new_v', 'new_v_s5_b_im': 'new_v', 'new_v_s5_c_re': 'new_v', 'new_v_s5_c_im': 'new_v', 'new_v_s5_d': 'new_v', 'new_v_s5_w_glu': 'new_v', 'new_v_s5_b_glu': 'new_v', 'new_v_sgu_ln_g': 'new_v', 'new_v_sgu_ln_b': 'new_v', 'new_v_sgu_w': 'new_v', 'new_v_sgu_b': 'new_v', 'new_v_w_proj_a': 'new_v', 'new_v_w_proj_b': 'new_v', 'new_v_b_gate': 'new_v', 'new_v_w_out': 'new_v', 'new_v_g_ffn': 'new_v', 'new_v_w_up': 'new_v', 'new_v_conv_w': 'new_v', 'new_v_conv_b': 'new_v', 'new_v_w_down': 'new_v', 'new_v_g_final': 'new_v'}


def _forward(args):
    return _fwd_reference(*[args[k] for k in FWD_PARAMS])


def _output_shape():
    def fwd():
        inp = _fwd_setup_inputs(0)
        return _fwd_reference(*[inp[k] for k in FWD_PARAMS])
    out = _jax.eval_shape(fwd)
    return out.shape, out.dtype

N_MICROBATCH = 1
ADAM_LR = 0.001
ADAM_B1 = 0.9
ADAM_B2 = 0.999
ADAM_EPS = 1e-08
ADAM_WD = 0.01
ADAM_STEP = 10
PER_EXAMPLE_BATCH_AXIS = {'x': 0, 'c': 0, 'ctx': 0, 'loss_target': 0}
SHARED_INPUTS = []
_WEIGHT_DTYPES = {'c_ctx': _jnp.float32, 'w_ada': _jnp.float32, 'b_ada': _jnp.float32, 'g_mix': _jnp.float32, 'w_in': _jnp.float32, 's5_a_re': _jnp.float32, 's5_a_im': _jnp.float32, 's5_log_step': _jnp.float32, 's5_b_re': _jnp.float32, 's5_b_im': _jnp.float32, 's5_c_re': _jnp.float32, 's5_c_im': _jnp.float32, 's5_d': _jnp.float32, 's5_w_glu': _jnp.float32, 's5_b_glu': _jnp.float32, 'sgu_ln_g': _jnp.float32, 'sgu_ln_b': _jnp.float32, 'sgu_w': _jnp.float32, 'sgu_b': _jnp.float32, 'w_proj_a': _jnp.float32, 'w_proj_b': _jnp.float32, 'b_gate': _jnp.float32, 'w_out': _jnp.float32, 'g_ffn': _jnp.float32, 'w_up': _jnp.float32, 'conv_w': _jnp.float32, 'conv_b': _jnp.float32, 'w_down': _jnp.float32, 'g_final': _jnp.float32}
MOMENT_SCALE = {'c_ctx': 5.674621e-04, 'w_ada': 7.205833e-02, 'b_ada': 1.196007e-01, 'g_mix': 6.261159e-02, 'w_in': 3.295438e-02, 's5_a_re': 1.743481e-03, 's5_a_im': 1.734301e-03, 's5_log_step': 7.741217e-01, 's5_b_re': 1.262244e-03, 's5_b_im': 1.314052e-03, 's5_c_re': 1.867717e-03, 's5_c_im': 1.792216e-03, 's5_d': 2.626590e-02, 's5_w_glu': 7.921489e-03, 's5_b_glu': 1.032028e-02, 'sgu_ln_g': 4.676591e-02, 'sgu_ln_b': 4.289308e-02, 'sgu_w': 2.996453e-02, 'sgu_b': 3.001223e-02, 'w_proj_a': 1.704095e-02, 'w_proj_b': 4.275861e-02, 'b_gate': 1.231840e-02, 'w_out': 4.639450e-02, 'g_ffn': 8.637214e-02, 'w_up': 3.702698e-02, 'conv_w': 3.669035e-02, 'conv_b': 2.959801e-02, 'w_down': 6.071642e-02, 'g_final': 6.401183e+01}


def _to_microbatches(a, axis):
    t = _jnp.moveaxis(a, axis, 0)
    t = t.reshape((N_MICROBATCH, t.shape[0] // N_MICROBATCH) + t.shape[1:])
    return _jnp.moveaxis(t, 1, axis + 1)


def setup_inputs(seed: int = 0) -> dict:
    inp = _fwd_setup_inputs(seed)
    key = _jax.random.fold_in(_jax.random.key(seed), 7919)
    shape, _ = _output_shape()
    out = dict(inp)
    out["loss_target"] = _jax.random.normal(_jax.random.fold_in(key, 0), shape, _jnp.float32)
    for i, name in enumerate(TWIN_WEIGHTS):
        w = inp[name].astype(_jnp.float32)
        if MOMENT_SCALE is None:
            s = _jnp.sqrt(_jnp.mean(_jnp.square(w)) + 1e-30)
        else:
            s = MOMENT_SCALE[name]
        km, kv = _jax.random.split(_jax.random.fold_in(key, i + 1))
        out[name] = w
        out["m_" + name] = s * _jax.random.normal(km, w.shape, _jnp.float32)
        out["v_" + name] = (s * s) * _jax.random.uniform(kv, w.shape, _jnp.float32, 0.5, 1.5)
    if N_MICROBATCH > 1:
        for name, axis in PER_EXAMPLE_BATCH_AXIS.items():
            out[name] = _to_microbatches(out[name], axis)
    return {'x': out['x'], 'c': out['c'], 'ctx': out['ctx'], 'c_ctx': out['c_ctx'], 'w_ada': out['w_ada'], 'b_ada': out['b_ada'], 'g_mix': out['g_mix'], 'w_in': out['w_in'], 's5_a_re': out['s5_a_re'], 's5_a_im': out['s5_a_im'], 's5_log_step': out['s5_log_step'], 's5_b_re': out['s5_b_re'], 's5_b_im': out['s5_b_im'], 's5_c_re': out['s5_c_re'], 's5_c_im': out['s5_c_im'], 's5_d': out['s5_d'], 's5_w_glu': out['s5_w_glu'], 's5_b_glu': out['s5_b_glu'], 'sgu_ln_g': out['sgu_ln_g'], 'sgu_ln_b': out['sgu_ln_b'], 'sgu_w': out['sgu_w'], 'sgu_b': out['sgu_b'], 'w_proj_a': out['w_proj_a'], 'w_proj_b': out['w_proj_b'], 'b_gate': out['b_gate'], 'w_out': out['w_out'], 'g_ffn': out['g_ffn'], 'w_up': out['w_up'], 'conv_w': out['conv_w'], 'conv_b': out['conv_b'], 'w_down': out['w_down'], 'g_final': out['g_final'], 'loss_target': out['loss_target'], 'm_c_ctx': out['m_c_ctx'], 'm_w_ada': out['m_w_ada'], 'm_b_ada': out['m_b_ada'], 'm_g_mix': out['m_g_mix'], 'm_w_in': out['m_w_in'], 'm_s5_a_re': out['m_s5_a_re'], 'm_s5_a_im': out['m_s5_a_im'], 'm_s5_log_step': out['m_s5_log_step'], 'm_s5_b_re': out['m_s5_b_re'], 'm_s5_b_im': out['m_s5_b_im'], 'm_s5_c_re': out['m_s5_c_re'], 'm_s5_c_im': out['m_s5_c_im'], 'm_s5_d': out['m_s5_d'], 'm_s5_w_glu': out['m_s5_w_glu'], 'm_s5_b_glu': out['m_s5_b_glu'], 'm_sgu_ln_g': out['m_sgu_ln_g'], 'm_sgu_ln_b': out['m_sgu_ln_b'], 'm_sgu_w': out['m_sgu_w'], 'm_sgu_b': out['m_sgu_b'], 'm_w_proj_a': out['m_w_proj_a'], 'm_w_proj_b': out['m_w_proj_b'], 'm_b_gate': out['m_b_gate'], 'm_w_out': out['m_w_out'], 'm_g_ffn': out['m_g_ffn'], 'm_w_up': out['m_w_up'], 'm_conv_w': out['m_conv_w'], 'm_conv_b': out['m_conv_b'], 'm_w_down': out['m_w_down'], 'm_g_final': out['m_g_final'], 'v_c_ctx': out['v_c_ctx'], 'v_w_ada': out['v_w_ada'], 'v_b_ada': out['v_b_ada'], 'v_g_mix': out['v_g_mix'], 'v_w_in': out['v_w_in'], 'v_s5_a_re': out['v_s5_a_re'], 'v_s5_a_im': out['v_s5_a_im'], 'v_s5_log_step': out['v_s5_log_step'], 'v_s5_b_re': out['v_s5_b_re'], 'v_s5_b_im': out['v_s5_b_im'], 'v_s5_c_re': out['v_s5_c_re'], 'v_s5_c_im': out['v_s5_c_im'], 'v_s5_d': out['v_s5_d'], 'v_s5_w_glu': out['v_s5_w_glu'], 'v_s5_b_glu': out['v_s5_b_glu'], 'v_sgu_ln_g': out['v_sgu_ln_g'], 'v_sgu_ln_b': out['v_sgu_ln_b'], 'v_sgu_w': out['v_sgu_w'], 'v_sgu_b': out['v_sgu_b'], 'v_w_proj_a': out['v_w_proj_a'], 'v_w_proj_b': out['v_w_proj_b'], 'v_b_gate': out['v_b_gate'], 'v_w_out': out['v_w_out'], 'v_g_ffn': out['v_g_ffn'], 'v_w_up': out['v_w_up'], 'v_conv_w': out['v_conv_w'], 'v_conv_b': out['v_conv_b'], 'v_w_down': out['v_w_down'], 'v_g_final': out['v_g_final']}


def _loss(weights, diff, rest, loss_target):
    with _jax.named_scope("forward"):
        args = {**rest, TWIN_DIFF_INPUT: diff, **{k: w.astype(_WEIGHT_DTYPES[k]) for k, w in weights.items()}}
        y = _forward(args)
    with _jax.named_scope("loss_head"):
        err = _jnp.square(y.astype(_jnp.float32) - loss_target)
        return 0.5 * _jnp.sum(_jnp.mean(err, axis=-1)) if err.ndim else 0.5 * err


def _adamw(w, g, m, v):
    m = ADAM_B1 * m + (1.0 - ADAM_B1) * g
    v = ADAM_B2 * v + (1.0 - ADAM_B2) * _jnp.square(g)
    m_hat = m / (1.0 - ADAM_B1 ** ADAM_STEP)
    v_hat = v / (1.0 - ADAM_B2 ** ADAM_STEP)
    delta = -ADAM_LR * (m_hat / (_jnp.sqrt(v_hat) + ADAM_EPS) + ADAM_WD * w)
    return delta, m, v


def reference(x, c, ctx, c_ctx, w_ada, b_ada, g_mix, w_in, s5_a_re, s5_a_im, s5_log_step, s5_b_re, s5_b_im, s5_c_re, s5_c_im, s5_d, s5_w_glu, s5_b_glu, sgu_ln_g, sgu_ln_b, sgu_w, sgu_b, w_proj_a, w_proj_b, b_gate, w_out, g_ffn, w_up, conv_w, conv_b, w_down, g_final, loss_target, m_c_ctx, m_w_ada, m_b_ada, m_g_mix, m_w_in, m_s5_a_re, m_s5_a_im, m_s5_log_step, m_s5_b_re, m_s5_b_im, m_s5_c_re, m_s5_c_im, m_s5_d, m_s5_w_glu, m_s5_b_glu, m_sgu_ln_g, m_sgu_ln_b, m_sgu_w, m_sgu_b, m_w_proj_a, m_w_proj_b, m_b_gate, m_w_out, m_g_ffn, m_w_up, m_conv_w, m_conv_b, m_w_down, m_g_final, v_c_ctx, v_w_ada, v_b_ada, v_g_mix, v_w_in, v_s5_a_re, v_s5_a_im, v_s5_log_step, v_s5_b_re, v_s5_b_im, v_s5_c_re, v_s5_c_im, v_s5_d, v_s5_w_glu, v_s5_b_glu, v_sgu_ln_g, v_sgu_ln_b, v_sgu_w, v_sgu_b, v_w_proj_a, v_w_proj_b, v_b_gate, v_w_out, v_g_ffn, v_w_up, v_conv_w, v_conv_b, v_w_down, v_g_final):
    given = dict(x=x, c=c, ctx=ctx, c_ctx=c_ctx, w_ada=w_ada, b_ada=b_ada, g_mix=g_mix, w_in=w_in, s5_a_re=s5_a_re, s5_a_im=s5_a_im, s5_log_step=s5_log_step, s5_b_re=s5_b_re, s5_b_im=s5_b_im, s5_c_re=s5_c_re, s5_c_im=s5_c_im, s5_d=s5_d, s5_w_glu=s5_w_glu, s5_b_glu=s5_b_glu, sgu_ln_g=sgu_ln_g, sgu_ln_b=sgu_ln_b, sgu_w=sgu_w, sgu_b=sgu_b, w_proj_a=w_proj_a, w_proj_b=w_proj_b, b_gate=b_gate, w_out=w_out, g_ffn=g_ffn, w_up=w_up, conv_w=conv_w, conv_b=conv_b, w_down=w_down, g_final=g_final, loss_target=loss_target, m_c_ctx=m_c_ctx, m_w_ada=m_w_ada, m_b_ada=m_b_ada, m_g_mix=m_g_mix, m_w_in=m_w_in, m_s5_a_re=m_s5_a_re, m_s5_a_im=m_s5_a_im, m_s5_log_step=m_s5_log_step, m_s5_b_re=m_s5_b_re, m_s5_b_im=m_s5_b_im, m_s5_c_re=m_s5_c_re, m_s5_c_im=m_s5_c_im, m_s5_d=m_s5_d, m_s5_w_glu=m_s5_w_glu, m_s5_b_glu=m_s5_b_glu, m_sgu_ln_g=m_sgu_ln_g, m_sgu_ln_b=m_sgu_ln_b, m_sgu_w=m_sgu_w, m_sgu_b=m_sgu_b, m_w_proj_a=m_w_proj_a, m_w_proj_b=m_w_proj_b, m_b_gate=m_b_gate, m_w_out=m_w_out, m_g_ffn=m_g_ffn, m_w_up=m_w_up, m_conv_w=m_conv_w, m_conv_b=m_conv_b, m_w_down=m_w_down, m_g_final=m_g_final, v_c_ctx=v_c_ctx, v_w_ada=v_w_ada, v_b_ada=v_b_ada, v_g_mix=v_g_mix, v_w_in=v_w_in, v_s5_a_re=v_s5_a_re, v_s5_a_im=v_s5_a_im, v_s5_log_step=v_s5_log_step, v_s5_b_re=v_s5_b_re, v_s5_b_im=v_s5_b_im, v_s5_c_re=v_s5_c_re, v_s5_c_im=v_s5_c_im, v_s5_d=v_s5_d, v_s5_w_glu=v_s5_w_glu, v_s5_b_glu=v_s5_b_glu, v_sgu_ln_g=v_sgu_ln_g, v_sgu_ln_b=v_sgu_ln_b, v_sgu_w=v_sgu_w, v_sgu_b=v_sgu_b, v_w_proj_a=v_w_proj_a, v_w_proj_b=v_w_proj_b, v_b_gate=v_b_gate, v_w_out=v_w_out, v_g_ffn=v_g_ffn, v_w_up=v_w_up, v_conv_w=v_conv_w, v_conv_b=v_conv_b, v_w_down=v_w_down, v_g_final=v_g_final)
    weights = {n: given[n] for n in TWIN_WEIGHTS}
    shared = {n: given[n] for n in SHARED_INPUTS}
    per_example = {n: given[n] for n in ['x', 'c', 'ctx']}
    grad_fn = _jax.value_and_grad(_loss, argnums=(0, 1))

    def one_microbatch(ex, loss_target):
        ex = dict(ex)
        diff = ex.pop(TWIN_DIFF_INPUT)
        return grad_fn(weights, diff, {**shared, **ex}, loss_target)

    if N_MICROBATCH == 1:
        loss, (grad_w, grad_x) = one_microbatch(per_example, given["loss_target"])
    else:
        def body(carry, xs):
            loss_sum, grad_sum = carry
            l_k, (gw_k, gx_k) = one_microbatch(xs[0], xs[1])
            with _jax.named_scope("update"):
                return (loss_sum + l_k, _jax.tree.map(_jnp.add, grad_sum, gw_k)), gx_k

        init = (_jnp.zeros((), _jnp.float32), _jax.tree.map(_jnp.zeros_like, weights))
        (loss, grad_w), grad_x = _jax.lax.scan(body, init, (per_example, given["loss_target"]))
    with _jax.named_scope("update"):
        delta_w, new_m, new_v = {}, {}, {}
        for n in TWIN_WEIGHTS:
            delta_w[n], new_m[n], new_v[n] = _adamw(weights[n], grad_w[n], given["m_" + n], given["v_" + n])
    return (loss, grad_x, *[grad_w[n] for n in TWIN_WEIGHTS], *[delta_w[n] for n in TWIN_WEIGHTS],
            *[new_m[n] for n in TWIN_WEIGHTS], *[new_v[n] for n in TWIN_WEIGHTS])
```

```python
import functools

import jax
import jax.numpy as jnp
from jax import lax
from jax.experimental import pallas as pl
from jax.experimental.pallas import tpu as pltpu

F32, BF16 = jnp.float32, jnp.bfloat16
MESH = pl.DeviceIdType.MESH

D_MODEL = 1024
S5_WIDTH = 512
S5_GROUP = 16
S5_GROUPS = 32
S5_STATE = 64
SGU_WIDTH = 512
SGU_GROUPS = 8
CHUNK = 128
FFN_HIDDEN = 2816
GRID_W = 64
N_MOD = 6
EPS = 1e-6
N_STATE = S5_GROUPS * S5_STATE
OCTETS = 4
SCAN_T = 128
N_CHIPS = 4
N_DEV = 8
LANES = 128
VMEM_LIMIT_BYTES = 56 * 1024 * 1024
CONV_PAD = 72
CONV_ROWS = 256

ADAM_LR, ADAM_B1, ADAM_B2, ADAM_EPS, ADAM_WD, ADAM_STEP = 0.001, 0.9, 0.999, 1e-08, 0.01, 10


def _call(body, name, out_shape, grid=None, in_specs=None, out_specs=None, scratch=(), sem=None, **kw):
    params = pltpu.CompilerParams(dimension_semantics=sem, vmem_limit_bytes=VMEM_LIMIT_BYTES)
    extra = {} if grid is None else {"grid": grid}
    return pl.pallas_call(body, name=name, out_shape=out_shape, in_specs=in_specs, out_specs=out_specs,
                          scratch_shapes=list(scratch), compiler_params=params, **extra, **kw)


def _tile(n, target, mult=LANES):
    best = None
    t = mult
    while t <= min(n, target):
        if n % t == 0:
            best = t
        t += mult
    return best or n


@jax.custom_vjp
def mmul(a, b):
    return jnp.dot(a.astype(BF16), b.astype(BF16), preferred_element_type=F32)


def _mmul_fwd(a, b):
    return mmul(a, b), (a, b)


def _mmul_bwd(res, ct):
    a, b = res
    ctb = ct.astype(BF16)
    da = lax.dot_general(ctb, b.astype(BF16), (((1,), (1,)), ((), ())), preferred_element_type=F32)
    db = lax.dot_general(a.astype(BF16), ctb, (((0,), (0,)), ((), ())), preferred_element_type=F32)
    return da.astype(a.dtype), db.astype(b.dtype)


mmul.defvjp(_mmul_fwd, _mmul_bwd)

_DOT_DIMS = {"nn": ((1,), (0,)), "nt": ((1,), (1,)), "tn": ((0,), (0,))}


def matmul(a, b, mode, out_dtype, name, init=None, tm=512, tn=512, tk=512):
    if mode == "nn":
        (M, K), (_, N) = a.shape, b.shape
    elif mode == "nt":
        (M, K), (N, _) = a.shape, b.shape
    else:
        (K, M), (_, N) = a.shape, b.shape
    tm, tn, tk = _tile(M, tm, 8 if M < LANES else LANES), _tile(N, tn), _tile(K, tk)
    nk = K // tk
    dims = (_DOT_DIMS[mode], ((), ()))
    has_init = init is not None

    def body(*refs):
        if has_init:
            a_ref, b_ref, i_ref, o_ref, acc = refs
        else:
            a_ref, b_ref, o_ref, acc = refs
        k = pl.program_id(2)

        @pl.when(k == 0)
        def _():
            acc[...] = i_ref[...].astype(F32) if has_init else jnp.zeros_like(acc)

        acc[...] += lax.dot_general(a_ref[...].astype(BF16), b_ref[...].astype(BF16), dims,
                                    preferred_element_type=F32)

        @pl.when(k == nk - 1)
        def _():
            o_ref[...] = acc[...].astype(o_ref.dtype)

    if mode == "tn":
        a_spec = pl.BlockSpec((tk, tm), lambda i, j, k: (k, i))
    else:
        a_spec = pl.BlockSpec((tm, tk), lambda i, j, k: (i, k))
    if mode == "nt":
        b_spec = pl.BlockSpec((tn, tk), lambda i, j, k: (j, k))
    else:
        b_spec = pl.BlockSpec((tk, tn), lambda i, j, k: (k, j))
    o_spec = pl.BlockSpec((tm, tn), lambda i, j, k: (i, j))
    in_specs = [a_spec, b_spec] + ([o_spec] if has_init else [])
    args = (a, b) + ((init,) if has_init else ())
    return _call(body, name, jax.ShapeDtypeStruct((M, N), out_dtype), grid=(M // tm, N // tn, nk),
                 in_specs=in_specs, out_specs=o_spec, scratch=[pltpu.VMEM((tm, tn), F32)],
                 sem=("parallel", "parallel", "arbitrary"))(*args)


def rowcall(fn, name, nrows, tm, rins, vins, routs, aouts, ainit=None):
    n_r, n_v, n_ro = len(rins), len(vins), len(routs)
    n_i = len(aouts) if ainit is not None else 0

    def body(*refs):
        r_in, v_in, i_in = refs[:n_r], refs[n_r:n_r + n_v], refs[n_r + n_v:n_r + n_v + n_i]
        r_out, a_out = refs[n_r + n_v + n_i:n_r + n_v + n_i + n_ro], refs[n_r + n_v + n_i + n_ro:]
        outs = fn(*[r[...].astype(F32) for r in r_in], *[v[...] for v in v_in])
        if not isinstance(outs, (tuple, list)):
            outs = (outs,)
        for ref, val in zip(r_out, outs[:n_ro]):
            ref[...] = val.astype(ref.dtype)
        if a_out:
            @pl.when(pl.program_id(0) == 0)
            def _():
                for k, ref in enumerate(a_out):
                    ref[...] = i_in[k][...] if n_i else jnp.zeros_like(ref)

            for ref, val in zip(a_out, outs[n_ro:]):
                ref[...] += val.astype(F32)

    def rspec(width, cblk, roff):
        return pl.BlockSpec((tm, width), lambda i: (i + roff, cblk))

    def whole(shape):
        nd = len(shape)
        return pl.BlockSpec(tuple(shape), lambda i: (0,) * nd)

    inits = list(ainit) if n_i else []
    in_specs = [rspec(w, cb, ro) for (_, w, cb, ro) in rins] + [whole(v.shape) for v in vins + inits]
    out_specs = [rspec(w, 0, 0) for (w, _) in routs] + [whole(s) for s in aouts]
    out_shape = [jax.ShapeDtypeStruct((nrows, w), dt) for (w, dt) in routs] + \
                [jax.ShapeDtypeStruct(tuple(s), F32) for s in aouts]
    res = _call(body, name, out_shape, grid=(nrows // tm,), in_specs=in_specs, out_specs=out_specs,
                sem=("arbitrary",))(*[r[0] for r in rins], *vins, *inits)
    return res


def _rms(x):
    return lax.rsqrt(jnp.mean(x * x, axis=-1, keepdims=True) + EPS)


def f_modulate(x, g, sc, sh):
    return (x * _rms(x)) * g * (1.0 + sc) + sh


def f_resid_mod(x, o, ga, g, sc, sh):
    x1 = x + ga * o
    return x1, f_modulate(x1, g, sc, sh)


def f_final_loss(x1, dn, tgt, ga2, gf):
    x2 = x1 + ga2 * dn
    y = (x2 * _rms(x2)) * gf
    err = jnp.square(y - tgt)
    return 0.5 * jnp.sum(jnp.mean(err, axis=-1))


def _sgu_spatial(vn, w, bt):
    lo = lax.broadcasted_iota(jnp.int32, (1, LANES), 1) < (SGU_WIDTH // SGU_GROUPS)
    row_blocks = []
    for r in range(vn.shape[0] // CHUNK):
        rows = vn[r * CHUNK:(r + 1) * CHUNK]
        cols = []
        for j in range(SGU_WIDTH // LANES):
            blk = rows[:, j * LANES:(j + 1) * LANES]
            v_lo = jnp.where(lo, blk, 0.0)
            v_hi = jnp.where(lo, 0.0, blk)
            s = mmul(w[2 * j], v_lo) + mmul(w[2 * j + 1], v_hi)
            bias = jnp.where(lo, bt[:, 2 * j:2 * j + 1], bt[:, 2 * j + 1:2 * j + 2])
            cols.append(s + bias)
        row_blocks.append(jnp.concatenate(cols, axis=1))
    return jnp.concatenate(row_blocks, axis=0) if len(row_blocks) > 1 else row_blocks[0]


def f_mixer(u_a, y0, y1, zu, zv, ga0, ga1, gb0, gb1, d_skip, w_glu, b_glu, ln_g, ln_b, sgu_w, sgu_bt,
            w_pa, w_pb, b_gate):
    ys = u_a * d_skip + y0 + y1
    ge = jax.nn.gelu(ys)
    y_a = ge * jax.nn.sigmoid(mmul(ge, w_glu) + b_glu)
    u_sg = jax.nn.gelu(zu)
    v = jax.nn.gelu(zv)
    vc = v - jnp.mean(v, axis=-1, keepdims=True)
    vn = (vc * lax.rsqrt(jnp.mean(vc * vc, axis=-1, keepdims=True) + EPS)) * ln_g + ln_b
    y_b = u_sg * _sgu_spatial(vn, sgu_w, sgu_bt)
    gl_a = jnp.concatenate([ga0, ga1], axis=1) + b_gate[:, :D_MODEL]
    gl_b = jnp.concatenate([gb0, gb1], axis=1) + b_gate[:, D_MODEL:]
    return jax.nn.sigmoid(gl_a) * mmul(y_a, w_pa) + jax.nn.sigmoid(gl_b) * mmul(y_b, w_pb)


def _cmul(ar, ai, xr, xi):
    return ar * xr - ai * xi, ar * xi + ai * xr


def _scan_chunk(xr, xi, pf_ref, col, rev, conj, rows):
    t_len = xr.shape[0]
    s = 1
    while s < t_len:
        ar = pf_ref[s - 1:s, col:col + LANES]
        ai = pf_ref[s - 1:s, col + 512:col + 512 + LANES]
        if conj:
            ai = -ai
        shift = (t_len - s) if rev else s
        m = (rows < t_len - s) if rev else (rows >= s)
        tr, ti = _cmul(ar, ai, pltpu.roll(xr, shift, 0), pltpu.roll(xi, shift, 0))
        xr = xr + jnp.where(m, tr, 0.0)
        xi = xi + jnp.where(m, ti, 0.0)
        s *= 2
    return xr, xi


def _add_carry(xr, xi, tab_ref, c_ref, col, conj):
    tr = tab_ref[:, col:col + LANES]
    ti = tab_ref[:, col + 512:col + 512 + LANES]
    if conj:
        ti = -ti
    cr = c_ref[0:1, col:col + LANES]
    ci = c_ref[0:1, col + 512:col + 512 + LANES]
    ar, ai = _cmul(tr, ti, cr, ci)
    return xr + ar, xi + ai


def s5_forward(u3, bm, cm, pf, pb, rev, n_chunks, blk0, name):
    T = SCAN_T

    def pos(i):
        return (n_chunks - 1 - i) if rev else i

    def body(u_ref, bm_ref, cm_ref, pf_ref, pb_ref, y_ref, cin_ref, carry):
        @pl.when(pl.program_id(0) == 0)
        def _():
            carry[...] = jnp.zeros_like(carry)

        cin_ref[...] = carry[...]
        rows = lax.broadcasted_iota(jnp.int32, (T, 1), 0)
        tab = pb_ref if rev else pf_ref
        edge = 0 if rev else T - 1
        u = u_ref[...]
        for o in range(OCTETS):
            bu = jnp.dot(u[:, o * LANES:(o + 1) * LANES], bm_ref[o], preferred_element_type=F32)
            hr, hi = [], []
            for j in range(4):
                col = o * 1024 + j * LANES
                xr, xi = _scan_chunk(bu[:, j * LANES:(j + 1) * LANES], bu[:, 512 + j * LANES:512 + (j + 1) * LANES],
                                     pf_ref, col, rev, False, rows)
                xr, xi = _add_carry(xr, xi, tab, cin_ref, col, False)
                carry[0:1, col:col + LANES] = xr[edge:edge + 1]
                carry[0:1, col + 512:col + 512 + LANES] = xi[edge:edge + 1]
                hr.append(xr)
                hi.append(xi)
            h = jnp.concatenate(hr + hi, axis=1).astype(BF16)
            y_ref[:, o * LANES:(o + 1) * LANES] = jnp.dot(h, cm_ref[o], preferred_element_type=F32)

    whole3 = lambda s: pl.BlockSpec(s, lambda i: (0, 0, 0))
    whole2 = lambda s: pl.BlockSpec(s, lambda i: (0, 0))
    return _call(
        body, name,
        [jax.ShapeDtypeStruct((n_chunks * T, S5_WIDTH), F32), jax.ShapeDtypeStruct((n_chunks, 1, 2 * N_STATE), F32)],
        grid=(n_chunks,),
        in_specs=[pl.BlockSpec((T, S5_WIDTH), lambda i: (blk0 + pos(i), 0)), whole3(bm.shape), whole3(cm.shape),
                  whole2(pf.shape), whole2(pb.shape)],
        out_specs=[pl.BlockSpec((T, S5_WIDTH), lambda i: (pos(i), 0)),
                   pl.BlockSpec((None, 1, 2 * N_STATE), lambda i: (pos(i), 0, 0))],
        scratch=[pltpu.VMEM((1, 2 * N_STATE), F32)], sem=("arbitrary",))(u3, bm, cm, pf, pb)


def s5_backward(u3, dy3, cin, bm, cm, pf, pb, rev, n_chunks, blk0, name):
    T = SCAN_T

    def pos(i):
        return i if rev else (n_chunks - 1 - i)

    def body(u_ref, dy_ref, cin_ref, bm_ref, cm_ref, pf_ref, pb_ref, du_ref, dbm_ref, dcm_ref, da_ref, lcarry):
        @pl.when(pl.program_id(0) == 0)
        def _():
            lcarry[...] = jnp.zeros_like(lcarry)
            dbm_ref[...] = jnp.zeros_like(dbm_ref)
            dcm_ref[...] = jnp.zeros_like(dcm_ref)
            da_ref[...] = jnp.zeros_like(da_ref)

        rows = lax.broadcasted_iota(jnp.int32, (T, 1), 0)
        tab_h = pb_ref if rev else pf_ref
        tab_l = pf_ref if rev else pb_ref
        first = rows == (T - 1 if rev else 0)
        ledge = T - 1 if rev else 0
        u = u_ref[...]
        dy = dy_ref[...]
        for o in range(OCTETS):
            u_o = u[:, o * LANES:(o + 1) * LANES]
            dy_o = dy[:, o * LANES:(o + 1) * LANES]
            bu = jnp.dot(u_o, bm_ref[o], preferred_element_type=F32)
            g = lax.dot_general(dy_o, cm_ref[o], (((1,), (1,)), ((), ())), preferred_element_type=F32)
            hr, hi, lr, li = [], [], [], []
            for j in range(4):
                col = o * 1024 + j * LANES
                sl_r = slice(j * LANES, (j + 1) * LANES)
                sl_i = slice(512 + j * LANES, 512 + (j + 1) * LANES)
                xr, xi = _scan_chunk(bu[:, sl_r], bu[:, sl_i], pf_ref, col, rev, False, rows)
                xr, xi = _add_carry(xr, xi, tab_h, cin_ref, col, False)
                shift = (T - 1) if rev else 1
                pr = jnp.where(first, cin_ref[0:1, col:col + LANES], pltpu.roll(xr, shift, 0))
                pi = jnp.where(first, cin_ref[0:1, col + 512:col + 512 + LANES], pltpu.roll(xi, shift, 0))
                ar_, ai_ = _scan_chunk(g[:, sl_r], g[:, sl_i], pf_ref, col, not rev, True, rows)
                ar_, ai_ = _add_carry(ar_, ai_, tab_l, lcarry, col, True)
                da_ref[0:1, col:col + LANES] += jnp.sum(ar_ * pr + ai_ * pi, axis=0, keepdims=True)
                da_ref[0:1, col + 512:col + 512 + LANES] += jnp.sum(ai_ * pr - ar_ * pi, axis=0, keepdims=True)
                lcarry[0:1, col:col + LANES] = ar_[ledge:ledge + 1]
                lcarry[0:1, col + 512:col + 512 + LANES] = ai_[ledge:ledge + 1]
                hr.append(xr)
                hi.append(xi)
                lr.append(ar_)
                li.append(ai_)
            h = jnp.concatenate(hr + hi, axis=1).astype(BF16)
            lam = jnp.concatenate(lr + li, axis=1).astype(BF16)
            du_ref[:, o * LANES:(o + 1) * LANES] = lax.dot_general(
                lam, bm_ref[o], (((1,), (1,)), ((), ())), preferred_element_type=F32)
            dbm_ref[o] += lax.dot_general(u_o, lam, (((0,), (0,)), ((), ())), preferred_element_type=F32)
            dcm_ref[o] += lax.dot_general(h, dy_o, (((0,), (0,)), ((), ())), preferred_element_type=F32)

    whole3 = lambda s: pl.BlockSpec(s, lambda i: (0, 0, 0))
    whole2 = lambda s: pl.BlockSpec(s, lambda i: (0, 0))
    row_spec = pl.BlockSpec((T, S5_WIDTH), lambda i: (blk0 + pos(i), 0))
    return _call(
        body, name,
        [jax.ShapeDtypeStruct((n_chunks * T, S5_WIDTH), F32), jax.ShapeDtypeStruct(bm.shape, F32),
         jax.ShapeDtypeStruct(cm.shape, F32), jax.ShapeDtypeStruct((1, 2 * N_STATE), F32)],
        grid=(n_chunks,),
        in_specs=[row_spec, row_spec, pl.BlockSpec((None, 1, 2 * N_STATE), lambda i: (pos(i), 0, 0)),
                  whole3(bm.shape), whole3(cm.shape), whole2(pf.shape), whole2(pb.shape)],
        out_specs=[pl.BlockSpec((T, S5_WIDTH), lambda i: (pos(i), 0)), whole3(bm.shape), whole3(cm.shape),
                   whole2((1, 2 * N_STATE))],
        scratch=[pltpu.VMEM((1, 2 * N_STATE), F32)], sem=("arbitrary",))(u3, dy3, cin, bm, cm, pf, pb)


def s5_tables(ar, ai, ls):
    def body(ar_ref, ai_ref, ls_ref, pr_ref, pi_ref):
        m = (lax.broadcasted_iota(jnp.int32, (8, 1), 0) + 1).astype(F32)
        dt = jnp.exp(ls_ref[...])
        mag = jnp.exp(m * (ar_ref[...] * dt))
        ang = m * (ai_ref[...] * dt)
        pr, pi = mag * jnp.cos(ang), mag * jnp.sin(ang)
        s = 8
        while s < SCAN_T:
            nr, ni = _cmul(pr[s - 1:s], pi[s - 1:s], pr, pi)
            pr, pi = jnp.concatenate([pr, nr], axis=0), jnp.concatenate([pi, ni], axis=0)
            s *= 2
        pr_ref[...] = pr
        pi_ref[...] = pi

    vec = pl.BlockSpec((None, 1, N_STATE), lambda d: (d, 0, 0))
    tab = pl.BlockSpec((None, SCAN_T, N_STATE), lambda d: (d, 0, 0))
    return _call(body, "s5_tables", [jax.ShapeDtypeStruct((2, SCAN_T, N_STATE), F32)] * 2, grid=(2,),
                 in_specs=[vec, vec, vec], out_specs=[tab, tab], sem=("arbitrary",))(ar, ai, ls)


def f_discretize(a_re, a_im, ls, b_re, b_im):
    dt = jnp.exp(ls)
    mag = jnp.exp(a_re * dt)
    ab_re = mag * jnp.cos(a_im * dt)
    ab_im = mag * jnp.sin(a_im * dt)
    p = ab_re - 1.0
    q = ab_im
    den = a_re * a_re + a_im * a_im
    k_re = ((p * a_re + q * a_im) / den)[None]
    k_im = ((q * a_re - p * a_im) / den)[None]
    return ab_re, ab_im, k_re * b_re - k_im * b_im, k_re * b_im + k_im * b_re


def _disc_specs():
    a = pl.BlockSpec((None, S5_GROUPS, S5_STATE), lambda d: (d, 0, 0))
    s = pl.BlockSpec((None, S5_GROUPS, 1), lambda d: (d, 0, 0))
    b = pl.BlockSpec((None, S5_GROUP, S5_GROUPS, S5_STATE), lambda d: (d, 0, 0, 0))
    return a, s, b


def s5_discretize(a_re, a_im, ls, b_re, b_im):
    def body(ar, ai, l, br, bi, obr, obi):
        _, _, r, i = f_discretize(ar[...], ai[...], l[...], br[...], bi[...])
        obr[...] = r
        obi[...] = i

    a, s, b = _disc_specs()
    return _call(body, "s5_discretize", [jax.ShapeDtypeStruct(b_re.shape, F32)] * 2, grid=(2,),
                 in_specs=[a, a, s, b, b], out_specs=[b, b], sem=("arbitrary",))(a_re, a_im, ls, b_re, b_im)


def s5_discretize_bwd(a_re, a_im, ls, b_re, b_im, dab_re, dab_im, dbb_re, dbb_im):
    def body(ar, ai, l, br, bi, c0, c1, c2, c3, o0, o1, o2, o3, o4):
        _, vjp = jax.vjp(f_discretize, ar[...], ai[...], l[...], br[...], bi[...])
        outs = vjp((c0[...], c1[...], c2[...], c3[...]))
        for ref, val in zip((o0, o1, o2, o3, o4), outs):
            ref[...] = val

    a, s, b = _disc_specs()
    shapes = [jax.ShapeDtypeStruct(t.shape, F32) for t in (a_re, a_im, ls, b_re, b_im)]
    return _call(body, "s5_discretize_bwd", shapes, grid=(2,), in_specs=[a, a, s, b, b, a, a, b, b],
                 out_specs=[a, a, s, b, b], sem=("arbitrary",))(a_re, a_im, ls, b_re, b_im, dab_re, dab_im,
                                                                 dbb_re, dbb_im)


def _conv_taps(s_ref, base, n_rows):
    n = n_rows + 2 * CONV_PAD
    ext = s_ref[pl.ds(base, n), :]
    col = (lax.broadcasted_iota(jnp.int32, (n, 1), 0) + (2 * GRID_W - CONV_PAD)) % GRID_W
    left = jnp.where(col == 0, 0.0, pltpu.roll(ext, 1, 0))
    right = jnp.where(col == GRID_W - 1, 0.0, pltpu.roll(ext, n - 1, 0))
    return left, ext, right


def _conv_apply(taps, w_ref, n_rows, flip):
    out = None
    for i in range(3):
        wi = 2 - i if flip else i
        comb = None
        for j in range(3):
            wj = 2 - j if flip else j
            term = w_ref[wi * 3 + wj:wi * 3 + wj + 1, :] * taps[j]
            comb = term if comb is None else comb + term
        start = CONV_PAD + (i - 1) * GRID_W
        part = comb[start:start + n_rows]
        out = part if out is None else out + part
    return out


def _conv_fill(dst_ref, src_ref, n_tok):
    zeros = jnp.zeros((CONV_PAD, LANES), F32)
    dst_ref[0:CONV_PAD, :] = zeros
    dst_ref[CONV_PAD + n_tok:2 * CONV_PAD + n_tok, :] = zeros

    def step(r, carry):
        base = pl.multiple_of(r * CONV_ROWS, CONV_ROWS)
        dst_ref[pl.ds(base + CONV_PAD, CONV_ROWS), :] = src_ref[pl.ds(base, CONV_ROWS), :].astype(F32)
        return carry

    lax.fori_loop(0, n_tok // CONV_ROWS, step, 0)


def conv_forward(up, wg, wv, bias):
    n_tok = up.shape[0]
    nb = FFN_HIDDEN // LANES

    def body(ug_ref, uv_ref, wg_ref, wv_ref, bg_ref, bv_ref, act_ref, sg, sv):
        _conv_fill(sg, ug_ref, n_tok)
        _conv_fill(sv, uv_ref, n_tok)

        def step(r, carry):
            base = pl.multiple_of(r * CONV_ROWS, CONV_ROWS)
            gate = _conv_apply(_conv_taps(sg, base, CONV_ROWS), wg_ref, CONV_ROWS, False) + bg_ref[...]
            val = _conv_apply(_conv_taps(sv, base, CONV_ROWS), wv_ref, CONV_ROWS, False) + bv_ref[...]
            act_ref[pl.ds(base, CONV_ROWS), :] = (gate * jax.nn.sigmoid(gate) * val).astype(BF16)
            return carry

        lax.fori_loop(0, n_tok // CONV_ROWS, step, 0)

    col = lambda off: pl.BlockSpec((n_tok, LANES), lambda k: (0, k + off))
    wsp = lambda off: pl.BlockSpec((16, LANES), lambda k: (0, k + off))
    bsp = lambda off: pl.BlockSpec((1, LANES), lambda k: (0, k + off))
    pad = pltpu.VMEM((n_tok + 2 * CONV_PAD, LANES), F32)
    return _call(body, "conv_forward", jax.ShapeDtypeStruct((n_tok, FFN_HIDDEN), BF16), grid=(nb,),
                 in_specs=[col(0), col(nb), wsp(0), wsp(0), bsp(0), bsp(nb)], out_specs=col(0),
                 scratch=[pad, pad], sem=("arbitrary",))(up, up, wg, wv, bias, bias)


def conv_backward(up, dact, wg, wv, bias):
    n_tok = up.shape[0]
    nb = FFN_HIDDEN // LANES
    n_steps = n_tok // CONV_ROWS

    def body(ug_ref, uv_ref, da_ref, wg_ref, wv_ref, bg_ref, bv_ref, dug_ref, duv_ref, dwg_ref, dwv_ref,
             sg, sv, sdg, sdv):
        _conv_fill(sg, ug_ref, n_tok)
        _conv_fill(sv, uv_ref, n_tok)
        zeros = jnp.zeros((CONV_PAD, LANES), F32)
        for s_ref in (sdg, sdv):
            s_ref[0:CONV_PAD, :] = zeros
            s_ref[CONV_PAD + n_tok:2 * CONV_PAD + n_tok, :] = zeros
        dwg_ref[...] = jnp.zeros_like(dwg_ref)
        dwv_ref[...] = jnp.zeros_like(dwv_ref)

        def grads(r, carry):
            base = pl.multiple_of(r * CONV_ROWS, CONV_ROWS)
            taps_g = _conv_taps(sg, base, CONV_ROWS)
            taps_v = _conv_taps(sv, base, CONV_ROWS)
            gate = _conv_apply(taps_g, wg_ref, CONV_ROWS, False) + bg_ref[...]
            val = _conv_apply(taps_v, wv_ref, CONV_ROWS, False) + bv_ref[...]
            d_act = da_ref[pl.ds(base, CONV_ROWS), :].astype(F32)
            sig = jax.nn.sigmoid(gate)
            d_gate = d_act * val * (sig * (1.0 + gate * (1.0 - sig)))
            d_val = d_act * (gate * sig)
            sdg[pl.ds(base + CONV_PAD, CONV_ROWS), :] = d_gate
            sdv[pl.ds(base + CONV_PAD, CONV_ROWS), :] = d_val
            for d_out, taps, dw_ref in ((d_gate, taps_g, dwg_ref), (d_val, taps_v, dwv_ref)):
                for i in range(3):
                    start = CONV_PAD + (i - 1) * GRID_W
                    for j in range(3):
                        k = i * 3 + j
                        dw_ref[k:k + 1, :] += jnp.sum(d_out * taps[j][start:start + CONV_ROWS], axis=0, keepdims=True)
                dw_ref[9:10, :] += jnp.sum(d_out, axis=0, keepdims=True)
            return carry

        lax.fori_loop(0, n_steps, grads, 0)

        def spread(r, carry):
            base = pl.multiple_of(r * CONV_ROWS, CONV_ROWS)
            dug_ref[pl.ds(base, CONV_ROWS), :] = _conv_apply(
                _conv_taps(sdg, base, CONV_ROWS), wg_ref, CONV_ROWS, True).astype(BF16)
            duv_ref[pl.ds(base, CONV_ROWS), :] = _conv_apply(
                _conv_taps(sdv, base, CONV_ROWS), wv_ref, CONV_ROWS, True).astype(BF16)
            return carry

        lax.fori_loop(0, n_steps, spread, 0)

    col = lambda off: pl.BlockSpec((n_tok, LANES), lambda k: (0, k + off))
    wsp = pl.BlockSpec((16, LANES), lambda k: (0, k))
    bsp = lambda off: pl.BlockSpec((1, LANES), lambda k: (0, k + off))
    pad = pltpu.VMEM((n_tok + 2 * CONV_PAD, LANES), F32)
    half = jax.ShapeDtypeStruct((n_tok, FFN_HIDDEN), BF16)
    dw = jax.ShapeDtypeStruct((16, FFN_HIDDEN), F32)
    return _call(body, "conv_backward", [half, half, dw, dw], grid=(nb,),
                 in_specs=[col(0), col(nb), col(0), wsp, wsp, bsp(0), bsp(nb)],
                 out_specs=[col(0), col(0), wsp, wsp], scratch=[pad, pad, pad, pad],
                 sem=("arbitrary",))(up, up, dact, wg, wv, bias, bias)


def f_adamw(w, g, m, v):
    m = ADAM_B1 * m + (1.0 - ADAM_B1) * g
    v = ADAM_B2 * v + (1.0 - ADAM_B2) * jnp.square(g)
    m_hat = m / (1.0 - ADAM_B1 ** ADAM_STEP)
    v_hat = v / (1.0 - ADAM_B2 ** ADAM_STEP)
    delta = -ADAM_LR * (m_hat / (jnp.sqrt(v_hat) + ADAM_EPS) + ADAM_WD * w)
    return delta, m, v


def adamw(w, g, m, v, name):
    shape = w.shape
    cols = shape[-1]
    rows = w.size // cols
    two_d = [t.reshape(rows, cols) for t in (w, g, m, v)]
    tm = _tile(rows, 256, 8) if rows % 8 == 0 else rows
    outs = rowcall(f_adamw, name, rows, tm, [(t, cols, 0, 0) for t in two_d], [], [(cols, F32)] * 3, [])
    return tuple(o.reshape(shape) for o in outs)


def _place():
    return lax.axis_index("x"), lax.axis_index("y"), lax.axis_index("c")


_ANY = pl.BlockSpec(memory_space=pl.ANY)


def allgather_devices(v, name):
    def body(v_ref, out_ref, send_sems, recv_sems, local_sem):
        x, y, c = _place()
        me, sibling = (x, y, c), (x, y, 1 - c)
        chips = [(1 - x, y), (x, 1 - y), (1 - x, 1 - y)]

        def slot(p):
            return out_ref.at[4 * p[0] + 2 * p[1] + p[2]]

        def copy(k, block, to, src=None):
            return pltpu.make_async_remote_copy(
                src_ref=slot(block) if src is None else src, dst_ref=slot(block),
                send_sem=send_sems.at[k], recv_sem=recv_sems.at[k], device_id=to, device_id_type=MESH)

        mine = pltpu.make_async_copy(v_ref, slot(me), local_sem)
        mine.start()
        first = [copy(0, me, sibling, src=v_ref)]
        first += [copy(1 + j, me, (*chip, c), src=v_ref) for j, chip in enumerate(chips)]
        for cp in first:
            cp.start()
        passed = [copy(4 + j, (*chip, c), sibling) for j, chip in enumerate(chips)]
        for j, chip in enumerate(chips):
            copy(1 + j, (*chip, c), me).wait_recv()
            passed[j].start()
        copy(0, sibling, me).wait_recv()
        for j, chip in enumerate(chips):
            copy(4 + j, (*chip, 1 - c), me).wait_recv()
        for cp in first + passed:
            cp.wait_send()
        mine.wait()

    return _call(body, name, jax.ShapeDtypeStruct((N_DEV,) + v.shape, v.dtype), in_specs=[_ANY], out_specs=_ANY,
                 scratch=[pltpu.SemaphoreType.DMA((7,)), pltpu.SemaphoreType.DMA((7,)), pltpu.SemaphoreType.DMA])(v)


def allgather_chips(v, name):
    half = v.shape[0] // 2

    def body(v_ref, out_ref, send_sems, recv_sems, local_sem):
        x, y, c = _place()
        sibling = (x, y, 1 - c)
        chips = [(1 - x, y), (x, 1 - y), (1 - x, 1 - y)]

        def rows(chip, h):
            return out_ref.at[2 * chip[0] + chip[1], pl.ds(h * half, half)]

        def copy(k, chip, h, to, src=None):
            return pltpu.make_async_remote_copy(
                src_ref=rows(chip, h) if src is None else src, dst_ref=rows(chip, h),
                send_sem=send_sems.at[k], recv_sem=recv_sems.at[k], device_id=to, device_id_type=MESH)

        mine = pltpu.make_async_copy(v_ref, out_ref.at[2 * x + y], local_sem)
        mine.start()
        first = [copy(j, (x, y), c, (*chip, c), src=v_ref.at[pl.ds(c * half, half)]) for j, chip in enumerate(chips)]
        for cp in first:
            cp.start()
        passed = [copy(3 + j, chip, c, sibling) for j, chip in enumerate(chips)]
        for j, chip in enumerate(chips):
            copy(j, chip, c, (x, y, c)).wait_recv()
            passed[j].start()
        for j, chip in enumerate(chips):
            copy(3 + j, chip, 1 - c, (x, y, c)).wait_recv()
        for cp in first + passed:
            cp.wait_send()
        mine.wait()

    return _call(body, name, jax.ShapeDtypeStruct((N_CHIPS,) + v.shape, v.dtype), in_specs=[_ANY], out_specs=_ANY,
                 scratch=[pltpu.SemaphoreType.DMA((6,)), pltpu.SemaphoreType.DMA((6,)), pltpu.SemaphoreType.DMA])(v)


def swap_sibling(v, name):
    def body(v_ref, out_ref, send_sem, recv_sem):
        x, y, c = _place()
        cp = pltpu.make_async_remote_copy(src_ref=v_ref, dst_ref=out_ref, send_sem=send_sem, recv_sem=recv_sem,
                                          device_id=(x, y, 1 - c), device_id_type=MESH)
        cp.start()
        cp.wait()

    return _call(body, name, jax.ShapeDtypeStruct(v.shape, v.dtype), in_specs=[_ANY], out_specs=_ANY,
                 scratch=[pltpu.SemaphoreType.DMA, pltpu.SemaphoreType.DMA])(v)


def exchange_chips(v, name):
    def body(v_ref, out_ref, send_sems, recv_sems, local_sem):
        x, y, c = _place()
        chips = [(1 - x, y), (x, 1 - y), (1 - x, 1 - y)]
        mine = pltpu.make_async_copy(v_ref.at[2 * x + y], out_ref.at[2 * x + y], local_sem)
        mine.start()
        sends = [pltpu.make_async_remote_copy(
            src_ref=v_ref.at[2 * chip[0] + chip[1]], dst_ref=out_ref.at[2 * x + y], send_sem=send_sems.at[j],
            recv_sem=recv_sems.at[j], device_id=(*chip, c), device_id_type=MESH) for j, chip in enumerate(chips)]
        for cp in sends:
            cp.start()
        for j, chip in enumerate(chips):
            pltpu.make_async_remote_copy(
                src_ref=v_ref.at[2 * x + y], dst_ref=out_ref.at[2 * chip[0] + chip[1]], send_sem=send_sems.at[j],
                recv_sem=recv_sems.at[j], device_id=(*chip, c), device_id_type=MESH).wait_recv()
        for cp in sends:
            cp.wait_send()
        mine.wait()

    return _call(body, name, jax.ShapeDtypeStruct(v.shape, v.dtype), in_specs=[_ANY], out_specs=_ANY,
                 scratch=[pltpu.SemaphoreType.DMA((3,)), pltpu.SemaphoreType.DMA((3,)), pltpu.SemaphoreType.DMA])(v)


ROW_TILE = 256

BIG_SHARDS = (("w_in", (D_MODEL, 896), 1), ("s5_w_glu", (128, S5_WIDTH), 0), ("w_proj_a", (S5_WIDTH, 256), 1),
              ("w_proj_b", (SGU_WIDTH, 256), 1), ("w_out", (256, D_MODEL), 0), ("w_up", (D_MODEL, 1408), 1),
              ("w_down", (704, D_MODEL), 0))
SMALL_PARAMS = ("g_mix", "s5_a_re", "s5_a_im", "s5_log_step", "s5_b_re", "s5_b_im", "s5_c_re", "s5_c_im", "s5_d",
                "s5_b_glu", "sgu_ln_g", "sgu_ln_b", "sgu_w", "sgu_b", "b_gate", "g_ffn", "conv_b", "g_final")
PACK_COLS = 1024


def _rows_of(n):
    return -(-n // PACK_COLS)


def _pack_rows(arrays, total_rows, dtype):
    parts = []
    used = 0
    for a in arrays:
        r = _rows_of(a.size)
        parts.append(jnp.pad(a.reshape(-1).astype(dtype), (0, r * PACK_COLS - a.size)).reshape(r, PACK_COLS))
        used += r
    if total_rows > used:
        parts.append(jnp.zeros((total_rows - used, PACK_COLS), dtype))
    return jnp.concatenate(parts, axis=0)


def _unpack_rows(packed, shapes, row0=0):
    out = []
    for s in shapes:
        n = 1
        for d in s:
            n *= d
        r = _rows_of(n)
        out.append(packed[row0:row0 + r].reshape(-1)[:n].reshape(s))
        row0 += r
    return out


def _octet_major(re, im):
    parts = []
    for o in range(OCTETS):
        parts += [re[:, o * 512:(o + 1) * 512], im[:, o * 512:(o + 1) * 512]]
    return jnp.concatenate(parts, axis=1)


def _octet_split(v):
    v = v.reshape(OCTETS, 2, 512)
    return v[:, 0].reshape(N_STATE), v[:, 1].reshape(N_STATE)


def _s5_bmat(bb):
    t = bb.reshape(S5_GROUP, OCTETS, 8, 1, S5_STATE) * jnp.eye(8, dtype=F32)[None, None, :, :, None]
    return jnp.transpose(t, (1, 2, 0, 3, 4)).reshape(OCTETS, LANES, 512)


def _s5_bmat_t(dm):
    t = dm.reshape(OCTETS, 8, S5_GROUP, 8, S5_STATE) * jnp.eye(8, dtype=F32)[None, :, None, :, None]
    return jnp.transpose(t.sum(axis=3), (2, 0, 1, 3)).reshape(S5_GROUP, S5_GROUPS, S5_STATE)


def _s5_cmat(c):
    t = c.reshape(OCTETS, 8, 1, S5_GROUP, S5_STATE) * jnp.eye(8, dtype=F32)[None, :, :, None, None]
    return jnp.transpose(t, (0, 2, 4, 1, 3)).reshape(OCTETS, 512, LANES)


def _s5_cmat_t(dm):
    t = dm.reshape(OCTETS, 8, S5_STATE, 8, S5_GROUP) * jnp.eye(8, dtype=F32)[None, :, None, :, None]
    return jnp.transpose(t.sum(axis=1), (0, 2, 3, 1)).reshape(S5_GROUPS, S5_GROUP, S5_STATE)


def local_step(x, ctx, tgt, mod, modc, W):
    n_tok, n_ctx = x.shape[0], ctx.shape[0]
    tm = ROW_TILE
    D = D_MODEL
    sh1, sc1, ga1, sh2, sc2, ga2 = [mod[:, k * D:(k + 1) * D] for k in range(N_MOD)]
    sh1c, sc1c = modc[:, :D], modc[:, D:2 * D]
    g_mix, g_ffn, g_final = W["g_mix"], W["g_ffn"], W["g_final"]
    w_in = W["w_in"]
    w_in_u = w_in[:, :S5_WIDTH]

    h = rowcall(f_modulate, "mod1", n_tok, tm, [(x, D, 0, 0)], [g_mix, sc1, sh1], [(D, BF16)], [])[0]
    hc = rowcall(f_modulate, "mod1_ctx", n_ctx, tm, [(ctx, D, 0, 0)], [g_mix, sc1c, sh1c], [(D, BF16)], [])[0]
    proj = matmul(h, w_in, "nn", BF16, "proj_in")
    uc = matmul(hc, w_in_u, "nn", BF16, "proj_in_ctx")
    u3 = jnp.concatenate([uc, proj[:, :S5_WIDTH], uc], axis=0)

    a_re, a_im, ls = W["s5_a_re"], W["s5_a_im"], W["s5_log_step"][..., None]
    b_re_t = jnp.transpose(W["s5_b_re"], (0, 3, 1, 2))
    b_im_t = jnp.transpose(W["s5_b_im"], (0, 3, 1, 2))
    bb_re, bb_im = s5_discretize(a_re, a_im, ls, b_re_t, b_im_t)
    ls_rep = jnp.repeat(W["s5_log_step"], S5_STATE, axis=1).reshape(2, 1, N_STATE)
    p_re, p_im = s5_tables(a_re.reshape(2, 1, N_STATE), a_im.reshape(2, 1, N_STATE), ls_rep)
    n_chunks = (n_ctx + n_tok) // SCAN_T
    ctx_blk = n_ctx // SCAN_T
    pf, pb, bm, cm = [], [], [], []
    for d in range(2):
        pf.append(_octet_major(p_re[d], p_im[d]))
        pb.append(pf[d][::-1])
        bm.append(jnp.concatenate([_s5_bmat(bb_re[d]), _s5_bmat(bb_im[d])], axis=2).astype(BF16))
        cm.append(jnp.concatenate([_s5_cmat(W["s5_c_re"][d]), -_s5_cmat(W["s5_c_im"][d])], axis=1).astype(BF16))
    y0, cin0 = s5_forward(u3, bm[0], cm[0], pf[0], pb[0], False, n_chunks, 0, "s5_fwd0")
    y1, cin1 = s5_forward(u3, bm[1], cm[1], pf[1], pb[1], True, n_chunks, ctx_blk, "s5_fwd1")

    mix_rows = [(proj, 512, 0, 0), (y0, 512, 0, n_ctx // tm), (y1, 512, 0, 0)] + \
               [(proj, 512, k, 0) for k in range(1, 7)]
    mix_vecs = [W["s5_d"], W["s5_w_glu"], W["s5_b_glu"], W["sgu_ln_g"], W["sgu_ln_b"], W["sgu_w"],
                jnp.transpose(W["sgu_b"]), W["w_proj_a"], W["w_proj_b"], W["b_gate"]]
    mrg = rowcall(f_mixer, "mixer", n_tok, tm, mix_rows, mix_vecs, [(D, BF16)], [])[0]
    o = matmul(mrg, W["w_out"], "nn", F32, "proj_out")
    x1, h2 = rowcall(f_resid_mod, "resid_mod2", n_tok, tm, [(x, D, 0, 0), (o, D, 0, 0)], [ga1, g_ffn, sc2, sh2],
                     [(D, F32), (D, BF16)], [])
    up = matmul(h2, W["w_up"], "nn", BF16, "ffn_up")
    conv_w = W["conv_w"].reshape(9, 2 * FFN_HIDDEN)
    wg = jnp.pad(conv_w[:, :FFN_HIDDEN], ((0, 7), (0, 0)))
    wv = jnp.pad(conv_w[:, FFN_HIDDEN:], ((0, 7), (0, 0)))
    act = conv_forward(up, wg, wv, W["conv_b"])
    dn = matmul(act, W["w_down"], "nn", F32, "ffn_down")

    def final_fn(x1_, dn_, tgt_, ga2_, gf_):
        loss, (dx1_, ddn_, dga2_, dgf_) = jax.value_and_grad(f_final_loss, argnums=(0, 1, 3, 4))(
            x1_, dn_, tgt_, ga2_, gf_)
        return dx1_, ddn_, loss.reshape(1, 1), dga2_, dgf_

    dx2, ddn, loss, d_ga2, d_gfinal = rowcall(
        final_fn, "final_loss", n_tok, tm, [(x1, D, 0, 0), (dn, D, 0, 0), (tgt, D, 0, 0)], [ga2, g_final],
        [(D, F32), (D, BF16)], [(1, 1), (1, D), (1, D)])

    dact = matmul(ddn, W["w_down"], "nt", BF16, "ffn_down_dx")
    d_w_down = matmul(act, ddn, "tn", F32, "ffn_down_dw")
    dup_g, dup_v, dwg, dwv = conv_backward(up, dact, wg, wv, W["conv_b"])
    dup = jnp.concatenate([dup_g, dup_v], axis=1)
    d_conv_w = jnp.concatenate([dwg[:9], dwv[:9]], axis=1).reshape(3, 3, 2 * FFN_HIDDEN)
    d_conv_b = jnp.concatenate([dwg[9:10], dwv[9:10]], axis=1)
    dh2 = matmul(dup, W["w_up"], "nt", BF16, "ffn_up_dx")
    d_w_up = matmul(h2, dup, "tn", F32, "ffn_up_dw")

    def resid_bwd(x_, o_, dx1_, dh2_, ga_, g_, sc_, sh_):
        _, vjp = jax.vjp(f_resid_mod, x_, o_, ga_, g_, sc_, sh_)
        return vjp((dx1_, dh2_))

    dxa, do, d_ga1, d_gffn, d_sc2, d_sh2 = rowcall(
        resid_bwd, "resid_mod2_bwd", n_tok, tm, [(x, D, 0, 0), (o, D, 0, 0), (dx2, D, 0, 0), (dh2, D, 0, 0)],
        [ga1, g_ffn, sc2, sh2], [(D, F32), (D, BF16)], [(1, D)] * 4)

    dmrg = matmul(do, W["w_out"], "nt", BF16, "proj_out_dx")
    d_w_out = matmul(mrg, do, "tn", F32, "proj_out_dw")

    def mixer_bwd(*args):
        rows, dm, vecs = args[:9], args[9], [v.astype(F32) for v in args[10:]]
        _, vjp = jax.vjp(f_mixer, *rows, *vecs)
        g = vjp(dm)
        return (g[0], g[1], jnp.concatenate([g[3], g[4]], axis=1), jnp.concatenate(g[5:9], axis=1)) + tuple(g[9:])

    mb = rowcall(mixer_bwd, "mixer_bwd", n_tok, tm, mix_rows + [(dmrg, D, 0, 0)], mix_vecs,
                 [(512, BF16), (512, BF16), (1024, BF16), (2048, BF16)], [v.shape for v in mix_vecs])
    du_direct, dys, dzb, dgl = mb[:4]
    d_s5d, d_w_glu, d_b_glu, d_ln_g, d_ln_b, d_sgu_w, d_sgu_bt, d_w_pa, d_w_pb, d_b_gate = mb[4:]

    zc = jnp.zeros((n_ctx, S5_WIDTH), BF16)
    dy3 = jnp.concatenate([zc, dys, zc], axis=0)
    du0, dbm0, dcm0, da0 = s5_backward(u3, dy3, cin0, bm[0], cm[0], pf[0], pb[0], False, n_chunks, 0, "s5_bwd0")
    du1, dbm1, dcm1, da1 = s5_backward(u3, dy3, cin1, bm[1], cm[1], pf[1], pb[1], True, n_chunks, ctx_blk, "s5_bwd1")
    add3 = lambda a, b, c: a + b + c
    du_a = rowcall(add3, "du_sum", n_tok, tm, [(du_direct, 512, 0, 0), (du0, 512, 0, n_ctx // tm), (du1, 512, 0, 0)],
                   [], [(512, BF16)], [])[0]
    du_c = rowcall(lambda a, b: a + b, "du_sum_ctx", n_ctx, tm, [(du0, 512, 0, 0), (du1, 512, 0, n_tok // tm)],
                   [], [(512, BF16)], [])[0]

    dab_re, dab_im, dbb_re, dbb_im, d_c_re, d_c_im = [], [], [], [], [], []
    for dbm, dcm, da in ((dbm0, dcm0, da0), (dbm1, dcm1, da1)):
        r, i = _octet_split(da)
        dab_re.append(r.reshape(S5_GROUPS, S5_STATE))
        dab_im.append(i.reshape(S5_GROUPS, S5_STATE))
        dbb_re.append(_s5_bmat_t(dbm[:, :, :512]))
        dbb_im.append(_s5_bmat_t(dbm[:, :, 512:]))
        d_c_re.append(_s5_cmat_t(dcm[:, :512]))
        d_c_im.append(-_s5_cmat_t(dcm[:, 512:]))
    d_a_re, d_a_im, d_ls, d_b_re_t, d_b_im_t = s5_discretize_bwd(
        a_re, a_im, ls, b_re_t, b_im_t, jnp.stack(dab_re), jnp.stack(dab_im), jnp.stack(dbb_re), jnp.stack(dbb_im))

    dproj = jnp.concatenate([du_a, dzb, dgl], axis=1)
    dh = matmul(dproj, w_in, "nt", BF16, "proj_in_dx")
    dhc = matmul(du_c, w_in_u, "nt", BF16, "proj_in_ctx_dx")
    d_w_in_c = matmul(hc, du_c, "tn", F32, "proj_in_ctx_dw")
    d_w_in = matmul(h, dproj, "tn", F32, "proj_in_dw",
                    init=jnp.pad(d_w_in_c, ((0, 0), (0, w_in.shape[1] - S5_WIDTH))))

    def mod_bwd_ctx(x_, dh_, g_, sc_, sh_):
        _, vjp = jax.vjp(f_modulate, x_, g_, sc_, sh_)
        return vjp(dh_)[1:]

    d_gmix_c, d_sc1c, d_sh1c = rowcall(mod_bwd_ctx, "mod1_ctx_bwd", n_ctx, tm, [(ctx, D, 0, 0), (dhc, D, 0, 0)],
                                       [g_mix, sc1c, sh1c], [], [(1, D)] * 3)

    def mod_bwd(x_, dh_, dxa_, g_, sc_, sh_):
        _, vjp = jax.vjp(f_modulate, x_, g_, sc_, sh_)
        dx_, dg_, dsc_, dsh_ = vjp(dh_)
        return dx_ + dxa_, dg_, dsc_, dsh_

    zero_d = jnp.zeros((1, D), F32)
    grad_x, d_gmix, d_sc1, d_sh1 = rowcall(
        mod_bwd, "mod1_bwd", n_tok, tm, [(x, D, 0, 0), (dh, D, 0, 0), (dxa, D, 0, 0)], [g_mix, sc1, sh1],
        [(D, F32)], [(1, D)] * 3, ainit=[d_gmix_c, zero_d, zero_d])

    grads = {
        "dmod": jnp.concatenate([d_sh1, d_sc1, d_ga1, d_sh2, d_sc2, d_ga2], axis=1),
        "dmodc": jnp.concatenate([d_sh1c, d_sc1c], axis=1),
        "g_mix": d_gmix,
        "s5_a_re": d_a_re, "s5_a_im": d_a_im, "s5_log_step": d_ls[..., 0],
        "s5_b_re": jnp.transpose(d_b_re_t, (0, 2, 3, 1)), "s5_b_im": jnp.transpose(d_b_im_t, (0, 2, 3, 1)),
        "s5_c_re": jnp.stack(d_c_re), "s5_c_im": jnp.stack(d_c_im), "s5_d": d_s5d, "s5_b_glu": d_b_glu,
        "sgu_ln_g": d_ln_g, "sgu_ln_b": d_ln_b, "sgu_w": d_sgu_w, "sgu_b": jnp.transpose(d_sgu_bt),
        "b_gate": d_b_gate, "g_ffn": d_gffn, "conv_b": d_conv_b, "g_final": d_gfinal, "conv_w": d_conv_w,
        "w_in": d_w_in, "s5_w_glu": d_w_glu, "w_proj_a": d_w_pa, "w_proj_b": d_w_pb, "w_out": d_w_out,
        "w_up": d_w_up, "w_down": d_w_down,
    }
    return loss, grad_x, grads


ADA_COLS = N_MOD * D_MODEL // N_CHIPS
MOD_ROWS = 16


def mod_forward(c16, w, b):
    n = w.shape[1]
    tn = 512

    def body(c_ref, w_ref, b_ref, o_ref):
        cv = c_ref[...]
        cs = cv * jax.nn.sigmoid(cv)
        o_ref[...] = jnp.dot(cs.astype(BF16), w_ref[...].astype(BF16), preferred_element_type=F32) + b_ref[...]

    return _call(body, "mod_forward", jax.ShapeDtypeStruct((MOD_ROWS, n), F32), grid=(n // tn,),
                 in_specs=[pl.BlockSpec((MOD_ROWS, D_MODEL), lambda j: (0, 0)),
                           pl.BlockSpec((D_MODEL, tn), lambda j: (0, j)), pl.BlockSpec((1, tn), lambda j: (0, j))],
                 out_specs=pl.BlockSpec((MOD_ROWS, tn), lambda j: (0, j)), sem=("arbitrary",))(c16, w, b)


def f_ada_outer(ct, dm):
    cs = ct * jax.nn.sigmoid(ct)
    acc = cs[:, 0:1] * dm[0:1]
    for k in range(1, 9):
        acc = acc + cs[:, k:k + 1] * dm[k:k + 1]
    return acc


def f_cctx_grad(z, p4):
    s = jax.nn.sigmoid(z)
    return (p4[0:1] + p4[1:2] + p4[2:3] + p4[3:4]) * (s + z * s * (1.0 - s))


WEIGHT_NAMES = ("c_ctx", "w_ada", "b_ada", "g_mix", "w_in", "s5_a_re", "s5_a_im", "s5_log_step", "s5_b_re",
                "s5_b_im", "s5_c_re", "s5_c_im", "s5_d", "s5_w_glu", "s5_b_glu", "sgu_ln_g", "sgu_ln_b", "sgu_w",
                "sgu_b", "w_proj_a", "w_proj_b", "b_gate", "w_out", "g_ffn", "w_up", "conv_w", "conv_b", "w_down",
                "g_final")
CONV_SHARD = 2 * FFN_HIDDEN // N_CHIPS
WEIGHT_PACK_ROWS = 3616
GRAD_PACK_ROWS = 3584
SMALL_PACK_ROWS = 512
SMALL_ROW0 = 58
ADAM_PACK_ROWS = 512


def kernel(x, c, ctx, c_ctx, w_ada, b_ada, g_mix, w_in, s5_a_re, s5_a_im, s5_log_step, s5_b_re, s5_b_im, s5_c_re, s5_c_im, s5_d, s5_w_glu, s5_b_glu, sgu_ln_g, sgu_ln_b, sgu_w, sgu_b, w_proj_a, w_proj_b, b_gate, w_out, g_ffn, w_up, conv_w, conv_b, w_down, g_final, loss_target, m_c_ctx, m_w_ada, m_b_ada, m_g_mix, m_w_in, m_s5_a_re, m_s5_a_im, m_s5_log_step, m_s5_b_re, m_s5_b_im, m_s5_c_re, m_s5_c_im, m_s5_d, m_s5_w_glu, m_s5_b_glu, m_sgu_ln_g, m_sgu_ln_b, m_sgu_w, m_sgu_b, m_w_proj_a, m_w_proj_b, m_b_gate, m_w_out, m_g_ffn, m_w_up, m_conv_w, m_conv_b, m_w_down, m_g_final, v_c_ctx, v_w_ada, v_b_ada, v_g_mix, v_w_in, v_s5_a_re, v_s5_a_im, v_s5_log_step, v_s5_b_re, v_s5_b_im, v_s5_c_re, v_s5_c_im, v_s5_d, v_s5_w_glu, v_s5_b_glu, v_sgu_ln_g, v_sgu_ln_b, v_sgu_w, v_sgu_b, v_w_proj_a, v_w_proj_b, v_b_gate, v_w_out, v_g_ffn, v_w_up, v_conv_w, v_conv_b, v_w_down, v_g_final):
    given = dict(locals())
    wts = {n: given[n] for n in WEIGHT_NAMES}
    ms = {n: given["m_" + n] for n in WEIGHT_NAMES}
    vs = {n: given["v_" + n] for n in WEIGHT_NAMES}
    xi, yi, ci = _place()
    chip = 2 * xi + yi
    dev = 2 * chip + ci
    D = D_MODEL

    c8 = allgather_devices(jnp.pad(c, ((0, 7), (0, 0))), "gather_c")[:, 0, :]
    c16 = jnp.concatenate([c8, c_ctx[None], jnp.zeros((MOD_ROWS - 9, D), F32)], axis=0)
    b_shard = lax.dynamic_slice(b_ada, (0, chip * ADA_COLS), (1, ADA_COLS))
    mod_shard = mod_forward(c16, w_ada[0], b_shard)
    mod_all = allgather_devices(mod_shard, "gather_mod")
    mod_full = jnp.concatenate([mod_all[2 * q] for q in range(N_CHIPS)], axis=1)
    mod = lax.dynamic_slice(mod_full, (dev, 0), (1, N_MOD * D))
    modc = mod_full[8:9]

    conv_shard = conv_w[0].reshape(9, CONV_SHARD)
    conv_hi = conv_shard.astype(BF16)
    conv_lo = (conv_shard - conv_hi.astype(F32)).astype(BF16)
    pack = _pack_rows([wts[n][0] for n, _, _ in BIG_SHARDS] + [conv_hi, conv_lo], WEIGHT_PACK_ROWS, BF16)
    w_all = allgather_chips(pack, "gather_weights")
    shapes = [s for _, s, _ in BIG_SHARDS] + [(9, CONV_SHARD)] * 2
    parts = [_unpack_rows(w_all[q], shapes) for q in range(N_CHIPS)]
    W = {}
    for k, (n, _, axis) in enumerate(BIG_SHARDS):
        W[n] = jnp.concatenate([parts[q][k] for q in range(N_CHIPS)], axis=axis)
    nb = len(BIG_SHARDS)
    W["conv_w"] = jnp.concatenate([parts[q][nb].astype(F32) + parts[q][nb + 1].astype(F32) for q in range(N_CHIPS)],
                                  axis=1).reshape(3, 3, 2 * FFN_HIDDEN)
    for n in ("g_mix", "g_ffn", "s5_d", "s5_b_glu", "sgu_ln_g", "sgu_ln_b", "b_gate", "conv_b"):
        W[n] = wts[n]
    W["g_final"] = g_final[None]
    for n in ("s5_a_re", "s5_a_im", "s5_log_step", "s5_b_re", "s5_b_im", "s5_c_re", "s5_c_im", "sgu_w", "sgu_b"):
        W[n] = wts[n][0]

    loss_part, grad_x, g = local_step(x[0], ctx[0], loss_target[0], mod, modc, W)
    loss = lax.psum(loss_part[0, 0], ("x", "y", "c"))

    def shard(a, q, size, axis):
        return lax.slice_in_dim(a, q * size, (q + 1) * size, axis=axis)

    g_pack = jnp.stack([_pack_rows([shard(g[n], q, s[axis], axis) for n, s, axis in BIG_SHARDS], GRAD_PACK_ROWS, F32)
                        for q in range(N_CHIPS)])
    half = GRAD_PACK_ROWS // 2
    keep = lax.dynamic_slice(g_pack, (0, ci * half, 0), (N_CHIPS, half, PACK_COLS))
    send = lax.dynamic_slice(g_pack, (0, (1 - ci) * half, 0), (N_CHIPS, half, PACK_COLS))
    recv = swap_sibling(send, "grad_pair_swap")
    add2 = lambda a, b: a + b
    pair = rowcall(add2, "grad_pair_sum", N_CHIPS * half, 256,
                   [(keep.reshape(N_CHIPS * half, PACK_COLS), PACK_COLS, 0, 0),
                    (recv.reshape(N_CHIPS * half, PACK_COLS), PACK_COLS, 0, 0)], [], [(PACK_COLS, F32)], [])[0]
    from_chips = exchange_chips(pair.reshape(N_CHIPS, half, PACK_COLS), "grad_chip_exchange")
    from_chips = from_chips.reshape(N_CHIPS * half, PACK_COLS)
    add4 = lambda a, b, c_, d: ((a + b) + c_) + d
    mine = rowcall(add4, "grad_chip_sum", half, 256,
                   [(from_chips, PACK_COLS, 0, q * half // 256) for q in range(N_CHIPS)], [], [(PACK_COLS, F32)], [])[0]
    other = swap_sibling(mine, "grad_half_swap")
    g_shard = jnp.where(ci == 0, jnp.concatenate([mine, other], axis=0), jnp.concatenate([other, mine], axis=0))
    big_grads = dict(zip([n for n, _, _ in BIG_SHARDS], _unpack_rows(g_shard, [s for _, s, _ in BIG_SHARDS])))

    small_pack = _pack_rows([g["dmod"], g["dmodc"], g["conv_w"]] + [g[n] for n in SMALL_PARAMS], SMALL_PACK_ROWS, F32)
    small_all = allgather_devices(small_pack, "gather_small_grads")
    small_2d = small_all.reshape(N_DEV * SMALL_PACK_ROWS, PACK_COLS)

    def add8(*a):
        s = a[0]
        for t in a[1:]:
            s = s + t
        return s

    small_sum = rowcall(add8, "small_grad_sum", SMALL_PACK_ROWS, 256,
                        [(small_2d, PACK_COLS, 0, k * SMALL_PACK_ROWS // 256) for k in range(N_DEV)], [],
                        [(PACK_COLS, F32)], [])[0]
    dmod_all = small_all[:, 0:N_MOD].reshape(N_DEV, N_MOD * D)
    dmod_sum = small_sum[0:N_MOD].reshape(1, N_MOD * D)
    dmodc_sum = jnp.pad(small_sum[N_MOD:N_MOD + 2].reshape(1, 2 * D), ((0, 0), (0, (N_MOD - 2) * D)))
    conv_grad = _unpack_rows(small_sum, [(3, 3, 2 * FFN_HIDDEN)], row0=N_MOD + 2)[0]
    small_grads = dict(zip(SMALL_PARAMS, _unpack_rows(small_sum, [wts[n].shape for n in SMALL_PARAMS], row0=SMALL_ROW0)))

    dm16 = jnp.concatenate([dmod_all, dmodc_sum, jnp.zeros((MOD_ROWS - 9, N_MOD * D), F32)], axis=0)
    dm_shard = lax.dynamic_slice(dm16, (0, chip * ADA_COLS), (MOD_ROWS, ADA_COLS))
    g_w_ada = rowcall(f_ada_outer, "w_ada_grad", D, 256, [(jnp.transpose(c16), MOD_ROWS, 0, 0)], [dm_shard],
                      [(ADA_COLS, F32)], [])[0]
    g_b_ada = rowcall(add2, "b_ada_grad", 1, 1, [(dmod_sum, N_MOD * D, 0, 0), (dmodc_sum, N_MOD * D, 0, 0)], [],
                      [(N_MOD * D, F32)], [])[0]
    dmc_rows = jnp.pad(dm_shard[8:9], ((0, 7), (0, 0)))
    cctx_part = matmul(dmc_rows, w_ada[0], "nt", F32, "c_ctx_partial")
    cctx_all = allgather_devices(cctx_part, "gather_c_ctx")
    cctx_4 = jnp.stack([cctx_all[2 * q, 0] for q in range(N_CHIPS)])
    g_c_ctx = rowcall(f_cctx_grad, "c_ctx_grad", 1, 1, [(c_ctx[None], D, 0, 0)], [cctx_4], [(D, F32)], [])[0]

    grads = dict(small_grads)
    grads.update(big_grads)
    grads["w_ada"] = g_w_ada
    grads["b_ada"] = g_b_ada
    grads["c_ctx"] = g_c_ctx
    grads["conv_w"] = lax.dynamic_slice(conv_grad, (0, 0, chip * CONV_SHARD), (3, 3, CONV_SHARD))
    grads = {n: grads[n].reshape(wts[n].shape) for n in WEIGHT_NAMES}

    delta, new_m, new_v = {}, {}, {}
    large = [n for n, _, _ in BIG_SHARDS] + ["w_ada", "conv_w"]
    for n in large:
        shape2d = (-1, wts[n].shape[-1])
        d_, m_, v_ = adamw(wts[n].reshape(shape2d), grads[n].reshape(shape2d), ms[n].reshape(shape2d),
                           vs[n].reshape(shape2d), "adamw_" + n)
        delta[n], new_m[n], new_v[n] = [t.reshape(wts[n].shape) for t in (d_, m_, v_)]
    rest = [n for n in WEIGHT_NAMES if n not in large]
    packs = [_pack_rows([src[n] for n in rest], ADAM_PACK_ROWS, F32) for src in (wts, grads, ms, vs)]
    outs = adamw(*packs, "adamw_replicated")
    for dst, packed in zip((delta, new_m, new_v), outs):
        dst.update(zip(rest, _unpack_rows(packed, [wts[n].shape for n in rest])))

    return (loss, grad_x[None], *[grads[n] for n in WEIGHT_NAMES], *[delta[n] for n in WEIGHT_NAMES],
            *[new_m[n] for n in WEIGHT_NAMES], *[new_v[n] for n in WEIGHT_NAMES])
```

```python
import functools

import jax
import jax.numpy as jnp
from jax import lax
from jax.experimental import pallas as pl
from jax.experimental.pallas import tpu as pltpu

F32, BF16 = jnp.float32, jnp.bfloat16
MESH = pl.DeviceIdType.MESH

D_MODEL = 1024
S5_WIDTH = 512
S5_GROUP = 16
S5_GROUPS = 32
S5_STATE = 64
SGU_WIDTH = 512
SGU_GROUPS = 8
CHUNK = 128
FFN_HIDDEN = 2816
GRID_W = 64
N_MOD = 6
EPS = 1e-6
N_STATE = S5_GROUPS * S5_STATE
OCTETS = 4
SCAN_T = 128
N_CHIPS = 4
N_DEV = 8
LANES = 128
VMEM_LIMIT_BYTES = 56 * 1024 * 1024
CONV_PAD = 72
CONV_ROWS = 256

ADAM_LR, ADAM_B1, ADAM_B2, ADAM_EPS, ADAM_WD, ADAM_STEP = 0.001, 0.9, 0.999, 1e-08, 0.01, 10


def _call(body, name, out_shape, grid=None, in_specs=None, out_specs=None, scratch=(), sem=None, **kw):
    params = pltpu.CompilerParams(dimension_semantics=sem, vmem_limit_bytes=VMEM_LIMIT_BYTES)
    extra = {} if grid is None else {"grid": grid}
    return pl.pallas_call(body, name=name, out_shape=out_shape, in_specs=in_specs, out_specs=out_specs,
                          scratch_shapes=list(scratch), compiler_params=params, **extra, **kw)


def _tile(n, target, mult=LANES):
    best = None
    t = mult
    while t <= min(n, target):
        if n % t == 0:
            best = t
        t += mult
    return best or n


@jax.custom_vjp
def mmul(a, b):
    return jnp.dot(a.astype(BF16), b.astype(BF16), preferred_element_type=F32)


def _mmul_fwd(a, b):
    return mmul(a, b), (a, b)


def _mmul_bwd(res, ct):
    a, b = res
    ctb = ct.astype(BF16)
    da = lax.dot_general(ctb, b.astype(BF16), (((1,), (1,)), ((), ())), preferred_element_type=F32)
    db = lax.dot_general(a.astype(BF16), ctb, (((0,), (0,)), ((), ())), preferred_element_type=F32)
    return da.astype(a.dtype), db.astype(b.dtype)


mmul.defvjp(_mmul_fwd, _mmul_bwd)

_DOT_DIMS = {"nn": ((1,), (0,)), "nt": ((1,), (1,)), "tn": ((0,), (0,))}


MM_TILE = 1408
MM_FULL_K = 2048


def matmul(a, b, mode, out_dtype, name, init=None):
    if mode == "nn":
        (M, K), (_, N) = a.shape, b.shape
    elif mode == "nt":
        (M, K), (N, _) = a.shape, b.shape
    else:
        (K, M), (_, N) = a.shape, b.shape
    tm, tn = _tile(M, MM_TILE, 8 if M < LANES else LANES), _tile(N, MM_TILE)
    tk = K if K <= MM_FULL_K else _tile(K, MM_TILE)
    nk = K // tk
    dims = (_DOT_DIMS[mode], ((), ()))
    has_init = init is not None
    use_acc = nk > 1 and out_dtype != F32

    def body(*refs):
        a_ref, b_ref = refs[:2]
        i_ref = refs[2] if has_init else None
        o_ref = refs[3] if has_init else refs[2]
        acc = refs[-1] if use_acc else o_ref
        k = pl.program_id(2)
        part = lax.dot_general(a_ref[...].astype(BF16), b_ref[...].astype(BF16), dims, preferred_element_type=F32)

        @pl.when(k == 0)
        def _():
            first = part + i_ref[...].astype(F32) if has_init else part
            acc[...] = first.astype(acc.dtype)

        if nk > 1:
            @pl.when(k > 0)
            def _():
                acc[...] += part

        if use_acc:
            @pl.when(k == nk - 1)
            def _():
                o_ref[...] = acc[...].astype(o_ref.dtype)

    if mode == "tn":
        a_spec = pl.BlockSpec((tk, tm), lambda i, j, k: (k, i))
    else:
        a_spec = pl.BlockSpec((tm, tk), lambda i, j, k: (i, k))
    if mode == "nt":
        b_spec = pl.BlockSpec((tn, tk), lambda i, j, k: (j, k))
    else:
        b_spec = pl.BlockSpec((tk, tn), lambda i, j, k: (k, j))
    o_spec = pl.BlockSpec((tm, tn), lambda i, j, k: (i, j))
    in_specs = [a_spec, b_spec] + ([o_spec] if has_init else [])
    args = (a, b) + ((init,) if has_init else ())
    return _call(body, name, jax.ShapeDtypeStruct((M, N), out_dtype), grid=(M // tm, N // tn, nk),
                 in_specs=in_specs, out_specs=o_spec, scratch=[pltpu.VMEM((tm, tn), F32)] if use_acc else [],
                 sem=("parallel", "parallel", "arbitrary"))(*args)


def rowcall(fn, name, nrows, tm, rins, vins, routs, aouts, ainit=None):
    n_r, n_v, n_ro = len(rins), len(vins), len(routs)
    n_i = len(aouts) if ainit is not None else 0

    def body(*refs):
        r_in, v_in, i_in = refs[:n_r], refs[n_r:n_r + n_v], refs[n_r + n_v:n_r + n_v + n_i]
        r_out, a_out = refs[n_r + n_v + n_i:n_r + n_v + n_i + n_ro], refs[n_r + n_v + n_i + n_ro:]
        outs = fn(*[r[...].astype(F32) for r in r_in], *[v[...] for v in v_in])
        if not isinstance(outs, (tuple, list)):
            outs = (outs,)
        for ref, val in zip(r_out, outs[:n_ro]):
            ref[...] = val.astype(ref.dtype)
        if a_out:
            @pl.when(pl.program_id(0) == 0)
            def _():
                for k, ref in enumerate(a_out):
                    ref[...] = i_in[k][...] if n_i else jnp.zeros_like(ref)

            for ref, val in zip(a_out, outs[n_ro:]):
                ref[...] += val.astype(F32)

    def rspec(width, cblk, roff):
        return pl.BlockSpec((tm, width), lambda i: (i + roff, cblk))

    def whole(shape):
        nd = len(shape)
        return pl.BlockSpec(tuple(shape), lambda i: (0,) * nd)

    inits = list(ainit) if n_i else []
    in_specs = [rspec(w, cb, ro) for (_, w, cb, ro) in rins] + [whole(v.shape) for v in vins + inits]
    out_specs = [rspec(w, 0, 0) for (w, _) in routs] + [whole(s) for s in aouts]
    out_shape = [jax.ShapeDtypeStruct((nrows, w), dt) for (w, dt) in routs] + \
                [jax.ShapeDtypeStruct(tuple(s), F32) for s in aouts]
    res = _call(body, name, out_shape, grid=(nrows // tm,), in_specs=in_specs, out_specs=out_specs,
                sem=("arbitrary",))(*[r[0] for r in rins], *vins, *inits)
    return res


def _rms(x):
    return lax.rsqrt(jnp.mean(x * x, axis=-1, keepdims=True) + EPS)


def f_modulate(x, g, sc, sh):
    return (x * _rms(x)) * g * (1.0 + sc) + sh


def f_resid_mod(x, o, ga, g, sc, sh):
    x1 = x + ga * o
    return x1, f_modulate(x1, g, sc, sh)


def f_final_loss(x1, dn, tgt, ga2, gf):
    x2 = x1 + ga2 * dn
    y = (x2 * _rms(x2)) * gf
    err = jnp.square(y - tgt)
    return 0.5 * jnp.sum(jnp.mean(err, axis=-1))


def _sgu_spatial(vn, w, bt):
    lo = lax.broadcasted_iota(jnp.int32, (1, LANES), 1) < (SGU_WIDTH // SGU_GROUPS)
    row_blocks = []
    for r in range(vn.shape[0] // CHUNK):
        rows = vn[r * CHUNK:(r + 1) * CHUNK]
        cols = []
        for j in range(SGU_WIDTH // LANES):
            blk = rows[:, j * LANES:(j + 1) * LANES]
            v_lo = jnp.where(lo, blk, 0.0)
            v_hi = jnp.where(lo, 0.0, blk)
            s = mmul(w[2 * j], v_lo) + mmul(w[2 * j + 1], v_hi)
            bias = jnp.where(lo, bt[:, 2 * j:2 * j + 1], bt[:, 2 * j + 1:2 * j + 2])
            cols.append(s + bias)
        row_blocks.append(jnp.concatenate(cols, axis=1))
    return jnp.concatenate(row_blocks, axis=0) if len(row_blocks) > 1 else row_blocks[0]


def f_mixer(u_a, y0, y1, zu, zv, ga0, ga1, gb0, gb1, d_skip, w_glu, b_glu, ln_g, ln_b, sgu_w, sgu_bt,
            w_pa, w_pb, b_gate):
    ys = u_a * d_skip + y0 + y1
    ge = jax.nn.gelu(ys)
    y_a = ge * jax.nn.sigmoid(mmul(ge, w_glu) + b_glu)
    u_sg = jax.nn.gelu(zu)
    v = jax.nn.gelu(zv)
    vc = v - jnp.mean(v, axis=-1, keepdims=True)
    vn = (vc * lax.rsqrt(jnp.mean(vc * vc, axis=-1, keepdims=True) + EPS)) * ln_g + ln_b
    y_b = u_sg * _sgu_spatial(vn, sgu_w, sgu_bt)
    gl_a = jnp.concatenate([ga0, ga1], axis=1) + b_gate[:, :D_MODEL]
    gl_b = jnp.concatenate([gb0, gb1], axis=1) + b_gate[:, D_MODEL:]
    return jax.nn.sigmoid(gl_a) * mmul(y_a, w_pa) + jax.nn.sigmoid(gl_b) * mmul(y_b, w_pb)


def _cmul(ar, ai, xr, xi):
    return ar * xr - ai * xi, ar * xi + ai * xr


SUB = 8


def _scan_chunk(xr, xi, tab_ref, col, rev, conj, cr, ci):
    n_grp = xr.shape[0] // SUB
    tr = tab_ref[:, col:col + LANES]
    ti = tab_ref[:, col + 512:col + 512 + LANES]
    if conj:
        ti = -ti
    sub = lax.broadcasted_iota(jnp.int32, (1, SUB, 1), 1)
    xr = xr.reshape(n_grp, SUB, LANES)
    xi = xi.reshape(n_grp, SUB, LANES)
    s = 1
    while s < SUB:
        row = (SUB - s) if rev else (s - 1)
        shift = (SUB - s) if rev else s
        m = (sub < SUB - s) if rev else (sub >= s)
        pr, pi = _cmul(tr[row:row + 1], ti[row:row + 1], pltpu.roll(xr, shift, 1), pltpu.roll(xi, shift, 1))
        xr = xr + jnp.where(m, pr, 0.0)
        xi = xi + jnp.where(m, pi, 0.0)
        s *= 2
    edge = 0 if rev else SUB - 1
    out_r, out_i = [None] * n_grp, [None] * n_grp
    for k in (range(n_grp - 1, -1, -1) if rev else range(n_grp)):
        pr, pi = _cmul(tr, ti, jnp.broadcast_to(cr, (SUB, LANES)), jnp.broadcast_to(ci, (SUB, LANES)))
        out_r[k] = xr[k] + pr
        out_i[k] = xi[k] + pi
        cr = out_r[k][edge:edge + 1]
        ci = out_i[k][edge:edge + 1]
    return jnp.concatenate(out_r, axis=0), jnp.concatenate(out_i, axis=0), cr, ci


def s5_forward(u3, bm, cm, pf, pb, rev, n_chunks, blk0, name):
    T = SCAN_T

    def pos(i):
        return (n_chunks - 1 - i) if rev else i

    def body(u_ref, bm_ref, cm_ref, pf_ref, pb_ref, y_ref, cin_ref, carry):
        @pl.when(pl.program_id(0) == 0)
        def _():
            carry[...] = jnp.zeros_like(carry)

        cin_ref[...] = carry[...]
        tab = pb_ref if rev else pf_ref
        u = u_ref[...]
        for o in range(OCTETS):
            bu = jnp.dot(u[:, o * LANES:(o + 1) * LANES], bm_ref[o], preferred_element_type=F32)
            hr, hi = [], []
            for j in range(4):
                col = o * 1024 + j * LANES
                xr, xi, cr, ci = _scan_chunk(
                    bu[:, j * LANES:(j + 1) * LANES], bu[:, 512 + j * LANES:512 + (j + 1) * LANES], tab, col, rev, False,
                    cin_ref[0:1, col:col + LANES], cin_ref[0:1, col + 512:col + 512 + LANES])
                carry[0:1, col:col + LANES] = cr
                carry[0:1, col + 512:col + 512 + LANES] = ci
                hr.append(xr)
                hi.append(xi)
            h = jnp.concatenate(hr + hi, axis=1).astype(BF16)
            y_ref[:, o * LANES:(o + 1) * LANES] = jnp.dot(h, cm_ref[o], preferred_element_type=F32)

    whole3 = lambda s: pl.BlockSpec(s, lambda i: (0, 0, 0))
    whole2 = lambda s: pl.BlockSpec(s, lambda i: (0, 0))
    return _call(
        body, name,
        [jax.ShapeDtypeStruct((n_chunks * T, S5_WIDTH), F32), jax.ShapeDtypeStruct((n_chunks, 1, 2 * N_STATE), F32)],
        grid=(n_chunks,),
        in_specs=[pl.BlockSpec((T, S5_WIDTH), lambda i: (blk0 + pos(i), 0)), whole3(bm.shape), whole3(cm.shape),
                  whole2(pf.shape), whole2(pb.shape)],
        out_specs=[pl.BlockSpec((T, S5_WIDTH), lambda i: (pos(i), 0)),
                   pl.BlockSpec((None, 1, 2 * N_STATE), lambda i: (pos(i), 0, 0))],
        scratch=[pltpu.VMEM((1, 2 * N_STATE), F32)], sem=("arbitrary",))(u3, bm, cm, pf, pb)


def s5_backward(u3, dy3, cin, bm, cm, pf, pb, rev, n_chunks, blk0, name):
    T = SCAN_T

    def pos(i):
        return i if rev else (n_chunks - 1 - i)

    def body(u_ref, dy_ref, cin_ref, bm_ref, cm_ref, pf_ref, pb_ref, du_ref, dbm_ref, dcm_ref, da_ref, lcarry):
        @pl.when(pl.program_id(0) == 0)
        def _():
            lcarry[...] = jnp.zeros_like(lcarry)
            dbm_ref[...] = jnp.zeros_like(dbm_ref)
            dcm_ref[...] = jnp.zeros_like(dcm_ref)
            da_ref[...] = jnp.zeros_like(da_ref)

        rows = lax.broadcasted_iota(jnp.int32, (T, 1), 0)
        tab_h = pb_ref if rev else pf_ref
        tab_l = pf_ref if rev else pb_ref
        first = rows == (T - 1 if rev else 0)
        u = u_ref[...]
        dy = dy_ref[...]
        for o in range(OCTETS):
            u_o = u[:, o * LANES:(o + 1) * LANES]
            dy_o = dy[:, o * LANES:(o + 1) * LANES]
            bu = jnp.dot(u_o, bm_ref[o], preferred_element_type=F32)
            g = lax.dot_general(dy_o, cm_ref[o], (((1,), (1,)), ((), ())), preferred_element_type=F32)
            hr, hi, lr, li = [], [], [], []
            for j in range(4):
                col = o * 1024 + j * LANES
                sl_r = slice(j * LANES, (j + 1) * LANES)
                sl_i = slice(512 + j * LANES, 512 + (j + 1) * LANES)
                c_r = cin_ref[0:1, col:col + LANES]
                c_i = cin_ref[0:1, col + 512:col + 512 + LANES]
                xr, xi, _, _ = _scan_chunk(bu[:, sl_r], bu[:, sl_i], tab_h, col, rev, False, c_r, c_i)
                shift = (T - 1) if rev else 1
                pr = jnp.where(first, c_r, pltpu.roll(xr, shift, 0))
                pi = jnp.where(first, c_i, pltpu.roll(xi, shift, 0))
                ar_, ai_, l_r, l_i = _scan_chunk(g[:, sl_r], g[:, sl_i], tab_l, col, not rev, True,
                                                 lcarry[0:1, col:col + LANES], lcarry[0:1, col + 512:col + 512 + LANES])
                da_ref[0:1, col:col + LANES] += jnp.sum(ar_ * pr + ai_ * pi, axis=0, keepdims=True)
                da_ref[0:1, col + 512:col + 512 + LANES] += jnp.sum(ai_ * pr - ar_ * pi, axis=0, keepdims=True)
                lcarry[0:1, col:col + LANES] = l_r
                lcarry[0:1, col + 512:col + 512 + LANES] = l_i
                hr.append(xr)
                hi.append(xi)
                lr.append(ar_)
                li.append(ai_)
            h = jnp.concatenate(hr + hi, axis=1).astype(BF16)
            lam = jnp.concatenate(lr + li, axis=1).astype(BF16)
            du_ref[:, o * LANES:(o + 1) * LANES] = lax.dot_general(
                lam, bm_ref[o], (((1,), (1,)), ((), ())), preferred_element_type=F32)
            dbm_ref[o] += lax.dot_general(u_o, lam, (((0,), (0,)), ((), ())), preferred_element_type=F32)
            dcm_ref[o] += lax.dot_general(h, dy_o, (((0,), (0,)), ((), ())), preferred_element_type=F32)

    whole3 = lambda s: pl.BlockSpec(s, lambda i: (0, 0, 0))
    whole2 = lambda s: pl.BlockSpec(s, lambda i: (0, 0))
    row_spec = pl.BlockSpec((T, S5_WIDTH), lambda i: (blk0 + pos(i), 0))
    return _call(
        body, name,
        [jax.ShapeDtypeStruct((n_chunks * T, S5_WIDTH), F32), jax.ShapeDtypeStruct(bm.shape, F32),
         jax.ShapeDtypeStruct(cm.shape, F32), jax.ShapeDtypeStruct((1, 2 * N_STATE), F32)],
        grid=(n_chunks,),
        in_specs=[row_spec, row_spec, pl.BlockSpec((None, 1, 2 * N_STATE), lambda i: (pos(i), 0, 0)),
                  whole3(bm.shape), whole3(cm.shape), whole2(pf.shape), whole2(pb.shape)],
        out_specs=[pl.BlockSpec((T, S5_WIDTH), lambda i: (pos(i), 0)), whole3(bm.shape), whole3(cm.shape),
                   whole2((1, 2 * N_STATE))],
        scratch=[pltpu.VMEM((1, 2 * N_STATE), F32)], sem=("arbitrary",))(u3, dy3, cin, bm, cm, pf, pb)


def s5_tables(ar, ai, ls):
    def body(ar_ref, ai_ref, ls_ref, pr_ref, pi_ref):
        m = (lax.broadcasted_iota(jnp.int32, (SUB, 1), 0) + 1).astype(F32)
        dt = jnp.exp(ls_ref[...])
        mag = jnp.exp(m * (ar_ref[...] * dt))
        ang = m * (ai_ref[...] * dt)
        pr_ref[...] = mag * jnp.cos(ang)
        pi_ref[...] = mag * jnp.sin(ang)

    vec = pl.BlockSpec((None, 1, N_STATE), lambda d: (d, 0, 0))
    tab = pl.BlockSpec((None, SUB, N_STATE), lambda d: (d, 0, 0))
    return _call(body, "s5_tables", [jax.ShapeDtypeStruct((2, SUB, N_STATE), F32)] * 2, grid=(2,),
                 in_specs=[vec, vec, vec], out_specs=[tab, tab], sem=("arbitrary",))(ar, ai, ls)


def f_discretize(a_re, a_im, ls, b_re, b_im):
    dt = jnp.exp(ls)
    mag = jnp.exp(a_re * dt)
    ab_re = mag * jnp.cos(a_im * dt)
    ab_im = mag * jnp.sin(a_im * dt)
    p = ab_re - 1.0
    q = ab_im
    den = a_re * a_re + a_im * a_im
    k_re = ((p * a_re + q * a_im) / den)[None]
    k_im = ((q * a_re - p * a_im) / den)[None]
    return ab_re, ab_im, k_re * b_re - k_im * b_im, k_re * b_im + k_im * b_re


def _disc_specs():
    a = pl.BlockSpec((None, S5_GROUPS, S5_STATE), lambda d: (d, 0, 0))
    s = pl.BlockSpec((None, S5_GROUPS, 1), lambda d: (d, 0, 0))
    b = pl.BlockSpec((None, S5_GROUP, S5_GROUPS, S5_STATE), lambda d: (d, 0, 0, 0))
    return a, s, b


def s5_discretize(a_re, a_im, ls, b_re, b_im):
    def body(ar, ai, l, br, bi, obr, obi):
        _, _, r, i = f_discretize(ar[...], ai[...], l[...], br[...], bi[...])
        obr[...] = r
        obi[...] = i

    a, s, b = _disc_specs()
    return _call(body, "s5_discretize", [jax.ShapeDtypeStruct(b_re.shape, F32)] * 2, grid=(2,),
                 in_specs=[a, a, s, b, b], out_specs=[b, b], sem=("arbitrary",))(a_re, a_im, ls, b_re, b_im)


def s5_discretize_bwd(a_re, a_im, ls, b_re, b_im, dab_re, dab_im, dbb_re, dbb_im):
    def body(ar, ai, l, br, bi, c0, c1, c2, c3, o0, o1, o2, o3, o4):
        _, vjp = jax.vjp(f_discretize, ar[...], ai[...], l[...], br[...], bi[...])
        outs = vjp((c0[...], c1[...], c2[...], c3[...]))
        for ref, val in zip((o0, o1, o2, o3, o4), outs):
            ref[...] = val

    a, s, b = _disc_specs()
    shapes = [jax.ShapeDtypeStruct(t.shape, F32) for t in (a_re, a_im, ls, b_re, b_im)]
    return _call(body, "s5_discretize_bwd", shapes, grid=(2,), in_specs=[a, a, s, b, b, a, a, b, b],
                 out_specs=[a, a, s, b, b], sem=("arbitrary",))(a_re, a_im, ls, b_re, b_im, dab_re, dab_im,
                                                                 dbb_re, dbb_im)


def _conv_taps(s_ref, base, n_rows):
    n = n_rows + 2 * CONV_PAD
    ext = s_ref[pl.ds(base, n), :]
    col = (lax.broadcasted_iota(jnp.int32, (n, 1), 0) + (2 * GRID_W - CONV_PAD)) % GRID_W
    left = jnp.where(col == 0, 0.0, pltpu.roll(ext, 1, 0))
    right = jnp.where(col == GRID_W - 1, 0.0, pltpu.roll(ext, n - 1, 0))
    return left, ext, right


def _conv_apply(taps, w_ref, n_rows, flip):
    out = None
    for i in range(3):
        wi = 2 - i if flip else i
        comb = None
        for j in range(3):
            wj = 2 - j if flip else j
            term = w_ref[wi * 3 + wj:wi * 3 + wj + 1, :] * taps[j]
            comb = term if comb is None else comb + term
        start = CONV_PAD + (i - 1) * GRID_W
        part = comb[start:start + n_rows]
        out = part if out is None else out + part
    return out


def _conv_fill(dst_ref, src_ref, n_tok):
    zeros = jnp.zeros((CONV_PAD, LANES), F32)
    dst_ref[0:CONV_PAD, :] = zeros
    dst_ref[CONV_PAD + n_tok:2 * CONV_PAD + n_tok, :] = zeros

    def step(r, carry):
        base = pl.multiple_of(r * CONV_ROWS, CONV_ROWS)
        dst_ref[pl.ds(base + CONV_PAD, CONV_ROWS), :] = src_ref[pl.ds(base, CONV_ROWS), :].astype(F32)
        return carry

    lax.fori_loop(0, n_tok // CONV_ROWS, step, 0)


def conv_forward(up, wg, wv, bias):
    n_tok = up.shape[0]
    nb = FFN_HIDDEN // LANES

    def body(ug_ref, uv_ref, wg_ref, wv_ref, bg_ref, bv_ref, act_ref, sg, sv):
        _conv_fill(sg, ug_ref, n_tok)
        _conv_fill(sv, uv_ref, n_tok)

        def step(r, carry):
            base = pl.multiple_of(r * CONV_ROWS, CONV_ROWS)
            gate = _conv_apply(_conv_taps(sg, base, CONV_ROWS), wg_ref, CONV_ROWS, False) + bg_ref[...]
            val = _conv_apply(_conv_taps(sv, base, CONV_ROWS), wv_ref, CONV_ROWS, False) + bv_ref[...]
            act_ref[pl.ds(base, CONV_ROWS), :] = (gate * jax.nn.sigmoid(gate) * val).astype(BF16)
            return carry

        lax.fori_loop(0, n_tok // CONV_ROWS, step, 0)

    col = lambda off: pl.BlockSpec((n_tok, LANES), lambda k: (0, k + off))
    wsp = lambda off: pl.BlockSpec((16, LANES), lambda k: (0, k + off))
    bsp = lambda off: pl.BlockSpec((1, LANES), lambda k: (0, k + off))
    pad = pltpu.VMEM((n_tok + 2 * CONV_PAD, LANES), F32)
    return _call(body, "conv_forward", jax.ShapeDtypeStruct((n_tok, FFN_HIDDEN), BF16), grid=(nb,),
                 in_specs=[col(0), col(nb), wsp(0), wsp(0), bsp(0), bsp(nb)], out_specs=col(0),
                 scratch=[pad, pad], sem=("arbitrary",))(up, up, wg, wv, bias, bias)


def conv_backward(up, dact, wg, wv, bias):
    n_tok = up.shape[0]
    nb = FFN_HIDDEN // LANES
    n_steps = n_tok // CONV_ROWS

    def body(ug_ref, uv_ref, da_ref, wg_ref, wv_ref, bg_ref, bv_ref, dug_ref, duv_ref, dwg_ref, dwv_ref,
             sg, sv, sdg, sdv):
        _conv_fill(sg, ug_ref, n_tok)
        _conv_fill(sv, uv_ref, n_tok)
        zeros = jnp.zeros((CONV_PAD, LANES), F32)
        for s_ref in (sdg, sdv):
            s_ref[0:CONV_PAD, :] = zeros
            s_ref[CONV_PAD + n_tok:2 * CONV_PAD + n_tok, :] = zeros
        dwg_ref[...] = jnp.zeros_like(dwg_ref)
        dwv_ref[...] = jnp.zeros_like(dwv_ref)

        def grads(r, carry):
            base = pl.multiple_of(r * CONV_ROWS, CONV_ROWS)
            taps_g = _conv_taps(sg, base, CONV_ROWS)
            taps_v = _conv_taps(sv, base, CONV_ROWS)
            gate = _conv_apply(taps_g, wg_ref, CONV_ROWS, False) + bg_ref[...]
            val = _conv_apply(taps_v, wv_ref, CONV_ROWS, False) + bv_ref[...]
            d_act = da_ref[pl.ds(base, CONV_ROWS), :].astype(F32)
            sig = jax.nn.sigmoid(gate)
            d_gate = d_act * val * (sig * (1.0 + gate * (1.0 - sig)))
            d_val = d_act * (gate * sig)
            sdg[pl.ds(base + CONV_PAD, CONV_ROWS), :] = d_gate
            sdv[pl.ds(base + CONV_PAD, CONV_ROWS), :] = d_val
            for d_out, taps, dw_ref in ((d_gate, taps_g, dwg_ref), (d_val, taps_v, dwv_ref)):
                for i in range(3):
                    start = CONV_PAD + (i - 1) * GRID_W
                    for j in range(3):
                        k = i * 3 + j
                        dw_ref[k:k + 1, :] += jnp.sum(d_out * taps[j][start:start + CONV_ROWS], axis=0, keepdims=True)
                dw_ref[9:10, :] += jnp.sum(d_out, axis=0, keepdims=True)
            return carry

        lax.fori_loop(0, n_steps, grads, 0)

        def spread(r, carry):
            base = pl.multiple_of(r * CONV_ROWS, CONV_ROWS)
            dug_ref[pl.ds(base, CONV_ROWS), :] = _conv_apply(
                _conv_taps(sdg, base, CONV_ROWS), wg_ref, CONV_ROWS, True).astype(BF16)
            duv_ref[pl.ds(base, CONV_ROWS), :] = _conv_apply(
                _conv_taps(sdv, base, CONV_ROWS), wv_ref, CONV_ROWS, True).astype(BF16)
            return carry

        lax.fori_loop(0, n_steps, spread, 0)

    col = lambda off: pl.BlockSpec((n_tok, LANES), lambda k: (0, k + off))
    wsp = pl.BlockSpec((16, LANES), lambda k: (0, k))
    bsp = lambda off: pl.BlockSpec((1, LANES), lambda k: (0, k + off))
    pad = pltpu.VMEM((n_tok + 2 * CONV_PAD, LANES), F32)
    half = jax.ShapeDtypeStruct((n_tok, FFN_HIDDEN), BF16)
    dw = jax.ShapeDtypeStruct((16, FFN_HIDDEN), F32)
    return _call(body, "conv_backward", [half, half, dw, dw], grid=(nb,),
                 in_specs=[col(0), col(nb), col(0), wsp, wsp, bsp(0), bsp(nb)],
                 out_specs=[col(0), col(0), wsp, wsp], scratch=[pad, pad, pad, pad],
                 sem=("arbitrary",))(up, up, dact, wg, wv, bias, bias)


def f_adamw(w, g, m, v):
    m = ADAM_B1 * m + (1.0 - ADAM_B1) * g
    v = ADAM_B2 * v + (1.0 - ADAM_B2) * jnp.square(g)
    m_hat = m / (1.0 - ADAM_B1 ** ADAM_STEP)
    v_hat = v / (1.0 - ADAM_B2 ** ADAM_STEP)
    delta = -ADAM_LR * (m_hat / (jnp.sqrt(v_hat) + ADAM_EPS) + ADAM_WD * w)
    return delta, m, v


def adamw(w, g, m, v, name):
    shape = w.shape
    cols = shape[-1]
    rows = w.size // cols
    two_d = [t.reshape(rows, cols) for t in (w, g, m, v)]
    tm = _tile(rows, 256, 8) if rows % 8 == 0 else rows
    outs = rowcall(f_adamw, name, rows, tm, [(t, cols, 0, 0) for t in two_d], [], [(cols, F32)] * 3, [])
    return tuple(o.reshape(shape) for o in outs)


def _place():
    return lax.axis_index("x"), lax.axis_index("y"), lax.axis_index("c")


_ANY = pl.BlockSpec(memory_space=pl.ANY)


def allgather_devices(v, name):
    def body(v_ref, out_ref, send_sems, recv_sems, local_sem):
        x, y, c = _place()
        me, sibling = (x, y, c), (x, y, 1 - c)
        chips = [(1 - x, y), (x, 1 - y), (1 - x, 1 - y)]

        def slot(p):
            return out_ref.at[4 * p[0] + 2 * p[1] + p[2]]

        def copy(k, block, to, src=None):
            return pltpu.make_async_remote_copy(
                src_ref=slot(block) if src is None else src, dst_ref=slot(block),
                send_sem=send_sems.at[k], recv_sem=recv_sems.at[k], device_id=to, device_id_type=MESH)

        mine = pltpu.make_async_copy(v_ref, slot(me), local_sem)
        mine.start()
        first = [copy(0, me, sibling, src=v_ref)]
        first += [copy(1 + j, me, (*chip, c), src=v_ref) for j, chip in enumerate(chips)]
        for cp in first:
            cp.start()
        passed = [copy(4 + j, (*chip, c), sibling) for j, chip in enumerate(chips)]
        for j, chip in enumerate(chips):
            copy(1 + j, (*chip, c), me).wait_recv()
            passed[j].start()
        copy(0, sibling, me).wait_recv()
        for j, chip in enumerate(chips):
            copy(4 + j, (*chip, 1 - c), me).wait_recv()
        for cp in first + passed:
            cp.wait_send()
        mine.wait()

    return _call(body, name, jax.ShapeDtypeStruct((N_DEV,) + v.shape, v.dtype), in_specs=[_ANY], out_specs=_ANY,
                 scratch=[pltpu.SemaphoreType.DMA((7,)), pltpu.SemaphoreType.DMA((7,)), pltpu.SemaphoreType.DMA])(v)


def allgather_chips(v, name):
    half = v.shape[0] // 2

    def body(v_ref, out_ref, send_sems, recv_sems, local_sem):
        x, y, c = _place()
        sibling = (x, y, 1 - c)
        chips = [(1 - x, y), (x, 1 - y), (1 - x, 1 - y)]

        def rows(chip, h):
            return out_ref.at[2 * chip[0] + chip[1], pl.ds(h * half, half)]

        def copy(k, chip, h, to, src=None):
            return pltpu.make_async_remote_copy(
                src_ref=rows(chip, h) if src is None else src, dst_ref=rows(chip, h),
                send_sem=send_sems.at[k], recv_sem=recv_sems.at[k], device_id=to, device_id_type=MESH)

        mine = pltpu.make_async_copy(v_ref, out_ref.at[2 * x + y], local_sem)
        mine.start()
        first = [copy(j, (x, y), c, (*chip, c), src=v_ref.at[pl.ds(c * half, half)]) for j, chip in enumerate(chips)]
        for cp in first:
            cp.start()
        passed = [copy(3 + j, chip, c, sibling) for j, chip in enumerate(chips)]
        for j, chip in enumerate(chips):
            copy(j, chip, c, (x, y, c)).wait_recv()
            passed[j].start()
        for j, chip in enumerate(chips):
            copy(3 + j, chip, 1 - c, (x, y, c)).wait_recv()
        for cp in first + passed:
            cp.wait_send()
        mine.wait()

    return _call(body, name, jax.ShapeDtypeStruct((N_CHIPS,) + v.shape, v.dtype), in_specs=[_ANY], out_specs=_ANY,
                 scratch=[pltpu.SemaphoreType.DMA((6,)), pltpu.SemaphoreType.DMA((6,)), pltpu.SemaphoreType.DMA])(v)


def swap_sibling(v, name):
    def body(v_ref, out_ref, send_sem, recv_sem):
        x, y, c = _place()
        cp = pltpu.make_async_remote_copy(src_ref=v_ref, dst_ref=out_ref, send_sem=send_sem, recv_sem=recv_sem,
                                          device_id=(x, y, 1 - c), device_id_type=MESH)
        cp.start()
        cp.wait()

    return _call(body, name, jax.ShapeDtypeStruct(v.shape, v.dtype), in_specs=[_ANY], out_specs=_ANY,
                 scratch=[pltpu.SemaphoreType.DMA, pltpu.SemaphoreType.DMA])(v)


def exchange_chips(v, name):
    def body(v_ref, out_ref, send_sems, recv_sems, local_sem):
        x, y, c = _place()
        chips = [(1 - x, y), (x, 1 - y), (1 - x, 1 - y)]
        mine = pltpu.make_async_copy(v_ref.at[2 * x + y], out_ref.at[2 * x + y], local_sem)
        mine.start()
        sends = [pltpu.make_async_remote_copy(
            src_ref=v_ref.at[2 * chip[0] + chip[1]], dst_ref=out_ref.at[2 * x + y], send_sem=send_sems.at[j],
            recv_sem=recv_sems.at[j], device_id=(*chip, c), device_id_type=MESH) for j, chip in enumerate(chips)]
        for cp in sends:
            cp.start()
        for j, chip in enumerate(chips):
            pltpu.make_async_remote_copy(
                src_ref=v_ref.at[2 * x + y], dst_ref=out_ref.at[2 * chip[0] + chip[1]], send_sem=send_sems.at[j],
                recv_sem=recv_sems.at[j], device_id=(*chip, c), device_id_type=MESH).wait_recv()
        for cp in sends:
            cp.wait_send()
        mine.wait()

    return _call(body, name, jax.ShapeDtypeStruct(v.shape, v.dtype), in_specs=[_ANY], out_specs=_ANY,
                 scratch=[pltpu.SemaphoreType.DMA((3,)), pltpu.SemaphoreType.DMA((3,)), pltpu.SemaphoreType.DMA])(v)


ROW_TILE = 256

BIG_SHARDS = (("w_in", (D_MODEL, 896), 1), ("s5_w_glu", (128, S5_WIDTH), 0), ("w_proj_a", (S5_WIDTH, 256), 1),
              ("w_proj_b", (SGU_WIDTH, 256), 1), ("w_out", (256, D_MODEL), 0), ("w_up", (D_MODEL, 1408), 1),
              ("w_down", (704, D_MODEL), 0))
SMALL_PARAMS = ("g_mix", "s5_a_re", "s5_a_im", "s5_log_step", "s5_b_re", "s5_b_im", "s5_c_re", "s5_c_im", "s5_d",
                "s5_b_glu", "sgu_ln_g", "sgu_ln_b", "sgu_w", "sgu_b", "b_gate", "g_ffn", "conv_b", "g_final")
PACK_COLS = 1024


def _rows_of(n):
    return -(-n // PACK_COLS)


def _pack_rows(arrays, total_rows, dtype):
    parts = []
    used = 0
    for a in arrays:
        r = _rows_of(a.size)
        parts.append(jnp.pad(a.reshape(-1).astype(dtype), (0, r * PACK_COLS - a.size)).reshape(r, PACK_COLS))
        used += r
    if total_rows > used:
        parts.append(jnp.zeros((total_rows - used, PACK_COLS), dtype))
    return jnp.concatenate(parts, axis=0)


def _unpack_rows(packed, shapes, row0=0):
    out = []
    for s in shapes:
        n = 1
        for d in s:
            n *= d
        r = _rows_of(n)
        out.append(packed[row0:row0 + r].reshape(-1)[:n].reshape(s))
        row0 += r
    return out


def _octet_major(re, im):
    parts = []
    for o in range(OCTETS):
        parts += [re[:, o * 512:(o + 1) * 512], im[:, o * 512:(o + 1) * 512]]
    return jnp.concatenate(parts, axis=1)


def _octet_split(v):
    v = v.reshape(OCTETS, 2, 512)
    return v[:, 0].reshape(N_STATE), v[:, 1].reshape(N_STATE)


def _s5_bmat(bb):
    t = bb.reshape(S5_GROUP, OCTETS, 8, 1, S5_STATE) * jnp.eye(8, dtype=F32)[None, None, :, :, None]
    return jnp.transpose(t, (1, 2, 0, 3, 4)).reshape(OCTETS, LANES, 512)


def _s5_bmat_t(dm):
    t = dm.reshape(OCTETS, 8, S5_GROUP, 8, S5_STATE) * jnp.eye(8, dtype=F32)[None, :, None, :, None]
    return jnp.transpose(t.sum(axis=3), (2, 0, 1, 3)).reshape(S5_GROUP, S5_GROUPS, S5_STATE)


def _s5_cmat(c):
    t = c.reshape(OCTETS, 8, 1, S5_GROUP, S5_STATE) * jnp.eye(8, dtype=F32)[None, :, :, None, None]
    return jnp.transpose(t, (0, 2, 4, 1, 3)).reshape(OCTETS, 512, LANES)


def _s5_cmat_t(dm):
    t = dm.reshape(OCTETS, 8, S5_STATE, 8, S5_GROUP) * jnp.eye(8, dtype=F32)[None, :, None, :, None]
    return jnp.transpose(t.sum(axis=1), (0, 2, 3, 1)).reshape(S5_GROUPS, S5_GROUP, S5_STATE)


def local_step(x, ctx, tgt, mod, modc, W):
    n_tok, n_ctx = x.shape[0], ctx.shape[0]
    tm = ROW_TILE
    D = D_MODEL
    sh1, sc1, ga1, sh2, sc2, ga2 = [mod[:, k * D:(k + 1) * D] for k in range(N_MOD)]
    sh1c, sc1c = modc[:, :D], modc[:, D:2 * D]
    g_mix, g_ffn, g_final = W["g_mix"], W["g_ffn"], W["g_final"]
    w_in = W["w_in"]
    w_in_u = w_in[:, :S5_WIDTH]

    h = rowcall(f_modulate, "mod1", n_tok, tm, [(x, D, 0, 0)], [g_mix, sc1, sh1], [(D, BF16)], [])[0]
    hc = rowcall(f_modulate, "mod1_ctx", n_ctx, tm, [(ctx, D, 0, 0)], [g_mix, sc1c, sh1c], [(D, BF16)], [])[0]
    proj = matmul(h, w_in, "nn", BF16, "proj_in")
    uc = matmul(hc, w_in_u, "nn", BF16, "proj_in_ctx")
    u3 = jnp.concatenate([uc, proj[:, :S5_WIDTH], uc], axis=0)

    a_re, a_im, ls = W["s5_a_re"], W["s5_a_im"], W["s5_log_step"][..., None]
    b_re_t = jnp.transpose(W["s5_b_re"], (0, 3, 1, 2))
    b_im_t = jnp.transpose(W["s5_b_im"], (0, 3, 1, 2))
    bb_re, bb_im = s5_discretize(a_re, a_im, ls, b_re_t, b_im_t)
    ls_rep = jnp.repeat(W["s5_log_step"], S5_STATE, axis=1).reshape(2, 1, N_STATE)
    p_re, p_im = s5_tables(a_re.reshape(2, 1, N_STATE), a_im.reshape(2, 1, N_STATE), ls_rep)
    n_chunks = (n_ctx + n_tok) // SCAN_T
    ctx_blk = n_ctx // SCAN_T
    pf, pb, bm, cm = [], [], [], []
    for d in range(2):
        pf.append(_octet_major(p_re[d], p_im[d]))
        pb.append(pf[d][::-1])
        bm.append(jnp.concatenate([_s5_bmat(bb_re[d]), _s5_bmat(bb_im[d])], axis=2).astype(BF16))
        cm.append(jnp.concatenate([_s5_cmat(W["s5_c_re"][d]), -_s5_cmat(W["s5_c_im"][d])], axis=1).astype(BF16))
    y0, cin0 = s5_forward(u3, bm[0], cm[0], pf[0], pb[0], False, n_chunks, 0, "s5_fwd0")
    y1, cin1 = s5_forward(u3, bm[1], cm[1], pf[1], pb[1], True, n_chunks, ctx_blk, "s5_fwd1")

    mix_rows = [(proj, 512, 0, 0), (y0, 512, 0, n_ctx // tm), (y1, 512, 0, 0)] + \
               [(proj, 512, k, 0) for k in range(1, 7)]
    mix_vecs = [W["s5_d"], W["s5_w_glu"], W["s5_b_glu"], W["sgu_ln_g"], W["sgu_ln_b"], W["sgu_w"],
                jnp.transpose(W["sgu_b"]), W["w_proj_a"], W["w_proj_b"], W["b_gate"]]
    mrg = rowcall(f_mixer, "mixer", n_tok, tm, mix_rows, mix_vecs, [(D, BF16)], [])[0]
    o = matmul(mrg, W["w_out"], "nn", F32, "proj_out")
    x1, h2 = rowcall(f_resid_mod, "resid_mod2", n_tok, tm, [(x, D, 0, 0), (o, D, 0, 0)], [ga1, g_ffn, sc2, sh2],
                     [(D, F32), (D, BF16)], [])
    up = matmul(h2, W["w_up"], "nn", BF16, "ffn_up")
    conv_w = W["conv_w"].reshape(9, 2 * FFN_HIDDEN)
    wg = jnp.pad(conv_w[:, :FFN_HIDDEN], ((0, 7), (0, 0)))
    wv = jnp.pad(conv_w[:, FFN_HIDDEN:], ((0, 7), (0, 0)))
    act = conv_forward(up, wg, wv, W["conv_b"])
    dn = matmul(act, W["w_down"], "nn", F32, "ffn_down")

    def final_fn(x1_, dn_, tgt_, ga2_, gf_):
        loss, (dx1_, ddn_, dga2_, dgf_) = jax.value_and_grad(f_final_loss, argnums=(0, 1, 3, 4))(
            x1_, dn_, tgt_, ga2_, gf_)
        return dx1_, ddn_, loss.reshape(1, 1), dga2_, dgf_

    dx2, ddn, loss, d_ga2, d_gfinal = rowcall(
        final_fn, "final_loss", n_tok, tm, [(x1, D, 0, 0), (dn, D, 0, 0), (tgt, D, 0, 0)], [ga2, g_final],
        [(D, F32), (D, BF16)], [(1, 1), (1, D), (1, D)])

    dact = matmul(ddn, W["w_down"], "nt", BF16, "ffn_down_dx")
    d_w_down = matmul(act, ddn, "tn", F32, "ffn_down_dw")
    dup_g, dup_v, dwg, dwv = conv_backward(up, dact, wg, wv, W["conv_b"])
    dup = jnp.concatenate([dup_g, dup_v], axis=1)
    d_conv_w = jnp.concatenate([dwg[:9], dwv[:9]], axis=1).reshape(3, 3, 2 * FFN_HIDDEN)
    d_conv_b = jnp.concatenate([dwg[9:10], dwv[9:10]], axis=1)
    dh2 = matmul(dup, W["w_up"], "nt", BF16, "ffn_up_dx")
    d_w_up = matmul(h2, dup, "tn", F32, "ffn_up_dw")

    def resid_bwd(x_, o_, dx1_, dh2_, ga_, g_, sc_, sh_):
        _, vjp = jax.vjp(f_resid_mod, x_, o_, ga_, g_, sc_, sh_)
        return vjp((dx1_, dh2_))

    dxa, do, d_ga1, d_gffn, d_sc2, d_sh2 = rowcall(
        resid_bwd, "resid_mod2_bwd", n_tok, tm, [(x, D, 0, 0), (o, D, 0, 0), (dx2, D, 0, 0), (dh2, D, 0, 0)],
        [ga1, g_ffn, sc2, sh2], [(D, F32), (D, BF16)], [(1, D)] * 4)

    dmrg = matmul(do, W["w_out"], "nt", BF16, "proj_out_dx")
    d_w_out = matmul(mrg, do, "tn", F32, "proj_out_dw")

    def mixer_bwd(*args):
        rows, dm, vecs = args[:9], args[9], [v.astype(F32) for v in args[10:]]
        _, vjp = jax.vjp(f_mixer, *rows, *vecs)
        g = vjp(dm)
        return (g[0], g[1], jnp.concatenate([g[3], g[4]], axis=1), jnp.concatenate(g[5:9], axis=1)) + tuple(g[9:])

    mb = rowcall(mixer_bwd, "mixer_bwd", n_tok, tm, mix_rows + [(dmrg, D, 0, 0)], mix_vecs,
                 [(512, BF16), (512, BF16), (1024, BF16), (2048, BF16)], [v.shape for v in mix_vecs])
    du_direct, dys, dzb, dgl = mb[:4]
    d_s5d, d_w_glu, d_b_glu, d_ln_g, d_ln_b, d_sgu_w, d_sgu_bt, d_w_pa, d_w_pb, d_b_gate = mb[4:]

    zc = jnp.zeros((n_ctx, S5_WIDTH), BF16)
    dy3 = jnp.concatenate([zc, dys, zc], axis=0)
    du0, dbm0, dcm0, da0 = s5_backward(u3, dy3, cin0, bm[0], cm[0], pf[0], pb[0], False, n_chunks, 0, "s5_bwd0")
    du1, dbm1, dcm1, da1 = s5_backward(u3, dy3, cin1, bm[1], cm[1], pf[1], pb[1], True, n_chunks, ctx_blk, "s5_bwd1")
    add3 = lambda a, b, c: a + b + c
    du_a = rowcall(add3, "du_sum", n_tok, tm, [(du_direct, 512, 0, 0), (du0, 512, 0, n_ctx // tm), (du1, 512, 0, 0)],
                   [], [(512, BF16)], [])[0]
    du_c = rowcall(lambda a, b: a + b, "du_sum_ctx", n_ctx, tm, [(du0, 512, 0, 0), (du1, 512, 0, n_tok // tm)],
                   [], [(512, BF16)], [])[0]

    dab_re, dab_im, dbb_re, dbb_im, d_c_re, d_c_im = [], [], [], [], [], []
    for dbm, dcm, da in ((dbm0, dcm0, da0), (dbm1, dcm1, da1)):
        r, i = _octet_split(da)
        dab_re.append(r.reshape(S5_GROUPS, S5_STATE))
        dab_im.append(i.reshape(S5_GROUPS, S5_STATE))
        dbb_re.append(_s5_bmat_t(dbm[:, :, :512]))
        dbb_im.append(_s5_bmat_t(dbm[:, :, 512:]))
        d_c_re.append(_s5_cmat_t(dcm[:, :512]))
        d_c_im.append(-_s5_cmat_t(dcm[:, 512:]))
    d_a_re, d_a_im, d_ls, d_b_re_t, d_b_im_t = s5_discretize_bwd(
        a_re, a_im, ls, b_re_t, b_im_t, jnp.stack(dab_re), jnp.stack(dab_im), jnp.stack(dbb_re), jnp.stack(dbb_im))

    dproj = jnp.concatenate([du_a, dzb, dgl], axis=1)
    dh = matmul(dproj, w_in, "nt", BF16, "proj_in_dx")
    dhc = matmul(du_c, w_in_u, "nt", BF16, "proj_in_ctx_dx")
    d_w_in_c = matmul(hc, du_c, "tn", F32, "proj_in_ctx_dw")
    d_w_in = matmul(h, dproj, "tn", F32, "proj_in_dw",
                    init=jnp.pad(d_w_in_c, ((0, 0), (0, w_in.shape[1] - S5_WIDTH))))

    def mod_bwd_ctx(x_, dh_, g_, sc_, sh_):
        _, vjp = jax.vjp(f_modulate, x_, g_, sc_, sh_)
        return vjp(dh_)[1:]

    d_gmix_c, d_sc1c, d_sh1c = rowcall(mod_bwd_ctx, "mod1_ctx_bwd", n_ctx, tm, [(ctx, D, 0, 0), (dhc, D, 0, 0)],
                                       [g_mix, sc1c, sh1c], [], [(1, D)] * 3)

    def mod_bwd(x_, dh_, dxa_, g_, sc_, sh_):
        _, vjp = jax.vjp(f_modulate, x_, g_, sc_, sh_)
        dx_, dg_, dsc_, dsh_ = vjp(dh_)
        return dx_ + dxa_, dg_, dsc_, dsh_

    zero_d = jnp.zeros((1, D), F32)
    grad_x, d_gmix, d_sc1, d_sh1 = rowcall(
        mod_bwd, "mod1_bwd", n_tok, tm, [(x, D, 0, 0), (dh, D, 0, 0), (dxa, D, 0, 0)], [g_mix, sc1, sh1],
        [(D, F32)], [(1, D)] * 3, ainit=[d_gmix_c, zero_d, zero_d])

    grads = {
        "dmod": jnp.concatenate([d_sh1, d_sc1, d_ga1, d_sh2, d_sc2, d_ga2], axis=1),
        "dmodc": jnp.concatenate([d_sh1c, d_sc1c], axis=1),
        "g_mix": d_gmix,
        "s5_a_re": d_a_re, "s5_a_im": d_a_im, "s5_log_step": d_ls[..., 0],
        "s5_b_re": jnp.transpose(d_b_re_t, (0, 2, 3, 1)), "s5_b_im": jnp.transpose(d_b_im_t, (0, 2, 3, 1)),
        "s5_c_re": jnp.stack(d_c_re), "s5_c_im": jnp.stack(d_c_im), "s5_d": d_s5d, "s5_b_glu": d_b_glu,
        "sgu_ln_g": d_ln_g, "sgu_ln_b": d_ln_b, "sgu_w": d_sgu_w, "sgu_b": jnp.transpose(d_sgu_bt),
        "b_gate": d_b_gate, "g_ffn": d_gffn, "conv_b": d_conv_b, "g_final": d_gfinal, "conv_w": d_conv_w,
        "w_in": d_w_in, "s5_w_glu": d_w_glu, "w_proj_a": d_w_pa, "w_proj_b": d_w_pb, "w_out": d_w_out,
        "w_up": d_w_up, "w_down": d_w_down,
    }
    return loss, grad_x, grads


ADA_COLS = N_MOD * D_MODEL // N_CHIPS
MOD_ROWS = 16


def mod_forward(c16, w, b):
    n = w.shape[1]
    tn = 512

    def body(c_ref, w_ref, b_ref, o_ref):
        cv = c_ref[...]
        cs = cv * jax.nn.sigmoid(cv)
        o_ref[...] = jnp.dot(cs.astype(BF16), w_ref[...].astype(BF16), preferred_element_type=F32) + b_ref[...]

    return _call(body, "mod_forward", jax.ShapeDtypeStruct((MOD_ROWS, n), F32), grid=(n // tn,),
                 in_specs=[pl.BlockSpec((MOD_ROWS, D_MODEL), lambda j: (0, 0)),
                           pl.BlockSpec((D_MODEL, tn), lambda j: (0, j)), pl.BlockSpec((1, tn), lambda j: (0, j))],
                 out_specs=pl.BlockSpec((MOD_ROWS, tn), lambda j: (0, j)), sem=("arbitrary",))(c16, w, b)


def f_ada_outer(ct, dm):
    cs = ct * jax.nn.sigmoid(ct)
    acc = cs[:, 0:1] * dm[0:1]
    for k in range(1, 9):
        acc = acc + cs[:, k:k + 1] * dm[k:k + 1]
    return acc


def f_cctx_grad(z, p4):
    s = jax.nn.sigmoid(z)
    return (p4[0:1] + p4[1:2] + p4[2:3] + p4[3:4]) * (s + z * s * (1.0 - s))


WEIGHT_NAMES = ("c_ctx", "w_ada", "b_ada", "g_mix", "w_in", "s5_a_re", "s5_a_im", "s5_log_step", "s5_b_re",
                "s5_b_im", "s5_c_re", "s5_c_im", "s5_d", "s5_w_glu", "s5_b_glu", "sgu_ln_g", "sgu_ln_b", "sgu_w",
                "sgu_b", "w_proj_a", "w_proj_b", "b_gate", "w_out", "g_ffn", "w_up", "conv_w", "conv_b", "w_down",
                "g_final")
CONV_SHARD = 2 * FFN_HIDDEN // N_CHIPS
WEIGHT_PACK_ROWS = 3616
GRAD_PACK_ROWS = 3584
SMALL_PACK_ROWS = 512
SMALL_ROW0 = 58
ADAM_PACK_ROWS = 512


def kernel(x, c, ctx, c_ctx, w_ada, b_ada, g_mix, w_in, s5_a_re, s5_a_im, s5_log_step, s5_b_re, s5_b_im, s5_c_re, s5_c_im, s5_d, s5_w_glu, s5_b_glu, sgu_ln_g, sgu_ln_b, sgu_w, sgu_b, w_proj_a, w_proj_b, b_gate, w_out, g_ffn, w_up, conv_w, conv_b, w_down, g_final, loss_target, m_c_ctx, m_w_ada, m_b_ada, m_g_mix, m_w_in, m_s5_a_re, m_s5_a_im, m_s5_log_step, m_s5_b_re, m_s5_b_im, m_s5_c_re, m_s5_c_im, m_s5_d, m_s5_w_glu, m_s5_b_glu, m_sgu_ln_g, m_sgu_ln_b, m_sgu_w, m_sgu_b, m_w_proj_a, m_w_proj_b, m_b_gate, m_w_out, m_g_ffn, m_w_up, m_conv_w, m_conv_b, m_w_down, m_g_final, v_c_ctx, v_w_ada, v_b_ada, v_g_mix, v_w_in, v_s5_a_re, v_s5_a_im, v_s5_log_step, v_s5_b_re, v_s5_b_im, v_s5_c_re, v_s5_c_im, v_s5_d, v_s5_w_glu, v_s5_b_glu, v_sgu_ln_g, v_sgu_ln_b, v_sgu_w, v_sgu_b, v_w_proj_a, v_w_proj_b, v_b_gate, v_w_out, v_g_ffn, v_w_up, v_conv_w, v_conv_b, v_w_down, v_g_final):
    given = dict(locals())
    wts = {n: given[n] for n in WEIGHT_NAMES}
    ms = {n: given["m_" + n] for n in WEIGHT_NAMES}
    vs = {n: given["v_" + n] for n in WEIGHT_NAMES}
    xi, yi, ci = _place()
    chip = 2 * xi + yi
    dev = 2 * chip + ci
    D = D_MODEL

    c8 = allgather_devices(jnp.pad(c, ((0, 7), (0, 0))), "gather_c")[:, 0, :]
    c16 = jnp.concatenate([c8, c_ctx[None], jnp.zeros((MOD_ROWS - 9, D), F32)], axis=0)
    b_shard = lax.dynamic_slice(b_ada, (0, chip * ADA_COLS), (1, ADA_COLS))
    mod_shard = mod_forward(c16, w_ada[0], b_shard)
    mod_all = allgather_devices(mod_shard, "gather_mod")
    mod_full = jnp.concatenate([mod_all[2 * q] for q in range(N_CHIPS)], axis=1)
    mod = lax.dynamic_slice(mod_full, (dev, 0), (1, N_MOD * D))
    modc = mod_full[8:9]

    conv_shard = conv_w[0].reshape(9, CONV_SHARD)
    conv_hi = conv_shard.astype(BF16)
    conv_lo = (conv_shard - conv_hi.astype(F32)).astype(BF16)
    pack = _pack_rows([wts[n][0] for n, _, _ in BIG_SHARDS] + [conv_hi, conv_lo], WEIGHT_PACK_ROWS, BF16)
    w_all = allgather_chips(pack, "gather_weights")
    shapes = [s for _, s, _ in BIG_SHARDS] + [(9, CONV_SHARD)] * 2
    parts = [_unpack_rows(w_all[q], shapes) for q in range(N_CHIPS)]
    W = {}
    for k, (n, _, axis) in enumerate(BIG_SHARDS):
        W[n] = jnp.concatenate([parts[q][k] for q in range(N_CHIPS)], axis=axis)
    nb = len(BIG_SHARDS)
    W["conv_w"] = jnp.concatenate([parts[q][nb].astype(F32) + parts[q][nb + 1].astype(F32) for q in range(N_CHIPS)],
                                  axis=1).reshape(3, 3, 2 * FFN_HIDDEN)
    for n in ("g_mix", "g_ffn", "s5_d", "s5_b_glu", "sgu_ln_g", "sgu_ln_b", "b_gate", "conv_b"):
        W[n] = wts[n]
    W["g_final"] = g_final[None]
    for n in ("s5_a_re", "s5_a_im", "s5_log_step", "s5_b_re", "s5_b_im", "s5_c_re", "s5_c_im", "sgu_w", "sgu_b"):
        W[n] = wts[n][0]

    loss_part, grad_x, g = local_step(x[0], ctx[0], loss_target[0], mod, modc, W)
    loss = lax.psum(loss_part[0, 0], ("x", "y", "c"))

    def shard(a, q, size, axis):
        return lax.slice_in_dim(a, q * size, (q + 1) * size, axis=axis)

    g_pack = jnp.stack([_pack_rows([shard(g[n], q, s[axis], axis) for n, s, axis in BIG_SHARDS], GRAD_PACK_ROWS, F32)
                        for q in range(N_CHIPS)])
    half = GRAD_PACK_ROWS // 2
    keep = lax.dynamic_slice(g_pack, (0, ci * half, 0), (N_CHIPS, half, PACK_COLS))
    send = lax.dynamic_slice(g_pack, (0, (1 - ci) * half, 0), (N_CHIPS, half, PACK_COLS))
    recv = swap_sibling(send, "grad_pair_swap")
    add2 = lambda a, b: a + b
    pair = rowcall(add2, "grad_pair_sum", N_CHIPS * half, 256,
                   [(keep.reshape(N_CHIPS * half, PACK_COLS), PACK_COLS, 0, 0),
                    (recv.reshape(N_CHIPS * half, PACK_COLS), PACK_COLS, 0, 0)], [], [(PACK_COLS, F32)], [])[0]
    from_chips = exchange_chips(pair.reshape(N_CHIPS, half, PACK_COLS), "grad_chip_exchange")
    from_chips = from_chips.reshape(N_CHIPS * half, PACK_COLS)
    add4 = lambda a, b, c_, d: ((a + b) + c_) + d
    mine = rowcall(add4, "grad_chip_sum", half, 256,
                   [(from_chips, PACK_COLS, 0, q * half // 256) for q in range(N_CHIPS)], [], [(PACK_COLS, F32)], [])[0]
    other = swap_sibling(mine, "grad_half_swap")
    g_shard = jnp.where(ci == 0, jnp.concatenate([mine, other], axis=0), jnp.concatenate([other, mine], axis=0))
    big_grads = dict(zip([n for n, _, _ in BIG_SHARDS], _unpack_rows(g_shard, [s for _, s, _ in BIG_SHARDS])))

    small_pack = _pack_rows([g["dmod"], g["dmodc"], g["conv_w"]] + [g[n] for n in SMALL_PARAMS], SMALL_PACK_ROWS, F32)
    small_all = allgather_devices(small_pack, "gather_small_grads")
    small_2d = small_all.reshape(N_DEV * SMALL_PACK_ROWS, PACK_COLS)

    def add8(*a):
        s = a[0]
        for t in a[1:]:
            s = s + t
        return s

    small_sum = rowcall(add8, "small_grad_sum", SMALL_PACK_ROWS, 256,
                        [(small_2d, PACK_COLS, 0, k * SMALL_PACK_ROWS // 256) for k in range(N_DEV)], [],
                        [(PACK_COLS, F32)], [])[0]
    dmod_all = small_all[:, 0:N_MOD].reshape(N_DEV, N_MOD * D)
    dmod_sum = small_sum[0:N_MOD].reshape(1, N_MOD * D)
    dmodc_sum = jnp.pad(small_sum[N_MOD:N_MOD + 2].reshape(1, 2 * D), ((0, 0), (0, (N_MOD - 2) * D)))
    conv_grad = _unpack_rows(small_sum, [(3, 3, 2 * FFN_HIDDEN)], row0=N_MOD + 2)[0]
    small_grads = dict(zip(SMALL_PARAMS, _unpack_rows(small_sum, [wts[n].shape for n in SMALL_PARAMS], row0=SMALL_ROW0)))

    dm16 = jnp.concatenate([dmod_all, dmodc_sum, jnp.zeros((MOD_ROWS - 9, N_MOD * D), F32)], axis=0)
    dm_shard = lax.dynamic_slice(dm16, (0, chip * ADA_COLS), (MOD_ROWS, ADA_COLS))
    g_w_ada = rowcall(f_ada_outer, "w_ada_grad", D, 256, [(jnp.transpose(c16), MOD_ROWS, 0, 0)], [dm_shard],
                      [(ADA_COLS, F32)], [])[0]
    g_b_ada = rowcall(add2, "b_ada_grad", 1, 1, [(dmod_sum, N_MOD * D, 0, 0), (dmodc_sum, N_MOD * D, 0, 0)], [],
                      [(N_MOD * D, F32)], [])[0]
    dmc_rows = jnp.pad(dm_shard[8:9], ((0, 7), (0, 0)))
    cctx_part = matmul(dmc_rows, w_ada[0], "nt", F32, "c_ctx_partial")
    cctx_all = allgather_devices(cctx_part, "gather_c_ctx")
    cctx_4 = jnp.stack([cctx_all[2 * q, 0] for q in range(N_CHIPS)])
    g_c_ctx = rowcall(f_cctx_grad, "c_ctx_grad", 1, 1, [(c_ctx[None], D, 0, 0)], [cctx_4], [(D, F32)], [])[0]

    grads = dict(small_grads)
    grads.update(big_grads)
    grads["w_ada"] = g_w_ada
    grads["b_ada"] = g_b_ada
    grads["c_ctx"] = g_c_ctx
    grads["conv_w"] = lax.dynamic_slice(conv_grad, (0, 0, chip * CONV_SHARD), (3, 3, CONV_SHARD))
    grads = {n: grads[n].reshape(wts[n].shape) for n in WEIGHT_NAMES}

    delta, new_m, new_v = {}, {}, {}
    large = [n for n, _, _ in BIG_SHARDS] + ["w_ada", "conv_w"]
    for n in large:
        shape2d = (-1, wts[n].shape[-1])
        d_, m_, v_ = adamw(wts[n].reshape(shape2d), grads[n].reshape(shape2d), ms[n].reshape(shape2d),
                           vs[n].reshape(shape2d), "adamw_" + n)
        delta[n], new_m[n], new_v[n] = [t.reshape(wts[n].shape) for t in (d_, m_, v_)]
    rest = [n for n in WEIGHT_NAMES if n not in large]
    packs = [_pack_rows([src[n] for n in rest], ADAM_PACK_ROWS, F32) for src in (wts, grads, ms, vs)]
    outs = adamw(*packs, "adamw_replicated")
    for dst, packed in zip((delta, new_m, new_v), outs):
        dst.update(zip(rest, _unpack_rows(packed, [wts[n].shape for n in rest])))

    return (loss, grad_x[None], *[grads[n] for n in WEIGHT_NAMES], *[delta[n] for n in WEIGHT_NAMES],
            *[new_m[n] for n in WEIGHT_NAMES], *[new_v[n] for n in WEIGHT_NAMES])
```

```python
import functools

import jax
import jax.numpy as jnp
from jax import lax
from jax.experimental import pallas as pl
from jax.experimental.pallas import tpu as pltpu

F32, BF16 = jnp.float32, jnp.bfloat16
MESH = pl.DeviceIdType.MESH

D_MODEL = 1024
S5_WIDTH = 512
S5_GROUP = 16
S5_GROUPS = 32
S5_STATE = 64
SGU_WIDTH = 512
SGU_GROUPS = 8
CHUNK = 128
FFN_HIDDEN = 2816
GRID_W = 64
N_MOD = 6
EPS = 1e-6
N_STATE = S5_GROUPS * S5_STATE
OCTETS = 4
SCAN_T = 128
N_CHIPS = 4
N_DEV = 8
LANES = 128
VMEM_LIMIT_BYTES = 56 * 1024 * 1024
CONV_PAD = 72
CONV_ROWS = 256

ADAM_LR, ADAM_B1, ADAM_B2, ADAM_EPS, ADAM_WD, ADAM_STEP = 0.001, 0.9, 0.999, 1e-08, 0.01, 10


def _call(body, name, out_shape, grid=None, in_specs=None, out_specs=None, scratch=(), sem=None, **kw):
    params = pltpu.CompilerParams(dimension_semantics=sem, vmem_limit_bytes=VMEM_LIMIT_BYTES)
    extra = {} if grid is None else {"grid": grid}
    return pl.pallas_call(body, name=name, out_shape=out_shape, in_specs=in_specs, out_specs=out_specs,
                          scratch_shapes=list(scratch), compiler_params=params, **extra, **kw)


def _tile(n, target, mult=LANES):
    best = None
    t = mult
    while t <= min(n, target):
        if n % t == 0:
            best = t
        t += mult
    return best or n


@jax.custom_vjp
def mmul(a, b):
    return jnp.dot(a.astype(BF16), b.astype(BF16), preferred_element_type=F32)


def _mmul_fwd(a, b):
    return mmul(a, b), (a, b)


def _mmul_bwd(res, ct):
    a, b = res
    ctb = ct.astype(BF16)
    da = lax.dot_general(ctb, b.astype(BF16), (((1,), (1,)), ((), ())), preferred_element_type=F32)
    db = lax.dot_general(a.astype(BF16), ctb, (((0,), (0,)), ((), ())), preferred_element_type=F32)
    return da.astype(a.dtype), db.astype(b.dtype)


mmul.defvjp(_mmul_fwd, _mmul_bwd)

_DOT_DIMS = {"nn": ((1,), (0,)), "nt": ((1,), (1,)), "tn": ((0,), (0,))}


MM_TILE = 1408
MM_FULL_K = 2048


def matmul(a, b, mode, out_dtype, name, init=None):
    if mode == "nn":
        (M, K), (_, N) = a.shape, b.shape
    elif mode == "nt":
        (M, K), (N, _) = a.shape, b.shape
    else:
        (K, M), (_, N) = a.shape, b.shape
    tm, tn = _tile(M, MM_TILE, 8 if M < LANES else LANES), _tile(N, MM_TILE)
    tk = K if K <= MM_FULL_K else _tile(K, MM_TILE)
    nk = K // tk
    dims = (_DOT_DIMS[mode], ((), ()))
    has_init = init is not None
    use_acc = nk > 1 and out_dtype != F32

    def body(*refs):
        a_ref, b_ref = refs[:2]
        i_ref = refs[2] if has_init else None
        o_ref = refs[3] if has_init else refs[2]
        acc = refs[-1] if use_acc else o_ref
        k = pl.program_id(2)
        part = lax.dot_general(a_ref[...].astype(BF16), b_ref[...].astype(BF16), dims, preferred_element_type=F32)

        @pl.when(k == 0)
        def _():
            first = part + i_ref[...].astype(F32) if has_init else part
            acc[...] = first.astype(acc.dtype)

        if nk > 1:
            @pl.when(k > 0)
            def _():
                acc[...] += part

        if use_acc:
            @pl.when(k == nk - 1)
            def _():
                o_ref[...] = acc[...].astype(o_ref.dtype)

    if mode == "tn":
        a_spec = pl.BlockSpec((tk, tm), lambda i, j, k: (k, i))
    else:
        a_spec = pl.BlockSpec((tm, tk), lambda i, j, k: (i, k))
    if mode == "nt":
        b_spec = pl.BlockSpec((tn, tk), lambda i, j, k: (j, k))
    else:
        b_spec = pl.BlockSpec((tk, tn), lambda i, j, k: (k, j))
    o_spec = pl.BlockSpec((tm, tn), lambda i, j, k: (i, j))
    in_specs = [a_spec, b_spec] + ([o_spec] if has_init else [])
    args = (a, b) + ((init,) if has_init else ())
    return _call(body, name, jax.ShapeDtypeStruct((M, N), out_dtype), grid=(M // tm, N // tn, nk),
                 in_specs=in_specs, out_specs=o_spec, scratch=[pltpu.VMEM((tm, tn), F32)] if use_acc else [],
                 sem=("parallel", "parallel", "arbitrary"))(*args)


def rowcall(fn, name, nrows, tm, rins, vins, routs, aouts, ainit=None):
    n_r, n_v, n_ro = len(rins), len(vins), len(routs)
    n_i = len(aouts) if ainit is not None else 0

    def body(*refs):
        r_in, v_in, i_in = refs[:n_r], refs[n_r:n_r + n_v], refs[n_r + n_v:n_r + n_v + n_i]
        r_out, a_out = refs[n_r + n_v + n_i:n_r + n_v + n_i + n_ro], refs[n_r + n_v + n_i + n_ro:]
        outs = fn(*[r[...].astype(F32) for r in r_in], *[v[...] for v in v_in])
        if not isinstance(outs, (tuple, list)):
            outs = (outs,)
        for ref, val in zip(r_out, outs[:n_ro]):
            ref[...] = val.astype(ref.dtype)
        if a_out:
            @pl.when(pl.program_id(0) == 0)
            def _():
                for k, ref in enumerate(a_out):
                    ref[...] = i_in[k][...] if n_i else jnp.zeros_like(ref)

            for ref, val in zip(a_out, outs[n_ro:]):
                ref[...] += val.astype(F32)

    def rspec(width, cblk, roff):
        return pl.BlockSpec((tm, width), lambda i: (i + roff, cblk))

    def whole(shape):
        nd = len(shape)
        return pl.BlockSpec(tuple(shape), lambda i: (0,) * nd)

    inits = list(ainit) if n_i else []
    in_specs = [rspec(w, cb, ro) for (_, w, cb, ro) in rins] + [whole(v.shape) for v in vins + inits]
    out_specs = [rspec(w, 0, 0) for (w, _) in routs] + [whole(s) for s in aouts]
    out_shape = [jax.ShapeDtypeStruct((nrows, w), dt) for (w, dt) in routs] + \
                [jax.ShapeDtypeStruct(tuple(s), F32) for s in aouts]
    res = _call(body, name, out_shape, grid=(nrows // tm,), in_specs=in_specs, out_specs=out_specs,
                sem=("arbitrary",))(*[r[0] for r in rins], *vins, *inits)
    return res


def _rms(x):
    return lax.rsqrt(jnp.mean(x * x, axis=-1, keepdims=True) + EPS)


def f_modulate(x, g, sc, sh):
    return (x * _rms(x)) * g * (1.0 + sc) + sh


def f_resid_mod(x, o, ga, g, sc, sh):
    x1 = x + ga * o
    return x1, f_modulate(x1, g, sc, sh)


def f_final_loss(x1, dn, tgt, ga2, gf):
    x2 = x1 + ga2 * dn
    y = (x2 * _rms(x2)) * gf
    err = jnp.square(y - tgt)
    return 0.5 * jnp.sum(jnp.mean(err, axis=-1))


def _sgu_spatial(vn, w, bt):
    lo = lax.broadcasted_iota(jnp.int32, (1, LANES), 1) < (SGU_WIDTH // SGU_GROUPS)
    row_blocks = []
    for r in range(vn.shape[0] // CHUNK):
        rows = vn[r * CHUNK:(r + 1) * CHUNK]
        cols = []
        for j in range(SGU_WIDTH // LANES):
            blk = rows[:, j * LANES:(j + 1) * LANES]
            v_lo = jnp.where(lo, blk, 0.0)
            v_hi = jnp.where(lo, 0.0, blk)
            s = mmul(w[2 * j], v_lo) + mmul(w[2 * j + 1], v_hi)
            bias = jnp.where(lo, bt[:, 2 * j:2 * j + 1], bt[:, 2 * j + 1:2 * j + 2])
            cols.append(s + bias)
        row_blocks.append(jnp.concatenate(cols, axis=1))
    return jnp.concatenate(row_blocks, axis=0) if len(row_blocks) > 1 else row_blocks[0]


def f_mixer(u_a, y0, y1, zu, zv, ga0, ga1, gb0, gb1, d_skip, w_glu, b_glu, ln_g, ln_b, sgu_w, sgu_bt,
            w_pa, w_pb, b_gate):
    ys = u_a * d_skip + y0 + y1
    ge = jax.nn.gelu(ys)
    y_a = ge * jax.nn.sigmoid(mmul(ge, w_glu) + b_glu)
    u_sg = jax.nn.gelu(zu)
    v = jax.nn.gelu(zv)
    vc = v - jnp.mean(v, axis=-1, keepdims=True)
    vn = (vc * lax.rsqrt(jnp.mean(vc * vc, axis=-1, keepdims=True) + EPS)) * ln_g + ln_b
    y_b = u_sg * _sgu_spatial(vn, sgu_w, sgu_bt)
    gl_a = jnp.concatenate([ga0, ga1], axis=1) + b_gate[:, :D_MODEL]
    gl_b = jnp.concatenate([gb0, gb1], axis=1) + b_gate[:, D_MODEL:]
    return jax.nn.sigmoid(gl_a) * mmul(y_a, w_pa) + jax.nn.sigmoid(gl_b) * mmul(y_b, w_pb)


def _cmul(ar, ai, xr, xi):
    return ar * xr - ai * xi, ar * xi + ai * xr


SUB = 8
STRAND = SCAN_T // SUB


def _strand_perms():
    i = lax.broadcasted_iota(jnp.int32, (SCAN_T, SCAN_T), 0)
    j = lax.broadcasted_iota(jnp.int32, (SCAN_T, SCAN_T), 1)
    to_strands = jnp.where(j == STRAND * (i % SUB) + i // SUB, 1.0, 0.0).astype(BF16)
    to_tokens = jnp.where(i == STRAND * (j % SUB) + j // SUB, 1.0, 0.0).astype(BF16)
    return to_strands, to_tokens


def _permute(perm, v):
    return jnp.dot(perm, v, preferred_element_type=F32).astype(BF16)


def _scan_strands(xr, xi, pw_ref, q_ref, col, rev, conj, cr, ci):
    def tab(ref, lo):
        t_r = ref[lo:lo + SUB, col:col + LANES]
        t_i = ref[lo:lo + SUB, col + 512:col + 512 + LANES]
        return t_r, (-t_i if conj else t_i)

    a_r, a_i = tab(pw_ref, (STRAND - 1) * SUB if rev else 0)
    order = list(range(STRAND - 1, -1, -1) if rev else range(STRAND))
    lr, li = [None] * STRAND, [None] * STRAND
    for n, k in enumerate(order):
        lr[k], li[k] = xr[k * SUB:(k + 1) * SUB], xi[k * SUB:(k + 1) * SUB]
        if n:
            m_r, m_i = _cmul(a_r, a_i, lr[order[n - 1]], li[order[n - 1]])
            lr[k], li[k] = lr[k] + m_r, li[k] + m_i
    f_r, f_i = lr[order[-1]], li[order[-1]]
    q_r, q_i = tab(q_ref, 0)
    sub = lax.broadcasted_iota(jnp.int32, (SUB, 1), 0)
    s = 1
    while s < SUB:
        row = (SUB - s) if rev else (s - 1)
        shift = (SUB - s) if rev else s
        m = (sub < SUB - s) if rev else (sub >= s)
        p_r, p_i = _cmul(q_r[row:row + 1], q_i[row:row + 1], pltpu.roll(f_r, shift, 0), pltpu.roll(f_i, shift, 0))
        f_r, f_i = f_r + jnp.where(m, p_r, 0.0), f_i + jnp.where(m, p_i, 0.0)
        s *= 2
    c_r, c_i = jnp.broadcast_to(cr, (SUB, LANES)), jnp.broadcast_to(ci, (SUB, LANES))
    p_r, p_i = _cmul(q_r, q_i, c_r, c_i)
    s_r, s_i = f_r + p_r, f_i + p_i
    edge = 0 if rev else SUB - 1
    first = sub == (SUB - 1 if rev else 0)
    e_r = jnp.where(first, c_r, pltpu.roll(s_r, SUB - 1 if rev else 1, 0))
    e_i = jnp.where(first, c_i, pltpu.roll(s_i, SUB - 1 if rev else 1, 0))
    for k in range(STRAND):
        t_r, t_i = tab(pw_ref, k * SUB)
        p_r, p_i = _cmul(t_r, t_i, e_r, e_i)
        lr[k], li[k] = lr[k] + p_r, li[k] + p_i
    return lr, li, (s_r[edge:edge + 1], s_i[edge:edge + 1]), (e_r, e_i)


def s5_forward(u3, bm, cm, pw, q, rev, n_chunks, blk0, name):
    T = SCAN_T

    def pos(i):
        return (n_chunks - 1 - i) if rev else i

    def body(u_ref, bm_ref, cm_ref, pw_ref, q_ref, y_ref, cin_ref, carry):
        @pl.when(pl.program_id(0) == 0)
        def _():
            carry[...] = jnp.zeros_like(carry)

        cin_ref[...] = carry[...]
        to_strands, to_tokens = _strand_perms()
        u = _permute(to_strands, u_ref[...])
        for o in range(OCTETS):
            bu = jnp.dot(u[:, o * LANES:(o + 1) * LANES], bm_ref[o], preferred_element_type=F32)
            hr, hi = [], []
            for j in range(4):
                col = o * 1024 + j * LANES
                xr, xi, (cr, ci), _ = _scan_strands(
                    bu[:, j * LANES:(j + 1) * LANES], bu[:, 512 + j * LANES:512 + (j + 1) * LANES], pw_ref, q_ref, col,
                    rev, False, cin_ref[0:1, col:col + LANES], cin_ref[0:1, col + 512:col + 512 + LANES])
                carry[0:1, col:col + LANES] = cr
                carry[0:1, col + 512:col + 512 + LANES] = ci
                hr.append(jnp.concatenate(xr, axis=0))
                hi.append(jnp.concatenate(xi, axis=0))
            h = _permute(to_tokens, jnp.concatenate(hr + hi, axis=1).astype(BF16))
            y_ref[:, o * LANES:(o + 1) * LANES] = jnp.dot(h, cm_ref[o], preferred_element_type=F32)

    whole3 = lambda s: pl.BlockSpec(s, lambda i: (0, 0, 0))
    whole2 = lambda s: pl.BlockSpec(s, lambda i: (0, 0))
    return _call(
        body, name,
        [jax.ShapeDtypeStruct((n_chunks * T, S5_WIDTH), F32), jax.ShapeDtypeStruct((n_chunks, 1, 2 * N_STATE), F32)],
        grid=(n_chunks,),
        in_specs=[pl.BlockSpec((T, S5_WIDTH), lambda i: (blk0 + pos(i), 0)), whole3(bm.shape), whole3(cm.shape),
                  whole2(pw.shape), whole2(q.shape)],
        out_specs=[pl.BlockSpec((T, S5_WIDTH), lambda i: (pos(i), 0)),
                   pl.BlockSpec((None, 1, 2 * N_STATE), lambda i: (pos(i), 0, 0))],
        scratch=[pltpu.VMEM((1, 2 * N_STATE), F32)], sem=("arbitrary",))(u3, bm, cm, pw, q)


def s5_backward(u3, dy3, cin, bm, cm, pw_h, q_h, pw_l, q_l, rev, n_chunks, blk0, name):
    T = SCAN_T

    def pos(i):
        return i if rev else (n_chunks - 1 - i)

    def body(u_ref, dy_ref, cin_ref, bm_ref, cm_ref, pwh_ref, qh_ref, pwl_ref, ql_ref, du_ref, dbm_ref, dcm_ref,
             da_ref, lcarry):
        @pl.when(pl.program_id(0) == 0)
        def _():
            lcarry[...] = jnp.zeros_like(lcarry)
            dbm_ref[...] = jnp.zeros_like(dbm_ref)
            dcm_ref[...] = jnp.zeros_like(dcm_ref)
            da_ref[...] = jnp.zeros_like(da_ref)

        to_strands, to_tokens = _strand_perms()
        u = _permute(to_strands, u_ref[...])
        dy = _permute(to_strands, dy_ref[...])
        for o in range(OCTETS):
            u_o = u[:, o * LANES:(o + 1) * LANES]
            dy_o = dy[:, o * LANES:(o + 1) * LANES]
            bu = jnp.dot(u_o, bm_ref[o], preferred_element_type=F32)
            g = lax.dot_general(dy_o, cm_ref[o], (((1,), (1,)), ((), ())), preferred_element_type=F32)
            hr, hi, lr, li = [], [], [], []
            for j in range(4):
                col = o * 1024 + j * LANES
                sl_r = slice(j * LANES, (j + 1) * LANES)
                sl_i = slice(512 + j * LANES, 512 + (j + 1) * LANES)
                xr, xi, _, (e_r, e_i) = _scan_strands(
                    bu[:, sl_r], bu[:, sl_i], pwh_ref, qh_ref, col, rev, False,
                    cin_ref[0:1, col:col + LANES], cin_ref[0:1, col + 512:col + 512 + LANES])
                ar_, ai_, (l_r, l_i), _ = _scan_strands(
                    g[:, sl_r], g[:, sl_i], pwl_ref, ql_ref, col, not rev, True,
                    lcarry[0:1, col:col + LANES], lcarry[0:1, col + 512:col + 512 + LANES])
                lcarry[0:1, col:col + LANES] = l_r
                lcarry[0:1, col + 512:col + 512 + LANES] = l_i
                acc_r = acc_i = None
                for k in range(STRAND):
                    kp = k + 1 if rev else k - 1
                    p_r, p_i = (e_r, e_i) if not 0 <= kp < STRAND else (xr[kp], xi[kp])
                    t_r = ar_[k] * p_r + ai_[k] * p_i
                    t_i = ai_[k] * p_r - ar_[k] * p_i
                    acc_r, acc_i = (t_r, t_i) if acc_r is None else (acc_r + t_r, acc_i + t_i)
                da_ref[0:1, col:col + LANES] += jnp.sum(acc_r, axis=0, keepdims=True)
                da_ref[0:1, col + 512:col + 512 + LANES] += jnp.sum(acc_i, axis=0, keepdims=True)
                hr.append(jnp.concatenate(xr, axis=0))
                hi.append(jnp.concatenate(xi, axis=0))
                lr.append(jnp.concatenate(ar_, axis=0))
                li.append(jnp.concatenate(ai_, axis=0))
            h = jnp.concatenate(hr + hi, axis=1).astype(BF16)
            lam = jnp.concatenate(lr + li, axis=1).astype(BF16)
            du_ref[:, o * LANES:(o + 1) * LANES] = lax.dot_general(
                _permute(to_tokens, lam), bm_ref[o], (((1,), (1,)), ((), ())), preferred_element_type=F32)
            dbm_ref[o] += lax.dot_general(u_o, lam, (((0,), (0,)), ((), ())), preferred_element_type=F32)
            dcm_ref[o] += lax.dot_general(h, dy_o, (((0,), (0,)), ((), ())), preferred_element_type=F32)

    whole3 = lambda s: pl.BlockSpec(s, lambda i: (0, 0, 0))
    whole2 = lambda s: pl.BlockSpec(s, lambda i: (0, 0))
    row_spec = pl.BlockSpec((T, S5_WIDTH), lambda i: (blk0 + pos(i), 0))
    return _call(
        body, name,
        [jax.ShapeDtypeStruct((n_chunks * T, S5_WIDTH), F32), jax.ShapeDtypeStruct(bm.shape, F32),
         jax.ShapeDtypeStruct(cm.shape, F32), jax.ShapeDtypeStruct((1, 2 * N_STATE), F32)],
        grid=(n_chunks,),
        in_specs=[row_spec, row_spec, pl.BlockSpec((None, 1, 2 * N_STATE), lambda i: (pos(i), 0, 0)),
                  whole3(bm.shape), whole3(cm.shape), whole2(pw_h.shape), whole2(q_h.shape), whole2(pw_l.shape),
                  whole2(q_l.shape)],
        out_specs=[pl.BlockSpec((T, S5_WIDTH), lambda i: (pos(i), 0)), whole3(bm.shape), whole3(cm.shape),
                   whole2((1, 2 * N_STATE))],
        scratch=[pltpu.VMEM((1, 2 * N_STATE), F32)], sem=("arbitrary",))(u3, dy3, cin, bm, cm, pw_h, q_h, pw_l, q_l)


def s5_tables(ar, ai, ls):
    def body(ar_ref, ai_ref, ls_ref, pr_ref, pi_ref, qr_ref, qi_ref):
        dt = jnp.exp(ls_ref[...])
        for n_rows, step, o_r, o_i in ((STRAND, 1.0, pr_ref, pi_ref), (SUB, float(STRAND), qr_ref, qi_ref)):
            m = (lax.broadcasted_iota(jnp.int32, (n_rows, 1), 0) + 1).astype(F32) * step
            mag = jnp.exp(m * (ar_ref[...] * dt))
            ang = m * (ai_ref[...] * dt)
            o_r[...] = mag * jnp.cos(ang)
            o_i[...] = mag * jnp.sin(ang)

    vec = pl.BlockSpec((None, 1, N_STATE), lambda d: (d, 0, 0))
    tab = lambda n: pl.BlockSpec((None, n, N_STATE), lambda d: (d, 0, 0))
    shp = lambda n: jax.ShapeDtypeStruct((2, n, N_STATE), F32)
    return _call(body, "s5_tables", [shp(STRAND), shp(STRAND), shp(SUB), shp(SUB)], grid=(2,),
                 in_specs=[vec, vec, vec], out_specs=[tab(STRAND), tab(STRAND), tab(SUB), tab(SUB)],
                 sem=("arbitrary",))(ar, ai, ls)


def f_discretize(a_re, a_im, ls, b_re, b_im):
    dt = jnp.exp(ls)
    mag = jnp.exp(a_re * dt)
    ab_re = mag * jnp.cos(a_im * dt)
    ab_im = mag * jnp.sin(a_im * dt)
    p = ab_re - 1.0
    q = ab_im
    den = a_re * a_re + a_im * a_im
    k_re = ((p * a_re + q * a_im) / den)[None]
    k_im = ((q * a_re - p * a_im) / den)[None]
    return ab_re, ab_im, k_re * b_re - k_im * b_im, k_re * b_im + k_im * b_re


def _disc_specs():
    a = pl.BlockSpec((None, S5_GROUPS, S5_STATE), lambda d: (d, 0, 0))
    s = pl.BlockSpec((None, S5_GROUPS, 1), lambda d: (d, 0, 0))
    b = pl.BlockSpec((None, S5_GROUP, S5_GROUPS, S5_STATE), lambda d: (d, 0, 0, 0))
    return a, s, b


def s5_discretize(a_re, a_im, ls, b_re, b_im):
    def body(ar, ai, l, br, bi, obr, obi):
        _, _, r, i = f_discretize(ar[...], ai[...], l[...], br[...], bi[...])
        obr[...] = r
        obi[...] = i

    a, s, b = _disc_specs()
    return _call(body, "s5_discretize", [jax.ShapeDtypeStruct(b_re.shape, F32)] * 2, grid=(2,),
                 in_specs=[a, a, s, b, b], out_specs=[b, b], sem=("arbitrary",))(a_re, a_im, ls, b_re, b_im)


def s5_discretize_bwd(a_re, a_im, ls, b_re, b_im, dab_re, dab_im, dbb_re, dbb_im):
    def body(ar, ai, l, br, bi, c0, c1, c2, c3, o0, o1, o2, o3, o4):
        _, vjp = jax.vjp(f_discretize, ar[...], ai[...], l[...], br[...], bi[...])
        outs = vjp((c0[...], c1[...], c2[...], c3[...]))
        for ref, val in zip((o0, o1, o2, o3, o4), outs):
            ref[...] = val

    a, s, b = _disc_specs()
    shapes = [jax.ShapeDtypeStruct(t.shape, F32) for t in (a_re, a_im, ls, b_re, b_im)]
    return _call(body, "s5_discretize_bwd", shapes, grid=(2,), in_specs=[a, a, s, b, b, a, a, b, b],
                 out_specs=[a, a, s, b, b], sem=("arbitrary",))(a_re, a_im, ls, b_re, b_im, dab_re, dab_im,
                                                                 dbb_re, dbb_im)


def _conv_taps(s_ref, base, n_rows):
    n = n_rows + 2 * CONV_PAD
    ext = s_ref[pl.ds(base, n), :]
    col = (lax.broadcasted_iota(jnp.int32, (n, 1), 0) + (2 * GRID_W - CONV_PAD)) % GRID_W
    left = jnp.where(col == 0, 0.0, pltpu.roll(ext, 1, 0))
    right = jnp.where(col == GRID_W - 1, 0.0, pltpu.roll(ext, n - 1, 0))
    return left, ext, right


def _conv_apply(taps, w_ref, n_rows, flip):
    out = None
    for i in range(3):
        wi = 2 - i if flip else i
        comb = None
        for j in range(3):
            wj = 2 - j if flip else j
            term = w_ref[wi * 3 + wj:wi * 3 + wj + 1, :] * taps[j]
            comb = term if comb is None else comb + term
        start = CONV_PAD + (i - 1) * GRID_W
        part = comb[start:start + n_rows]
        out = part if out is None else out + part
    return out


def _conv_fill(dst_ref, src_ref, n_tok):
    zeros = jnp.zeros((CONV_PAD, LANES), F32)
    dst_ref[0:CONV_PAD, :] = zeros
    dst_ref[CONV_PAD + n_tok:2 * CONV_PAD + n_tok, :] = zeros

    def step(r, carry):
        base = pl.multiple_of(r * CONV_ROWS, CONV_ROWS)
        dst_ref[pl.ds(base + CONV_PAD, CONV_ROWS), :] = src_ref[pl.ds(base, CONV_ROWS), :].astype(F32)
        return carry

    lax.fori_loop(0, n_tok // CONV_ROWS, step, 0)


def conv_forward(up, wg, wv, bias):
    n_tok = up.shape[0]
    nb = FFN_HIDDEN // LANES

    def body(ug_ref, uv_ref, wg_ref, wv_ref, bg_ref, bv_ref, act_ref, gate_ref, val_ref, sg, sv):
        _conv_fill(sg, ug_ref, n_tok)
        _conv_fill(sv, uv_ref, n_tok)

        def step(r, carry):
            base = pl.multiple_of(r * CONV_ROWS, CONV_ROWS)
            gate = _conv_apply(_conv_taps(sg, base, CONV_ROWS), wg_ref, CONV_ROWS, False) + bg_ref[...]
            val = _conv_apply(_conv_taps(sv, base, CONV_ROWS), wv_ref, CONV_ROWS, False) + bv_ref[...]
            act_ref[pl.ds(base, CONV_ROWS), :] = (gate * jax.nn.sigmoid(gate) * val).astype(BF16)
            gate_ref[pl.ds(base, CONV_ROWS), :] = gate.astype(BF16)
            val_ref[pl.ds(base, CONV_ROWS), :] = val.astype(BF16)
            return carry

        lax.fori_loop(0, n_tok // CONV_ROWS, step, 0)

    col = lambda off: pl.BlockSpec((n_tok, LANES), lambda k: (0, k + off))
    wsp = lambda off: pl.BlockSpec((16, LANES), lambda k: (0, k + off))
    bsp = lambda off: pl.BlockSpec((1, LANES), lambda k: (0, k + off))
    pad = pltpu.VMEM((n_tok + 2 * CONV_PAD, LANES), F32)
    half = jax.ShapeDtypeStruct((n_tok, FFN_HIDDEN), BF16)
    return _call(body, "conv_forward", [half, half, half], grid=(nb,),
                 in_specs=[col(0), col(nb), wsp(0), wsp(0), bsp(0), bsp(nb)], out_specs=[col(0), col(0), col(0)],
                 scratch=[pad, pad], sem=("arbitrary",))(up, up, wg, wv, bias, bias)


def conv_backward(up, gate, val, dact, wg, wv):
    n_tok = up.shape[0]
    nb = FFN_HIDDEN // LANES
    n_steps = n_tok // CONV_ROWS

    def body(ug_ref, uv_ref, gate_ref, val_ref, da_ref, wg_ref, wv_ref, dug_ref, duv_ref, dwg_ref, dwv_ref,
             sg, sv, sdg, sdv):
        _conv_fill(sg, ug_ref, n_tok)
        _conv_fill(sv, uv_ref, n_tok)
        zeros = jnp.zeros((CONV_PAD, LANES), F32)
        for s_ref in (sdg, sdv):
            s_ref[0:CONV_PAD, :] = zeros
            s_ref[CONV_PAD + n_tok:2 * CONV_PAD + n_tok, :] = zeros
        dwg_ref[...] = jnp.zeros_like(dwg_ref)
        dwv_ref[...] = jnp.zeros_like(dwv_ref)

        def grads(r, carry):
            base = pl.multiple_of(r * CONV_ROWS, CONV_ROWS)
            taps_g = _conv_taps(sg, base, CONV_ROWS)
            taps_v = _conv_taps(sv, base, CONV_ROWS)
            gate = gate_ref[pl.ds(base, CONV_ROWS), :].astype(F32)
            val = val_ref[pl.ds(base, CONV_ROWS), :].astype(F32)
            d_act = da_ref[pl.ds(base, CONV_ROWS), :].astype(F32)
            sig = jax.nn.sigmoid(gate)
            d_gate = d_act * val * (sig * (1.0 + gate * (1.0 - sig)))
            d_val = d_act * (gate * sig)
            sdg[pl.ds(base + CONV_PAD, CONV_ROWS), :] = d_gate
            sdv[pl.ds(base + CONV_PAD, CONV_ROWS), :] = d_val
            for d_out, taps, dw_ref in ((d_gate, taps_g, dwg_ref), (d_val, taps_v, dwv_ref)):
                for i in range(3):
                    start = CONV_PAD + (i - 1) * GRID_W
                    for j in range(3):
                        k = i * 3 + j
                        dw_ref[k:k + 1, :] += jnp.sum(d_out * taps[j][start:start + CONV_ROWS], axis=0, keepdims=True)
                dw_ref[9:10, :] += jnp.sum(d_out, axis=0, keepdims=True)
            return carry

        lax.fori_loop(0, n_steps, grads, 0)

        def spread(r, carry):
            base = pl.multiple_of(r * CONV_ROWS, CONV_ROWS)
            dug_ref[pl.ds(base, CONV_ROWS), :] = _conv_apply(
                _conv_taps(sdg, base, CONV_ROWS), wg_ref, CONV_ROWS, True).astype(BF16)
            duv_ref[pl.ds(base, CONV_ROWS), :] = _conv_apply(
                _conv_taps(sdv, base, CONV_ROWS), wv_ref, CONV_ROWS, True).astype(BF16)
            return carry

        lax.fori_loop(0, n_steps, spread, 0)

    col = lambda off: pl.BlockSpec((n_tok, LANES), lambda k: (0, k + off))
    wsp = pl.BlockSpec((16, LANES), lambda k: (0, k))
    pad = pltpu.VMEM((n_tok + 2 * CONV_PAD, LANES), F32)
    half = jax.ShapeDtypeStruct((n_tok, FFN_HIDDEN), BF16)
    dw = jax.ShapeDtypeStruct((16, FFN_HIDDEN), F32)
    return _call(body, "conv_backward", [half, half, dw, dw], grid=(nb,),
                 in_specs=[col(0), col(nb), col(0), col(0), col(0), wsp, wsp],
                 out_specs=[col(0), col(0), wsp, wsp], scratch=[pad, pad, pad, pad],
                 sem=("arbitrary",))(up, up, gate, val, dact, wg, wv)


def f_adamw(w, g, m, v):
    m = ADAM_B1 * m + (1.0 - ADAM_B1) * g
    v = ADAM_B2 * v + (1.0 - ADAM_B2) * jnp.square(g)
    m_hat = m / (1.0 - ADAM_B1 ** ADAM_STEP)
    v_hat = v / (1.0 - ADAM_B2 ** ADAM_STEP)
    delta = -ADAM_LR * (m_hat / (jnp.sqrt(v_hat) + ADAM_EPS) + ADAM_WD * w)
    return delta, m, v


def adamw(w, g, m, v, name):
    shape = w.shape
    cols = shape[-1]
    rows = w.size // cols
    two_d = [t.reshape(rows, cols) for t in (w, g, m, v)]
    tm = _tile(rows, 256, 8) if rows % 8 == 0 else rows
    outs = rowcall(f_adamw, name, rows, tm, [(t, cols, 0, 0) for t in two_d], [], [(cols, F32)] * 3, [])
    return tuple(o.reshape(shape) for o in outs)


def _place():
    return lax.axis_index("x"), lax.axis_index("y"), lax.axis_index("c")


_ANY = pl.BlockSpec(memory_space=pl.ANY)


def allgather_devices(v, name):
    def body(v_ref, out_ref, send_sems, recv_sems, local_sem):
        x, y, c = _place()
        me, sibling = (x, y, c), (x, y, 1 - c)
        chips = [(1 - x, y), (x, 1 - y), (1 - x, 1 - y)]

        def slot(p):
            return out_ref.at[4 * p[0] + 2 * p[1] + p[2]]

        def copy(k, block, to, src=None):
            return pltpu.make_async_remote_copy(
                src_ref=slot(block) if src is None else src, dst_ref=slot(block),
                send_sem=send_sems.at[k], recv_sem=recv_sems.at[k], device_id=to, device_id_type=MESH)

        mine = pltpu.make_async_copy(v_ref, slot(me), local_sem)
        mine.start()
        first = [copy(0, me, sibling, src=v_ref)]
        first += [copy(1 + j, me, (*chip, c), src=v_ref) for j, chip in enumerate(chips)]
        for cp in first:
            cp.start()
        passed = [copy(4 + j, (*chip, c), sibling) for j, chip in enumerate(chips)]
        for j, chip in enumerate(chips):
            copy(1 + j, (*chip, c), me).wait_recv()
            passed[j].start()
        copy(0, sibling, me).wait_recv()
        for j, chip in enumerate(chips):
            copy(4 + j, (*chip, 1 - c), me).wait_recv()
        for cp in first + passed:
            cp.wait_send()
        mine.wait()

    return _call(body, name, jax.ShapeDtypeStruct((N_DEV,) + v.shape, v.dtype), in_specs=[_ANY], out_specs=_ANY,
                 scratch=[pltpu.SemaphoreType.DMA((7,)), pltpu.SemaphoreType.DMA((7,)), pltpu.SemaphoreType.DMA])(v)


def allgather_chips(v, name):
    half = v.shape[0] // 2

    def body(v_ref, out_ref, send_sems, recv_sems, local_sem):
        x, y, c = _place()
        sibling = (x, y, 1 - c)
        chips = [(1 - x, y), (x, 1 - y), (1 - x, 1 - y)]

        def rows(chip, h):
            return out_ref.at[2 * chip[0] + chip[1], pl.ds(h * half, half)]

        def copy(k, chip, h, to, src=None):
            return pltpu.make_async_remote_copy(
                src_ref=rows(chip, h) if src is None else src, dst_ref=rows(chip, h),
                send_sem=send_sems.at[k], recv_sem=recv_sems.at[k], device_id=to, device_id_type=MESH)

        mine = pltpu.make_async_copy(v_ref, out_ref.at[2 * x + y], local_sem)
        mine.start()
        first = [copy(j, (x, y), c, (*chip, c), src=v_ref.at[pl.ds(c * half, half)]) for j, chip in enumerate(chips)]
        for cp in first:
            cp.start()
        passed = [copy(3 + j, chip, c, sibling) for j, chip in enumerate(chips)]
        for j, chip in enumerate(chips):
            copy(j, chip, c, (x, y, c)).wait_recv()
            passed[j].start()
        for j, chip in enumerate(chips):
            copy(3 + j, chip, 1 - c, (x, y, c)).wait_recv()
        for cp in first + passed:
            cp.wait_send()
        mine.wait()

    return _call(body, name, jax.ShapeDtypeStruct((N_CHIPS,) + v.shape, v.dtype), in_specs=[_ANY], out_specs=_ANY,
                 scratch=[pltpu.SemaphoreType.DMA((6,)), pltpu.SemaphoreType.DMA((6,)), pltpu.SemaphoreType.DMA])(v)


def swap_sibling(v, name):
    def body(v_ref, out_ref, send_sem, recv_sem):
        x, y, c = _place()
        cp = pltpu.make_async_remote_copy(src_ref=v_ref, dst_ref=out_ref, send_sem=send_sem, recv_sem=recv_sem,
                                          device_id=(x, y, 1 - c), device_id_type=MESH)
        cp.start()
        cp.wait()

    return _call(body, name, jax.ShapeDtypeStruct(v.shape, v.dtype), in_specs=[_ANY], out_specs=_ANY,
                 scratch=[pltpu.SemaphoreType.DMA, pltpu.SemaphoreType.DMA])(v)


def exchange_chips(v, name):
    def body(v_ref, out_ref, send_sems, recv_sems, local_sem):
        x, y, c = _place()
        chips = [(1 - x, y), (x, 1 - y), (1 - x, 1 - y)]
        mine = pltpu.make_async_copy(v_ref.at[2 * x + y], out_ref.at[2 * x + y], local_sem)
        mine.start()
        sends = [pltpu.make_async_remote_copy(
            src_ref=v_ref.at[2 * chip[0] + chip[1]], dst_ref=out_ref.at[2 * x + y], send_sem=send_sems.at[j],
            recv_sem=recv_sems.at[j], device_id=(*chip, c), device_id_type=MESH) for j, chip in enumerate(chips)]
        for cp in sends:
            cp.start()
        for j, chip in enumerate(chips):
            pltpu.make_async_remote_copy(
                src_ref=v_ref.at[2 * x + y], dst_ref=out_ref.at[2 * chip[0] + chip[1]], send_sem=send_sems.at[j],
                recv_sem=recv_sems.at[j], device_id=(*chip, c), device_id_type=MESH).wait_recv()
        for cp in sends:
            cp.wait_send()
        mine.wait()

    return _call(body, name, jax.ShapeDtypeStruct(v.shape, v.dtype), in_specs=[_ANY], out_specs=_ANY,
                 scratch=[pltpu.SemaphoreType.DMA((3,)), pltpu.SemaphoreType.DMA((3,)), pltpu.SemaphoreType.DMA])(v)


ROW_TILE = 256

BIG_SHARDS = (("w_in", (D_MODEL, 896), 1), ("s5_w_glu", (128, S5_WIDTH), 0), ("w_proj_a", (S5_WIDTH, 256), 1),
              ("w_proj_b", (SGU_WIDTH, 256), 1), ("w_out", (256, D_MODEL), 0), ("w_up", (D_MODEL, 1408), 1),
              ("w_down", (704, D_MODEL), 0))
SMALL_PARAMS = ("g_mix", "s5_a_re", "s5_a_im", "s5_log_step", "s5_b_re", "s5_b_im", "s5_c_re", "s5_c_im", "s5_d",
                "s5_b_glu", "sgu_ln_g", "sgu_ln_b", "sgu_w", "sgu_b", "b_gate", "g_ffn", "conv_b", "g_final")
PACK_COLS = 1024


def _rows_of(n):
    return -(-n // PACK_COLS)


def _pack_rows(arrays, total_rows, dtype):
    parts = []
    used = 0
    for a in arrays:
        r = _rows_of(a.size)
        parts.append(jnp.pad(a.reshape(-1).astype(dtype), (0, r * PACK_COLS - a.size)).reshape(r, PACK_COLS))
        used += r
    if total_rows > used:
        parts.append(jnp.zeros((total_rows - used, PACK_COLS), dtype))
    return jnp.concatenate(parts, axis=0)


def _unpack_rows(packed, shapes, row0=0):
    out = []
    for s in shapes:
        n = 1
        for d in s:
            n *= d
        r = _rows_of(n)
        out.append(packed[row0:row0 + r].reshape(-1)[:n].reshape(s))
        row0 += r
    return out


def _octet_major(re, im):
    parts = []
    for o in range(OCTETS):
        parts += [re[:, o * 512:(o + 1) * 512], im[:, o * 512:(o + 1) * 512]]
    return jnp.concatenate(parts, axis=1)


def _octet_split(v):
    v = v.reshape(OCTETS, 2, 512)
    return v[:, 0].reshape(N_STATE), v[:, 1].reshape(N_STATE)


def _s5_bmat(bb):
    t = bb.reshape(S5_GROUP, OCTETS, 8, 1, S5_STATE) * jnp.eye(8, dtype=F32)[None, None, :, :, None]
    return jnp.transpose(t, (1, 2, 0, 3, 4)).reshape(OCTETS, LANES, 512)


def _s5_bmat_t(dm):
    t = dm.reshape(OCTETS, 8, S5_GROUP, 8, S5_STATE) * jnp.eye(8, dtype=F32)[None, :, None, :, None]
    return jnp.transpose(t.sum(axis=3), (2, 0, 1, 3)).reshape(S5_GROUP, S5_GROUPS, S5_STATE)


def _s5_cmat(c):
    t = c.reshape(OCTETS, 8, 1, S5_GROUP, S5_STATE) * jnp.eye(8, dtype=F32)[None, :, :, None, None]
    return jnp.transpose(t, (0, 2, 4, 1, 3)).reshape(OCTETS, 512, LANES)


def _s5_cmat_t(dm):
    t = dm.reshape(OCTETS, 8, S5_STATE, 8, S5_GROUP) * jnp.eye(8, dtype=F32)[None, :, None, :, None]
    return jnp.transpose(t.sum(axis=1), (0, 2, 3, 1)).reshape(S5_GROUPS, S5_GROUP, S5_STATE)


def local_step(x, ctx, tgt, mod, modc, W):
    n_tok, n_ctx = x.shape[0], ctx.shape[0]
    tm = ROW_TILE
    D = D_MODEL
    sh1, sc1, ga1, sh2, sc2, ga2 = [mod[:, k * D:(k + 1) * D] for k in range(N_MOD)]
    sh1c, sc1c = modc[:, :D], modc[:, D:2 * D]
    g_mix, g_ffn, g_final = W["g_mix"], W["g_ffn"], W["g_final"]
    w_in = W["w_in"]
    w_in_u = w_in[:, :S5_WIDTH]

    h = rowcall(f_modulate, "mod1", n_tok, tm, [(x, D, 0, 0)], [g_mix, sc1, sh1], [(D, BF16)], [])[0]
    hc = rowcall(f_modulate, "mod1_ctx", n_ctx, tm, [(ctx, D, 0, 0)], [g_mix, sc1c, sh1c], [(D, BF16)], [])[0]
    proj = matmul(h, w_in, "nn", BF16, "proj_in")
    uc = matmul(hc, w_in_u, "nn", BF16, "proj_in_ctx")
    u3 = jnp.concatenate([uc, proj[:, :S5_WIDTH], uc], axis=0)

    a_re, a_im, ls = W["s5_a_re"], W["s5_a_im"], W["s5_log_step"][..., None]
    b_re_t = jnp.transpose(W["s5_b_re"], (0, 3, 1, 2))
    b_im_t = jnp.transpose(W["s5_b_im"], (0, 3, 1, 2))
    bb_re, bb_im = s5_discretize(a_re, a_im, ls, b_re_t, b_im_t)
    ls_rep = jnp.repeat(W["s5_log_step"], S5_STATE, axis=1).reshape(2, 1, N_STATE)
    p_re, p_im, q_re, q_im = s5_tables(a_re.reshape(2, 1, N_STATE), a_im.reshape(2, 1, N_STATE), ls_rep)
    n_chunks = (n_ctx + n_tok) // SCAN_T
    ctx_blk = n_ctx // SCAN_T
    pw, qt, bm, cm = [], [], [], []
    for d in range(2):
        p16 = _octet_major(p_re[d], p_im[d])
        q8 = _octet_major(q_re[d], q_im[d])
        pw.append((jnp.repeat(p16, SUB, axis=0), jnp.repeat(p16[::-1], SUB, axis=0)))
        qt.append((q8, q8[::-1]))
        bm.append(jnp.concatenate([_s5_bmat(bb_re[d]), _s5_bmat(bb_im[d])], axis=2).astype(BF16))
        cm.append(jnp.concatenate([_s5_cmat(W["s5_c_re"][d]), -_s5_cmat(W["s5_c_im"][d])], axis=1).astype(BF16))
    y0, cin0 = s5_forward(u3, bm[0], cm[0], pw[0][0], qt[0][0], False, n_chunks, 0, "s5_fwd0")
    y1, cin1 = s5_forward(u3, bm[1], cm[1], pw[1][1], qt[1][1], True, n_chunks, ctx_blk, "s5_fwd1")

    mix_rows = [(proj, 512, 0, 0), (y0, 512, 0, n_ctx // tm), (y1, 512, 0, 0)] + \
               [(proj, 512, k, 0) for k in range(1, 7)]
    mix_vecs = [W["s5_d"], W["s5_w_glu"], W["s5_b_glu"], W["sgu_ln_g"], W["sgu_ln_b"], W["sgu_w"],
                jnp.transpose(W["sgu_b"]), W["w_proj_a"], W["w_proj_b"], W["b_gate"]]
    mrg = rowcall(f_mixer, "mixer", n_tok, tm, mix_rows, mix_vecs, [(D, BF16)], [])[0]
    o = matmul(mrg, W["w_out"], "nn", F32, "proj_out")
    x1, h2 = rowcall(f_resid_mod, "resid_mod2", n_tok, tm, [(x, D, 0, 0), (o, D, 0, 0)], [ga1, g_ffn, sc2, sh2],
                     [(D, F32), (D, BF16)], [])
    up = matmul(h2, W["w_up"], "nn", BF16, "ffn_up")
    conv_w = W["conv_w"].reshape(9, 2 * FFN_HIDDEN)
    wg = jnp.pad(conv_w[:, :FFN_HIDDEN], ((0, 7), (0, 0)))
    wv = jnp.pad(conv_w[:, FFN_HIDDEN:], ((0, 7), (0, 0)))
    act, gate, val = conv_forward(up, wg, wv, W["conv_b"])
    dn = matmul(act, W["w_down"], "nn", F32, "ffn_down")

    def final_fn(x1_, dn_, tgt_, ga2_, gf_):
        loss, (dx1_, ddn_, dga2_, dgf_) = jax.value_and_grad(f_final_loss, argnums=(0, 1, 3, 4))(
            x1_, dn_, tgt_, ga2_, gf_)
        return dx1_, ddn_, loss.reshape(1, 1), dga2_, dgf_

    dx2, ddn, loss, d_ga2, d_gfinal = rowcall(
        final_fn, "final_loss", n_tok, tm, [(x1, D, 0, 0), (dn, D, 0, 0), (tgt, D, 0, 0)], [ga2, g_final],
        [(D, F32), (D, BF16)], [(1, 1), (1, D), (1, D)])

    dact = matmul(ddn, W["w_down"], "nt", BF16, "ffn_down_dx")
    d_w_down = matmul(act, ddn, "tn", F32, "ffn_down_dw")
    dup_g, dup_v, dwg, dwv = conv_backward(up, gate, val, dact, wg, wv)
    dup = jnp.concatenate([dup_g, dup_v], axis=1)
    d_conv_w = jnp.concatenate([dwg[:9], dwv[:9]], axis=1).reshape(3, 3, 2 * FFN_HIDDEN)
    d_conv_b = jnp.concatenate([dwg[9:10], dwv[9:10]], axis=1)
    dh2 = matmul(dup, W["w_up"], "nt", BF16, "ffn_up_dx")
    d_w_up = matmul(h2, dup, "tn", F32, "ffn_up_dw")

    def resid_bwd(x_, o_, dx1_, dh2_, ga_, g_, sc_, sh_):
        _, vjp = jax.vjp(f_resid_mod, x_, o_, ga_, g_, sc_, sh_)
        return vjp((dx1_, dh2_))

    dxa, do, d_ga1, d_gffn, d_sc2, d_sh2 = rowcall(
        resid_bwd, "resid_mod2_bwd", n_tok, tm, [(x, D, 0, 0), (o, D, 0, 0), (dx2, D, 0, 0), (dh2, D, 0, 0)],
        [ga1, g_ffn, sc2, sh2], [(D, F32), (D, BF16)], [(1, D)] * 4)

    dmrg = matmul(do, W["w_out"], "nt", BF16, "proj_out_dx")
    d_w_out = matmul(mrg, do, "tn", F32, "proj_out_dw")

    def mixer_bwd(*args):
        rows, dm, vecs = args[:9], args[9], [v.astype(F32) for v in args[10:]]
        _, vjp = jax.vjp(f_mixer, *rows, *vecs)
        g = vjp(dm)
        return (g[0], g[1], jnp.concatenate([g[3], g[4]], axis=1), jnp.concatenate(g[5:9], axis=1)) + tuple(g[9:])

    mb = rowcall(mixer_bwd, "mixer_bwd", n_tok, tm, mix_rows + [(dmrg, D, 0, 0)], mix_vecs,
                 [(512, BF16), (512, BF16), (1024, BF16), (2048, BF16)], [v.shape for v in mix_vecs])
    du_direct, dys, dzb, dgl = mb[:4]
    d_s5d, d_w_glu, d_b_glu, d_ln_g, d_ln_b, d_sgu_w, d_sgu_bt, d_w_pa, d_w_pb, d_b_gate = mb[4:]

    zc = jnp.zeros((n_ctx, S5_WIDTH), BF16)
    dy3 = jnp.concatenate([zc, dys, zc], axis=0)
    du0, dbm0, dcm0, da0 = s5_backward(u3, dy3, cin0, bm[0], cm[0], pw[0][0], qt[0][0], pw[0][1], qt[0][1], False,
                                       n_chunks, 0, "s5_bwd0")
    du1, dbm1, dcm1, da1 = s5_backward(u3, dy3, cin1, bm[1], cm[1], pw[1][1], qt[1][1], pw[1][0], qt[1][0], True,
                                       n_chunks, ctx_blk, "s5_bwd1")
    add3 = lambda a, b, c: a + b + c
    du_a = rowcall(add3, "du_sum", n_tok, tm, [(du_direct, 512, 0, 0), (du0, 512, 0, n_ctx // tm), (du1, 512, 0, 0)],
                   [], [(512, BF16)], [])[0]
    du_c = rowcall(lambda a, b: a + b, "du_sum_ctx", n_ctx, tm, [(du0, 512, 0, 0), (du1, 512, 0, n_tok // tm)],
                   [], [(512, BF16)], [])[0]

    dab_re, dab_im, dbb_re, dbb_im, d_c_re, d_c_im = [], [], [], [], [], []
    for dbm, dcm, da in ((dbm0, dcm0, da0), (dbm1, dcm1, da1)):
        r, i = _octet_split(da)
        dab_re.append(r.reshape(S5_GROUPS, S5_STATE))
        dab_im.append(i.reshape(S5_GROUPS, S5_STATE))
        dbb_re.append(_s5_bmat_t(dbm[:, :, :512]))
        dbb_im.append(_s5_bmat_t(dbm[:, :, 512:]))
        d_c_re.append(_s5_cmat_t(dcm[:, :512]))
        d_c_im.append(-_s5_cmat_t(dcm[:, 512:]))
    d_a_re, d_a_im, d_ls, d_b_re_t, d_b_im_t = s5_discretize_bwd(
        a_re, a_im, ls, b_re_t, b_im_t, jnp.stack(dab_re), jnp.stack(dab_im), jnp.stack(dbb_re), jnp.stack(dbb_im))

    dproj = jnp.concatenate([du_a, dzb, dgl], axis=1)
    dh = matmul(dproj, w_in, "nt", BF16, "proj_in_dx")
    dhc = matmul(du_c, w_in_u, "nt", BF16, "proj_in_ctx_dx")
    d_w_in_c = matmul(hc, du_c, "tn", F32, "proj_in_ctx_dw")
    d_w_in = matmul(h, dproj, "tn", F32, "proj_in_dw",
                    init=jnp.pad(d_w_in_c, ((0, 0), (0, w_in.shape[1] - S5_WIDTH))))

    def mod_bwd_ctx(x_, dh_, g_, sc_, sh_):
        _, vjp = jax.vjp(f_modulate, x_, g_, sc_, sh_)
        return vjp(dh_)[1:]

    d_gmix_c, d_sc1c, d_sh1c = rowcall(mod_bwd_ctx, "mod1_ctx_bwd", n_ctx, tm, [(ctx, D, 0, 0), (dhc, D, 0, 0)],
                                       [g_mix, sc1c, sh1c], [], [(1, D)] * 3)

    def mod_bwd(x_, dh_, dxa_, g_, sc_, sh_):
        _, vjp = jax.vjp(f_modulate, x_, g_, sc_, sh_)
        dx_, dg_, dsc_, dsh_ = vjp(dh_)
        return dx_ + dxa_, dg_, dsc_, dsh_

    zero_d = jnp.zeros((1, D), F32)
    grad_x, d_gmix, d_sc1, d_sh1 = rowcall(
        mod_bwd, "mod1_bwd", n_tok, tm, [(x, D, 0, 0), (dh, D, 0, 0), (dxa, D, 0, 0)], [g_mix, sc1, sh1],
        [(D, F32)], [(1, D)] * 3, ainit=[d_gmix_c, zero_d, zero_d])

    grads = {
        "dmod": jnp.concatenate([d_sh1, d_sc1, d_ga1, d_sh2, d_sc2, d_ga2], axis=1),
        "dmodc": jnp.concatenate([d_sh1c, d_sc1c], axis=1),
        "g_mix": d_gmix,
        "s5_a_re": d_a_re, "s5_a_im": d_a_im, "s5_log_step": d_ls[..., 0],
        "s5_b_re": jnp.transpose(d_b_re_t, (0, 2, 3, 1)), "s5_b_im": jnp.transpose(d_b_im_t, (0, 2, 3, 1)),
        "s5_c_re": jnp.stack(d_c_re), "s5_c_im": jnp.stack(d_c_im), "s5_d": d_s5d, "s5_b_glu": d_b_glu,
        "sgu_ln_g": d_ln_g, "sgu_ln_b": d_ln_b, "sgu_w": d_sgu_w, "sgu_b": jnp.transpose(d_sgu_bt),
        "b_gate": d_b_gate, "g_ffn": d_gffn, "conv_b": d_conv_b, "g_final": d_gfinal, "conv_w": d_conv_w,
        "w_in": d_w_in, "s5_w_glu": d_w_glu, "w_proj_a": d_w_pa, "w_proj_b": d_w_pb, "w_out": d_w_out,
        "w_up": d_w_up, "w_down": d_w_down,
    }
    return loss, grad_x, grads


ADA_COLS = N_MOD * D_MODEL // N_CHIPS
MOD_ROWS = 16


def mod_forward(c16, w, b):
    n = w.shape[1]
    tn = 512

    def body(c_ref, w_ref, b_ref, o_ref):
        cv = c_ref[...]
        cs = cv * jax.nn.sigmoid(cv)
        o_ref[...] = jnp.dot(cs.astype(BF16), w_ref[...].astype(BF16), preferred_element_type=F32) + b_ref[...]

    return _call(body, "mod_forward", jax.ShapeDtypeStruct((MOD_ROWS, n), F32), grid=(n // tn,),
                 in_specs=[pl.BlockSpec((MOD_ROWS, D_MODEL), lambda j: (0, 0)),
                           pl.BlockSpec((D_MODEL, tn), lambda j: (0, j)), pl.BlockSpec((1, tn), lambda j: (0, j))],
                 out_specs=pl.BlockSpec((MOD_ROWS, tn), lambda j: (0, j)), sem=("arbitrary",))(c16, w, b)


def f_ada_outer(ct, dm):
    cs = ct * jax.nn.sigmoid(ct)
    acc = cs[:, 0:1] * dm[0:1]
    for k in range(1, 9):
        acc = acc + cs[:, k:k + 1] * dm[k:k + 1]
    return acc


def f_cctx_grad(z, p4):
    s = jax.nn.sigmoid(z)
    return (p4[0:1] + p4[1:2] + p4[2:3] + p4[3:4]) * (s + z * s * (1.0 - s))


WEIGHT_NAMES = ("c_ctx", "w_ada", "b_ada", "g_mix", "w_in", "s5_a_re", "s5_a_im", "s5_log_step", "s5_b_re",
                "s5_b_im", "s5_c_re", "s5_c_im", "s5_d", "s5_w_glu", "s5_b_glu", "sgu_ln_g", "sgu_ln_b", "sgu_w",
                "sgu_b", "w_proj_a", "w_proj_b", "b_gate", "w_out", "g_ffn", "w_up", "conv_w", "conv_b", "w_down",
                "g_final")
CONV_SHARD = 2 * FFN_HIDDEN // N_CHIPS
WEIGHT_PACK_ROWS = 3616
GRAD_PACK_ROWS = 3584
SMALL_PACK_ROWS = 512
SMALL_ROW0 = 58
ADAM_PACK_ROWS = 512


def kernel(x, c, ctx, c_ctx, w_ada, b_ada, g_mix, w_in, s5_a_re, s5_a_im, s5_log_step, s5_b_re, s5_b_im, s5_c_re, s5_c_im, s5_d, s5_w_glu, s5_b_glu, sgu_ln_g, sgu_ln_b, sgu_w, sgu_b, w_proj_a, w_proj_b, b_gate, w_out, g_ffn, w_up, conv_w, conv_b, w_down, g_final, loss_target, m_c_ctx, m_w_ada, m_b_ada, m_g_mix, m_w_in, m_s5_a_re, m_s5_a_im, m_s5_log_step, m_s5_b_re, m_s5_b_im, m_s5_c_re, m_s5_c_im, m_s5_d, m_s5_w_glu, m_s5_b_glu, m_sgu_ln_g, m_sgu_ln_b, m_sgu_w, m_sgu_b, m_w_proj_a, m_w_proj_b, m_b_gate, m_w_out, m_g_ffn, m_w_up, m_conv_w, m_conv_b, m_w_down, m_g_final, v_c_ctx, v_w_ada, v_b_ada, v_g_mix, v_w_in, v_s5_a_re, v_s5_a_im, v_s5_log_step, v_s5_b_re, v_s5_b_im, v_s5_c_re, v_s5_c_im, v_s5_d, v_s5_w_glu, v_s5_b_glu, v_sgu_ln_g, v_sgu_ln_b, v_sgu_w, v_sgu_b, v_w_proj_a, v_w_proj_b, v_b_gate, v_w_out, v_g_ffn, v_w_up, v_conv_w, v_conv_b, v_w_down, v_g_final):
    given = dict(locals())
    wts = {n: given[n] for n in WEIGHT_NAMES}
    ms = {n: given["m_" + n] for n in WEIGHT_NAMES}
    vs = {n: given["v_" + n] for n in WEIGHT_NAMES}
    xi, yi, ci = _place()
    chip = 2 * xi + yi
    dev = 2 * chip + ci
    D = D_MODEL

    c8 = allgather_devices(jnp.pad(c, ((0, 7), (0, 0))), "gather_c")[:, 0, :]
    c16 = jnp.concatenate([c8, c_ctx[None], jnp.zeros((MOD_ROWS - 9, D), F32)], axis=0)
    b_shard = lax.dynamic_slice(b_ada, (0, chip * ADA_COLS), (1, ADA_COLS))
    mod_shard = mod_forward(c16, w_ada[0], b_shard)
    mod_all = allgather_devices(mod_shard, "gather_mod")
    mod_full = jnp.concatenate([mod_all[2 * q] for q in range(N_CHIPS)], axis=1)
    mod = lax.dynamic_slice(mod_full, (dev, 0), (1, N_MOD * D))
    modc = mod_full[8:9]

    conv_shard = conv_w[0].reshape(9, CONV_SHARD)
    conv_hi = conv_shard.astype(BF16)
    conv_lo = (conv_shard - conv_hi.astype(F32)).astype(BF16)
    pack = _pack_rows([wts[n][0] for n, _, _ in BIG_SHARDS] + [conv_hi, conv_lo], WEIGHT_PACK_ROWS, BF16)
    w_all = allgather_chips(pack, "gather_weights")
    shapes = [s for _, s, _ in BIG_SHARDS] + [(9, CONV_SHARD)] * 2
    parts = [_unpack_rows(w_all[q], shapes) for q in range(N_CHIPS)]
    W = {}
    for k, (n, _, axis) in enumerate(BIG_SHARDS):
        W[n] = jnp.concatenate([parts[q][k] for q in range(N_CHIPS)], axis=axis)
    nb = len(BIG_SHARDS)
    W["conv_w"] = jnp.concatenate([parts[q][nb].astype(F32) + parts[q][nb + 1].astype(F32) for q in range(N_CHIPS)],
                                  axis=1).reshape(3, 3, 2 * FFN_HIDDEN)
    for n in ("g_mix", "g_ffn", "s5_d", "s5_b_glu", "sgu_ln_g", "sgu_ln_b", "b_gate", "conv_b"):
        W[n] = wts[n]
    W["g_final"] = g_final[None]
    for n in ("s5_a_re", "s5_a_im", "s5_log_step", "s5_b_re", "s5_b_im", "s5_c_re", "s5_c_im", "sgu_w", "sgu_b"):
        W[n] = wts[n][0]

    loss_part, grad_x, g = local_step(x[0], ctx[0], loss_target[0], mod, modc, W)
    loss = lax.psum(loss_part[0, 0], ("x", "y", "c"))

    def shard(a, q, size, axis):
        return lax.slice_in_dim(a, q * size, (q + 1) * size, axis=axis)

    g_pack = jnp.stack([_pack_rows([shard(g[n], q, s[axis], axis) for n, s, axis in BIG_SHARDS], GRAD_PACK_ROWS, F32)
                        for q in range(N_CHIPS)])
    half = GRAD_PACK_ROWS // 2
    keep = lax.dynamic_slice(g_pack, (0, ci * half, 0), (N_CHIPS, half, PACK_COLS))
    send = lax.dynamic_slice(g_pack, (0, (1 - ci) * half, 0), (N_CHIPS, half, PACK_COLS))
    recv = swap_sibling(send, "grad_pair_swap")
    add2 = lambda a, b: a + b
    pair = rowcall(add2, "grad_pair_sum", N_CHIPS * half, 256,
                   [(keep.reshape(N_CHIPS * half, PACK_COLS), PACK_COLS, 0, 0),
                    (recv.reshape(N_CHIPS * half, PACK_COLS), PACK_COLS, 0, 0)], [], [(PACK_COLS, F32)], [])[0]
    from_chips = exchange_chips(pair.reshape(N_CHIPS, half, PACK_COLS), "grad_chip_exchange")
    from_chips = from_chips.reshape(N_CHIPS * half, PACK_COLS)
    add4 = lambda a, b, c_, d: ((a + b) + c_) + d
    mine = rowcall(add4, "grad_chip_sum", half, 256,
                   [(from_chips, PACK_COLS, 0, q * half // 256) for q in range(N_CHIPS)], [], [(PACK_COLS, F32)], [])[0]
    other = swap_sibling(mine, "grad_half_swap")
    g_shard = jnp.where(ci == 0, jnp.concatenate([mine, other], axis=0), jnp.concatenate([other, mine], axis=0))
    big_grads = dict(zip([n for n, _, _ in BIG_SHARDS], _unpack_rows(g_shard, [s for _, s, _ in BIG_SHARDS])))

    small_pack = _pack_rows([g["dmod"], g["dmodc"], g["conv_w"]] + [g[n] for n in SMALL_PARAMS], SMALL_PACK_ROWS, F32)
    small_all = allgather_devices(small_pack, "gather_small_grads")
    small_2d = small_all.reshape(N_DEV * SMALL_PACK_ROWS, PACK_COLS)

    def add8(*a):
        s = a[0]
        for t in a[1:]:
            s = s + t
        return s

    small_sum = rowcall(add8, "small_grad_sum", SMALL_PACK_ROWS, 256,
                        [(small_2d, PACK_COLS, 0, k * SMALL_PACK_ROWS // 256) for k in range(N_DEV)], [],
                        [(PACK_COLS, F32)], [])[0]
    dmod_all = small_all[:, 0:N_MOD].reshape(N_DEV, N_MOD * D)
    dmod_sum = small_sum[0:N_MOD].reshape(1, N_MOD * D)
    dmodc_sum = jnp.pad(small_sum[N_MOD:N_MOD + 2].reshape(1, 2 * D), ((0, 0), (0, (N_MOD - 2) * D)))
    conv_grad = _unpack_rows(small_sum, [(3, 3, 2 * FFN_HIDDEN)], row0=N_MOD + 2)[0]
    small_grads = dict(zip(SMALL_PARAMS, _unpack_rows(small_sum, [wts[n].shape for n in SMALL_PARAMS], row0=SMALL_ROW0)))

    dm16 = jnp.concatenate([dmod_all, dmodc_sum, jnp.zeros((MOD_ROWS - 9, N_MOD * D), F32)], axis=0)
    dm_shard = lax.dynamic_slice(dm16, (0, chip * ADA_COLS), (MOD_ROWS, ADA_COLS))
    g_w_ada = rowcall(f_ada_outer, "w_ada_grad", D, 256, [(jnp.transpose(c16), MOD_ROWS, 0, 0)], [dm_shard],
                      [(ADA_COLS, F32)], [])[0]
    g_b_ada = rowcall(add2, "b_ada_grad", 1, 1, [(dmod_sum, N_MOD * D, 0, 0), (dmodc_sum, N_MOD * D, 0, 0)], [],
                      [(N_MOD * D, F32)], [])[0]
    dmc_rows = jnp.pad(dm_shard[8:9], ((0, 7), (0, 0)))
    cctx_part = matmul(dmc_rows, w_ada[0], "nt", F32, "c_ctx_partial")
    cctx_all = allgather_devices(cctx_part, "gather_c_ctx")
    cctx_4 = jnp.stack([cctx_all[2 * q, 0] for q in range(N_CHIPS)])
    g_c_ctx = rowcall(f_cctx_grad, "c_ctx_grad", 1, 1, [(c_ctx[None], D, 0, 0)], [cctx_4], [(D, F32)], [])[0]

    grads = dict(small_grads)
    grads.update(big_grads)
    grads["w_ada"] = g_w_ada
    grads["b_ada"] = g_b_ada
    grads["c_ctx"] = g_c_ctx
    grads["conv_w"] = lax.dynamic_slice(conv_grad, (0, 0, chip * CONV_SHARD), (3, 3, CONV_SHARD))
    grads = {n: grads[n].reshape(wts[n].shape) for n in WEIGHT_NAMES}

    delta, new_m, new_v = {}, {}, {}
    large = [n for n, _, _ in BIG_SHARDS] + ["w_ada", "conv_w"]
    for n in large:
        shape2d = (-1, wts[n].shape[-1])
        d_, m_, v_ = adamw(wts[n].reshape(shape2d), grads[n].reshape(shape2d), ms[n].reshape(shape2d),
                           vs[n].reshape(shape2d), "adamw_" + n)
        delta[n], new_m[n], new_v[n] = [t.reshape(wts[n].shape) for t in (d_, m_, v_)]
    rest = [n for n in WEIGHT_NAMES if n not in large]
    packs = [_pack_rows([src[n] for n in rest], ADAM_PACK_ROWS, F32) for src in (wts, grads, ms, vs)]
    outs = adamw(*packs, "adamw_replicated")
    for dst, packed in zip((delta, new_m, new_v), outs):
        dst.update(zip(rest, _unpack_rows(packed, [wts[n].shape for n in rest])))

    return (loss, grad_x[None], *[grads[n] for n in WEIGHT_NAMES], *[delta[n] for n in WEIGHT_NAMES],
            *[new_m[n] for n in WEIGHT_NAMES], *[new_v[n] for n in WEIGHT_NAMES])
```

```python
import functools

import jax
import jax.numpy as jnp
from jax import lax
from jax.experimental import pallas as pl
from jax.experimental.pallas import tpu as pltpu

F32, BF16 = jnp.float32, jnp.bfloat16
MESH = pl.DeviceIdType.MESH

D_MODEL = 1024
S5_WIDTH = 512
S5_GROUP = 16
S5_GROUPS = 32
S5_STATE = 64
SGU_WIDTH = 512
SGU_GROUPS = 8
CHUNK = 128
FFN_HIDDEN = 2816
GRID_W = 64
N_MOD = 6
EPS = 1e-6
N_STATE = S5_GROUPS * S5_STATE
OCTETS = 4
SCAN_T = 128
N_CHIPS = 4
N_DEV = 8
LANES = 128
VMEM_LIMIT_BYTES = 56 * 1024 * 1024
CONV_PAD = 72
CONV_ROWS = 256

ADAM_LR, ADAM_B1, ADAM_B2, ADAM_EPS, ADAM_WD, ADAM_STEP = 0.001, 0.9, 0.999, 1e-08, 0.01, 10


def _call(body, name, out_shape, grid=None, in_specs=None, out_specs=None, scratch=(), sem=None, **kw):
    params = pltpu.CompilerParams(dimension_semantics=sem, vmem_limit_bytes=VMEM_LIMIT_BYTES)
    extra = {} if grid is None else {"grid": grid}
    return pl.pallas_call(body, name=name, out_shape=out_shape, in_specs=in_specs, out_specs=out_specs,
                          scratch_shapes=list(scratch), compiler_params=params, **extra, **kw)


def _tile(n, target, mult=LANES):
    best = None
    t = mult
    while t <= min(n, target):
        if n % t == 0:
            best = t
        t += mult
    return best or n


@jax.custom_vjp
def mmul(a, b):
    return jnp.dot(a.astype(BF16), b.astype(BF16), preferred_element_type=F32)


def _mmul_fwd(a, b):
    return mmul(a, b), (a, b)


def _mmul_bwd(res, ct):
    a, b = res
    ctb = ct.astype(BF16)
    da = lax.dot_general(ctb, b.astype(BF16), (((1,), (1,)), ((), ())), preferred_element_type=F32)
    db = lax.dot_general(a.astype(BF16), ctb, (((0,), (0,)), ((), ())), preferred_element_type=F32)
    return da.astype(a.dtype), db.astype(b.dtype)


mmul.defvjp(_mmul_fwd, _mmul_bwd)

_DOT_DIMS = {"nn": ((1,), (0,)), "nt": ((1,), (1,)), "tn": ((0,), (0,))}


MM_TILE = 1408
MM_FULL_K = 2048


def matmul(a, b, mode, out_dtype, name, init=None, shards=False):
    if mode == "nn":
        (M, K), N = a.shape, (b.shape[2] * N_CHIPS if shards else b.shape[1])
    elif mode == "nt":
        M, N, K = a.shape[0], b.shape[-2], a.shape[1]
    else:
        (K, M), N = a.shape, b.shape[1]
    ns = (K if mode == "nt" else N) // N_CHIPS
    tm = _tile(M, MM_TILE, 8 if M < LANES else LANES)
    tn = _tile(ns if shards and mode != "nt" else N, MM_TILE)
    if shards and mode == "nt":
        tk = _tile(ns, MM_TILE)
    else:
        tk = K if K <= MM_FULL_K else _tile(K, MM_TILE)
    nk = K // tk
    per = ns // (tk if mode == "nt" else tn)
    dims = (_DOT_DIMS[mode], ((), ()))
    has_init = init is not None
    use_acc = nk > 1 and out_dtype != F32

    def body(*refs):
        a_ref, b_ref = refs[:2]
        i_ref = refs[2] if has_init else None
        o_ref = refs[3] if has_init else refs[2]
        acc = refs[-1] if use_acc else o_ref
        k = pl.program_id(2)
        part = lax.dot_general(a_ref[...].astype(BF16), b_ref[...].astype(BF16), dims, preferred_element_type=F32)

        @pl.when(k == 0)
        def _():
            first = part + i_ref[...].astype(F32) if has_init else part
            acc[...] = first.astype(acc.dtype)

        if nk > 1:
            @pl.when(k > 0)
            def _():
                acc[...] += part

        if use_acc:
            @pl.when(k == nk - 1)
            def _():
                o_ref[...] = acc[...].astype(o_ref.dtype)

    if mode == "tn":
        a_spec = pl.BlockSpec((tk, tm), lambda i, j, k: (k, i))
    else:
        a_spec = pl.BlockSpec((tm, tk), lambda i, j, k: (i, k))
    if mode == "nt":
        b_spec = (pl.BlockSpec((None, tn, tk), lambda i, j, k: (k // per, j, k % per)) if shards
                  else pl.BlockSpec((tn, tk), lambda i, j, k: (j, k)))
    else:
        b_spec = (pl.BlockSpec((None, tk, tn), lambda i, j, k: (j // per, k, j % per)) if shards and mode == "nn"
                  else pl.BlockSpec((tk, tn), lambda i, j, k: (k, j)))
    if shards and mode == "tn":
        o_spec = pl.BlockSpec((None, tm, tn), lambda i, j, k: (j // per, i, j % per))
        out_shape = jax.ShapeDtypeStruct((N_CHIPS, M, ns), out_dtype)
    else:
        o_spec = pl.BlockSpec((tm, tn), lambda i, j, k: (i, j))
        out_shape = jax.ShapeDtypeStruct((M, N), out_dtype)
    in_specs = [a_spec, b_spec] + ([o_spec] if has_init else [])
    args = (a, b) + ((init,) if has_init else ())
    return _call(body, name, out_shape, grid=(M // tm, N // tn, nk),
                 in_specs=in_specs, out_specs=o_spec, scratch=[pltpu.VMEM((tm, tn), F32)] if use_acc else [],
                 sem=("parallel", "parallel", "arbitrary"))(*args)


def rowcall(fn, name, nrows, tm, rins, vins, routs, aouts, ainit=None):
    n_r, n_v, n_ro = len(rins), len(vins), len(routs)
    n_i = len(aouts) if ainit is not None else 0

    def body(*refs):
        r_in, v_in, i_in = refs[:n_r], refs[n_r:n_r + n_v], refs[n_r + n_v:n_r + n_v + n_i]
        r_out, a_out = refs[n_r + n_v + n_i:n_r + n_v + n_i + n_ro], refs[n_r + n_v + n_i + n_ro:]
        outs = fn(*[r[...].astype(F32) for r in r_in], *[v[...] for v in v_in])
        if not isinstance(outs, (tuple, list)):
            outs = (outs,)
        for ref, val in zip(r_out, outs[:n_ro]):
            ref[...] = val.astype(ref.dtype)
        if a_out:
            @pl.when(pl.program_id(0) == 0)
            def _():
                for k, ref in enumerate(a_out):
                    ref[...] = i_in[k][...] if n_i else jnp.zeros_like(ref)

            for ref, val in zip(a_out, outs[n_ro:]):
                ref[...] += val.astype(F32)

    def rspec(width, cblk, roff):
        return pl.BlockSpec((tm, width), lambda i: (i + roff, cblk))

    def whole(shape):
        nd = len(shape)
        return pl.BlockSpec(tuple(shape), lambda i: (0,) * nd)

    inits = list(ainit) if n_i else []
    in_specs = [rspec(w, cb, ro) for (_, w, cb, ro) in rins] + [whole(v.shape) for v in vins + inits]
    out_specs = [rspec(w, 0, 0) for (w, _) in routs] + [whole(s) for s in aouts]
    out_shape = [jax.ShapeDtypeStruct((nrows, w), dt) for (w, dt) in routs] + \
                [jax.ShapeDtypeStruct(tuple(s), F32) for s in aouts]
    res = _call(body, name, out_shape, grid=(nrows // tm,), in_specs=in_specs, out_specs=out_specs,
                sem=("arbitrary",))(*[r[0] for r in rins], *vins, *inits)
    return res


def _rms(x):
    return lax.rsqrt(jnp.mean(x * x, axis=-1, keepdims=True) + EPS)


def f_modulate(x, g, sc, sh):
    return (x * _rms(x)) * g * (1.0 + sc) + sh


def f_resid_mod(x, o, ga, g, sc, sh):
    x1 = x + ga * o
    return x1, f_modulate(x1, g, sc, sh)


def f_final_loss(x1, dn, tgt, ga2, gf):
    x2 = x1 + ga2 * dn
    y = (x2 * _rms(x2)) * gf
    err = jnp.square(y - tgt)
    return 0.5 * jnp.sum(jnp.mean(err, axis=-1))


def _sgu_spatial(vn, w, bt):
    lo = lax.broadcasted_iota(jnp.int32, (1, LANES), 1) < (SGU_WIDTH // SGU_GROUPS)
    row_blocks = []
    for r in range(vn.shape[0] // CHUNK):
        rows = vn[r * CHUNK:(r + 1) * CHUNK]
        cols = []
        for j in range(SGU_WIDTH // LANES):
            blk = rows[:, j * LANES:(j + 1) * LANES]
            v_lo = jnp.where(lo, blk, 0.0)
            v_hi = jnp.where(lo, 0.0, blk)
            s = mmul(w[2 * j], v_lo) + mmul(w[2 * j + 1], v_hi)
            bias = jnp.where(lo, bt[:, 2 * j:2 * j + 1], bt[:, 2 * j + 1:2 * j + 2])
            cols.append(s + bias)
        row_blocks.append(jnp.concatenate(cols, axis=1))
    return jnp.concatenate(row_blocks, axis=0) if len(row_blocks) > 1 else row_blocks[0]


def f_mixer(u_a, y0, y1, zu, zv, ga0, ga1, gb0, gb1, d_skip, w_glu, b_glu, ln_g, ln_b, sgu_w, sgu_bt,
            w_pa, w_pb, b_gate):
    ys = u_a * d_skip + y0 + y1
    ge = jax.nn.gelu(ys)
    y_a = ge * jax.nn.sigmoid(mmul(ge, w_glu) + b_glu)
    u_sg = jax.nn.gelu(zu)
    v = jax.nn.gelu(zv)
    vc = v - jnp.mean(v, axis=-1, keepdims=True)
    vn = (vc * lax.rsqrt(jnp.mean(vc * vc, axis=-1, keepdims=True) + EPS)) * ln_g + ln_b
    y_b = u_sg * _sgu_spatial(vn, sgu_w, sgu_bt)
    gl_a = jnp.concatenate([ga0, ga1], axis=1) + b_gate[:, :D_MODEL]
    gl_b = jnp.concatenate([gb0, gb1], axis=1) + b_gate[:, D_MODEL:]
    return jax.nn.sigmoid(gl_a) * mmul(y_a, w_pa) + jax.nn.sigmoid(gl_b) * mmul(y_b, w_pb)


def _cmul(ar, ai, xr, xi):
    return ar * xr - ai * xi, ar * xi + ai * xr


SUB = 8
STRAND = SCAN_T // SUB


def _strand_perms():
    i = lax.broadcasted_iota(jnp.int32, (SCAN_T, SCAN_T), 0)
    j = lax.broadcasted_iota(jnp.int32, (SCAN_T, SCAN_T), 1)
    to_strands = jnp.where(j == STRAND * (i % SUB) + i // SUB, 1.0, 0.0).astype(BF16)
    to_tokens = jnp.where(i == STRAND * (j % SUB) + j // SUB, 1.0, 0.0).astype(BF16)
    return to_strands, to_tokens


def _permute(perm, v):
    return jnp.dot(perm, v, preferred_element_type=F32).astype(BF16)


def _scan_strands(xr, xi, pw_ref, q_ref, col, rev, conj, cr, ci):
    def tab(ref, lo):
        t_r = ref[lo:lo + SUB, col:col + LANES]
        t_i = ref[lo:lo + SUB, col + 512:col + 512 + LANES]
        return t_r, (-t_i if conj else t_i)

    a_r, a_i = tab(pw_ref, (STRAND - 1) * SUB if rev else 0)
    order = list(range(STRAND - 1, -1, -1) if rev else range(STRAND))
    lr, li = [None] * STRAND, [None] * STRAND
    for n, k in enumerate(order):
        lr[k], li[k] = xr[k * SUB:(k + 1) * SUB], xi[k * SUB:(k + 1) * SUB]
        if n:
            m_r, m_i = _cmul(a_r, a_i, lr[order[n - 1]], li[order[n - 1]])
            lr[k], li[k] = lr[k] + m_r, li[k] + m_i
    f_r, f_i = lr[order[-1]], li[order[-1]]
    q_r, q_i = tab(q_ref, 0)
    sub = lax.broadcasted_iota(jnp.int32, (SUB, 1), 0)
    s = 1
    while s < SUB:
        row = (SUB - s) if rev else (s - 1)
        shift = (SUB - s) if rev else s
        m = (sub < SUB - s) if rev else (sub >= s)
        p_r, p_i = _cmul(q_r[row:row + 1], q_i[row:row + 1], pltpu.roll(f_r, shift, 0), pltpu.roll(f_i, shift, 0))
        f_r, f_i = f_r + jnp.where(m, p_r, 0.0), f_i + jnp.where(m, p_i, 0.0)
        s *= 2
    c_r, c_i = jnp.broadcast_to(cr, (SUB, LANES)), jnp.broadcast_to(ci, (SUB, LANES))
    p_r, p_i = _cmul(q_r, q_i, c_r, c_i)
    s_r, s_i = f_r + p_r, f_i + p_i
    edge = 0 if rev else SUB - 1
    first = sub == (SUB - 1 if rev else 0)
    e_r = jnp.where(first, c_r, pltpu.roll(s_r, SUB - 1 if rev else 1, 0))
    e_i = jnp.where(first, c_i, pltpu.roll(s_i, SUB - 1 if rev else 1, 0))
    for k in range(STRAND):
        t_r, t_i = tab(pw_ref, k * SUB)
        p_r, p_i = _cmul(t_r, t_i, e_r, e_i)
        lr[k], li[k] = lr[k] + p_r, li[k] + p_i
    return lr, li, (s_r[edge:edge + 1], s_i[edge:edge + 1]), (e_r, e_i)


def s5_forward(u3, bm, cm, pw, q, rev, n_chunks, blk0, name):
    T = SCAN_T

    def pos(i):
        return (n_chunks - 1 - i) if rev else i

    def body(u_ref, bm_ref, cm_ref, pw_ref, q_ref, y_ref, cin_ref, carry):
        @pl.when(pl.program_id(0) == 0)
        def _():
            carry[...] = jnp.zeros_like(carry)

        cin_ref[...] = carry[...]
        to_strands, to_tokens = _strand_perms()
        u = _permute(to_strands, u_ref[...])
        for o in range(OCTETS):
            bu = jnp.dot(u[:, o * LANES:(o + 1) * LANES], bm_ref[o], preferred_element_type=F32)
            hr, hi = [], []
            for j in range(4):
                col = o * 1024 + j * LANES
                xr, xi, (cr, ci), _ = _scan_strands(
                    bu[:, j * LANES:(j + 1) * LANES], bu[:, 512 + j * LANES:512 + (j + 1) * LANES], pw_ref, q_ref, col,
                    rev, False, cin_ref[0:1, col:col + LANES], cin_ref[0:1, col + 512:col + 512 + LANES])
                carry[0:1, col:col + LANES] = cr
                carry[0:1, col + 512:col + 512 + LANES] = ci
                hr.append(jnp.concatenate(xr, axis=0))
                hi.append(jnp.concatenate(xi, axis=0))
            h = _permute(to_tokens, jnp.concatenate(hr + hi, axis=1).astype(BF16))
            y_ref[:, o * LANES:(o + 1) * LANES] = jnp.dot(h, cm_ref[o], preferred_element_type=F32)

    whole3 = lambda s: pl.BlockSpec(s, lambda i: (0, 0, 0))
    whole2 = lambda s: pl.BlockSpec(s, lambda i: (0, 0))
    return _call(
        body, name,
        [jax.ShapeDtypeStruct((n_chunks * T, S5_WIDTH), F32), jax.ShapeDtypeStruct((n_chunks, 1, 2 * N_STATE), F32)],
        grid=(n_chunks,),
        in_specs=[pl.BlockSpec((T, S5_WIDTH), lambda i: (blk0 + pos(i), 0)), whole3(bm.shape), whole3(cm.shape),
                  whole2(pw.shape), whole2(q.shape)],
        out_specs=[pl.BlockSpec((T, S5_WIDTH), lambda i: (pos(i), 0)),
                   pl.BlockSpec((None, 1, 2 * N_STATE), lambda i: (pos(i), 0, 0))],
        scratch=[pltpu.VMEM((1, 2 * N_STATE), F32)], sem=("arbitrary",))(u3, bm, cm, pw, q)


def s5_backward(u3, dy3, cin, bm, cm, pw_h, q_h, pw_l, q_l, rev, n_chunks, blk0, name):
    T = SCAN_T

    def pos(i):
        return i if rev else (n_chunks - 1 - i)

    def body(u_ref, dy_ref, cin_ref, bm_ref, cm_ref, pwh_ref, qh_ref, pwl_ref, ql_ref, du_ref, dbm_ref, dcm_ref,
             da_ref, lcarry):
        @pl.when(pl.program_id(0) == 0)
        def _():
            lcarry[...] = jnp.zeros_like(lcarry)
            dbm_ref[...] = jnp.zeros_like(dbm_ref)
            dcm_ref[...] = jnp.zeros_like(dcm_ref)
            da_ref[...] = jnp.zeros_like(da_ref)

        to_strands, to_tokens = _strand_perms()
        u = _permute(to_strands, u_ref[...])
        dy = _permute(to_strands, dy_ref[...])
        for o in range(OCTETS):
            u_o = u[:, o * LANES:(o + 1) * LANES]
            dy_o = dy[:, o * LANES:(o + 1) * LANES]
            bu = jnp.dot(u_o, bm_ref[o], preferred_element_type=F32)
            g = lax.dot_general(dy_o, cm_ref[o], (((1,), (1,)), ((), ())), preferred_element_type=F32)
            hr, hi, lr, li = [], [], [], []
            for j in range(4):
                col = o * 1024 + j * LANES
                sl_r = slice(j * LANES, (j + 1) * LANES)
                sl_i = slice(512 + j * LANES, 512 + (j + 1) * LANES)
                xr, xi, _, (e_r, e_i) = _scan_strands(
                    bu[:, sl_r], bu[:, sl_i], pwh_ref, qh_ref, col, rev, False,
                    cin_ref[0:1, col:col + LANES], cin_ref[0:1, col + 512:col + 512 + LANES])
                ar_, ai_, (l_r, l_i), _ = _scan_strands(
                    g[:, sl_r], g[:, sl_i], pwl_ref, ql_ref, col, not rev, True,
                    lcarry[0:1, col:col + LANES], lcarry[0:1, col + 512:col + 512 + LANES])
                lcarry[0:1, col:col + LANES] = l_r
                lcarry[0:1, col + 512:col + 512 + LANES] = l_i
                acc_r = acc_i = None
                for k in range(STRAND):
                    kp = k + 1 if rev else k - 1
                    p_r, p_i = (e_r, e_i) if not 0 <= kp < STRAND else (xr[kp], xi[kp])
                    t_r = ar_[k] * p_r + ai_[k] * p_i
                    t_i = ai_[k] * p_r - ar_[k] * p_i
                    acc_r, acc_i = (t_r, t_i) if acc_r is None else (acc_r + t_r, acc_i + t_i)
                da_ref[0:1, col:col + LANES] += jnp.sum(acc_r, axis=0, keepdims=True)
                da_ref[0:1, col + 512:col + 512 + LANES] += jnp.sum(acc_i, axis=0, keepdims=True)
                hr.append(jnp.concatenate(xr, axis=0))
                hi.append(jnp.concatenate(xi, axis=0))
                lr.append(jnp.concatenate(ar_, axis=0))
                li.append(jnp.concatenate(ai_, axis=0))
            h = jnp.concatenate(hr + hi, axis=1).astype(BF16)
            lam = jnp.concatenate(lr + li, axis=1).astype(BF16)
            du_ref[:, o * LANES:(o + 1) * LANES] = lax.dot_general(
                _permute(to_tokens, lam), bm_ref[o], (((1,), (1,)), ((), ())), preferred_element_type=F32)
            dbm_ref[o] += lax.dot_general(u_o, lam, (((0,), (0,)), ((), ())), preferred_element_type=F32)
            dcm_ref[o] += lax.dot_general(h, dy_o, (((0,), (0,)), ((), ())), preferred_element_type=F32)

    whole3 = lambda s: pl.BlockSpec(s, lambda i: (0, 0, 0))
    whole2 = lambda s: pl.BlockSpec(s, lambda i: (0, 0))
    row_spec = pl.BlockSpec((T, S5_WIDTH), lambda i: (blk0 + pos(i), 0))
    return _call(
        body, name,
        [jax.ShapeDtypeStruct((n_chunks * T, S5_WIDTH), F32), jax.ShapeDtypeStruct(bm.shape, F32),
         jax.ShapeDtypeStruct(cm.shape, F32), jax.ShapeDtypeStruct((1, 2 * N_STATE), F32)],
        grid=(n_chunks,),
        in_specs=[row_spec, row_spec, pl.BlockSpec((None, 1, 2 * N_STATE), lambda i: (pos(i), 0, 0)),
                  whole3(bm.shape), whole3(cm.shape), whole2(pw_h.shape), whole2(q_h.shape), whole2(pw_l.shape),
                  whole2(q_l.shape)],
        out_specs=[pl.BlockSpec((T, S5_WIDTH), lambda i: (pos(i), 0)), whole3(bm.shape), whole3(cm.shape),
                   whole2((1, 2 * N_STATE))],
        scratch=[pltpu.VMEM((1, 2 * N_STATE), F32)], sem=("arbitrary",))(u3, dy3, cin, bm, cm, pw_h, q_h, pw_l, q_l)


def s5_tables(ar, ai, ls):
    def body(ar_ref, ai_ref, ls_ref, pr_ref, pi_ref, qr_ref, qi_ref):
        dt = jnp.exp(ls_ref[...])
        for n_rows, step, o_r, o_i in ((STRAND, 1.0, pr_ref, pi_ref), (SUB, float(STRAND), qr_ref, qi_ref)):
            m = (lax.broadcasted_iota(jnp.int32, (n_rows, 1), 0) + 1).astype(F32) * step
            mag = jnp.exp(m * (ar_ref[...] * dt))
            ang = m * (ai_ref[...] * dt)
            o_r[...] = mag * jnp.cos(ang)
            o_i[...] = mag * jnp.sin(ang)

    vec = pl.BlockSpec((None, 1, N_STATE), lambda d: (d, 0, 0))
    tab = lambda n: pl.BlockSpec((None, n, N_STATE), lambda d: (d, 0, 0))
    shp = lambda n: jax.ShapeDtypeStruct((2, n, N_STATE), F32)
    return _call(body, "s5_tables", [shp(STRAND), shp(STRAND), shp(SUB), shp(SUB)], grid=(2,),
                 in_specs=[vec, vec, vec], out_specs=[tab(STRAND), tab(STRAND), tab(SUB), tab(SUB)],
                 sem=("arbitrary",))(ar, ai, ls)


def f_discretize(a_re, a_im, ls, b_re, b_im):
    dt = jnp.exp(ls)
    mag = jnp.exp(a_re * dt)
    ab_re = mag * jnp.cos(a_im * dt)
    ab_im = mag * jnp.sin(a_im * dt)
    p = ab_re - 1.0
    q = ab_im
    den = a_re * a_re + a_im * a_im
    k_re = ((p * a_re + q * a_im) / den)[None]
    k_im = ((q * a_re - p * a_im) / den)[None]
    return ab_re, ab_im, k_re * b_re - k_im * b_im, k_re * b_im + k_im * b_re


def _disc_specs():
    a = pl.BlockSpec((None, S5_GROUPS, S5_STATE), lambda d: (d, 0, 0))
    s = pl.BlockSpec((None, S5_GROUPS, 1), lambda d: (d, 0, 0))
    b = pl.BlockSpec((None, S5_GROUP, S5_GROUPS, S5_STATE), lambda d: (d, 0, 0, 0))
    return a, s, b


def s5_discretize(a_re, a_im, ls, b_re, b_im):
    def body(ar, ai, l, br, bi, obr, obi):
        _, _, r, i = f_discretize(ar[...], ai[...], l[...], br[...], bi[...])
        obr[...] = r
        obi[...] = i

    a, s, b = _disc_specs()
    return _call(body, "s5_discretize", [jax.ShapeDtypeStruct(b_re.shape, F32)] * 2, grid=(2,),
                 in_specs=[a, a, s, b, b], out_specs=[b, b], sem=("arbitrary",))(a_re, a_im, ls, b_re, b_im)


def s5_discretize_bwd(a_re, a_im, ls, b_re, b_im, dab_re, dab_im, dbb_re, dbb_im):
    def body(ar, ai, l, br, bi, c0, c1, c2, c3, o0, o1, o2, o3, o4):
        _, vjp = jax.vjp(f_discretize, ar[...], ai[...], l[...], br[...], bi[...])
        outs = vjp((c0[...], c1[...], c2[...], c3[...]))
        for ref, val in zip((o0, o1, o2, o3, o4), outs):
            ref[...] = val

    a, s, b = _disc_specs()
    shapes = [jax.ShapeDtypeStruct(t.shape, F32) for t in (a_re, a_im, ls, b_re, b_im)]
    return _call(body, "s5_discretize_bwd", shapes, grid=(2,), in_specs=[a, a, s, b, b, a, a, b, b],
                 out_specs=[a, a, s, b, b], sem=("arbitrary",))(a_re, a_im, ls, b_re, b_im, dab_re, dab_im,
                                                                 dbb_re, dbb_im)


def _conv_taps(s_ref, base, n_rows):
    n = n_rows + 2 * CONV_PAD
    ext = s_ref[pl.ds(base, n), :]
    col = (lax.broadcasted_iota(jnp.int32, (n, 1), 0) + (2 * GRID_W - CONV_PAD)) % GRID_W
    left = jnp.where(col == 0, 0.0, pltpu.roll(ext, 1, 0))
    right = jnp.where(col == GRID_W - 1, 0.0, pltpu.roll(ext, n - 1, 0))
    return left, ext, right


def _conv_apply(taps, w_ref, n_rows, flip):
    out = None
    for i in range(3):
        wi = 2 - i if flip else i
        comb = None
        for j in range(3):
            wj = 2 - j if flip else j
            term = w_ref[wi * 3 + wj:wi * 3 + wj + 1, :] * taps[j]
            comb = term if comb is None else comb + term
        start = CONV_PAD + (i - 1) * GRID_W
        part = comb[start:start + n_rows]
        out = part if out is None else out + part
    return out


def _conv_fill(dst_ref, src_ref, n_tok):
    zeros = jnp.zeros((CONV_PAD, LANES), F32)
    dst_ref[0:CONV_PAD, :] = zeros
    dst_ref[CONV_PAD + n_tok:2 * CONV_PAD + n_tok, :] = zeros

    def step(r, carry):
        base = pl.multiple_of(r * CONV_ROWS, CONV_ROWS)
        dst_ref[pl.ds(base + CONV_PAD, CONV_ROWS), :] = src_ref[pl.ds(base, CONV_ROWS), :].astype(F32)
        return carry

    lax.fori_loop(0, n_tok // CONV_ROWS, step, 0)


def conv_forward(up, wg, wv, bias):
    n_tok = up.shape[0]
    nb = FFN_HIDDEN // LANES

    def body(ug_ref, uv_ref, wg_ref, wv_ref, bg_ref, bv_ref, act_ref, gate_ref, val_ref, sg, sv):
        _conv_fill(sg, ug_ref, n_tok)
        _conv_fill(sv, uv_ref, n_tok)

        def step(r, carry):
            base = pl.multiple_of(r * CONV_ROWS, CONV_ROWS)
            gate = _conv_apply(_conv_taps(sg, base, CONV_ROWS), wg_ref, CONV_ROWS, False) + bg_ref[...]
            val = _conv_apply(_conv_taps(sv, base, CONV_ROWS), wv_ref, CONV_ROWS, False) + bv_ref[...]
            act_ref[pl.ds(base, CONV_ROWS), :] = (gate * jax.nn.sigmoid(gate) * val).astype(BF16)
            gate_ref[pl.ds(base, CONV_ROWS), :] = gate.astype(BF16)
            val_ref[pl.ds(base, CONV_ROWS), :] = val.astype(BF16)
            return carry

        lax.fori_loop(0, n_tok // CONV_ROWS, step, 0)

    col = lambda off: pl.BlockSpec((n_tok, LANES), lambda k: (0, k + off))
    wsp = lambda off: pl.BlockSpec((16, LANES), lambda k: (0, k + off))
    bsp = lambda off: pl.BlockSpec((1, LANES), lambda k: (0, k + off))
    pad = pltpu.VMEM((n_tok + 2 * CONV_PAD, LANES), F32)
    half = jax.ShapeDtypeStruct((n_tok, FFN_HIDDEN), BF16)
    return _call(body, "conv_forward", [half, half, half], grid=(nb,),
                 in_specs=[col(0), col(nb), wsp(0), wsp(0), bsp(0), bsp(nb)], out_specs=[col(0), col(0), col(0)],
                 scratch=[pad, pad], sem=("arbitrary",))(up, up, wg, wv, bias, bias)


def conv_backward(up, gate, val, dact, wg, wv):
    n_tok = up.shape[0]
    nb = FFN_HIDDEN // LANES
    n_steps = n_tok // CONV_ROWS

    def body(ug_ref, uv_ref, gate_ref, val_ref, da_ref, wg_ref, wv_ref, dug_ref, duv_ref, dwg_ref, dwv_ref,
             sg, sv, sdg, sdv):
        _conv_fill(sg, ug_ref, n_tok)
        _conv_fill(sv, uv_ref, n_tok)
        zeros = jnp.zeros((CONV_PAD, LANES), F32)
        for s_ref in (sdg, sdv):
            s_ref[0:CONV_PAD, :] = zeros
            s_ref[CONV_PAD + n_tok:2 * CONV_PAD + n_tok, :] = zeros
        dwg_ref[...] = jnp.zeros_like(dwg_ref)
        dwv_ref[...] = jnp.zeros_like(dwv_ref)

        def grads(r, carry):
            base = pl.multiple_of(r * CONV_ROWS, CONV_ROWS)
            taps_g = _conv_taps(sg, base, CONV_ROWS)
            taps_v = _conv_taps(sv, base, CONV_ROWS)
            gate = gate_ref[pl.ds(base, CONV_ROWS), :].astype(F32)
            val = val_ref[pl.ds(base, CONV_ROWS), :].astype(F32)
            d_act = da_ref[pl.ds(base, CONV_ROWS), :].astype(F32)
            sig = jax.nn.sigmoid(gate)
            d_gate = d_act * val * (sig * (1.0 + gate * (1.0 - sig)))
            d_val = d_act * (gate * sig)
            sdg[pl.ds(base + CONV_PAD, CONV_ROWS), :] = d_gate
            sdv[pl.ds(base + CONV_PAD, CONV_ROWS), :] = d_val
            for d_out, taps, dw_ref in ((d_gate, taps_g, dwg_ref), (d_val, taps_v, dwv_ref)):
                for i in range(3):
                    start = CONV_PAD + (i - 1) * GRID_W
                    for j in range(3):
                        k = i * 3 + j
                        dw_ref[k:k + 1, :] += jnp.sum(d_out * taps[j][start:start + CONV_ROWS], axis=0, keepdims=True)
                dw_ref[9:10, :] += jnp.sum(d_out, axis=0, keepdims=True)
            return carry

        lax.fori_loop(0, n_steps, grads, 0)

        def spread(r, carry):
            base = pl.multiple_of(r * CONV_ROWS, CONV_ROWS)
            dug_ref[pl.ds(base, CONV_ROWS), :] = _conv_apply(
                _conv_taps(sdg, base, CONV_ROWS), wg_ref, CONV_ROWS, True).astype(BF16)
            duv_ref[pl.ds(base, CONV_ROWS), :] = _conv_apply(
                _conv_taps(sdv, base, CONV_ROWS), wv_ref, CONV_ROWS, True).astype(BF16)
            return carry

        lax.fori_loop(0, n_steps, spread, 0)

    col = lambda off: pl.BlockSpec((n_tok, LANES), lambda k: (0, k + off))
    wsp = pl.BlockSpec((16, LANES), lambda k: (0, k))
    pad = pltpu.VMEM((n_tok + 2 * CONV_PAD, LANES), F32)
    half = jax.ShapeDtypeStruct((n_tok, FFN_HIDDEN), BF16)
    dw = jax.ShapeDtypeStruct((16, FFN_HIDDEN), F32)
    return _call(body, "conv_backward", [half, half, dw, dw], grid=(nb,),
                 in_specs=[col(0), col(nb), col(0), col(0), col(0), wsp, wsp],
                 out_specs=[col(0), col(0), wsp, wsp], scratch=[pad, pad, pad, pad],
                 sem=("arbitrary",))(up, up, gate, val, dact, wg, wv)


def f_adamw(w, g, m, v):
    m = ADAM_B1 * m + (1.0 - ADAM_B1) * g
    v = ADAM_B2 * v + (1.0 - ADAM_B2) * jnp.square(g)
    m_hat = m / (1.0 - ADAM_B1 ** ADAM_STEP)
    v_hat = v / (1.0 - ADAM_B2 ** ADAM_STEP)
    delta = -ADAM_LR * (m_hat / (jnp.sqrt(v_hat) + ADAM_EPS) + ADAM_WD * w)
    return delta, m, v


def adamw(w, g, m, v, name):
    shape = w.shape
    cols = shape[-1]
    rows = w.size // cols
    two_d = [t.reshape(rows, cols) for t in (w, g, m, v)]
    tm = _tile(rows, 256, 8) if rows % 8 == 0 else rows
    outs = rowcall(f_adamw, name, rows, tm, [(t, cols, 0, 0) for t in two_d], [], [(cols, F32)] * 3, [])
    return tuple(o.reshape(shape) for o in outs)


def _place():
    return lax.axis_index("x"), lax.axis_index("y"), lax.axis_index("c")


_ANY = pl.BlockSpec(memory_space=pl.ANY)


def allgather_devices(v, name):
    def body(v_ref, out_ref, send_sems, recv_sems, local_sem):
        x, y, c = _place()
        me, sibling = (x, y, c), (x, y, 1 - c)
        chips = [(1 - x, y), (x, 1 - y), (1 - x, 1 - y)]

        def slot(p):
            return out_ref.at[4 * p[0] + 2 * p[1] + p[2]]

        def copy(k, block, to, src=None):
            return pltpu.make_async_remote_copy(
                src_ref=slot(block) if src is None else src, dst_ref=slot(block),
                send_sem=send_sems.at[k], recv_sem=recv_sems.at[k], device_id=to, device_id_type=MESH)

        mine = pltpu.make_async_copy(v_ref, slot(me), local_sem)
        mine.start()
        first = [copy(0, me, sibling, src=v_ref)]
        first += [copy(1 + j, me, (*chip, c), src=v_ref) for j, chip in enumerate(chips)]
        for cp in first:
            cp.start()
        passed = [copy(4 + j, (*chip, c), sibling) for j, chip in enumerate(chips)]
        for j, chip in enumerate(chips):
            copy(1 + j, (*chip, c), me).wait_recv()
            passed[j].start()
        copy(0, sibling, me).wait_recv()
        for j, chip in enumerate(chips):
            copy(4 + j, (*chip, 1 - c), me).wait_recv()
        for cp in first + passed:
            cp.wait_send()
        mine.wait()

    return _call(body, name, jax.ShapeDtypeStruct((N_DEV,) + v.shape, v.dtype), in_specs=[_ANY], out_specs=_ANY,
                 scratch=[pltpu.SemaphoreType.DMA((7,)), pltpu.SemaphoreType.DMA((7,)), pltpu.SemaphoreType.DMA])(v)


def _other_chips(x, y):
    return [(1 - x, y), (x, 1 - y), (1 - x, 1 - y)]


def allgather_chips(vs, name):
    n = len(vs)
    halves = [v.shape[0] // 2 for v in vs]

    def body(*refs):
        v_refs, o_refs = refs[:n], refs[n:2 * n]
        send_sems, recv_sems, local_sems = refs[2 * n:]
        x, y, c = _place()
        sibling = (x, y, 1 - c)
        chips = _other_chips(x, y)

        def rows(a, chip, h):
            return o_refs[a].at[2 * chip[0] + chip[1], pl.ds(h * halves[a], halves[a])]

        def copy(a, k, chip, h, to, src=None):
            return pltpu.make_async_remote_copy(
                src_ref=rows(a, chip, h) if src is None else src, dst_ref=rows(a, chip, h),
                send_sem=send_sems.at[6 * a + k], recv_sem=recv_sems.at[6 * a + k], device_id=to, device_id_type=MESH)

        mine = [pltpu.make_async_copy(v_refs[a], o_refs[a].at[2 * x + y], local_sems.at[a]) for a in range(n)]
        first = [copy(a, j, (x, y), c, (*chip, c), src=v_refs[a].at[pl.ds(c * halves[a], halves[a])])
                 for a in range(n) for j, chip in enumerate(chips)]
        for cp in mine + first:
            cp.start()
        passed = []
        for j, chip in enumerate(chips):
            for a in range(n):
                copy(a, j, chip, c, (x, y, c)).wait_recv()
                passed.append(copy(a, 3 + j, chip, c, sibling))
                passed[-1].start()
        for j, chip in enumerate(chips):
            for a in range(n):
                copy(a, 3 + j, chip, 1 - c, (x, y, c)).wait_recv()
        for cp in first + passed:
            cp.wait_send()
        for cp in mine:
            cp.wait()

    return _call(body, name, [jax.ShapeDtypeStruct((N_CHIPS,) + v.shape, v.dtype) for v in vs], in_specs=[_ANY] * n,
                 out_specs=[_ANY] * n, scratch=[pltpu.SemaphoreType.DMA((6 * n,)), pltpu.SemaphoreType.DMA((6 * n,)),
                                                pltpu.SemaphoreType.DMA((n,))])(*vs)


def grad_pair_swap(gs, name):
    n = len(gs)

    def body(*refs):
        g_refs, o_refs, send_sems, recv_sems = refs[:n], refs[n:2 * n], refs[2 * n], refs[2 * n + 1]
        x, y, c = _place()
        cps = []
        for a in range(n):
            half = gs[a].shape[1] // 2
            cps.append(pltpu.make_async_remote_copy(
                src_ref=g_refs[a].at[:, pl.ds((1 - c) * half, half)], dst_ref=o_refs[a], send_sem=send_sems.at[a],
                recv_sem=recv_sems.at[a], device_id=(x, y, 1 - c), device_id_type=MESH))
        for cp in cps:
            cp.start()
        for cp in cps:
            cp.wait()

    return _call(body, name, [jax.ShapeDtypeStruct((N_CHIPS, g.shape[1] // 2, g.shape[2]), g.dtype) for g in gs],
                 in_specs=[_ANY] * n, out_specs=[_ANY] * n,
                 scratch=[pltpu.SemaphoreType.DMA((n,)), pltpu.SemaphoreType.DMA((n,))])(*gs)


def pair_sum(g, recv, core, name):
    r2, cols = recv.shape[1], recv.shape[2]
    tr = _tile(r2, 256, 8)
    nt = r2 // tr

    def body(c_ref, g_ref, r_ref, o_ref):
        o_ref[...] = g_ref[...] + r_ref[...]

    spec = pl.BlockSpec((None, tr, cols), lambda q, i, c_ref: (q, i, 0))
    grid_spec = pltpu.PrefetchScalarGridSpec(
        num_scalar_prefetch=1, grid=(N_CHIPS, nt),
        in_specs=[pl.BlockSpec((None, tr, cols), lambda q, i, c_ref: (q, c_ref[0] * nt + i, 0)), spec], out_specs=spec)
    return pl.pallas_call(body, name=name, out_shape=jax.ShapeDtypeStruct(recv.shape, F32), grid_spec=grid_spec,
                          compiler_params=pltpu.CompilerParams(dimension_semantics=("arbitrary", "arbitrary"),
                                                               vmem_limit_bytes=VMEM_LIMIT_BYTES))(core, g, recv)


def grad_chip_exchange(ps, name):
    n = len(ps)

    def body(*refs):
        p_refs, o_refs = refs[:n], refs[n:2 * n]
        send_sems, recv_sems, local_sems = refs[2 * n:]
        x, y, c = _place()
        chips = _other_chips(x, y)
        me = 2 * x + y
        mine = [pltpu.make_async_copy(p_refs[a].at[me], o_refs[a].at[me], local_sems.at[a]) for a in range(n)]
        sends = [pltpu.make_async_remote_copy(
            src_ref=p_refs[a].at[2 * chip[0] + chip[1]], dst_ref=o_refs[a].at[me], send_sem=send_sems.at[3 * a + j],
            recv_sem=recv_sems.at[3 * a + j], device_id=(*chip, c), device_id_type=MESH)
            for a in range(n) for j, chip in enumerate(chips)]
        for cp in mine + sends:
            cp.start()
        for a in range(n):
            for j, chip in enumerate(chips):
                pltpu.make_async_remote_copy(
                    src_ref=p_refs[a].at[me], dst_ref=o_refs[a].at[2 * chip[0] + chip[1]],
                    send_sem=send_sems.at[3 * a + j], recv_sem=recv_sems.at[3 * a + j], device_id=(*chip, c),
                    device_id_type=MESH).wait_recv()
        for cp in sends:
            cp.wait_send()
        for cp in mine:
            cp.wait()

    return _call(body, name, [jax.ShapeDtypeStruct(p.shape, p.dtype) for p in ps], in_specs=[_ANY] * n,
                 out_specs=[_ANY] * n, scratch=[pltpu.SemaphoreType.DMA((3 * n,)), pltpu.SemaphoreType.DMA((3 * n,)),
                                                pltpu.SemaphoreType.DMA((n,))])(*ps)


def grad_half_swap(ss, name):
    n = len(ss)

    def body(*refs):
        s_refs, o_refs = refs[:n], refs[n:2 * n]
        send_sems, recv_sems, local_sems = refs[2 * n:]
        x, y, c = _place()
        mine, sends = [], []
        for a in range(n):
            r2 = ss[a].shape[0]
            mine.append(pltpu.make_async_copy(s_refs[a], o_refs[a].at[pl.ds(c * r2, r2)], local_sems.at[a]))
            sends.append(pltpu.make_async_remote_copy(
                src_ref=s_refs[a], dst_ref=o_refs[a].at[pl.ds(c * r2, r2)], send_sem=send_sems.at[a],
                recv_sem=recv_sems.at[a], device_id=(x, y, 1 - c), device_id_type=MESH))
        for cp in mine + sends:
            cp.start()
        for a in range(n):
            r2 = ss[a].shape[0]
            pltpu.make_async_remote_copy(
                src_ref=s_refs[a], dst_ref=o_refs[a].at[pl.ds((1 - c) * r2, r2)], send_sem=send_sems.at[a],
                recv_sem=recv_sems.at[a], device_id=(x, y, 1 - c), device_id_type=MESH).wait_recv()
        for cp in sends:
            cp.wait_send()
        for cp in mine:
            cp.wait()

    return _call(body, name, [jax.ShapeDtypeStruct((2 * s.shape[0], s.shape[1]), s.dtype) for s in ss],
                 in_specs=[_ANY] * n, out_specs=[_ANY] * n,
                 scratch=[pltpu.SemaphoreType.DMA((n,)), pltpu.SemaphoreType.DMA((n,)), pltpu.SemaphoreType.DMA((n,))])(*ss)


ROW_TILE = 256

BIG_SHARDS = (("w_in", (D_MODEL, 896), 1), ("s5_w_glu", (128, S5_WIDTH), 0), ("w_proj_a", (S5_WIDTH, 256), 1),
              ("w_proj_b", (SGU_WIDTH, 256), 1), ("w_out", (256, D_MODEL), 0), ("w_up", (D_MODEL, 1408), 1),
              ("w_down", (704, D_MODEL), 0))
SMALL_PARAMS = ("g_mix", "s5_a_re", "s5_a_im", "s5_log_step", "s5_b_re", "s5_b_im", "s5_c_re", "s5_c_im", "s5_d",
                "s5_b_glu", "sgu_ln_g", "sgu_ln_b", "sgu_w", "sgu_b", "b_gate", "g_ffn", "conv_b", "g_final")
PACK_COLS = 1024


def _rows_of(n):
    return -(-n // PACK_COLS)


def _pack_rows(arrays, total_rows, dtype):
    parts = []
    used = 0
    for a in arrays:
        r = _rows_of(a.size)
        parts.append(jnp.pad(a.reshape(-1).astype(dtype), (0, r * PACK_COLS - a.size)).reshape(r, PACK_COLS))
        used += r
    if total_rows > used:
        parts.append(jnp.zeros((total_rows - used, PACK_COLS), dtype))
    return jnp.concatenate(parts, axis=0)


def _unpack_rows(packed, shapes, row0=0):
    out = []
    for s in shapes:
        n = 1
        for d in s:
            n *= d
        r = _rows_of(n)
        out.append(packed[row0:row0 + r].reshape(-1)[:n].reshape(s))
        row0 += r
    return out


def _octet_major(re, im):
    parts = []
    for o in range(OCTETS):
        parts += [re[:, o * 512:(o + 1) * 512], im[:, o * 512:(o + 1) * 512]]
    return jnp.concatenate(parts, axis=1)


def _octet_split(v):
    v = v.reshape(OCTETS, 2, 512)
    return v[:, 0].reshape(N_STATE), v[:, 1].reshape(N_STATE)


def _s5_bmat(bb):
    t = bb.reshape(S5_GROUP, OCTETS, 8, 1, S5_STATE) * jnp.eye(8, dtype=F32)[None, None, :, :, None]
    return jnp.transpose(t, (1, 2, 0, 3, 4)).reshape(OCTETS, LANES, 512)


def _s5_bmat_t(dm):
    t = dm.reshape(OCTETS, 8, S5_GROUP, 8, S5_STATE) * jnp.eye(8, dtype=F32)[None, :, None, :, None]
    return jnp.transpose(t.sum(axis=3), (2, 0, 1, 3)).reshape(S5_GROUP, S5_GROUPS, S5_STATE)


def _s5_cmat(c):
    t = c.reshape(OCTETS, 8, 1, S5_GROUP, S5_STATE) * jnp.eye(8, dtype=F32)[None, :, :, None, None]
    return jnp.transpose(t, (0, 2, 4, 1, 3)).reshape(OCTETS, 512, LANES)


def _s5_cmat_t(dm):
    t = dm.reshape(OCTETS, 8, S5_STATE, 8, S5_GROUP) * jnp.eye(8, dtype=F32)[None, :, None, :, None]
    return jnp.transpose(t.sum(axis=1), (0, 2, 3, 1)).reshape(S5_GROUPS, S5_GROUP, S5_STATE)


def local_step(x, ctx, tgt, mod, modc, W):
    n_tok, n_ctx = x.shape[0], ctx.shape[0]
    tm = ROW_TILE
    D = D_MODEL
    sh1, sc1, ga1, sh2, sc2, ga2 = [mod[:, k * D:(k + 1) * D] for k in range(N_MOD)]
    sh1c, sc1c = modc[:, :D], modc[:, D:2 * D]
    g_mix, g_ffn, g_final = W["g_mix"], W["g_ffn"], W["g_final"]
    w_in = W["w_in"]
    w_in_u = w_in[0][:, :S5_WIDTH]

    h = rowcall(f_modulate, "mod1", n_tok, tm, [(x, D, 0, 0)], [g_mix, sc1, sh1], [(D, BF16)], [])[0]
    hc = rowcall(f_modulate, "mod1_ctx", n_ctx, tm, [(ctx, D, 0, 0)], [g_mix, sc1c, sh1c], [(D, BF16)], [])[0]
    proj = matmul(h, w_in, "nn", BF16, "proj_in", shards=True)
    uc = matmul(hc, w_in_u, "nn", BF16, "proj_in_ctx")
    u3 = jnp.concatenate([uc, proj[:, :S5_WIDTH], uc], axis=0)

    a_re, a_im, ls = W["s5_a_re"], W["s5_a_im"], W["s5_log_step"][..., None]
    b_re_t = jnp.transpose(W["s5_b_re"], (0, 3, 1, 2))
    b_im_t = jnp.transpose(W["s5_b_im"], (0, 3, 1, 2))
    bb_re, bb_im = s5_discretize(a_re, a_im, ls, b_re_t, b_im_t)
    ls_rep = jnp.repeat(W["s5_log_step"], S5_STATE, axis=1).reshape(2, 1, N_STATE)
    p_re, p_im, q_re, q_im = s5_tables(a_re.reshape(2, 1, N_STATE), a_im.reshape(2, 1, N_STATE), ls_rep)
    n_chunks = (n_ctx + n_tok) // SCAN_T
    ctx_blk = n_ctx // SCAN_T
    pw, qt, bm, cm = [], [], [], []
    for d in range(2):
        p16 = _octet_major(p_re[d], p_im[d])
        q8 = _octet_major(q_re[d], q_im[d])
        pw.append((jnp.repeat(p16, SUB, axis=0), jnp.repeat(p16[::-1], SUB, axis=0)))
        qt.append((q8, q8[::-1]))
        bm.append(jnp.concatenate([_s5_bmat(bb_re[d]), _s5_bmat(bb_im[d])], axis=2).astype(BF16))
        cm.append(jnp.concatenate([_s5_cmat(W["s5_c_re"][d]), -_s5_cmat(W["s5_c_im"][d])], axis=1).astype(BF16))
    y0, cin0 = s5_forward(u3, bm[0], cm[0], pw[0][0], qt[0][0], False, n_chunks, 0, "s5_fwd0")
    y1, cin1 = s5_forward(u3, bm[1], cm[1], pw[1][1], qt[1][1], True, n_chunks, ctx_blk, "s5_fwd1")

    mix_rows = [(proj, 512, 0, 0), (y0, 512, 0, n_ctx // tm), (y1, 512, 0, 0)] + \
               [(proj, 512, k, 0) for k in range(1, 7)]
    mix_vecs = [W["s5_d"], W["s5_w_glu"], W["s5_b_glu"], W["sgu_ln_g"], W["sgu_ln_b"], W["sgu_w"],
                jnp.transpose(W["sgu_b"]), W["w_proj_a"], W["w_proj_b"], W["b_gate"]]
    mrg = rowcall(f_mixer, "mixer", n_tok, tm, mix_rows, mix_vecs, [(D, BF16)], [])[0]
    o = matmul(mrg, W["w_out"], "nn", F32, "proj_out")
    x1, h2 = rowcall(f_resid_mod, "resid_mod2", n_tok, tm, [(x, D, 0, 0), (o, D, 0, 0)], [ga1, g_ffn, sc2, sh2],
                     [(D, F32), (D, BF16)], [])
    up = matmul(h2, W["w_up"], "nn", BF16, "ffn_up", shards=True)
    conv_w = W["conv_w"].reshape(9, 2 * FFN_HIDDEN)
    wg = jnp.pad(conv_w[:, :FFN_HIDDEN], ((0, 7), (0, 0)))
    wv = jnp.pad(conv_w[:, FFN_HIDDEN:], ((0, 7), (0, 0)))
    act, gate, val = conv_forward(up, wg, wv, W["conv_b"])
    dn = matmul(act, W["w_down"], "nn", F32, "ffn_down")

    def final_fn(x1_, dn_, tgt_, ga2_, gf_):
        loss, (dx1_, ddn_, dga2_, dgf_) = jax.value_and_grad(f_final_loss, argnums=(0, 1, 3, 4))(
            x1_, dn_, tgt_, ga2_, gf_)
        return dx1_, ddn_, loss.reshape(1, 1), dga2_, dgf_

    dx2, ddn, loss, d_ga2, d_gfinal = rowcall(
        final_fn, "final_loss", n_tok, tm, [(x1, D, 0, 0), (dn, D, 0, 0), (tgt, D, 0, 0)], [ga2, g_final],
        [(D, F32), (D, BF16)], [(1, 1), (1, D), (1, D)])

    dact = matmul(ddn, W["w_down"], "nt", BF16, "ffn_down_dx")
    d_w_down = matmul(act, ddn, "tn", F32, "ffn_down_dw")
    dup_g, dup_v, dwg, dwv = conv_backward(up, gate, val, dact, wg, wv)
    dup = jnp.concatenate([dup_g, dup_v], axis=1)
    d_conv_w = jnp.concatenate([dwg[:9], dwv[:9]], axis=1).reshape(3, 3, 2 * FFN_HIDDEN)
    d_conv_b = jnp.concatenate([dwg[9:10], dwv[9:10]], axis=1)
    dh2 = matmul(dup, W["w_up"], "nt", BF16, "ffn_up_dx", shards=True)
    d_w_up = matmul(h2, dup, "tn", F32, "ffn_up_dw", shards=True)

    def resid_bwd(x_, o_, dx1_, dh2_, ga_, g_, sc_, sh_):
        _, vjp = jax.vjp(f_resid_mod, x_, o_, ga_, g_, sc_, sh_)
        return vjp((dx1_, dh2_))

    dxa, do, d_ga1, d_gffn, d_sc2, d_sh2 = rowcall(
        resid_bwd, "resid_mod2_bwd", n_tok, tm, [(x, D, 0, 0), (o, D, 0, 0), (dx2, D, 0, 0), (dh2, D, 0, 0)],
        [ga1, g_ffn, sc2, sh2], [(D, F32), (D, BF16)], [(1, D)] * 4)

    dmrg = matmul(do, W["w_out"], "nt", BF16, "proj_out_dx")
    d_w_out = matmul(mrg, do, "tn", F32, "proj_out_dw")

    def mixer_bwd(*args):
        rows, dm, vecs = args[:9], args[9], [v.astype(F32) for v in args[10:]]
        _, vjp = jax.vjp(f_mixer, *rows, *vecs)
        g = vjp(dm)
        return (g[0], g[1], jnp.concatenate([g[3], g[4]], axis=1), jnp.concatenate(g[5:9], axis=1)) + tuple(g[9:])

    mb = rowcall(mixer_bwd, "mixer_bwd", n_tok, tm, mix_rows + [(dmrg, D, 0, 0)], mix_vecs,
                 [(512, BF16), (512, BF16), (1024, BF16), (2048, BF16)], [v.shape for v in mix_vecs])
    du_direct, dys, dzb, dgl = mb[:4]
    d_s5d, d_w_glu, d_b_glu, d_ln_g, d_ln_b, d_sgu_w, d_sgu_bt, d_w_pa, d_w_pb, d_b_gate = mb[4:]

    zc = jnp.zeros((n_ctx, S5_WIDTH), BF16)
    dy3 = jnp.concatenate([zc, dys, zc], axis=0)
    du0, dbm0, dcm0, da0 = s5_backward(u3, dy3, cin0, bm[0], cm[0], pw[0][0], qt[0][0], pw[0][1], qt[0][1], False,
                                       n_chunks, 0, "s5_bwd0")
    du1, dbm1, dcm1, da1 = s5_backward(u3, dy3, cin1, bm[1], cm[1], pw[1][1], qt[1][1], pw[1][0], qt[1][0], True,
                                       n_chunks, ctx_blk, "s5_bwd1")
    add3 = lambda a, b, c: a + b + c
    du_a = rowcall(add3, "du_sum", n_tok, tm, [(du_direct, 512, 0, 0), (du0, 512, 0, n_ctx // tm), (du1, 512, 0, 0)],
                   [], [(512, BF16)], [])[0]
    du_c = rowcall(lambda a, b: a + b, "du_sum_ctx", n_ctx, tm, [(du0, 512, 0, 0), (du1, 512, 0, n_tok // tm)],
                   [], [(512, BF16)], [])[0]

    dab_re, dab_im, dbb_re, dbb_im, d_c_re, d_c_im = [], [], [], [], [], []
    for dbm, dcm, da in ((dbm0, dcm0, da0), (dbm1, dcm1, da1)):
        r, i = _octet_split(da)
        dab_re.append(r.reshape(S5_GROUPS, S5_STATE))
        dab_im.append(i.reshape(S5_GROUPS, S5_STATE))
        dbb_re.append(_s5_bmat_t(dbm[:, :, :512]))
        dbb_im.append(_s5_bmat_t(dbm[:, :, 512:]))
        d_c_re.append(_s5_cmat_t(dcm[:, :512]))
        d_c_im.append(-_s5_cmat_t(dcm[:, 512:]))
    d_a_re, d_a_im, d_ls, d_b_re_t, d_b_im_t = s5_discretize_bwd(
        a_re, a_im, ls, b_re_t, b_im_t, jnp.stack(dab_re), jnp.stack(dab_im), jnp.stack(dbb_re), jnp.stack(dbb_im))

    dproj = jnp.concatenate([du_a, dzb, dgl], axis=1)
    dh = matmul(dproj, w_in, "nt", BF16, "proj_in_dx", shards=True)
    dhc = matmul(du_c, w_in_u, "nt", BF16, "proj_in_ctx_dx")
    d_w_in_c = matmul(hc, du_c, "tn", F32, "proj_in_ctx_dw")
    d_w_in = matmul(h, dproj, "tn", F32, "proj_in_dw", shards=True,
                    init=jnp.pad(d_w_in_c[None], ((0, N_CHIPS - 1), (0, 0), (0, w_in.shape[2] - S5_WIDTH))))

    def mod_bwd_ctx(x_, dh_, g_, sc_, sh_):
        _, vjp = jax.vjp(f_modulate, x_, g_, sc_, sh_)
        return vjp(dh_)[1:]

    d_gmix_c, d_sc1c, d_sh1c = rowcall(mod_bwd_ctx, "mod1_ctx_bwd", n_ctx, tm, [(ctx, D, 0, 0), (dhc, D, 0, 0)],
                                       [g_mix, sc1c, sh1c], [], [(1, D)] * 3)

    def mod_bwd(x_, dh_, dxa_, g_, sc_, sh_):
        _, vjp = jax.vjp(f_modulate, x_, g_, sc_, sh_)
        dx_, dg_, dsc_, dsh_ = vjp(dh_)
        return dx_ + dxa_, dg_, dsc_, dsh_

    zero_d = jnp.zeros((1, D), F32)
    grad_x, d_gmix, d_sc1, d_sh1 = rowcall(
        mod_bwd, "mod1_bwd", n_tok, tm, [(x, D, 0, 0), (dh, D, 0, 0), (dxa, D, 0, 0)], [g_mix, sc1, sh1],
        [(D, F32)], [(1, D)] * 3, ainit=[d_gmix_c, zero_d, zero_d])

    grads = {
        "dmod": jnp.concatenate([d_sh1, d_sc1, d_ga1, d_sh2, d_sc2, d_ga2], axis=1),
        "dmodc": jnp.concatenate([d_sh1c, d_sc1c], axis=1),
        "g_mix": d_gmix,
        "s5_a_re": d_a_re, "s5_a_im": d_a_im, "s5_log_step": d_ls[..., 0],
        "s5_b_re": jnp.transpose(d_b_re_t, (0, 2, 3, 1)), "s5_b_im": jnp.transpose(d_b_im_t, (0, 2, 3, 1)),
        "s5_c_re": jnp.stack(d_c_re), "s5_c_im": jnp.stack(d_c_im), "s5_d": d_s5d, "s5_b_glu": d_b_glu,
        "sgu_ln_g": d_ln_g, "sgu_ln_b": d_ln_b, "sgu_w": d_sgu_w, "sgu_b": jnp.transpose(d_sgu_bt),
        "b_gate": d_b_gate, "g_ffn": d_gffn, "conv_b": d_conv_b, "g_final": d_gfinal, "conv_w": d_conv_w,
        "w_in": d_w_in, "s5_w_glu": d_w_glu, "w_proj_a": d_w_pa, "w_proj_b": d_w_pb, "w_out": d_w_out,
        "w_up": d_w_up, "w_down": d_w_down,
    }
    return loss, grad_x, grads


ADA_COLS = N_MOD * D_MODEL // N_CHIPS
MOD_ROWS = 16


def mod_forward(c16, w, b):
    n = w.shape[1]
    tn = 512

    def body(c_ref, w_ref, b_ref, o_ref):
        cv = c_ref[...]
        cs = cv * jax.nn.sigmoid(cv)
        o_ref[...] = jnp.dot(cs.astype(BF16), w_ref[...].astype(BF16), preferred_element_type=F32) + b_ref[...]

    return _call(body, "mod_forward", jax.ShapeDtypeStruct((MOD_ROWS, n), F32), grid=(n // tn,),
                 in_specs=[pl.BlockSpec((MOD_ROWS, D_MODEL), lambda j: (0, 0)),
                           pl.BlockSpec((D_MODEL, tn), lambda j: (0, j)), pl.BlockSpec((1, tn), lambda j: (0, j))],
                 out_specs=pl.BlockSpec((MOD_ROWS, tn), lambda j: (0, j)), sem=("arbitrary",))(c16, w, b)


def f_ada_outer(ct, dm):
    cs = ct * jax.nn.sigmoid(ct)
    acc = cs[:, 0:1] * dm[0:1]
    for k in range(1, 9):
        acc = acc + cs[:, k:k + 1] * dm[k:k + 1]
    return acc


def f_cctx_grad(z, p4):
    s = jax.nn.sigmoid(z)
    return (p4[0:1] + p4[1:2] + p4[2:3] + p4[3:4]) * (s + z * s * (1.0 - s))


WEIGHT_NAMES = ("c_ctx", "w_ada", "b_ada", "g_mix", "w_in", "s5_a_re", "s5_a_im", "s5_log_step", "s5_b_re",
                "s5_b_im", "s5_c_re", "s5_c_im", "s5_d", "s5_w_glu", "s5_b_glu", "sgu_ln_g", "sgu_ln_b", "sgu_w",
                "sgu_b", "w_proj_a", "w_proj_b", "b_gate", "w_out", "g_ffn", "w_up", "conv_w", "conv_b", "w_down",
                "g_final")
CONV_SHARD = 2 * FFN_HIDDEN // N_CHIPS
SMALL_PACK_ROWS = 512
SMALL_ROW0 = 58
ADAM_PACK_ROWS = 512


def kernel(x, c, ctx, c_ctx, w_ada, b_ada, g_mix, w_in, s5_a_re, s5_a_im, s5_log_step, s5_b_re, s5_b_im, s5_c_re, s5_c_im, s5_d, s5_w_glu, s5_b_glu, sgu_ln_g, sgu_ln_b, sgu_w, sgu_b, w_proj_a, w_proj_b, b_gate, w_out, g_ffn, w_up, conv_w, conv_b, w_down, g_final, loss_target, m_c_ctx, m_w_ada, m_b_ada, m_g_mix, m_w_in, m_s5_a_re, m_s5_a_im, m_s5_log_step, m_s5_b_re, m_s5_b_im, m_s5_c_re, m_s5_c_im, m_s5_d, m_s5_w_glu, m_s5_b_glu, m_sgu_ln_g, m_sgu_ln_b, m_sgu_w, m_sgu_b, m_w_proj_a, m_w_proj_b, m_b_gate, m_w_out, m_g_ffn, m_w_up, m_conv_w, m_conv_b, m_w_down, m_g_final, v_c_ctx, v_w_ada, v_b_ada, v_g_mix, v_w_in, v_s5_a_re, v_s5_a_im, v_s5_log_step, v_s5_b_re, v_s5_b_im, v_s5_c_re, v_s5_c_im, v_s5_d, v_s5_w_glu, v_s5_b_glu, v_sgu_ln_g, v_sgu_ln_b, v_sgu_w, v_sgu_b, v_w_proj_a, v_w_proj_b, v_b_gate, v_w_out, v_g_ffn, v_w_up, v_conv_w, v_conv_b, v_w_down, v_g_final):
    given = dict(locals())
    wts = {n: given[n] for n in WEIGHT_NAMES}
    ms = {n: given["m_" + n] for n in WEIGHT_NAMES}
    vs = {n: given["v_" + n] for n in WEIGHT_NAMES}
    xi, yi, ci = _place()
    chip = 2 * xi + yi
    dev = 2 * chip + ci
    D = D_MODEL

    c8 = allgather_devices(jnp.pad(c, ((0, 7), (0, 0))), "gather_c")[:, 0, :]
    c16 = jnp.concatenate([c8, c_ctx[None], jnp.zeros((MOD_ROWS - 9, D), F32)], axis=0)
    b_shard = lax.dynamic_slice(b_ada, (0, chip * ADA_COLS), (1, ADA_COLS))
    mod_shard = mod_forward(c16, w_ada[0], b_shard)
    mod_all = allgather_devices(mod_shard, "gather_mod")
    mod_full = jnp.concatenate([mod_all[2 * q] for q in range(N_CHIPS)], axis=1)
    mod = lax.dynamic_slice(mod_full, (dev, 0), (1, N_MOD * D))
    modc = mod_full[8:9]

    big_names = [n for n, _, _ in BIG_SHARDS]
    conv_rows = jnp.pad(conv_w[0].reshape(9, CONV_SHARD), ((0, 7), (0, 0)))
    gathered = allgather_chips([wts[n][0].astype(BF16) for n in big_names] + [conv_rows], "gather_weights")
    W = dict(zip(big_names, gathered[:-1]))
    for n, shape, axis in BIG_SHARDS:
        if axis == 0:
            W[n] = W[n].reshape(N_CHIPS * shape[0], shape[1])
    for n in ("w_proj_a", "w_proj_b"):
        W[n] = jnp.transpose(W[n], (1, 0, 2)).reshape(W[n].shape[1], -1)
    W["conv_w"] = jnp.transpose(gathered[-1][:, :9], (1, 0, 2)).reshape(3, 3, 2 * FFN_HIDDEN)
    for n in ("g_mix", "g_ffn", "s5_d", "s5_b_glu", "sgu_ln_g", "sgu_ln_b", "b_gate", "conv_b"):
        W[n] = wts[n]
    W["g_final"] = g_final[None]
    for n in ("s5_a_re", "s5_a_im", "s5_log_step", "s5_b_re", "s5_b_im", "s5_c_re", "s5_c_im", "sgu_w", "sgu_b"):
        W[n] = wts[n][0]

    loss_part, grad_x, g = local_step(x[0], ctx[0], loss_target[0], mod, modc, W)
    loss = lax.psum(loss_part[0, 0], ("x", "y", "c"))

    g_slots = []
    for n, shape, axis in BIG_SHARDS:
        if n in ("w_proj_a", "w_proj_b"):
            g_slots.append(jnp.transpose(g[n].reshape(shape[0], N_CHIPS, shape[1]), (1, 0, 2)))
        else:
            g_slots.append(g[n].reshape((N_CHIPS,) + shape))
    core = ci.astype(jnp.int32).reshape(1)
    from_sibling = grad_pair_swap(g_slots, "grad_pair_swap")
    pair = [pair_sum(gs, rv, core, "grad_pair_sum_" + n) for gs, rv, n in zip(g_slots, from_sibling, big_names)]
    from_chips = grad_chip_exchange(pair, "grad_chip_exchange")
    add2 = lambda a, b: a + b
    add4 = lambda a, b, c_, d: ((a + b) + c_) + d
    halves = []
    for fc, n in zip(from_chips, big_names):
        r2, cols = fc.shape[1], fc.shape[2]
        tr = _tile(r2, 256, 8)
        halves.append(rowcall(add4, "grad_chip_sum_" + n, r2, tr,
                              [(fc.reshape(N_CHIPS * r2, cols), cols, 0, q * r2 // tr) for q in range(N_CHIPS)], [],
                              [(cols, F32)], [])[0])
    big_grads = dict(zip(big_names, grad_half_swap(halves, "grad_half_swap")))

    small_pack = _pack_rows([g["dmod"], g["dmodc"], g["conv_w"]] + [g[n] for n in SMALL_PARAMS], SMALL_PACK_ROWS, F32)
    small_all = allgather_devices(small_pack, "gather_small_grads")
    small_2d = small_all.reshape(N_DEV * SMALL_PACK_ROWS, PACK_COLS)

    def add8(*a):
        s = a[0]
        for t in a[1:]:
            s = s + t
        return s

    small_sum = rowcall(add8, "small_grad_sum", SMALL_PACK_ROWS, 256,
                        [(small_2d, PACK_COLS, 0, k * SMALL_PACK_ROWS // 256) for k in range(N_DEV)], [],
                        [(PACK_COLS, F32)], [])[0]
    dmod_all = small_all[:, 0:N_MOD].reshape(N_DEV, N_MOD * D)
    dmod_sum = small_sum[0:N_MOD].reshape(1, N_MOD * D)
    dmodc_sum = jnp.pad(small_sum[N_MOD:N_MOD + 2].reshape(1, 2 * D), ((0, 0), (0, (N_MOD - 2) * D)))
    conv_grad = _unpack_rows(small_sum, [(3, 3, 2 * FFN_HIDDEN)], row0=N_MOD + 2)[0]
    small_grads = dict(zip(SMALL_PARAMS, _unpack_rows(small_sum, [wts[n].shape for n in SMALL_PARAMS], row0=SMALL_ROW0)))

    dm16 = jnp.concatenate([dmod_all, dmodc_sum, jnp.zeros((MOD_ROWS - 9, N_MOD * D), F32)], axis=0)
    dm_shard = lax.dynamic_slice(dm16, (0, chip * ADA_COLS), (MOD_ROWS, ADA_COLS))
    g_w_ada = rowcall(f_ada_outer, "w_ada_grad", D, 256, [(jnp.transpose(c16), MOD_ROWS, 0, 0)], [dm_shard],
                      [(ADA_COLS, F32)], [])[0]
    g_b_ada = rowcall(add2, "b_ada_grad", 1, 1, [(dmod_sum, N_MOD * D, 0, 0), (dmodc_sum, N_MOD * D, 0, 0)], [],
                      [(N_MOD * D, F32)], [])[0]
    dmc_rows = jnp.pad(dm_shard[8:9], ((0, 7), (0, 0)))
    cctx_part = matmul(dmc_rows, w_ada[0], "nt", F32, "c_ctx_partial")
    cctx_all = allgather_devices(cctx_part, "gather_c_ctx")
    cctx_4 = jnp.stack([cctx_all[2 * q, 0] for q in range(N_CHIPS)])
    g_c_ctx = rowcall(f_cctx_grad, "c_ctx_grad", 1, 1, [(c_ctx[None], D, 0, 0)], [cctx_4], [(D, F32)], [])[0]

    grads = dict(small_grads)
    grads.update(big_grads)
    grads["w_ada"] = g_w_ada
    grads["b_ada"] = g_b_ada
    grads["c_ctx"] = g_c_ctx
    grads["conv_w"] = lax.dynamic_slice(conv_grad, (0, 0, chip * CONV_SHARD), (3, 3, CONV_SHARD))
    grads = {n: grads[n].reshape(wts[n].shape) for n in WEIGHT_NAMES}

    delta, new_m, new_v = {}, {}, {}
    large = [n for n, _, _ in BIG_SHARDS] + ["w_ada", "conv_w"]
    for n in large:
        shape2d = (-1, wts[n].shape[-1])
        d_, m_, v_ = adamw(wts[n].reshape(shape2d), grads[n].reshape(shape2d), ms[n].reshape(shape2d),
                           vs[n].reshape(shape2d), "adamw_" + n)
        delta[n], new_m[n], new_v[n] = [t.reshape(wts[n].shape) for t in (d_, m_, v_)]
    rest = [n for n in WEIGHT_NAMES if n not in large]
    packs = [_pack_rows([src[n] for n in rest], ADAM_PACK_ROWS, F32) for src in (wts, grads, ms, vs)]
    outs = adamw(*packs, "adamw_replicated")
    for dst, packed in zip((delta, new_m, new_v), outs):
        dst.update(zip(rest, _unpack_rows(packed, [wts[n].shape for n in rest])))

    return (loss, grad_x[None], *[grads[n] for n in WEIGHT_NAMES], *[delta[n] for n in WEIGHT_NAMES],
            *[new_m[n] for n in WEIGHT_NAMES], *[new_v[n] for n in WEIGHT_NAMES])
```

```python
import functools

import jax
import jax.numpy as jnp
from jax import lax
from jax.experimental import pallas as pl
from jax.experimental.pallas import tpu as pltpu

F32, BF16 = jnp.float32, jnp.bfloat16
MESH = pl.DeviceIdType.MESH

D_MODEL = 1024
S5_WIDTH = 512
S5_GROUP = 16
S5_GROUPS = 32
S5_STATE = 64
SGU_WIDTH = 512
SGU_GROUPS = 8
CHUNK = 128
FFN_HIDDEN = 2816
GRID_W = 64
N_MOD = 6
EPS = 1e-6
N_STATE = S5_GROUPS * S5_STATE
OCTETS = 4
SCAN_T = 128
N_CHIPS = 4
N_DEV = 8
LANES = 128
VMEM_LIMIT_BYTES = 56 * 1024 * 1024
CONV_PAD = 72
CONV_ROWS = 256

ADAM_LR, ADAM_B1, ADAM_B2, ADAM_EPS, ADAM_WD, ADAM_STEP = 0.001, 0.9, 0.999, 1e-08, 0.01, 10


def _call(body, name, out_shape, grid=None, in_specs=None, out_specs=None, scratch=(), sem=None, **kw):
    params = pltpu.CompilerParams(dimension_semantics=sem, vmem_limit_bytes=VMEM_LIMIT_BYTES)
    extra = {} if grid is None else {"grid": grid}
    return pl.pallas_call(body, name=name, out_shape=out_shape, in_specs=in_specs, out_specs=out_specs,
                          scratch_shapes=list(scratch), compiler_params=params, **extra, **kw)


def _tile(n, target, mult=LANES):
    best = None
    t = mult
    while t <= min(n, target):
        if n % t == 0:
            best = t
        t += mult
    return best or n


@jax.custom_vjp
def mmul(a, b):
    return jnp.dot(a.astype(BF16), b.astype(BF16), preferred_element_type=F32)


def _mmul_fwd(a, b):
    return mmul(a, b), (a, b)


def _mmul_bwd(res, ct):
    a, b = res
    ctb = ct.astype(BF16)
    da = lax.dot_general(ctb, b.astype(BF16), (((1,), (1,)), ((), ())), preferred_element_type=F32)
    db = lax.dot_general(a.astype(BF16), ctb, (((0,), (0,)), ((), ())), preferred_element_type=F32)
    return da.astype(a.dtype), db.astype(b.dtype)


mmul.defvjp(_mmul_fwd, _mmul_bwd)

_DOT_DIMS = {"nn": ((1,), (0,)), "nt": ((1,), (1,)), "tn": ((0,), (0,))}


MM_TILE = 1408
MM_FULL_K = 2048


def matmul(a, b, mode, out_dtype, name, init=None, shards=False):
    if mode == "nn":
        (M, K), N = a.shape, (b.shape[2] * N_CHIPS if shards else b.shape[1])
    elif mode == "nt":
        M, N, K = a.shape[0], b.shape[-2], a.shape[1]
    else:
        (K, M), N = a.shape, b.shape[1]
    ns = (K if mode == "nt" else N) // N_CHIPS
    tm = _tile(M, MM_TILE, 8 if M < LANES else LANES)
    tn = _tile(ns if shards and mode != "nt" else N, MM_TILE)
    if shards and mode == "nt":
        tk = _tile(ns, MM_TILE)
    else:
        tk = K if K <= MM_FULL_K else _tile(K, MM_TILE)
    nk = K // tk
    per = ns // (tk if mode == "nt" else tn)
    dims = (_DOT_DIMS[mode], ((), ()))
    has_init = init is not None
    use_acc = nk > 1 and out_dtype != F32

    def body(*refs):
        a_ref, b_ref = refs[:2]
        i_ref = refs[2] if has_init else None
        o_ref = refs[3] if has_init else refs[2]
        acc = refs[-1] if use_acc else o_ref
        k = pl.program_id(2)
        part = lax.dot_general(a_ref[...].astype(BF16), b_ref[...].astype(BF16), dims, preferred_element_type=F32)

        @pl.when(k == 0)
        def _():
            first = part + i_ref[...].astype(F32) if has_init else part
            acc[...] = first.astype(acc.dtype)

        if nk > 1:
            @pl.when(k > 0)
            def _():
                acc[...] += part

        if use_acc:
            @pl.when(k == nk - 1)
            def _():
                o_ref[...] = acc[...].astype(o_ref.dtype)

    if mode == "tn":
        a_spec = pl.BlockSpec((tk, tm), lambda i, j, k: (k, i))
    else:
        a_spec = pl.BlockSpec((tm, tk), lambda i, j, k: (i, k))
    if mode == "nt":
        b_spec = (pl.BlockSpec((None, tn, tk), lambda i, j, k: (k // per, j, k % per)) if shards
                  else pl.BlockSpec((tn, tk), lambda i, j, k: (j, k)))
    else:
        b_spec = (pl.BlockSpec((None, tk, tn), lambda i, j, k: (j // per, k, j % per)) if shards and mode == "nn"
                  else pl.BlockSpec((tk, tn), lambda i, j, k: (k, j)))
    if shards and mode == "tn":
        o_spec = pl.BlockSpec((None, tm, tn), lambda i, j, k: (j // per, i, j % per))
        out_shape = jax.ShapeDtypeStruct((N_CHIPS, M, ns), out_dtype)
    else:
        o_spec = pl.BlockSpec((tm, tn), lambda i, j, k: (i, j))
        out_shape = jax.ShapeDtypeStruct((M, N), out_dtype)
    in_specs = [a_spec, b_spec] + ([o_spec] if has_init else [])
    args = (a, b) + ((init,) if has_init else ())
    return _call(body, name, out_shape, grid=(M // tm, N // tn, nk),
                 in_specs=in_specs, out_specs=o_spec, scratch=[pltpu.VMEM((tm, tn), F32)] if use_acc else [],
                 sem=("parallel", "parallel", "arbitrary"))(*args)


def rowcall(fn, name, nrows, tm, rins, vins, routs, aouts, ainit=None):
    n_r, n_v, n_ro = len(rins), len(vins), len(routs)
    n_i = len(aouts) if ainit is not None else 0

    def body(*refs):
        r_in, v_in, i_in = refs[:n_r], refs[n_r:n_r + n_v], refs[n_r + n_v:n_r + n_v + n_i]
        r_out, a_out = refs[n_r + n_v + n_i:n_r + n_v + n_i + n_ro], refs[n_r + n_v + n_i + n_ro:]
        outs = fn(*[r[...].astype(F32) for r in r_in], *[v[...] for v in v_in])
        if not isinstance(outs, (tuple, list)):
            outs = (outs,)
        for ref, val in zip(r_out, outs[:n_ro]):
            ref[...] = val.astype(ref.dtype)
        if a_out:
            @pl.when(pl.program_id(0) == 0)
            def _():
                for k, ref in enumerate(a_out):
                    ref[...] = i_in[k][...] if n_i else jnp.zeros_like(ref)

            for ref, val in zip(a_out, outs[n_ro:]):
                ref[...] += val.astype(F32)

    def rspec(width, cblk, roff):
        return pl.BlockSpec((tm, width), lambda i: (i + roff, cblk))

    def whole(shape):
        nd = len(shape)
        return pl.BlockSpec(tuple(shape), lambda i: (0,) * nd)

    inits = list(ainit) if n_i else []
    in_specs = [rspec(w, cb, ro) for (_, w, cb, ro) in rins] + [whole(v.shape) for v in vins + inits]
    out_specs = [rspec(w, 0, 0) for (w, _) in routs] + [whole(s) for s in aouts]
    out_shape = [jax.ShapeDtypeStruct((nrows, w), dt) for (w, dt) in routs] + \
                [jax.ShapeDtypeStruct(tuple(s), F32) for s in aouts]
    res = _call(body, name, out_shape, grid=(nrows // tm,), in_specs=in_specs, out_specs=out_specs,
                sem=("arbitrary",))(*[r[0] for r in rins], *vins, *inits)
    return res


def _rms(x):
    return lax.rsqrt(jnp.mean(x * x, axis=-1, keepdims=True) + EPS)


def f_modulate(x, g, sc, sh):
    return (x * _rms(x)) * g * (1.0 + sc) + sh


def f_resid_mod(x, o, ga, g, sc, sh):
    x1 = x + ga * o
    return x1, f_modulate(x1, g, sc, sh)


def f_final_loss(x1, dn, tgt, ga2, gf):
    x2 = x1 + ga2 * dn
    y = (x2 * _rms(x2)) * gf
    err = jnp.square(y - tgt)
    return 0.5 * jnp.sum(jnp.mean(err, axis=-1))


def _sgu_spatial(vn, w, bt):
    lo = lax.broadcasted_iota(jnp.int32, (1, LANES), 1) < (SGU_WIDTH // SGU_GROUPS)
    row_blocks = []
    for r in range(vn.shape[0] // CHUNK):
        rows = vn[r * CHUNK:(r + 1) * CHUNK]
        cols = []
        for j in range(SGU_WIDTH // LANES):
            blk = rows[:, j * LANES:(j + 1) * LANES]
            v_lo = jnp.where(lo, blk, 0.0)
            v_hi = jnp.where(lo, 0.0, blk)
            s = mmul(w[2 * j], v_lo) + mmul(w[2 * j + 1], v_hi)
            bias = jnp.where(lo, bt[:, 2 * j:2 * j + 1], bt[:, 2 * j + 1:2 * j + 2])
            cols.append(s + bias)
        row_blocks.append(jnp.concatenate(cols, axis=1))
    return jnp.concatenate(row_blocks, axis=0) if len(row_blocks) > 1 else row_blocks[0]


def f_mixer(u_a, y0, y1, zu, zv, ga0, ga1, gb0, gb1, d_skip, w_glu, b_glu, ln_g, ln_b, sgu_w, sgu_bt,
            w_pa, w_pb, b_gate):
    ys = u_a * d_skip + y0 + y1
    ge = jax.nn.gelu(ys)
    y_a = ge * jax.nn.sigmoid(mmul(ge, w_glu) + b_glu)
    u_sg = jax.nn.gelu(zu)
    v = jax.nn.gelu(zv)
    vc = v - jnp.mean(v, axis=-1, keepdims=True)
    vn = (vc * lax.rsqrt(jnp.mean(vc * vc, axis=-1, keepdims=True) + EPS)) * ln_g + ln_b
    y_b = u_sg * _sgu_spatial(vn, sgu_w, sgu_bt)
    gl_a = jnp.concatenate([ga0, ga1], axis=1) + b_gate[:, :D_MODEL]
    gl_b = jnp.concatenate([gb0, gb1], axis=1) + b_gate[:, D_MODEL:]
    return jax.nn.sigmoid(gl_a) * mmul(y_a, w_pa) + jax.nn.sigmoid(gl_b) * mmul(y_b, w_pb)


def _cmul(ar, ai, xr, xi):
    return ar * xr - ai * xi, ar * xi + ai * xr


SUB = 8
STRAND = SCAN_T // SUB


def _strand_perms():
    i = lax.broadcasted_iota(jnp.int32, (SCAN_T, SCAN_T), 0)
    j = lax.broadcasted_iota(jnp.int32, (SCAN_T, SCAN_T), 1)
    to_strands = jnp.where(j == STRAND * (i % SUB) + i // SUB, 1.0, 0.0).astype(BF16)
    to_tokens = jnp.where(i == STRAND * (j % SUB) + j // SUB, 1.0, 0.0).astype(BF16)
    return to_strands, to_tokens


def _permute(perm, v):
    return jnp.dot(perm, v, preferred_element_type=F32).astype(BF16)


def _scan_strands(xr, xi, pw_ref, q_ref, col, rev, conj, cr, ci):
    def tab(ref, lo):
        t_r = ref[lo:lo + SUB, col:col + LANES]
        t_i = ref[lo:lo + SUB, col + 512:col + 512 + LANES]
        return t_r, (-t_i if conj else t_i)

    a_r, a_i = tab(pw_ref, (STRAND - 1) * SUB if rev else 0)
    order = list(range(STRAND - 1, -1, -1) if rev else range(STRAND))
    lr, li = [None] * STRAND, [None] * STRAND
    for n, k in enumerate(order):
        lr[k], li[k] = xr[k * SUB:(k + 1) * SUB], xi[k * SUB:(k + 1) * SUB]
        if n:
            m_r, m_i = _cmul(a_r, a_i, lr[order[n - 1]], li[order[n - 1]])
            lr[k], li[k] = lr[k] + m_r, li[k] + m_i
    f_r, f_i = lr[order[-1]], li[order[-1]]
    q_r, q_i = tab(q_ref, 0)
    sub = lax.broadcasted_iota(jnp.int32, (SUB, 1), 0)
    s = 1
    while s < SUB:
        row = (SUB - s) if rev else (s - 1)
        shift = (SUB - s) if rev else s
        m = (sub < SUB - s) if rev else (sub >= s)
        p_r, p_i = _cmul(q_r[row:row + 1], q_i[row:row + 1], pltpu.roll(f_r, shift, 0), pltpu.roll(f_i, shift, 0))
        f_r, f_i = f_r + jnp.where(m, p_r, 0.0), f_i + jnp.where(m, p_i, 0.0)
        s *= 2
    c_r, c_i = jnp.broadcast_to(cr, (SUB, LANES)), jnp.broadcast_to(ci, (SUB, LANES))
    p_r, p_i = _cmul(q_r, q_i, c_r, c_i)
    s_r, s_i = f_r + p_r, f_i + p_i
    edge = 0 if rev else SUB - 1
    first = sub == (SUB - 1 if rev else 0)
    e_r = jnp.where(first, c_r, pltpu.roll(s_r, SUB - 1 if rev else 1, 0))
    e_i = jnp.where(first, c_i, pltpu.roll(s_i, SUB - 1 if rev else 1, 0))
    for k in range(STRAND):
        t_r, t_i = tab(pw_ref, k * SUB)
        p_r, p_i = _cmul(t_r, t_i, e_r, e_i)
        lr[k], li[k] = lr[k] + p_r, li[k] + p_i
    return lr, li, (s_r[edge:edge + 1], s_i[edge:edge + 1]), (e_r, e_i)


def s5_forward(u3, bm, cm, pw, q, rev, n_chunks, blk0, name):
    T = SCAN_T

    def pos(i):
        return (n_chunks - 1 - i) if rev else i

    def body(u_ref, bm_ref, cm_ref, pw_ref, q_ref, y_ref, cin_ref, carry):
        @pl.when(pl.program_id(0) == 0)
        def _():
            carry[...] = jnp.zeros_like(carry)

        cin_ref[...] = carry[...]
        to_strands, to_tokens = _strand_perms()
        u = _permute(to_strands, u_ref[...])
        for o in range(OCTETS):
            bu = jnp.dot(u[:, o * LANES:(o + 1) * LANES], bm_ref[o], preferred_element_type=F32)
            hr, hi = [], []
            for j in range(4):
                col = o * 1024 + j * LANES
                xr, xi, (cr, ci), _ = _scan_strands(
                    bu[:, j * LANES:(j + 1) * LANES], bu[:, 512 + j * LANES:512 + (j + 1) * LANES], pw_ref, q_ref, col,
                    rev, False, cin_ref[0:1, col:col + LANES], cin_ref[0:1, col + 512:col + 512 + LANES])
                carry[0:1, col:col + LANES] = cr
                carry[0:1, col + 512:col + 512 + LANES] = ci
                hr.append(jnp.concatenate(xr, axis=0))
                hi.append(jnp.concatenate(xi, axis=0))
            h = _permute(to_tokens, jnp.concatenate(hr + hi, axis=1).astype(BF16))
            y_ref[:, o * LANES:(o + 1) * LANES] = jnp.dot(h, cm_ref[o], preferred_element_type=F32)

    whole3 = lambda s: pl.BlockSpec(s, lambda i: (0, 0, 0))
    whole2 = lambda s: pl.BlockSpec(s, lambda i: (0, 0))
    return _call(
        body, name,
        [jax.ShapeDtypeStruct((n_chunks * T, S5_WIDTH), F32), jax.ShapeDtypeStruct((n_chunks, 1, 2 * N_STATE), F32)],
        grid=(n_chunks,),
        in_specs=[pl.BlockSpec((T, S5_WIDTH), lambda i: (blk0 + pos(i), 0)), whole3(bm.shape), whole3(cm.shape),
                  whole2(pw.shape), whole2(q.shape)],
        out_specs=[pl.BlockSpec((T, S5_WIDTH), lambda i: (pos(i), 0)),
                   pl.BlockSpec((None, 1, 2 * N_STATE), lambda i: (pos(i), 0, 0))],
        scratch=[pltpu.VMEM((1, 2 * N_STATE), F32)], sem=("arbitrary",))(u3, bm, cm, pw, q)


def s5_backward(u3, dy3, cin, bm, cm, pw_h, q_h, pw_l, q_l, rev, n_chunks, blk0, name):
    T = SCAN_T

    def pos(i):
        return i if rev else (n_chunks - 1 - i)

    def body(u_ref, dy_ref, cin_ref, bm_ref, cm_ref, pwh_ref, qh_ref, pwl_ref, ql_ref, du_ref, dbm_ref, dcm_ref,
             da_ref, lcarry):
        @pl.when(pl.program_id(0) == 0)
        def _():
            lcarry[...] = jnp.zeros_like(lcarry)
            dbm_ref[...] = jnp.zeros_like(dbm_ref)
            dcm_ref[...] = jnp.zeros_like(dcm_ref)
            da_ref[...] = jnp.zeros_like(da_ref)

        to_strands, to_tokens = _strand_perms()
        u = _permute(to_strands, u_ref[...])
        dy = _permute(to_strands, dy_ref[...])
        for o in range(OCTETS):
            u_o = u[:, o * LANES:(o + 1) * LANES]
            dy_o = dy[:, o * LANES:(o + 1) * LANES]
            bu = jnp.dot(u_o, bm_ref[o], preferred_element_type=F32)
            g = lax.dot_general(dy_o, cm_ref[o], (((1,), (1,)), ((), ())), preferred_element_type=F32)
            hr, hi, lr, li = [], [], [], []
            for j in range(4):
                col = o * 1024 + j * LANES
                sl_r = slice(j * LANES, (j + 1) * LANES)
                sl_i = slice(512 + j * LANES, 512 + (j + 1) * LANES)
                xr, xi, _, (e_r, e_i) = _scan_strands(
                    bu[:, sl_r], bu[:, sl_i], pwh_ref, qh_ref, col, rev, False,
                    cin_ref[0:1, col:col + LANES], cin_ref[0:1, col + 512:col + 512 + LANES])
                ar_, ai_, (l_r, l_i), _ = _scan_strands(
                    g[:, sl_r], g[:, sl_i], pwl_ref, ql_ref, col, not rev, True,
                    lcarry[0:1, col:col + LANES], lcarry[0:1, col + 512:col + 512 + LANES])
                lcarry[0:1, col:col + LANES] = l_r
                lcarry[0:1, col + 512:col + 512 + LANES] = l_i
                acc_r = acc_i = None
                for k in range(STRAND):
                    kp = k + 1 if rev else k - 1
                    p_r, p_i = (e_r, e_i) if not 0 <= kp < STRAND else (xr[kp], xi[kp])
                    t_r = ar_[k] * p_r + ai_[k] * p_i
                    t_i = ai_[k] * p_r - ar_[k] * p_i
                    acc_r, acc_i = (t_r, t_i) if acc_r is None else (acc_r + t_r, acc_i + t_i)
                da_ref[0:1, col:col + LANES] += jnp.sum(acc_r, axis=0, keepdims=True)
                da_ref[0:1, col + 512:col + 512 + LANES] += jnp.sum(acc_i, axis=0, keepdims=True)
                hr.append(jnp.concatenate(xr, axis=0))
                hi.append(jnp.concatenate(xi, axis=0))
                lr.append(jnp.concatenate(ar_, axis=0))
                li.append(jnp.concatenate(ai_, axis=0))
            h = jnp.concatenate(hr + hi, axis=1).astype(BF16)
            lam = jnp.concatenate(lr + li, axis=1).astype(BF16)
            du_ref[:, o * LANES:(o + 1) * LANES] = lax.dot_general(
                _permute(to_tokens, lam), bm_ref[o], (((1,), (1,)), ((), ())), preferred_element_type=F32)
            dbm_ref[o] += lax.dot_general(u_o, lam, (((0,), (0,)), ((), ())), preferred_element_type=F32)
            dcm_ref[o] += lax.dot_general(h, dy_o, (((0,), (0,)), ((), ())), preferred_element_type=F32)

    whole3 = lambda s: pl.BlockSpec(s, lambda i: (0, 0, 0))
    whole2 = lambda s: pl.BlockSpec(s, lambda i: (0, 0))
    row_spec = pl.BlockSpec((T, S5_WIDTH), lambda i: (blk0 + pos(i), 0))
    return _call(
        body, name,
        [jax.ShapeDtypeStruct((n_chunks * T, S5_WIDTH), F32), jax.ShapeDtypeStruct(bm.shape, F32),
         jax.ShapeDtypeStruct(cm.shape, F32), jax.ShapeDtypeStruct((1, 2 * N_STATE), F32)],
        grid=(n_chunks,),
        in_specs=[row_spec, row_spec, pl.BlockSpec((None, 1, 2 * N_STATE), lambda i: (pos(i), 0, 0)),
                  whole3(bm.shape), whole3(cm.shape), whole2(pw_h.shape), whole2(q_h.shape), whole2(pw_l.shape),
                  whole2(q_l.shape)],
        out_specs=[pl.BlockSpec((T, S5_WIDTH), lambda i: (pos(i), 0)), whole3(bm.shape), whole3(cm.shape),
                   whole2((1, 2 * N_STATE))],
        scratch=[pltpu.VMEM((1, 2 * N_STATE), F32)], sem=("arbitrary",))(u3, dy3, cin, bm, cm, pw_h, q_h, pw_l, q_l)


def s5_tables(ar, ai, ls):
    def body(ar_ref, ai_ref, ls_ref, pr_ref, pi_ref, qr_ref, qi_ref):
        dt = jnp.exp(ls_ref[...])
        for n_rows, step, o_r, o_i in ((STRAND, 1.0, pr_ref, pi_ref), (SUB, float(STRAND), qr_ref, qi_ref)):
            m = (lax.broadcasted_iota(jnp.int32, (n_rows, 1), 0) + 1).astype(F32) * step
            mag = jnp.exp(m * (ar_ref[...] * dt))
            ang = m * (ai_ref[...] * dt)
            o_r[...] = mag * jnp.cos(ang)
            o_i[...] = mag * jnp.sin(ang)

    vec = pl.BlockSpec((None, 1, N_STATE), lambda d: (d, 0, 0))
    tab = lambda n: pl.BlockSpec((None, n, N_STATE), lambda d: (d, 0, 0))
    shp = lambda n: jax.ShapeDtypeStruct((2, n, N_STATE), F32)
    return _call(body, "s5_tables", [shp(STRAND), shp(STRAND), shp(SUB), shp(SUB)], grid=(2,),
                 in_specs=[vec, vec, vec], out_specs=[tab(STRAND), tab(STRAND), tab(SUB), tab(SUB)],
                 sem=("arbitrary",))(ar, ai, ls)


def f_discretize(a_re, a_im, ls, b_re, b_im):
    dt = jnp.exp(ls)
    mag = jnp.exp(a_re * dt)
    ab_re = mag * jnp.cos(a_im * dt)
    ab_im = mag * jnp.sin(a_im * dt)
    p = ab_re - 1.0
    q = ab_im
    den = a_re * a_re + a_im * a_im
    k_re = ((p * a_re + q * a_im) / den)[None]
    k_im = ((q * a_re - p * a_im) / den)[None]
    return ab_re, ab_im, k_re * b_re - k_im * b_im, k_re * b_im + k_im * b_re


def _disc_specs():
    a = pl.BlockSpec((None, S5_GROUPS, S5_STATE), lambda d: (d, 0, 0))
    s = pl.BlockSpec((None, S5_GROUPS, 1), lambda d: (d, 0, 0))
    b = pl.BlockSpec((None, S5_GROUP, S5_GROUPS, S5_STATE), lambda d: (d, 0, 0, 0))
    return a, s, b


def s5_discretize(a_re, a_im, ls, b_re, b_im):
    def body(ar, ai, l, br, bi, obr, obi):
        _, _, r, i = f_discretize(ar[...], ai[...], l[...], br[...], bi[...])
        obr[...] = r
        obi[...] = i

    a, s, b = _disc_specs()
    return _call(body, "s5_discretize", [jax.ShapeDtypeStruct(b_re.shape, F32)] * 2, grid=(2,),
                 in_specs=[a, a, s, b, b], out_specs=[b, b], sem=("arbitrary",))(a_re, a_im, ls, b_re, b_im)


def s5_discretize_bwd(a_re, a_im, ls, b_re, b_im, dab_re, dab_im, dbb_re, dbb_im):
    def body(ar, ai, l, br, bi, c0, c1, c2, c3, o0, o1, o2, o3, o4):
        _, vjp = jax.vjp(f_discretize, ar[...], ai[...], l[...], br[...], bi[...])
        outs = vjp((c0[...], c1[...], c2[...], c3[...]))
        for ref, val in zip((o0, o1, o2, o3, o4), outs):
            ref[...] = val

    a, s, b = _disc_specs()
    shapes = [jax.ShapeDtypeStruct(t.shape, F32) for t in (a_re, a_im, ls, b_re, b_im)]
    return _call(body, "s5_discretize_bwd", shapes, grid=(2,), in_specs=[a, a, s, b, b, a, a, b, b],
                 out_specs=[a, a, s, b, b], sem=("arbitrary",))(a_re, a_im, ls, b_re, b_im, dab_re, dab_im,
                                                                 dbb_re, dbb_im)


def _conv_taps(s_ref, base, n_rows):
    n = n_rows + 2 * CONV_PAD
    ext = s_ref[pl.ds(base, n), :]
    col = (lax.broadcasted_iota(jnp.int32, (n, 1), 0) + (2 * GRID_W - CONV_PAD)) % GRID_W
    left = jnp.where(col == 0, 0.0, pltpu.roll(ext, 1, 0))
    right = jnp.where(col == GRID_W - 1, 0.0, pltpu.roll(ext, n - 1, 0))
    return left, ext, right


def _conv_apply(taps, w_ref, n_rows, flip):
    out = None
    for i in range(3):
        wi = 2 - i if flip else i
        comb = None
        for j in range(3):
            wj = 2 - j if flip else j
            term = w_ref[wi * 3 + wj:wi * 3 + wj + 1, :] * taps[j]
            comb = term if comb is None else comb + term
        start = CONV_PAD + (i - 1) * GRID_W
        part = comb[start:start + n_rows]
        out = part if out is None else out + part
    return out


def _conv_fill(dst_ref, src_ref, n_tok):
    zeros = jnp.zeros((CONV_PAD, LANES), F32)
    dst_ref[0:CONV_PAD, :] = zeros
    dst_ref[CONV_PAD + n_tok:2 * CONV_PAD + n_tok, :] = zeros

    def step(r, carry):
        base = pl.multiple_of(r * CONV_ROWS, CONV_ROWS)
        dst_ref[pl.ds(base + CONV_PAD, CONV_ROWS), :] = src_ref[pl.ds(base, CONV_ROWS), :].astype(F32)
        return carry

    lax.fori_loop(0, n_tok // CONV_ROWS, step, 0)


def conv_forward(up, wg, wv, bias):
    n_tok = up.shape[0]
    nb = FFN_HIDDEN // LANES

    def body(ug_ref, uv_ref, wg_ref, wv_ref, bg_ref, bv_ref, act_ref, gate_ref, val_ref, sg, sv):
        _conv_fill(sg, ug_ref, n_tok)
        _conv_fill(sv, uv_ref, n_tok)

        def step(r, carry):
            base = pl.multiple_of(r * CONV_ROWS, CONV_ROWS)
            gate = _conv_apply(_conv_taps(sg, base, CONV_ROWS), wg_ref, CONV_ROWS, False) + bg_ref[...]
            val = _conv_apply(_conv_taps(sv, base, CONV_ROWS), wv_ref, CONV_ROWS, False) + bv_ref[...]
            act_ref[pl.ds(base, CONV_ROWS), :] = (gate * jax.nn.sigmoid(gate) * val).astype(BF16)
            gate_ref[pl.ds(base, CONV_ROWS), :] = gate.astype(BF16)
            val_ref[pl.ds(base, CONV_ROWS), :] = val.astype(BF16)
            return carry

        lax.fori_loop(0, n_tok // CONV_ROWS, step, 0)

    col = lambda off: pl.BlockSpec((n_tok, LANES), lambda k: (0, k + off))
    wsp = lambda off: pl.BlockSpec((16, LANES), lambda k: (0, k + off))
    bsp = lambda off: pl.BlockSpec((1, LANES), lambda k: (0, k + off))
    pad = pltpu.VMEM((n_tok + 2 * CONV_PAD, LANES), F32)
    half = jax.ShapeDtypeStruct((n_tok, FFN_HIDDEN), BF16)
    return _call(body, "conv_forward", [half, half, half], grid=(nb,),
                 in_specs=[col(0), col(nb), wsp(0), wsp(0), bsp(0), bsp(nb)], out_specs=[col(0), col(0), col(0)],
                 scratch=[pad, pad], sem=("arbitrary",))(up, up, wg, wv, bias, bias)


def conv_backward(up, gate, val, dact, wg, wv):
    n_tok = up.shape[0]
    nb = FFN_HIDDEN // LANES
    n_steps = n_tok // CONV_ROWS

    def body(ug_ref, uv_ref, gate_ref, val_ref, da_ref, wg_ref, wv_ref, dug_ref, duv_ref, dwg_ref, dwv_ref,
             sg, sv, sdg, sdv):
        _conv_fill(sg, ug_ref, n_tok)
        _conv_fill(sv, uv_ref, n_tok)
        zeros = jnp.zeros((CONV_PAD, LANES), F32)
        for s_ref in (sdg, sdv):
            s_ref[0:CONV_PAD, :] = zeros
            s_ref[CONV_PAD + n_tok:2 * CONV_PAD + n_tok, :] = zeros
        dwg_ref[...] = jnp.zeros_like(dwg_ref)
        dwv_ref[...] = jnp.zeros_like(dwv_ref)

        def grads(r, carry):
            base = pl.multiple_of(r * CONV_ROWS, CONV_ROWS)
            taps_g = _conv_taps(sg, base, CONV_ROWS)
            taps_v = _conv_taps(sv, base, CONV_ROWS)
            gate = gate_ref[pl.ds(base, CONV_ROWS), :].astype(F32)
            val = val_ref[pl.ds(base, CONV_ROWS), :].astype(F32)
            d_act = da_ref[pl.ds(base, CONV_ROWS), :].astype(F32)
            sig = jax.nn.sigmoid(gate)
            d_gate = d_act * val * (sig * (1.0 + gate * (1.0 - sig)))
            d_val = d_act * (gate * sig)
            sdg[pl.ds(base + CONV_PAD, CONV_ROWS), :] = d_gate
            sdv[pl.ds(base + CONV_PAD, CONV_ROWS), :] = d_val
            for d_out, taps, dw_ref in ((d_gate, taps_g, dwg_ref), (d_val, taps_v, dwv_ref)):
                for i in range(3):
                    start = CONV_PAD + (i - 1) * GRID_W
                    for j in range(3):
                        k = i * 3 + j
                        dw_ref[k:k + 1, :] += jnp.sum(d_out * taps[j][start:start + CONV_ROWS], axis=0, keepdims=True)
                dw_ref[9:10, :] += jnp.sum(d_out, axis=0, keepdims=True)
            return carry

        lax.fori_loop(0, n_steps, grads, 0)

        def spread(r, carry):
            base = pl.multiple_of(r * CONV_ROWS, CONV_ROWS)
            dug_ref[pl.ds(base, CONV_ROWS), :] = _conv_apply(
                _conv_taps(sdg, base, CONV_ROWS), wg_ref, CONV_ROWS, True).astype(BF16)
            duv_ref[pl.ds(base, CONV_ROWS), :] = _conv_apply(
                _conv_taps(sdv, base, CONV_ROWS), wv_ref, CONV_ROWS, True).astype(BF16)
            return carry

        lax.fori_loop(0, n_steps, spread, 0)

    col = lambda off: pl.BlockSpec((n_tok, LANES), lambda k: (0, k + off))
    wsp = pl.BlockSpec((16, LANES), lambda k: (0, k))
    pad = pltpu.VMEM((n_tok + 2 * CONV_PAD, LANES), F32)
    half = jax.ShapeDtypeStruct((n_tok, FFN_HIDDEN), BF16)
    dw = jax.ShapeDtypeStruct((16, FFN_HIDDEN), F32)
    return _call(body, "conv_backward", [half, half, dw, dw], grid=(nb,),
                 in_specs=[col(0), col(nb), col(0), col(0), col(0), wsp, wsp],
                 out_specs=[col(0), col(0), wsp, wsp], scratch=[pad, pad, pad, pad],
                 sem=("arbitrary",))(up, up, gate, val, dact, wg, wv)


def f_adamw(w, g, m, v):
    m = ADAM_B1 * m + (1.0 - ADAM_B1) * g
    v = ADAM_B2 * v + (1.0 - ADAM_B2) * jnp.square(g)
    m_hat = m / (1.0 - ADAM_B1 ** ADAM_STEP)
    v_hat = v / (1.0 - ADAM_B2 ** ADAM_STEP)
    delta = -ADAM_LR * (m_hat / (jnp.sqrt(v_hat) + ADAM_EPS) + ADAM_WD * w)
    return delta, m, v


def adamw(w, g, m, v, name):
    shape = w.shape
    cols = shape[-1]
    rows = w.size // cols
    two_d = [t.reshape(rows, cols) for t in (w, g, m, v)]
    tm = _tile(rows, 256, 8) if rows % 8 == 0 else rows
    outs = rowcall(f_adamw, name, rows, tm, [(t, cols, 0, 0) for t in two_d], [], [(cols, F32)] * 3, [])
    return tuple(o.reshape(shape) for o in outs)


def _place():
    return lax.axis_index("x"), lax.axis_index("y"), lax.axis_index("c")


_ANY = pl.BlockSpec(memory_space=pl.ANY)


def allgather_devices(v, name):
    def body(v_ref, out_ref, send_sems, recv_sems, local_sem):
        x, y, c = _place()
        me, sibling = (x, y, c), (x, y, 1 - c)
        chips = [(1 - x, y), (x, 1 - y), (1 - x, 1 - y)]

        def slot(p):
            return out_ref.at[4 * p[0] + 2 * p[1] + p[2]]

        def copy(k, block, to, src=None):
            return pltpu.make_async_remote_copy(
                src_ref=slot(block) if src is None else src, dst_ref=slot(block),
                send_sem=send_sems.at[k], recv_sem=recv_sems.at[k], device_id=to, device_id_type=MESH)

        mine = pltpu.make_async_copy(v_ref, slot(me), local_sem)
        mine.start()
        first = [copy(0, me, sibling, src=v_ref)]
        first += [copy(1 + j, me, (*chip, c), src=v_ref) for j, chip in enumerate(chips)]
        for cp in first:
            cp.start()
        passed = [copy(4 + j, (*chip, c), sibling) for j, chip in enumerate(chips)]
        for j, chip in enumerate(chips):
            copy(1 + j, (*chip, c), me).wait_recv()
            passed[j].start()
        copy(0, sibling, me).wait_recv()
        for j, chip in enumerate(chips):
            copy(4 + j, (*chip, 1 - c), me).wait_recv()
        for cp in first + passed:
            cp.wait_send()
        mine.wait()

    return _call(body, name, jax.ShapeDtypeStruct((N_DEV,) + v.shape, v.dtype), in_specs=[_ANY], out_specs=_ANY,
                 scratch=[pltpu.SemaphoreType.DMA((7,)), pltpu.SemaphoreType.DMA((7,)), pltpu.SemaphoreType.DMA])(v)


def _other_chips(x, y):
    return [(1 - x, y), (x, 1 - y), (1 - x, 1 - y)]


def allgather_chips(vs, name):
    n = len(vs)
    shapes = [v.shape for v in vs]
    vs = [v.reshape((2, v.shape[0] // 2) + v.shape[1:]) for v in vs]

    def body(*refs):
        v_refs, o_refs = refs[:n], refs[n:2 * n]
        send_sems, recv_sems, local_sems = refs[2 * n:]
        x, y, c = _place()
        sibling = (x, y, 1 - c)
        chips = _other_chips(x, y)

        def rows(a, chip, h):
            return o_refs[a].at[2 * chip[0] + chip[1], h]

        def copy(a, k, chip, h, to, src=None):
            return pltpu.make_async_remote_copy(
                src_ref=rows(a, chip, h) if src is None else src, dst_ref=rows(a, chip, h),
                send_sem=send_sems.at[6 * a + k], recv_sem=recv_sems.at[6 * a + k], device_id=to, device_id_type=MESH)

        mine = [pltpu.make_async_copy(v_refs[a], o_refs[a].at[2 * x + y], local_sems.at[a]) for a in range(n)]
        first = [copy(a, j, (x, y), c, (*chip, c), src=v_refs[a].at[c])
                 for a in range(n) for j, chip in enumerate(chips)]
        for cp in mine + first:
            cp.start()
        passed = []
        for j, chip in enumerate(chips):
            for a in range(n):
                copy(a, j, chip, c, (x, y, c)).wait_recv()
                passed.append(copy(a, 3 + j, chip, c, sibling))
                passed[-1].start()
        for j, chip in enumerate(chips):
            for a in range(n):
                copy(a, 3 + j, chip, 1 - c, (x, y, c)).wait_recv()
        for cp in first + passed:
            cp.wait_send()
        for cp in mine:
            cp.wait()

    outs = _call(body, name, [jax.ShapeDtypeStruct((N_CHIPS,) + v.shape, v.dtype) for v in vs], in_specs=[_ANY] * n,
                 out_specs=[_ANY] * n, scratch=[pltpu.SemaphoreType.DMA((6 * n,)), pltpu.SemaphoreType.DMA((6 * n,)),
                                                pltpu.SemaphoreType.DMA((n,))])(*vs)
    return [o.reshape((N_CHIPS,) + s) for o, s in zip(outs, shapes)]


def grad_pair_swap(gs, name):
    n = len(gs)

    def body(*refs):
        g_refs, o_refs, send_sems, recv_sems = refs[:n], refs[n:2 * n], refs[2 * n], refs[2 * n + 1]
        x, y, c = _place()
        cps = [pltpu.make_async_remote_copy(
            src_ref=g_refs[a].at[:, 1 - c], dst_ref=o_refs[a], send_sem=send_sems.at[a], recv_sem=recv_sems.at[a],
            device_id=(x, y, 1 - c), device_id_type=MESH) for a in range(n)]
        for cp in cps:
            cp.start()
        for cp in cps:
            cp.wait()

    return _call(body, name, [jax.ShapeDtypeStruct((N_CHIPS,) + g.shape[2:], g.dtype) for g in gs],
                 in_specs=[_ANY] * n, out_specs=[_ANY] * n,
                 scratch=[pltpu.SemaphoreType.DMA((n,)), pltpu.SemaphoreType.DMA((n,))])(*gs)


def pair_sum(g, recv, core, name):
    r2, cols = recv.shape[1], recv.shape[2]
    tr = _tile(r2, 256, 8)
    nt = r2 // tr

    def body(c_ref, g_ref, r_ref, o_ref):
        o_ref[...] = (g_ref[...] + r_ref[...]).astype(o_ref.dtype)

    spec = pl.BlockSpec((None, tr, cols), lambda q, i, c_ref: (q, i, 0))
    grid_spec = pltpu.PrefetchScalarGridSpec(
        num_scalar_prefetch=1, grid=(N_CHIPS, nt),
        in_specs=[pl.BlockSpec((None, None, tr, cols), lambda q, i, c_ref: (q, c_ref[0], i, 0)), spec], out_specs=spec)
    return pl.pallas_call(body, name=name, out_shape=jax.ShapeDtypeStruct(recv.shape, BF16), grid_spec=grid_spec,
                          compiler_params=pltpu.CompilerParams(dimension_semantics=("arbitrary", "arbitrary"),
                                                               vmem_limit_bytes=VMEM_LIMIT_BYTES))(core, g, recv)


def grad_chip_exchange(ps, name):
    n = len(ps)

    def body(*refs):
        p_refs, o_refs = refs[:n], refs[n:2 * n]
        send_sems, recv_sems, local_sems = refs[2 * n:]
        x, y, c = _place()
        chips = _other_chips(x, y)
        me = 2 * x + y
        mine = [pltpu.make_async_copy(p_refs[a].at[me], o_refs[a].at[me], local_sems.at[a]) for a in range(n)]
        sends = [pltpu.make_async_remote_copy(
            src_ref=p_refs[a].at[2 * chip[0] + chip[1]], dst_ref=o_refs[a].at[me], send_sem=send_sems.at[3 * a + j],
            recv_sem=recv_sems.at[3 * a + j], device_id=(*chip, c), device_id_type=MESH)
            for a in range(n) for j, chip in enumerate(chips)]
        for cp in mine + sends:
            cp.start()
        for a in range(n):
            for j, chip in enumerate(chips):
                pltpu.make_async_remote_copy(
                    src_ref=p_refs[a].at[me], dst_ref=o_refs[a].at[2 * chip[0] + chip[1]],
                    send_sem=send_sems.at[3 * a + j], recv_sem=recv_sems.at[3 * a + j], device_id=(*chip, c),
                    device_id_type=MESH).wait_recv()
        for cp in sends:
            cp.wait_send()
        for cp in mine:
            cp.wait()

    return _call(body, name, [jax.ShapeDtypeStruct(p.shape, p.dtype) for p in ps], in_specs=[_ANY] * n,
                 out_specs=[_ANY] * n, scratch=[pltpu.SemaphoreType.DMA((3 * n,)), pltpu.SemaphoreType.DMA((3 * n,)),
                                                pltpu.SemaphoreType.DMA((n,))])(*ps)


def grad_half_swap(ss, name):
    n = len(ss)

    def body(*refs):
        s_refs, o_refs = refs[:n], refs[n:2 * n]
        send_sems, recv_sems, local_sems = refs[2 * n:]
        x, y, c = _place()
        mine, sends = [], []
        for a in range(n):
            mine.append(pltpu.make_async_copy(s_refs[a], o_refs[a].at[c], local_sems.at[a]))
            sends.append(pltpu.make_async_remote_copy(
                src_ref=s_refs[a], dst_ref=o_refs[a].at[c], send_sem=send_sems.at[a],
                recv_sem=recv_sems.at[a], device_id=(x, y, 1 - c), device_id_type=MESH))
        for cp in mine + sends:
            cp.start()
        for a in range(n):
            pltpu.make_async_remote_copy(
                src_ref=s_refs[a], dst_ref=o_refs[a].at[1 - c], send_sem=send_sems.at[a],
                recv_sem=recv_sems.at[a], device_id=(x, y, 1 - c), device_id_type=MESH).wait_recv()
        for cp in sends:
            cp.wait_send()
        for cp in mine:
            cp.wait()

    outs = _call(body, name, [jax.ShapeDtypeStruct((2,) + s.shape, s.dtype) for s in ss],
                 in_specs=[_ANY] * n, out_specs=[_ANY] * n,
                 scratch=[pltpu.SemaphoreType.DMA((n,)), pltpu.SemaphoreType.DMA((n,)), pltpu.SemaphoreType.DMA((n,))])(*ss)
    return [o.reshape(2 * s.shape[0], s.shape[1]) for o, s in zip(outs, ss)]


ROW_TILE = 256

BIG_SHARDS = (("w_in", (D_MODEL, 896), 1), ("s5_w_glu", (128, S5_WIDTH), 0), ("w_proj_a", (S5_WIDTH, 256), 1),
              ("w_proj_b", (SGU_WIDTH, 256), 1), ("w_out", (256, D_MODEL), 0), ("w_up", (D_MODEL, 1408), 1),
              ("w_down", (704, D_MODEL), 0))
SMALL_PARAMS = ("g_mix", "s5_a_re", "s5_a_im", "s5_log_step", "s5_b_re", "s5_b_im", "s5_c_re", "s5_c_im", "s5_d",
                "s5_b_glu", "sgu_ln_g", "sgu_ln_b", "sgu_w", "sgu_b", "b_gate", "g_ffn", "conv_b", "g_final")
PACK_COLS = 1024


def _rows_of(n):
    return -(-n // PACK_COLS)


def _pack_rows(arrays, total_rows, dtype):
    parts = []
    used = 0
    for a in arrays:
        r = _rows_of(a.size)
        parts.append(jnp.pad(a.reshape(-1).astype(dtype), (0, r * PACK_COLS - a.size)).reshape(r, PACK_COLS))
        used += r
    if total_rows > used:
        parts.append(jnp.zeros((total_rows - used, PACK_COLS), dtype))
    return jnp.concatenate(parts, axis=0)


def _unpack_rows(packed, shapes, row0=0):
    out = []
    for s in shapes:
        n = 1
        for d in s:
            n *= d
        r = _rows_of(n)
        out.append(packed[row0:row0 + r].reshape(-1)[:n].reshape(s))
        row0 += r
    return out


def _octet_major(re, im):
    parts = []
    for o in range(OCTETS):
        parts += [re[:, o * 512:(o + 1) * 512], im[:, o * 512:(o + 1) * 512]]
    return jnp.concatenate(parts, axis=1)


def _octet_split(v):
    v = v.reshape(OCTETS, 2, 512)
    return v[:, 0].reshape(N_STATE), v[:, 1].reshape(N_STATE)


def _s5_bmat(bb):
    t = bb.reshape(S5_GROUP, OCTETS, 8, 1, S5_STATE) * jnp.eye(8, dtype=F32)[None, None, :, :, None]
    return jnp.transpose(t, (1, 2, 0, 3, 4)).reshape(OCTETS, LANES, 512)


def _s5_bmat_t(dm):
    t = dm.reshape(OCTETS, 8, S5_GROUP, 8, S5_STATE) * jnp.eye(8, dtype=F32)[None, :, None, :, None]
    return jnp.transpose(t.sum(axis=3), (2, 0, 1, 3)).reshape(S5_GROUP, S5_GROUPS, S5_STATE)


def _s5_cmat(c):
    t = c.reshape(OCTETS, 8, 1, S5_GROUP, S5_STATE) * jnp.eye(8, dtype=F32)[None, :, :, None, None]
    return jnp.transpose(t, (0, 2, 4, 1, 3)).reshape(OCTETS, 512, LANES)


def _s5_cmat_t(dm):
    t = dm.reshape(OCTETS, 8, S5_STATE, 8, S5_GROUP) * jnp.eye(8, dtype=F32)[None, :, None, :, None]
    return jnp.transpose(t.sum(axis=1), (0, 2, 3, 1)).reshape(S5_GROUPS, S5_GROUP, S5_STATE)


def local_step(x, ctx, tgt, mod, modc, W):
    n_tok, n_ctx = x.shape[0], ctx.shape[0]
    tm = ROW_TILE
    D = D_MODEL
    sh1, sc1, ga1, sh2, sc2, ga2 = [mod[:, k * D:(k + 1) * D] for k in range(N_MOD)]
    sh1c, sc1c = modc[:, :D], modc[:, D:2 * D]
    g_mix, g_ffn, g_final = W["g_mix"], W["g_ffn"], W["g_final"]
    w_in = W["w_in"]
    w_in_u = w_in[0][:, :S5_WIDTH]

    h = rowcall(f_modulate, "mod1", n_tok, tm, [(x, D, 0, 0)], [g_mix, sc1, sh1], [(D, BF16)], [])[0]
    hc = rowcall(f_modulate, "mod1_ctx", n_ctx, tm, [(ctx, D, 0, 0)], [g_mix, sc1c, sh1c], [(D, BF16)], [])[0]
    proj = matmul(h, w_in, "nn", BF16, "proj_in", shards=True)
    uc = matmul(hc, w_in_u, "nn", BF16, "proj_in_ctx")
    u3 = jnp.concatenate([uc, proj[:, :S5_WIDTH], uc], axis=0)

    a_re, a_im, ls = W["s5_a_re"], W["s5_a_im"], W["s5_log_step"][..., None]
    b_re_t = jnp.transpose(W["s5_b_re"], (0, 3, 1, 2))
    b_im_t = jnp.transpose(W["s5_b_im"], (0, 3, 1, 2))
    bb_re, bb_im = s5_discretize(a_re, a_im, ls, b_re_t, b_im_t)
    ls_rep = jnp.repeat(W["s5_log_step"], S5_STATE, axis=1).reshape(2, 1, N_STATE)
    p_re, p_im, q_re, q_im = s5_tables(a_re.reshape(2, 1, N_STATE), a_im.reshape(2, 1, N_STATE), ls_rep)
    n_chunks = (n_ctx + n_tok) // SCAN_T
    ctx_blk = n_ctx // SCAN_T
    pw, qt, bm, cm = [], [], [], []
    for d in range(2):
        p16 = _octet_major(p_re[d], p_im[d])
        q8 = _octet_major(q_re[d], q_im[d])
        pw.append((jnp.repeat(p16, SUB, axis=0), jnp.repeat(p16[::-1], SUB, axis=0)))
        qt.append((q8, q8[::-1]))
        bm.append(jnp.concatenate([_s5_bmat(bb_re[d]), _s5_bmat(bb_im[d])], axis=2).astype(BF16))
        cm.append(jnp.concatenate([_s5_cmat(W["s5_c_re"][d]), -_s5_cmat(W["s5_c_im"][d])], axis=1).astype(BF16))
    y0, cin0 = s5_forward(u3, bm[0], cm[0], pw[0][0], qt[0][0], False, n_chunks, 0, "s5_fwd0")
    y1, cin1 = s5_forward(u3, bm[1], cm[1], pw[1][1], qt[1][1], True, n_chunks, ctx_blk, "s5_fwd1")

    mix_rows = [(proj, 512, 0, 0), (y0, 512, 0, n_ctx // tm), (y1, 512, 0, 0)] + \
               [(proj, 512, k, 0) for k in range(1, 7)]
    mix_vecs = [W["s5_d"], W["s5_w_glu"], W["s5_b_glu"], W["sgu_ln_g"], W["sgu_ln_b"], W["sgu_w"],
                jnp.transpose(W["sgu_b"]), W["w_proj_a"], W["w_proj_b"], W["b_gate"]]
    mrg = rowcall(f_mixer, "mixer", n_tok, tm, mix_rows, mix_vecs, [(D, BF16)], [])[0]
    o = matmul(mrg, W["w_out"], "nn", F32, "proj_out")
    x1, h2 = rowcall(f_resid_mod, "resid_mod2", n_tok, tm, [(x, D, 0, 0), (o, D, 0, 0)], [ga1, g_ffn, sc2, sh2],
                     [(D, F32), (D, BF16)], [])
    up = matmul(h2, W["w_up"], "nn", BF16, "ffn_up", shards=True)
    conv_w = W["conv_w"].reshape(9, 2 * FFN_HIDDEN)
    wg = jnp.pad(conv_w[:, :FFN_HIDDEN], ((0, 7), (0, 0)))
    wv = jnp.pad(conv_w[:, FFN_HIDDEN:], ((0, 7), (0, 0)))
    act, gate, val = conv_forward(up, wg, wv, W["conv_b"])
    dn = matmul(act, W["w_down"], "nn", F32, "ffn_down")

    def final_fn(x1_, dn_, tgt_, ga2_, gf_):
        loss, (dx1_, ddn_, dga2_, dgf_) = jax.value_and_grad(f_final_loss, argnums=(0, 1, 3, 4))(
            x1_, dn_, tgt_, ga2_, gf_)
        return dx1_, ddn_, loss.reshape(1, 1), dga2_, dgf_

    dx2, ddn, loss, d_ga2, d_gfinal = rowcall(
        final_fn, "final_loss", n_tok, tm, [(x1, D, 0, 0), (dn, D, 0, 0), (tgt, D, 0, 0)], [ga2, g_final],
        [(D, F32), (D, BF16)], [(1, 1), (1, D), (1, D)])

    dact = matmul(ddn, W["w_down"], "nt", BF16, "ffn_down_dx")
    d_w_down = matmul(act, ddn, "tn", F32, "ffn_down_dw")
    dup_g, dup_v, dwg, dwv = conv_backward(up, gate, val, dact, wg, wv)
    dup = jnp.concatenate([dup_g, dup_v], axis=1)
    d_conv_w = jnp.concatenate([dwg[:9], dwv[:9]], axis=1).reshape(3, 3, 2 * FFN_HIDDEN)
    d_conv_b = jnp.concatenate([dwg[9:10], dwv[9:10]], axis=1)
    dh2 = matmul(dup, W["w_up"], "nt", BF16, "ffn_up_dx", shards=True)
    d_w_up = matmul(h2, dup, "tn", F32, "ffn_up_dw", shards=True)

    def resid_bwd(x_, o_, dx1_, dh2_, ga_, g_, sc_, sh_):
        _, vjp = jax.vjp(f_resid_mod, x_, o_, ga_, g_, sc_, sh_)
        return vjp((dx1_, dh2_))

    dxa, do, d_ga1, d_gffn, d_sc2, d_sh2 = rowcall(
        resid_bwd, "resid_mod2_bwd", n_tok, tm, [(x, D, 0, 0), (o, D, 0, 0), (dx2, D, 0, 0), (dh2, D, 0, 0)],
        [ga1, g_ffn, sc2, sh2], [(D, F32), (D, BF16)], [(1, D)] * 4)

    dmrg = matmul(do, W["w_out"], "nt", BF16, "proj_out_dx")
    d_w_out = matmul(mrg, do, "tn", F32, "proj_out_dw")

    def mixer_bwd(*args):
        rows, dm, vecs = args[:9], args[9], [v.astype(F32) for v in args[10:]]
        _, vjp = jax.vjp(f_mixer, *rows, *vecs)
        g = vjp(dm)
        return (g[0], g[1], jnp.concatenate([g[3], g[4]], axis=1), jnp.concatenate(g[5:9], axis=1)) + tuple(g[9:])

    mb = rowcall(mixer_bwd, "mixer_bwd", n_tok, tm, mix_rows + [(dmrg, D, 0, 0)], mix_vecs,
                 [(512, BF16), (512, BF16), (1024, BF16), (2048, BF16)], [v.shape for v in mix_vecs])
    du_direct, dys, dzb, dgl = mb[:4]
    d_s5d, d_w_glu, d_b_glu, d_ln_g, d_ln_b, d_sgu_w, d_sgu_bt, d_w_pa, d_w_pb, d_b_gate = mb[4:]

    zc = jnp.zeros((n_ctx, S5_WIDTH), BF16)
    dy3 = jnp.concatenate([zc, dys, zc], axis=0)
    du0, dbm0, dcm0, da0 = s5_backward(u3, dy3, cin0, bm[0], cm[0], pw[0][0], qt[0][0], pw[0][1], qt[0][1], False,
                                       n_chunks, 0, "s5_bwd0")
    du1, dbm1, dcm1, da1 = s5_backward(u3, dy3, cin1, bm[1], cm[1], pw[1][1], qt[1][1], pw[1][0], qt[1][0], True,
                                       n_chunks, ctx_blk, "s5_bwd1")
    add3 = lambda a, b, c: a + b + c
    du_a = rowcall(add3, "du_sum", n_tok, tm, [(du_direct, 512, 0, 0), (du0, 512, 0, n_ctx // tm), (du1, 512, 0, 0)],
                   [], [(512, BF16)], [])[0]
    du_c = rowcall(lambda a, b: a + b, "du_sum_ctx", n_ctx, tm, [(du0, 512, 0, 0), (du1, 512, 0, n_tok // tm)],
                   [], [(512, BF16)], [])[0]

    dab_re, dab_im, dbb_re, dbb_im, d_c_re, d_c_im = [], [], [], [], [], []
    for dbm, dcm, da in ((dbm0, dcm0, da0), (dbm1, dcm1, da1)):
        r, i = _octet_split(da)
        dab_re.append(r.reshape(S5_GROUPS, S5_STATE))
        dab_im.append(i.reshape(S5_GROUPS, S5_STATE))
        dbb_re.append(_s5_bmat_t(dbm[:, :, :512]))
        dbb_im.append(_s5_bmat_t(dbm[:, :, 512:]))
        d_c_re.append(_s5_cmat_t(dcm[:, :512]))
        d_c_im.append(-_s5_cmat_t(dcm[:, 512:]))
    d_a_re, d_a_im, d_ls, d_b_re_t, d_b_im_t = s5_discretize_bwd(
        a_re, a_im, ls, b_re_t, b_im_t, jnp.stack(dab_re), jnp.stack(dab_im), jnp.stack(dbb_re), jnp.stack(dbb_im))

    dproj = jnp.concatenate([du_a, dzb, dgl], axis=1)
    dh = matmul(dproj, w_in, "nt", BF16, "proj_in_dx", shards=True)
    dhc = matmul(du_c, w_in_u, "nt", BF16, "proj_in_ctx_dx")
    d_w_in_c = matmul(hc, du_c, "tn", F32, "proj_in_ctx_dw")
    d_w_in = matmul(h, dproj, "tn", F32, "proj_in_dw", shards=True,
                    init=jnp.pad(d_w_in_c[None], ((0, N_CHIPS - 1), (0, 0), (0, w_in.shape[2] - S5_WIDTH))))

    def mod_bwd_ctx(x_, dh_, g_, sc_, sh_):
        _, vjp = jax.vjp(f_modulate, x_, g_, sc_, sh_)
        return vjp(dh_)[1:]

    d_gmix_c, d_sc1c, d_sh1c = rowcall(mod_bwd_ctx, "mod1_ctx_bwd", n_ctx, tm, [(ctx, D, 0, 0), (dhc, D, 0, 0)],
                                       [g_mix, sc1c, sh1c], [], [(1, D)] * 3)

    def mod_bwd(x_, dh_, dxa_, g_, sc_, sh_):
        _, vjp = jax.vjp(f_modulate, x_, g_, sc_, sh_)
        dx_, dg_, dsc_, dsh_ = vjp(dh_)
        return dx_ + dxa_, dg_, dsc_, dsh_

    zero_d = jnp.zeros((1, D), F32)
    grad_x, d_gmix, d_sc1, d_sh1 = rowcall(
        mod_bwd, "mod1_bwd", n_tok, tm, [(x, D, 0, 0), (dh, D, 0, 0), (dxa, D, 0, 0)], [g_mix, sc1, sh1],
        [(D, F32)], [(1, D)] * 3, ainit=[d_gmix_c, zero_d, zero_d])

    grads = {
        "dmod": jnp.concatenate([d_sh1, d_sc1, d_ga1, d_sh2, d_sc2, d_ga2], axis=1),
        "dmodc": jnp.concatenate([d_sh1c, d_sc1c], axis=1),
        "g_mix": d_gmix,
        "s5_a_re": d_a_re, "s5_a_im": d_a_im, "s5_log_step": d_ls[..., 0],
        "s5_b_re": jnp.transpose(d_b_re_t, (0, 2, 3, 1)), "s5_b_im": jnp.transpose(d_b_im_t, (0, 2, 3, 1)),
        "s5_c_re": jnp.stack(d_c_re), "s5_c_im": jnp.stack(d_c_im), "s5_d": d_s5d, "s5_b_glu": d_b_glu,
        "sgu_ln_g": d_ln_g, "sgu_ln_b": d_ln_b, "sgu_w": d_sgu_w, "sgu_b": jnp.transpose(d_sgu_bt),
        "b_gate": d_b_gate, "g_ffn": d_gffn, "conv_b": d_conv_b, "g_final": d_gfinal, "conv_w": d_conv_w,
        "w_in": d_w_in, "s5_w_glu": d_w_glu, "w_proj_a": d_w_pa, "w_proj_b": d_w_pb, "w_out": d_w_out,
        "w_up": d_w_up, "w_down": d_w_down,
    }
    return loss, grad_x, grads


ADA_COLS = N_MOD * D_MODEL // N_CHIPS
MOD_ROWS = 16


def mod_forward(c16, w, b):
    n = w.shape[1]
    tn = 512

    def body(c_ref, w_ref, b_ref, o_ref):
        cv = c_ref[...]
        cs = cv * jax.nn.sigmoid(cv)
        o_ref[...] = jnp.dot(cs.astype(BF16), w_ref[...].astype(BF16), preferred_element_type=F32) + b_ref[...]

    return _call(body, "mod_forward", jax.ShapeDtypeStruct((MOD_ROWS, n), F32), grid=(n // tn,),
                 in_specs=[pl.BlockSpec((MOD_ROWS, D_MODEL), lambda j: (0, 0)),
                           pl.BlockSpec((D_MODEL, tn), lambda j: (0, j)), pl.BlockSpec((1, tn), lambda j: (0, j))],
                 out_specs=pl.BlockSpec((MOD_ROWS, tn), lambda j: (0, j)), sem=("arbitrary",))(c16, w, b)


def f_ada_outer(ct, dm):
    cs = ct * jax.nn.sigmoid(ct)
    acc = cs[:, 0:1] * dm[0:1]
    for k in range(1, 9):
        acc = acc + cs[:, k:k + 1] * dm[k:k + 1]
    return acc


def f_cctx_grad(z, p4):
    s = jax.nn.sigmoid(z)
    return (p4[0:1] + p4[1:2] + p4[2:3] + p4[3:4]) * (s + z * s * (1.0 - s))


WEIGHT_NAMES = ("c_ctx", "w_ada", "b_ada", "g_mix", "w_in", "s5_a_re", "s5_a_im", "s5_log_step", "s5_b_re",
                "s5_b_im", "s5_c_re", "s5_c_im", "s5_d", "s5_w_glu", "s5_b_glu", "sgu_ln_g", "sgu_ln_b", "sgu_w",
                "sgu_b", "w_proj_a", "w_proj_b", "b_gate", "w_out", "g_ffn", "w_up", "conv_w", "conv_b", "w_down",
                "g_final")
CONV_SHARD = 2 * FFN_HIDDEN // N_CHIPS
SMALL_PACK_ROWS = 512
SMALL_ROW0 = 58
ADAM_PACK_ROWS = 512


def kernel(x, c, ctx, c_ctx, w_ada, b_ada, g_mix, w_in, s5_a_re, s5_a_im, s5_log_step, s5_b_re, s5_b_im, s5_c_re, s5_c_im, s5_d, s5_w_glu, s5_b_glu, sgu_ln_g, sgu_ln_b, sgu_w, sgu_b, w_proj_a, w_proj_b, b_gate, w_out, g_ffn, w_up, conv_w, conv_b, w_down, g_final, loss_target, m_c_ctx, m_w_ada, m_b_ada, m_g_mix, m_w_in, m_s5_a_re, m_s5_a_im, m_s5_log_step, m_s5_b_re, m_s5_b_im, m_s5_c_re, m_s5_c_im, m_s5_d, m_s5_w_glu, m_s5_b_glu, m_sgu_ln_g, m_sgu_ln_b, m_sgu_w, m_sgu_b, m_w_proj_a, m_w_proj_b, m_b_gate, m_w_out, m_g_ffn, m_w_up, m_conv_w, m_conv_b, m_w_down, m_g_final, v_c_ctx, v_w_ada, v_b_ada, v_g_mix, v_w_in, v_s5_a_re, v_s5_a_im, v_s5_log_step, v_s5_b_re, v_s5_b_im, v_s5_c_re, v_s5_c_im, v_s5_d, v_s5_w_glu, v_s5_b_glu, v_sgu_ln_g, v_sgu_ln_b, v_sgu_w, v_sgu_b, v_w_proj_a, v_w_proj_b, v_b_gate, v_w_out, v_g_ffn, v_w_up, v_conv_w, v_conv_b, v_w_down, v_g_final):
    given = dict(locals())
    wts = {n: given[n] for n in WEIGHT_NAMES}
    ms = {n: given["m_" + n] for n in WEIGHT_NAMES}
    vs = {n: given["v_" + n] for n in WEIGHT_NAMES}
    xi, yi, ci = _place()
    chip = 2 * xi + yi
    dev = 2 * chip + ci
    D = D_MODEL

    c8 = allgather_devices(jnp.pad(c, ((0, 7), (0, 0))), "gather_c")[:, 0, :]
    c16 = jnp.concatenate([c8, c_ctx[None], jnp.zeros((MOD_ROWS - 9, D), F32)], axis=0)
    b_shard = lax.dynamic_slice(b_ada, (0, chip * ADA_COLS), (1, ADA_COLS))
    mod_shard = mod_forward(c16, w_ada[0], b_shard)
    mod_all = allgather_devices(mod_shard, "gather_mod")
    mod_full = jnp.concatenate([mod_all[2 * q] for q in range(N_CHIPS)], axis=1)
    mod = lax.dynamic_slice(mod_full, (dev, 0), (1, N_MOD * D))
    modc = mod_full[8:9]

    big_names = [n for n, _, _ in BIG_SHARDS]
    conv_rows = jnp.pad(conv_w[0].reshape(9, CONV_SHARD), ((0, 7), (0, 0)))
    gathered = allgather_chips([wts[n][0].astype(BF16) for n in big_names] + [conv_rows], "gather_weights")
    W = dict(zip(big_names, gathered[:-1]))
    for n, shape, axis in BIG_SHARDS:
        if axis == 0:
            W[n] = W[n].reshape(N_CHIPS * shape[0], shape[1])
    for n in ("w_proj_a", "w_proj_b"):
        W[n] = jnp.transpose(W[n], (1, 0, 2)).reshape(W[n].shape[1], -1)
    W["conv_w"] = jnp.transpose(gathered[-1][:, :9], (1, 0, 2)).reshape(3, 3, 2 * FFN_HIDDEN)
    for n in ("g_mix", "g_ffn", "s5_d", "s5_b_glu", "sgu_ln_g", "sgu_ln_b", "b_gate", "conv_b"):
        W[n] = wts[n]
    W["g_final"] = g_final[None]
    for n in ("s5_a_re", "s5_a_im", "s5_log_step", "s5_b_re", "s5_b_im", "s5_c_re", "s5_c_im", "sgu_w", "sgu_b"):
        W[n] = wts[n][0]

    loss_part, grad_x, g = local_step(x[0], ctx[0], loss_target[0], mod, modc, W)
    loss = lax.psum(loss_part[0, 0], ("x", "y", "c"))

    g_slots = []
    for n, shape, axis in BIG_SHARDS:
        if n in ("w_proj_a", "w_proj_b"):
            g_slots.append(jnp.transpose(g[n].reshape(shape[0], N_CHIPS, shape[1]), (1, 0, 2)))
        else:
            g_slots.append(g[n].reshape((N_CHIPS,) + shape))
    g_slots = [t.reshape(N_CHIPS, 2, t.shape[1] // 2, t.shape[2]) for t in g_slots]
    core = ci.astype(jnp.int32).reshape(1)
    from_sibling = grad_pair_swap(g_slots, "grad_pair_swap")
    pair = [pair_sum(gs, rv, core, "grad_pair_sum_" + n) for gs, rv, n in zip(g_slots, from_sibling, big_names)]
    from_chips = grad_chip_exchange(pair, "grad_chip_exchange")
    add2 = lambda a, b: a + b
    add4 = lambda a, b, c_, d: ((a + b) + c_) + d
    halves = []
    for fc, n in zip(from_chips, big_names):
        r2, cols = fc.shape[1], fc.shape[2]
        tr = _tile(r2, 256, 8)
        halves.append(rowcall(add4, "grad_chip_sum_" + n, r2, tr,
                              [(fc.reshape(N_CHIPS * r2, cols), cols, 0, q * r2 // tr) for q in range(N_CHIPS)], [],
                              [(cols, F32)], [])[0])
    big_grads = dict(zip(big_names, grad_half_swap(halves, "grad_half_swap")))

    small_pack = _pack_rows([g["dmod"], g["dmodc"], g["conv_w"]] + [g[n] for n in SMALL_PARAMS], SMALL_PACK_ROWS, F32)
    small_all = allgather_devices(small_pack, "gather_small_grads")
    small_2d = small_all.reshape(N_DEV * SMALL_PACK_ROWS, PACK_COLS)

    def add8(*a):
        s = a[0]
        for t in a[1:]:
            s = s + t
        return s

    small_sum = rowcall(add8, "small_grad_sum", SMALL_PACK_ROWS, 256,
                        [(small_2d, PACK_COLS, 0, k * SMALL_PACK_ROWS // 256) for k in range(N_DEV)], [],
                        [(PACK_COLS, F32)], [])[0]
    dmod_all = small_all[:, 0:N_MOD].reshape(N_DEV, N_MOD * D)
    dmod_sum = small_sum[0:N_MOD].reshape(1, N_MOD * D)
    dmodc_sum = jnp.pad(small_sum[N_MOD:N_MOD + 2].reshape(1, 2 * D), ((0, 0), (0, (N_MOD - 2) * D)))
    conv_grad = _unpack_rows(small_sum, [(3, 3, 2 * FFN_HIDDEN)], row0=N_MOD + 2)[0]
    small_grads = dict(zip(SMALL_PARAMS, _unpack_rows(small_sum, [wts[n].shape for n in SMALL_PARAMS], row0=SMALL_ROW0)))

    dm16 = jnp.concatenate([dmod_all, dmodc_sum, jnp.zeros((MOD_ROWS - 9, N_MOD * D), F32)], axis=0)
    dm_shard = lax.dynamic_slice(dm16, (0, chip * ADA_COLS), (MOD_ROWS, ADA_COLS))
    g_w_ada = rowcall(f_ada_outer, "w_ada_grad", D, 256, [(jnp.transpose(c16), MOD_ROWS, 0, 0)], [dm_shard],
                      [(ADA_COLS, F32)], [])[0]
    g_b_ada = rowcall(add2, "b_ada_grad", 1, 1, [(dmod_sum, N_MOD * D, 0, 0), (dmodc_sum, N_MOD * D, 0, 0)], [],
                      [(N_MOD * D, F32)], [])[0]
    dmc_rows = jnp.pad(dm_shard[8:9], ((0, 7), (0, 0)))
    cctx_part = matmul(dmc_rows, w_ada[0], "nt", F32, "c_ctx_partial")
    cctx_all = allgather_devices(cctx_part, "gather_c_ctx")
    cctx_4 = jnp.stack([cctx_all[2 * q, 0] for q in range(N_CHIPS)])
    g_c_ctx = rowcall(f_cctx_grad, "c_ctx_grad", 1, 1, [(c_ctx[None], D, 0, 0)], [cctx_4], [(D, F32)], [])[0]

    grads = dict(small_grads)
    grads.update(big_grads)
    grads["w_ada"] = g_w_ada
    grads["b_ada"] = g_b_ada
    grads["c_ctx"] = g_c_ctx
    grads["conv_w"] = lax.dynamic_slice(conv_grad, (0, 0, chip * CONV_SHARD), (3, 3, CONV_SHARD))
    grads = {n: grads[n].reshape(wts[n].shape) for n in WEIGHT_NAMES}

    delta, new_m, new_v = {}, {}, {}
    large = [n for n, _, _ in BIG_SHARDS] + ["w_ada", "conv_w"]
    for n in large:
        shape2d = (-1, wts[n].shape[-1])
        d_, m_, v_ = adamw(wts[n].reshape(shape2d), grads[n].reshape(shape2d), ms[n].reshape(shape2d),
                           vs[n].reshape(shape2d), "adamw_" + n)
        delta[n], new_m[n], new_v[n] = [t.reshape(wts[n].shape) for t in (d_, m_, v_)]
    rest = [n for n in WEIGHT_NAMES if n not in large]
    packs = [_pack_rows([src[n] for n in rest], ADAM_PACK_ROWS, F32) for src in (wts, grads, ms, vs)]
    outs = adamw(*packs, "adamw_replicated")
    for dst, packed in zip((delta, new_m, new_v), outs):
        dst.update(zip(rest, _unpack_rows(packed, [wts[n].shape for n in rest])))

    return (loss, grad_x[None], *[grads[n] for n in WEIGHT_NAMES], *[delta[n] for n in WEIGHT_NAMES],
            *[new_m[n] for n in WEIGHT_NAMES], *[new_v[n] for n in WEIGHT_NAMES])
```

```python
import functools

import jax
import jax.numpy as jnp
from jax import lax
from jax.experimental import pallas as pl
from jax.experimental.pallas import tpu as pltpu

F32, BF16 = jnp.float32, jnp.bfloat16
MESH = pl.DeviceIdType.MESH

D_MODEL = 1024
S5_WIDTH = 512
S5_GROUP = 16
S5_GROUPS = 32
S5_STATE = 64
SGU_WIDTH = 512
SGU_GROUPS = 8
CHUNK = 128
FFN_HIDDEN = 2816
GRID_W = 64
N_MOD = 6
EPS = 1e-6
N_STATE = S5_GROUPS * S5_STATE
OCTETS = 4
SCAN_T = 128
N_CHIPS = 4
N_DEV = 8
LANES = 128
VMEM_LIMIT_BYTES = 56 * 1024 * 1024
CONV_PAD = 72
CONV_ROWS = 256

ADAM_LR, ADAM_B1, ADAM_B2, ADAM_EPS, ADAM_WD, ADAM_STEP = 0.001, 0.9, 0.999, 1e-08, 0.01, 10


def _call(body, name, out_shape, grid=None, in_specs=None, out_specs=None, scratch=(), sem=None, **kw):
    params = pltpu.CompilerParams(dimension_semantics=sem, vmem_limit_bytes=VMEM_LIMIT_BYTES)
    extra = {} if grid is None else {"grid": grid}
    return pl.pallas_call(body, name=name, out_shape=out_shape, in_specs=in_specs, out_specs=out_specs,
                          scratch_shapes=list(scratch), compiler_params=params, **extra, **kw)


def _tile(n, target, mult=LANES):
    best = None
    t = mult
    while t <= min(n, target):
        if n % t == 0:
            best = t
        t += mult
    return best or n


@jax.custom_vjp
def mmul(a, b):
    return jnp.dot(a.astype(BF16), b.astype(BF16), preferred_element_type=F32)


def _mmul_fwd(a, b):
    return mmul(a, b), (a, b)


def _mmul_bwd(res, ct):
    a, b = res
    ctb = ct.astype(BF16)
    da = lax.dot_general(ctb, b.astype(BF16), (((1,), (1,)), ((), ())), preferred_element_type=F32)
    db = lax.dot_general(a.astype(BF16), ctb, (((0,), (0,)), ((), ())), preferred_element_type=F32)
    return da.astype(a.dtype), db.astype(b.dtype)


mmul.defvjp(_mmul_fwd, _mmul_bwd)

_DOT_DIMS = {"nn": ((1,), (0,)), "nt": ((1,), (1,)), "tn": ((0,), (0,))}


MM_TILE = 1408
MM_FULL_K = 2048


def matmul(a, b, mode, out_dtype, name, init=None, shards=False):
    if mode == "nn":
        (M, K), N = a.shape, (b.shape[2] * N_CHIPS if shards else b.shape[1])
    elif mode == "nt":
        M, N, K = a.shape[0], b.shape[-2], a.shape[1]
    else:
        (K, M), N = a.shape, b.shape[1]
    ns = (K if mode == "nt" else N) // N_CHIPS
    tm = _tile(M, MM_TILE, 8 if M < LANES else LANES)
    tn = _tile(ns if shards and mode != "nt" else N, MM_TILE)
    if shards and mode == "nt":
        tk = _tile(ns, MM_TILE)
    else:
        tk = K if K <= MM_FULL_K else _tile(K, MM_TILE)
    nk = K // tk
    per = ns // (tk if mode == "nt" else tn)
    dims = (_DOT_DIMS[mode], ((), ()))
    has_init = init is not None
    use_acc = nk > 1 and out_dtype != F32

    def body(*refs):
        a_ref, b_ref = refs[:2]
        i_ref = refs[2] if has_init else None
        o_ref = refs[3] if has_init else refs[2]
        acc = refs[-1] if use_acc else o_ref
        k = pl.program_id(2)
        part = lax.dot_general(a_ref[...].astype(BF16), b_ref[...].astype(BF16), dims, preferred_element_type=F32)

        @pl.when(k == 0)
        def _():
            first = part + i_ref[...].astype(F32) if has_init else part
            acc[...] = first.astype(acc.dtype)

        if nk > 1:
            @pl.when(k > 0)
            def _():
                acc[...] += part

        if use_acc:
            @pl.when(k == nk - 1)
            def _():
                o_ref[...] = acc[...].astype(o_ref.dtype)

    if mode == "tn":
        a_spec = pl.BlockSpec((tk, tm), lambda i, j, k: (k, i))
    else:
        a_spec = pl.BlockSpec((tm, tk), lambda i, j, k: (i, k))
    if mode == "nt":
        b_spec = (pl.BlockSpec((None, tn, tk), lambda i, j, k: (k // per, j, k % per)) if shards
                  else pl.BlockSpec((tn, tk), lambda i, j, k: (j, k)))
    else:
        b_spec = (pl.BlockSpec((None, tk, tn), lambda i, j, k: (j // per, k, j % per)) if shards and mode == "nn"
                  else pl.BlockSpec((tk, tn), lambda i, j, k: (k, j)))
    if shards and mode == "tn":
        o_spec = pl.BlockSpec((None, tm, tn), lambda i, j, k: (j // per, i, j % per))
        out_shape = jax.ShapeDtypeStruct((N_CHIPS, M, ns), out_dtype)
    else:
        o_spec = pl.BlockSpec((tm, tn), lambda i, j, k: (i, j))
        out_shape = jax.ShapeDtypeStruct((M, N), out_dtype)
    in_specs = [a_spec, b_spec] + ([o_spec] if has_init else [])
    args = (a, b) + ((init,) if has_init else ())
    return _call(body, name, out_shape, grid=(M // tm, N // tn, nk),
                 in_specs=in_specs, out_specs=o_spec, scratch=[pltpu.VMEM((tm, tn), F32)] if use_acc else [],
                 sem=("parallel", "parallel", "arbitrary"))(*args)


def rowcall(fn, name, nrows, tm, rins, vins, routs, aouts, ainit=None):
    n_r, n_v, n_ro = len(rins), len(vins), len(routs)
    n_i = len(aouts) if ainit is not None else 0

    def body(*refs):
        r_in, v_in, i_in = refs[:n_r], refs[n_r:n_r + n_v], refs[n_r + n_v:n_r + n_v + n_i]
        r_out, a_out = refs[n_r + n_v + n_i:n_r + n_v + n_i + n_ro], refs[n_r + n_v + n_i + n_ro:]
        outs = fn(*[r[...].astype(F32) for r in r_in], *[v[...] for v in v_in])
        if not isinstance(outs, (tuple, list)):
            outs = (outs,)
        for ref, val in zip(r_out, outs[:n_ro]):
            ref[...] = val.astype(ref.dtype)
        if a_out:
            @pl.when(pl.program_id(0) == 0)
            def _():
                for k, ref in enumerate(a_out):
                    ref[...] = i_in[k][...] if n_i else jnp.zeros_like(ref)

            for ref, val in zip(a_out, outs[n_ro:]):
                ref[...] += val.astype(F32)

    def rspec(width, cblk, roff):
        return pl.BlockSpec((tm, width), lambda i: (i + roff, cblk))

    def whole(shape):
        nd = len(shape)
        return pl.BlockSpec(tuple(shape), lambda i: (0,) * nd)

    inits = list(ainit) if n_i else []
    in_specs = [rspec(w, cb, ro) for (_, w, cb, ro) in rins] + [whole(v.shape) for v in vins + inits]
    out_specs = [rspec(w, 0, 0) for (w, _) in routs] + [whole(s) for s in aouts]
    out_shape = [jax.ShapeDtypeStruct((nrows, w), dt) for (w, dt) in routs] + \
                [jax.ShapeDtypeStruct(tuple(s), F32) for s in aouts]
    res = _call(body, name, out_shape, grid=(nrows // tm,), in_specs=in_specs, out_specs=out_specs,
                sem=("arbitrary",))(*[r[0] for r in rins], *vins, *inits)
    return res


def _rms(x):
    return lax.rsqrt(jnp.mean(x * x, axis=-1, keepdims=True) + EPS)


def f_modulate(x, g, sc, sh):
    return (x * _rms(x)) * g * (1.0 + sc) + sh


def f_resid_mod(x, o, ga, g, sc, sh):
    x1 = x + ga * o
    return x1, f_modulate(x1, g, sc, sh)


def f_final_loss(x1, dn, tgt, ga2, gf):
    x2 = x1 + ga2 * dn
    y = (x2 * _rms(x2)) * gf
    err = jnp.square(y - tgt)
    return 0.5 * jnp.sum(jnp.mean(err, axis=-1))


def _sgu_spatial(vn, w, bt):
    lo = lax.broadcasted_iota(jnp.int32, (1, LANES), 1) < (SGU_WIDTH // SGU_GROUPS)
    row_blocks = []
    for r in range(vn.shape[0] // CHUNK):
        rows = vn[r * CHUNK:(r + 1) * CHUNK]
        cols = []
        for j in range(SGU_WIDTH // LANES):
            blk = rows[:, j * LANES:(j + 1) * LANES]
            v_lo = jnp.where(lo, blk, 0.0)
            v_hi = jnp.where(lo, 0.0, blk)
            s = mmul(w[2 * j], v_lo) + mmul(w[2 * j + 1], v_hi)
            bias = jnp.where(lo, bt[:, 2 * j:2 * j + 1], bt[:, 2 * j + 1:2 * j + 2])
            cols.append(s + bias)
        row_blocks.append(jnp.concatenate(cols, axis=1))
    return jnp.concatenate(row_blocks, axis=0) if len(row_blocks) > 1 else row_blocks[0]


def f_mixer(u_a, y0, y1, zu, zv, ga0, ga1, gb0, gb1, d_skip, w_glu, b_glu, ln_g, ln_b, sgu_w, sgu_bt,
            w_pa, w_pb, b_gate):
    ys = u_a * d_skip + y0 + y1
    ge = jax.nn.gelu(ys)
    y_a = ge * jax.nn.sigmoid(mmul(ge, w_glu) + b_glu)
    u_sg = jax.nn.gelu(zu)
    v = jax.nn.gelu(zv)
    vc = v - jnp.mean(v, axis=-1, keepdims=True)
    vn = (vc * lax.rsqrt(jnp.mean(vc * vc, axis=-1, keepdims=True) + EPS)) * ln_g + ln_b
    y_b = u_sg * _sgu_spatial(vn, sgu_w, sgu_bt)
    gl_a = jnp.concatenate([ga0, ga1], axis=1) + b_gate[:, :D_MODEL]
    gl_b = jnp.concatenate([gb0, gb1], axis=1) + b_gate[:, D_MODEL:]
    return jax.nn.sigmoid(gl_a) * mmul(y_a, w_pa) + jax.nn.sigmoid(gl_b) * mmul(y_b, w_pb)


def _cmul(ar, ai, xr, xi):
    return ar * xr - ai * xi, ar * xi + ai * xr


SUB = 8
STRAND = SCAN_T // SUB


def _strand_perms():
    i = lax.broadcasted_iota(jnp.int32, (SCAN_T, SCAN_T), 0)
    j = lax.broadcasted_iota(jnp.int32, (SCAN_T, SCAN_T), 1)
    to_strands = jnp.where(j == STRAND * (i % SUB) + i // SUB, 1.0, 0.0).astype(BF16)
    to_tokens = jnp.where(i == STRAND * (j % SUB) + j // SUB, 1.0, 0.0).astype(BF16)
    return to_strands, to_tokens


def _permute(perm, v):
    return jnp.dot(perm, v, preferred_element_type=F32).astype(BF16)


def _scan_strands(xr, xi, pw_ref, q_ref, col, rev, conj, cr, ci):
    def tab(ref, lo):
        t_r = ref[lo:lo + SUB, col:col + LANES]
        t_i = ref[lo:lo + SUB, col + 512:col + 512 + LANES]
        return t_r, (-t_i if conj else t_i)

    a_r, a_i = tab(pw_ref, (STRAND - 1) * SUB if rev else 0)
    order = list(range(STRAND - 1, -1, -1) if rev else range(STRAND))
    lr, li = [None] * STRAND, [None] * STRAND
    for n, k in enumerate(order):
        lr[k], li[k] = xr[k * SUB:(k + 1) * SUB], xi[k * SUB:(k + 1) * SUB]
        if n:
            m_r, m_i = _cmul(a_r, a_i, lr[order[n - 1]], li[order[n - 1]])
            lr[k], li[k] = lr[k] + m_r, li[k] + m_i
    f_r, f_i = lr[order[-1]], li[order[-1]]
    q_r, q_i = tab(q_ref, 0)
    sub = lax.broadcasted_iota(jnp.int32, (SUB, 1), 0)
    s = 1
    while s < SUB:
        row = (SUB - s) if rev else (s - 1)
        shift = (SUB - s) if rev else s
        m = (sub < SUB - s) if rev else (sub >= s)
        p_r, p_i = _cmul(q_r[row:row + 1], q_i[row:row + 1], pltpu.roll(f_r, shift, 0), pltpu.roll(f_i, shift, 0))
        f_r, f_i = f_r + jnp.where(m, p_r, 0.0), f_i + jnp.where(m, p_i, 0.0)
        s *= 2
    c_r, c_i = jnp.broadcast_to(cr, (SUB, LANES)), jnp.broadcast_to(ci, (SUB, LANES))
    p_r, p_i = _cmul(q_r, q_i, c_r, c_i)
    s_r, s_i = f_r + p_r, f_i + p_i
    edge = 0 if rev else SUB - 1
    first = sub == (SUB - 1 if rev else 0)
    e_r = jnp.where(first, c_r, pltpu.roll(s_r, SUB - 1 if rev else 1, 0))
    e_i = jnp.where(first, c_i, pltpu.roll(s_i, SUB - 1 if rev else 1, 0))
    for k in range(STRAND):
        t_r, t_i = tab(pw_ref, k * SUB)
        p_r, p_i = _cmul(t_r, t_i, e_r, e_i)
        lr[k], li[k] = lr[k] + p_r, li[k] + p_i
    return lr, li, (s_r[edge:edge + 1], s_i[edge:edge + 1]), (e_r, e_i)


def s5_forward(u3, bm, cm, pw, q, rev, n_chunks, blk0, name):
    T = SCAN_T

    def pos(i):
        return (n_chunks - 1 - i) if rev else i

    def body(u_ref, bm_ref, cm_ref, pw_ref, q_ref, y_ref, cin_ref, carry):
        @pl.when(pl.program_id(0) == 0)
        def _():
            carry[...] = jnp.zeros_like(carry)

        cin_ref[...] = carry[...]
        to_strands, to_tokens = _strand_perms()
        u = _permute(to_strands, u_ref[...])
        for o in range(OCTETS):
            bu = jnp.dot(u[:, o * LANES:(o + 1) * LANES], bm_ref[o], preferred_element_type=F32)
            hr, hi = [], []
            for j in range(4):
                col = o * 1024 + j * LANES
                xr, xi, (cr, ci), _ = _scan_strands(
                    bu[:, j * LANES:(j + 1) * LANES], bu[:, 512 + j * LANES:512 + (j + 1) * LANES], pw_ref, q_ref, col,
                    rev, False, cin_ref[0:1, col:col + LANES], cin_ref[0:1, col + 512:col + 512 + LANES])
                carry[0:1, col:col + LANES] = cr
                carry[0:1, col + 512:col + 512 + LANES] = ci
                hr.append(jnp.concatenate(xr, axis=0))
                hi.append(jnp.concatenate(xi, axis=0))
            h = _permute(to_tokens, jnp.concatenate(hr + hi, axis=1).astype(BF16))
            y_ref[:, o * LANES:(o + 1) * LANES] = jnp.dot(h, cm_ref[o], preferred_element_type=F32)

    whole3 = lambda s: pl.BlockSpec(s, lambda i: (0, 0, 0))
    whole2 = lambda s: pl.BlockSpec(s, lambda i: (0, 0))
    return _call(
        body, name,
        [jax.ShapeDtypeStruct((n_chunks * T, S5_WIDTH), F32), jax.ShapeDtypeStruct((n_chunks, 1, 2 * N_STATE), F32)],
        grid=(n_chunks,),
        in_specs=[pl.BlockSpec((T, S5_WIDTH), lambda i: (blk0 + pos(i), 0)), whole3(bm.shape), whole3(cm.shape),
                  whole2(pw.shape), whole2(q.shape)],
        out_specs=[pl.BlockSpec((T, S5_WIDTH), lambda i: (pos(i), 0)),
                   pl.BlockSpec((None, 1, 2 * N_STATE), lambda i: (pos(i), 0, 0))],
        scratch=[pltpu.VMEM((1, 2 * N_STATE), F32)], sem=("arbitrary",))(u3, bm, cm, pw, q)


def s5_backward(u3, dy3, cin, bm, cm, pw_h, q_h, pw_l, q_l, rev, n_chunks, blk0, name):
    T = SCAN_T

    def pos(i):
        return i if rev else (n_chunks - 1 - i)

    def body(u_ref, dy_ref, cin_ref, bm_ref, cm_ref, pwh_ref, qh_ref, pwl_ref, ql_ref, du_ref, dbm_ref, dcm_ref,
             da_ref, lcarry):
        @pl.when(pl.program_id(0) == 0)
        def _():
            lcarry[...] = jnp.zeros_like(lcarry)
            dbm_ref[...] = jnp.zeros_like(dbm_ref)
            dcm_ref[...] = jnp.zeros_like(dcm_ref)
            da_ref[...] = jnp.zeros_like(da_ref)

        to_strands, to_tokens = _strand_perms()
        u = _permute(to_strands, u_ref[...])
        dy = _permute(to_strands, dy_ref[...])
        for o in range(OCTETS):
            u_o = u[:, o * LANES:(o + 1) * LANES]
            dy_o = dy[:, o * LANES:(o + 1) * LANES]
            bu = jnp.dot(u_o, bm_ref[o], preferred_element_type=F32)
            g = lax.dot_general(dy_o, cm_ref[o], (((1,), (1,)), ((), ())), preferred_element_type=F32)
            hr, hi, lr, li = [], [], [], []
            for j in range(4):
                col = o * 1024 + j * LANES
                sl_r = slice(j * LANES, (j + 1) * LANES)
                sl_i = slice(512 + j * LANES, 512 + (j + 1) * LANES)
                xr, xi, _, (e_r, e_i) = _scan_strands(
                    bu[:, sl_r], bu[:, sl_i], pwh_ref, qh_ref, col, rev, False,
                    cin_ref[0:1, col:col + LANES], cin_ref[0:1, col + 512:col + 512 + LANES])
                ar_, ai_, (l_r, l_i), _ = _scan_strands(
                    g[:, sl_r], g[:, sl_i], pwl_ref, ql_ref, col, not rev, True,
                    lcarry[0:1, col:col + LANES], lcarry[0:1, col + 512:col + 512 + LANES])
                lcarry[0:1, col:col + LANES] = l_r
                lcarry[0:1, col + 512:col + 512 + LANES] = l_i
                acc_r = acc_i = None
                for k in range(STRAND):
                    kp = k + 1 if rev else k - 1
                    p_r, p_i = (e_r, e_i) if not 0 <= kp < STRAND else (xr[kp], xi[kp])
                    t_r = ar_[k] * p_r + ai_[k] * p_i
                    t_i = ai_[k] * p_r - ar_[k] * p_i
                    acc_r, acc_i = (t_r, t_i) if acc_r is None else (acc_r + t_r, acc_i + t_i)
                da_ref[0:1, col:col + LANES] += jnp.sum(acc_r, axis=0, keepdims=True)
                da_ref[0:1, col + 512:col + 512 + LANES] += jnp.sum(acc_i, axis=0, keepdims=True)
                hr.append(jnp.concatenate(xr, axis=0))
                hi.append(jnp.concatenate(xi, axis=0))
                lr.append(jnp.concatenate(ar_, axis=0))
                li.append(jnp.concatenate(ai_, axis=0))
            h = jnp.concatenate(hr + hi, axis=1).astype(BF16)
            lam = jnp.concatenate(lr + li, axis=1).astype(BF16)
            du_ref[:, o * LANES:(o + 1) * LANES] = lax.dot_general(
                _permute(to_tokens, lam), bm_ref[o], (((1,), (1,)), ((), ())), preferred_element_type=F32)
            dbm_ref[o] += lax.dot_general(u_o, lam, (((0,), (0,)), ((), ())), preferred_element_type=F32)
            dcm_ref[o] += lax.dot_general(h, dy_o, (((0,), (0,)), ((), ())), preferred_element_type=F32)

    whole3 = lambda s: pl.BlockSpec(s, lambda i: (0, 0, 0))
    whole2 = lambda s: pl.BlockSpec(s, lambda i: (0, 0))
    row_spec = pl.BlockSpec((T, S5_WIDTH), lambda i: (blk0 + pos(i), 0))
    return _call(
        body, name,
        [jax.ShapeDtypeStruct((n_chunks * T, S5_WIDTH), F32), jax.ShapeDtypeStruct(bm.shape, F32),
         jax.ShapeDtypeStruct(cm.shape, F32), jax.ShapeDtypeStruct((1, 2 * N_STATE), F32)],
        grid=(n_chunks,),
        in_specs=[row_spec, row_spec, pl.BlockSpec((None, 1, 2 * N_STATE), lambda i: (pos(i), 0, 0)),
                  whole3(bm.shape), whole3(cm.shape), whole2(pw_h.shape), whole2(q_h.shape), whole2(pw_l.shape),
                  whole2(q_l.shape)],
        out_specs=[pl.BlockSpec((T, S5_WIDTH), lambda i: (pos(i), 0)), whole3(bm.shape), whole3(cm.shape),
                   whole2((1, 2 * N_STATE))],
        scratch=[pltpu.VMEM((1, 2 * N_STATE), F32)], sem=("arbitrary",))(u3, dy3, cin, bm, cm, pw_h, q_h, pw_l, q_l)


def s5_tables(ar, ai, ls):
    def body(ar_ref, ai_ref, ls_ref, *outs):
        dt = jnp.exp(ls_ref[...])
        k = 0
        for n_rows, step in ((STRAND, 1.0), (SUB, float(STRAND))):
            row = lax.broadcasted_iota(jnp.int32, (n_rows, 1), 0)
            for m_int in (row + 1, n_rows - row):
                m = m_int.astype(F32) * step
                mag = jnp.exp(m * (ar_ref[...] * dt))
                ang = m * (ai_ref[...] * dt)
                outs[k][...] = mag * jnp.cos(ang)
                outs[k + 1][...] = mag * jnp.sin(ang)
                k += 2

    vec = pl.BlockSpec((None, 1, N_STATE), lambda d: (d, 0, 0))
    tab = lambda n: pl.BlockSpec((None, n, N_STATE), lambda d: (d, 0, 0))
    shp = lambda n: jax.ShapeDtypeStruct((2, n, N_STATE), F32)
    sizes = [STRAND] * 4 + [SUB] * 4
    return _call(body, "s5_tables", [shp(n) for n in sizes], grid=(2,), in_specs=[vec, vec, vec],
                 out_specs=[tab(n) for n in sizes], sem=("arbitrary",))(ar, ai, ls)


def f_discretize(a_re, a_im, ls, b_re, b_im):
    dt = jnp.exp(ls)
    mag = jnp.exp(a_re * dt)
    ab_re = mag * jnp.cos(a_im * dt)
    ab_im = mag * jnp.sin(a_im * dt)
    p = ab_re - 1.0
    q = ab_im
    den = a_re * a_re + a_im * a_im
    k_re = ((p * a_re + q * a_im) / den)[None]
    k_im = ((q * a_re - p * a_im) / den)[None]
    return ab_re, ab_im, k_re * b_re - k_im * b_im, k_re * b_im + k_im * b_re


def _disc_specs():
    a = pl.BlockSpec((None, S5_GROUPS, S5_STATE), lambda d: (d, 0, 0))
    s = pl.BlockSpec((None, S5_GROUPS, 1), lambda d: (d, 0, 0))
    b = pl.BlockSpec((None, S5_GROUP, S5_GROUPS, S5_STATE), lambda d: (d, 0, 0, 0))
    return a, s, b


def s5_discretize(a_re, a_im, ls, b_re, b_im):
    def body(ar, ai, l, br, bi, obr, obi):
        _, _, r, i = f_discretize(ar[...], ai[...], l[...], br[...], bi[...])
        obr[...] = r
        obi[...] = i

    a, s, b = _disc_specs()
    return _call(body, "s5_discretize", [jax.ShapeDtypeStruct(b_re.shape, F32)] * 2, grid=(2,),
                 in_specs=[a, a, s, b, b], out_specs=[b, b], sem=("arbitrary",))(a_re, a_im, ls, b_re, b_im)


def s5_discretize_bwd(a_re, a_im, ls, b_re, b_im, dab_re, dab_im, dbb_re, dbb_im):
    def body(ar, ai, l, br, bi, c0, c1, c2, c3, o0, o1, o2, o3, o4):
        _, vjp = jax.vjp(f_discretize, ar[...], ai[...], l[...], br[...], bi[...])
        outs = vjp((c0[...], c1[...], c2[...], c3[...]))
        for ref, val in zip((o0, o1, o2, o3, o4), outs):
            ref[...] = val

    a, s, b = _disc_specs()
    shapes = [jax.ShapeDtypeStruct(t.shape, F32) for t in (a_re, a_im, ls, b_re, b_im)]
    return _call(body, "s5_discretize_bwd", shapes, grid=(2,), in_specs=[a, a, s, b, b, a, a, b, b],
                 out_specs=[a, a, s, b, b], sem=("arbitrary",))(a_re, a_im, ls, b_re, b_im, dab_re, dab_im,
                                                                 dbb_re, dbb_im)


def _conv_taps(s_ref, base, n_rows):
    n = n_rows + 2 * CONV_PAD
    ext = s_ref[pl.ds(base, n), :]
    col = (lax.broadcasted_iota(jnp.int32, (n, 1), 0) + (2 * GRID_W - CONV_PAD)) % GRID_W
    left = jnp.where(col == 0, 0.0, pltpu.roll(ext, 1, 0))
    right = jnp.where(col == GRID_W - 1, 0.0, pltpu.roll(ext, n - 1, 0))
    return left, ext, right


def _conv_apply(taps, w_ref, n_rows, flip):
    out = None
    for i in range(3):
        wi = 2 - i if flip else i
        comb = None
        for j in range(3):
            wj = 2 - j if flip else j
            term = w_ref[wi * 3 + wj:wi * 3 + wj + 1, :] * taps[j]
            comb = term if comb is None else comb + term
        start = CONV_PAD + (i - 1) * GRID_W
        part = comb[start:start + n_rows]
        out = part if out is None else out + part
    return out


def _conv_fill(dst_ref, src_ref, n_tok):
    zeros = jnp.zeros((CONV_PAD, LANES), F32)
    dst_ref[0:CONV_PAD, :] = zeros
    dst_ref[CONV_PAD + n_tok:2 * CONV_PAD + n_tok, :] = zeros

    def step(r, carry):
        base = pl.multiple_of(r * CONV_ROWS, CONV_ROWS)
        dst_ref[pl.ds(base + CONV_PAD, CONV_ROWS), :] = src_ref[pl.ds(base, CONV_ROWS), :].astype(F32)
        return carry

    lax.fori_loop(0, n_tok // CONV_ROWS, step, 0)


def conv_forward(up, wg, wv, bias):
    n_tok = up.shape[0]
    nb = FFN_HIDDEN // LANES

    def body(ug_ref, uv_ref, wg_ref, wv_ref, bg_ref, bv_ref, act_ref, gate_ref, val_ref, sg, sv):
        _conv_fill(sg, ug_ref, n_tok)
        _conv_fill(sv, uv_ref, n_tok)

        def step(r, carry):
            base = pl.multiple_of(r * CONV_ROWS, CONV_ROWS)
            gate = _conv_apply(_conv_taps(sg, base, CONV_ROWS), wg_ref, CONV_ROWS, False) + bg_ref[...]
            val = _conv_apply(_conv_taps(sv, base, CONV_ROWS), wv_ref, CONV_ROWS, False) + bv_ref[...]
            act_ref[pl.ds(base, CONV_ROWS), :] = (gate * jax.nn.sigmoid(gate) * val).astype(BF16)
            gate_ref[pl.ds(base, CONV_ROWS), :] = gate.astype(BF16)
            val_ref[pl.ds(base, CONV_ROWS), :] = val.astype(BF16)
            return carry

        lax.fori_loop(0, n_tok // CONV_ROWS, step, 0)

    col = lambda off: pl.BlockSpec((n_tok, LANES), lambda k: (0, k + off))
    wsp = lambda off: pl.BlockSpec((16, LANES), lambda k: (0, k + off))
    bsp = lambda off: pl.BlockSpec((1, LANES), lambda k: (0, k + off))
    pad = pltpu.VMEM((n_tok + 2 * CONV_PAD, LANES), F32)
    half = jax.ShapeDtypeStruct((n_tok, FFN_HIDDEN), BF16)
    return _call(body, "conv_forward", [half, half, half], grid=(nb,),
                 in_specs=[col(0), col(nb), wsp(0), wsp(0), bsp(0), bsp(nb)], out_specs=[col(0), col(0), col(0)],
                 scratch=[pad, pad], sem=("arbitrary",))(up, up, wg, wv, bias, bias)


def conv_backward(up, gate, val, dact, wg, wv):
    n_tok = up.shape[0]
    nb = FFN_HIDDEN // LANES
    n_steps = n_tok // CONV_ROWS

    def body(ug_ref, uv_ref, gate_ref, val_ref, da_ref, wg_ref, wv_ref, dug_ref, duv_ref, dwg_ref, dwv_ref,
             sg, sv, sdg, sdv):
        _conv_fill(sg, ug_ref, n_tok)
        _conv_fill(sv, uv_ref, n_tok)
        zeros = jnp.zeros((CONV_PAD, LANES), F32)
        for s_ref in (sdg, sdv):
            s_ref[0:CONV_PAD, :] = zeros
            s_ref[CONV_PAD + n_tok:2 * CONV_PAD + n_tok, :] = zeros
        dwg_ref[...] = jnp.zeros_like(dwg_ref)
        dwv_ref[...] = jnp.zeros_like(dwv_ref)

        def grads(r, carry):
            base = pl.multiple_of(r * CONV_ROWS, CONV_ROWS)
            taps_g = _conv_taps(sg, base, CONV_ROWS)
            taps_v = _conv_taps(sv, base, CONV_ROWS)
            gate = gate_ref[pl.ds(base, CONV_ROWS), :].astype(F32)
            val = val_ref[pl.ds(base, CONV_ROWS), :].astype(F32)
            d_act = da_ref[pl.ds(base, CONV_ROWS), :].astype(F32)
            sig = jax.nn.sigmoid(gate)
            d_gate = d_act * val * (sig * (1.0 + gate * (1.0 - sig)))
            d_val = d_act * (gate * sig)
            sdg[pl.ds(base + CONV_PAD, CONV_ROWS), :] = d_gate
            sdv[pl.ds(base + CONV_PAD, CONV_ROWS), :] = d_val
            for d_out, taps, dw_ref in ((d_gate, taps_g, dwg_ref), (d_val, taps_v, dwv_ref)):
                for i in range(3):
                    start = CONV_PAD + (i - 1) * GRID_W
                    for j in range(3):
                        k = i * 3 + j
                        dw_ref[k:k + 1, :] += jnp.sum(d_out * taps[j][start:start + CONV_ROWS], axis=0, keepdims=True)
                dw_ref[9:10, :] += jnp.sum(d_out, axis=0, keepdims=True)
            return carry

        lax.fori_loop(0, n_steps, grads, 0)

        def spread(r, carry):
            base = pl.multiple_of(r * CONV_ROWS, CONV_ROWS)
            dug_ref[pl.ds(base, CONV_ROWS), :] = _conv_apply(
                _conv_taps(sdg, base, CONV_ROWS), wg_ref, CONV_ROWS, True).astype(BF16)
            duv_ref[pl.ds(base, CONV_ROWS), :] = _conv_apply(
                _conv_taps(sdv, base, CONV_ROWS), wv_ref, CONV_ROWS, True).astype(BF16)
            return carry

        lax.fori_loop(0, n_steps, spread, 0)

    col = lambda off: pl.BlockSpec((n_tok, LANES), lambda k: (0, k + off))
    wsp = pl.BlockSpec((16, LANES), lambda k: (0, k))
    pad = pltpu.VMEM((n_tok + 2 * CONV_PAD, LANES), F32)
    half = jax.ShapeDtypeStruct((n_tok, FFN_HIDDEN), BF16)
    dw = jax.ShapeDtypeStruct((16, FFN_HIDDEN), F32)
    return _call(body, "conv_backward", [half, half, dw, dw], grid=(nb,),
                 in_specs=[col(0), col(nb), col(0), col(0), col(0), wsp, wsp],
                 out_specs=[col(0), col(0), wsp, wsp], scratch=[pad, pad, pad, pad],
                 sem=("arbitrary",))(up, up, gate, val, dact, wg, wv)


def f_adamw(w, g, m, v):
    m = ADAM_B1 * m + (1.0 - ADAM_B1) * g
    v = ADAM_B2 * v + (1.0 - ADAM_B2) * jnp.square(g)
    m_hat = m / (1.0 - ADAM_B1 ** ADAM_STEP)
    v_hat = v / (1.0 - ADAM_B2 ** ADAM_STEP)
    delta = -ADAM_LR * (m_hat / (jnp.sqrt(v_hat) + ADAM_EPS) + ADAM_WD * w)
    return delta, m, v


def adamw(w, g, m, v, name):
    shape = w.shape
    cols = shape[-1]
    rows = w.size // cols
    two_d = [t.reshape(rows, cols) for t in (w, g, m, v)]
    tm = _tile(rows, 256, 8) if rows % 8 == 0 else rows
    outs = rowcall(f_adamw, name, rows, tm, [(t, cols, 0, 0) for t in two_d], [], [(cols, F32)] * 3, [])
    return tuple(o.reshape(shape) for o in outs)


def _place():
    return lax.axis_index("x"), lax.axis_index("y"), lax.axis_index("c")


_ANY = pl.BlockSpec(memory_space=pl.ANY)


def _fill_own(gathered, own, index):
    return lax.dynamic_update_slice(gathered, own[None], (index,) + (0,) * own.ndim)


def allgather_devices(v, name, copy_own=True):
    def body(v_ref, out_ref, send_sems, recv_sems, local_sem):
        x, y, c = _place()
        me, sibling = (x, y, c), (x, y, 1 - c)
        chips = [(1 - x, y), (x, 1 - y), (1 - x, 1 - y)]

        def slot(p):
            return out_ref.at[4 * p[0] + 2 * p[1] + p[2]]

        def copy(k, block, to, src=None):
            return pltpu.make_async_remote_copy(
                src_ref=slot(block) if src is None else src, dst_ref=slot(block),
                send_sem=send_sems.at[k], recv_sem=recv_sems.at[k], device_id=to, device_id_type=MESH)

        mine = pltpu.make_async_copy(v_ref, slot(me), local_sem)
        if copy_own:
            mine.start()
        first = [copy(0, me, sibling, src=v_ref)]
        first += [copy(1 + j, me, (*chip, c), src=v_ref) for j, chip in enumerate(chips)]
        for cp in first:
            cp.start()
        passed = [copy(4 + j, (*chip, c), sibling) for j, chip in enumerate(chips)]
        for j, chip in enumerate(chips):
            copy(1 + j, (*chip, c), me).wait_recv()
            passed[j].start()
        copy(0, sibling, me).wait_recv()
        for j, chip in enumerate(chips):
            copy(4 + j, (*chip, 1 - c), me).wait_recv()
        for cp in first + passed:
            cp.wait_send()
        if copy_own:
            mine.wait()

    return _call(body, name, jax.ShapeDtypeStruct((N_DEV,) + v.shape, v.dtype), in_specs=[_ANY], out_specs=_ANY,
                 scratch=[pltpu.SemaphoreType.DMA((7,)), pltpu.SemaphoreType.DMA((7,)), pltpu.SemaphoreType.DMA])(v)


def _other_chips(x, y):
    return [(1 - x, y), (x, 1 - y), (1 - x, 1 - y)]


def allgather_chips(vs, name):
    n = len(vs)
    shapes = [v.shape for v in vs]
    vs = [v.reshape((2, v.shape[0] // 2) + v.shape[1:]) for v in vs]

    def body(*refs):
        v_refs, o_refs = refs[:n], refs[n:2 * n]
        send_sems, recv_sems = refs[2 * n:]
        x, y, c = _place()
        sibling = (x, y, 1 - c)
        chips = _other_chips(x, y)

        def rows(a, chip, h):
            return o_refs[a].at[2 * chip[0] + chip[1], h]

        def copy(a, k, chip, h, to, src=None):
            return pltpu.make_async_remote_copy(
                src_ref=rows(a, chip, h) if src is None else src, dst_ref=rows(a, chip, h),
                send_sem=send_sems.at[6 * a + k], recv_sem=recv_sems.at[6 * a + k], device_id=to, device_id_type=MESH)

        first = [copy(a, j, (x, y), c, (*chip, c), src=v_refs[a].at[c])
                 for a in range(n) for j, chip in enumerate(chips)]
        for cp in first:
            cp.start()
        passed = []
        for j, chip in enumerate(chips):
            for a in range(n):
                copy(a, j, chip, c, (x, y, c)).wait_recv()
                passed.append(copy(a, 3 + j, chip, c, sibling))
                passed[-1].start()
        for j, chip in enumerate(chips):
            for a in range(n):
                copy(a, 3 + j, chip, 1 - c, (x, y, c)).wait_recv()
        for cp in first + passed:
            cp.wait_send()

    outs = _call(body, name, [jax.ShapeDtypeStruct((N_CHIPS,) + v.shape, v.dtype) for v in vs], in_specs=[_ANY] * n,
                 out_specs=[_ANY] * n,
                 scratch=[pltpu.SemaphoreType.DMA((6 * n,)), pltpu.SemaphoreType.DMA((6 * n,))])(*vs)
    return [o.reshape((N_CHIPS,) + s) for o, s in zip(outs, shapes)]


def grad_pair_swap(gs, name):
    n = len(gs)

    def body(*refs):
        g_refs, o_refs, send_sems, recv_sems = refs[:n], refs[n:2 * n], refs[2 * n], refs[2 * n + 1]
        x, y, c = _place()
        cps = [pltpu.make_async_remote_copy(
            src_ref=g_refs[a].at[:, 1 - c], dst_ref=o_refs[a], send_sem=send_sems.at[a], recv_sem=recv_sems.at[a],
            device_id=(x, y, 1 - c), device_id_type=MESH) for a in range(n)]
        for cp in cps:
            cp.start()
        for cp in cps:
            cp.wait()

    return _call(body, name, [jax.ShapeDtypeStruct((N_CHIPS,) + g.shape[2:], g.dtype) for g in gs],
                 in_specs=[_ANY] * n, out_specs=[_ANY] * n,
                 scratch=[pltpu.SemaphoreType.DMA((n,)), pltpu.SemaphoreType.DMA((n,))])(*gs)


def pair_sum(g, recv, core, name):
    r2, cols = recv.shape[1], recv.shape[2]
    tr = _tile(r2, 256, 8)
    nt = r2 // tr

    def body(c_ref, g_ref, r_ref, o_ref):
        o_ref[...] = (g_ref[...] + r_ref[...]).astype(o_ref.dtype)

    spec = pl.BlockSpec((None, tr, cols), lambda q, i, c_ref: (q, i, 0))
    grid_spec = pltpu.PrefetchScalarGridSpec(
        num_scalar_prefetch=1, grid=(N_CHIPS, nt),
        in_specs=[pl.BlockSpec((None, None, tr, cols), lambda q, i, c_ref: (q, c_ref[0], i, 0)), spec], out_specs=spec)
    return pl.pallas_call(body, name=name, out_shape=jax.ShapeDtypeStruct(recv.shape, BF16), grid_spec=grid_spec,
                          compiler_params=pltpu.CompilerParams(dimension_semantics=("arbitrary", "arbitrary"),
                                                               vmem_limit_bytes=VMEM_LIMIT_BYTES))(core, g, recv)


def grad_chip_exchange(ps, name):
    n = len(ps)

    def body(*refs):
        p_refs, o_refs = refs[:n], refs[n:2 * n]
        send_sems, recv_sems, local_sems = refs[2 * n:]
        x, y, c = _place()
        chips = _other_chips(x, y)
        me = 2 * x + y
        mine = [pltpu.make_async_copy(p_refs[a].at[me], o_refs[a].at[me], local_sems.at[a]) for a in range(n)]
        sends = [pltpu.make_async_remote_copy(
            src_ref=p_refs[a].at[2 * chip[0] + chip[1]], dst_ref=o_refs[a].at[me], send_sem=send_sems.at[3 * a + j],
            recv_sem=recv_sems.at[3 * a + j], device_id=(*chip, c), device_id_type=MESH)
            for a in range(n) for j, chip in enumerate(chips)]
        for cp in mine + sends:
            cp.start()
        for a in range(n):
            for j, chip in enumerate(chips):
                pltpu.make_async_remote_copy(
                    src_ref=p_refs[a].at[me], dst_ref=o_refs[a].at[2 * chip[0] + chip[1]],
                    send_sem=send_sems.at[3 * a + j], recv_sem=recv_sems.at[3 * a + j], device_id=(*chip, c),
                    device_id_type=MESH).wait_recv()
        for cp in sends:
            cp.wait_send()
        for cp in mine:
            cp.wait()

    return _call(body, name, [jax.ShapeDtypeStruct(p.shape, p.dtype) for p in ps], in_specs=[_ANY] * n,
                 out_specs=[_ANY] * n, scratch=[pltpu.SemaphoreType.DMA((3 * n,)), pltpu.SemaphoreType.DMA((3 * n,)),
                                                pltpu.SemaphoreType.DMA((n,))])(*ps)


def grad_half_swap(ss, name):
    n = len(ss)

    def body(*refs):
        s_refs, o_refs, send_sems, recv_sems = refs[:n], refs[n:2 * n], refs[2 * n], refs[2 * n + 1]
        x, y, c = _place()
        cps = [pltpu.make_async_remote_copy(
            src_ref=s_refs[a], dst_ref=o_refs[a], send_sem=send_sems.at[a], recv_sem=recv_sems.at[a],
            device_id=(x, y, 1 - c), device_id_type=MESH) for a in range(n)]
        for cp in cps:
            cp.start()
        for cp in cps:
            cp.wait()

    return _call(body, name, [jax.ShapeDtypeStruct(s.shape, s.dtype) for s in ss], in_specs=[_ANY] * n,
                 out_specs=[_ANY] * n, scratch=[pltpu.SemaphoreType.DMA((n,)), pltpu.SemaphoreType.DMA((n,))])(*ss)


ROW_TILE = 256

BIG_SHARDS = (("w_in", (D_MODEL, 896), 1), ("s5_w_glu", (128, S5_WIDTH), 0), ("w_proj_a", (S5_WIDTH, 256), 1),
              ("w_proj_b", (SGU_WIDTH, 256), 1), ("w_out", (256, D_MODEL), 0), ("w_up", (D_MODEL, 1408), 1),
              ("w_down", (704, D_MODEL), 0))
SMALL_PARAMS = ("g_mix", "s5_a_re", "s5_a_im", "s5_log_step", "s5_b_re", "s5_b_im", "s5_c_re", "s5_c_im", "s5_d",
                "s5_b_glu", "sgu_ln_g", "sgu_ln_b", "sgu_w", "sgu_b", "b_gate", "g_ffn", "conv_b", "g_final")
PACK_COLS = 1024


def _rows_of(n):
    return -(-n // PACK_COLS)


def _pack_rows(arrays, total_rows, dtype):
    parts = []
    used = 0
    for a in arrays:
        r = _rows_of(a.size)
        parts.append(jnp.pad(a.reshape(-1).astype(dtype), (0, r * PACK_COLS - a.size)).reshape(r, PACK_COLS))
        used += r
    if total_rows > used:
        parts.append(jnp.zeros((total_rows - used, PACK_COLS), dtype))
    return jnp.concatenate(parts, axis=0)


def _unpack_rows(packed, shapes, row0=0):
    out = []
    for s in shapes:
        n = 1
        for d in s:
            n *= d
        r = _rows_of(n)
        out.append(packed[row0:row0 + r].reshape(-1)[:n].reshape(s))
        row0 += r
    return out


def _octet_major(re, im):
    parts = []
    for o in range(OCTETS):
        parts += [re[:, o * 512:(o + 1) * 512], im[:, o * 512:(o + 1) * 512]]
    return jnp.concatenate(parts, axis=1)


def _octet_split(v):
    v = v.reshape(OCTETS, 2, 512)
    return v[:, 0].reshape(N_STATE), v[:, 1].reshape(N_STATE)


def _s5_bmat(bb):
    t = bb.reshape(S5_GROUP, OCTETS, 8, 1, S5_STATE) * jnp.eye(8, dtype=F32)[None, None, :, :, None]
    return jnp.transpose(t, (1, 2, 0, 3, 4)).reshape(OCTETS, LANES, 512)


def _s5_bmat_t(dm):
    t = dm.reshape(OCTETS, 8, S5_GROUP, 8, S5_STATE) * jnp.eye(8, dtype=F32)[None, :, None, :, None]
    return jnp.transpose(t.sum(axis=3), (2, 0, 1, 3)).reshape(S5_GROUP, S5_GROUPS, S5_STATE)


def _s5_cmat(c):
    t = c.reshape(OCTETS, 8, 1, S5_GROUP, S5_STATE) * jnp.eye(8, dtype=F32)[None, :, :, None, None]
    return jnp.transpose(t, (0, 2, 4, 1, 3)).reshape(OCTETS, 512, LANES)


def _s5_cmat_t(dm):
    t = dm.reshape(OCTETS, 8, S5_STATE, 8, S5_GROUP) * jnp.eye(8, dtype=F32)[None, :, None, :, None]
    return jnp.transpose(t.sum(axis=1), (0, 2, 3, 1)).reshape(S5_GROUPS, S5_GROUP, S5_STATE)


def local_step(x, ctx, tgt, mod, modc, W):
    n_tok, n_ctx = x.shape[0], ctx.shape[0]
    tm = ROW_TILE
    D = D_MODEL
    sh1, sc1, ga1, sh2, sc2, ga2 = [mod[:, k * D:(k + 1) * D] for k in range(N_MOD)]
    sh1c, sc1c = modc[:, :D], modc[:, D:2 * D]
    g_mix, g_ffn, g_final = W["g_mix"], W["g_ffn"], W["g_final"]
    w_in = W["w_in"]
    w_in_u = w_in[0][:, :S5_WIDTH]

    h = rowcall(f_modulate, "mod1", n_tok, tm, [(x, D, 0, 0)], [g_mix, sc1, sh1], [(D, BF16)], [])[0]
    hc = rowcall(f_modulate, "mod1_ctx", n_ctx, tm, [(ctx, D, 0, 0)], [g_mix, sc1c, sh1c], [(D, BF16)], [])[0]
    proj = matmul(h, w_in, "nn", BF16, "proj_in", shards=True)
    uc = matmul(hc, w_in_u, "nn", BF16, "proj_in_ctx")
    u3 = jnp.concatenate([uc, proj[:, :S5_WIDTH], uc], axis=0)

    a_re, a_im, ls = W["s5_a_re"], W["s5_a_im"], W["s5_log_step"][..., None]
    b_re_t = jnp.transpose(W["s5_b_re"], (0, 3, 1, 2))
    b_im_t = jnp.transpose(W["s5_b_im"], (0, 3, 1, 2))
    bb_re, bb_im = s5_discretize(a_re, a_im, ls, b_re_t, b_im_t)
    ls_rep = jnp.repeat(W["s5_log_step"], S5_STATE, axis=1).reshape(2, 1, N_STATE)
    tabs = s5_tables(a_re.reshape(2, 1, N_STATE), a_im.reshape(2, 1, N_STATE), ls_rep)
    n_chunks = (n_ctx + n_tok) // SCAN_T
    ctx_blk = n_ctx // SCAN_T
    pw, qt, bm, cm = [], [], [], []
    for d in range(2):
        pw.append(tuple(jnp.repeat(_octet_major(tabs[k][d], tabs[k + 1][d]), SUB, axis=0) for k in (0, 2)))
        qt.append(tuple(_octet_major(tabs[k][d], tabs[k + 1][d]) for k in (4, 6)))
        bm.append(jnp.concatenate([_s5_bmat(bb_re[d]), _s5_bmat(bb_im[d])], axis=2).astype(BF16))
        cm.append(jnp.concatenate([_s5_cmat(W["s5_c_re"][d]), -_s5_cmat(W["s5_c_im"][d])], axis=1).astype(BF16))
    y0, cin0 = s5_forward(u3, bm[0], cm[0], pw[0][0], qt[0][0], False, n_chunks, 0, "s5_fwd0")
    y1, cin1 = s5_forward(u3, bm[1], cm[1], pw[1][1], qt[1][1], True, n_chunks, ctx_blk, "s5_fwd1")

    mix_rows = [(proj, 512, 0, 0), (y0, 512, 0, n_ctx // tm), (y1, 512, 0, 0)] + \
               [(proj, 512, k, 0) for k in range(1, 7)]
    mix_vecs = [W["s5_d"], W["s5_w_glu"], W["s5_b_glu"], W["sgu_ln_g"], W["sgu_ln_b"], W["sgu_w"],
                jnp.transpose(W["sgu_b"]), W["w_proj_a"], W["w_proj_b"], W["b_gate"]]
    mrg = rowcall(f_mixer, "mixer", n_tok, tm, mix_rows, mix_vecs, [(D, BF16)], [])[0]
    o = matmul(mrg, W["w_out"], "nn", F32, "proj_out")
    x1, h2 = rowcall(f_resid_mod, "resid_mod2", n_tok, tm, [(x, D, 0, 0), (o, D, 0, 0)], [ga1, g_ffn, sc2, sh2],
                     [(D, F32), (D, BF16)], [])
    up = matmul(h2, W["w_up"], "nn", BF16, "ffn_up", shards=True)
    conv_w = W["conv_w"].reshape(9, 2 * FFN_HIDDEN)
    wg = jnp.pad(conv_w[:, :FFN_HIDDEN], ((0, 7), (0, 0)))
    wv = jnp.pad(conv_w[:, FFN_HIDDEN:], ((0, 7), (0, 0)))
    act, gate, val = conv_forward(up, wg, wv, W["conv_b"])
    dn = matmul(act, W["w_down"], "nn", F32, "ffn_down")

    def final_fn(x1_, dn_, tgt_, ga2_, gf_):
        loss, (dx1_, ddn_, dga2_, dgf_) = jax.value_and_grad(f_final_loss, argnums=(0, 1, 3, 4))(
            x1_, dn_, tgt_, ga2_, gf_)
        return dx1_, ddn_, loss.reshape(1, 1), dga2_, dgf_

    dx2, ddn, loss, d_ga2, d_gfinal = rowcall(
        final_fn, "final_loss", n_tok, tm, [(x1, D, 0, 0), (dn, D, 0, 0), (tgt, D, 0, 0)], [ga2, g_final],
        [(D, F32), (D, BF16)], [(1, 1), (1, D), (1, D)])

    dact = matmul(ddn, W["w_down"], "nt", BF16, "ffn_down_dx")
    d_w_down = matmul(act, ddn, "tn", F32, "ffn_down_dw")
    dup_g, dup_v, dwg, dwv = conv_backward(up, gate, val, dact, wg, wv)
    dup = jnp.concatenate([dup_g, dup_v], axis=1)
    d_conv_w = jnp.concatenate([dwg[:9], dwv[:9]], axis=1).reshape(3, 3, 2 * FFN_HIDDEN)
    d_conv_b = jnp.concatenate([dwg[9:10], dwv[9:10]], axis=1)
    dh2 = matmul(dup, W["w_up"], "nt", BF16, "ffn_up_dx", shards=True)
    d_w_up = matmul(h2, dup, "tn", F32, "ffn_up_dw", shards=True)

    def resid_bwd(x_, o_, dx1_, dh2_, ga_, g_, sc_, sh_):
        _, vjp = jax.vjp(f_resid_mod, x_, o_, ga_, g_, sc_, sh_)
        return vjp((dx1_, dh2_))

    dxa, do, d_ga1, d_gffn, d_sc2, d_sh2 = rowcall(
        resid_bwd, "resid_mod2_bwd", n_tok, tm, [(x, D, 0, 0), (o, D, 0, 0), (dx2, D, 0, 0), (dh2, D, 0, 0)],
        [ga1, g_ffn, sc2, sh2], [(D, F32), (D, BF16)], [(1, D)] * 4)

    dmrg = matmul(do, W["w_out"], "nt", BF16, "proj_out_dx")
    d_w_out = matmul(mrg, do, "tn", F32, "proj_out_dw")

    def mixer_bwd(*args):
        rows, dm, vecs = args[:9], args[9], [v.astype(F32) for v in args[10:]]
        _, vjp = jax.vjp(f_mixer, *rows, *vecs)
        g = vjp(dm)
        return (g[0], g[1], jnp.concatenate([g[3], g[4]], axis=1), jnp.concatenate(g[5:9], axis=1)) + tuple(g[9:])

    mb = rowcall(mixer_bwd, "mixer_bwd", n_tok, tm, mix_rows + [(dmrg, D, 0, 0)], mix_vecs,
                 [(512, BF16), (512, BF16), (1024, BF16), (2048, BF16)], [v.shape for v in mix_vecs])
    du_direct, dys, dzb, dgl = mb[:4]
    d_s5d, d_w_glu, d_b_glu, d_ln_g, d_ln_b, d_sgu_w, d_sgu_bt, d_w_pa, d_w_pb, d_b_gate = mb[4:]

    zc = jnp.zeros((n_ctx, S5_WIDTH), BF16)
    dy3 = jnp.concatenate([zc, dys, zc], axis=0)
    du0, dbm0, dcm0, da0 = s5_backward(u3, dy3, cin0, bm[0], cm[0], pw[0][0], qt[0][0], pw[0][1], qt[0][1], False,
                                       n_chunks, 0, "s5_bwd0")
    du1, dbm1, dcm1, da1 = s5_backward(u3, dy3, cin1, bm[1], cm[1], pw[1][1], qt[1][1], pw[1][0], qt[1][0], True,
                                       n_chunks, ctx_blk, "s5_bwd1")
    add3 = lambda a, b, c: a + b + c
    du_a = rowcall(add3, "du_sum", n_tok, tm, [(du_direct, 512, 0, 0), (du0, 512, 0, n_ctx // tm), (du1, 512, 0, 0)],
                   [], [(512, BF16)], [])[0]
    du_c = rowcall(lambda a, b: a + b, "du_sum_ctx", n_ctx, tm, [(du0, 512, 0, 0), (du1, 512, 0, n_tok // tm)],
                   [], [(512, BF16)], [])[0]

    dab_re, dab_im, dbb_re, dbb_im, d_c_re, d_c_im = [], [], [], [], [], []
    for dbm, dcm, da in ((dbm0, dcm0, da0), (dbm1, dcm1, da1)):
        r, i = _octet_split(da)
        dab_re.append(r.reshape(S5_GROUPS, S5_STATE))
        dab_im.append(i.reshape(S5_GROUPS, S5_STATE))
        dbb_re.append(_s5_bmat_t(dbm[:, :, :512]))
        dbb_im.append(_s5_bmat_t(dbm[:, :, 512:]))
        d_c_re.append(_s5_cmat_t(dcm[:, :512]))
        d_c_im.append(-_s5_cmat_t(dcm[:, 512:]))
    d_a_re, d_a_im, d_ls, d_b_re_t, d_b_im_t = s5_discretize_bwd(
        a_re, a_im, ls, b_re_t, b_im_t, jnp.stack(dab_re), jnp.stack(dab_im), jnp.stack(dbb_re), jnp.stack(dbb_im))

    dproj = jnp.concatenate([du_a, dzb, dgl], axis=1)
    dh = matmul(dproj, w_in, "nt", BF16, "proj_in_dx", shards=True)
    dhc = matmul(du_c, w_in_u, "nt", BF16, "proj_in_ctx_dx")
    d_w_in_c = matmul(hc, du_c, "tn", F32, "proj_in_ctx_dw")
    d_w_in = matmul(h, dproj, "tn", F32, "proj_in_dw", shards=True,
                    init=jnp.pad(d_w_in_c[None], ((0, N_CHIPS - 1), (0, 0), (0, w_in.shape[2] - S5_WIDTH))))

    def mod_bwd_ctx(x_, dh_, g_, sc_, sh_):
        _, vjp = jax.vjp(f_modulate, x_, g_, sc_, sh_)
        return vjp(dh_)[1:]

    d_gmix_c, d_sc1c, d_sh1c = rowcall(mod_bwd_ctx, "mod1_ctx_bwd", n_ctx, tm, [(ctx, D, 0, 0), (dhc, D, 0, 0)],
                                       [g_mix, sc1c, sh1c], [], [(1, D)] * 3)

    def mod_bwd(x_, dh_, dxa_, g_, sc_, sh_):
        _, vjp = jax.vjp(f_modulate, x_, g_, sc_, sh_)
        dx_, dg_, dsc_, dsh_ = vjp(dh_)
        return dx_ + dxa_, dg_, dsc_, dsh_

    zero_d = jnp.zeros((1, D), F32)
    grad_x, d_gmix, d_sc1, d_sh1 = rowcall(
        mod_bwd, "mod1_bwd", n_tok, tm, [(x, D, 0, 0), (dh, D, 0, 0), (dxa, D, 0, 0)], [g_mix, sc1, sh1],
        [(D, F32)], [(1, D)] * 3, ainit=[d_gmix_c, zero_d, zero_d])

    grads = {
        "dmod": jnp.concatenate([d_sh1, d_sc1, d_ga1, d_sh2, d_sc2, d_ga2], axis=1),
        "dmodc": jnp.concatenate([d_sh1c, d_sc1c], axis=1),
        "g_mix": d_gmix,
        "s5_a_re": d_a_re, "s5_a_im": d_a_im, "s5_log_step": d_ls[..., 0],
        "s5_b_re": jnp.transpose(d_b_re_t, (0, 2, 3, 1)), "s5_b_im": jnp.transpose(d_b_im_t, (0, 2, 3, 1)),
        "s5_c_re": jnp.stack(d_c_re), "s5_c_im": jnp.stack(d_c_im), "s5_d": d_s5d, "s5_b_glu": d_b_glu,
        "sgu_ln_g": d_ln_g, "sgu_ln_b": d_ln_b, "sgu_w": d_sgu_w, "sgu_b": jnp.transpose(d_sgu_bt),
        "b_gate": d_b_gate, "g_ffn": d_gffn, "conv_b": d_conv_b, "g_final": d_gfinal, "conv_w": d_conv_w,
        "w_in": d_w_in, "s5_w_glu": d_w_glu, "w_proj_a": d_w_pa, "w_proj_b": d_w_pb, "w_out": d_w_out,
        "w_up": d_w_up, "w_down": d_w_down,
    }
    return loss, grad_x, grads


ADA_COLS = N_MOD * D_MODEL // N_CHIPS
MOD_ROWS = 16


def mod_forward(c16, w, b):
    n = w.shape[1]
    tn = 512

    def body(c_ref, w_ref, b_ref, o_ref):
        cv = c_ref[...]
        cs = cv * jax.nn.sigmoid(cv)
        o_ref[...] = jnp.dot(cs.astype(BF16), w_ref[...].astype(BF16), preferred_element_type=F32) + b_ref[...]

    return _call(body, "mod_forward", jax.ShapeDtypeStruct((MOD_ROWS, n), F32), grid=(n // tn,),
                 in_specs=[pl.BlockSpec((MOD_ROWS, D_MODEL), lambda j: (0, 0)),
                           pl.BlockSpec((D_MODEL, tn), lambda j: (0, j)), pl.BlockSpec((1, tn), lambda j: (0, j))],
                 out_specs=pl.BlockSpec((MOD_ROWS, tn), lambda j: (0, j)), sem=("arbitrary",))(c16, w, b)


def f_ada_outer(ct, dm):
    cs = ct * jax.nn.sigmoid(ct)
    acc = cs[:, 0:1] * dm[0:1]
    for k in range(1, 9):
        acc = acc + cs[:, k:k + 1] * dm[k:k + 1]
    return acc


def f_cctx_grad(z, p4):
    s = jax.nn.sigmoid(z)
    return (p4[0:1] + p4[1:2] + p4[2:3] + p4[3:4]) * (s + z * s * (1.0 - s))


WEIGHT_NAMES = ("c_ctx", "w_ada", "b_ada", "g_mix", "w_in", "s5_a_re", "s5_a_im", "s5_log_step", "s5_b_re",
                "s5_b_im", "s5_c_re", "s5_c_im", "s5_d", "s5_w_glu", "s5_b_glu", "sgu_ln_g", "sgu_ln_b", "sgu_w",
                "sgu_b", "w_proj_a", "w_proj_b", "b_gate", "w_out", "g_ffn", "w_up", "conv_w", "conv_b", "w_down",
                "g_final")
CONV_SHARD = 2 * FFN_HIDDEN // N_CHIPS
SMALL_PACK_ROWS = 512
SMALL_ROW0 = 58
ADAM_PACK_ROWS = 512


def kernel(x, c, ctx, c_ctx, w_ada, b_ada, g_mix, w_in, s5_a_re, s5_a_im, s5_log_step, s5_b_re, s5_b_im, s5_c_re, s5_c_im, s5_d, s5_w_glu, s5_b_glu, sgu_ln_g, sgu_ln_b, sgu_w, sgu_b, w_proj_a, w_proj_b, b_gate, w_out, g_ffn, w_up, conv_w, conv_b, w_down, g_final, loss_target, m_c_ctx, m_w_ada, m_b_ada, m_g_mix, m_w_in, m_s5_a_re, m_s5_a_im, m_s5_log_step, m_s5_b_re, m_s5_b_im, m_s5_c_re, m_s5_c_im, m_s5_d, m_s5_w_glu, m_s5_b_glu, m_sgu_ln_g, m_sgu_ln_b, m_sgu_w, m_sgu_b, m_w_proj_a, m_w_proj_b, m_b_gate, m_w_out, m_g_ffn, m_w_up, m_conv_w, m_conv_b, m_w_down, m_g_final, v_c_ctx, v_w_ada, v_b_ada, v_g_mix, v_w_in, v_s5_a_re, v_s5_a_im, v_s5_log_step, v_s5_b_re, v_s5_b_im, v_s5_c_re, v_s5_c_im, v_s5_d, v_s5_w_glu, v_s5_b_glu, v_sgu_ln_g, v_sgu_ln_b, v_sgu_w, v_sgu_b, v_w_proj_a, v_w_proj_b, v_b_gate, v_w_out, v_g_ffn, v_w_up, v_conv_w, v_conv_b, v_w_down, v_g_final):
    given = dict(locals())
    wts = {n: given[n] for n in WEIGHT_NAMES}
    ms = {n: given["m_" + n] for n in WEIGHT_NAMES}
    vs = {n: given["v_" + n] for n in WEIGHT_NAMES}
    xi, yi, ci = _place()
    chip = 2 * xi + yi
    dev = 2 * chip + ci
    D = D_MODEL

    c8 = allgather_devices(jnp.pad(c, ((0, 7), (0, 0))), "gather_c")[:, 0, :]
    c16 = jnp.concatenate([c8, c_ctx[None], jnp.zeros((MOD_ROWS - 9, D), F32)], axis=0)
    b_shard = lax.dynamic_slice(b_ada, (0, chip * ADA_COLS), (1, ADA_COLS))
    mod_shard = mod_forward(c16, w_ada[0], b_shard)
    mod_all = allgather_devices(mod_shard, "gather_mod")
    mod_full = jnp.concatenate([mod_all[2 * q] for q in range(N_CHIPS)], axis=1)
    mod = lax.dynamic_slice(mod_full, (dev, 0), (1, N_MOD * D))
    modc = mod_full[8:9]

    big_names = [n for n, _, _ in BIG_SHARDS]
    conv_rows = jnp.pad(conv_w[0].reshape(9, CONV_SHARD), ((0, 7), (0, 0)))
    shards = [wts[n][0].astype(BF16) for n in big_names] + [conv_rows]
    gathered = allgather_chips(shards, "gather_weights")
    gathered = [_fill_own(t, s, chip) for t, s in zip(gathered, shards)]
    W = dict(zip(big_names, gathered[:-1]))
    for n, shape, axis in BIG_SHARDS:
        if axis == 0:
            W[n] = W[n].reshape(N_CHIPS * shape[0], shape[1])
    for n in ("w_proj_a", "w_proj_b"):
        W[n] = jnp.transpose(W[n], (1, 0, 2)).reshape(W[n].shape[1], -1)
    W["conv_w"] = jnp.transpose(gathered[-1][:, :9], (1, 0, 2)).reshape(3, 3, 2 * FFN_HIDDEN)
    for n in ("g_mix", "g_ffn", "s5_d", "s5_b_glu", "sgu_ln_g", "sgu_ln_b", "b_gate", "conv_b"):
        W[n] = wts[n]
    W["g_final"] = g_final[None]
    for n in ("s5_a_re", "s5_a_im", "s5_log_step", "s5_b_re", "s5_b_im", "s5_c_re", "s5_c_im", "sgu_w", "sgu_b"):
        W[n] = wts[n][0]

    loss_part, grad_x, g = local_step(x[0], ctx[0], loss_target[0], mod, modc, W)
    loss = lax.psum(loss_part[0, 0], ("x", "y", "c"))

    g_slots = []
    for n, shape, axis in BIG_SHARDS:
        if n in ("w_proj_a", "w_proj_b"):
            g_slots.append(jnp.transpose(g[n].reshape(shape[0], N_CHIPS, shape[1]), (1, 0, 2)))
        else:
            g_slots.append(g[n].reshape((N_CHIPS,) + shape))
    g_slots = [t.reshape(N_CHIPS, 2, t.shape[1] // 2, t.shape[2]) for t in g_slots]
    core = ci.astype(jnp.int32).reshape(1)
    from_sibling = grad_pair_swap(g_slots, "grad_pair_swap")
    pair = [pair_sum(gs, rv, core, "grad_pair_sum_" + n) for gs, rv, n in zip(g_slots, from_sibling, big_names)]
    from_chips = grad_chip_exchange(pair, "grad_chip_exchange")
    add2 = lambda a, b: a + b
    add4 = lambda a, b, c_, d: ((a + b) + c_) + d
    halves = []
    for fc, n in zip(from_chips, big_names):
        r2, cols = fc.shape[1], fc.shape[2]
        tr = _tile(r2, 256, 8)
        halves.append(rowcall(add4, "grad_chip_sum_" + n, r2, tr,
                              [(fc.reshape(N_CHIPS * r2, cols), cols, 0, q * r2 // tr) for q in range(N_CHIPS)], [],
                              [(cols, F32)], [])[0])
    others = grad_half_swap(halves, "grad_half_swap")
    big_grads = {n: jnp.where(ci == 0, jnp.concatenate([mine, other], axis=0), jnp.concatenate([other, mine], axis=0))
                 for n, mine, other in zip(big_names, halves, others)}

    small_pack = _pack_rows([g["dmod"], g["dmodc"], g["conv_w"]] + [g[n] for n in SMALL_PARAMS], SMALL_PACK_ROWS, F32)
    small_all = _fill_own(allgather_devices(small_pack, "gather_small_grads", copy_own=False), small_pack, dev)
    small_2d = small_all.reshape(N_DEV * SMALL_PACK_ROWS, PACK_COLS)

    def add8(*a):
        s = a[0]
        for t in a[1:]:
            s = s + t
        return s

    small_sum = rowcall(add8, "small_grad_sum", SMALL_PACK_ROWS, 256,
                        [(small_2d, PACK_COLS, 0, k * SMALL_PACK_ROWS // 256) for k in range(N_DEV)], [],
                        [(PACK_COLS, F32)], [])[0]
    dmod_all = small_all[:, 0:N_MOD].reshape(N_DEV, N_MOD * D)
    dmod_sum = small_sum[0:N_MOD].reshape(1, N_MOD * D)
    dmodc_sum = jnp.pad(small_sum[N_MOD:N_MOD + 2].reshape(1, 2 * D), ((0, 0), (0, (N_MOD - 2) * D)))
    conv_grad = _unpack_rows(small_sum, [(3, 3, 2 * FFN_HIDDEN)], row0=N_MOD + 2)[0]
    small_grads = dict(zip(SMALL_PARAMS, _unpack_rows(small_sum, [wts[n].shape for n in SMALL_PARAMS], row0=SMALL_ROW0)))

    dm16 = jnp.concatenate([dmod_all, dmodc_sum, jnp.zeros((MOD_ROWS - 9, N_MOD * D), F32)], axis=0)
    dm_shard = lax.dynamic_slice(dm16, (0, chip * ADA_COLS), (MOD_ROWS, ADA_COLS))
    g_w_ada = rowcall(f_ada_outer, "w_ada_grad", D, 256, [(jnp.transpose(c16), MOD_ROWS, 0, 0)], [dm_shard],
                      [(ADA_COLS, F32)], [])[0]
    g_b_ada = rowcall(add2, "b_ada_grad", 1, 1, [(dmod_sum, N_MOD * D, 0, 0), (dmodc_sum, N_MOD * D, 0, 0)], [],
                      [(N_MOD * D, F32)], [])[0]
    dmc_rows = jnp.pad(dm_shard[8:9], ((0, 7), (0, 0)))
    cctx_part = matmul(dmc_rows, w_ada[0], "nt", F32, "c_ctx_partial")
    cctx_all = allgather_devices(cctx_part, "gather_c_ctx")
    cctx_4 = jnp.stack([cctx_all[2 * q, 0] for q in range(N_CHIPS)])
    g_c_ctx = rowcall(f_cctx_grad, "c_ctx_grad", 1, 1, [(c_ctx[None], D, 0, 0)], [cctx_4], [(D, F32)], [])[0]

    grads = dict(small_grads)
    grads.update(big_grads)
    grads["w_ada"] = g_w_ada
    grads["b_ada"] = g_b_ada
    grads["c_ctx"] = g_c_ctx
    grads["conv_w"] = lax.dynamic_slice(conv_grad, (0, 0, chip * CONV_SHARD), (3, 3, CONV_SHARD))
    grads = {n: grads[n].reshape(wts[n].shape) for n in WEIGHT_NAMES}

    delta, new_m, new_v = {}, {}, {}
    large = [n for n, _, _ in BIG_SHARDS] + ["w_ada", "conv_w"]
    for n in large:
        shape2d = (-1, wts[n].shape[-1])
        d_, m_, v_ = adamw(wts[n].reshape(shape2d), grads[n].reshape(shape2d), ms[n].reshape(shape2d),
                           vs[n].reshape(shape2d), "adamw_" + n)
        delta[n], new_m[n], new_v[n] = [t.reshape(wts[n].shape) for t in (d_, m_, v_)]
    rest = [n for n in WEIGHT_NAMES if n not in large]
    packs = [_pack_rows([src[n] for n in rest], ADAM_PACK_ROWS, F32) for src in (wts, grads, ms, vs)]
    outs = adamw(*packs, "adamw_replicated")
    for dst, packed in zip((delta, new_m, new_v), outs):
        dst.update(zip(rest, _unpack_rows(packed, [wts[n].shape for n in rest])))

    return (loss, grad_x[None], *[grads[n] for n in WEIGHT_NAMES], *[delta[n] for n in WEIGHT_NAMES],
            *[new_m[n] for n in WEIGHT_NAMES], *[new_v[n] for n in WEIGHT_NAMES])
```

```python
import functools

import jax
import jax.numpy as jnp
from jax import lax
from jax.experimental import pallas as pl
from jax.experimental.pallas import tpu as pltpu

F32, BF16 = jnp.float32, jnp.bfloat16
MESH = pl.DeviceIdType.MESH

D_MODEL = 1024
S5_WIDTH = 512
S5_GROUP = 16
S5_GROUPS = 32
S5_STATE = 64
SGU_WIDTH = 512
SGU_GROUPS = 8
CHUNK = 128
FFN_HIDDEN = 2816
GRID_W = 64
N_MOD = 6
EPS = 1e-6
N_STATE = S5_GROUPS * S5_STATE
OCTETS = 4
SCAN_T = 128
N_CHIPS = 4
N_DEV = 8
LANES = 128
VMEM_LIMIT_BYTES = 56 * 1024 * 1024
CONV_PAD = 72
CONV_ROWS = 256

ADAM_LR, ADAM_B1, ADAM_B2, ADAM_EPS, ADAM_WD, ADAM_STEP = 0.001, 0.9, 0.999, 1e-08, 0.01, 10


def _call(body, name, out_shape, grid=None, in_specs=None, out_specs=None, scratch=(), sem=None, **kw):
    params = pltpu.CompilerParams(dimension_semantics=sem, vmem_limit_bytes=VMEM_LIMIT_BYTES)
    extra = {} if grid is None else {"grid": grid}
    return pl.pallas_call(body, name=name, out_shape=out_shape, in_specs=in_specs, out_specs=out_specs,
                          scratch_shapes=list(scratch), compiler_params=params, **extra, **kw)


def _tile(n, target, mult=LANES):
    best = None
    t = mult
    while t <= min(n, target):
        if n % t == 0:
            best = t
        t += mult
    return best or n


@jax.custom_vjp
def mmul(a, b):
    return jnp.dot(a.astype(BF16), b.astype(BF16), preferred_element_type=F32)


def _mmul_fwd(a, b):
    return mmul(a, b), (a, b)


def _mmul_bwd(res, ct):
    a, b = res
    ctb = ct.astype(BF16)
    da = lax.dot_general(ctb, b.astype(BF16), (((1,), (1,)), ((), ())), preferred_element_type=F32)
    db = lax.dot_general(a.astype(BF16), ctb, (((0,), (0,)), ((), ())), preferred_element_type=F32)
    return da.astype(a.dtype), db.astype(b.dtype)


mmul.defvjp(_mmul_fwd, _mmul_bwd)

_DOT_DIMS = {"nn": ((1,), (0,)), "nt": ((1,), (1,)), "tn": ((0,), (0,))}


MM_TILE = 1408
MM_FULL_K = 2048


def matmul(a, b, mode, out_dtype, name, init=None, shards=False):
    if mode == "nn":
        (M, K), N = a.shape, (b.shape[2] * N_CHIPS if shards else b.shape[1])
    elif mode == "nt":
        M, N, K = a.shape[0], b.shape[-2], a.shape[1]
    else:
        (K, M), N = a.shape, b.shape[1]
    ns = (K if mode == "nt" else N) // N_CHIPS
    tm = _tile(M, MM_TILE, 8 if M < LANES else LANES)
    tn = _tile(ns if shards and mode != "nt" else N, MM_TILE)
    if shards and mode == "nt":
        tk = _tile(ns, MM_TILE)
    else:
        tk = K if K <= MM_FULL_K else _tile(K, MM_TILE)
    nk = K // tk
    per = ns // (tk if mode == "nt" else tn)
    dims = (_DOT_DIMS[mode], ((), ()))
    has_init = init is not None
    use_acc = nk > 1 and out_dtype != F32

    def body(*refs):
        a_ref, b_ref = refs[:2]
        i_ref = refs[2] if has_init else None
        o_ref = refs[3] if has_init else refs[2]
        acc = refs[-1] if use_acc else o_ref
        k = pl.program_id(2)
        part = lax.dot_general(a_ref[...].astype(BF16), b_ref[...].astype(BF16), dims, preferred_element_type=F32)

        @pl.when(k == 0)
        def _():
            first = part + i_ref[...].astype(F32) if has_init else part
            acc[...] = first.astype(acc.dtype)

        if nk > 1:
            @pl.when(k > 0)
            def _():
                acc[...] += part

        if use_acc:
            @pl.when(k == nk - 1)
            def _():
                o_ref[...] = acc[...].astype(o_ref.dtype)

    if mode == "tn":
        a_spec = pl.BlockSpec((tk, tm), lambda i, j, k: (k, i))
    else:
        a_spec = pl.BlockSpec((tm, tk), lambda i, j, k: (i, k))
    if mode == "nt":
        b_spec = (pl.BlockSpec((None, tn, tk), lambda i, j, k: (k // per, j, k % per)) if shards
                  else pl.BlockSpec((tn, tk), lambda i, j, k: (j, k)))
    else:
        b_spec = (pl.BlockSpec((None, tk, tn), lambda i, j, k: (j // per, k, j % per)) if shards and mode == "nn"
                  else pl.BlockSpec((tk, tn), lambda i, j, k: (k, j)))
    if shards and mode == "tn":
        o_spec = pl.BlockSpec((None, tm, tn), lambda i, j, k: (j // per, i, j % per))
        out_shape = jax.ShapeDtypeStruct((N_CHIPS, M, ns), out_dtype)
    else:
        o_spec = pl.BlockSpec((tm, tn), lambda i, j, k: (i, j))
        out_shape = jax.ShapeDtypeStruct((M, N), out_dtype)
    in_specs = [a_spec, b_spec] + ([o_spec] if has_init else [])
    args = (a, b) + ((init,) if has_init else ())
    return _call(body, name, out_shape, grid=(M // tm, N // tn, nk),
                 in_specs=in_specs, out_specs=o_spec, scratch=[pltpu.VMEM((tm, tn), F32)] if use_acc else [],
                 sem=("parallel", "parallel", "arbitrary"))(*args)


def rowcall(fn, name, nrows, tm, rins, vins, routs, aouts, ainit=None):
    n_r, n_v, n_ro = len(rins), len(vins), len(routs)
    n_i = len(aouts) if ainit is not None else 0

    def body(*refs):
        r_in, v_in, i_in = refs[:n_r], refs[n_r:n_r + n_v], refs[n_r + n_v:n_r + n_v + n_i]
        r_out, a_out = refs[n_r + n_v + n_i:n_r + n_v + n_i + n_ro], refs[n_r + n_v + n_i + n_ro:]
        outs = fn(*[r[...].astype(F32) for r in r_in], *[v[...] for v in v_in])
        if not isinstance(outs, (tuple, list)):
            outs = (outs,)
        for ref, val in zip(r_out, outs[:n_ro]):
            ref[...] = val.astype(ref.dtype)
        if a_out:
            @pl.when(pl.program_id(0) == 0)
            def _():
                for k, ref in enumerate(a_out):
                    ref[...] = i_in[k][...] if n_i else jnp.zeros_like(ref)

            for ref, val in zip(a_out, outs[n_ro:]):
                ref[...] += val.astype(F32)

    def rspec(width, cblk, roff):
        return pl.BlockSpec((tm, width), lambda i: (i + roff, cblk))

    def whole(shape):
        nd = len(shape)
        return pl.BlockSpec(tuple(shape), lambda i: (0,) * nd)

    inits = list(ainit) if n_i else []
    in_specs = [rspec(w, cb, ro) for (_, w, cb, ro) in rins] + [whole(v.shape) for v in vins + inits]
    out_specs = [rspec(w, 0, 0) for (w, _) in routs] + [whole(s) for s in aouts]
    out_shape = [jax.ShapeDtypeStruct((nrows, w), dt) for (w, dt) in routs] + \
                [jax.ShapeDtypeStruct(tuple(s), F32) for s in aouts]
    res = _call(body, name, out_shape, grid=(nrows // tm,), in_specs=in_specs, out_specs=out_specs,
                sem=("arbitrary",))(*[r[0] for r in rins], *vins, *inits)
    return res


def _rms(x):
    return lax.rsqrt(jnp.mean(x * x, axis=-1, keepdims=True) + EPS)


def f_modulate(x, g, sc, sh):
    return (x * _rms(x)) * g * (1.0 + sc) + sh


def f_resid_mod(x, o, ga, g, sc, sh):
    x1 = x + ga * o
    return x1, f_modulate(x1, g, sc, sh)


def f_final_loss(x1, dn, tgt, ga2, gf):
    x2 = x1 + ga2 * dn
    y = (x2 * _rms(x2)) * gf
    err = jnp.square(y - tgt)
    return 0.5 * jnp.sum(jnp.mean(err, axis=-1))


def _sgu_spatial(vn, w, bt):
    lo = lax.broadcasted_iota(jnp.int32, (1, LANES), 1) < (SGU_WIDTH // SGU_GROUPS)
    row_blocks = []
    for r in range(vn.shape[0] // CHUNK):
        rows = vn[r * CHUNK:(r + 1) * CHUNK]
        cols = []
        for j in range(SGU_WIDTH // LANES):
            blk = rows[:, j * LANES:(j + 1) * LANES]
            v_lo = jnp.where(lo, blk, 0.0)
            v_hi = jnp.where(lo, 0.0, blk)
            s = mmul(w[2 * j], v_lo) + mmul(w[2 * j + 1], v_hi)
            bias = jnp.where(lo, bt[:, 2 * j:2 * j + 1], bt[:, 2 * j + 1:2 * j + 2])
            cols.append(s + bias)
        row_blocks.append(jnp.concatenate(cols, axis=1))
    return jnp.concatenate(row_blocks, axis=0) if len(row_blocks) > 1 else row_blocks[0]


def f_mixer(u_a, y0, y1, zu, zv, ga0, ga1, gb0, gb1, d_skip, w_glu, b_glu, ln_g, ln_b, sgu_w, sgu_bt,
            w_pa, w_pb, b_gate):
    ys = u_a * d_skip + y0 + y1
    ge = jax.nn.gelu(ys)
    y_a = ge * jax.nn.sigmoid(mmul(ge, w_glu) + b_glu)
    u_sg = jax.nn.gelu(zu)
    v = jax.nn.gelu(zv)
    vc = v - jnp.mean(v, axis=-1, keepdims=True)
    vn = (vc * lax.rsqrt(jnp.mean(vc * vc, axis=-1, keepdims=True) + EPS)) * ln_g + ln_b
    y_b = u_sg * _sgu_spatial(vn, sgu_w, sgu_bt)
    gl_a = jnp.concatenate([ga0, ga1], axis=1) + b_gate[:, :D_MODEL]
    gl_b = jnp.concatenate([gb0, gb1], axis=1) + b_gate[:, D_MODEL:]
    return jax.nn.sigmoid(gl_a) * mmul(y_a, w_pa) + jax.nn.sigmoid(gl_b) * mmul(y_b, w_pb)


def _cmul(ar, ai, xr, xi):
    return ar * xr - ai * xi, ar * xi + ai * xr


SUB = 8
STRAND = SCAN_T // SUB


def to_strand_order(v):
    n = v.shape[0] // SCAN_T
    return jnp.transpose(v.reshape(n, SUB, STRAND, v.shape[1]), (0, 2, 1, 3)).reshape(v.shape)


def _to_token_order(v):
    i = lax.broadcasted_iota(jnp.int32, (SCAN_T, SCAN_T), 0)
    j = lax.broadcasted_iota(jnp.int32, (SCAN_T, SCAN_T), 1)
    perm = jnp.where(i == STRAND * (j % SUB) + j // SUB, 1.0, 0.0).astype(BF16)
    hi = v.astype(BF16)
    lo = (v - hi.astype(F32)).astype(BF16)
    return jnp.dot(perm, hi, preferred_element_type=F32) + jnp.dot(perm, lo, preferred_element_type=F32)


def _scan_strands(xr, xi, pw_ref, q_ref, col, rev, conj, cr, ci):
    def tab(ref, lo):
        t_r = ref[lo:lo + SUB, col:col + LANES]
        t_i = ref[lo:lo + SUB, col + 512:col + 512 + LANES]
        return t_r, (-t_i if conj else t_i)

    a_r, a_i = tab(pw_ref, (STRAND - 1) * SUB if rev else 0)
    order = list(range(STRAND - 1, -1, -1) if rev else range(STRAND))
    lr, li = [None] * STRAND, [None] * STRAND
    for n, k in enumerate(order):
        lr[k], li[k] = xr[k * SUB:(k + 1) * SUB], xi[k * SUB:(k + 1) * SUB]
        if n:
            m_r, m_i = _cmul(a_r, a_i, lr[order[n - 1]], li[order[n - 1]])
            lr[k], li[k] = lr[k] + m_r, li[k] + m_i
    f_r, f_i = lr[order[-1]], li[order[-1]]
    q_r, q_i = tab(q_ref, 0)
    sub = lax.broadcasted_iota(jnp.int32, (SUB, 1), 0)
    s = 1
    while s < SUB:
        row = (SUB - s) if rev else (s - 1)
        shift = (SUB - s) if rev else s
        m = (sub < SUB - s) if rev else (sub >= s)
        p_r, p_i = _cmul(q_r[row:row + 1], q_i[row:row + 1], pltpu.roll(f_r, shift, 0), pltpu.roll(f_i, shift, 0))
        f_r, f_i = f_r + jnp.where(m, p_r, 0.0), f_i + jnp.where(m, p_i, 0.0)
        s *= 2
    c_r, c_i = jnp.broadcast_to(cr, (SUB, LANES)), jnp.broadcast_to(ci, (SUB, LANES))
    p_r, p_i = _cmul(q_r, q_i, c_r, c_i)
    s_r, s_i = f_r + p_r, f_i + p_i
    edge = 0 if rev else SUB - 1
    first = sub == (SUB - 1 if rev else 0)
    e_r = jnp.where(first, c_r, pltpu.roll(s_r, SUB - 1 if rev else 1, 0))
    e_i = jnp.where(first, c_i, pltpu.roll(s_i, SUB - 1 if rev else 1, 0))
    for k in range(STRAND):
        t_r, t_i = tab(pw_ref, k * SUB)
        p_r, p_i = _cmul(t_r, t_i, e_r, e_i)
        lr[k], li[k] = lr[k] + p_r, li[k] + p_i
    return lr, li, (s_r[edge:edge + 1], s_i[edge:edge + 1]), (e_r, e_i)


def s5_forward(u3, bm, cm, pw, q, rev, n_chunks, blk0, name):
    T = SCAN_T

    def pos(i):
        return (n_chunks - 1 - i) if rev else i

    def body(u_ref, bm_ref, cm_ref, pw_ref, q_ref, y_ref, cin_ref, carry):
        @pl.when(pl.program_id(0) == 0)
        def _():
            carry[...] = jnp.zeros_like(carry)

        cin_ref[...] = carry[...]
        u = u_ref[...]
        for o in range(OCTETS):
            bu = jnp.dot(u[:, o * LANES:(o + 1) * LANES], bm_ref[o], preferred_element_type=F32)
            hr, hi = [], []
            for j in range(4):
                col = o * 1024 + j * LANES
                xr, xi, (cr, ci), _ = _scan_strands(
                    bu[:, j * LANES:(j + 1) * LANES], bu[:, 512 + j * LANES:512 + (j + 1) * LANES], pw_ref, q_ref, col,
                    rev, False, cin_ref[0:1, col:col + LANES], cin_ref[0:1, col + 512:col + 512 + LANES])
                carry[0:1, col:col + LANES] = cr
                carry[0:1, col + 512:col + 512 + LANES] = ci
                hr.append(jnp.concatenate(xr, axis=0))
                hi.append(jnp.concatenate(xi, axis=0))
            h = jnp.concatenate(hr + hi, axis=1).astype(BF16)
            y_ref[:, o * LANES:(o + 1) * LANES] = _to_token_order(jnp.dot(h, cm_ref[o], preferred_element_type=F32))

    whole3 = lambda s: pl.BlockSpec(s, lambda i: (0, 0, 0))
    whole2 = lambda s: pl.BlockSpec(s, lambda i: (0, 0))
    return _call(
        body, name,
        [jax.ShapeDtypeStruct((n_chunks * T, S5_WIDTH), F32), jax.ShapeDtypeStruct((n_chunks, 1, 2 * N_STATE), F32)],
        grid=(n_chunks,),
        in_specs=[pl.BlockSpec((T, S5_WIDTH), lambda i: (blk0 + pos(i), 0)), whole3(bm.shape), whole3(cm.shape),
                  whole2(pw.shape), whole2(q.shape)],
        out_specs=[pl.BlockSpec((T, S5_WIDTH), lambda i: (pos(i), 0)),
                   pl.BlockSpec((None, 1, 2 * N_STATE), lambda i: (pos(i), 0, 0))],
        scratch=[pltpu.VMEM((1, 2 * N_STATE), F32)], sem=("arbitrary",))(u3, bm, cm, pw, q)


def s5_backward(u3, dy3, cin, bm, cm, pw_h, q_h, pw_l, q_l, rev, n_chunks, blk0, name):
    T = SCAN_T

    def pos(i):
        return i if rev else (n_chunks - 1 - i)

    def body(u_ref, dy_ref, cin_ref, bm_ref, cm_ref, pwh_ref, qh_ref, pwl_ref, ql_ref, du_ref, dbm_ref, dcm_ref,
             da_ref, lcarry):
        @pl.when(pl.program_id(0) == 0)
        def _():
            lcarry[...] = jnp.zeros_like(lcarry)
            dbm_ref[...] = jnp.zeros_like(dbm_ref)
            dcm_ref[...] = jnp.zeros_like(dcm_ref)
            da_ref[...] = jnp.zeros_like(da_ref)

        u = u_ref[...]
        dy = dy_ref[...]
        for o in range(OCTETS):
            u_o = u[:, o * LANES:(o + 1) * LANES]
            dy_o = dy[:, o * LANES:(o + 1) * LANES]
            bu = jnp.dot(u_o, bm_ref[o], preferred_element_type=F32)
            g = lax.dot_general(dy_o, cm_ref[o], (((1,), (1,)), ((), ())), preferred_element_type=F32)
            hr, hi, lr, li = [], [], [], []
            for j in range(4):
                col = o * 1024 + j * LANES
                sl_r = slice(j * LANES, (j + 1) * LANES)
                sl_i = slice(512 + j * LANES, 512 + (j + 1) * LANES)
                xr, xi, _, (e_r, e_i) = _scan_strands(
                    bu[:, sl_r], bu[:, sl_i], pwh_ref, qh_ref, col, rev, False,
                    cin_ref[0:1, col:col + LANES], cin_ref[0:1, col + 512:col + 512 + LANES])
                ar_, ai_, (l_r, l_i), _ = _scan_strands(
                    g[:, sl_r], g[:, sl_i], pwl_ref, ql_ref, col, not rev, True,
                    lcarry[0:1, col:col + LANES], lcarry[0:1, col + 512:col + 512 + LANES])
                lcarry[0:1, col:col + LANES] = l_r
                lcarry[0:1, col + 512:col + 512 + LANES] = l_i
                acc_r = acc_i = None
                for k in range(STRAND):
                    kp = k + 1 if rev else k - 1
                    p_r, p_i = (e_r, e_i) if not 0 <= kp < STRAND else (xr[kp], xi[kp])
                    t_r = ar_[k] * p_r + ai_[k] * p_i
                    t_i = ai_[k] * p_r - ar_[k] * p_i
                    acc_r, acc_i = (t_r, t_i) if acc_r is None else (acc_r + t_r, acc_i + t_i)
                da_ref[0:1, col:col + LANES] += jnp.sum(acc_r, axis=0, keepdims=True)
                da_ref[0:1, col + 512:col + 512 + LANES] += jnp.sum(acc_i, axis=0, keepdims=True)
                hr.append(jnp.concatenate(xr, axis=0))
                hi.append(jnp.concatenate(xi, axis=0))
                lr.append(jnp.concatenate(ar_, axis=0))
                li.append(jnp.concatenate(ai_, axis=0))
            h = jnp.concatenate(hr + hi, axis=1).astype(BF16)
            lam = jnp.concatenate(lr + li, axis=1).astype(BF16)
            du_ref[:, o * LANES:(o + 1) * LANES] = _to_token_order(lax.dot_general(
                lam, bm_ref[o], (((1,), (1,)), ((), ())), preferred_element_type=F32))
            dbm_ref[o] += lax.dot_general(u_o, lam, (((0,), (0,)), ((), ())), preferred_element_type=F32)
            dcm_ref[o] += lax.dot_general(h, dy_o, (((0,), (0,)), ((), ())), preferred_element_type=F32)

    whole3 = lambda s: pl.BlockSpec(s, lambda i: (0, 0, 0))
    whole2 = lambda s: pl.BlockSpec(s, lambda i: (0, 0))
    row_spec = pl.BlockSpec((T, S5_WIDTH), lambda i: (blk0 + pos(i), 0))
    return _call(
        body, name,
        [jax.ShapeDtypeStruct((n_chunks * T, S5_WIDTH), F32), jax.ShapeDtypeStruct(bm.shape, F32),
         jax.ShapeDtypeStruct(cm.shape, F32), jax.ShapeDtypeStruct((1, 2 * N_STATE), F32)],
        grid=(n_chunks,),
        in_specs=[row_spec, row_spec, pl.BlockSpec((None, 1, 2 * N_STATE), lambda i: (pos(i), 0, 0)),
                  whole3(bm.shape), whole3(cm.shape), whole2(pw_h.shape), whole2(q_h.shape), whole2(pw_l.shape),
                  whole2(q_l.shape)],
        out_specs=[pl.BlockSpec((T, S5_WIDTH), lambda i: (pos(i), 0)), whole3(bm.shape), whole3(cm.shape),
                   whole2((1, 2 * N_STATE))],
        scratch=[pltpu.VMEM((1, 2 * N_STATE), F32)], sem=("arbitrary",))(u3, dy3, cin, bm, cm, pw_h, q_h, pw_l, q_l)


def s5_tables(ar, ai, ls):
    def body(ar_ref, ai_ref, ls_ref, *outs):
        dt = jnp.exp(ls_ref[...])
        k = 0
        for n_rows, step in ((STRAND, 1.0), (SUB, float(STRAND))):
            row = lax.broadcasted_iota(jnp.int32, (n_rows, 1), 0)
            for m_int in (row + 1, n_rows - row):
                m = m_int.astype(F32) * step
                mag = jnp.exp(m * (ar_ref[...] * dt))
                ang = m * (ai_ref[...] * dt)
                outs[k][...] = mag * jnp.cos(ang)
                outs[k + 1][...] = mag * jnp.sin(ang)
                k += 2

    vec = pl.BlockSpec((None, 1, N_STATE), lambda d: (d, 0, 0))
    tab = lambda n: pl.BlockSpec((None, n, N_STATE), lambda d: (d, 0, 0))
    shp = lambda n: jax.ShapeDtypeStruct((2, n, N_STATE), F32)
    sizes = [STRAND] * 4 + [SUB] * 4
    return _call(body, "s5_tables", [shp(n) for n in sizes], grid=(2,), in_specs=[vec, vec, vec],
                 out_specs=[tab(n) for n in sizes], sem=("arbitrary",))(ar, ai, ls)


def f_discretize(a_re, a_im, ls, b_re, b_im):
    dt = jnp.exp(ls)
    mag = jnp.exp(a_re * dt)
    ab_re = mag * jnp.cos(a_im * dt)
    ab_im = mag * jnp.sin(a_im * dt)
    p = ab_re - 1.0
    q = ab_im
    den = a_re * a_re + a_im * a_im
    k_re = ((p * a_re + q * a_im) / den)[None]
    k_im = ((q * a_re - p * a_im) / den)[None]
    return ab_re, ab_im, k_re * b_re - k_im * b_im, k_re * b_im + k_im * b_re


def _disc_specs():
    a = pl.BlockSpec((None, S5_GROUPS, S5_STATE), lambda d: (d, 0, 0))
    s = pl.BlockSpec((None, S5_GROUPS, 1), lambda d: (d, 0, 0))
    b = pl.BlockSpec((None, S5_GROUP, S5_GROUPS, S5_STATE), lambda d: (d, 0, 0, 0))
    return a, s, b


def s5_discretize(a_re, a_im, ls, b_re, b_im):
    def body(ar, ai, l, br, bi, obr, obi):
        _, _, r, i = f_discretize(ar[...], ai[...], l[...], br[...], bi[...])
        obr[...] = r
        obi[...] = i

    a, s, b = _disc_specs()
    return _call(body, "s5_discretize", [jax.ShapeDtypeStruct(b_re.shape, F32)] * 2, grid=(2,),
                 in_specs=[a, a, s, b, b], out_specs=[b, b], sem=("arbitrary",))(a_re, a_im, ls, b_re, b_im)


def s5_discretize_bwd(a_re, a_im, ls, b_re, b_im, dab_re, dab_im, dbb_re, dbb_im):
    def body(ar, ai, l, br, bi, c0, c1, c2, c3, o0, o1, o2, o3, o4):
        _, vjp = jax.vjp(f_discretize, ar[...], ai[...], l[...], br[...], bi[...])
        outs = vjp((c0[...], c1[...], c2[...], c3[...]))
        for ref, val in zip((o0, o1, o2, o3, o4), outs):
            ref[...] = val

    a, s, b = _disc_specs()
    shapes = [jax.ShapeDtypeStruct(t.shape, F32) for t in (a_re, a_im, ls, b_re, b_im)]
    return _call(body, "s5_discretize_bwd", shapes, grid=(2,), in_specs=[a, a, s, b, b, a, a, b, b],
                 out_specs=[a, a, s, b, b], sem=("arbitrary",))(a_re, a_im, ls, b_re, b_im, dab_re, dab_im,
                                                                 dbb_re, dbb_im)


def _conv_taps(s_ref, base, n_rows):
    n = n_rows + 2 * CONV_PAD
    ext = s_ref[pl.ds(base, n), :]
    col = (lax.broadcasted_iota(jnp.int32, (n, 1), 0) + (2 * GRID_W - CONV_PAD)) % GRID_W
    left = jnp.where(col == 0, 0.0, pltpu.roll(ext, 1, 0))
    right = jnp.where(col == GRID_W - 1, 0.0, pltpu.roll(ext, n - 1, 0))
    return left, ext, right


def _conv_apply(taps, w_ref, n_rows, flip):
    out = None
    for i in range(3):
        wi = 2 - i if flip else i
        comb = None
        for j in range(3):
            wj = 2 - j if flip else j
            term = w_ref[wi * 3 + wj:wi * 3 + wj + 1, :] * taps[j]
            comb = term if comb is None else comb + term
        start = CONV_PAD + (i - 1) * GRID_W
        part = comb[start:start + n_rows]
        out = part if out is None else out + part
    return out


def _conv_fill(dst_ref, src_ref, n_tok):
    zeros = jnp.zeros((CONV_PAD, LANES), F32)
    dst_ref[0:CONV_PAD, :] = zeros
    dst_ref[CONV_PAD + n_tok:2 * CONV_PAD + n_tok, :] = zeros

    def step(r, carry):
        base = pl.multiple_of(r * CONV_ROWS, CONV_ROWS)
        dst_ref[pl.ds(base + CONV_PAD, CONV_ROWS), :] = src_ref[pl.ds(base, CONV_ROWS), :].astype(F32)
        return carry

    lax.fori_loop(0, n_tok // CONV_ROWS, step, 0)


def conv_forward(up, wg, wv, bias):
    n_tok = up.shape[0]
    nb = FFN_HIDDEN // LANES

    def body(ug_ref, uv_ref, wg_ref, wv_ref, bg_ref, bv_ref, act_ref, gate_ref, val_ref, sg, sv):
        _conv_fill(sg, ug_ref, n_tok)
        _conv_fill(sv, uv_ref, n_tok)

        def step(r, carry):
            base = pl.multiple_of(r * CONV_ROWS, CONV_ROWS)
            gate = _conv_apply(_conv_taps(sg, base, CONV_ROWS), wg_ref, CONV_ROWS, False) + bg_ref[...]
            val = _conv_apply(_conv_taps(sv, base, CONV_ROWS), wv_ref, CONV_ROWS, False) + bv_ref[...]
            act_ref[pl.ds(base, CONV_ROWS), :] = (gate * jax.nn.sigmoid(gate) * val).astype(BF16)
            gate_ref[pl.ds(base, CONV_ROWS), :] = gate.astype(BF16)
            val_ref[pl.ds(base, CONV_ROWS), :] = val.astype(BF16)
            return carry

        lax.fori_loop(0, n_tok // CONV_ROWS, step, 0)

    col = lambda off: pl.BlockSpec((n_tok, LANES), lambda k: (0, k + off))
    wsp = lambda off: pl.BlockSpec((16, LANES), lambda k: (0, k + off))
    bsp = lambda off: pl.BlockSpec((1, LANES), lambda k: (0, k + off))
    pad = pltpu.VMEM((n_tok + 2 * CONV_PAD, LANES), F32)
    half = jax.ShapeDtypeStruct((n_tok, FFN_HIDDEN), BF16)
    return _call(body, "conv_forward", [half, half, half], grid=(nb,),
                 in_specs=[col(0), col(nb), wsp(0), wsp(0), bsp(0), bsp(nb)], out_specs=[col(0), col(0), col(0)],
                 scratch=[pad, pad], sem=("arbitrary",))(up, up, wg, wv, bias, bias)


def conv_backward(up, gate, val, dact, wg, wv):
    n_tok = up.shape[0]
    nb = FFN_HIDDEN // LANES
    n_steps = n_tok // CONV_ROWS

    def body(ug_ref, uv_ref, gate_ref, val_ref, da_ref, wg_ref, wv_ref, dug_ref, duv_ref, dwg_ref, dwv_ref,
             sg, sv, sdg, sdv):
        _conv_fill(sg, ug_ref, n_tok)
        _conv_fill(sv, uv_ref, n_tok)
        zeros = jnp.zeros((CONV_PAD, LANES), F32)
        for s_ref in (sdg, sdv):
            s_ref[0:CONV_PAD, :] = zeros
            s_ref[CONV_PAD + n_tok:2 * CONV_PAD + n_tok, :] = zeros
        dwg_ref[...] = jnp.zeros_like(dwg_ref)
        dwv_ref[...] = jnp.zeros_like(dwv_ref)

        def grads(r, carry):
            base = pl.multiple_of(r * CONV_ROWS, CONV_ROWS)
            taps_g = _conv_taps(sg, base, CONV_ROWS)
            taps_v = _conv_taps(sv, base, CONV_ROWS)
            gate = gate_ref[pl.ds(base, CONV_ROWS), :].astype(F32)
            val = val_ref[pl.ds(base, CONV_ROWS), :].astype(F32)
            d_act = da_ref[pl.ds(base, CONV_ROWS), :].astype(F32)
            sig = jax.nn.sigmoid(gate)
            d_gate = d_act * val * (sig * (1.0 + gate * (1.0 - sig)))
            d_val = d_act * (gate * sig)
            sdg[pl.ds(base + CONV_PAD, CONV_ROWS), :] = d_gate
            sdv[pl.ds(base + CONV_PAD, CONV_ROWS), :] = d_val
            for d_out, taps, dw_ref in ((d_gate, taps_g, dwg_ref), (d_val, taps_v, dwv_ref)):
                for i in range(3):
                    start = CONV_PAD + (i - 1) * GRID_W
                    for j in range(3):
                        k = i * 3 + j
                        dw_ref[k:k + 1, :] += jnp.sum(d_out * taps[j][start:start + CONV_ROWS], axis=0, keepdims=True)
                dw_ref[9:10, :] += jnp.sum(d_out, axis=0, keepdims=True)
            return carry

        lax.fori_loop(0, n_steps, grads, 0)

        def spread(r, carry):
            base = pl.multiple_of(r * CONV_ROWS, CONV_ROWS)
            dug_ref[pl.ds(base, CONV_ROWS), :] = _conv_apply(
                _conv_taps(sdg, base, CONV_ROWS), wg_ref, CONV_ROWS, True).astype(BF16)
            duv_ref[pl.ds(base, CONV_ROWS), :] = _conv_apply(
                _conv_taps(sdv, base, CONV_ROWS), wv_ref, CONV_ROWS, True).astype(BF16)
            return carry

        lax.fori_loop(0, n_steps, spread, 0)

    col = lambda off: pl.BlockSpec((n_tok, LANES), lambda k: (0, k + off))
    wsp = pl.BlockSpec((16, LANES), lambda k: (0, k))
    pad = pltpu.VMEM((n_tok + 2 * CONV_PAD, LANES), F32)
    half = jax.ShapeDtypeStruct((n_tok, FFN_HIDDEN), BF16)
    dw = jax.ShapeDtypeStruct((16, FFN_HIDDEN), F32)
    return _call(body, "conv_backward", [half, half, dw, dw], grid=(nb,),
                 in_specs=[col(0), col(nb), col(0), col(0), col(0), wsp, wsp],
                 out_specs=[col(0), col(0), wsp, wsp], scratch=[pad, pad, pad, pad],
                 sem=("arbitrary",))(up, up, gate, val, dact, wg, wv)


def f_adamw(w, g, m, v):
    m = ADAM_B1 * m + (1.0 - ADAM_B1) * g
    v = ADAM_B2 * v + (1.0 - ADAM_B2) * jnp.square(g)
    m_hat = m / (1.0 - ADAM_B1 ** ADAM_STEP)
    v_hat = v / (1.0 - ADAM_B2 ** ADAM_STEP)
    delta = -ADAM_LR * (m_hat / (jnp.sqrt(v_hat) + ADAM_EPS) + ADAM_WD * w)
    return delta, m, v


def adamw(w, g, m, v, name):
    shape = w.shape
    cols = shape[-1]
    rows = w.size // cols
    two_d = [t.reshape(rows, cols) for t in (w, g, m, v)]
    tm = _tile(rows, 256, 8) if rows % 8 == 0 else rows
    outs = rowcall(f_adamw, name, rows, tm, [(t, cols, 0, 0) for t in two_d], [], [(cols, F32)] * 3, [])
    return tuple(o.reshape(shape) for o in outs)


def _place():
    return lax.axis_index("x"), lax.axis_index("y"), lax.axis_index("c")


_ANY = pl.BlockSpec(memory_space=pl.ANY)


def _fill_own(gathered, own, index):
    return lax.dynamic_update_slice(gathered, own[None], (index,) + (0,) * own.ndim)


def allgather_devices(v, name, copy_own=True):
    def body(v_ref, out_ref, send_sems, recv_sems, local_sem):
        x, y, c = _place()
        me, sibling = (x, y, c), (x, y, 1 - c)
        chips = [(1 - x, y), (x, 1 - y), (1 - x, 1 - y)]

        def slot(p):
            return out_ref.at[4 * p[0] + 2 * p[1] + p[2]]

        def copy(k, block, to, src=None):
            return pltpu.make_async_remote_copy(
                src_ref=slot(block) if src is None else src, dst_ref=slot(block),
                send_sem=send_sems.at[k], recv_sem=recv_sems.at[k], device_id=to, device_id_type=MESH)

        mine = pltpu.make_async_copy(v_ref, slot(me), local_sem)
        if copy_own:
            mine.start()
        first = [copy(0, me, sibling, src=v_ref)]
        first += [copy(1 + j, me, (*chip, c), src=v_ref) for j, chip in enumerate(chips)]
        for cp in first:
            cp.start()
        passed = [copy(4 + j, (*chip, c), sibling) for j, chip in enumerate(chips)]
        for j, chip in enumerate(chips):
            copy(1 + j, (*chip, c), me).wait_recv()
            passed[j].start()
        copy(0, sibling, me).wait_recv()
        for j, chip in enumerate(chips):
            copy(4 + j, (*chip, 1 - c), me).wait_recv()
        for cp in first + passed:
            cp.wait_send()
        if copy_own:
            mine.wait()

    return _call(body, name, jax.ShapeDtypeStruct((N_DEV,) + v.shape, v.dtype), in_specs=[_ANY], out_specs=_ANY,
                 scratch=[pltpu.SemaphoreType.DMA((7,)), pltpu.SemaphoreType.DMA((7,)), pltpu.SemaphoreType.DMA])(v)


def _other_chips(x, y):
    return [(1 - x, y), (x, 1 - y), (1 - x, 1 - y)]


def allgather_chips(vs, name):
    n = len(vs)
    shapes = [v.shape for v in vs]
    vs = [v.reshape((2, v.shape[0] // 2) + v.shape[1:]) for v in vs]

    def body(*refs):
        v_refs, o_refs = refs[:n], refs[n:2 * n]
        send_sems, recv_sems = refs[2 * n:]
        x, y, c = _place()
        sibling = (x, y, 1 - c)
        chips = _other_chips(x, y)

        def rows(a, chip, h):
            return o_refs[a].at[2 * chip[0] + chip[1], h]

        def copy(a, k, chip, h, to, src=None):
            return pltpu.make_async_remote_copy(
                src_ref=rows(a, chip, h) if src is None else src, dst_ref=rows(a, chip, h),
                send_sem=send_sems.at[6 * a + k], recv_sem=recv_sems.at[6 * a + k], device_id=to, device_id_type=MESH)

        first = [copy(a, j, (x, y), c, (*chip, c), src=v_refs[a].at[c])
                 for a in range(n) for j, chip in enumerate(chips)]
        for cp in first:
            cp.start()
        passed = []
        for j, chip in enumerate(chips):
            for a in range(n):
                copy(a, j, chip, c, (x, y, c)).wait_recv()
                passed.append(copy(a, 3 + j, chip, c, sibling))
                passed[-1].start()
        for j, chip in enumerate(chips):
            for a in range(n):
                copy(a, 3 + j, chip, 1 - c, (x, y, c)).wait_recv()
        for cp in first + passed:
            cp.wait_send()

    outs = _call(body, name, [jax.ShapeDtypeStruct((N_CHIPS,) + v.shape, v.dtype) for v in vs], in_specs=[_ANY] * n,
                 out_specs=[_ANY] * n,
                 scratch=[pltpu.SemaphoreType.DMA((6 * n,)), pltpu.SemaphoreType.DMA((6 * n,))])(*vs)
    return [o.reshape((N_CHIPS,) + s) for o, s in zip(outs, shapes)]


def grad_pair_swap(gs, name):
    n = len(gs)

    def body(*refs):
        g_refs, o_refs, send_sems, recv_sems = refs[:n], refs[n:2 * n], refs[2 * n], refs[2 * n + 1]
        x, y, c = _place()
        cps = [pltpu.make_async_remote_copy(
            src_ref=g_refs[a].at[:, 1 - c], dst_ref=o_refs[a], send_sem=send_sems.at[a], recv_sem=recv_sems.at[a],
            device_id=(x, y, 1 - c), device_id_type=MESH) for a in range(n)]
        for cp in cps:
            cp.start()
        for cp in cps:
            cp.wait()

    return _call(body, name, [jax.ShapeDtypeStruct((N_CHIPS,) + g.shape[2:], g.dtype) for g in gs],
                 in_specs=[_ANY] * n, out_specs=[_ANY] * n,
                 scratch=[pltpu.SemaphoreType.DMA((n,)), pltpu.SemaphoreType.DMA((n,))])(*gs)


def pair_sum(g, recv, core, name):
    r2, cols = recv.shape[1], recv.shape[2]
    tr = _tile(r2, 256, 8)
    nt = r2 // tr

    def body(c_ref, g_ref, r_ref, o_ref):
        o_ref[...] = (g_ref[...] + r_ref[...]).astype(o_ref.dtype)

    spec = pl.BlockSpec((None, tr, cols), lambda q, i, c_ref: (q, i, 0))
    grid_spec = pltpu.PrefetchScalarGridSpec(
        num_scalar_prefetch=1, grid=(N_CHIPS, nt),
        in_specs=[pl.BlockSpec((None, None, tr, cols), lambda q, i, c_ref: (q, c_ref[0], i, 0)), spec], out_specs=spec)
    return pl.pallas_call(body, name=name, out_shape=jax.ShapeDtypeStruct(recv.shape, BF16), grid_spec=grid_spec,
                          compiler_params=pltpu.CompilerParams(dimension_semantics=("arbitrary", "arbitrary"),
                                                               vmem_limit_bytes=VMEM_LIMIT_BYTES))(core, g, recv)


def grad_chip_exchange(ps, name):
    n = len(ps)

    def body(*refs):
        p_refs, o_refs = refs[:n], refs[n:2 * n]
        send_sems, recv_sems, local_sems = refs[2 * n:]
        x, y, c = _place()
        chips = _other_chips(x, y)
        me = 2 * x + y
        mine = [pltpu.make_async_copy(p_refs[a].at[me], o_refs[a].at[me], local_sems.at[a]) for a in range(n)]
        sends = [pltpu.make_async_remote_copy(
            src_ref=p_refs[a].at[2 * chip[0] + chip[1]], dst_ref=o_refs[a].at[me], send_sem=send_sems.at[3 * a + j],
            recv_sem=recv_sems.at[3 * a + j], device_id=(*chip, c), device_id_type=MESH)
            for a in range(n) for j, chip in enumerate(chips)]
        for cp in mine + sends:
            cp.start()
        for a in range(n):
            for j, chip in enumerate(chips):
                pltpu.make_async_remote_copy(
                    src_ref=p_refs[a].at[me], dst_ref=o_refs[a].at[2 * chip[0] + chip[1]],
                    send_sem=send_sems.at[3 * a + j], recv_sem=recv_sems.at[3 * a + j], device_id=(*chip, c),
                    device_id_type=MESH).wait_recv()
        for cp in sends:
            cp.wait_send()
        for cp in mine:
            cp.wait()

    return _call(body, name, [jax.ShapeDtypeStruct(p.shape, p.dtype) for p in ps], in_specs=[_ANY] * n,
                 out_specs=[_ANY] * n, scratch=[pltpu.SemaphoreType.DMA((3 * n,)), pltpu.SemaphoreType.DMA((3 * n,)),
                                                pltpu.SemaphoreType.DMA((n,))])(*ps)


def grad_half_swap(ss, name):
    n = len(ss)

    def body(*refs):
        s_refs, o_refs, send_sems, recv_sems = refs[:n], refs[n:2 * n], refs[2 * n], refs[2 * n + 1]
        x, y, c = _place()
        cps = [pltpu.make_async_remote_copy(
            src_ref=s_refs[a], dst_ref=o_refs[a], send_sem=send_sems.at[a], recv_sem=recv_sems.at[a],
            device_id=(x, y, 1 - c), device_id_type=MESH) for a in range(n)]
        for cp in cps:
            cp.start()
        for cp in cps:
            cp.wait()

    return _call(body, name, [jax.ShapeDtypeStruct(s.shape, s.dtype) for s in ss], in_specs=[_ANY] * n,
                 out_specs=[_ANY] * n, scratch=[pltpu.SemaphoreType.DMA((n,)), pltpu.SemaphoreType.DMA((n,))])(*ss)


ROW_TILE = 256

BIG_SHARDS = (("w_in", (D_MODEL, 896), 1), ("s5_w_glu", (128, S5_WIDTH), 0), ("w_proj_a", (S5_WIDTH, 256), 1),
              ("w_proj_b", (SGU_WIDTH, 256), 1), ("w_out", (256, D_MODEL), 0), ("w_up", (D_MODEL, 1408), 1),
              ("w_down", (704, D_MODEL), 0))
SMALL_PARAMS = ("g_mix", "s5_a_re", "s5_a_im", "s5_log_step", "s5_b_re", "s5_b_im", "s5_c_re", "s5_c_im", "s5_d",
                "s5_b_glu", "sgu_ln_g", "sgu_ln_b", "sgu_w", "sgu_b", "b_gate", "g_ffn", "conv_b", "g_final")
PACK_COLS = 1024


def _rows_of(n):
    return -(-n // PACK_COLS)


def _pack_rows(arrays, total_rows, dtype):
    parts = []
    used = 0
    for a in arrays:
        r = _rows_of(a.size)
        parts.append(jnp.pad(a.reshape(-1).astype(dtype), (0, r * PACK_COLS - a.size)).reshape(r, PACK_COLS))
        used += r
    if total_rows > used:
        parts.append(jnp.zeros((total_rows - used, PACK_COLS), dtype))
    return jnp.concatenate(parts, axis=0)


def _unpack_rows(packed, shapes, row0=0):
    out = []
    for s in shapes:
        n = 1
        for d in s:
            n *= d
        r = _rows_of(n)
        out.append(packed[row0:row0 + r].reshape(-1)[:n].reshape(s))
        row0 += r
    return out


def _octet_major(re, im):
    parts = []
    for o in range(OCTETS):
        parts += [re[:, o * 512:(o + 1) * 512], im[:, o * 512:(o + 1) * 512]]
    return jnp.concatenate(parts, axis=1)


def _octet_split(v):
    v = v.reshape(OCTETS, 2, 512)
    return v[:, 0].reshape(N_STATE), v[:, 1].reshape(N_STATE)


def _s5_bmat(bb):
    t = bb.reshape(S5_GROUP, OCTETS, 8, 1, S5_STATE) * jnp.eye(8, dtype=F32)[None, None, :, :, None]
    return jnp.transpose(t, (1, 2, 0, 3, 4)).reshape(OCTETS, LANES, 512)


def _s5_bmat_t(dm):
    t = dm.reshape(OCTETS, 8, S5_GROUP, 8, S5_STATE) * jnp.eye(8, dtype=F32)[None, :, None, :, None]
    return jnp.transpose(t.sum(axis=3), (2, 0, 1, 3)).reshape(S5_GROUP, S5_GROUPS, S5_STATE)


def _s5_cmat(c):
    t = c.reshape(OCTETS, 8, 1, S5_GROUP, S5_STATE) * jnp.eye(8, dtype=F32)[None, :, :, None, None]
    return jnp.transpose(t, (0, 2, 4, 1, 3)).reshape(OCTETS, 512, LANES)


def _s5_cmat_t(dm):
    t = dm.reshape(OCTETS, 8, S5_STATE, 8, S5_GROUP) * jnp.eye(8, dtype=F32)[None, :, None, :, None]
    return jnp.transpose(t.sum(axis=1), (0, 2, 3, 1)).reshape(S5_GROUPS, S5_GROUP, S5_STATE)


def local_step(x, ctx, tgt, mod, modc, W):
    n_tok, n_ctx = x.shape[0], ctx.shape[0]
    tm = ROW_TILE
    D = D_MODEL
    sh1, sc1, ga1, sh2, sc2, ga2 = [mod[:, k * D:(k + 1) * D] for k in range(N_MOD)]
    sh1c, sc1c = modc[:, :D], modc[:, D:2 * D]
    g_mix, g_ffn, g_final = W["g_mix"], W["g_ffn"], W["g_final"]
    w_in = W["w_in"]
    w_in_u = w_in[0][:, :S5_WIDTH]

    h = rowcall(f_modulate, "mod1", n_tok, tm, [(x, D, 0, 0)], [g_mix, sc1, sh1], [(D, BF16)], [])[0]
    hc = rowcall(f_modulate, "mod1_ctx", n_ctx, tm, [(ctx, D, 0, 0)], [g_mix, sc1c, sh1c], [(D, BF16)], [])[0]
    proj = matmul(h, w_in, "nn", BF16, "proj_in", shards=True)
    uc = matmul(hc, w_in_u, "nn", BF16, "proj_in_ctx")
    u3 = to_strand_order(jnp.concatenate([uc, proj[:, :S5_WIDTH], uc], axis=0))

    a_re, a_im, ls = W["s5_a_re"], W["s5_a_im"], W["s5_log_step"][..., None]
    b_re_t = jnp.transpose(W["s5_b_re"], (0, 3, 1, 2))
    b_im_t = jnp.transpose(W["s5_b_im"], (0, 3, 1, 2))
    bb_re, bb_im = s5_discretize(a_re, a_im, ls, b_re_t, b_im_t)
    ls_rep = jnp.repeat(W["s5_log_step"], S5_STATE, axis=1).reshape(2, 1, N_STATE)
    tabs = s5_tables(a_re.reshape(2, 1, N_STATE), a_im.reshape(2, 1, N_STATE), ls_rep)
    n_chunks = (n_ctx + n_tok) // SCAN_T
    ctx_blk = n_ctx // SCAN_T
    pw, qt, bm, cm = [], [], [], []
    for d in range(2):
        pw.append(tuple(jnp.repeat(_octet_major(tabs[k][d], tabs[k + 1][d]), SUB, axis=0) for k in (0, 2)))
        qt.append(tuple(_octet_major(tabs[k][d], tabs[k + 1][d]) for k in (4, 6)))
        bm.append(jnp.concatenate([_s5_bmat(bb_re[d]), _s5_bmat(bb_im[d])], axis=2).astype(BF16))
        cm.append(jnp.concatenate([_s5_cmat(W["s5_c_re"][d]), -_s5_cmat(W["s5_c_im"][d])], axis=1).astype(BF16))
    y0, cin0 = s5_forward(u3, bm[0], cm[0], pw[0][0], qt[0][0], False, n_chunks, 0, "s5_fwd0")
    y1, cin1 = s5_forward(u3, bm[1], cm[1], pw[1][1], qt[1][1], True, n_chunks, ctx_blk, "s5_fwd1")

    mix_rows = [(proj, 512, 0, 0), (y0, 512, 0, n_ctx // tm), (y1, 512, 0, 0)] + \
               [(proj, 512, k, 0) for k in range(1, 7)]
    mix_vecs = [W["s5_d"], W["s5_w_glu"], W["s5_b_glu"], W["sgu_ln_g"], W["sgu_ln_b"], W["sgu_w"],
                jnp.transpose(W["sgu_b"]), W["w_proj_a"], W["w_proj_b"], W["b_gate"]]
    mrg = rowcall(f_mixer, "mixer", n_tok, tm, mix_rows, mix_vecs, [(D, BF16)], [])[0]
    o = matmul(mrg, W["w_out"], "nn", F32, "proj_out")
    x1, h2 = rowcall(f_resid_mod, "resid_mod2", n_tok, tm, [(x, D, 0, 0), (o, D, 0, 0)], [ga1, g_ffn, sc2, sh2],
                     [(D, F32), (D, BF16)], [])
    up = matmul(h2, W["w_up"], "nn", BF16, "ffn_up", shards=True)
    conv_w = W["conv_w"].reshape(9, 2 * FFN_HIDDEN)
    wg = jnp.pad(conv_w[:, :FFN_HIDDEN], ((0, 7), (0, 0)))
    wv = jnp.pad(conv_w[:, FFN_HIDDEN:], ((0, 7), (0, 0)))
    act, gate, val = conv_forward(up, wg, wv, W["conv_b"])
    dn = matmul(act, W["w_down"], "nn", F32, "ffn_down")

    def final_fn(x1_, dn_, tgt_, ga2_, gf_):
        loss, (dx1_, ddn_, dga2_, dgf_) = jax.value_and_grad(f_final_loss, argnums=(0, 1, 3, 4))(
            x1_, dn_, tgt_, ga2_, gf_)
        return dx1_, ddn_, loss.reshape(1, 1), dga2_, dgf_

    dx2, ddn, loss, d_ga2, d_gfinal = rowcall(
        final_fn, "final_loss", n_tok, tm, [(x1, D, 0, 0), (dn, D, 0, 0), (tgt, D, 0, 0)], [ga2, g_final],
        [(D, F32), (D, BF16)], [(1, 1), (1, D), (1, D)])

    dact = matmul(ddn, W["w_down"], "nt", BF16, "ffn_down_dx")
    d_w_down = matmul(act, ddn, "tn", F32, "ffn_down_dw")
    dup_g, dup_v, dwg, dwv = conv_backward(up, gate, val, dact, wg, wv)
    dup = jnp.concatenate([dup_g, dup_v], axis=1)
    d_conv_w = jnp.concatenate([dwg[:9], dwv[:9]], axis=1).reshape(3, 3, 2 * FFN_HIDDEN)
    d_conv_b = jnp.concatenate([dwg[9:10], dwv[9:10]], axis=1)
    dh2 = matmul(dup, W["w_up"], "nt", BF16, "ffn_up_dx", shards=True)
    d_w_up = matmul(h2, dup, "tn", F32, "ffn_up_dw", shards=True)

    def resid_bwd(x_, o_, dx1_, dh2_, ga_, g_, sc_, sh_):
        _, vjp = jax.vjp(f_resid_mod, x_, o_, ga_, g_, sc_, sh_)
        return vjp((dx1_, dh2_))

    dxa, do, d_ga1, d_gffn, d_sc2, d_sh2 = rowcall(
        resid_bwd, "resid_mod2_bwd", n_tok, tm, [(x, D, 0, 0), (o, D, 0, 0), (dx2, D, 0, 0), (dh2, D, 0, 0)],
        [ga1, g_ffn, sc2, sh2], [(D, F32), (D, BF16)], [(1, D)] * 4)

    dmrg = matmul(do, W["w_out"], "nt", BF16, "proj_out_dx")
    d_w_out = matmul(mrg, do, "tn", F32, "proj_out_dw")

    def mixer_bwd(*args):
        rows, dm, vecs = args[:9], args[9], [v.astype(F32) for v in args[10:]]
        _, vjp = jax.vjp(f_mixer, *rows, *vecs)
        g = vjp(dm)
        return (g[0], g[1], jnp.concatenate([g[3], g[4]], axis=1), jnp.concatenate(g[5:9], axis=1)) + tuple(g[9:])

    mb = rowcall(mixer_bwd, "mixer_bwd", n_tok, tm, mix_rows + [(dmrg, D, 0, 0)], mix_vecs,
                 [(512, BF16), (512, BF16), (1024, BF16), (2048, BF16)], [v.shape for v in mix_vecs])
    du_direct, dys, dzb, dgl = mb[:4]
    d_s5d, d_w_glu, d_b_glu, d_ln_g, d_ln_b, d_sgu_w, d_sgu_bt, d_w_pa, d_w_pb, d_b_gate = mb[4:]

    zc = jnp.zeros((n_ctx, S5_WIDTH), BF16)
    dy3 = to_strand_order(jnp.concatenate([zc, dys, zc], axis=0))
    du0, dbm0, dcm0, da0 = s5_backward(u3, dy3, cin0, bm[0], cm[0], pw[0][0], qt[0][0], pw[0][1], qt[0][1], False,
                                       n_chunks, 0, "s5_bwd0")
    du1, dbm1, dcm1, da1 = s5_backward(u3, dy3, cin1, bm[1], cm[1], pw[1][1], qt[1][1], pw[1][0], qt[1][0], True,
                                       n_chunks, ctx_blk, "s5_bwd1")
    add3 = lambda a, b, c: a + b + c
    du_a = rowcall(add3, "du_sum", n_tok, tm, [(du_direct, 512, 0, 0), (du0, 512, 0, n_ctx // tm), (du1, 512, 0, 0)],
                   [], [(512, BF16)], [])[0]
    du_c = rowcall(lambda a, b: a + b, "du_sum_ctx", n_ctx, tm, [(du0, 512, 0, 0), (du1, 512, 0, n_tok // tm)],
                   [], [(512, BF16)], [])[0]

    dab_re, dab_im, dbb_re, dbb_im, d_c_re, d_c_im = [], [], [], [], [], []
    for dbm, dcm, da in ((dbm0, dcm0, da0), (dbm1, dcm1, da1)):
        r, i = _octet_split(da)
        dab_re.append(r.reshape(S5_GROUPS, S5_STATE))
        dab_im.append(i.reshape(S5_GROUPS, S5_STATE))
        dbb_re.append(_s5_bmat_t(dbm[:, :, :512]))
        dbb_im.append(_s5_bmat_t(dbm[:, :, 512:]))
        d_c_re.append(_s5_cmat_t(dcm[:, :512]))
        d_c_im.append(-_s5_cmat_t(dcm[:, 512:]))
    d_a_re, d_a_im, d_ls, d_b_re_t, d_b_im_t = s5_discretize_bwd(
        a_re, a_im, ls, b_re_t, b_im_t, jnp.stack(dab_re), jnp.stack(dab_im), jnp.stack(dbb_re), jnp.stack(dbb_im))

    dproj = jnp.concatenate([du_a, dzb, dgl], axis=1)
    dh = matmul(dproj, w_in, "nt", BF16, "proj_in_dx", shards=True)
    dhc = matmul(du_c, w_in_u, "nt", BF16, "proj_in_ctx_dx")
    d_w_in_c = matmul(hc, du_c, "tn", F32, "proj_in_ctx_dw")
    d_w_in = matmul(h, dproj, "tn", F32, "proj_in_dw", shards=True,
                    init=jnp.pad(d_w_in_c[None], ((0, N_CHIPS - 1), (0, 0), (0, w_in.shape[2] - S5_WIDTH))))

    def mod_bwd_ctx(x_, dh_, g_, sc_, sh_):
        _, vjp = jax.vjp(f_modulate, x_, g_, sc_, sh_)
        return vjp(dh_)[1:]

    d_gmix_c, d_sc1c, d_sh1c = rowcall(mod_bwd_ctx, "mod1_ctx_bwd", n_ctx, tm, [(ctx, D, 0, 0), (dhc, D, 0, 0)],
                                       [g_mix, sc1c, sh1c], [], [(1, D)] * 3)

    def mod_bwd(x_, dh_, dxa_, g_, sc_, sh_):
        _, vjp = jax.vjp(f_modulate, x_, g_, sc_, sh_)
        dx_, dg_, dsc_, dsh_ = vjp(dh_)
        return dx_ + dxa_, dg_, dsc_, dsh_

    zero_d = jnp.zeros((1, D), F32)
    grad_x, d_gmix, d_sc1, d_sh1 = rowcall(
        mod_bwd, "mod1_bwd", n_tok, tm, [(x, D, 0, 0), (dh, D, 0, 0), (dxa, D, 0, 0)], [g_mix, sc1, sh1],
        [(D, F32)], [(1, D)] * 3, ainit=[d_gmix_c, zero_d, zero_d])

    grads = {
        "dmod": jnp.concatenate([d_sh1, d_sc1, d_ga1, d_sh2, d_sc2, d_ga2], axis=1),
        "dmodc": jnp.concatenate([d_sh1c, d_sc1c], axis=1),
        "g_mix": d_gmix,
        "s5_a_re": d_a_re, "s5_a_im": d_a_im, "s5_log_step": d_ls[..., 0],
        "s5_b_re": jnp.transpose(d_b_re_t, (0, 2, 3, 1)), "s5_b_im": jnp.transpose(d_b_im_t, (0, 2, 3, 1)),
        "s5_c_re": jnp.stack(d_c_re), "s5_c_im": jnp.stack(d_c_im), "s5_d": d_s5d, "s5_b_glu": d_b_glu,
        "sgu_ln_g": d_ln_g, "sgu_ln_b": d_ln_b, "sgu_w": d_sgu_w, "sgu_b": jnp.transpose(d_sgu_bt),
        "b_gate": d_b_gate, "g_ffn": d_gffn, "conv_b": d_conv_b, "g_final": d_gfinal, "conv_w": d_conv_w,
        "w_in": d_w_in, "s5_w_glu": d_w_glu, "w_proj_a": d_w_pa, "w_proj_b": d_w_pb, "w_out": d_w_out,
        "w_up": d_w_up, "w_down": d_w_down,
    }
    return loss, grad_x, grads


ADA_COLS = N_MOD * D_MODEL // N_CHIPS
MOD_ROWS = 16


def mod_forward(c16, w, b):
    n = w.shape[1]
    tn = 512

    def body(c_ref, w_ref, b_ref, o_ref):
        cv = c_ref[...]
        cs = cv * jax.nn.sigmoid(cv)
        o_ref[...] = jnp.dot(cs.astype(BF16), w_ref[...].astype(BF16), preferred_element_type=F32) + b_ref[...]

    return _call(body, "mod_forward", jax.ShapeDtypeStruct((MOD_ROWS, n), F32), grid=(n // tn,),
                 in_specs=[pl.BlockSpec((MOD_ROWS, D_MODEL), lambda j: (0, 0)),
                           pl.BlockSpec((D_MODEL, tn), lambda j: (0, j)), pl.BlockSpec((1, tn), lambda j: (0, j))],
                 out_specs=pl.BlockSpec((MOD_ROWS, tn), lambda j: (0, j)), sem=("arbitrary",))(c16, w, b)


def f_ada_outer(ct, dm):
    cs = ct * jax.nn.sigmoid(ct)
    acc = cs[:, 0:1] * dm[0:1]
    for k in range(1, 9):
        acc = acc + cs[:, k:k + 1] * dm[k:k + 1]
    return acc


def f_cctx_grad(z, p4):
    s = jax.nn.sigmoid(z)
    return (p4[0:1] + p4[1:2] + p4[2:3] + p4[3:4]) * (s + z * s * (1.0 - s))


WEIGHT_NAMES = ("c_ctx", "w_ada", "b_ada", "g_mix", "w_in", "s5_a_re", "s5_a_im", "s5_log_step", "s5_b_re",
                "s5_b_im", "s5_c_re", "s5_c_im", "s5_d", "s5_w_glu", "s5_b_glu", "sgu_ln_g", "sgu_ln_b", "sgu_w",
                "sgu_b", "w_proj_a", "w_proj_b", "b_gate", "w_out", "g_ffn", "w_up", "conv_w", "conv_b", "w_down",
                "g_final")
CONV_SHARD = 2 * FFN_HIDDEN // N_CHIPS
SMALL_PACK_ROWS = 512
SMALL_ROW0 = 58


def kernel(x, c, ctx, c_ctx, w_ada, b_ada, g_mix, w_in, s5_a_re, s5_a_im, s5_log_step, s5_b_re, s5_b_im, s5_c_re, s5_c_im, s5_d, s5_w_glu, s5_b_glu, sgu_ln_g, sgu_ln_b, sgu_w, sgu_b, w_proj_a, w_proj_b, b_gate, w_out, g_ffn, w_up, conv_w, conv_b, w_down, g_final, loss_target, m_c_ctx, m_w_ada, m_b_ada, m_g_mix, m_w_in, m_s5_a_re, m_s5_a_im, m_s5_log_step, m_s5_b_re, m_s5_b_im, m_s5_c_re, m_s5_c_im, m_s5_d, m_s5_w_glu, m_s5_b_glu, m_sgu_ln_g, m_sgu_ln_b, m_sgu_w, m_sgu_b, m_w_proj_a, m_w_proj_b, m_b_gate, m_w_out, m_g_ffn, m_w_up, m_conv_w, m_conv_b, m_w_down, m_g_final, v_c_ctx, v_w_ada, v_b_ada, v_g_mix, v_w_in, v_s5_a_re, v_s5_a_im, v_s5_log_step, v_s5_b_re, v_s5_b_im, v_s5_c_re, v_s5_c_im, v_s5_d, v_s5_w_glu, v_s5_b_glu, v_sgu_ln_g, v_sgu_ln_b, v_sgu_w, v_sgu_b, v_w_proj_a, v_w_proj_b, v_b_gate, v_w_out, v_g_ffn, v_w_up, v_conv_w, v_conv_b, v_w_down, v_g_final):
    given = dict(locals())
    wts = {n: given[n] for n in WEIGHT_NAMES}
    ms = {n: given["m_" + n] for n in WEIGHT_NAMES}
    vs = {n: given["v_" + n] for n in WEIGHT_NAMES}
    xi, yi, ci = _place()
    chip = 2 * xi + yi
    dev = 2 * chip + ci
    D = D_MODEL

    c8 = allgather_devices(jnp.pad(c, ((0, 7), (0, 0))), "gather_c")[:, 0, :]
    c16 = jnp.concatenate([c8, c_ctx[None], jnp.zeros((MOD_ROWS - 9, D), F32)], axis=0)
    b_shard = lax.dynamic_slice(b_ada, (0, chip * ADA_COLS), (1, ADA_COLS))
    mod_shard = mod_forward(c16, w_ada[0], b_shard)
    mod_all = allgather_devices(mod_shard, "gather_mod")
    mod_full = jnp.concatenate([mod_all[2 * q] for q in range(N_CHIPS)], axis=1)
    mod = lax.dynamic_slice(mod_full, (dev, 0), (1, N_MOD * D))
    modc = mod_full[8:9]

    big_names = [n for n, _, _ in BIG_SHARDS]
    conv_rows = jnp.pad(conv_w[0].reshape(9, CONV_SHARD), ((0, 7), (0, 0)))
    shards = [wts[n][0].astype(BF16) for n in big_names] + [conv_rows]
    gathered = allgather_chips(shards, "gather_weights")
    gathered = [_fill_own(t, s, chip) for t, s in zip(gathered, shards)]
    W = dict(zip(big_names, gathered[:-1]))
    for n, shape, axis in BIG_SHARDS:
        if axis == 0:
            W[n] = W[n].reshape(N_CHIPS * shape[0], shape[1])
    for n in ("w_proj_a", "w_proj_b"):
        W[n] = jnp.transpose(W[n], (1, 0, 2)).reshape(W[n].shape[1], -1)
    W["conv_w"] = jnp.transpose(gathered[-1][:, :9], (1, 0, 2)).reshape(3, 3, 2 * FFN_HIDDEN)
    for n in ("g_mix", "g_ffn", "s5_d", "s5_b_glu", "sgu_ln_g", "sgu_ln_b", "b_gate", "conv_b"):
        W[n] = wts[n]
    W["g_final"] = g_final[None]
    for n in ("s5_a_re", "s5_a_im", "s5_log_step", "s5_b_re", "s5_b_im", "s5_c_re", "s5_c_im", "sgu_w", "sgu_b"):
        W[n] = wts[n][0]

    loss_part, grad_x, g = local_step(x[0], ctx[0], loss_target[0], mod, modc, W)
    loss = lax.psum(loss_part[0, 0], ("x", "y", "c"))

    g_slots = []
    for n, shape, axis in BIG_SHARDS:
        if n in ("w_proj_a", "w_proj_b"):
            g_slots.append(jnp.transpose(g[n].reshape(shape[0], N_CHIPS, shape[1]), (1, 0, 2)))
        else:
            g_slots.append(g[n].reshape((N_CHIPS,) + shape))
    g_slots = [t.reshape(N_CHIPS, 2, t.shape[1] // 2, t.shape[2]) for t in g_slots]
    core = ci.astype(jnp.int32).reshape(1)
    from_sibling = grad_pair_swap(g_slots, "grad_pair_swap")
    pair = [pair_sum(gs, rv, core, "grad_pair_sum_" + n) for gs, rv, n in zip(g_slots, from_sibling, big_names)]
    from_chips = grad_chip_exchange(pair, "grad_chip_exchange")
    add2 = lambda a, b: a + b
    add4 = lambda a, b, c_, d: ((a + b) + c_) + d
    halves = []
    for fc, n in zip(from_chips, big_names):
        r2, cols = fc.shape[1], fc.shape[2]
        tr = _tile(r2, 256, 8)
        halves.append(rowcall(add4, "grad_chip_sum_" + n, r2, tr,
                              [(fc.reshape(N_CHIPS * r2, cols), cols, 0, q * r2 // tr) for q in range(N_CHIPS)], [],
                              [(cols, F32)], [])[0])
    others = grad_half_swap(halves, "grad_half_swap")
    big_grads = {n: jnp.where(ci == 0, jnp.concatenate([mine, other], axis=0), jnp.concatenate([other, mine], axis=0))
                 for n, mine, other in zip(big_names, halves, others)}

    small_pack = _pack_rows([g["dmod"], g["dmodc"], g["conv_w"]] + [g[n] for n in SMALL_PARAMS], SMALL_PACK_ROWS, F32)
    small_all = _fill_own(allgather_devices(small_pack, "gather_small_grads", copy_own=False), small_pack, dev)
    small_2d = small_all.reshape(N_DEV * SMALL_PACK_ROWS, PACK_COLS)

    def add8(*a):
        s = a[0]
        for t in a[1:]:
            s = s + t
        return s

    small_sum = rowcall(add8, "small_grad_sum", SMALL_PACK_ROWS, 256,
                        [(small_2d, PACK_COLS, 0, k * SMALL_PACK_ROWS // 256) for k in range(N_DEV)], [],
                        [(PACK_COLS, F32)], [])[0]
    dmod_all = small_all[:, 0:N_MOD].reshape(N_DEV, N_MOD * D)
    dmod_sum = small_sum[0:N_MOD].reshape(1, N_MOD * D)
    dmodc_sum = jnp.pad(small_sum[N_MOD:N_MOD + 2].reshape(1, 2 * D), ((0, 0), (0, (N_MOD - 2) * D)))
    conv_grad = _unpack_rows(small_sum, [(3, 3, 2 * FFN_HIDDEN)], row0=N_MOD + 2)[0]
    small_grads = dict(zip(SMALL_PARAMS, _unpack_rows(small_sum, [wts[n].shape for n in SMALL_PARAMS], row0=SMALL_ROW0)))

    dm16 = jnp.concatenate([dmod_all, dmodc_sum, jnp.zeros((MOD_ROWS - 9, N_MOD * D), F32)], axis=0)
    dm_shard = lax.dynamic_slice(dm16, (0, chip * ADA_COLS), (MOD_ROWS, ADA_COLS))
    g_w_ada = rowcall(f_ada_outer, "w_ada_grad", D, 256, [(jnp.transpose(c16), MOD_ROWS, 0, 0)], [dm_shard],
                      [(ADA_COLS, F32)], [])[0]
    g_b_ada = rowcall(add2, "b_ada_grad", 1, 1, [(dmod_sum, N_MOD * D, 0, 0), (dmodc_sum, N_MOD * D, 0, 0)], [],
                      [(N_MOD * D, F32)], [])[0]
    dmc_rows = jnp.pad(dm_shard[8:9], ((0, 7), (0, 0)))
    cctx_part = matmul(dmc_rows, w_ada[0], "nt", F32, "c_ctx_partial")
    cctx_all = allgather_devices(cctx_part, "gather_c_ctx")
    cctx_4 = jnp.stack([cctx_all[2 * q, 0] for q in range(N_CHIPS)])
    g_c_ctx = rowcall(f_cctx_grad, "c_ctx_grad", 1, 1, [(c_ctx[None], D, 0, 0)], [cctx_4], [(D, F32)], [])[0]

    grads = dict(small_grads)
    grads.update(big_grads)
    grads["w_ada"] = g_w_ada
    grads["b_ada"] = g_b_ada
    grads["c_ctx"] = g_c_ctx
    grads["conv_w"] = lax.dynamic_slice(conv_grad, (0, 0, chip * CONV_SHARD), (3, 3, CONV_SHARD))
    grads = {n: grads[n].reshape(wts[n].shape) for n in WEIGHT_NAMES}

    delta, new_m, new_v = {}, {}, {}
    for n in WEIGHT_NAMES:
        shape2d = (-1, wts[n].shape[-1])
        d_, m_, v_ = adamw(wts[n].reshape(shape2d), grads[n].reshape(shape2d), ms[n].reshape(shape2d),
                           vs[n].reshape(shape2d), "adamw_" + n)
        delta[n], new_m[n], new_v[n] = [t.reshape(wts[n].shape) for t in (d_, m_, v_)]

    return (loss, grad_x[None], *[grads[n] for n in WEIGHT_NAMES], *[delta[n] for n in WEIGHT_NAMES],
            *[new_m[n] for n in WEIGHT_NAMES], *[new_v[n] for n in WEIGHT_NAMES])
```

```python
import functools

import jax
import jax.numpy as jnp
from jax import lax
from jax.experimental import pallas as pl
from jax.experimental.pallas import tpu as pltpu

F32, BF16 = jnp.float32, jnp.bfloat16
MESH = pl.DeviceIdType.MESH

D_MODEL = 1024
S5_WIDTH = 512
S5_GROUP = 16
S5_GROUPS = 32
S5_STATE = 64
SGU_WIDTH = 512
SGU_GROUPS = 8
CHUNK = 128
FFN_HIDDEN = 2816
GRID_W = 64
N_MOD = 6
EPS = 1e-6
N_STATE = S5_GROUPS * S5_STATE
OCTETS = 4
SCAN_T = 128
N_CHIPS = 4
N_DEV = 8
LANES = 128
VMEM_LIMIT_BYTES = 56 * 1024 * 1024
CONV_PAD = 72
CONV_ROWS = 256

ADAM_LR, ADAM_B1, ADAM_B2, ADAM_EPS, ADAM_WD, ADAM_STEP = 0.001, 0.9, 0.999, 1e-08, 0.01, 10


def _call(body, name, out_shape, grid=None, in_specs=None, out_specs=None, scratch=(), sem=None, **kw):
    params = pltpu.CompilerParams(dimension_semantics=sem, vmem_limit_bytes=VMEM_LIMIT_BYTES)
    extra = {} if grid is None else {"grid": grid}
    return pl.pallas_call(body, name=name, out_shape=out_shape, in_specs=in_specs, out_specs=out_specs,
                          scratch_shapes=list(scratch), compiler_params=params, **extra, **kw)


def _tile(n, target, mult=LANES):
    best = None
    t = mult
    while t <= min(n, target):
        if n % t == 0:
            best = t
        t += mult
    return best or n


@jax.custom_vjp
def mmul(a, b):
    return jnp.dot(a.astype(BF16), b.astype(BF16), preferred_element_type=F32)


def _mmul_fwd(a, b):
    return mmul(a, b), (a, b)


def _mmul_bwd(res, ct):
    a, b = res
    ctb = ct.astype(BF16)
    da = lax.dot_general(ctb, b.astype(BF16), (((1,), (1,)), ((), ())), preferred_element_type=F32)
    db = lax.dot_general(a.astype(BF16), ctb, (((0,), (0,)), ((), ())), preferred_element_type=F32)
    return da.astype(a.dtype), db.astype(b.dtype)


mmul.defvjp(_mmul_fwd, _mmul_bwd)

_DOT_DIMS = {"nn": ((1,), (0,)), "nt": ((1,), (1,)), "tn": ((0,), (0,))}


MM_TILE = 1408
MM_FULL_K = 2048


def matmul(a, b, mode, out_dtype, name, init=None, shards=False):
    if mode == "nn":
        (M, K), N = a.shape, (b.shape[2] * N_CHIPS if shards else b.shape[1])
    elif mode == "nt":
        M, N, K = a.shape[0], b.shape[-2], a.shape[1]
    else:
        (K, M), N = a.shape, b.shape[1]
    ns = (K if mode == "nt" else N) // N_CHIPS
    tm = _tile(M, MM_TILE, 8 if M < LANES else LANES)
    tn = _tile(ns if shards and mode != "nt" else N, MM_TILE)
    if shards and mode == "nt":
        tk = _tile(ns, MM_TILE)
    else:
        tk = K if K <= MM_FULL_K else _tile(K, MM_TILE)
    nk = K // tk
    per = ns // (tk if mode == "nt" else tn)
    dims = (_DOT_DIMS[mode], ((), ()))
    has_init = init is not None
    use_acc = nk > 1 and out_dtype != F32

    def body(*refs):
        a_ref, b_ref = refs[:2]
        i_ref = refs[2] if has_init else None
        o_ref = refs[3] if has_init else refs[2]
        acc = refs[-1] if use_acc else o_ref
        k = pl.program_id(2)
        part = lax.dot_general(a_ref[...].astype(BF16), b_ref[...].astype(BF16), dims, preferred_element_type=F32)

        @pl.when(k == 0)
        def _():
            first = part + i_ref[...].astype(F32) if has_init else part
            acc[...] = first.astype(acc.dtype)

        if nk > 1:
            @pl.when(k > 0)
            def _():
                acc[...] += part

        if use_acc:
            @pl.when(k == nk - 1)
            def _():
                o_ref[...] = acc[...].astype(o_ref.dtype)

    if mode == "tn":
        a_spec = pl.BlockSpec((tk, tm), lambda i, j, k: (k, i))
    else:
        a_spec = pl.BlockSpec((tm, tk), lambda i, j, k: (i, k))
    if mode == "nt":
        b_spec = (pl.BlockSpec((None, tn, tk), lambda i, j, k: (k // per, j, k % per)) if shards
                  else pl.BlockSpec((tn, tk), lambda i, j, k: (j, k)))
    else:
        b_spec = (pl.BlockSpec((None, tk, tn), lambda i, j, k: (j // per, k, j % per)) if shards and mode == "nn"
                  else pl.BlockSpec((tk, tn), lambda i, j, k: (k, j)))
    if shards and mode == "tn":
        o_spec = pl.BlockSpec((None, tm, tn), lambda i, j, k: (j // per, i, j % per))
        out_shape = jax.ShapeDtypeStruct((N_CHIPS, M, ns), out_dtype)
    else:
        o_spec = pl.BlockSpec((tm, tn), lambda i, j, k: (i, j))
        out_shape = jax.ShapeDtypeStruct((M, N), out_dtype)
    in_specs = [a_spec, b_spec] + ([o_spec] if has_init else [])
    args = (a, b) + ((init,) if has_init else ())
    return _call(body, name, out_shape, grid=(M // tm, N // tn, nk),
                 in_specs=in_specs, out_specs=o_spec, scratch=[pltpu.VMEM((tm, tn), F32)] if use_acc else [],
                 sem=("parallel", "parallel", "arbitrary"))(*args)


def rowcall(fn, name, nrows, tm, rins, vins, routs, aouts, ainit=None):
    n_r, n_v, n_ro = len(rins), len(vins), len(routs)
    n_i = len(aouts) if ainit is not None else 0

    def body(*refs):
        r_in, v_in, i_in = refs[:n_r], refs[n_r:n_r + n_v], refs[n_r + n_v:n_r + n_v + n_i]
        r_out, a_out = refs[n_r + n_v + n_i:n_r + n_v + n_i + n_ro], refs[n_r + n_v + n_i + n_ro:]
        outs = fn(*[r[...].astype(F32) for r in r_in], *[v[...] for v in v_in])
        if not isinstance(outs, (tuple, list)):
            outs = (outs,)
        for ref, val in zip(r_out, outs[:n_ro]):
            ref[...] = val.astype(ref.dtype)
        if a_out:
            @pl.when(pl.program_id(0) == 0)
            def _():
                for k, ref in enumerate(a_out):
                    ref[...] = i_in[k][...] if n_i else jnp.zeros_like(ref)

            for ref, val in zip(a_out, outs[n_ro:]):
                ref[...] += val.astype(F32)

    def rspec(width, cblk, roff):
        return pl.BlockSpec((tm, width), lambda i: (i + roff, cblk))

    def whole(shape):
        nd = len(shape)
        return pl.BlockSpec(tuple(shape), lambda i: (0,) * nd)

    inits = list(ainit) if n_i else []
    in_specs = [rspec(w, cb, ro) for (_, w, cb, ro) in rins] + [whole(v.shape) for v in vins + inits]
    out_specs = [rspec(w, 0, 0) for (w, _) in routs] + [whole(s) for s in aouts]
    out_shape = [jax.ShapeDtypeStruct((nrows, w), dt) for (w, dt) in routs] + \
                [jax.ShapeDtypeStruct(tuple(s), F32) for s in aouts]
    res = _call(body, name, out_shape, grid=(nrows // tm,), in_specs=in_specs, out_specs=out_specs,
                sem=("arbitrary",))(*[r[0] for r in rins], *vins, *inits)
    return res


def _rms(x):
    return lax.rsqrt(jnp.mean(x * x, axis=-1, keepdims=True) + EPS)


def f_modulate(x, g, sc, sh):
    return (x * _rms(x)) * g * (1.0 + sc) + sh


def f_resid_mod(x, o, ga, g, sc, sh):
    x1 = x + ga * o
    return x1, f_modulate(x1, g, sc, sh)


def f_final_loss(x1, dn, tgt, ga2, gf):
    x2 = x1 + ga2 * dn
    y = (x2 * _rms(x2)) * gf
    err = jnp.square(y - tgt)
    return 0.5 * jnp.sum(jnp.mean(err, axis=-1))


def _sgu_spatial(vn, w, bt):
    lo = lax.broadcasted_iota(jnp.int32, (1, LANES), 1) < (SGU_WIDTH // SGU_GROUPS)
    row_blocks = []
    for r in range(vn.shape[0] // CHUNK):
        rows = vn[r * CHUNK:(r + 1) * CHUNK]
        cols = []
        for j in range(SGU_WIDTH // LANES):
            blk = rows[:, j * LANES:(j + 1) * LANES]
            v_lo = jnp.where(lo, blk, 0.0)
            v_hi = jnp.where(lo, 0.0, blk)
            s = mmul(w[2 * j], v_lo) + mmul(w[2 * j + 1], v_hi)
            bias = jnp.where(lo, bt[:, 2 * j:2 * j + 1], bt[:, 2 * j + 1:2 * j + 2])
            cols.append(s + bias)
        row_blocks.append(jnp.concatenate(cols, axis=1))
    return jnp.concatenate(row_blocks, axis=0) if len(row_blocks) > 1 else row_blocks[0]


def f_mixer(u_a, y0, y1, zu, zv, ga0, ga1, gb0, gb1, d_skip, w_glu, b_glu, ln_g, ln_b, sgu_w, sgu_bt,
            w_pa, w_pb, b_gate):
    ys = u_a * d_skip + y0 + y1
    ge = jax.nn.gelu(ys)
    y_a = ge * jax.nn.sigmoid(mmul(ge, w_glu) + b_glu)
    u_sg = jax.nn.gelu(zu)
    v = jax.nn.gelu(zv)
    vc = v - jnp.mean(v, axis=-1, keepdims=True)
    vn = (vc * lax.rsqrt(jnp.mean(vc * vc, axis=-1, keepdims=True) + EPS)) * ln_g + ln_b
    y_b = u_sg * _sgu_spatial(vn, sgu_w, sgu_bt)
    gl_a = jnp.concatenate([ga0, ga1], axis=1) + b_gate[:, :D_MODEL]
    gl_b = jnp.concatenate([gb0, gb1], axis=1) + b_gate[:, D_MODEL:]
    return jax.nn.sigmoid(gl_a) * mmul(y_a, w_pa) + jax.nn.sigmoid(gl_b) * mmul(y_b, w_pb)


def _cmul(ar, ai, xr, xi):
    return ar * xr - ai * xi, ar * xi + ai * xr


SUB = 8
STRAND = SCAN_T // SUB


def to_strand_order(v):
    n = v.shape[0] // SCAN_T
    return jnp.transpose(v.reshape(n, SUB, STRAND, v.shape[1]), (0, 2, 1, 3)).reshape(v.shape)


def _to_token_order(v):
    i = lax.broadcasted_iota(jnp.int32, (SCAN_T, SCAN_T), 0)
    j = lax.broadcasted_iota(jnp.int32, (SCAN_T, SCAN_T), 1)
    perm = jnp.where(i == STRAND * (j % SUB) + j // SUB, 1.0, 0.0).astype(BF16)
    hi = v.astype(BF16)
    lo = (v - hi.astype(F32)).astype(BF16)
    return jnp.dot(perm, hi, preferred_element_type=F32) + jnp.dot(perm, lo, preferred_element_type=F32)


def _scan_strands(xr, xi, pw_ref, q_ref, col, rev, conj, cr, ci):
    def tab(ref, lo):
        t_r = ref[lo:lo + SUB, col:col + LANES]
        t_i = ref[lo:lo + SUB, col + 512:col + 512 + LANES]
        return t_r, (-t_i if conj else t_i)

    a_r, a_i = tab(pw_ref, (STRAND - 1) * SUB if rev else 0)
    order = list(range(STRAND - 1, -1, -1) if rev else range(STRAND))
    lr, li = [None] * STRAND, [None] * STRAND
    for n, k in enumerate(order):
        lr[k], li[k] = xr[k * SUB:(k + 1) * SUB], xi[k * SUB:(k + 1) * SUB]
        if n:
            m_r, m_i = _cmul(a_r, a_i, lr[order[n - 1]], li[order[n - 1]])
            lr[k], li[k] = lr[k] + m_r, li[k] + m_i
    f_r, f_i = lr[order[-1]], li[order[-1]]
    q_r, q_i = tab(q_ref, 0)
    sub = lax.broadcasted_iota(jnp.int32, (SUB, 1), 0)
    s = 1
    while s < SUB:
        row = (SUB - s) if rev else (s - 1)
        shift = (SUB - s) if rev else s
        m = (sub < SUB - s) if rev else (sub >= s)
        p_r, p_i = _cmul(q_r[row:row + 1], q_i[row:row + 1], pltpu.roll(f_r, shift, 0), pltpu.roll(f_i, shift, 0))
        f_r, f_i = f_r + jnp.where(m, p_r, 0.0), f_i + jnp.where(m, p_i, 0.0)
        s *= 2
    c_r, c_i = jnp.broadcast_to(cr, (SUB, LANES)), jnp.broadcast_to(ci, (SUB, LANES))
    p_r, p_i = _cmul(q_r, q_i, c_r, c_i)
    s_r, s_i = f_r + p_r, f_i + p_i
    edge = 0 if rev else SUB - 1
    first = sub == (SUB - 1 if rev else 0)
    e_r = jnp.where(first, c_r, pltpu.roll(s_r, SUB - 1 if rev else 1, 0))
    e_i = jnp.where(first, c_i, pltpu.roll(s_i, SUB - 1 if rev else 1, 0))
    for k in range(STRAND):
        t_r, t_i = tab(pw_ref, k * SUB)
        p_r, p_i = _cmul(t_r, t_i, e_r, e_i)
        lr[k], li[k] = lr[k] + p_r, li[k] + p_i
    return lr, li, (s_r[edge:edge + 1], s_i[edge:edge + 1]), (e_r, e_i)


SCAN_GROUP = 3


def _lane_cols(o, j):
    col = o * 1024 + j * LANES
    return col, slice(col, col + LANES), slice(col + 512, col + 512 + LANES)


def s5_forward(u, bm, cm, pw, q, rev, name):
    T, G = SCAN_T, SCAN_GROUP
    n_chunks = u.shape[0] // T
    n_steps = n_chunks // G
    order = list(range(G - 1, -1, -1) if rev else range(G))

    def pos(i):
        return (n_steps - 1 - i) if rev else i

    def body(u_ref, bm_ref, cm_ref, pw_ref, q_ref, y_ref, cin_ref, carry):
        @pl.when(pl.program_id(0) == 0)
        def _():
            carry[...] = jnp.zeros_like(carry)

        uv = u_ref[...]
        for o in range(OCTETS):
            bu = jnp.dot(uv[:, o * LANES:(o + 1) * LANES], bm_ref[o], preferred_element_type=F32)
            hr = [[None] * 4 for _ in range(G)]
            hi = [[None] * 4 for _ in range(G)]
            for j in range(4):
                col, sl_r, sl_i = _lane_cols(o, j)
                cr, ci = carry[0:1, sl_r], carry[0:1, sl_i]
                for g in order:
                    rows = slice(g * T, (g + 1) * T)
                    cin_ref[g, 0:1, sl_r] = cr
                    cin_ref[g, 0:1, sl_i] = ci
                    xr, xi, (cr, ci), _ = _scan_strands(
                        bu[rows, j * LANES:(j + 1) * LANES], bu[rows, 512 + j * LANES:512 + (j + 1) * LANES],
                        pw_ref, q_ref, col, rev, False, cr, ci)
                    hr[g][j] = jnp.concatenate(xr, axis=0)
                    hi[g][j] = jnp.concatenate(xi, axis=0)
                carry[0:1, sl_r] = cr
                carry[0:1, sl_i] = ci
            h = jnp.concatenate([jnp.concatenate(hr[g] + hi[g], axis=1) for g in range(G)], axis=0).astype(BF16)
            y = jnp.dot(h, cm_ref[o], preferred_element_type=F32)
            for g in range(G):
                y_ref[g * T:(g + 1) * T, o * LANES:(o + 1) * LANES] = _to_token_order(y[g * T:(g + 1) * T])

    whole3 = lambda s: pl.BlockSpec(s, lambda i: (0, 0, 0))
    whole2 = lambda s: pl.BlockSpec(s, lambda i: (0, 0))
    rows_spec = pl.BlockSpec((G * T, S5_WIDTH), lambda i: (pos(i), 0))
    return _call(
        body, name,
        [jax.ShapeDtypeStruct((n_chunks * T, S5_WIDTH), F32), jax.ShapeDtypeStruct((n_chunks, 1, 2 * N_STATE), F32)],
        grid=(n_steps,),
        in_specs=[rows_spec, whole3(bm.shape), whole3(cm.shape), whole2(pw.shape), whole2(q.shape)],
        out_specs=[rows_spec, pl.BlockSpec((G, 1, 2 * N_STATE), lambda i: (pos(i), 0, 0))],
        scratch=[pltpu.VMEM((1, 2 * N_STATE), F32)], sem=("arbitrary",))(u, bm, cm, pw, q)


def s5_backward(u, dy, cin, bm, cm, pw_h, q_h, pw_l, q_l, rev, name):
    T, G = SCAN_T, SCAN_GROUP
    n_chunks = u.shape[0] // T
    n_steps = n_chunks // G
    adjoint_order = list(range(G) if rev else range(G - 1, -1, -1))

    def pos(i):
        return i if rev else (n_steps - 1 - i)

    def body(u_ref, dy_ref, cin_ref, bm_ref, cm_ref, pwh_ref, qh_ref, pwl_ref, ql_ref, du_ref, dbm_ref, dcm_ref,
             da_ref, lcarry):
        @pl.when(pl.program_id(0) == 0)
        def _():
            lcarry[...] = jnp.zeros_like(lcarry)
            dbm_ref[...] = jnp.zeros_like(dbm_ref)
            dcm_ref[...] = jnp.zeros_like(dcm_ref)
            da_ref[...] = jnp.zeros_like(da_ref)

        uv = u_ref[...]
        dyv = dy_ref[...]
        for o in range(OCTETS):
            u_o = uv[:, o * LANES:(o + 1) * LANES]
            dy_o = dyv[:, o * LANES:(o + 1) * LANES]
            bu = jnp.dot(u_o, bm_ref[o], preferred_element_type=F32)
            gy = lax.dot_general(dy_o, cm_ref[o], (((1,), (1,)), ((), ())), preferred_element_type=F32)
            hs = [[None] * 8 for _ in range(G)]
            ls = [[None] * 8 for _ in range(G)]
            for j in range(4):
                col, sl_r, sl_i = _lane_cols(o, j)
                b_r = slice(j * LANES, (j + 1) * LANES)
                b_i = slice(512 + j * LANES, 512 + (j + 1) * LANES)
                l_r, l_i = lcarry[0:1, sl_r], lcarry[0:1, sl_i]
                acc_r = acc_i = None
                for g in adjoint_order:
                    rows = slice(g * T, (g + 1) * T)
                    xr, xi, _, (e_r, e_i) = _scan_strands(bu[rows, b_r], bu[rows, b_i], pwh_ref, qh_ref, col, rev, False,
                                                          cin_ref[g, 0:1, sl_r], cin_ref[g, 0:1, sl_i])
                    ar_, ai_, (l_r, l_i), _ = _scan_strands(gy[rows, b_r], gy[rows, b_i], pwl_ref, ql_ref, col, not rev,
                                                            True, l_r, l_i)
                    for k in range(STRAND):
                        kp = k + 1 if rev else k - 1
                        p_r, p_i = (e_r, e_i) if not 0 <= kp < STRAND else (xr[kp], xi[kp])
                        t_r = ar_[k] * p_r + ai_[k] * p_i
                        t_i = ai_[k] * p_r - ar_[k] * p_i
                        acc_r, acc_i = (t_r, t_i) if acc_r is None else (acc_r + t_r, acc_i + t_i)
                    hs[g][j], hs[g][4 + j] = jnp.concatenate(xr, axis=0), jnp.concatenate(xi, axis=0)
                    ls[g][j], ls[g][4 + j] = jnp.concatenate(ar_, axis=0), jnp.concatenate(ai_, axis=0)
                lcarry[0:1, sl_r] = l_r
                lcarry[0:1, sl_i] = l_i
                da_ref[0:1, sl_r] += jnp.sum(acc_r, axis=0, keepdims=True)
                da_ref[0:1, sl_i] += jnp.sum(acc_i, axis=0, keepdims=True)
            h = jnp.concatenate([jnp.concatenate(hs[g], axis=1) for g in range(G)], axis=0).astype(BF16)
            lam = jnp.concatenate([jnp.concatenate(ls[g], axis=1) for g in range(G)], axis=0).astype(BF16)
            du = lax.dot_general(lam, bm_ref[o], (((1,), (1,)), ((), ())), preferred_element_type=F32)
            for g in range(G):
                du_ref[g * T:(g + 1) * T, o * LANES:(o + 1) * LANES] = _to_token_order(du[g * T:(g + 1) * T])
            dbm_ref[o] += lax.dot_general(u_o, lam, (((0,), (0,)), ((), ())), preferred_element_type=F32)
            dcm_ref[o] += lax.dot_general(h, dy_o, (((0,), (0,)), ((), ())), preferred_element_type=F32)

    whole3 = lambda s: pl.BlockSpec(s, lambda i: (0, 0, 0))
    whole2 = lambda s: pl.BlockSpec(s, lambda i: (0, 0))
    rows_spec = pl.BlockSpec((G * T, S5_WIDTH), lambda i: (pos(i), 0))
    return _call(
        body, name,
        [jax.ShapeDtypeStruct((n_chunks * T, S5_WIDTH), F32), jax.ShapeDtypeStruct(bm.shape, F32),
         jax.ShapeDtypeStruct(cm.shape, F32), jax.ShapeDtypeStruct((1, 2 * N_STATE), F32)],
        grid=(n_steps,),
        in_specs=[rows_spec, rows_spec, pl.BlockSpec((G, 1, 2 * N_STATE), lambda i: (pos(i), 0, 0)),
                  whole3(bm.shape), whole3(cm.shape), whole2(pw_h.shape), whole2(q_h.shape), whole2(pw_l.shape),
                  whole2(q_l.shape)],
        out_specs=[rows_spec, whole3(bm.shape), whole3(cm.shape), whole2((1, 2 * N_STATE))],
        scratch=[pltpu.VMEM((1, 2 * N_STATE), F32)], sem=("arbitrary",))(u, dy, cin, bm, cm, pw_h, q_h, pw_l, q_l)


def s5_tables(ar, ai, ls):
    def body(ar_ref, ai_ref, ls_ref, *outs):
        dt = jnp.exp(ls_ref[...])
        k = 0
        for n_rows, step in ((STRAND, 1.0), (SUB, float(STRAND))):
            row = lax.broadcasted_iota(jnp.int32, (n_rows, 1), 0)
            for m_int in (row + 1, n_rows - row):
                m = m_int.astype(F32) * step
                mag = jnp.exp(m * (ar_ref[...] * dt))
                ang = m * (ai_ref[...] * dt)
                outs[k][...] = mag * jnp.cos(ang)
                outs[k + 1][...] = mag * jnp.sin(ang)
                k += 2

    vec = pl.BlockSpec((None, 1, N_STATE), lambda d: (d, 0, 0))
    tab = lambda n: pl.BlockSpec((None, n, N_STATE), lambda d: (d, 0, 0))
    shp = lambda n: jax.ShapeDtypeStruct((2, n, N_STATE), F32)
    sizes = [STRAND] * 4 + [SUB] * 4
    return _call(body, "s5_tables", [shp(n) for n in sizes], grid=(2,), in_specs=[vec, vec, vec],
                 out_specs=[tab(n) for n in sizes], sem=("arbitrary",))(ar, ai, ls)


def f_discretize(a_re, a_im, ls, b_re, b_im):
    dt = jnp.exp(ls)
    mag = jnp.exp(a_re * dt)
    ab_re = mag * jnp.cos(a_im * dt)
    ab_im = mag * jnp.sin(a_im * dt)
    p = ab_re - 1.0
    q = ab_im
    den = a_re * a_re + a_im * a_im
    k_re = ((p * a_re + q * a_im) / den)[None]
    k_im = ((q * a_re - p * a_im) / den)[None]
    return ab_re, ab_im, k_re * b_re - k_im * b_im, k_re * b_im + k_im * b_re


def _disc_specs():
    a = pl.BlockSpec((None, S5_GROUPS, S5_STATE), lambda d: (d, 0, 0))
    s = pl.BlockSpec((None, S5_GROUPS, 1), lambda d: (d, 0, 0))
    b = pl.BlockSpec((None, S5_GROUP, S5_GROUPS, S5_STATE), lambda d: (d, 0, 0, 0))
    return a, s, b


def s5_discretize(a_re, a_im, ls, b_re, b_im):
    def body(ar, ai, l, br, bi, obr, obi):
        _, _, r, i = f_discretize(ar[...], ai[...], l[...], br[...], bi[...])
        obr[...] = r
        obi[...] = i

    a, s, b = _disc_specs()
    return _call(body, "s5_discretize", [jax.ShapeDtypeStruct(b_re.shape, F32)] * 2, grid=(2,),
                 in_specs=[a, a, s, b, b], out_specs=[b, b], sem=("arbitrary",))(a_re, a_im, ls, b_re, b_im)


def s5_discretize_bwd(a_re, a_im, ls, b_re, b_im, dab_re, dab_im, dbb_re, dbb_im):
    def body(ar, ai, l, br, bi, c0, c1, c2, c3, o0, o1, o2, o3, o4):
        _, vjp = jax.vjp(f_discretize, ar[...], ai[...], l[...], br[...], bi[...])
        outs = vjp((c0[...], c1[...], c2[...], c3[...]))
        for ref, val in zip((o0, o1, o2, o3, o4), outs):
            ref[...] = val

    a, s, b = _disc_specs()
    shapes = [jax.ShapeDtypeStruct(t.shape, F32) for t in (a_re, a_im, ls, b_re, b_im)]
    return _call(body, "s5_discretize_bwd", shapes, grid=(2,), in_specs=[a, a, s, b, b, a, a, b, b],
                 out_specs=[a, a, s, b, b], sem=("arbitrary",))(a_re, a_im, ls, b_re, b_im, dab_re, dab_im,
                                                                 dbb_re, dbb_im)


def _conv_taps(s_ref, base, n_rows):
    n = n_rows + 2 * CONV_PAD
    ext = s_ref[pl.ds(base, n), :]
    col = (lax.broadcasted_iota(jnp.int32, (n, 1), 0) + (2 * GRID_W - CONV_PAD)) % GRID_W
    left = jnp.where(col == 0, 0.0, pltpu.roll(ext, 1, 0))
    right = jnp.where(col == GRID_W - 1, 0.0, pltpu.roll(ext, n - 1, 0))
    return left, ext, right


def _conv_apply(taps, w_ref, n_rows, flip):
    out = None
    for i in range(3):
        wi = 2 - i if flip else i
        comb = None
        for j in range(3):
            wj = 2 - j if flip else j
            term = w_ref[wi * 3 + wj:wi * 3 + wj + 1, :] * taps[j]
            comb = term if comb is None else comb + term
        start = CONV_PAD + (i - 1) * GRID_W
        part = comb[start:start + n_rows]
        out = part if out is None else out + part
    return out


def _conv_fill(dst_ref, src_ref, n_tok):
    zeros = jnp.zeros((CONV_PAD, LANES), F32)
    dst_ref[0:CONV_PAD, :] = zeros
    dst_ref[CONV_PAD + n_tok:2 * CONV_PAD + n_tok, :] = zeros

    def step(r, carry):
        base = pl.multiple_of(r * CONV_ROWS, CONV_ROWS)
        dst_ref[pl.ds(base + CONV_PAD, CONV_ROWS), :] = src_ref[pl.ds(base, CONV_ROWS), :].astype(F32)
        return carry

    lax.fori_loop(0, n_tok // CONV_ROWS, step, 0)


def conv_forward(up, wg, wv, bias):
    n_tok = up.shape[0]
    nb = FFN_HIDDEN // LANES

    def body(ug_ref, uv_ref, wg_ref, wv_ref, bg_ref, bv_ref, act_ref, gate_ref, val_ref, sg, sv):
        _conv_fill(sg, ug_ref, n_tok)
        _conv_fill(sv, uv_ref, n_tok)

        def step(r, carry):
            base = pl.multiple_of(r * CONV_ROWS, CONV_ROWS)
            gate = _conv_apply(_conv_taps(sg, base, CONV_ROWS), wg_ref, CONV_ROWS, False) + bg_ref[...]
            val = _conv_apply(_conv_taps(sv, base, CONV_ROWS), wv_ref, CONV_ROWS, False) + bv_ref[...]
            act_ref[pl.ds(base, CONV_ROWS), :] = (gate * jax.nn.sigmoid(gate) * val).astype(BF16)
            gate_ref[pl.ds(base, CONV_ROWS), :] = gate.astype(BF16)
            val_ref[pl.ds(base, CONV_ROWS), :] = val.astype(BF16)
            return carry

        lax.fori_loop(0, n_tok // CONV_ROWS, step, 0)

    col = lambda off: pl.BlockSpec((n_tok, LANES), lambda k: (0, k + off))
    wsp = lambda off: pl.BlockSpec((16, LANES), lambda k: (0, k + off))
    bsp = lambda off: pl.BlockSpec((1, LANES), lambda k: (0, k + off))
    pad = pltpu.VMEM((n_tok + 2 * CONV_PAD, LANES), F32)
    half = jax.ShapeDtypeStruct((n_tok, FFN_HIDDEN), BF16)
    return _call(body, "conv_forward", [half, half, half], grid=(nb,),
                 in_specs=[col(0), col(nb), wsp(0), wsp(0), bsp(0), bsp(nb)], out_specs=[col(0), col(0), col(0)],
                 scratch=[pad, pad], sem=("arbitrary",))(up, up, wg, wv, bias, bias)


def conv_backward(up, gate, val, dact, wg, wv):
    n_tok = up.shape[0]
    nb = FFN_HIDDEN // LANES
    n_steps = n_tok // CONV_ROWS

    def body(ug_ref, uv_ref, gate_ref, val_ref, da_ref, wg_ref, wv_ref, dug_ref, duv_ref, dwg_ref, dwv_ref,
             sg, sv, sdg, sdv):
        _conv_fill(sg, ug_ref, n_tok)
        _conv_fill(sv, uv_ref, n_tok)
        zeros = jnp.zeros((CONV_PAD, LANES), F32)
        for s_ref in (sdg, sdv):
            s_ref[0:CONV_PAD, :] = zeros
            s_ref[CONV_PAD + n_tok:2 * CONV_PAD + n_tok, :] = zeros
        dwg_ref[...] = jnp.zeros_like(dwg_ref)
        dwv_ref[...] = jnp.zeros_like(dwv_ref)

        def grads(r, carry):
            base = pl.multiple_of(r * CONV_ROWS, CONV_ROWS)
            taps_g = _conv_taps(sg, base, CONV_ROWS)
            taps_v = _conv_taps(sv, base, CONV_ROWS)
            gate = gate_ref[pl.ds(base, CONV_ROWS), :].astype(F32)
            val = val_ref[pl.ds(base, CONV_ROWS), :].astype(F32)
            d_act = da_ref[pl.ds(base, CONV_ROWS), :].astype(F32)
            sig = jax.nn.sigmoid(gate)
            d_gate = d_act * val * (sig * (1.0 + gate * (1.0 - sig)))
            d_val = d_act * (gate * sig)
            sdg[pl.ds(base + CONV_PAD, CONV_ROWS), :] = d_gate
            sdv[pl.ds(base + CONV_PAD, CONV_ROWS), :] = d_val
            for d_out, taps, dw_ref in ((d_gate, taps_g, dwg_ref), (d_val, taps_v, dwv_ref)):
                for i in range(3):
                    start = CONV_PAD + (i - 1) * GRID_W
                    for j in range(3):
                        k = i * 3 + j
                        dw_ref[k:k + 1, :] += jnp.sum(d_out * taps[j][start:start + CONV_ROWS], axis=0, keepdims=True)
                dw_ref[9:10, :] += jnp.sum(d_out, axis=0, keepdims=True)
            return carry

        lax.fori_loop(0, n_steps, grads, 0)

        def spread(r, carry):
            base = pl.multiple_of(r * CONV_ROWS, CONV_ROWS)
            dug_ref[pl.ds(base, CONV_ROWS), :] = _conv_apply(
                _conv_taps(sdg, base, CONV_ROWS), wg_ref, CONV_ROWS, True).astype(BF16)
            duv_ref[pl.ds(base, CONV_ROWS), :] = _conv_apply(
                _conv_taps(sdv, base, CONV_ROWS), wv_ref, CONV_ROWS, True).astype(BF16)
            return carry

        lax.fori_loop(0, n_steps, spread, 0)

    col = lambda off: pl.BlockSpec((n_tok, LANES), lambda k: (0, k + off))
    wsp = pl.BlockSpec((16, LANES), lambda k: (0, k))
    pad = pltpu.VMEM((n_tok + 2 * CONV_PAD, LANES), F32)
    half = jax.ShapeDtypeStruct((n_tok, FFN_HIDDEN), BF16)
    dw = jax.ShapeDtypeStruct((16, FFN_HIDDEN), F32)
    return _call(body, "conv_backward", [half, half, dw, dw], grid=(nb,),
                 in_specs=[col(0), col(nb), col(0), col(0), col(0), wsp, wsp],
                 out_specs=[col(0), col(0), wsp, wsp], scratch=[pad, pad, pad, pad],
                 sem=("arbitrary",))(up, up, gate, val, dact, wg, wv)


def f_adamw(w, g, m, v):
    m = ADAM_B1 * m + (1.0 - ADAM_B1) * g
    v = ADAM_B2 * v + (1.0 - ADAM_B2) * jnp.square(g)
    m_hat = m / (1.0 - ADAM_B1 ** ADAM_STEP)
    v_hat = v / (1.0 - ADAM_B2 ** ADAM_STEP)
    delta = -ADAM_LR * (m_hat / (jnp.sqrt(v_hat) + ADAM_EPS) + ADAM_WD * w)
    return delta, m, v


def adamw(w, g, m, v, name):
    shape = w.shape
    cols = shape[-1]
    rows = w.size // cols
    two_d = [t.reshape(rows, cols) for t in (w, g, m, v)]
    tm = _tile(rows, 256, 8) if rows % 8 == 0 else rows
    outs = rowcall(f_adamw, name, rows, tm, [(t, cols, 0, 0) for t in two_d], [], [(cols, F32)] * 3, [])
    return tuple(o.reshape(shape) for o in outs)


def _place():
    return lax.axis_index("x"), lax.axis_index("y"), lax.axis_index("c")


_ANY = pl.BlockSpec(memory_space=pl.ANY)


def _fill_own(gathered, own, index):
    return lax.dynamic_update_slice(gathered, own[None], (index,) + (0,) * own.ndim)


def allgather_devices(v, name, copy_own=True):
    def body(v_ref, out_ref, send_sems, recv_sems, local_sem):
        x, y, c = _place()
        me, sibling = (x, y, c), (x, y, 1 - c)
        chips = [(1 - x, y), (x, 1 - y), (1 - x, 1 - y)]

        def slot(p):
            return out_ref.at[4 * p[0] + 2 * p[1] + p[2]]

        def copy(k, block, to, src=None):
            return pltpu.make_async_remote_copy(
                src_ref=slot(block) if src is None else src, dst_ref=slot(block),
                send_sem=send_sems.at[k], recv_sem=recv_sems.at[k], device_id=to, device_id_type=MESH)

        mine = pltpu.make_async_copy(v_ref, slot(me), local_sem)
        if copy_own:
            mine.start()
        first = [copy(0, me, sibling, src=v_ref)]
        first += [copy(1 + j, me, (*chip, c), src=v_ref) for j, chip in enumerate(chips)]
        for cp in first:
            cp.start()
        passed = [copy(4 + j, (*chip, c), sibling) for j, chip in enumerate(chips)]
        for j, chip in enumerate(chips):
            copy(1 + j, (*chip, c), me).wait_recv()
            passed[j].start()
        copy(0, sibling, me).wait_recv()
        for j, chip in enumerate(chips):
            copy(4 + j, (*chip, 1 - c), me).wait_recv()
        for cp in first + passed:
            cp.wait_send()
        if copy_own:
            mine.wait()

    return _call(body, name, jax.ShapeDtypeStruct((N_DEV,) + v.shape, v.dtype), in_specs=[_ANY], out_specs=_ANY,
                 scratch=[pltpu.SemaphoreType.DMA((7,)), pltpu.SemaphoreType.DMA((7,)), pltpu.SemaphoreType.DMA])(v)


def _other_chips(x, y):
    return [(1 - x, y), (x, 1 - y), (1 - x, 1 - y)]


def allgather_chips(vs, name):
    n = len(vs)
    shapes = [v.shape for v in vs]
    vs = [v.reshape((2, v.shape[0] // 2) + v.shape[1:]) for v in vs]

    def body(*refs):
        v_refs, o_refs = refs[:n], refs[n:2 * n]
        send_sems, recv_sems = refs[2 * n:]
        x, y, c = _place()
        sibling = (x, y, 1 - c)
        chips = _other_chips(x, y)

        def rows(a, chip, h):
            return o_refs[a].at[2 * chip[0] + chip[1], h]

        def copy(a, k, chip, h, to, src=None):
            return pltpu.make_async_remote_copy(
                src_ref=rows(a, chip, h) if src is None else src, dst_ref=rows(a, chip, h),
                send_sem=send_sems.at[6 * a + k], recv_sem=recv_sems.at[6 * a + k], device_id=to, device_id_type=MESH)

        first = [copy(a, j, (x, y), c, (*chip, c), src=v_refs[a].at[c])
                 for a in range(n) for j, chip in enumerate(chips)]
        for cp in first:
            cp.start()
        passed = []
        for j, chip in enumerate(chips):
            for a in range(n):
                copy(a, j, chip, c, (x, y, c)).wait_recv()
                passed.append(copy(a, 3 + j, chip, c, sibling))
                passed[-1].start()
        for j, chip in enumerate(chips):
            for a in range(n):
                copy(a, 3 + j, chip, 1 - c, (x, y, c)).wait_recv()
        for cp in first + passed:
            cp.wait_send()

    outs = _call(body, name, [jax.ShapeDtypeStruct((N_CHIPS,) + v.shape, v.dtype) for v in vs], in_specs=[_ANY] * n,
                 out_specs=[_ANY] * n,
                 scratch=[pltpu.SemaphoreType.DMA((6 * n,)), pltpu.SemaphoreType.DMA((6 * n,))])(*vs)
    return [o.reshape((N_CHIPS,) + s) for o, s in zip(outs, shapes)]


def grad_pair_swap(gs, name):
    n = len(gs)

    def body(*refs):
        g_refs, o_refs, send_sems, recv_sems = refs[:n], refs[n:2 * n], refs[2 * n], refs[2 * n + 1]
        x, y, c = _place()
        cps = [pltpu.make_async_remote_copy(
            src_ref=g_refs[a].at[:, 1 - c], dst_ref=o_refs[a], send_sem=send_sems.at[a], recv_sem=recv_sems.at[a],
            device_id=(x, y, 1 - c), device_id_type=MESH) for a in range(n)]
        for cp in cps:
            cp.start()
        for cp in cps:
            cp.wait()

    return _call(body, name, [jax.ShapeDtypeStruct((N_CHIPS,) + g.shape[2:], g.dtype) for g in gs],
                 in_specs=[_ANY] * n, out_specs=[_ANY] * n,
                 scratch=[pltpu.SemaphoreType.DMA((n,)), pltpu.SemaphoreType.DMA((n,))])(*gs)


def pair_sum(g, recv, core, name):
    r2, cols = recv.shape[1], recv.shape[2]
    tr = _tile(r2, 256, 8)
    nt = r2 // tr

    def body(c_ref, g_ref, r_ref, o_ref):
        o_ref[...] = (g_ref[...] + r_ref[...]).astype(o_ref.dtype)

    spec = pl.BlockSpec((None, tr, cols), lambda q, i, c_ref: (q, i, 0))
    grid_spec = pltpu.PrefetchScalarGridSpec(
        num_scalar_prefetch=1, grid=(N_CHIPS, nt),
        in_specs=[pl.BlockSpec((None, None, tr, cols), lambda q, i, c_ref: (q, c_ref[0], i, 0)), spec], out_specs=spec)
    return pl.pallas_call(body, name=name, out_shape=jax.ShapeDtypeStruct(recv.shape, BF16), grid_spec=grid_spec,
                          compiler_params=pltpu.CompilerParams(dimension_semantics=("arbitrary", "arbitrary"),
                                                               vmem_limit_bytes=VMEM_LIMIT_BYTES))(core, g, recv)


def grad_chip_exchange(ps, name):
    n = len(ps)

    def body(*refs):
        p_refs, o_refs = refs[:n], refs[n:2 * n]
        send_sems, recv_sems, local_sems = refs[2 * n:]
        x, y, c = _place()
        chips = _other_chips(x, y)
        me = 2 * x + y
        mine = [pltpu.make_async_copy(p_refs[a].at[me], o_refs[a].at[me], local_sems.at[a]) for a in range(n)]
        sends = [pltpu.make_async_remote_copy(
            src_ref=p_refs[a].at[2 * chip[0] + chip[1]], dst_ref=o_refs[a].at[me], send_sem=send_sems.at[3 * a + j],
            recv_sem=recv_sems.at[3 * a + j], device_id=(*chip, c), device_id_type=MESH)
            for a in range(n) for j, chip in enumerate(chips)]
        for cp in mine + sends:
            cp.start()
        for a in range(n):
            for j, chip in enumerate(chips):
                pltpu.make_async_remote_copy(
                    src_ref=p_refs[a].at[me], dst_ref=o_refs[a].at[2 * chip[0] + chip[1]],
                    send_sem=send_sems.at[3 * a + j], recv_sem=recv_sems.at[3 * a + j], device_id=(*chip, c),
                    device_id_type=MESH).wait_recv()
        for cp in sends:
            cp.wait_send()
        for cp in mine:
            cp.wait()

    return _call(body, name, [jax.ShapeDtypeStruct(p.shape, p.dtype) for p in ps], in_specs=[_ANY] * n,
                 out_specs=[_ANY] * n, scratch=[pltpu.SemaphoreType.DMA((3 * n,)), pltpu.SemaphoreType.DMA((3 * n,)),
                                                pltpu.SemaphoreType.DMA((n,))])(*ps)


def grad_half_swap(ss, name):
    n = len(ss)

    def body(*refs):
        s_refs, o_refs, send_sems, recv_sems = refs[:n], refs[n:2 * n], refs[2 * n], refs[2 * n + 1]
        x, y, c = _place()
        cps = [pltpu.make_async_remote_copy(
            src_ref=s_refs[a], dst_ref=o_refs[a], send_sem=send_sems.at[a], recv_sem=recv_sems.at[a],
            device_id=(x, y, 1 - c), device_id_type=MESH) for a in range(n)]
        for cp in cps:
            cp.start()
        for cp in cps:
            cp.wait()

    return _call(body, name, [jax.ShapeDtypeStruct(s.shape, s.dtype) for s in ss], in_specs=[_ANY] * n,
                 out_specs=[_ANY] * n, scratch=[pltpu.SemaphoreType.DMA((n,)), pltpu.SemaphoreType.DMA((n,))])(*ss)


ROW_TILE = 256

BIG_SHARDS = (("w_in", (D_MODEL, 896), 1), ("s5_w_glu", (128, S5_WIDTH), 0), ("w_proj_a", (S5_WIDTH, 256), 1),
              ("w_proj_b", (SGU_WIDTH, 256), 1), ("w_out", (256, D_MODEL), 0), ("w_up", (D_MODEL, 1408), 1),
              ("w_down", (704, D_MODEL), 0))
SMALL_PARAMS = ("g_mix", "s5_a_re", "s5_a_im", "s5_log_step", "s5_b_re", "s5_b_im", "s5_c_re", "s5_c_im", "s5_d",
                "s5_b_glu", "sgu_ln_g", "sgu_ln_b", "sgu_w", "sgu_b", "b_gate", "g_ffn", "conv_b", "g_final")
PACK_COLS = 1024


def _rows_of(n):
    return -(-n // PACK_COLS)


def _pack_rows(arrays, total_rows, dtype):
    parts = []
    used = 0
    for a in arrays:
        r = _rows_of(a.size)
        parts.append(jnp.pad(a.reshape(-1).astype(dtype), (0, r * PACK_COLS - a.size)).reshape(r, PACK_COLS))
        used += r
    if total_rows > used:
        parts.append(jnp.zeros((total_rows - used, PACK_COLS), dtype))
    return jnp.concatenate(parts, axis=0)


def _unpack_rows(packed, shapes, row0=0):
    out = []
    for s in shapes:
        n = 1
        for d in s:
            n *= d
        r = _rows_of(n)
        out.append(packed[row0:row0 + r].reshape(-1)[:n].reshape(s))
        row0 += r
    return out


def _octet_major(re, im):
    parts = []
    for o in range(OCTETS):
        parts += [re[:, o * 512:(o + 1) * 512], im[:, o * 512:(o + 1) * 512]]
    return jnp.concatenate(parts, axis=1)


def _octet_split(v):
    v = v.reshape(OCTETS, 2, 512)
    return v[:, 0].reshape(N_STATE), v[:, 1].reshape(N_STATE)


def _s5_bmat(bb):
    t = bb.reshape(S5_GROUP, OCTETS, 8, 1, S5_STATE) * jnp.eye(8, dtype=F32)[None, None, :, :, None]
    return jnp.transpose(t, (1, 2, 0, 3, 4)).reshape(OCTETS, LANES, 512)


def _s5_bmat_t(dm):
    t = dm.reshape(OCTETS, 8, S5_GROUP, 8, S5_STATE) * jnp.eye(8, dtype=F32)[None, :, None, :, None]
    return jnp.transpose(t.sum(axis=3), (2, 0, 1, 3)).reshape(S5_GROUP, S5_GROUPS, S5_STATE)


def _s5_cmat(c):
    t = c.reshape(OCTETS, 8, 1, S5_GROUP, S5_STATE) * jnp.eye(8, dtype=F32)[None, :, :, None, None]
    return jnp.transpose(t, (0, 2, 4, 1, 3)).reshape(OCTETS, 512, LANES)


def _s5_cmat_t(dm):
    t = dm.reshape(OCTETS, 8, S5_STATE, 8, S5_GROUP) * jnp.eye(8, dtype=F32)[None, :, None, :, None]
    return jnp.transpose(t.sum(axis=1), (0, 2, 3, 1)).reshape(S5_GROUPS, S5_GROUP, S5_STATE)


def local_step(x, ctx, tgt, mod, modc, W):
    n_tok, n_ctx = x.shape[0], ctx.shape[0]
    tm = ROW_TILE
    D = D_MODEL
    sh1, sc1, ga1, sh2, sc2, ga2 = [mod[:, k * D:(k + 1) * D] for k in range(N_MOD)]
    sh1c, sc1c = modc[:, :D], modc[:, D:2 * D]
    g_mix, g_ffn, g_final = W["g_mix"], W["g_ffn"], W["g_final"]
    w_in = W["w_in"]
    w_in_u = w_in[0][:, :S5_WIDTH]

    h = rowcall(f_modulate, "mod1", n_tok, tm, [(x, D, 0, 0)], [g_mix, sc1, sh1], [(D, BF16)], [])[0]
    hc = rowcall(f_modulate, "mod1_ctx", n_ctx, tm, [(ctx, D, 0, 0)], [g_mix, sc1c, sh1c], [(D, BF16)], [])[0]
    proj = matmul(h, w_in, "nn", BF16, "proj_in", shards=True)
    uc = matmul(hc, w_in_u, "nn", BF16, "proj_in_ctx")
    u_lat = proj[:, :S5_WIDTH]
    u_s5 = (to_strand_order(jnp.concatenate([uc, u_lat], axis=0)), to_strand_order(jnp.concatenate([u_lat, uc], axis=0)))

    a_re, a_im, ls = W["s5_a_re"], W["s5_a_im"], W["s5_log_step"][..., None]
    b_re_t = jnp.transpose(W["s5_b_re"], (0, 3, 1, 2))
    b_im_t = jnp.transpose(W["s5_b_im"], (0, 3, 1, 2))
    bb_re, bb_im = s5_discretize(a_re, a_im, ls, b_re_t, b_im_t)
    ls_rep = jnp.repeat(W["s5_log_step"], S5_STATE, axis=1).reshape(2, 1, N_STATE)
    tabs = s5_tables(a_re.reshape(2, 1, N_STATE), a_im.reshape(2, 1, N_STATE), ls_rep)
    pw, qt, bm, cm = [], [], [], []
    for d in range(2):
        pw.append(tuple(jnp.repeat(_octet_major(tabs[k][d], tabs[k + 1][d]), SUB, axis=0) for k in (0, 2)))
        qt.append(tuple(_octet_major(tabs[k][d], tabs[k + 1][d]) for k in (4, 6)))
        bm.append(jnp.concatenate([_s5_bmat(bb_re[d]), _s5_bmat(bb_im[d])], axis=2).astype(BF16))
        cm.append(jnp.concatenate([_s5_cmat(W["s5_c_re"][d]), -_s5_cmat(W["s5_c_im"][d])], axis=1).astype(BF16))
    y0, cin0 = s5_forward(u_s5[0], bm[0], cm[0], pw[0][0], qt[0][0], False, "s5_fwd0")
    y1, cin1 = s5_forward(u_s5[1], bm[1], cm[1], pw[1][1], qt[1][1], True, "s5_fwd1")

    mix_rows = [(proj, 512, 0, 0), (y0, 512, 0, n_ctx // tm), (y1, 512, 0, 0)] + \
               [(proj, 512, k, 0) for k in range(1, 7)]
    mix_vecs = [W["s5_d"], W["s5_w_glu"], W["s5_b_glu"], W["sgu_ln_g"], W["sgu_ln_b"], W["sgu_w"],
                jnp.transpose(W["sgu_b"]), W["w_proj_a"], W["w_proj_b"], W["b_gate"]]
    mrg = rowcall(f_mixer, "mixer", n_tok, tm, mix_rows, mix_vecs, [(D, BF16)], [])[0]
    o = matmul(mrg, W["w_out"], "nn", F32, "proj_out")
    x1, h2 = rowcall(f_resid_mod, "resid_mod2", n_tok, tm, [(x, D, 0, 0), (o, D, 0, 0)], [ga1, g_ffn, sc2, sh2],
                     [(D, F32), (D, BF16)], [])
    up = matmul(h2, W["w_up"], "nn", BF16, "ffn_up", shards=True)
    conv_w = W["conv_w"].reshape(9, 2 * FFN_HIDDEN)
    wg = jnp.pad(conv_w[:, :FFN_HIDDEN], ((0, 7), (0, 0)))
    wv = jnp.pad(conv_w[:, FFN_HIDDEN:], ((0, 7), (0, 0)))
    act, gate, val = conv_forward(up, wg, wv, W["conv_b"])
    dn = matmul(act, W["w_down"], "nn", F32, "ffn_down")

    def final_fn(x1_, dn_, tgt_, ga2_, gf_):
        loss, (dx1_, ddn_, dga2_, dgf_) = jax.value_and_grad(f_final_loss, argnums=(0, 1, 3, 4))(
            x1_, dn_, tgt_, ga2_, gf_)
        return dx1_, ddn_, loss.reshape(1, 1), dga2_, dgf_

    dx2, ddn, loss, d_ga2, d_gfinal = rowcall(
        final_fn, "final_loss", n_tok, tm, [(x1, D, 0, 0), (dn, D, 0, 0), (tgt, D, 0, 0)], [ga2, g_final],
        [(D, F32), (D, BF16)], [(1, 1), (1, D), (1, D)])

    dact = matmul(ddn, W["w_down"], "nt", BF16, "ffn_down_dx")
    d_w_down = matmul(act, ddn, "tn", F32, "ffn_down_dw")
    dup_g, dup_v, dwg, dwv = conv_backward(up, gate, val, dact, wg, wv)
    dup = jnp.concatenate([dup_g, dup_v], axis=1)
    d_conv_w = jnp.concatenate([dwg[:9], dwv[:9]], axis=1).reshape(3, 3, 2 * FFN_HIDDEN)
    d_conv_b = jnp.concatenate([dwg[9:10], dwv[9:10]], axis=1)
    dh2 = matmul(dup, W["w_up"], "nt", BF16, "ffn_up_dx", shards=True)
    d_w_up = matmul(h2, dup, "tn", F32, "ffn_up_dw", shards=True)

    def resid_bwd(x_, o_, dx1_, dh2_, ga_, g_, sc_, sh_):
        _, vjp = jax.vjp(f_resid_mod, x_, o_, ga_, g_, sc_, sh_)
        return vjp((dx1_, dh2_))

    dxa, do, d_ga1, d_gffn, d_sc2, d_sh2 = rowcall(
        resid_bwd, "resid_mod2_bwd", n_tok, tm, [(x, D, 0, 0), (o, D, 0, 0), (dx2, D, 0, 0), (dh2, D, 0, 0)],
        [ga1, g_ffn, sc2, sh2], [(D, F32), (D, BF16)], [(1, D)] * 4)

    dmrg = matmul(do, W["w_out"], "nt", BF16, "proj_out_dx")
    d_w_out = matmul(mrg, do, "tn", F32, "proj_out_dw")

    def mixer_bwd(*args):
        rows, dm, vecs = args[:9], args[9], [v.astype(F32) for v in args[10:]]
        _, vjp = jax.vjp(f_mixer, *rows, *vecs)
        g = vjp(dm)
        return (g[0], g[1], jnp.concatenate([g[3], g[4]], axis=1), jnp.concatenate(g[5:9], axis=1)) + tuple(g[9:])

    mb = rowcall(mixer_bwd, "mixer_bwd", n_tok, tm, mix_rows + [(dmrg, D, 0, 0)], mix_vecs,
                 [(512, BF16), (512, BF16), (1024, BF16), (2048, BF16)], [v.shape for v in mix_vecs])
    du_direct, dys, dzb, dgl = mb[:4]
    d_s5d, d_w_glu, d_b_glu, d_ln_g, d_ln_b, d_sgu_w, d_sgu_bt, d_w_pa, d_w_pb, d_b_gate = mb[4:]

    zc = jnp.zeros((n_ctx, S5_WIDTH), BF16)
    dy_s5 = (to_strand_order(jnp.concatenate([zc, dys], axis=0)), to_strand_order(jnp.concatenate([dys, zc], axis=0)))
    du0, dbm0, dcm0, da0 = s5_backward(u_s5[0], dy_s5[0], cin0, bm[0], cm[0], pw[0][0], qt[0][0], pw[0][1], qt[0][1],
                                       False, "s5_bwd0")
    du1, dbm1, dcm1, da1 = s5_backward(u_s5[1], dy_s5[1], cin1, bm[1], cm[1], pw[1][1], qt[1][1], pw[1][0], qt[1][0],
                                       True, "s5_bwd1")
    add3 = lambda a, b, c: a + b + c
    du_a = rowcall(add3, "du_sum", n_tok, tm, [(du_direct, 512, 0, 0), (du0, 512, 0, n_ctx // tm), (du1, 512, 0, 0)],
                   [], [(512, BF16)], [])[0]
    du_c = rowcall(lambda a, b: a + b, "du_sum_ctx", n_ctx, tm, [(du0, 512, 0, 0), (du1, 512, 0, n_tok // tm)],
                   [], [(512, BF16)], [])[0]

    dab_re, dab_im, dbb_re, dbb_im, d_c_re, d_c_im = [], [], [], [], [], []
    for dbm, dcm, da in ((dbm0, dcm0, da0), (dbm1, dcm1, da1)):
        r, i = _octet_split(da)
        dab_re.append(r.reshape(S5_GROUPS, S5_STATE))
        dab_im.append(i.reshape(S5_GROUPS, S5_STATE))
        dbb_re.append(_s5_bmat_t(dbm[:, :, :512]))
        dbb_im.append(_s5_bmat_t(dbm[:, :, 512:]))
        d_c_re.append(_s5_cmat_t(dcm[:, :512]))
        d_c_im.append(-_s5_cmat_t(dcm[:, 512:]))
    d_a_re, d_a_im, d_ls, d_b_re_t, d_b_im_t = s5_discretize_bwd(
        a_re, a_im, ls, b_re_t, b_im_t, jnp.stack(dab_re), jnp.stack(dab_im), jnp.stack(dbb_re), jnp.stack(dbb_im))

    dproj = jnp.concatenate([du_a, dzb, dgl], axis=1)
    dh = matmul(dproj, w_in, "nt", BF16, "proj_in_dx", shards=True)
    dhc = matmul(du_c, w_in_u, "nt", BF16, "proj_in_ctx_dx")
    d_w_in_c = matmul(hc, du_c, "tn", F32, "proj_in_ctx_dw")
    d_w_in = matmul(h, dproj, "tn", F32, "proj_in_dw", shards=True,
                    init=jnp.pad(d_w_in_c[None], ((0, N_CHIPS - 1), (0, 0), (0, w_in.shape[2] - S5_WIDTH))))

    def mod_bwd_ctx(x_, dh_, g_, sc_, sh_):
        _, vjp = jax.vjp(f_modulate, x_, g_, sc_, sh_)
        return vjp(dh_)[1:]

    d_gmix_c, d_sc1c, d_sh1c = rowcall(mod_bwd_ctx, "mod1_ctx_bwd", n_ctx, tm, [(ctx, D, 0, 0), (dhc, D, 0, 0)],
                                       [g_mix, sc1c, sh1c], [], [(1, D)] * 3)

    def mod_bwd(x_, dh_, dxa_, g_, sc_, sh_):
        _, vjp = jax.vjp(f_modulate, x_, g_, sc_, sh_)
        dx_, dg_, dsc_, dsh_ = vjp(dh_)
        return dx_ + dxa_, dg_, dsc_, dsh_

    zero_d = jnp.zeros((1, D), F32)
    grad_x, d_gmix, d_sc1, d_sh1 = rowcall(
        mod_bwd, "mod1_bwd", n_tok, tm, [(x, D, 0, 0), (dh, D, 0, 0), (dxa, D, 0, 0)], [g_mix, sc1, sh1],
        [(D, F32)], [(1, D)] * 3, ainit=[d_gmix_c, zero_d, zero_d])

    grads = {
        "dmod": jnp.concatenate([d_sh1, d_sc1, d_ga1, d_sh2, d_sc2, d_ga2], axis=1),
        "dmodc": jnp.concatenate([d_sh1c, d_sc1c], axis=1),
        "g_mix": d_gmix,
        "s5_a_re": d_a_re, "s5_a_im": d_a_im, "s5_log_step": d_ls[..., 0],
        "s5_b_re": jnp.transpose(d_b_re_t, (0, 2, 3, 1)), "s5_b_im": jnp.transpose(d_b_im_t, (0, 2, 3, 1)),
        "s5_c_re": jnp.stack(d_c_re), "s5_c_im": jnp.stack(d_c_im), "s5_d": d_s5d, "s5_b_glu": d_b_glu,
        "sgu_ln_g": d_ln_g, "sgu_ln_b": d_ln_b, "sgu_w": d_sgu_w, "sgu_b": jnp.transpose(d_sgu_bt),
        "b_gate": d_b_gate, "g_ffn": d_gffn, "conv_b": d_conv_b, "g_final": d_gfinal, "conv_w": d_conv_w,
        "w_in": d_w_in, "s5_w_glu": d_w_glu, "w_proj_a": d_w_pa, "w_proj_b": d_w_pb, "w_out": d_w_out,
        "w_up": d_w_up, "w_down": d_w_down,
    }
    return loss, grad_x, grads


ADA_COLS = N_MOD * D_MODEL // N_CHIPS
MOD_ROWS = 16


def mod_forward(c16, w, b):
    n = w.shape[1]
    tn = 512

    def body(c_ref, w_ref, b_ref, o_ref):
        cv = c_ref[...]
        cs = cv * jax.nn.sigmoid(cv)
        o_ref[...] = jnp.dot(cs.astype(BF16), w_ref[...].astype(BF16), preferred_element_type=F32) + b_ref[...]

    return _call(body, "mod_forward", jax.ShapeDtypeStruct((MOD_ROWS, n), F32), grid=(n // tn,),
                 in_specs=[pl.BlockSpec((MOD_ROWS, D_MODEL), lambda j: (0, 0)),
                           pl.BlockSpec((D_MODEL, tn), lambda j: (0, j)), pl.BlockSpec((1, tn), lambda j: (0, j))],
                 out_specs=pl.BlockSpec((MOD_ROWS, tn), lambda j: (0, j)), sem=("arbitrary",))(c16, w, b)


def f_ada_outer(ct, dm):
    cs = ct * jax.nn.sigmoid(ct)
    acc = cs[:, 0:1] * dm[0:1]
    for k in range(1, 9):
        acc = acc + cs[:, k:k + 1] * dm[k:k + 1]
    return acc


def f_cctx_grad(z, p4):
    s = jax.nn.sigmoid(z)
    return (p4[0:1] + p4[1:2] + p4[2:3] + p4[3:4]) * (s + z * s * (1.0 - s))


WEIGHT_NAMES = ("c_ctx", "w_ada", "b_ada", "g_mix", "w_in", "s5_a_re", "s5_a_im", "s5_log_step", "s5_b_re",
                "s5_b_im", "s5_c_re", "s5_c_im", "s5_d", "s5_w_glu", "s5_b_glu", "sgu_ln_g", "sgu_ln_b", "sgu_w",
                "sgu_b", "w_proj_a", "w_proj_b", "b_gate", "w_out", "g_ffn", "w_up", "conv_w", "conv_b", "w_down",
                "g_final")
CONV_SHARD = 2 * FFN_HIDDEN // N_CHIPS
SMALL_PACK_ROWS = 512
SMALL_ROW0 = 58


def kernel(x, c, ctx, c_ctx, w_ada, b_ada, g_mix, w_in, s5_a_re, s5_a_im, s5_log_step, s5_b_re, s5_b_im, s5_c_re, s5_c_im, s5_d, s5_w_glu, s5_b_glu, sgu_ln_g, sgu_ln_b, sgu_w, sgu_b, w_proj_a, w_proj_b, b_gate, w_out, g_ffn, w_up, conv_w, conv_b, w_down, g_final, loss_target, m_c_ctx, m_w_ada, m_b_ada, m_g_mix, m_w_in, m_s5_a_re, m_s5_a_im, m_s5_log_step, m_s5_b_re, m_s5_b_im, m_s5_c_re, m_s5_c_im, m_s5_d, m_s5_w_glu, m_s5_b_glu, m_sgu_ln_g, m_sgu_ln_b, m_sgu_w, m_sgu_b, m_w_proj_a, m_w_proj_b, m_b_gate, m_w_out, m_g_ffn, m_w_up, m_conv_w, m_conv_b, m_w_down, m_g_final, v_c_ctx, v_w_ada, v_b_ada, v_g_mix, v_w_in, v_s5_a_re, v_s5_a_im, v_s5_log_step, v_s5_b_re, v_s5_b_im, v_s5_c_re, v_s5_c_im, v_s5_d, v_s5_w_glu, v_s5_b_glu, v_sgu_ln_g, v_sgu_ln_b, v_sgu_w, v_sgu_b, v_w_proj_a, v_w_proj_b, v_b_gate, v_w_out, v_g_ffn, v_w_up, v_conv_w, v_conv_b, v_w_down, v_g_final):
    given = dict(locals())
    wts = {n: given[n] for n in WEIGHT_NAMES}
    ms = {n: given["m_" + n] for n in WEIGHT_NAMES}
    vs = {n: given["v_" + n] for n in WEIGHT_NAMES}
    xi, yi, ci = _place()
    chip = 2 * xi + yi
    dev = 2 * chip + ci
    D = D_MODEL

    c8 = allgather_devices(jnp.pad(c, ((0, 7), (0, 0))), "gather_c")[:, 0, :]
    c16 = jnp.concatenate([c8, c_ctx[None], jnp.zeros((MOD_ROWS - 9, D), F32)], axis=0)
    b_shard = lax.dynamic_slice(b_ada, (0, chip * ADA_COLS), (1, ADA_COLS))
    mod_shard = mod_forward(c16, w_ada[0], b_shard)
    mod_all = allgather_devices(mod_shard, "gather_mod")
    mod_full = jnp.concatenate([mod_all[2 * q] for q in range(N_CHIPS)], axis=1)
    mod = lax.dynamic_slice(mod_full, (dev, 0), (1, N_MOD * D))
    modc = mod_full[8:9]

    big_names = [n for n, _, _ in BIG_SHARDS]
    conv_rows = jnp.pad(conv_w[0].reshape(9, CONV_SHARD), ((0, 7), (0, 0)))
    shards = [wts[n][0].astype(BF16) for n in big_names] + [conv_rows]
    gathered = allgather_chips(shards, "gather_weights")
    gathered = [_fill_own(t, s, chip) for t, s in zip(gathered, shards)]
    W = dict(zip(big_names, gathered[:-1]))
    for n, shape, axis in BIG_SHARDS:
        if axis == 0:
            W[n] = W[n].reshape(N_CHIPS * shape[0], shape[1])
    for n in ("w_proj_a", "w_proj_b"):
        W[n] = jnp.transpose(W[n], (1, 0, 2)).reshape(W[n].shape[1], -1)
    W["conv_w"] = jnp.transpose(gathered[-1][:, :9], (1, 0, 2)).reshape(3, 3, 2 * FFN_HIDDEN)
    for n in ("g_mix", "g_ffn", "s5_d", "s5_b_glu", "sgu_ln_g", "sgu_ln_b", "b_gate", "conv_b"):
        W[n] = wts[n]
    W["g_final"] = g_final[None]
    for n in ("s5_a_re", "s5_a_im", "s5_log_step", "s5_b_re", "s5_b_im", "s5_c_re", "s5_c_im", "sgu_w", "sgu_b"):
        W[n] = wts[n][0]

    loss_part, grad_x, g = local_step(x[0], ctx[0], loss_target[0], mod, modc, W)
    loss = lax.psum(loss_part[0, 0], ("x", "y", "c"))

    g_slots = []
    for n, shape, axis in BIG_SHARDS:
        if n in ("w_proj_a", "w_proj_b"):
            g_slots.append(jnp.transpose(g[n].reshape(shape[0], N_CHIPS, shape[1]), (1, 0, 2)))
        else:
            g_slots.append(g[n].reshape((N_CHIPS,) + shape))
    g_slots = [t.reshape(N_CHIPS, 2, t.shape[1] // 2, t.shape[2]) for t in g_slots]
    core = ci.astype(jnp.int32).reshape(1)
    from_sibling = grad_pair_swap(g_slots, "grad_pair_swap")
    pair = [pair_sum(gs, rv, core, "grad_pair_sum_" + n) for gs, rv, n in zip(g_slots, from_sibling, big_names)]
    from_chips = grad_chip_exchange(pair, "grad_chip_exchange")
    add2 = lambda a, b: a + b
    add4 = lambda a, b, c_, d: ((a + b) + c_) + d
    halves = []
    for fc, n in zip(from_chips, big_names):
        r2, cols = fc.shape[1], fc.shape[2]
        tr = _tile(r2, 256, 8)
        halves.append(rowcall(add4, "grad_chip_sum_" + n, r2, tr,
                              [(fc.reshape(N_CHIPS * r2, cols), cols, 0, q * r2 // tr) for q in range(N_CHIPS)], [],
                              [(cols, F32)], [])[0])
    others = grad_half_swap(halves, "grad_half_swap")
    big_grads = {n: jnp.where(ci == 0, jnp.concatenate([mine, other], axis=0), jnp.concatenate([other, mine], axis=0))
                 for n, mine, other in zip(big_names, halves, others)}

    small_pack = _pack_rows([g["dmod"], g["dmodc"], g["conv_w"]] + [g[n] for n in SMALL_PARAMS], SMALL_PACK_ROWS, F32)
    small_all = _fill_own(allgather_devices(small_pack, "gather_small_grads", copy_own=False), small_pack, dev)
    small_2d = small_all.reshape(N_DEV * SMALL_PACK_ROWS, PACK_COLS)

    def add8(*a):
        s = a[0]
        for t in a[1:]:
            s = s + t
        return s

    small_sum = rowcall(add8, "small_grad_sum", SMALL_PACK_ROWS, 256,
                        [(small_2d, PACK_COLS, 0, k * SMALL_PACK_ROWS // 256) for k in range(N_DEV)], [],
                        [(PACK_COLS, F32)], [])[0]
    dmod_all = small_all[:, 0:N_MOD].reshape(N_DEV, N_MOD * D)
    dmod_sum = small_sum[0:N_MOD].reshape(1, N_MOD * D)
    dmodc_sum = jnp.pad(small_sum[N_MOD:N_MOD + 2].reshape(1, 2 * D), ((0, 0), (0, (N_MOD - 2) * D)))
    conv_grad = _unpack_rows(small_sum, [(3, 3, 2 * FFN_HIDDEN)], row0=N_MOD + 2)[0]
    small_grads = dict(zip(SMALL_PARAMS, _unpack_rows(small_sum, [wts[n].shape for n in SMALL_PARAMS], row0=SMALL_ROW0)))

    dm16 = jnp.concatenate([dmod_all, dmodc_sum, jnp.zeros((MOD_ROWS - 9, N_MOD * D), F32)], axis=0)
    dm_shard = lax.dynamic_slice(dm16, (0, chip * ADA_COLS), (MOD_ROWS, ADA_COLS))
    g_w_ada = rowcall(f_ada_outer, "w_ada_grad", D, 256, [(jnp.transpose(c16), MOD_ROWS, 0, 0)], [dm_shard],
                      [(ADA_COLS, F32)], [])[0]
    g_b_ada = rowcall(add2, "b_ada_grad", 1, 1, [(dmod_sum, N_MOD * D, 0, 0), (dmodc_sum, N_MOD * D, 0, 0)], [],
                      [(N_MOD * D, F32)], [])[0]
    dmc_rows = jnp.pad(dm_shard[8:9], ((0, 7), (0, 0)))
    cctx_part = matmul(dmc_rows, w_ada[0], "nt", F32, "c_ctx_partial")
    cctx_all = allgather_devices(cctx_part, "gather_c_ctx")
    cctx_4 = jnp.stack([cctx_all[2 * q, 0] for q in range(N_CHIPS)])
    g_c_ctx = rowcall(f_cctx_grad, "c_ctx_grad", 1, 1, [(c_ctx[None], D, 0, 0)], [cctx_4], [(D, F32)], [])[0]

    grads = dict(small_grads)
    grads.update(big_grads)
    grads["w_ada"] = g_w_ada
    grads["b_ada"] = g_b_ada
    grads["c_ctx"] = g_c_ctx
    grads["conv_w"] = lax.dynamic_slice(conv_grad, (0, 0, chip * CONV_SHARD), (3, 3, CONV_SHARD))
    grads = {n: grads[n].reshape(wts[n].shape) for n in WEIGHT_NAMES}

    delta, new_m, new_v = {}, {}, {}
    for n in WEIGHT_NAMES:
        shape2d = (-1, wts[n].shape[-1])
        d_, m_, v_ = adamw(wts[n].reshape(shape2d), grads[n].reshape(shape2d), ms[n].reshape(shape2d),
                           vs[n].reshape(shape2d), "adamw_" + n)
        delta[n], new_m[n], new_v[n] = [t.reshape(wts[n].shape) for t in (d_, m_, v_)]

    return (loss, grad_x[None], *[grads[n] for n in WEIGHT_NAMES], *[delta[n] for n in WEIGHT_NAMES],
            *[new_m[n] for n in WEIGHT_NAMES], *[new_v[n] for n in WEIGHT_NAMES])
```

```python
import functools

import jax
import jax.numpy as jnp
from jax import lax
from jax.experimental import pallas as pl
from jax.experimental.pallas import tpu as pltpu

F32, BF16 = jnp.float32, jnp.bfloat16
MESH = pl.DeviceIdType.MESH

D_MODEL = 1024
S5_WIDTH = 512
S5_GROUP = 16
S5_GROUPS = 32
S5_STATE = 64
SGU_WIDTH = 512
SGU_GROUPS = 8
CHUNK = 128
FFN_HIDDEN = 2816
GRID_W = 64
N_MOD = 6
EPS = 1e-6
N_STATE = S5_GROUPS * S5_STATE
OCTETS = 4
SCAN_T = 128
N_CHIPS = 4
N_DEV = 8
LANES = 128
VMEM_LIMIT_BYTES = 56 * 1024 * 1024
CONV_PAD = 72
CONV_ROWS = 256

ADAM_LR, ADAM_B1, ADAM_B2, ADAM_EPS, ADAM_WD, ADAM_STEP = 0.001, 0.9, 0.999, 1e-08, 0.01, 10


def _call(body, name, out_shape, grid=None, in_specs=None, out_specs=None, scratch=(), sem=None, **kw):
    params = pltpu.CompilerParams(dimension_semantics=sem, vmem_limit_bytes=VMEM_LIMIT_BYTES)
    extra = {} if grid is None else {"grid": grid}
    return pl.pallas_call(body, name=name, out_shape=out_shape, in_specs=in_specs, out_specs=out_specs,
                          scratch_shapes=list(scratch), compiler_params=params, **extra, **kw)


def _tile(n, target, mult=LANES):
    best = None
    t = mult
    while t <= min(n, target):
        if n % t == 0:
            best = t
        t += mult
    return best or n


@jax.custom_vjp
def mmul(a, b):
    return jnp.dot(a.astype(BF16), b.astype(BF16), preferred_element_type=F32)


def _mmul_fwd(a, b):
    return mmul(a, b), (a, b)


def _mmul_bwd(res, ct):
    a, b = res
    ctb = ct.astype(BF16)
    da = lax.dot_general(ctb, b.astype(BF16), (((1,), (1,)), ((), ())), preferred_element_type=F32)
    db = lax.dot_general(a.astype(BF16), ctb, (((0,), (0,)), ((), ())), preferred_element_type=F32)
    return da.astype(a.dtype), db.astype(b.dtype)


mmul.defvjp(_mmul_fwd, _mmul_bwd)

_DOT_DIMS = {"nn": ((1,), (0,)), "nt": ((1,), (1,)), "tn": ((0,), (0,))}


MM_TILE = 1408
MM_FULL_K = 2048


def matmul(a, b, mode, out_dtype, name, init=None, shards=False):
    if mode == "nn":
        (M, K), N = a.shape, (b.shape[2] * N_CHIPS if shards else b.shape[1])
    elif mode == "nt":
        M, N, K = a.shape[0], b.shape[-2], a.shape[1]
    else:
        (K, M), N = a.shape, b.shape[1]
    ns = (K if mode == "nt" else N) // N_CHIPS
    tm = _tile(M, MM_TILE, 8 if M < LANES else LANES)
    tn = _tile(ns if shards and mode != "nt" else N, MM_TILE)
    if shards and mode == "nt":
        tk = _tile(ns, MM_TILE)
    else:
        tk = K if K <= MM_FULL_K else _tile(K, MM_TILE)
    nk = K // tk
    per = ns // (tk if mode == "nt" else tn)
    dims = (_DOT_DIMS[mode], ((), ()))
    has_init = init is not None
    use_acc = nk > 1 and out_dtype != F32

    def body(*refs):
        a_ref, b_ref = refs[:2]
        i_ref = refs[2] if has_init else None
        o_ref = refs[3] if has_init else refs[2]
        acc = refs[-1] if use_acc else o_ref
        k = pl.program_id(2)
        part = lax.dot_general(a_ref[...].astype(BF16), b_ref[...].astype(BF16), dims, preferred_element_type=F32)

        @pl.when(k == 0)
        def _():
            first = part + i_ref[...].astype(F32) if has_init else part
            acc[...] = first.astype(acc.dtype)

        if nk > 1:
            @pl.when(k > 0)
            def _():
                acc[...] += part

        if use_acc:
            @pl.when(k == nk - 1)
            def _():
                o_ref[...] = acc[...].astype(o_ref.dtype)

    if mode == "tn":
        a_spec = pl.BlockSpec((tk, tm), lambda i, j, k: (k, i))
    else:
        a_spec = pl.BlockSpec((tm, tk), lambda i, j, k: (i, k))
    if mode == "nt":
        b_spec = (pl.BlockSpec((None, tn, tk), lambda i, j, k: (k // per, j, k % per)) if shards
                  else pl.BlockSpec((tn, tk), lambda i, j, k: (j, k)))
    else:
        b_spec = (pl.BlockSpec((None, tk, tn), lambda i, j, k: (j // per, k, j % per)) if shards and mode == "nn"
                  else pl.BlockSpec((tk, tn), lambda i, j, k: (k, j)))
    if shards and mode == "tn":
        o_spec = pl.BlockSpec((None, tm, tn), lambda i, j, k: (j // per, i, j % per))
        out_shape = jax.ShapeDtypeStruct((N_CHIPS, M, ns), out_dtype)
    else:
        o_spec = pl.BlockSpec((tm, tn), lambda i, j, k: (i, j))
        out_shape = jax.ShapeDtypeStruct((M, N), out_dtype)
    in_specs = [a_spec, b_spec] + ([o_spec] if has_init else [])
    args = (a, b) + ((init,) if has_init else ())
    return _call(body, name, out_shape, grid=(M // tm, N // tn, nk),
                 in_specs=in_specs, out_specs=o_spec, scratch=[pltpu.VMEM((tm, tn), F32)] if use_acc else [],
                 sem=("parallel", "parallel", "arbitrary"))(*args)


def rowcall(fn, name, nrows, tm, rins, vins, routs, aouts, ainit=None):
    n_r, n_v, n_ro = len(rins), len(vins), len(routs)
    n_i = len(aouts) if ainit is not None else 0

    def body(*refs):
        r_in, v_in, i_in = refs[:n_r], refs[n_r:n_r + n_v], refs[n_r + n_v:n_r + n_v + n_i]
        r_out, a_out = refs[n_r + n_v + n_i:n_r + n_v + n_i + n_ro], refs[n_r + n_v + n_i + n_ro:]
        outs = fn(*[r[...].astype(F32) for r in r_in], *[v[...] for v in v_in])
        if not isinstance(outs, (tuple, list)):
            outs = (outs,)
        for ref, val in zip(r_out, outs[:n_ro]):
            ref[...] = val.astype(ref.dtype)
        if a_out:
            @pl.when(pl.program_id(0) == 0)
            def _():
                for k, ref in enumerate(a_out):
                    ref[...] = i_in[k][...] if n_i else jnp.zeros_like(ref)

            for ref, val in zip(a_out, outs[n_ro:]):
                ref[...] += val.astype(F32)

    def rspec(width, cblk, roff):
        return pl.BlockSpec((tm, width), lambda i: (i + roff, cblk))

    def whole(shape):
        nd = len(shape)
        return pl.BlockSpec(tuple(shape), lambda i: (0,) * nd)

    inits = list(ainit) if n_i else []
    in_specs = [rspec(w, cb, ro) for (_, w, cb, ro) in rins] + [whole(v.shape) for v in vins + inits]
    out_specs = [rspec(w, 0, 0) for (w, _) in routs] + [whole(s) for s in aouts]
    out_shape = [jax.ShapeDtypeStruct((nrows, w), dt) for (w, dt) in routs] + \
                [jax.ShapeDtypeStruct(tuple(s), F32) for s in aouts]
    res = _call(body, name, out_shape, grid=(nrows // tm,), in_specs=in_specs, out_specs=out_specs,
                sem=("arbitrary",))(*[r[0] for r in rins], *vins, *inits)
    return res


def _rms(x):
    return lax.rsqrt(jnp.mean(x * x, axis=-1, keepdims=True) + EPS)


def f_modulate(x, g, sc, sh):
    return (x * _rms(x)) * g * (1.0 + sc) + sh


def f_resid_mod(x, o, ga, g, sc, sh):
    x1 = x + ga * o
    return x1, f_modulate(x1, g, sc, sh)


def f_final_loss(x1, dn, tgt, ga2, gf):
    x2 = x1 + ga2 * dn
    y = (x2 * _rms(x2)) * gf
    err = jnp.square(y - tgt)
    return 0.5 * jnp.sum(jnp.mean(err, axis=-1))


def _sgu_spatial(vn, w, bt):
    lo = lax.broadcasted_iota(jnp.int32, (1, LANES), 1) < (SGU_WIDTH // SGU_GROUPS)
    row_blocks = []
    for r in range(vn.shape[0] // CHUNK):
        rows = vn[r * CHUNK:(r + 1) * CHUNK]
        cols = []
        for j in range(SGU_WIDTH // LANES):
            blk = rows[:, j * LANES:(j + 1) * LANES]
            v_lo = jnp.where(lo, blk, 0.0)
            v_hi = jnp.where(lo, 0.0, blk)
            s = mmul(w[2 * j], v_lo) + mmul(w[2 * j + 1], v_hi)
            bias = jnp.where(lo, bt[:, 2 * j:2 * j + 1], bt[:, 2 * j + 1:2 * j + 2])
            cols.append(s + bias)
        row_blocks.append(jnp.concatenate(cols, axis=1))
    return jnp.concatenate(row_blocks, axis=0) if len(row_blocks) > 1 else row_blocks[0]


def f_mixer(u_a, y0, y1, zu, zv, ga0, ga1, gb0, gb1, d_skip, w_glu, b_glu, ln_g, ln_b, sgu_w, sgu_bt,
            w_pa, w_pb, b_gate):
    ys = u_a * d_skip + y0 + y1
    ge = jax.nn.gelu(ys)
    y_a = ge * jax.nn.sigmoid(mmul(ge, w_glu) + b_glu)
    u_sg = jax.nn.gelu(zu)
    v = jax.nn.gelu(zv)
    vc = v - jnp.mean(v, axis=-1, keepdims=True)
    vn = (vc * lax.rsqrt(jnp.mean(vc * vc, axis=-1, keepdims=True) + EPS)) * ln_g + ln_b
    y_b = u_sg * _sgu_spatial(vn, sgu_w, sgu_bt)
    gl_a = jnp.concatenate([ga0, ga1], axis=1) + b_gate[:, :D_MODEL]
    gl_b = jnp.concatenate([gb0, gb1], axis=1) + b_gate[:, D_MODEL:]
    return jax.nn.sigmoid(gl_a) * mmul(y_a, w_pa) + jax.nn.sigmoid(gl_b) * mmul(y_b, w_pb)


def _cmul(ar, ai, xr, xi):
    return ar * xr - ai * xi, ar * xi + ai * xr


SUB = 8
STRAND = SCAN_T // SUB


def to_strand_order(v):
    n = v.shape[0] // SCAN_T
    return jnp.transpose(v.reshape(n, SUB, STRAND, v.shape[1]), (0, 2, 1, 3)).reshape(v.shape)


def _to_token_order(v):
    i = lax.broadcasted_iota(jnp.int32, (SCAN_T, SCAN_T), 0)
    j = lax.broadcasted_iota(jnp.int32, (SCAN_T, SCAN_T), 1)
    perm = jnp.where(i == STRAND * (j % SUB) + j // SUB, 1.0, 0.0).astype(BF16)
    hi = v.astype(BF16)
    lo = (v - hi.astype(F32)).astype(BF16)
    return jnp.dot(perm, hi, preferred_element_type=F32) + jnp.dot(perm, lo, preferred_element_type=F32)


def _scan_strands(xr, xi, pw_ref, q_ref, col, rev, conj, cr, ci):
    def tab(ref, lo):
        t_r = ref[lo:lo + SUB, col:col + LANES]
        t_i = ref[lo:lo + SUB, col + 512:col + 512 + LANES]
        return t_r, (-t_i if conj else t_i)

    a_r, a_i = tab(pw_ref, (STRAND - 1) * SUB if rev else 0)
    order = list(range(STRAND - 1, -1, -1) if rev else range(STRAND))
    lr, li = [None] * STRAND, [None] * STRAND
    for n, k in enumerate(order):
        lr[k], li[k] = xr[k * SUB:(k + 1) * SUB], xi[k * SUB:(k + 1) * SUB]
        if n:
            m_r, m_i = _cmul(a_r, a_i, lr[order[n - 1]], li[order[n - 1]])
            lr[k], li[k] = lr[k] + m_r, li[k] + m_i
    f_r, f_i = lr[order[-1]], li[order[-1]]
    q_r, q_i = tab(q_ref, 0)
    sub = lax.broadcasted_iota(jnp.int32, (SUB, 1), 0)
    s = 1
    while s < SUB:
        row = (SUB - s) if rev else (s - 1)
        shift = (SUB - s) if rev else s
        m = (sub < SUB - s) if rev else (sub >= s)
        p_r, p_i = _cmul(q_r[row:row + 1], q_i[row:row + 1], pltpu.roll(f_r, shift, 0), pltpu.roll(f_i, shift, 0))
        f_r, f_i = f_r + jnp.where(m, p_r, 0.0), f_i + jnp.where(m, p_i, 0.0)
        s *= 2
    c_r, c_i = jnp.broadcast_to(cr, (SUB, LANES)), jnp.broadcast_to(ci, (SUB, LANES))
    p_r, p_i = _cmul(q_r, q_i, c_r, c_i)
    s_r, s_i = f_r + p_r, f_i + p_i
    edge = 0 if rev else SUB - 1
    first = sub == (SUB - 1 if rev else 0)
    e_r = jnp.where(first, c_r, pltpu.roll(s_r, SUB - 1 if rev else 1, 0))
    e_i = jnp.where(first, c_i, pltpu.roll(s_i, SUB - 1 if rev else 1, 0))
    for k in range(STRAND):
        t_r, t_i = tab(pw_ref, k * SUB)
        p_r, p_i = _cmul(t_r, t_i, e_r, e_i)
        lr[k], li[k] = lr[k] + p_r, li[k] + p_i
    return lr, li, (s_r[edge:edge + 1], s_i[edge:edge + 1]), (e_r, e_i)


SCAN_GROUP = 6


def _lane_cols(o, j):
    col = o * 1024 + j * LANES
    return col, slice(col, col + LANES), slice(col + 512, col + 512 + LANES)


def s5_forward(u, bm, cm, pw, q, rev, name):
    T, G = SCAN_T, SCAN_GROUP
    n_chunks = u.shape[0] // T
    n_steps = n_chunks // G
    order = list(range(G - 1, -1, -1) if rev else range(G))

    def pos(i):
        return (n_steps - 1 - i) if rev else i

    def body(u_ref, bm_ref, cm_ref, pw_ref, q_ref, y_ref, cin_ref, carry):
        @pl.when(pl.program_id(0) == 0)
        def _():
            carry[...] = jnp.zeros_like(carry)

        uv = u_ref[...]
        for o in range(OCTETS):
            bu = jnp.dot(uv[:, o * LANES:(o + 1) * LANES], bm_ref[o], preferred_element_type=F32)
            hr = [[None] * 4 for _ in range(G)]
            hi = [[None] * 4 for _ in range(G)]
            for j in range(4):
                col, sl_r, sl_i = _lane_cols(o, j)
                cr, ci = carry[0:1, sl_r], carry[0:1, sl_i]
                for g in order:
                    rows = slice(g * T, (g + 1) * T)
                    cin_ref[g, 0:1, sl_r] = cr
                    cin_ref[g, 0:1, sl_i] = ci
                    xr, xi, (cr, ci), _ = _scan_strands(
                        bu[rows, j * LANES:(j + 1) * LANES], bu[rows, 512 + j * LANES:512 + (j + 1) * LANES],
                        pw_ref, q_ref, col, rev, False, cr, ci)
                    hr[g][j] = jnp.concatenate(xr, axis=0)
                    hi[g][j] = jnp.concatenate(xi, axis=0)
                carry[0:1, sl_r] = cr
                carry[0:1, sl_i] = ci
            h = jnp.concatenate([jnp.concatenate(hr[g] + hi[g], axis=1) for g in range(G)], axis=0).astype(BF16)
            y = jnp.dot(h, cm_ref[o], preferred_element_type=F32)
            for g in range(G):
                y_ref[g * T:(g + 1) * T, o * LANES:(o + 1) * LANES] = _to_token_order(y[g * T:(g + 1) * T])

    whole3 = lambda s: pl.BlockSpec(s, lambda i: (0, 0, 0))
    whole2 = lambda s: pl.BlockSpec(s, lambda i: (0, 0))
    rows_spec = pl.BlockSpec((G * T, S5_WIDTH), lambda i: (pos(i), 0))
    return _call(
        body, name,
        [jax.ShapeDtypeStruct((n_chunks * T, S5_WIDTH), F32), jax.ShapeDtypeStruct((n_chunks, 1, 2 * N_STATE), F32)],
        grid=(n_steps,),
        in_specs=[rows_spec, whole3(bm.shape), whole3(cm.shape), whole2(pw.shape), whole2(q.shape)],
        out_specs=[rows_spec, pl.BlockSpec((G, 1, 2 * N_STATE), lambda i: (pos(i), 0, 0))],
        scratch=[pltpu.VMEM((1, 2 * N_STATE), F32)], sem=("arbitrary",))(u, bm, cm, pw, q)


def s5_backward(u, dy, cin, bm, cm, pw_h, q_h, pw_l, q_l, rev, name):
    T, G = SCAN_T, SCAN_GROUP
    n_chunks = u.shape[0] // T
    n_steps = n_chunks // G
    adjoint_order = list(range(G) if rev else range(G - 1, -1, -1))

    def pos(i):
        return i if rev else (n_steps - 1 - i)

    def body(u_ref, dy_ref, cin_ref, bm_ref, cm_ref, pwh_ref, qh_ref, pwl_ref, ql_ref, du_ref, dbm_ref, dcm_ref,
             da_ref, lcarry):
        @pl.when(pl.program_id(0) == 0)
        def _():
            lcarry[...] = jnp.zeros_like(lcarry)
            dbm_ref[...] = jnp.zeros_like(dbm_ref)
            dcm_ref[...] = jnp.zeros_like(dcm_ref)
            da_ref[...] = jnp.zeros_like(da_ref)

        uv = u_ref[...]
        dyv = dy_ref[...]
        for o in range(OCTETS):
            u_o = uv[:, o * LANES:(o + 1) * LANES]
            dy_o = dyv[:, o * LANES:(o + 1) * LANES]
            bu = jnp.dot(u_o, bm_ref[o], preferred_element_type=F32)
            gy = lax.dot_general(dy_o, cm_ref[o], (((1,), (1,)), ((), ())), preferred_element_type=F32)
            hs = [[None] * 8 for _ in range(G)]
            ls = [[None] * 8 for _ in range(G)]
            for j in range(4):
                col, sl_r, sl_i = _lane_cols(o, j)
                b_r = slice(j * LANES, (j + 1) * LANES)
                b_i = slice(512 + j * LANES, 512 + (j + 1) * LANES)
                l_r, l_i = lcarry[0:1, sl_r], lcarry[0:1, sl_i]
                acc_r = acc_i = None
                for g in adjoint_order:
                    rows = slice(g * T, (g + 1) * T)
                    xr, xi, _, (e_r, e_i) = _scan_strands(bu[rows, b_r], bu[rows, b_i], pwh_ref, qh_ref, col, rev, False,
                                                          cin_ref[g, 0:1, sl_r], cin_ref[g, 0:1, sl_i])
                    ar_, ai_, (l_r, l_i), _ = _scan_strands(gy[rows, b_r], gy[rows, b_i], pwl_ref, ql_ref, col, not rev,
                                                            True, l_r, l_i)
                    for k in range(STRAND):
                        kp = k + 1 if rev else k - 1
                        p_r, p_i = (e_r, e_i) if not 0 <= kp < STRAND else (xr[kp], xi[kp])
                        t_r = ar_[k] * p_r + ai_[k] * p_i
                        t_i = ai_[k] * p_r - ar_[k] * p_i
                        acc_r, acc_i = (t_r, t_i) if acc_r is None else (acc_r + t_r, acc_i + t_i)
                    hs[g][j], hs[g][4 + j] = jnp.concatenate(xr, axis=0), jnp.concatenate(xi, axis=0)
                    ls[g][j], ls[g][4 + j] = jnp.concatenate(ar_, axis=0), jnp.concatenate(ai_, axis=0)
                lcarry[0:1, sl_r] = l_r
                lcarry[0:1, sl_i] = l_i
                da_ref[0:1, sl_r] += jnp.sum(acc_r, axis=0, keepdims=True)
                da_ref[0:1, sl_i] += jnp.sum(acc_i, axis=0, keepdims=True)
            h = jnp.concatenate([jnp.concatenate(hs[g], axis=1) for g in range(G)], axis=0).astype(BF16)
            lam = jnp.concatenate([jnp.concatenate(ls[g], axis=1) for g in range(G)], axis=0).astype(BF16)
            du = lax.dot_general(lam, bm_ref[o], (((1,), (1,)), ((), ())), preferred_element_type=F32)
            for g in range(G):
                du_ref[g * T:(g + 1) * T, o * LANES:(o + 1) * LANES] = _to_token_order(du[g * T:(g + 1) * T])
            dbm_ref[o] += lax.dot_general(u_o, lam, (((0,), (0,)), ((), ())), preferred_element_type=F32)
            dcm_ref[o] += lax.dot_general(h, dy_o, (((0,), (0,)), ((), ())), preferred_element_type=F32)

    whole3 = lambda s: pl.BlockSpec(s, lambda i: (0, 0, 0))
    whole2 = lambda s: pl.BlockSpec(s, lambda i: (0, 0))
    rows_spec = pl.BlockSpec((G * T, S5_WIDTH), lambda i: (pos(i), 0))
    return _call(
        body, name,
        [jax.ShapeDtypeStruct((n_chunks * T, S5_WIDTH), F32), jax.ShapeDtypeStruct(bm.shape, F32),
         jax.ShapeDtypeStruct(cm.shape, F32), jax.ShapeDtypeStruct((1, 2 * N_STATE), F32)],
        grid=(n_steps,),
        in_specs=[rows_spec, rows_spec, pl.BlockSpec((G, 1, 2 * N_STATE), lambda i: (pos(i), 0, 0)),
                  whole3(bm.shape), whole3(cm.shape), whole2(pw_h.shape), whole2(q_h.shape), whole2(pw_l.shape),
                  whole2(q_l.shape)],
        out_specs=[rows_spec, whole3(bm.shape), whole3(cm.shape), whole2((1, 2 * N_STATE))],
        scratch=[pltpu.VMEM((1, 2 * N_STATE), F32)], sem=("arbitrary",))(u, dy, cin, bm, cm, pw_h, q_h, pw_l, q_l)


def s5_tables(ar, ai, ls):
    def body(ar_ref, ai_ref, ls_ref, *outs):
        dt = jnp.exp(ls_ref[...])
        k = 0
        for n_rows, step in ((STRAND, 1.0), (SUB, float(STRAND))):
            row = lax.broadcasted_iota(jnp.int32, (n_rows, 1), 0)
            for m_int in (row + 1, n_rows - row):
                m = m_int.astype(F32) * step
                mag = jnp.exp(m * (ar_ref[...] * dt))
                ang = m * (ai_ref[...] * dt)
                outs[k][...] = mag * jnp.cos(ang)
                outs[k + 1][...] = mag * jnp.sin(ang)
                k += 2

    vec = pl.BlockSpec((None, 1, N_STATE), lambda d: (d, 0, 0))
    tab = lambda n: pl.BlockSpec((None, n, N_STATE), lambda d: (d, 0, 0))
    shp = lambda n: jax.ShapeDtypeStruct((2, n, N_STATE), F32)
    sizes = [STRAND] * 4 + [SUB] * 4
    return _call(body, "s5_tables", [shp(n) for n in sizes], grid=(2,), in_specs=[vec, vec, vec],
                 out_specs=[tab(n) for n in sizes], sem=("arbitrary",))(ar, ai, ls)


def f_discretize(a_re, a_im, ls, b_re, b_im):
    dt = jnp.exp(ls)
    mag = jnp.exp(a_re * dt)
    ab_re = mag * jnp.cos(a_im * dt)
    ab_im = mag * jnp.sin(a_im * dt)
    p = ab_re - 1.0
    q = ab_im
    den = a_re * a_re + a_im * a_im
    k_re = ((p * a_re + q * a_im) / den)[None]
    k_im = ((q * a_re - p * a_im) / den)[None]
    return ab_re, ab_im, k_re * b_re - k_im * b_im, k_re * b_im + k_im * b_re


def _disc_specs():
    a = pl.BlockSpec((None, S5_GROUPS, S5_STATE), lambda d: (d, 0, 0))
    s = pl.BlockSpec((None, S5_GROUPS, 1), lambda d: (d, 0, 0))
    b = pl.BlockSpec((None, S5_GROUP, S5_GROUPS, S5_STATE), lambda d: (d, 0, 0, 0))
    return a, s, b


def s5_discretize(a_re, a_im, ls, b_re, b_im):
    def body(ar, ai, l, br, bi, obr, obi):
        _, _, r, i = f_discretize(ar[...], ai[...], l[...], br[...], bi[...])
        obr[...] = r
        obi[...] = i

    a, s, b = _disc_specs()
    return _call(body, "s5_discretize", [jax.ShapeDtypeStruct(b_re.shape, F32)] * 2, grid=(2,),
                 in_specs=[a, a, s, b, b], out_specs=[b, b], sem=("arbitrary",))(a_re, a_im, ls, b_re, b_im)


def s5_discretize_bwd(a_re, a_im, ls, b_re, b_im, dab_re, dab_im, dbb_re, dbb_im):
    def body(ar, ai, l, br, bi, c0, c1, c2, c3, o0, o1, o2, o3, o4):
        _, vjp = jax.vjp(f_discretize, ar[...], ai[...], l[...], br[...], bi[...])
        outs = vjp((c0[...], c1[...], c2[...], c3[...]))
        for ref, val in zip((o0, o1, o2, o3, o4), outs):
            ref[...] = val

    a, s, b = _disc_specs()
    shapes = [jax.ShapeDtypeStruct(t.shape, F32) for t in (a_re, a_im, ls, b_re, b_im)]
    return _call(body, "s5_discretize_bwd", shapes, grid=(2,), in_specs=[a, a, s, b, b, a, a, b, b],
                 out_specs=[a, a, s, b, b], sem=("arbitrary",))(a_re, a_im, ls, b_re, b_im, dab_re, dab_im,
                                                                 dbb_re, dbb_im)


def _conv_taps(s_ref, base, n_rows):
    n = n_rows + 2 * CONV_PAD
    ext = s_ref[pl.ds(base, n), :]
    col = (lax.broadcasted_iota(jnp.int32, (n, 1), 0) + (2 * GRID_W - CONV_PAD)) % GRID_W
    left = jnp.where(col == 0, 0.0, pltpu.roll(ext, 1, 0))
    right = jnp.where(col == GRID_W - 1, 0.0, pltpu.roll(ext, n - 1, 0))
    return left, ext, right


def _conv_apply(taps, w_ref, n_rows, flip):
    out = None
    for i in range(3):
        wi = 2 - i if flip else i
        start = CONV_PAD + (i - 1) * GRID_W
        for j in range(3):
            wj = 2 - j if flip else j
            term = w_ref[wi * 3 + wj:wi * 3 + wj + 1, :] * taps[j][start:start + n_rows]
            out = term if out is None else out + term
    return out


def _conv_fill(dst_ref, src_ref, n_tok):
    zeros = jnp.zeros((CONV_PAD, LANES), F32)
    dst_ref[0:CONV_PAD, :] = zeros
    dst_ref[CONV_PAD + n_tok:2 * CONV_PAD + n_tok, :] = zeros

    def step(r, carry):
        base = pl.multiple_of(r * CONV_ROWS, CONV_ROWS)
        dst_ref[pl.ds(base + CONV_PAD, CONV_ROWS), :] = src_ref[pl.ds(base, CONV_ROWS), :].astype(F32)
        return carry

    lax.fori_loop(0, n_tok // CONV_ROWS, step, 0)


def conv_forward(up, wg, wv, bias):
    n_tok = up.shape[0]
    nb = FFN_HIDDEN // LANES

    def body(ug_ref, uv_ref, wg_ref, wv_ref, bg_ref, bv_ref, act_ref, gate_ref, val_ref, sg, sv):
        _conv_fill(sg, ug_ref, n_tok)
        _conv_fill(sv, uv_ref, n_tok)

        def step(r, carry):
            base = pl.multiple_of(r * CONV_ROWS, CONV_ROWS)
            gate = _conv_apply(_conv_taps(sg, base, CONV_ROWS), wg_ref, CONV_ROWS, False) + bg_ref[...]
            val = _conv_apply(_conv_taps(sv, base, CONV_ROWS), wv_ref, CONV_ROWS, False) + bv_ref[...]
            act_ref[pl.ds(base, CONV_ROWS), :] = (gate * jax.nn.sigmoid(gate) * val).astype(BF16)
            gate_ref[pl.ds(base, CONV_ROWS), :] = gate.astype(BF16)
            val_ref[pl.ds(base, CONV_ROWS), :] = val.astype(BF16)
            return carry

        lax.fori_loop(0, n_tok // CONV_ROWS, step, 0)

    col = lambda off: pl.BlockSpec((n_tok, LANES), lambda k: (0, k + off))
    wsp = lambda off: pl.BlockSpec((16, LANES), lambda k: (0, k + off))
    bsp = lambda off: pl.BlockSpec((1, LANES), lambda k: (0, k + off))
    pad = pltpu.VMEM((n_tok + 2 * CONV_PAD, LANES), F32)
    half = jax.ShapeDtypeStruct((n_tok, FFN_HIDDEN), BF16)
    return _call(body, "conv_forward", [half, half, half], grid=(nb,),
                 in_specs=[col(0), col(nb), wsp(0), wsp(0), bsp(0), bsp(nb)], out_specs=[col(0), col(0), col(0)],
                 scratch=[pad, pad], sem=("arbitrary",))(up, up, wg, wv, bias, bias)


def conv_backward(up, gate, val, dact, wg, wv):
    n_tok = up.shape[0]
    nb = FFN_HIDDEN // LANES
    n_steps = n_tok // CONV_ROWS

    def body(ug_ref, uv_ref, gate_ref, val_ref, da_ref, wg_ref, wv_ref, dug_ref, duv_ref, dwg_ref, dwv_ref,
             sg, sv, sdg, sdv):
        _conv_fill(sg, ug_ref, n_tok)
        _conv_fill(sv, uv_ref, n_tok)
        zeros = jnp.zeros((CONV_PAD, LANES), F32)
        for s_ref in (sdg, sdv):
            s_ref[0:CONV_PAD, :] = zeros
            s_ref[CONV_PAD + n_tok:2 * CONV_PAD + n_tok, :] = zeros
        dwg_ref[...] = jnp.zeros_like(dwg_ref)
        dwv_ref[...] = jnp.zeros_like(dwv_ref)

        def grads(r, carry):
            base = pl.multiple_of(r * CONV_ROWS, CONV_ROWS)
            taps_g = _conv_taps(sg, base, CONV_ROWS)
            taps_v = _conv_taps(sv, base, CONV_ROWS)
            gate = gate_ref[pl.ds(base, CONV_ROWS), :].astype(F32)
            val = val_ref[pl.ds(base, CONV_ROWS), :].astype(F32)
            d_act = da_ref[pl.ds(base, CONV_ROWS), :].astype(F32)
            sig = jax.nn.sigmoid(gate)
            d_gate = d_act * val * (sig * (1.0 + gate * (1.0 - sig)))
            d_val = d_act * (gate * sig)
            sdg[pl.ds(base + CONV_PAD, CONV_ROWS), :] = d_gate
            sdv[pl.ds(base + CONV_PAD, CONV_ROWS), :] = d_val
            for d_out, taps, dw_ref in ((d_gate, taps_g, dwg_ref), (d_val, taps_v, dwv_ref)):
                for i in range(3):
                    start = CONV_PAD + (i - 1) * GRID_W
                    for j in range(3):
                        k = i * 3 + j
                        dw_ref[k:k + 1, :] += jnp.sum(d_out * taps[j][start:start + CONV_ROWS], axis=0, keepdims=True)
                dw_ref[9:10, :] += jnp.sum(d_out, axis=0, keepdims=True)
            return carry

        lax.fori_loop(0, n_steps, grads, 0)

        def spread(r, carry):
            base = pl.multiple_of(r * CONV_ROWS, CONV_ROWS)
            dug_ref[pl.ds(base, CONV_ROWS), :] = _conv_apply(
                _conv_taps(sdg, base, CONV_ROWS), wg_ref, CONV_ROWS, True).astype(BF16)
            duv_ref[pl.ds(base, CONV_ROWS), :] = _conv_apply(
                _conv_taps(sdv, base, CONV_ROWS), wv_ref, CONV_ROWS, True).astype(BF16)
            return carry

        lax.fori_loop(0, n_steps, spread, 0)

    col = lambda off: pl.BlockSpec((n_tok, LANES), lambda k: (0, k + off))
    wsp = pl.BlockSpec((16, LANES), lambda k: (0, k))
    pad = pltpu.VMEM((n_tok + 2 * CONV_PAD, LANES), F32)
    half = jax.ShapeDtypeStruct((n_tok, FFN_HIDDEN), BF16)
    dw = jax.ShapeDtypeStruct((16, FFN_HIDDEN), F32)
    return _call(body, "conv_backward", [half, half, dw, dw], grid=(nb,),
                 in_specs=[col(0), col(nb), col(0), col(0), col(0), wsp, wsp],
                 out_specs=[col(0), col(0), wsp, wsp], scratch=[pad, pad, pad, pad],
                 sem=("arbitrary",))(up, up, gate, val, dact, wg, wv)


def f_adamw(w, g, m, v):
    m = ADAM_B1 * m + (1.0 - ADAM_B1) * g
    v = ADAM_B2 * v + (1.0 - ADAM_B2) * jnp.square(g)
    m_hat = m / (1.0 - ADAM_B1 ** ADAM_STEP)
    v_hat = v / (1.0 - ADAM_B2 ** ADAM_STEP)
    delta = -ADAM_LR * (m_hat / (jnp.sqrt(v_hat) + ADAM_EPS) + ADAM_WD * w)
    return delta, m, v


def adamw(w, g, m, v, name):
    shape = w.shape
    cols = shape[-1]
    rows = w.size // cols
    two_d = [t.reshape(rows, cols) for t in (w, g, m, v)]
    tm = _tile(rows, 256, 8) if rows % 8 == 0 else rows
    outs = rowcall(f_adamw, name, rows, tm, [(t, cols, 0, 0) for t in two_d], [], [(cols, F32)] * 3, [])
    return tuple(o.reshape(shape) for o in outs)


def _place():
    return lax.axis_index("x"), lax.axis_index("y"), lax.axis_index("c")


_ANY = pl.BlockSpec(memory_space=pl.ANY)


def _fill_own(gathered, own, index):
    return lax.dynamic_update_slice(gathered, own[None], (index,) + (0,) * own.ndim)


def allgather_devices(v, name, copy_own=True):
    def body(v_ref, out_ref, send_sems, recv_sems, local_sem):
        x, y, c = _place()
        me, sibling = (x, y, c), (x, y, 1 - c)
        chips = [(1 - x, y), (x, 1 - y), (1 - x, 1 - y)]

        def slot(p):
            return out_ref.at[4 * p[0] + 2 * p[1] + p[2]]

        def copy(k, block, to, src=None):
            return pltpu.make_async_remote_copy(
                src_ref=slot(block) if src is None else src, dst_ref=slot(block),
                send_sem=send_sems.at[k], recv_sem=recv_sems.at[k], device_id=to, device_id_type=MESH)

        mine = pltpu.make_async_copy(v_ref, slot(me), local_sem)
        if copy_own:
            mine.start()
        first = [copy(0, me, sibling, src=v_ref)]
        first += [copy(1 + j, me, (*chip, c), src=v_ref) for j, chip in enumerate(chips)]
        for cp in first:
            cp.start()
        passed = [copy(4 + j, (*chip, c), sibling) for j, chip in enumerate(chips)]
        for j, chip in enumerate(chips):
            copy(1 + j, (*chip, c), me).wait_recv()
            passed[j].start()
        copy(0, sibling, me).wait_recv()
        for j, chip in enumerate(chips):
            copy(4 + j, (*chip, 1 - c), me).wait_recv()
        for cp in first + passed:
            cp.wait_send()
        if copy_own:
            mine.wait()

    return _call(body, name, jax.ShapeDtypeStruct((N_DEV,) + v.shape, v.dtype), in_specs=[_ANY], out_specs=_ANY,
                 scratch=[pltpu.SemaphoreType.DMA((7,)), pltpu.SemaphoreType.DMA((7,)), pltpu.SemaphoreType.DMA])(v)


def _other_chips(x, y):
    return [(1 - x, y), (x, 1 - y), (1 - x, 1 - y)]


def allgather_chips(vs, name):
    n = len(vs)
    shapes = [v.shape for v in vs]
    vs = [v.reshape((2, v.shape[0] // 2) + v.shape[1:]) for v in vs]

    def body(*refs):
        v_refs, o_refs = refs[:n], refs[n:2 * n]
        send_sems, recv_sems = refs[2 * n:]
        x, y, c = _place()
        sibling = (x, y, 1 - c)
        chips = _other_chips(x, y)

        def rows(a, chip, h):
            return o_refs[a].at[2 * chip[0] + chip[1], h]

        def copy(a, k, chip, h, to, src=None):
            return pltpu.make_async_remote_copy(
                src_ref=rows(a, chip, h) if src is None else src, dst_ref=rows(a, chip, h),
                send_sem=send_sems.at[6 * a + k], recv_sem=recv_sems.at[6 * a + k], device_id=to, device_id_type=MESH)

        first = [copy(a, j, (x, y), c, (*chip, c), src=v_refs[a].at[c])
                 for a in range(n) for j, chip in enumerate(chips)]
        for cp in first:
            cp.start()
        passed = []
        for j, chip in enumerate(chips):
            for a in range(n):
                copy(a, j, chip, c, (x, y, c)).wait_recv()
                passed.append(copy(a, 3 + j, chip, c, sibling))
                passed[-1].start()
        for j, chip in enumerate(chips):
            for a in range(n):
                copy(a, 3 + j, chip, 1 - c, (x, y, c)).wait_recv()
        for cp in first + passed:
            cp.wait_send()

    outs = _call(body, name, [jax.ShapeDtypeStruct((N_CHIPS,) + v.shape, v.dtype) for v in vs], in_specs=[_ANY] * n,
                 out_specs=[_ANY] * n,
                 scratch=[pltpu.SemaphoreType.DMA((6 * n,)), pltpu.SemaphoreType.DMA((6 * n,))])(*vs)
    return [o.reshape((N_CHIPS,) + s) for o, s in zip(outs, shapes)]


def grad_pair_swap(gs, name):
    n = len(gs)

    def body(*refs):
        g_refs, o_refs, send_sems, recv_sems = refs[:n], refs[n:2 * n], refs[2 * n], refs[2 * n + 1]
        x, y, c = _place()
        cps = [pltpu.make_async_remote_copy(
            src_ref=g_refs[a].at[:, 1 - c], dst_ref=o_refs[a], send_sem=send_sems.at[a], recv_sem=recv_sems.at[a],
            device_id=(x, y, 1 - c), device_id_type=MESH) for a in range(n)]
        for cp in cps:
            cp.start()
        for cp in cps:
            cp.wait()

    return _call(body, name, [jax.ShapeDtypeStruct((N_CHIPS,) + g.shape[2:], g.dtype) for g in gs],
                 in_specs=[_ANY] * n, out_specs=[_ANY] * n,
                 scratch=[pltpu.SemaphoreType.DMA((n,)), pltpu.SemaphoreType.DMA((n,))])(*gs)


def pair_sum(g, recv, core, name):
    r2, cols = recv.shape[1], recv.shape[2]
    tr = _tile(r2, 256, 8)
    nt = r2 // tr

    def body(c_ref, g_ref, r_ref, o_ref):
        o_ref[...] = (g_ref[...] + r_ref[...]).astype(o_ref.dtype)

    spec = pl.BlockSpec((None, tr, cols), lambda q, i, c_ref: (q, i, 0))
    grid_spec = pltpu.PrefetchScalarGridSpec(
        num_scalar_prefetch=1, grid=(N_CHIPS, nt),
        in_specs=[pl.BlockSpec((None, None, tr, cols), lambda q, i, c_ref: (q, c_ref[0], i, 0)), spec], out_specs=spec)
    return pl.pallas_call(body, name=name, out_shape=jax.ShapeDtypeStruct(recv.shape, BF16), grid_spec=grid_spec,
                          compiler_params=pltpu.CompilerParams(dimension_semantics=("arbitrary", "arbitrary"),
                                                               vmem_limit_bytes=VMEM_LIMIT_BYTES))(core, g, recv)


def grad_chip_exchange(ps, name):
    n = len(ps)

    def body(*refs):
        p_refs, o_refs = refs[:n], refs[n:2 * n]
        send_sems, recv_sems, local_sems = refs[2 * n:]
        x, y, c = _place()
        chips = _other_chips(x, y)
        me = 2 * x + y
        mine = [pltpu.make_async_copy(p_refs[a].at[me], o_refs[a].at[me], local_sems.at[a]) for a in range(n)]
        sends = [pltpu.make_async_remote_copy(
            src_ref=p_refs[a].at[2 * chip[0] + chip[1]], dst_ref=o_refs[a].at[me], send_sem=send_sems.at[3 * a + j],
            recv_sem=recv_sems.at[3 * a + j], device_id=(*chip, c), device_id_type=MESH)
            for a in range(n) for j, chip in enumerate(chips)]
        for cp in mine + sends:
            cp.start()
        for a in range(n):
            for j, chip in enumerate(chips):
                pltpu.make_async_remote_copy(
                    src_ref=p_refs[a].at[me], dst_ref=o_refs[a].at[2 * chip[0] + chip[1]],
                    send_sem=send_sems.at[3 * a + j], recv_sem=recv_sems.at[3 * a + j], device_id=(*chip, c),
                    device_id_type=MESH).wait_recv()
        for cp in sends:
            cp.wait_send()
        for cp in mine:
            cp.wait()

    return _call(body, name, [jax.ShapeDtypeStruct(p.shape, p.dtype) for p in ps], in_specs=[_ANY] * n,
                 out_specs=[_ANY] * n, scratch=[pltpu.SemaphoreType.DMA((3 * n,)), pltpu.SemaphoreType.DMA((3 * n,)),
                                                pltpu.SemaphoreType.DMA((n,))])(*ps)


def grad_half_swap(ss, name):
    n = len(ss)

    def body(*refs):
        s_refs, o_refs, send_sems, recv_sems = refs[:n], refs[n:2 * n], refs[2 * n], refs[2 * n + 1]
        x, y, c = _place()
        cps = [pltpu.make_async_remote_copy(
            src_ref=s_refs[a], dst_ref=o_refs[a], send_sem=send_sems.at[a], recv_sem=recv_sems.at[a],
            device_id=(x, y, 1 - c), device_id_type=MESH) for a in range(n)]
        for cp in cps:
            cp.start()
        for cp in cps:
            cp.wait()

    return _call(body, name, [jax.ShapeDtypeStruct(s.shape, s.dtype) for s in ss], in_specs=[_ANY] * n,
                 out_specs=[_ANY] * n, scratch=[pltpu.SemaphoreType.DMA((n,)), pltpu.SemaphoreType.DMA((n,))])(*ss)


ROW_TILE = 256

BIG_SHARDS = (("w_in", (D_MODEL, 896), 1), ("s5_w_glu", (128, S5_WIDTH), 0), ("w_proj_a", (S5_WIDTH, 256), 1),
              ("w_proj_b", (SGU_WIDTH, 256), 1), ("w_out", (256, D_MODEL), 0), ("w_up", (D_MODEL, 1408), 1),
              ("w_down", (704, D_MODEL), 0))
SMALL_PARAMS = ("g_mix", "s5_a_re", "s5_a_im", "s5_log_step", "s5_b_re", "s5_b_im", "s5_c_re", "s5_c_im", "s5_d",
                "s5_b_glu", "sgu_ln_g", "sgu_ln_b", "sgu_w", "sgu_b", "b_gate", "g_ffn", "conv_b", "g_final")
PACK_COLS = 1024


def _rows_of(n):
    return -(-n // PACK_COLS)


def _pack_rows(arrays, total_rows, dtype):
    parts = []
    used = 0
    for a in arrays:
        r = _rows_of(a.size)
        parts.append(jnp.pad(a.reshape(-1).astype(dtype), (0, r * PACK_COLS - a.size)).reshape(r, PACK_COLS))
        used += r
    if total_rows > used:
        parts.append(jnp.zeros((total_rows - used, PACK_COLS), dtype))
    return jnp.concatenate(parts, axis=0)


def _unpack_rows(packed, shapes, row0=0):
    out = []
    for s in shapes:
        n = 1
        for d in s:
            n *= d
        r = _rows_of(n)
        out.append(packed[row0:row0 + r].reshape(-1)[:n].reshape(s))
        row0 += r
    return out


def _octet_major(re, im):
    parts = []
    for o in range(OCTETS):
        parts += [re[:, o * 512:(o + 1) * 512], im[:, o * 512:(o + 1) * 512]]
    return jnp.concatenate(parts, axis=1)


def _octet_split(v):
    v = v.reshape(OCTETS, 2, 512)
    return v[:, 0].reshape(N_STATE), v[:, 1].reshape(N_STATE)


def _s5_bmat(bb):
    t = bb.reshape(S5_GROUP, OCTETS, 8, 1, S5_STATE) * jnp.eye(8, dtype=F32)[None, None, :, :, None]
    return jnp.transpose(t, (1, 2, 0, 3, 4)).reshape(OCTETS, LANES, 512)


def _s5_bmat_t(dm):
    t = dm.reshape(OCTETS, 8, S5_GROUP, 8, S5_STATE) * jnp.eye(8, dtype=F32)[None, :, None, :, None]
    return jnp.transpose(t.sum(axis=3), (2, 0, 1, 3)).reshape(S5_GROUP, S5_GROUPS, S5_STATE)


def _s5_cmat(c):
    t = c.reshape(OCTETS, 8, 1, S5_GROUP, S5_STATE) * jnp.eye(8, dtype=F32)[None, :, :, None, None]
    return jnp.transpose(t, (0, 2, 4, 1, 3)).reshape(OCTETS, 512, LANES)


def _s5_cmat_t(dm):
    t = dm.reshape(OCTETS, 8, S5_STATE, 8, S5_GROUP) * jnp.eye(8, dtype=F32)[None, :, None, :, None]
    return jnp.transpose(t.sum(axis=1), (0, 2, 3, 1)).reshape(S5_GROUPS, S5_GROUP, S5_STATE)


def local_step(x, ctx, tgt, mod, modc, W):
    n_tok, n_ctx = x.shape[0], ctx.shape[0]
    tm = ROW_TILE
    D = D_MODEL
    sh1, sc1, ga1, sh2, sc2, ga2 = [mod[:, k * D:(k + 1) * D] for k in range(N_MOD)]
    sh1c, sc1c = modc[:, :D], modc[:, D:2 * D]
    g_mix, g_ffn, g_final = W["g_mix"], W["g_ffn"], W["g_final"]
    w_in = W["w_in"]
    w_in_u = w_in[0][:, :S5_WIDTH]

    h = rowcall(f_modulate, "mod1", n_tok, tm, [(x, D, 0, 0)], [g_mix, sc1, sh1], [(D, BF16)], [])[0]
    hc = rowcall(f_modulate, "mod1_ctx", n_ctx, tm, [(ctx, D, 0, 0)], [g_mix, sc1c, sh1c], [(D, BF16)], [])[0]
    proj = matmul(h, w_in, "nn", BF16, "proj_in", shards=True)
    uc = matmul(hc, w_in_u, "nn", BF16, "proj_in_ctx")
    u_lat = proj[:, :S5_WIDTH]
    u_s5 = (to_strand_order(jnp.concatenate([uc, u_lat], axis=0)), to_strand_order(jnp.concatenate([u_lat, uc], axis=0)))

    a_re, a_im, ls = W["s5_a_re"], W["s5_a_im"], W["s5_log_step"][..., None]
    b_re_t = jnp.transpose(W["s5_b_re"], (0, 3, 1, 2))
    b_im_t = jnp.transpose(W["s5_b_im"], (0, 3, 1, 2))
    bb_re, bb_im = s5_discretize(a_re, a_im, ls, b_re_t, b_im_t)
    ls_rep = jnp.repeat(W["s5_log_step"], S5_STATE, axis=1).reshape(2, 1, N_STATE)
    tabs = s5_tables(a_re.reshape(2, 1, N_STATE), a_im.reshape(2, 1, N_STATE), ls_rep)
    pw, qt, bm, cm = [], [], [], []
    for d in range(2):
        pw.append(tuple(jnp.repeat(_octet_major(tabs[k][d], tabs[k + 1][d]), SUB, axis=0) for k in (0, 2)))
        qt.append(tuple(_octet_major(tabs[k][d], tabs[k + 1][d]) for k in (4, 6)))
        bm.append(jnp.concatenate([_s5_bmat(bb_re[d]), _s5_bmat(bb_im[d])], axis=2).astype(BF16))
        cm.append(jnp.concatenate([_s5_cmat(W["s5_c_re"][d]), -_s5_cmat(W["s5_c_im"][d])], axis=1).astype(BF16))
    y0, cin0 = s5_forward(u_s5[0], bm[0], cm[0], pw[0][0], qt[0][0], False, "s5_fwd0")
    y1, cin1 = s5_forward(u_s5[1], bm[1], cm[1], pw[1][1], qt[1][1], True, "s5_fwd1")

    mix_rows = [(proj, 512, 0, 0), (y0, 512, 0, n_ctx // tm), (y1, 512, 0, 0)] + \
               [(proj, 512, k, 0) for k in range(1, 7)]
    mix_vecs = [W["s5_d"], W["s5_w_glu"], W["s5_b_glu"], W["sgu_ln_g"], W["sgu_ln_b"], W["sgu_w"],
                jnp.transpose(W["sgu_b"]), W["w_proj_a"], W["w_proj_b"], W["b_gate"]]
    mrg = rowcall(f_mixer, "mixer", n_tok, tm, mix_rows, mix_vecs, [(D, BF16)], [])[0]
    o = matmul(mrg, W["w_out"], "nn", F32, "proj_out")
    x1, h2 = rowcall(f_resid_mod, "resid_mod2", n_tok, tm, [(x, D, 0, 0), (o, D, 0, 0)], [ga1, g_ffn, sc2, sh2],
                     [(D, F32), (D, BF16)], [])
    up = matmul(h2, W["w_up"], "nn", BF16, "ffn_up", shards=True)
    conv_w = W["conv_w"].reshape(9, 2 * FFN_HIDDEN)
    wg = jnp.pad(conv_w[:, :FFN_HIDDEN], ((0, 7), (0, 0)))
    wv = jnp.pad(conv_w[:, FFN_HIDDEN:], ((0, 7), (0, 0)))
    act, gate, val = conv_forward(up, wg, wv, W["conv_b"])
    dn = matmul(act, W["w_down"], "nn", F32, "ffn_down")

    def final_fn(x1_, dn_, tgt_, ga2_, gf_):
        loss, (dx1_, ddn_, dga2_, dgf_) = jax.value_and_grad(f_final_loss, argnums=(0, 1, 3, 4))(
            x1_, dn_, tgt_, ga2_, gf_)
        return dx1_, ddn_, loss.reshape(1, 1), dga2_, dgf_

    dx2, ddn, loss, d_ga2, d_gfinal = rowcall(
        final_fn, "final_loss", n_tok, tm, [(x1, D, 0, 0), (dn, D, 0, 0), (tgt, D, 0, 0)], [ga2, g_final],
        [(D, F32), (D, BF16)], [(1, 1), (1, D), (1, D)])

    dact = matmul(ddn, W["w_down"], "nt", BF16, "ffn_down_dx")
    d_w_down = matmul(act, ddn, "tn", F32, "ffn_down_dw")
    dup_g, dup_v, dwg, dwv = conv_backward(up, gate, val, dact, wg, wv)
    dup = jnp.concatenate([dup_g, dup_v], axis=1)
    d_conv_w = jnp.concatenate([dwg[:9], dwv[:9]], axis=1).reshape(3, 3, 2 * FFN_HIDDEN)
    d_conv_b = jnp.concatenate([dwg[9:10], dwv[9:10]], axis=1)
    dh2 = matmul(dup, W["w_up"], "nt", BF16, "ffn_up_dx", shards=True)
    d_w_up = matmul(h2, dup, "tn", F32, "ffn_up_dw", shards=True)

    def resid_bwd(x_, o_, dx1_, dh2_, ga_, g_, sc_, sh_):
        _, vjp = jax.vjp(f_resid_mod, x_, o_, ga_, g_, sc_, sh_)
        return vjp((dx1_, dh2_))

    dxa, do, d_ga1, d_gffn, d_sc2, d_sh2 = rowcall(
        resid_bwd, "resid_mod2_bwd", n_tok, tm, [(x, D, 0, 0), (o, D, 0, 0), (dx2, D, 0, 0), (dh2, D, 0, 0)],
        [ga1, g_ffn, sc2, sh2], [(D, F32), (D, BF16)], [(1, D)] * 4)

    dmrg = matmul(do, W["w_out"], "nt", BF16, "proj_out_dx")
    d_w_out = matmul(mrg, do, "tn", F32, "proj_out_dw")

    def mixer_bwd(*args):
        rows, dm, vecs = args[:9], args[9], [v.astype(F32) for v in args[10:]]
        _, vjp = jax.vjp(f_mixer, *rows, *vecs)
        g = vjp(dm)
        return (g[0], g[1], jnp.concatenate([g[3], g[4]], axis=1), jnp.concatenate(g[5:9], axis=1)) + tuple(g[9:])

    mb = rowcall(mixer_bwd, "mixer_bwd", n_tok, tm, mix_rows + [(dmrg, D, 0, 0)], mix_vecs,
                 [(512, BF16), (512, BF16), (1024, BF16), (2048, BF16)], [v.shape for v in mix_vecs])
    du_direct, dys, dzb, dgl = mb[:4]
    d_s5d, d_w_glu, d_b_glu, d_ln_g, d_ln_b, d_sgu_w, d_sgu_bt, d_w_pa, d_w_pb, d_b_gate = mb[4:]

    zc = jnp.zeros((n_ctx, S5_WIDTH), BF16)
    dy_s5 = (to_strand_order(jnp.concatenate([zc, dys], axis=0)), to_strand_order(jnp.concatenate([dys, zc], axis=0)))
    du0, dbm0, dcm0, da0 = s5_backward(u_s5[0], dy_s5[0], cin0, bm[0], cm[0], pw[0][0], qt[0][0], pw[0][1], qt[0][1],
                                       False, "s5_bwd0")
    du1, dbm1, dcm1, da1 = s5_backward(u_s5[1], dy_s5[1], cin1, bm[1], cm[1], pw[1][1], qt[1][1], pw[1][0], qt[1][0],
                                       True, "s5_bwd1")
    add3 = lambda a, b, c: a + b + c
    du_a = rowcall(add3, "du_sum", n_tok, tm, [(du_direct, 512, 0, 0), (du0, 512, 0, n_ctx // tm), (du1, 512, 0, 0)],
                   [], [(512, BF16)], [])[0]
    du_c = rowcall(lambda a, b: a + b, "du_sum_ctx", n_ctx, tm, [(du0, 512, 0, 0), (du1, 512, 0, n_tok // tm)],
                   [], [(512, BF16)], [])[0]

    dab_re, dab_im, dbb_re, dbb_im, d_c_re, d_c_im = [], [], [], [], [], []
    for dbm, dcm, da in ((dbm0, dcm0, da0), (dbm1, dcm1, da1)):
        r, i = _octet_split(da)
        dab_re.append(r.reshape(S5_GROUPS, S5_STATE))
        dab_im.append(i.reshape(S5_GROUPS, S5_STATE))
        dbb_re.append(_s5_bmat_t(dbm[:, :, :512]))
        dbb_im.append(_s5_bmat_t(dbm[:, :, 512:]))
        d_c_re.append(_s5_cmat_t(dcm[:, :512]))
        d_c_im.append(-_s5_cmat_t(dcm[:, 512:]))
    d_a_re, d_a_im, d_ls, d_b_re_t, d_b_im_t = s5_discretize_bwd(
        a_re, a_im, ls, b_re_t, b_im_t, jnp.stack(dab_re), jnp.stack(dab_im), jnp.stack(dbb_re), jnp.stack(dbb_im))

    dproj = jnp.concatenate([du_a, dzb, dgl], axis=1)
    dh = matmul(dproj, w_in, "nt", BF16, "proj_in_dx", shards=True)
    dhc = matmul(du_c, w_in_u, "nt", BF16, "proj_in_ctx_dx")
    d_w_in_c = matmul(hc, du_c, "tn", F32, "proj_in_ctx_dw")
    d_w_in = matmul(h, dproj, "tn", F32, "proj_in_dw", shards=True,
                    init=jnp.pad(d_w_in_c[None], ((0, N_CHIPS - 1), (0, 0), (0, w_in.shape[2] - S5_WIDTH))))

    def mod_bwd_ctx(x_, dh_, g_, sc_, sh_):
        _, vjp = jax.vjp(f_modulate, x_, g_, sc_, sh_)
        return vjp(dh_)[1:]

    d_gmix_c, d_sc1c, d_sh1c = rowcall(mod_bwd_ctx, "mod1_ctx_bwd", n_ctx, tm, [(ctx, D, 0, 0), (dhc, D, 0, 0)],
                                       [g_mix, sc1c, sh1c], [], [(1, D)] * 3)

    def mod_bwd(x_, dh_, dxa_, g_, sc_, sh_):
        _, vjp = jax.vjp(f_modulate, x_, g_, sc_, sh_)
        dx_, dg_, dsc_, dsh_ = vjp(dh_)
        return dx_ + dxa_, dg_, dsc_, dsh_

    zero_d = jnp.zeros((1, D), F32)
    grad_x, d_gmix, d_sc1, d_sh1 = rowcall(
        mod_bwd, "mod1_bwd", n_tok, tm, [(x, D, 0, 0), (dh, D, 0, 0), (dxa, D, 0, 0)], [g_mix, sc1, sh1],
        [(D, F32)], [(1, D)] * 3, ainit=[d_gmix_c, zero_d, zero_d])

    grads = {
        "dmod": jnp.concatenate([d_sh1, d_sc1, d_ga1, d_sh2, d_sc2, d_ga2], axis=1),
        "dmodc": jnp.concatenate([d_sh1c, d_sc1c], axis=1),
        "g_mix": d_gmix,
        "s5_a_re": d_a_re, "s5_a_im": d_a_im, "s5_log_step": d_ls[..., 0],
        "s5_b_re": jnp.transpose(d_b_re_t, (0, 2, 3, 1)), "s5_b_im": jnp.transpose(d_b_im_t, (0, 2, 3, 1)),
        "s5_c_re": jnp.stack(d_c_re), "s5_c_im": jnp.stack(d_c_im), "s5_d": d_s5d, "s5_b_glu": d_b_glu,
        "sgu_ln_g": d_ln_g, "sgu_ln_b": d_ln_b, "sgu_w": d_sgu_w, "sgu_b": jnp.transpose(d_sgu_bt),
        "b_gate": d_b_gate, "g_ffn": d_gffn, "conv_b": d_conv_b, "g_final": d_gfinal, "conv_w": d_conv_w,
        "w_in": d_w_in, "s5_w_glu": d_w_glu, "w_proj_a": d_w_pa, "w_proj_b": d_w_pb, "w_out": d_w_out,
        "w_up": d_w_up, "w_down": d_w_down,
    }
    return loss, grad_x, grads


ADA_COLS = N_MOD * D_MODEL // N_CHIPS
MOD_ROWS = 16


def mod_forward(c16, w, b):
    n = w.shape[1]
    tn = 512

    def body(c_ref, w_ref, b_ref, o_ref):
        cv = c_ref[...]
        cs = cv * jax.nn.sigmoid(cv)
        o_ref[...] = jnp.dot(cs.astype(BF16), w_ref[...].astype(BF16), preferred_element_type=F32) + b_ref[...]

    return _call(body, "mod_forward", jax.ShapeDtypeStruct((MOD_ROWS, n), F32), grid=(n // tn,),
                 in_specs=[pl.BlockSpec((MOD_ROWS, D_MODEL), lambda j: (0, 0)),
                           pl.BlockSpec((D_MODEL, tn), lambda j: (0, j)), pl.BlockSpec((1, tn), lambda j: (0, j))],
                 out_specs=pl.BlockSpec((MOD_ROWS, tn), lambda j: (0, j)), sem=("arbitrary",))(c16, w, b)


def f_ada_outer(ct, dm):
    cs = ct * jax.nn.sigmoid(ct)
    acc = cs[:, 0:1] * dm[0:1]
    for k in range(1, 9):
        acc = acc + cs[:, k:k + 1] * dm[k:k + 1]
    return acc


def f_cctx_grad(z, p4):
    s = jax.nn.sigmoid(z)
    return (p4[0:1] + p4[1:2] + p4[2:3] + p4[3:4]) * (s + z * s * (1.0 - s))


WEIGHT_NAMES = ("c_ctx", "w_ada", "b_ada", "g_mix", "w_in", "s5_a_re", "s5_a_im", "s5_log_step", "s5_b_re",
                "s5_b_im", "s5_c_re", "s5_c_im", "s5_d", "s5_w_glu", "s5_b_glu", "sgu_ln_g", "sgu_ln_b", "sgu_w",
                "sgu_b", "w_proj_a", "w_proj_b", "b_gate", "w_out", "g_ffn", "w_up", "conv_w", "conv_b", "w_down",
                "g_final")
CONV_SHARD = 2 * FFN_HIDDEN // N_CHIPS
SMALL_PACK_ROWS = 512
SMALL_ROW0 = 58


def kernel(x, c, ctx, c_ctx, w_ada, b_ada, g_mix, w_in, s5_a_re, s5_a_im, s5_log_step, s5_b_re, s5_b_im, s5_c_re, s5_c_im, s5_d, s5_w_glu, s5_b_glu, sgu_ln_g, sgu_ln_b, sgu_w, sgu_b, w_proj_a, w_proj_b, b_gate, w_out, g_ffn, w_up, conv_w, conv_b, w_down, g_final, loss_target, m_c_ctx, m_w_ada, m_b_ada, m_g_mix, m_w_in, m_s5_a_re, m_s5_a_im, m_s5_log_step, m_s5_b_re, m_s5_b_im, m_s5_c_re, m_s5_c_im, m_s5_d, m_s5_w_glu, m_s5_b_glu, m_sgu_ln_g, m_sgu_ln_b, m_sgu_w, m_sgu_b, m_w_proj_a, m_w_proj_b, m_b_gate, m_w_out, m_g_ffn, m_w_up, m_conv_w, m_conv_b, m_w_down, m_g_final, v_c_ctx, v_w_ada, v_b_ada, v_g_mix, v_w_in, v_s5_a_re, v_s5_a_im, v_s5_log_step, v_s5_b_re, v_s5_b_im, v_s5_c_re, v_s5_c_im, v_s5_d, v_s5_w_glu, v_s5_b_glu, v_sgu_ln_g, v_sgu_ln_b, v_sgu_w, v_sgu_b, v_w_proj_a, v_w_proj_b, v_b_gate, v_w_out, v_g_ffn, v_w_up, v_conv_w, v_conv_b, v_w_down, v_g_final):
    given = dict(locals())
    wts = {n: given[n] for n in WEIGHT_NAMES}
    ms = {n: given["m_" + n] for n in WEIGHT_NAMES}
    vs = {n: given["v_" + n] for n in WEIGHT_NAMES}
    xi, yi, ci = _place()
    chip = 2 * xi + yi
    dev = 2 * chip + ci
    D = D_MODEL

    c8 = allgather_devices(jnp.pad(c, ((0, 7), (0, 0))), "gather_c")[:, 0, :]
    c16 = jnp.concatenate([c8, c_ctx[None], jnp.zeros((MOD_ROWS - 9, D), F32)], axis=0)
    b_shard = lax.dynamic_slice(b_ada, (0, chip * ADA_COLS), (1, ADA_COLS))
    mod_shard = mod_forward(c16, w_ada[0], b_shard)
    mod_all = allgather_devices(mod_shard, "gather_mod")
    mod_full = jnp.concatenate([mod_all[2 * q] for q in range(N_CHIPS)], axis=1)
    mod = lax.dynamic_slice(mod_full, (dev, 0), (1, N_MOD * D))
    modc = mod_full[8:9]

    big_names = [n for n, _, _ in BIG_SHARDS]
    conv_rows = jnp.pad(conv_w[0].reshape(9, CONV_SHARD), ((0, 7), (0, 0)))
    shards = [wts[n][0].astype(BF16) for n in big_names] + [conv_rows]
    gathered = allgather_chips(shards, "gather_weights")
    gathered = [_fill_own(t, s, chip) for t, s in zip(gathered, shards)]
    W = dict(zip(big_names, gathered[:-1]))
    for n, shape, axis in BIG_SHARDS:
        if axis == 0:
            W[n] = W[n].reshape(N_CHIPS * shape[0], shape[1])
    for n in ("w_proj_a", "w_proj_b"):
        W[n] = jnp.transpose(W[n], (1, 0, 2)).reshape(W[n].shape[1], -1)
    W["conv_w"] = jnp.transpose(gathered[-1][:, :9], (1, 0, 2)).reshape(3, 3, 2 * FFN_HIDDEN)
    for n in ("g_mix", "g_ffn", "s5_d", "s5_b_glu", "sgu_ln_g", "sgu_ln_b", "b_gate", "conv_b"):
        W[n] = wts[n]
    W["g_final"] = g_final[None]
    for n in ("s5_a_re", "s5_a_im", "s5_log_step", "s5_b_re", "s5_b_im", "s5_c_re", "s5_c_im", "sgu_w", "sgu_b"):
        W[n] = wts[n][0]

    loss_part, grad_x, g = local_step(x[0], ctx[0], loss_target[0], mod, modc, W)
    loss = lax.psum(loss_part[0, 0], ("x", "y", "c"))

    g_slots = []
    for n, shape, axis in BIG_SHARDS:
        if n in ("w_proj_a", "w_proj_b"):
            g_slots.append(jnp.transpose(g[n].reshape(shape[0], N_CHIPS, shape[1]), (1, 0, 2)))
        else:
            g_slots.append(g[n].reshape((N_CHIPS,) + shape))
    g_slots = [t.reshape(N_CHIPS, 2, t.shape[1] // 2, t.shape[2]) for t in g_slots]
    core = ci.astype(jnp.int32).reshape(1)
    from_sibling = grad_pair_swap(g_slots, "grad_pair_swap")
    pair = [pair_sum(gs, rv, core, "grad_pair_sum_" + n) for gs, rv, n in zip(g_slots, from_sibling, big_names)]
    from_chips = grad_chip_exchange(pair, "grad_chip_exchange")
    add2 = lambda a, b: a + b
    add4 = lambda a, b, c_, d: ((a + b) + c_) + d
    halves = []
    for fc, n in zip(from_chips, big_names):
        r2, cols = fc.shape[1], fc.shape[2]
        tr = _tile(r2, 256, 8)
        halves.append(rowcall(add4, "grad_chip_sum_" + n, r2, tr,
                              [(fc.reshape(N_CHIPS * r2, cols), cols, 0, q * r2 // tr) for q in range(N_CHIPS)], [],
                              [(cols, F32)], [])[0])
    others = grad_half_swap(halves, "grad_half_swap")
    big_grads = {n: jnp.where(ci == 0, jnp.concatenate([mine, other], axis=0), jnp.concatenate([other, mine], axis=0))
                 for n, mine, other in zip(big_names, halves, others)}

    small_pack = _pack_rows([g["dmod"], g["dmodc"], g["conv_w"]] + [g[n] for n in SMALL_PARAMS], SMALL_PACK_ROWS, F32)
    small_all = _fill_own(allgather_devices(small_pack, "gather_small_grads", copy_own=False), small_pack, dev)
    small_2d = small_all.reshape(N_DEV * SMALL_PACK_ROWS, PACK_COLS)

    def add8(*a):
        s = a[0]
        for t in a[1:]:
            s = s + t
        return s

    small_sum = rowcall(add8, "small_grad_sum", SMALL_PACK_ROWS, 256,
                        [(small_2d, PACK_COLS, 0, k * SMALL_PACK_ROWS // 256) for k in range(N_DEV)], [],
                        [(PACK_COLS, F32)], [])[0]
    dmod_all = small_all[:, 0:N_MOD].reshape(N_DEV, N_MOD * D)
    dmod_sum = small_sum[0:N_MOD].reshape(1, N_MOD * D)
    dmodc_sum = jnp.pad(small_sum[N_MOD:N_MOD + 2].reshape(1, 2 * D), ((0, 0), (0, (N_MOD - 2) * D)))
    conv_grad = _unpack_rows(small_sum, [(3, 3, 2 * FFN_HIDDEN)], row0=N_MOD + 2)[0]
    small_grads = dict(zip(SMALL_PARAMS, _unpack_rows(small_sum, [wts[n].shape for n in SMALL_PARAMS], row0=SMALL_ROW0)))

    dm16 = jnp.concatenate([dmod_all, dmodc_sum, jnp.zeros((MOD_ROWS - 9, N_MOD * D), F32)], axis=0)
    dm_shard = lax.dynamic_slice(dm16, (0, chip * ADA_COLS), (MOD_ROWS, ADA_COLS))
    g_w_ada = rowcall(f_ada_outer, "w_ada_grad", D, 256, [(jnp.transpose(c16), MOD_ROWS, 0, 0)], [dm_shard],
                      [(ADA_COLS, F32)], [])[0]
    g_b_ada = rowcall(add2, "b_ada_grad", 1, 1, [(dmod_sum, N_MOD * D, 0, 0), (dmodc_sum, N_MOD * D, 0, 0)], [],
                      [(N_MOD * D, F32)], [])[0]
    dmc_rows = jnp.pad(dm_shard[8:9], ((0, 7), (0, 0)))
    cctx_part = matmul(dmc_rows, w_ada[0], "nt", F32, "c_ctx_partial")
    cctx_all = allgather_devices(cctx_part, "gather_c_ctx")
    cctx_4 = jnp.stack([cctx_all[2 * q, 0] for q in range(N_CHIPS)])
    g_c_ctx = rowcall(f_cctx_grad, "c_ctx_grad", 1, 1, [(c_ctx[None], D, 0, 0)], [cctx_4], [(D, F32)], [])[0]

    grads = dict(small_grads)
    grads.update(big_grads)
    grads["w_ada"] = g_w_ada
    grads["b_ada"] = g_b_ada
    grads["c_ctx"] = g_c_ctx
    grads["conv_w"] = lax.dynamic_slice(conv_grad, (0, 0, chip * CONV_SHARD), (3, 3, CONV_SHARD))
    grads = {n: grads[n].reshape(wts[n].shape) for n in WEIGHT_NAMES}

    delta, new_m, new_v = {}, {}, {}
    for n in WEIGHT_NAMES:
        shape2d = (-1, wts[n].shape[-1])
        d_, m_, v_ = adamw(wts[n].reshape(shape2d), grads[n].reshape(shape2d), ms[n].reshape(shape2d),
                           vs[n].reshape(shape2d), "adamw_" + n)
        delta[n], new_m[n], new_v[n] = [t.reshape(wts[n].shape) for t in (d_, m_, v_)]

    return (loss, grad_x[None], *[grads[n] for n in WEIGHT_NAMES], *[delta[n] for n in WEIGHT_NAMES],
            *[new_m[n] for n in WEIGHT_NAMES], *[new_v[n] for n in WEIGHT_NAMES])
```

```python
import functools

import jax
import jax.numpy as jnp
from jax import lax
from jax.experimental import pallas as pl
from jax.experimental.pallas import tpu as pltpu

F32, BF16 = jnp.float32, jnp.bfloat16
MESH = pl.DeviceIdType.MESH

D_MODEL = 1024
S5_WIDTH = 512
S5_GROUP = 16
S5_GROUPS = 32
S5_STATE = 64
SGU_WIDTH = 512
SGU_GROUPS = 8
CHUNK = 128
FFN_HIDDEN = 2816
GRID_W = 64
N_MOD = 6
EPS = 1e-6
N_STATE = S5_GROUPS * S5_STATE
OCTETS = 4
SCAN_T = 128
N_CHIPS = 4
N_DEV = 8
LANES = 128
VMEM_LIMIT_BYTES = 56 * 1024 * 1024
CONV_PAD = 72
CONV_ROWS = 256

ADAM_LR, ADAM_B1, ADAM_B2, ADAM_EPS, ADAM_WD, ADAM_STEP = 0.001, 0.9, 0.999, 1e-08, 0.01, 10


def _call(body, name, out_shape, grid=None, in_specs=None, out_specs=None, scratch=(), sem=None, **kw):
    params = pltpu.CompilerParams(dimension_semantics=sem, vmem_limit_bytes=VMEM_LIMIT_BYTES)
    extra = {} if grid is None else {"grid": grid}
    return pl.pallas_call(body, name=name, out_shape=out_shape, in_specs=in_specs, out_specs=out_specs,
                          scratch_shapes=list(scratch), compiler_params=params, **extra, **kw)


def _tile(n, target, mult=LANES):
    best = None
    t = mult
    while t <= min(n, target):
        if n % t == 0:
            best = t
        t += mult
    return best or n


@jax.custom_vjp
def mmul(a, b):
    return jnp.dot(a.astype(BF16), b.astype(BF16), preferred_element_type=F32)


def _mmul_fwd(a, b):
    return mmul(a, b), (a, b)


def _mmul_bwd(res, ct):
    a, b = res
    ctb = ct.astype(BF16)
    da = lax.dot_general(ctb, b.astype(BF16), (((1,), (1,)), ((), ())), preferred_element_type=F32)
    db = lax.dot_general(a.astype(BF16), ctb, (((0,), (0,)), ((), ())), preferred_element_type=F32)
    return da.astype(a.dtype), db.astype(b.dtype)


mmul.defvjp(_mmul_fwd, _mmul_bwd)

_DOT_DIMS = {"nn": ((1,), (0,)), "nt": ((1,), (1,)), "tn": ((0,), (0,))}


MM_TILE = 1408
MM_FULL_K = 2048


def matmul(a, b, mode, out_dtype, name, init=None, shards=False, halves=False):
    if mode == "nn":
        (M, K), N = a.shape, (b.shape[2] * N_CHIPS if shards else b.shape[1])
    elif mode == "nt":
        M, N, K = a.shape[-2], b.shape[-2], a.shape[-1] * (2 if halves else 1)
    else:
        (K, M), N = a.shape, b.shape[-1] * (2 if halves else 1)
    ns = (K if mode == "nt" else N) // N_CHIPS
    tm = _tile(M, MM_TILE, 8 if M < LANES else LANES)
    tn = _tile(ns if shards and mode != "nt" else N, MM_TILE)
    if shards and mode == "nt":
        tk = _tile(ns, MM_TILE)
    else:
        tk = K if K <= MM_FULL_K else _tile(K, MM_TILE)
    nk = K // tk
    per = ns // (tk if mode == "nt" else tn)
    dims = (_DOT_DIMS[mode], ((), ()))
    has_init = init is not None
    use_acc = nk > 1 and out_dtype != F32

    def body(*refs):
        a_ref, b_ref = refs[:2]
        i_ref = refs[2] if has_init else None
        o_ref = refs[3] if has_init else refs[2]
        acc = refs[-1] if use_acc else o_ref
        k = pl.program_id(2)
        part = lax.dot_general(a_ref[...].astype(BF16), b_ref[...].astype(BF16), dims, preferred_element_type=F32)

        @pl.when(k == 0)
        def _():
            first = part + i_ref[...].astype(F32) if has_init else part
            acc[...] = first.astype(acc.dtype)

        if nk > 1:
            @pl.when(k > 0)
            def _():
                acc[...] += part

        if use_acc:
            @pl.when(k == nk - 1)
            def _():
                o_ref[...] = acc[...].astype(o_ref.dtype)

    if mode == "tn":
        a_spec = pl.BlockSpec((tk, tm), lambda i, j, k: (k, i))
    elif halves:
        ph = K // 2 // tk
        a_spec = pl.BlockSpec((None, tm, tk), lambda i, j, k: (k // ph, i, k % ph))
    else:
        a_spec = pl.BlockSpec((tm, tk), lambda i, j, k: (i, k))
    if mode == "nt":
        b_spec = (pl.BlockSpec((None, tn, tk), lambda i, j, k: (k // per, j, k % per)) if shards
                  else pl.BlockSpec((tn, tk), lambda i, j, k: (j, k)))
    elif mode == "tn" and halves:
        ph = N // 2 // tn
        b_spec = pl.BlockSpec((None, tk, tn), lambda i, j, k: (j // ph, k, j % ph))
    else:
        b_spec = (pl.BlockSpec((None, tk, tn), lambda i, j, k: (j // per, k, j % per)) if shards and mode == "nn"
                  else pl.BlockSpec((tk, tn), lambda i, j, k: (k, j)))
    if shards and mode == "tn":
        o_spec = pl.BlockSpec((None, tm, tn), lambda i, j, k: (j // per, i, j % per))
        out_shape = jax.ShapeDtypeStruct((N_CHIPS, M, ns), out_dtype)
    else:
        o_spec = pl.BlockSpec((tm, tn), lambda i, j, k: (i, j))
        out_shape = jax.ShapeDtypeStruct((M, N), out_dtype)
    in_specs = [a_spec, b_spec] + ([o_spec] if has_init else [])
    args = (a, b) + ((init,) if has_init else ())
    return _call(body, name, out_shape, grid=(M // tm, N // tn, nk),
                 in_specs=in_specs, out_specs=o_spec, scratch=[pltpu.VMEM((tm, tn), F32)] if use_acc else [],
                 sem=("parallel", "parallel", "arbitrary"))(*args)


def rowcall(fn, name, nrows, tm, rins, vins, routs, aouts, ainit=None):
    n_r, n_v, n_ro = len(rins), len(vins), len(routs)
    n_i = len(aouts) if ainit is not None else 0

    def body(*refs):
        r_in, v_in, i_in = refs[:n_r], refs[n_r:n_r + n_v], refs[n_r + n_v:n_r + n_v + n_i]
        r_out, a_out = refs[n_r + n_v + n_i:n_r + n_v + n_i + n_ro], refs[n_r + n_v + n_i + n_ro:]
        outs = fn(*[r[...].astype(F32) for r in r_in], *[v[...] for v in v_in])
        if not isinstance(outs, (tuple, list)):
            outs = (outs,)
        for ref, val in zip(r_out, outs[:n_ro]):
            ref[...] = val.astype(ref.dtype)
        if a_out:
            @pl.when(pl.program_id(0) == 0)
            def _():
                for k, ref in enumerate(a_out):
                    ref[...] = i_in[k][...] if n_i else jnp.zeros_like(ref)

            for ref, val in zip(a_out, outs[n_ro:]):
                ref[...] += val.astype(F32)

    def rspec(width, cblk, roff):
        return pl.BlockSpec((tm, width), lambda i: (i + roff, cblk))

    def whole(shape):
        nd = len(shape)
        return pl.BlockSpec(tuple(shape), lambda i: (0,) * nd)

    inits = list(ainit) if n_i else []
    in_specs = [rspec(w, cb, ro) for (_, w, cb, ro) in rins] + [whole(v.shape) for v in vins + inits]
    out_specs = [rspec(w, 0, 0) for (w, _) in routs] + [whole(s) for s in aouts]
    out_shape = [jax.ShapeDtypeStruct((nrows, w), dt) for (w, dt) in routs] + \
                [jax.ShapeDtypeStruct(tuple(s), F32) for s in aouts]
    res = _call(body, name, out_shape, grid=(nrows // tm,), in_specs=in_specs, out_specs=out_specs,
                sem=("arbitrary",))(*[r[0] for r in rins], *vins, *inits)
    return res


def add_into_columns(buf, parts, width, name, tm):
    n = len(parts)

    def body(*refs):
        acc = refs[0][...].astype(F32)
        for p_ref in refs[1:1 + n]:
            acc = acc + p_ref[...].astype(F32)
        refs[1 + n][...] = acc.astype(refs[1 + n].dtype)

    def rows_at(roff):
        return pl.BlockSpec((tm, width), lambda i: (i + roff, 0))

    return _call(body, name, jax.ShapeDtypeStruct(buf.shape, buf.dtype), grid=(buf.shape[0] // tm,),
                 in_specs=[rows_at(0)] + [rows_at(ro) for _, ro in parts], out_specs=rows_at(0), sem=("arbitrary",),
                 input_output_aliases={0: 0})(buf, *[p for p, _ in parts])


def _rms(x):
    return lax.rsqrt(jnp.mean(x * x, axis=-1, keepdims=True) + EPS)


def f_modulate(x, g, sc, sh):
    return (x * _rms(x)) * g * (1.0 + sc) + sh


def f_resid_mod(x, o, ga, g, sc, sh):
    x1 = x + ga * o
    return x1, f_modulate(x1, g, sc, sh)


def f_final_loss(x1, dn, tgt, ga2, gf):
    x2 = x1 + ga2 * dn
    y = (x2 * _rms(x2)) * gf
    err = jnp.square(y - tgt)
    return 0.5 * jnp.sum(jnp.mean(err, axis=-1))


def _sgu_spatial(vn, w, bt):
    lo = lax.broadcasted_iota(jnp.int32, (1, LANES), 1) < (SGU_WIDTH // SGU_GROUPS)
    row_blocks = []
    for r in range(vn.shape[0] // CHUNK):
        rows = vn[r * CHUNK:(r + 1) * CHUNK]
        cols = []
        for j in range(SGU_WIDTH // LANES):
            blk = rows[:, j * LANES:(j + 1) * LANES]
            v_lo = jnp.where(lo, blk, 0.0)
            v_hi = jnp.where(lo, 0.0, blk)
            s = mmul(w[2 * j], v_lo) + mmul(w[2 * j + 1], v_hi)
            bias = jnp.where(lo, bt[:, 2 * j:2 * j + 1], bt[:, 2 * j + 1:2 * j + 2])
            cols.append(s + bias)
        row_blocks.append(jnp.concatenate(cols, axis=1))
    return jnp.concatenate(row_blocks, axis=0) if len(row_blocks) > 1 else row_blocks[0]


def f_mixer(u_a, y0, y1, zu, zv, ga0, ga1, gb0, gb1, d_skip, w_glu, b_glu, ln_g, ln_b, sgu_w, sgu_bt,
            w_pa, w_pb, b_gate):
    ys = u_a * d_skip + y0 + y1
    ge = jax.nn.gelu(ys)
    y_a = ge * jax.nn.sigmoid(mmul(ge, w_glu) + b_glu)
    u_sg = jax.nn.gelu(zu)
    v = jax.nn.gelu(zv)
    vc = v - jnp.mean(v, axis=-1, keepdims=True)
    vn = (vc * lax.rsqrt(jnp.mean(vc * vc, axis=-1, keepdims=True) + EPS)) * ln_g + ln_b
    y_b = u_sg * _sgu_spatial(vn, sgu_w, sgu_bt)
    gl_a = jnp.concatenate([ga0, ga1], axis=1) + b_gate[:, :D_MODEL]
    gl_b = jnp.concatenate([gb0, gb1], axis=1) + b_gate[:, D_MODEL:]
    return jax.nn.sigmoid(gl_a) * mmul(y_a, w_pa) + jax.nn.sigmoid(gl_b) * mmul(y_b, w_pb)


def _cmul(ar, ai, xr, xi):
    return ar * xr - ai * xi, ar * xi + ai * xr


SUB = 8
STRAND = SCAN_T // SUB


def to_strand_order(v):
    n = v.shape[0] // SCAN_T
    return jnp.transpose(v.reshape(n, SUB, STRAND, v.shape[1]), (0, 2, 1, 3)).reshape(v.shape)


def _to_token_order(v):
    i = lax.broadcasted_iota(jnp.int32, (SCAN_T, SCAN_T), 0)
    j = lax.broadcasted_iota(jnp.int32, (SCAN_T, SCAN_T), 1)
    perm = jnp.where(i == STRAND * (j % SUB) + j // SUB, 1.0, 0.0).astype(BF16)
    hi = v.astype(BF16)
    lo = (v - hi.astype(F32)).astype(BF16)
    return jnp.dot(perm, hi, preferred_element_type=F32) + jnp.dot(perm, lo, preferred_element_type=F32)


def _scan_strands(xr, xi, pw_ref, q_ref, col, rev, conj, cr, ci):
    def tab(ref, lo):
        t_r = ref[lo:lo + SUB, col:col + LANES]
        t_i = ref[lo:lo + SUB, col + 512:col + 512 + LANES]
        return t_r, (-t_i if conj else t_i)

    a_r, a_i = tab(pw_ref, (STRAND - 1) * SUB if rev else 0)
    order = list(range(STRAND - 1, -1, -1) if rev else range(STRAND))
    lr, li = [None] * STRAND, [None] * STRAND
    for n, k in enumerate(order):
        lr[k], li[k] = xr[k * SUB:(k + 1) * SUB], xi[k * SUB:(k + 1) * SUB]
        if n:
            m_r, m_i = _cmul(a_r, a_i, lr[order[n - 1]], li[order[n - 1]])
            lr[k], li[k] = lr[k] + m_r, li[k] + m_i
    f_r, f_i = lr[order[-1]], li[order[-1]]
    q_r, q_i = tab(q_ref, 0)
    sub = lax.broadcasted_iota(jnp.int32, (SUB, 1), 0)
    s = 1
    while s < SUB:
        row = (SUB - s) if rev else (s - 1)
        shift = (SUB - s) if rev else s
        m = (sub < SUB - s) if rev else (sub >= s)
        p_r, p_i = _cmul(q_r[row:row + 1], q_i[row:row + 1], pltpu.roll(f_r, shift, 0), pltpu.roll(f_i, shift, 0))
        f_r, f_i = f_r + jnp.where(m, p_r, 0.0), f_i + jnp.where(m, p_i, 0.0)
        s *= 2
    c_r, c_i = jnp.broadcast_to(cr, (SUB, LANES)), jnp.broadcast_to(ci, (SUB, LANES))
    p_r, p_i = _cmul(q_r, q_i, c_r, c_i)
    s_r, s_i = f_r + p_r, f_i + p_i
    edge = 0 if rev else SUB - 1
    first = sub == (SUB - 1 if rev else 0)
    e_r = jnp.where(first, c_r, pltpu.roll(s_r, SUB - 1 if rev else 1, 0))
    e_i = jnp.where(first, c_i, pltpu.roll(s_i, SUB - 1 if rev else 1, 0))
    for k in range(STRAND):
        t_r, t_i = tab(pw_ref, k * SUB)
        p_r, p_i = _cmul(t_r, t_i, e_r, e_i)
        lr[k], li[k] = lr[k] + p_r, li[k] + p_i
    return lr, li, (s_r[edge:edge + 1], s_i[edge:edge + 1]), (e_r, e_i)


SCAN_GROUP = 6


def _lane_cols(o, j):
    col = o * 1024 + j * LANES
    return col, slice(col, col + LANES), slice(col + 512, col + 512 + LANES)


def s5_forward(u, bm, cm, pw, q, rev, name):
    T, G = SCAN_T, SCAN_GROUP
    n_chunks = u.shape[0] // T
    n_steps = n_chunks // G
    order = list(range(G - 1, -1, -1) if rev else range(G))

    def pos(i):
        return (n_steps - 1 - i) if rev else i

    def body(u_ref, bm_ref, cm_ref, pw_ref, q_ref, y_ref, cin_ref, carry):
        @pl.when(pl.program_id(0) == 0)
        def _():
            carry[...] = jnp.zeros_like(carry)

        uv = u_ref[...]
        for o in range(OCTETS):
            bu = jnp.dot(uv[:, o * LANES:(o + 1) * LANES], bm_ref[o], preferred_element_type=F32)
            hr = [[None] * 4 for _ in range(G)]
            hi = [[None] * 4 for _ in range(G)]
            for j in range(4):
                col, sl_r, sl_i = _lane_cols(o, j)
                cr, ci = carry[0:1, sl_r], carry[0:1, sl_i]
                for g in order:
                    rows = slice(g * T, (g + 1) * T)
                    cin_ref[g, 0:1, sl_r] = cr
                    cin_ref[g, 0:1, sl_i] = ci
                    xr, xi, (cr, ci), _ = _scan_strands(
                        bu[rows, j * LANES:(j + 1) * LANES], bu[rows, 512 + j * LANES:512 + (j + 1) * LANES],
                        pw_ref, q_ref, col, rev, False, cr, ci)
                    hr[g][j] = jnp.concatenate(xr, axis=0)
                    hi[g][j] = jnp.concatenate(xi, axis=0)
                carry[0:1, sl_r] = cr
                carry[0:1, sl_i] = ci
            h = jnp.concatenate([jnp.concatenate(hr[g] + hi[g], axis=1) for g in range(G)], axis=0).astype(BF16)
            y = jnp.dot(h, cm_ref[o], preferred_element_type=F32)
            for g in range(G):
                y_ref[g * T:(g + 1) * T, o * LANES:(o + 1) * LANES] = _to_token_order(y[g * T:(g + 1) * T])

    whole3 = lambda s: pl.BlockSpec(s, lambda i: (0, 0, 0))
    whole2 = lambda s: pl.BlockSpec(s, lambda i: (0, 0))
    rows_spec = pl.BlockSpec((G * T, S5_WIDTH), lambda i: (pos(i), 0))
    return _call(
        body, name,
        [jax.ShapeDtypeStruct((n_chunks * T, S5_WIDTH), F32), jax.ShapeDtypeStruct((n_chunks, 1, 2 * N_STATE), F32)],
        grid=(n_steps,),
        in_specs=[rows_spec, whole3(bm.shape), whole3(cm.shape), whole2(pw.shape), whole2(q.shape)],
        out_specs=[rows_spec, pl.BlockSpec((G, 1, 2 * N_STATE), lambda i: (pos(i), 0, 0))],
        scratch=[pltpu.VMEM((1, 2 * N_STATE), F32)], sem=("arbitrary",))(u, bm, cm, pw, q)


def s5_backward(u, dy, cin, bm, cm, pw_h, q_h, pw_l, q_l, rev, name):
    T, G = SCAN_T, SCAN_GROUP
    n_chunks = u.shape[0] // T
    n_steps = n_chunks // G
    adjoint_order = list(range(G) if rev else range(G - 1, -1, -1))

    def pos(i):
        return i if rev else (n_steps - 1 - i)

    def body(u_ref, dy_ref, cin_ref, bm_ref, cm_ref, pwh_ref, qh_ref, pwl_ref, ql_ref, du_ref, dbm_ref, dcm_ref,
             da_ref, lcarry):
        @pl.when(pl.program_id(0) == 0)
        def _():
            lcarry[...] = jnp.zeros_like(lcarry)
            dbm_ref[...] = jnp.zeros_like(dbm_ref)
            dcm_ref[...] = jnp.zeros_like(dcm_ref)
            da_ref[...] = jnp.zeros_like(da_ref)

        uv = u_ref[...]
        dyv = dy_ref[...]
        for o in range(OCTETS):
            u_o = uv[:, o * LANES:(o + 1) * LANES]
            dy_o = dyv[:, o * LANES:(o + 1) * LANES]
            bu = jnp.dot(u_o, bm_ref[o], preferred_element_type=F32)
            gy = lax.dot_general(dy_o, cm_ref[o], (((1,), (1,)), ((), ())), preferred_element_type=F32)
            hs = [[None] * 8 for _ in range(G)]
            ls = [[None] * 8 for _ in range(G)]
            for j in range(4):
                col, sl_r, sl_i = _lane_cols(o, j)
                b_r = slice(j * LANES, (j + 1) * LANES)
                b_i = slice(512 + j * LANES, 512 + (j + 1) * LANES)
                l_r, l_i = lcarry[0:1, sl_r], lcarry[0:1, sl_i]
                acc_r = acc_i = None
                for g in adjoint_order:
                    rows = slice(g * T, (g + 1) * T)
                    xr, xi, _, (e_r, e_i) = _scan_strands(bu[rows, b_r], bu[rows, b_i], pwh_ref, qh_ref, col, rev, False,
                                                          cin_ref[g, 0:1, sl_r], cin_ref[g, 0:1, sl_i])
                    ar_, ai_, (l_r, l_i), _ = _scan_strands(gy[rows, b_r], gy[rows, b_i], pwl_ref, ql_ref, col, not rev,
                                                            True, l_r, l_i)
                    for k in range(STRAND):
                        kp = k + 1 if rev else k - 1
                        p_r, p_i = (e_r, e_i) if not 0 <= kp < STRAND else (xr[kp], xi[kp])
                        t_r = ar_[k] * p_r + ai_[k] * p_i
                        t_i = ai_[k] * p_r - ar_[k] * p_i
                        acc_r, acc_i = (t_r, t_i) if acc_r is None else (acc_r + t_r, acc_i + t_i)
                    hs[g][j], hs[g][4 + j] = jnp.concatenate(xr, axis=0), jnp.concatenate(xi, axis=0)
                    ls[g][j], ls[g][4 + j] = jnp.concatenate(ar_, axis=0), jnp.concatenate(ai_, axis=0)
                lcarry[0:1, sl_r] = l_r
                lcarry[0:1, sl_i] = l_i
                da_ref[0:1, sl_r] += jnp.sum(acc_r, axis=0, keepdims=True)
                da_ref[0:1, sl_i] += jnp.sum(acc_i, axis=0, keepdims=True)
            h = jnp.concatenate([jnp.concatenate(hs[g], axis=1) for g in range(G)], axis=0).astype(BF16)
            lam = jnp.concatenate([jnp.concatenate(ls[g], axis=1) for g in range(G)], axis=0).astype(BF16)
            du = lax.dot_general(lam, bm_ref[o], (((1,), (1,)), ((), ())), preferred_element_type=F32)
            for g in range(G):
                du_ref[g * T:(g + 1) * T, o * LANES:(o + 1) * LANES] = _to_token_order(du[g * T:(g + 1) * T])
            dbm_ref[o] += lax.dot_general(u_o, lam, (((0,), (0,)), ((), ())), preferred_element_type=F32)
            dcm_ref[o] += lax.dot_general(h, dy_o, (((0,), (0,)), ((), ())), preferred_element_type=F32)

    whole3 = lambda s: pl.BlockSpec(s, lambda i: (0, 0, 0))
    whole2 = lambda s: pl.BlockSpec(s, lambda i: (0, 0))
    rows_spec = pl.BlockSpec((G * T, S5_WIDTH), lambda i: (pos(i), 0))
    return _call(
        body, name,
        [jax.ShapeDtypeStruct((n_chunks * T, S5_WIDTH), F32), jax.ShapeDtypeStruct(bm.shape, F32),
         jax.ShapeDtypeStruct(cm.shape, F32), jax.ShapeDtypeStruct((1, 2 * N_STATE), F32)],
        grid=(n_steps,),
        in_specs=[rows_spec, rows_spec, pl.BlockSpec((G, 1, 2 * N_STATE), lambda i: (pos(i), 0, 0)),
                  whole3(bm.shape), whole3(cm.shape), whole2(pw_h.shape), whole2(q_h.shape), whole2(pw_l.shape),
                  whole2(q_l.shape)],
        out_specs=[rows_spec, whole3(bm.shape), whole3(cm.shape), whole2((1, 2 * N_STATE))],
        scratch=[pltpu.VMEM((1, 2 * N_STATE), F32)], sem=("arbitrary",))(u, dy, cin, bm, cm, pw_h, q_h, pw_l, q_l)


def s5_tables(ar, ai, ls):
    def body(ar_ref, ai_ref, ls_ref, *outs):
        dt = jnp.exp(ls_ref[...])
        k = 0
        for n_rows, step in ((STRAND, 1.0), (SUB, float(STRAND))):
            row = lax.broadcasted_iota(jnp.int32, (n_rows, 1), 0)
            for m_int in (row + 1, n_rows - row):
                m = m_int.astype(F32) * step
                mag = jnp.exp(m * (ar_ref[...] * dt))
                ang = m * (ai_ref[...] * dt)
                outs[k][...] = mag * jnp.cos(ang)
                outs[k + 1][...] = mag * jnp.sin(ang)
                k += 2

    vec = pl.BlockSpec((None, 1, N_STATE), lambda d: (d, 0, 0))
    tab = lambda n: pl.BlockSpec((None, n, N_STATE), lambda d: (d, 0, 0))
    shp = lambda n: jax.ShapeDtypeStruct((2, n, N_STATE), F32)
    sizes = [STRAND] * 4 + [SUB] * 4
    return _call(body, "s5_tables", [shp(n) for n in sizes], grid=(2,), in_specs=[vec, vec, vec],
                 out_specs=[tab(n) for n in sizes], sem=("arbitrary",))(ar, ai, ls)


def f_discretize(a_re, a_im, ls, b_re, b_im):
    dt = jnp.exp(ls)
    mag = jnp.exp(a_re * dt)
    ab_re = mag * jnp.cos(a_im * dt)
    ab_im = mag * jnp.sin(a_im * dt)
    p = ab_re - 1.0
    q = ab_im
    den = a_re * a_re + a_im * a_im
    k_re = ((p * a_re + q * a_im) / den)[None]
    k_im = ((q * a_re - p * a_im) / den)[None]
    return ab_re, ab_im, k_re * b_re - k_im * b_im, k_re * b_im + k_im * b_re


def _disc_specs():
    a = pl.BlockSpec((None, S5_GROUPS, S5_STATE), lambda d: (d, 0, 0))
    s = pl.BlockSpec((None, S5_GROUPS, 1), lambda d: (d, 0, 0))
    b = pl.BlockSpec((None, S5_GROUP, S5_GROUPS, S5_STATE), lambda d: (d, 0, 0, 0))
    return a, s, b


def s5_discretize(a_re, a_im, ls, b_re, b_im):
    def body(ar, ai, l, br, bi, obr, obi):
        _, _, r, i = f_discretize(ar[...], ai[...], l[...], br[...], bi[...])
        obr[...] = r
        obi[...] = i

    a, s, b = _disc_specs()
    return _call(body, "s5_discretize", [jax.ShapeDtypeStruct(b_re.shape, F32)] * 2, grid=(2,),
                 in_specs=[a, a, s, b, b], out_specs=[b, b], sem=("arbitrary",))(a_re, a_im, ls, b_re, b_im)


def s5_discretize_bwd(a_re, a_im, ls, b_re, b_im, dab_re, dab_im, dbb_re, dbb_im):
    def body(ar, ai, l, br, bi, c0, c1, c2, c3, o0, o1, o2, o3, o4):
        _, vjp = jax.vjp(f_discretize, ar[...], ai[...], l[...], br[...], bi[...])
        outs = vjp((c0[...], c1[...], c2[...], c3[...]))
        for ref, val in zip((o0, o1, o2, o3, o4), outs):
            ref[...] = val

    a, s, b = _disc_specs()
    shapes = [jax.ShapeDtypeStruct(t.shape, F32) for t in (a_re, a_im, ls, b_re, b_im)]
    return _call(body, "s5_discretize_bwd", shapes, grid=(2,), in_specs=[a, a, s, b, b, a, a, b, b],
                 out_specs=[a, a, s, b, b], sem=("arbitrary",))(a_re, a_im, ls, b_re, b_im, dab_re, dab_im,
                                                                 dbb_re, dbb_im)


def _conv_taps(s_ref, base, n_rows):
    n = n_rows + 2 * CONV_PAD
    ext = s_ref[pl.ds(base, n), :]
    col = (lax.broadcasted_iota(jnp.int32, (n, 1), 0) + (2 * GRID_W - CONV_PAD)) % GRID_W
    left = jnp.where(col == 0, 0.0, pltpu.roll(ext, 1, 0))
    right = jnp.where(col == GRID_W - 1, 0.0, pltpu.roll(ext, n - 1, 0))
    return left, ext, right


def _conv_apply(taps, w_ref, n_rows, flip):
    out = None
    for i in range(3):
        wi = 2 - i if flip else i
        start = CONV_PAD + (i - 1) * GRID_W
        for j in range(3):
            wj = 2 - j if flip else j
            term = w_ref[wi * 3 + wj:wi * 3 + wj + 1, :] * taps[j][start:start + n_rows]
            out = term if out is None else out + term
    return out


def _conv_fill(dst_ref, src_ref, n_tok):
    zeros = jnp.zeros((CONV_PAD, LANES), F32)
    dst_ref[0:CONV_PAD, :] = zeros
    dst_ref[CONV_PAD + n_tok:2 * CONV_PAD + n_tok, :] = zeros

    def step(r, carry):
        base = pl.multiple_of(r * CONV_ROWS, CONV_ROWS)
        dst_ref[pl.ds(base + CONV_PAD, CONV_ROWS), :] = src_ref[pl.ds(base, CONV_ROWS), :].astype(F32)
        return carry

    lax.fori_loop(0, n_tok // CONV_ROWS, step, 0)


def conv_forward(up, wg, wv, bias):
    n_tok = up.shape[0]
    nb = FFN_HIDDEN // LANES

    def body(ug_ref, uv_ref, wg_ref, wv_ref, bg_ref, bv_ref, act_ref, gate_ref, val_ref, sg, sv):
        _conv_fill(sg, ug_ref, n_tok)
        _conv_fill(sv, uv_ref, n_tok)

        def step(r, carry):
            base = pl.multiple_of(r * CONV_ROWS, CONV_ROWS)
            gate = _conv_apply(_conv_taps(sg, base, CONV_ROWS), wg_ref, CONV_ROWS, False) + bg_ref[...]
            val = _conv_apply(_conv_taps(sv, base, CONV_ROWS), wv_ref, CONV_ROWS, False) + bv_ref[...]
            act_ref[pl.ds(base, CONV_ROWS), :] = (gate * jax.nn.sigmoid(gate) * val).astype(BF16)
            gate_ref[pl.ds(base, CONV_ROWS), :] = gate.astype(BF16)
            val_ref[pl.ds(base, CONV_ROWS), :] = val.astype(BF16)
            return carry

        lax.fori_loop(0, n_tok // CONV_ROWS, step, 0)

    col = lambda off: pl.BlockSpec((n_tok, LANES), lambda k: (0, k + off))
    wsp = lambda off: pl.BlockSpec((16, LANES), lambda k: (0, k + off))
    bsp = lambda off: pl.BlockSpec((1, LANES), lambda k: (0, k + off))
    pad = pltpu.VMEM((n_tok + 2 * CONV_PAD, LANES), F32)
    half = jax.ShapeDtypeStruct((n_tok, FFN_HIDDEN), BF16)
    return _call(body, "conv_forward", [half, half, half], grid=(nb,),
                 in_specs=[col(0), col(nb), wsp(0), wsp(0), bsp(0), bsp(nb)], out_specs=[col(0), col(0), col(0)],
                 scratch=[pad, pad], sem=("arbitrary",))(up, up, wg, wv, bias, bias)


def conv_backward(up, gate, val, dact, wg, wv):
    n_tok = up.shape[0]
    nb = FFN_HIDDEN // LANES
    n_steps = n_tok // CONV_ROWS

    def body(ug_ref, uv_ref, gate_ref, val_ref, da_ref, wg_ref, wv_ref, dup_ref, dwg_ref, dwv_ref, sg, sv, sdg, sdv):
        _conv_fill(sg, ug_ref, n_tok)
        _conv_fill(sv, uv_ref, n_tok)
        zeros = jnp.zeros((CONV_PAD, LANES), F32)
        for s_ref in (sdg, sdv):
            s_ref[0:CONV_PAD, :] = zeros
            s_ref[CONV_PAD + n_tok:2 * CONV_PAD + n_tok, :] = zeros
        dwg_ref[...] = jnp.zeros_like(dwg_ref)
        dwv_ref[...] = jnp.zeros_like(dwv_ref)

        def grads(r, carry):
            base = pl.multiple_of(r * CONV_ROWS, CONV_ROWS)
            taps_g = _conv_taps(sg, base, CONV_ROWS)
            taps_v = _conv_taps(sv, base, CONV_ROWS)
            gate = gate_ref[pl.ds(base, CONV_ROWS), :].astype(F32)
            val = val_ref[pl.ds(base, CONV_ROWS), :].astype(F32)
            d_act = da_ref[pl.ds(base, CONV_ROWS), :].astype(F32)
            sig = jax.nn.sigmoid(gate)
            d_gate = d_act * val * (sig * (1.0 + gate * (1.0 - sig)))
            d_val = d_act * (gate * sig)
            sdg[pl.ds(base + CONV_PAD, CONV_ROWS), :] = d_gate
            sdv[pl.ds(base + CONV_PAD, CONV_ROWS), :] = d_val
            for d_out, taps, dw_ref in ((d_gate, taps_g, dwg_ref), (d_val, taps_v, dwv_ref)):
                for i in range(3):
                    start = CONV_PAD + (i - 1) * GRID_W
                    for j in range(3):
                        k = i * 3 + j
                        dw_ref[k:k + 1, :] += jnp.sum(d_out * taps[j][start:start + CONV_ROWS], axis=0, keepdims=True)
                dw_ref[9:10, :] += jnp.sum(d_out, axis=0, keepdims=True)
            return carry

        lax.fori_loop(0, n_steps, grads, 0)

        def spread(r, carry):
            base = pl.multiple_of(r * CONV_ROWS, CONV_ROWS)
            dup_ref[0, pl.ds(base, CONV_ROWS), :] = _conv_apply(
                _conv_taps(sdg, base, CONV_ROWS), wg_ref, CONV_ROWS, True).astype(BF16)
            dup_ref[1, pl.ds(base, CONV_ROWS), :] = _conv_apply(
                _conv_taps(sdv, base, CONV_ROWS), wv_ref, CONV_ROWS, True).astype(BF16)
            return carry

        lax.fori_loop(0, n_steps, spread, 0)

    col = lambda off: pl.BlockSpec((n_tok, LANES), lambda k: (0, k + off))
    wsp = pl.BlockSpec((16, LANES), lambda k: (0, k))
    pad = pltpu.VMEM((n_tok + 2 * CONV_PAD, LANES), F32)
    both = jax.ShapeDtypeStruct((2, n_tok, FFN_HIDDEN), BF16)
    dw = jax.ShapeDtypeStruct((16, FFN_HIDDEN), F32)
    return _call(body, "conv_backward", [both, dw, dw], grid=(nb,),
                 in_specs=[col(0), col(nb), col(0), col(0), col(0), wsp, wsp],
                 out_specs=[pl.BlockSpec((2, n_tok, LANES), lambda k: (0, 0, k)), wsp, wsp],
                 scratch=[pad, pad, pad, pad], sem=("arbitrary",))(up, up, gate, val, dact, wg, wv)


def f_adamw(w, g, m, v):
    m = ADAM_B1 * m + (1.0 - ADAM_B1) * g
    v = ADAM_B2 * v + (1.0 - ADAM_B2) * jnp.square(g)
    m_hat = m / (1.0 - ADAM_B1 ** ADAM_STEP)
    v_hat = v / (1.0 - ADAM_B2 ** ADAM_STEP)
    delta = -ADAM_LR * (m_hat / (jnp.sqrt(v_hat) + ADAM_EPS) + ADAM_WD * w)
    return delta, m, v


def adamw(w, g, m, v, name):
    shape = w.shape
    cols = shape[-1]
    rows = w.size // cols
    two_d = [t.reshape(rows, cols) for t in (w, g, m, v)]
    tm = _tile(rows, 256, 8) if rows % 8 == 0 else rows
    outs = rowcall(f_adamw, name, rows, tm, [(t, cols, 0, 0) for t in two_d], [], [(cols, F32)] * 3, [])
    return tuple(o.reshape(shape) for o in outs)


def _place():
    return lax.axis_index("x"), lax.axis_index("y"), lax.axis_index("c")


_ANY = pl.BlockSpec(memory_space=pl.ANY)


def _fill_own(gathered, own, index):
    return lax.dynamic_update_slice(gathered, own[None], (index,) + (0,) * own.ndim)


def allgather_devices(v, name, copy_own=True):
    def body(v_ref, out_ref, send_sems, recv_sems, local_sem):
        x, y, c = _place()
        me, sibling = (x, y, c), (x, y, 1 - c)
        chips = [(1 - x, y), (x, 1 - y), (1 - x, 1 - y)]

        def slot(p):
            return out_ref.at[4 * p[0] + 2 * p[1] + p[2]]

        def copy(k, block, to, src=None):
            return pltpu.make_async_remote_copy(
                src_ref=slot(block) if src is None else src, dst_ref=slot(block),
                send_sem=send_sems.at[k], recv_sem=recv_sems.at[k], device_id=to, device_id_type=MESH)

        mine = pltpu.make_async_copy(v_ref, slot(me), local_sem)
        if copy_own:
            mine.start()
        first = [copy(0, me, sibling, src=v_ref)]
        first += [copy(1 + j, me, (*chip, c), src=v_ref) for j, chip in enumerate(chips)]
        for cp in first:
            cp.start()
        passed = [copy(4 + j, (*chip, c), sibling) for j, chip in enumerate(chips)]
        for j, chip in enumerate(chips):
            copy(1 + j, (*chip, c), me).wait_recv()
            passed[j].start()
        copy(0, sibling, me).wait_recv()
        for j, chip in enumerate(chips):
            copy(4 + j, (*chip, 1 - c), me).wait_recv()
        for cp in first + passed:
            cp.wait_send()
        if copy_own:
            mine.wait()

    return _call(body, name, jax.ShapeDtypeStruct((N_DEV,) + v.shape, v.dtype), in_specs=[_ANY], out_specs=_ANY,
                 scratch=[pltpu.SemaphoreType.DMA((7,)), pltpu.SemaphoreType.DMA((7,)), pltpu.SemaphoreType.DMA])(v)


def _other_chips(x, y):
    return [(1 - x, y), (x, 1 - y), (1 - x, 1 - y)]


def allgather_chips(vs, name):
    n = len(vs)
    shapes = [v.shape for v in vs]
    vs = [v.reshape((2, v.shape[0] // 2) + v.shape[1:]) for v in vs]

    def body(*refs):
        v_refs, o_refs = refs[:n], refs[n:2 * n]
        send_sems, recv_sems = refs[2 * n:]
        x, y, c = _place()
        sibling = (x, y, 1 - c)
        chips = _other_chips(x, y)

        def rows(a, chip, h):
            return o_refs[a].at[2 * chip[0] + chip[1], h]

        def copy(a, k, chip, h, to, src=None):
            return pltpu.make_async_remote_copy(
                src_ref=rows(a, chip, h) if src is None else src, dst_ref=rows(a, chip, h),
                send_sem=send_sems.at[6 * a + k], recv_sem=recv_sems.at[6 * a + k], device_id=to, device_id_type=MESH)

        first = [copy(a, j, (x, y), c, (*chip, c), src=v_refs[a].at[c])
                 for a in range(n) for j, chip in enumerate(chips)]
        for cp in first:
            cp.start()
        passed = []
        for j, chip in enumerate(chips):
            for a in range(n):
                copy(a, j, chip, c, (x, y, c)).wait_recv()
                passed.append(copy(a, 3 + j, chip, c, sibling))
                passed[-1].start()
        for j, chip in enumerate(chips):
            for a in range(n):
                copy(a, 3 + j, chip, 1 - c, (x, y, c)).wait_recv()
        for cp in first + passed:
            cp.wait_send()

    outs = _call(body, name, [jax.ShapeDtypeStruct((N_CHIPS,) + v.shape, v.dtype) for v in vs], in_specs=[_ANY] * n,
                 out_specs=[_ANY] * n,
                 scratch=[pltpu.SemaphoreType.DMA((6 * n,)), pltpu.SemaphoreType.DMA((6 * n,))])(*vs)
    return [o.reshape((N_CHIPS,) + s) for o, s in zip(outs, shapes)]


def grad_pair_swap(gs, name):
    n = len(gs)

    def body(*refs):
        g_refs, o_refs, send_sems, recv_sems = refs[:n], refs[n:2 * n], refs[2 * n], refs[2 * n + 1]
        x, y, c = _place()
        cps = [pltpu.make_async_remote_copy(
            src_ref=g_refs[a].at[:, 1 - c], dst_ref=o_refs[a], send_sem=send_sems.at[a], recv_sem=recv_sems.at[a],
            device_id=(x, y, 1 - c), device_id_type=MESH) for a in range(n)]
        for cp in cps:
            cp.start()
        for cp in cps:
            cp.wait()

    return _call(body, name, [jax.ShapeDtypeStruct((N_CHIPS,) + g.shape[2:], g.dtype) for g in gs],
                 in_specs=[_ANY] * n, out_specs=[_ANY] * n,
                 scratch=[pltpu.SemaphoreType.DMA((n,)), pltpu.SemaphoreType.DMA((n,))])(*gs)


def pair_sum(g, recv, core, name):
    r2, cols = recv.shape[1], recv.shape[2]
    tr = _tile(r2, 256, 8)
    nt = r2 // tr

    def body(c_ref, g_ref, r_ref, o_ref):
        o_ref[...] = (g_ref[...] + r_ref[...]).astype(o_ref.dtype)

    spec = pl.BlockSpec((None, tr, cols), lambda q, i, c_ref: (q, i, 0))
    grid_spec = pltpu.PrefetchScalarGridSpec(
        num_scalar_prefetch=1, grid=(N_CHIPS, nt),
        in_specs=[pl.BlockSpec((None, None, tr, cols), lambda q, i, c_ref: (q, c_ref[0], i, 0)), spec], out_specs=spec)
    return pl.pallas_call(body, name=name, out_shape=jax.ShapeDtypeStruct(recv.shape, BF16), grid_spec=grid_spec,
                          compiler_params=pltpu.CompilerParams(dimension_semantics=("arbitrary", "arbitrary"),
                                                               vmem_limit_bytes=VMEM_LIMIT_BYTES))(core, g, recv)


def grad_chip_exchange(ps, name):
    n = len(ps)

    def body(*refs):
        p_refs, o_refs = refs[:n], refs[n:2 * n]
        send_sems, recv_sems, local_sems = refs[2 * n:]
        x, y, c = _place()
        chips = _other_chips(x, y)
        me = 2 * x + y
        mine = [pltpu.make_async_copy(p_refs[a].at[me], o_refs[a].at[me], local_sems.at[a]) for a in range(n)]
        sends = [pltpu.make_async_remote_copy(
            src_ref=p_refs[a].at[2 * chip[0] + chip[1]], dst_ref=o_refs[a].at[me], send_sem=send_sems.at[3 * a + j],
            recv_sem=recv_sems.at[3 * a + j], device_id=(*chip, c), device_id_type=MESH)
            for a in range(n) for j, chip in enumerate(chips)]
        for cp in mine + sends:
            cp.start()
        for a in range(n):
            for j, chip in enumerate(chips):
                pltpu.make_async_remote_copy(
                    src_ref=p_refs[a].at[me], dst_ref=o_refs[a].at[2 * chip[0] + chip[1]],
                    send_sem=send_sems.at[3 * a + j], recv_sem=recv_sems.at[3 * a + j], device_id=(*chip, c),
                    device_id_type=MESH).wait_recv()
        for cp in sends:
            cp.wait_send()
        for cp in mine:
            cp.wait()

    return _call(body, name, [jax.ShapeDtypeStruct(p.shape, p.dtype) for p in ps], in_specs=[_ANY] * n,
                 out_specs=[_ANY] * n, scratch=[pltpu.SemaphoreType.DMA((3 * n,)), pltpu.SemaphoreType.DMA((3 * n,)),
                                                pltpu.SemaphoreType.DMA((n,))])(*ps)


def grad_half_swap(ss, name):
    n = len(ss)

    def body(*refs):
        s_refs, o_refs, send_sems, recv_sems = refs[:n], refs[n:2 * n], refs[2 * n], refs[2 * n + 1]
        x, y, c = _place()
        cps = [pltpu.make_async_remote_copy(
            src_ref=s_refs[a], dst_ref=o_refs[a], send_sem=send_sems.at[a], recv_sem=recv_sems.at[a],
            device_id=(x, y, 1 - c), device_id_type=MESH) for a in range(n)]
        for cp in cps:
            cp.start()
        for cp in cps:
            cp.wait()

    return _call(body, name, [jax.ShapeDtypeStruct(s.shape, s.dtype) for s in ss], in_specs=[_ANY] * n,
                 out_specs=[_ANY] * n, scratch=[pltpu.SemaphoreType.DMA((n,)), pltpu.SemaphoreType.DMA((n,))])(*ss)


ROW_TILE = 256

BIG_SHARDS = (("w_in", (D_MODEL, 896), 1), ("s5_w_glu", (128, S5_WIDTH), 0), ("w_proj_a", (S5_WIDTH, 256), 1),
              ("w_proj_b", (SGU_WIDTH, 256), 1), ("w_out", (256, D_MODEL), 0), ("w_up", (D_MODEL, 1408), 1),
              ("w_down", (704, D_MODEL), 0))
SMALL_PARAMS = ("g_mix", "s5_a_re", "s5_a_im", "s5_log_step", "s5_b_re", "s5_b_im", "s5_c_re", "s5_c_im", "s5_d",
                "s5_b_glu", "sgu_ln_g", "sgu_ln_b", "sgu_w", "sgu_b", "b_gate", "g_ffn", "conv_b", "g_final")
PACK_COLS = 1024


def _rows_of(n):
    return -(-n // PACK_COLS)


def _pack_rows(arrays, total_rows, dtype):
    parts = []
    used = 0
    for a in arrays:
        r = _rows_of(a.size)
        parts.append(jnp.pad(a.reshape(-1).astype(dtype), (0, r * PACK_COLS - a.size)).reshape(r, PACK_COLS))
        used += r
    if total_rows > used:
        parts.append(jnp.zeros((total_rows - used, PACK_COLS), dtype))
    return jnp.concatenate(parts, axis=0)


def _unpack_rows(packed, shapes, row0=0):
    out = []
    for s in shapes:
        n = 1
        for d in s:
            n *= d
        r = _rows_of(n)
        out.append(packed[row0:row0 + r].reshape(-1)[:n].reshape(s))
        row0 += r
    return out


def _octet_major(re, im):
    parts = []
    for o in range(OCTETS):
        parts += [re[:, o * 512:(o + 1) * 512], im[:, o * 512:(o + 1) * 512]]
    return jnp.concatenate(parts, axis=1)


def _octet_split(v):
    v = v.reshape(OCTETS, 2, 512)
    return v[:, 0].reshape(N_STATE), v[:, 1].reshape(N_STATE)


def _s5_bmat(bb):
    t = bb.reshape(S5_GROUP, OCTETS, 8, 1, S5_STATE) * jnp.eye(8, dtype=F32)[None, None, :, :, None]
    return jnp.transpose(t, (1, 2, 0, 3, 4)).reshape(OCTETS, LANES, 512)


def _s5_bmat_t(dm):
    t = dm.reshape(OCTETS, 8, S5_GROUP, 8, S5_STATE) * jnp.eye(8, dtype=F32)[None, :, None, :, None]
    return jnp.transpose(t.sum(axis=3), (2, 0, 1, 3)).reshape(S5_GROUP, S5_GROUPS, S5_STATE)


def _s5_cmat(c):
    t = c.reshape(OCTETS, 8, 1, S5_GROUP, S5_STATE) * jnp.eye(8, dtype=F32)[None, :, :, None, None]
    return jnp.transpose(t, (0, 2, 4, 1, 3)).reshape(OCTETS, 512, LANES)


def _s5_cmat_t(dm):
    t = dm.reshape(OCTETS, 8, S5_STATE, 8, S5_GROUP) * jnp.eye(8, dtype=F32)[None, :, None, :, None]
    return jnp.transpose(t.sum(axis=1), (0, 2, 3, 1)).reshape(S5_GROUPS, S5_GROUP, S5_STATE)


def local_step(x, ctx, tgt, mod, modc, W):
    n_tok, n_ctx = x.shape[0], ctx.shape[0]
    tm = ROW_TILE
    D = D_MODEL
    sh1, sc1, ga1, sh2, sc2, ga2 = [mod[:, k * D:(k + 1) * D] for k in range(N_MOD)]
    sh1c, sc1c = modc[:, :D], modc[:, D:2 * D]
    g_mix, g_ffn, g_final = W["g_mix"], W["g_ffn"], W["g_final"]
    w_in = W["w_in"]
    w_in_u = w_in[0][:, :S5_WIDTH]

    h = rowcall(f_modulate, "mod1", n_tok, tm, [(x, D, 0, 0)], [g_mix, sc1, sh1], [(D, BF16)], [])[0]
    hc = rowcall(f_modulate, "mod1_ctx", n_ctx, tm, [(ctx, D, 0, 0)], [g_mix, sc1c, sh1c], [(D, BF16)], [])[0]
    proj = matmul(h, w_in, "nn", BF16, "proj_in", shards=True)
    uc = matmul(hc, w_in_u, "nn", BF16, "proj_in_ctx")
    u_lat = proj[:, :S5_WIDTH]
    u_s5 = (to_strand_order(jnp.concatenate([uc, u_lat], axis=0)), to_strand_order(jnp.concatenate([u_lat, uc], axis=0)))

    a_re, a_im, ls = W["s5_a_re"], W["s5_a_im"], W["s5_log_step"][..., None]
    b_re_t = jnp.transpose(W["s5_b_re"], (0, 3, 1, 2))
    b_im_t = jnp.transpose(W["s5_b_im"], (0, 3, 1, 2))
    bb_re, bb_im = s5_discretize(a_re, a_im, ls, b_re_t, b_im_t)
    ls_rep = jnp.repeat(W["s5_log_step"], S5_STATE, axis=1).reshape(2, 1, N_STATE)
    tabs = s5_tables(a_re.reshape(2, 1, N_STATE), a_im.reshape(2, 1, N_STATE), ls_rep)
    pw, qt, bm, cm = [], [], [], []
    for d in range(2):
        pw.append(tuple(jnp.repeat(_octet_major(tabs[k][d], tabs[k + 1][d]), SUB, axis=0) for k in (0, 2)))
        qt.append(tuple(_octet_major(tabs[k][d], tabs[k + 1][d]) for k in (4, 6)))
        bm.append(jnp.concatenate([_s5_bmat(bb_re[d]), _s5_bmat(bb_im[d])], axis=2).astype(BF16))
        cm.append(jnp.concatenate([_s5_cmat(W["s5_c_re"][d]), -_s5_cmat(W["s5_c_im"][d])], axis=1).astype(BF16))
    y0, cin0 = s5_forward(u_s5[0], bm[0], cm[0], pw[0][0], qt[0][0], False, "s5_fwd0")
    y1, cin1 = s5_forward(u_s5[1], bm[1], cm[1], pw[1][1], qt[1][1], True, "s5_fwd1")

    mix_rows = [(proj, 512, 0, 0), (y0, 512, 0, n_ctx // tm), (y1, 512, 0, 0)] + \
               [(proj, 512, k, 0) for k in range(1, 7)]
    mix_vecs = [W["s5_d"], W["s5_w_glu"], W["s5_b_glu"], W["sgu_ln_g"], W["sgu_ln_b"], W["sgu_w"],
                jnp.transpose(W["sgu_b"]), W["w_proj_a"], W["w_proj_b"], W["b_gate"]]
    mrg = rowcall(f_mixer, "mixer", n_tok, tm, mix_rows, mix_vecs, [(D, BF16)], [])[0]
    o = matmul(mrg, W["w_out"], "nn", F32, "proj_out")
    x1, h2 = rowcall(f_resid_mod, "resid_mod2", n_tok, tm, [(x, D, 0, 0), (o, D, 0, 0)], [ga1, g_ffn, sc2, sh2],
                     [(D, F32), (D, BF16)], [])
    up = matmul(h2, W["w_up"], "nn", BF16, "ffn_up", shards=True)
    conv_w = W["conv_w"].reshape(9, 2 * FFN_HIDDEN)
    wg = jnp.pad(conv_w[:, :FFN_HIDDEN], ((0, 7), (0, 0)))
    wv = jnp.pad(conv_w[:, FFN_HIDDEN:], ((0, 7), (0, 0)))
    act, gate, val = conv_forward(up, wg, wv, W["conv_b"])
    dn = matmul(act, W["w_down"], "nn", F32, "ffn_down")

    def final_fn(x1_, dn_, tgt_, ga2_, gf_):
        loss, (dx1_, ddn_, dga2_, dgf_) = jax.value_and_grad(f_final_loss, argnums=(0, 1, 3, 4))(
            x1_, dn_, tgt_, ga2_, gf_)
        return dx1_, ddn_, loss.reshape(1, 1), dga2_, dgf_

    dx2, ddn, loss, d_ga2, d_gfinal = rowcall(
        final_fn, "final_loss", n_tok, tm, [(x1, D, 0, 0), (dn, D, 0, 0), (tgt, D, 0, 0)], [ga2, g_final],
        [(D, F32), (D, BF16)], [(1, 1), (1, D), (1, D)])

    dact = matmul(ddn, W["w_down"], "nt", BF16, "ffn_down_dx")
    d_w_down = matmul(act, ddn, "tn", F32, "ffn_down_dw")
    dup, dwg, dwv = conv_backward(up, gate, val, dact, wg, wv)
    d_conv_w = jnp.concatenate([dwg[:9], dwv[:9]], axis=1).reshape(3, 3, 2 * FFN_HIDDEN)
    d_conv_b = jnp.concatenate([dwg[9:10], dwv[9:10]], axis=1)
    dh2 = matmul(dup, W["w_up"], "nt", BF16, "ffn_up_dx", shards=True, halves=True)
    d_w_up = matmul(h2, dup, "tn", F32, "ffn_up_dw", shards=True, halves=True)

    def resid_bwd(x_, o_, dx1_, dh2_, ga_, g_, sc_, sh_):
        _, vjp = jax.vjp(f_resid_mod, x_, o_, ga_, g_, sc_, sh_)
        return vjp((dx1_, dh2_))

    dxa, do, d_ga1, d_gffn, d_sc2, d_sh2 = rowcall(
        resid_bwd, "resid_mod2_bwd", n_tok, tm, [(x, D, 0, 0), (o, D, 0, 0), (dx2, D, 0, 0), (dh2, D, 0, 0)],
        [ga1, g_ffn, sc2, sh2], [(D, F32), (D, BF16)], [(1, D)] * 4)

    dmrg = matmul(do, W["w_out"], "nt", BF16, "proj_out_dx")
    d_w_out = matmul(mrg, do, "tn", F32, "proj_out_dw")

    def mixer_bwd(*args):
        rows, dm, vecs = args[:9], args[9], [v.astype(F32) for v in args[10:]]
        _, vjp = jax.vjp(f_mixer, *rows, *vecs)
        g = vjp(dm)
        return (jnp.concatenate([g[0]] + list(g[3:9]), axis=1), g[1]) + tuple(g[9:])

    mb = rowcall(mixer_bwd, "mixer_bwd", n_tok, tm, mix_rows + [(dmrg, D, 0, 0)], mix_vecs,
                 [(w_in.shape[2] * N_CHIPS, BF16), (512, BF16)], [v.shape for v in mix_vecs])
    dproj, dys = mb[:2]
    d_s5d, d_w_glu, d_b_glu, d_ln_g, d_ln_b, d_sgu_w, d_sgu_bt, d_w_pa, d_w_pb, d_b_gate = mb[2:]

    zc = jnp.zeros((n_ctx, S5_WIDTH), BF16)
    dy_s5 = (to_strand_order(jnp.concatenate([zc, dys], axis=0)), to_strand_order(jnp.concatenate([dys, zc], axis=0)))
    du0, dbm0, dcm0, da0 = s5_backward(u_s5[0], dy_s5[0], cin0, bm[0], cm[0], pw[0][0], qt[0][0], pw[0][1], qt[0][1],
                                       False, "s5_bwd0")
    du1, dbm1, dcm1, da1 = s5_backward(u_s5[1], dy_s5[1], cin1, bm[1], cm[1], pw[1][1], qt[1][1], pw[1][0], qt[1][0],
                                       True, "s5_bwd1")
    dproj = add_into_columns(dproj, [(du0, n_ctx // tm), (du1, 0)], S5_WIDTH, "du_sum", tm)
    du_c = rowcall(lambda a, b: a + b, "du_sum_ctx", n_ctx, tm, [(du0, 512, 0, 0), (du1, 512, 0, n_tok // tm)],
                   [], [(512, BF16)], [])[0]

    dab_re, dab_im, dbb_re, dbb_im, d_c_re, d_c_im = [], [], [], [], [], []
    for dbm, dcm, da in ((dbm0, dcm0, da0), (dbm1, dcm1, da1)):
        r, i = _octet_split(da)
        dab_re.append(r.reshape(S5_GROUPS, S5_STATE))
        dab_im.append(i.reshape(S5_GROUPS, S5_STATE))
        dbb_re.append(_s5_bmat_t(dbm[:, :, :512]))
        dbb_im.append(_s5_bmat_t(dbm[:, :, 512:]))
        d_c_re.append(_s5_cmat_t(dcm[:, :512]))
        d_c_im.append(-_s5_cmat_t(dcm[:, 512:]))
    d_a_re, d_a_im, d_ls, d_b_re_t, d_b_im_t = s5_discretize_bwd(
        a_re, a_im, ls, b_re_t, b_im_t, jnp.stack(dab_re), jnp.stack(dab_im), jnp.stack(dbb_re), jnp.stack(dbb_im))

    dh = matmul(dproj, w_in, "nt", BF16, "proj_in_dx", shards=True)
    dhc = matmul(du_c, w_in_u, "nt", BF16, "proj_in_ctx_dx")
    du_c_wide = jnp.pad(du_c, ((0, 0), (0, dproj.shape[1] - S5_WIDTH)))
    d_w_in_c = matmul(hc, du_c_wide, "tn", F32, "proj_in_ctx_dw", shards=True)
    d_w_in = matmul(h, dproj, "tn", F32, "proj_in_dw", shards=True, init=d_w_in_c)

    def mod_bwd_ctx(x_, dh_, g_, sc_, sh_):
        _, vjp = jax.vjp(f_modulate, x_, g_, sc_, sh_)
        return vjp(dh_)[1:]

    d_gmix_c, d_sc1c, d_sh1c = rowcall(mod_bwd_ctx, "mod1_ctx_bwd", n_ctx, tm, [(ctx, D, 0, 0), (dhc, D, 0, 0)],
                                       [g_mix, sc1c, sh1c], [], [(1, D)] * 3)

    def mod_bwd(x_, dh_, dxa_, g_, sc_, sh_):
        _, vjp = jax.vjp(f_modulate, x_, g_, sc_, sh_)
        dx_, dg_, dsc_, dsh_ = vjp(dh_)
        return dx_ + dxa_, dg_, dsc_, dsh_

    zero_d = jnp.zeros((1, D), F32)
    grad_x, d_gmix, d_sc1, d_sh1 = rowcall(
        mod_bwd, "mod1_bwd", n_tok, tm, [(x, D, 0, 0), (dh, D, 0, 0), (dxa, D, 0, 0)], [g_mix, sc1, sh1],
        [(D, F32)], [(1, D)] * 3, ainit=[d_gmix_c, zero_d, zero_d])

    grads = {
        "dmod": jnp.concatenate([d_sh1, d_sc1, d_ga1, d_sh2, d_sc2, d_ga2], axis=1),
        "dmodc": jnp.concatenate([d_sh1c, d_sc1c], axis=1),
        "g_mix": d_gmix,
        "s5_a_re": d_a_re, "s5_a_im": d_a_im, "s5_log_step": d_ls[..., 0],
        "s5_b_re": jnp.transpose(d_b_re_t, (0, 2, 3, 1)), "s5_b_im": jnp.transpose(d_b_im_t, (0, 2, 3, 1)),
        "s5_c_re": jnp.stack(d_c_re), "s5_c_im": jnp.stack(d_c_im), "s5_d": d_s5d, "s5_b_glu": d_b_glu,
        "sgu_ln_g": d_ln_g, "sgu_ln_b": d_ln_b, "sgu_w": d_sgu_w, "sgu_b": jnp.transpose(d_sgu_bt),
        "b_gate": d_b_gate, "g_ffn": d_gffn, "conv_b": d_conv_b, "g_final": d_gfinal, "conv_w": d_conv_w,
        "w_in": d_w_in, "s5_w_glu": d_w_glu, "w_proj_a": d_w_pa, "w_proj_b": d_w_pb, "w_out": d_w_out,
        "w_up": d_w_up, "w_down": d_w_down,
    }
    return loss, grad_x, grads


ADA_COLS = N_MOD * D_MODEL // N_CHIPS
MOD_ROWS = 16


def mod_forward(c16, w, b):
    n = w.shape[1]
    tn = 512

    def body(c_ref, w_ref, b_ref, o_ref):
        cv = c_ref[...]
        cs = cv * jax.nn.sigmoid(cv)
        o_ref[...] = jnp.dot(cs.astype(BF16), w_ref[...].astype(BF16), preferred_element_type=F32) + b_ref[...]

    return _call(body, "mod_forward", jax.ShapeDtypeStruct((MOD_ROWS, n), F32), grid=(n // tn,),
                 in_specs=[pl.BlockSpec((MOD_ROWS, D_MODEL), lambda j: (0, 0)),
                           pl.BlockSpec((D_MODEL, tn), lambda j: (0, j)), pl.BlockSpec((1, tn), lambda j: (0, j))],
                 out_specs=pl.BlockSpec((MOD_ROWS, tn), lambda j: (0, j)), sem=("arbitrary",))(c16, w, b)


def f_ada_outer(ct, dm):
    cs = ct * jax.nn.sigmoid(ct)
    acc = cs[:, 0:1] * dm[0:1]
    for k in range(1, 9):
        acc = acc + cs[:, k:k + 1] * dm[k:k + 1]
    return acc


def f_cctx_grad(z, p4):
    s = jax.nn.sigmoid(z)
    return (p4[0:1] + p4[1:2] + p4[2:3] + p4[3:4]) * (s + z * s * (1.0 - s))


WEIGHT_NAMES = ("c_ctx", "w_ada", "b_ada", "g_mix", "w_in", "s5_a_re", "s5_a_im", "s5_log_step", "s5_b_re",
                "s5_b_im", "s5_c_re", "s5_c_im", "s5_d", "s5_w_glu", "s5_b_glu", "sgu_ln_g", "sgu_ln_b", "sgu_w",
                "sgu_b", "w_proj_a", "w_proj_b", "b_gate", "w_out", "g_ffn", "w_up", "conv_w", "conv_b", "w_down",
                "g_final")
CONV_SHARD = 2 * FFN_HIDDEN // N_CHIPS
SMALL_PACK_ROWS = 512
SMALL_ROW0 = 58


def kernel(x, c, ctx, c_ctx, w_ada, b_ada, g_mix, w_in, s5_a_re, s5_a_im, s5_log_step, s5_b_re, s5_b_im, s5_c_re, s5_c_im, s5_d, s5_w_glu, s5_b_glu, sgu_ln_g, sgu_ln_b, sgu_w, sgu_b, w_proj_a, w_proj_b, b_gate, w_out, g_ffn, w_up, conv_w, conv_b, w_down, g_final, loss_target, m_c_ctx, m_w_ada, m_b_ada, m_g_mix, m_w_in, m_s5_a_re, m_s5_a_im, m_s5_log_step, m_s5_b_re, m_s5_b_im, m_s5_c_re, m_s5_c_im, m_s5_d, m_s5_w_glu, m_s5_b_glu, m_sgu_ln_g, m_sgu_ln_b, m_sgu_w, m_sgu_b, m_w_proj_a, m_w_proj_b, m_b_gate, m_w_out, m_g_ffn, m_w_up, m_conv_w, m_conv_b, m_w_down, m_g_final, v_c_ctx, v_w_ada, v_b_ada, v_g_mix, v_w_in, v_s5_a_re, v_s5_a_im, v_s5_log_step, v_s5_b_re, v_s5_b_im, v_s5_c_re, v_s5_c_im, v_s5_d, v_s5_w_glu, v_s5_b_glu, v_sgu_ln_g, v_sgu_ln_b, v_sgu_w, v_sgu_b, v_w_proj_a, v_w_proj_b, v_b_gate, v_w_out, v_g_ffn, v_w_up, v_conv_w, v_conv_b, v_w_down, v_g_final):
    given = dict(locals())
    wts = {n: given[n] for n in WEIGHT_NAMES}
    ms = {n: given["m_" + n] for n in WEIGHT_NAMES}
    vs = {n: given["v_" + n] for n in WEIGHT_NAMES}
    xi, yi, ci = _place()
    chip = 2 * xi + yi
    dev = 2 * chip + ci
    D = D_MODEL

    c8 = allgather_devices(jnp.pad(c, ((0, 7), (0, 0))), "gather_c")[:, 0, :]
    c16 = jnp.concatenate([c8, c_ctx[None], jnp.zeros((MOD_ROWS - 9, D), F32)], axis=0)
    b_shard = lax.dynamic_slice(b_ada, (0, chip * ADA_COLS), (1, ADA_COLS))
    mod_shard = mod_forward(c16, w_ada[0], b_shard)
    mod_all = allgather_devices(mod_shard, "gather_mod")
    mod_full = jnp.concatenate([mod_all[2 * q] for q in range(N_CHIPS)], axis=1)
    mod = lax.dynamic_slice(mod_full, (dev, 0), (1, N_MOD * D))
    modc = mod_full[8:9]

    big_names = [n for n, _, _ in BIG_SHARDS]
    conv_rows = jnp.pad(conv_w[0].reshape(9, CONV_SHARD), ((0, 7), (0, 0)))
    shards = [wts[n][0].astype(BF16) for n in big_names] + [conv_rows]
    gathered = allgather_chips(shards, "gather_weights")
    gathered = [_fill_own(t, s, chip) for t, s in zip(gathered, shards)]
    W = dict(zip(big_names, gathered[:-1]))
    for n, shape, axis in BIG_SHARDS:
        if axis == 0:
            W[n] = W[n].reshape(N_CHIPS * shape[0], shape[1])
    for n in ("w_proj_a", "w_proj_b"):
        W[n] = jnp.transpose(W[n], (1, 0, 2)).reshape(W[n].shape[1], -1)
    W["conv_w"] = jnp.transpose(gathered[-1][:, :9], (1, 0, 2)).reshape(3, 3, 2 * FFN_HIDDEN)
    for n in ("g_mix", "g_ffn", "s5_d", "s5_b_glu", "sgu_ln_g", "sgu_ln_b", "b_gate", "conv_b"):
        W[n] = wts[n]
    W["g_final"] = g_final[None]
    for n in ("s5_a_re", "s5_a_im", "s5_log_step", "s5_b_re", "s5_b_im", "s5_c_re", "s5_c_im", "sgu_w", "sgu_b"):
        W[n] = wts[n][0]

    loss_part, grad_x, g = local_step(x[0], ctx[0], loss_target[0], mod, modc, W)
    loss = lax.psum(loss_part[0, 0], ("x", "y", "c"))

    g_slots = []
    for n, shape, axis in BIG_SHARDS:
        if n in ("w_proj_a", "w_proj_b"):
            g_slots.append(jnp.transpose(g[n].reshape(shape[0], N_CHIPS, shape[1]), (1, 0, 2)))
        else:
            g_slots.append(g[n].reshape((N_CHIPS,) + shape))
    g_slots = [t.reshape(N_CHIPS, 2, t.shape[1] // 2, t.shape[2]) for t in g_slots]
    core = ci.astype(jnp.int32).reshape(1)
    from_sibling = grad_pair_swap(g_slots, "grad_pair_swap")
    pair = [pair_sum(gs, rv, core, "grad_pair_sum_" + n) for gs, rv, n in zip(g_slots, from_sibling, big_names)]
    from_chips = grad_chip_exchange(pair, "grad_chip_exchange")
    add2 = lambda a, b: a + b
    add4 = lambda a, b, c_, d: ((a + b) + c_) + d
    halves = []
    for fc, n in zip(from_chips, big_names):
        r2, cols = fc.shape[1], fc.shape[2]
        tr = _tile(r2, 256, 8)
        halves.append(rowcall(add4, "grad_chip_sum_" + n, r2, tr,
                              [(fc.reshape(N_CHIPS * r2, cols), cols, 0, q * r2 // tr) for q in range(N_CHIPS)], [],
                              [(cols, F32)], [])[0])
    others = grad_half_swap(halves, "grad_half_swap")
    big_grads = {n: jnp.where(ci == 0, jnp.concatenate([mine, other], axis=0), jnp.concatenate([other, mine], axis=0))
                 for n, mine, other in zip(big_names, halves, others)}

    small_pack = _pack_rows([g["dmod"], g["dmodc"], g["conv_w"]] + [g[n] for n in SMALL_PARAMS], SMALL_PACK_ROWS, F32)
    small_all = _fill_own(allgather_devices(small_pack, "gather_small_grads", copy_own=False), small_pack, dev)
    small_2d = small_all.reshape(N_DEV * SMALL_PACK_ROWS, PACK_COLS)

    def add8(*a):
        s = a[0]
        for t in a[1:]:
            s = s + t
        return s

    small_sum = rowcall(add8, "small_grad_sum", SMALL_PACK_ROWS, 256,
                        [(small_2d, PACK_COLS, 0, k * SMALL_PACK_ROWS // 256) for k in range(N_DEV)], [],
                        [(PACK_COLS, F32)], [])[0]
    dmod_all = small_all[:, 0:N_MOD].reshape(N_DEV, N_MOD * D)
    dmod_sum = small_sum[0:N_MOD].reshape(1, N_MOD * D)
    dmodc_sum = jnp.pad(small_sum[N_MOD:N_MOD + 2].reshape(1, 2 * D), ((0, 0), (0, (N_MOD - 2) * D)))
    conv_grad = _unpack_rows(small_sum, [(3, 3, 2 * FFN_HIDDEN)], row0=N_MOD + 2)[0]
    small_grads = dict(zip(SMALL_PARAMS, _unpack_rows(small_sum, [wts[n].shape for n in SMALL_PARAMS], row0=SMALL_ROW0)))

    dm16 = jnp.concatenate([dmod_all, dmodc_sum, jnp.zeros((MOD_ROWS - 9, N_MOD * D), F32)], axis=0)
    dm_shard = lax.dynamic_slice(dm16, (0, chip * ADA_COLS), (MOD_ROWS, ADA_COLS))
    g_w_ada = rowcall(f_ada_outer, "w_ada_grad", D, 256, [(jnp.transpose(c16), MOD_ROWS, 0, 0)], [dm_shard],
                      [(ADA_COLS, F32)], [])[0]
    g_b_ada = rowcall(add2, "b_ada_grad", 1, 1, [(dmod_sum, N_MOD * D, 0, 0), (dmodc_sum, N_MOD * D, 0, 0)], [],
                      [(N_MOD * D, F32)], [])[0]
    dmc_rows = jnp.pad(dm_shard[8:9], ((0, 7), (0, 0)))
    cctx_part = matmul(dmc_rows, w_ada[0], "nt", F32, "c_ctx_partial")
    cctx_all = allgather_devices(cctx_part, "gather_c_ctx")
    cctx_4 = jnp.stack([cctx_all[2 * q, 0] for q in range(N_CHIPS)])
    g_c_ctx = rowcall(f_cctx_grad, "c_ctx_grad", 1, 1, [(c_ctx[None], D, 0, 0)], [cctx_4], [(D, F32)], [])[0]

    grads = dict(small_grads)
    grads.update(big_grads)
    grads["w_ada"] = g_w_ada
    grads["b_ada"] = g_b_ada
    grads["c_ctx"] = g_c_ctx
    grads["conv_w"] = lax.dynamic_slice(conv_grad, (0, 0, chip * CONV_SHARD), (3, 3, CONV_SHARD))
    grads = {n: grads[n].reshape(wts[n].shape) for n in WEIGHT_NAMES}

    delta, new_m, new_v = {}, {}, {}
    for n in WEIGHT_NAMES:
        shape2d = (-1, wts[n].shape[-1])
        d_, m_, v_ = adamw(wts[n].reshape(shape2d), grads[n].reshape(shape2d), ms[n].reshape(shape2d),
                           vs[n].reshape(shape2d), "adamw_" + n)
        delta[n], new_m[n], new_v[n] = [t.reshape(wts[n].shape) for t in (d_, m_, v_)]

    return (loss, grad_x[None], *[grads[n] for n in WEIGHT_NAMES], *[delta[n] for n in WEIGHT_NAMES],
            *[new_m[n] for n in WEIGHT_NAMES], *[new_v[n] for n in WEIGHT_NAMES])
```

```python
import functools

import jax
import jax.numpy as jnp
from jax import lax
from jax.experimental import pallas as pl
from jax.experimental.pallas import tpu as pltpu

F32, BF16 = jnp.float32, jnp.bfloat16
MESH = pl.DeviceIdType.MESH

D_MODEL = 1024
S5_WIDTH = 512
S5_GROUP = 16
S5_GROUPS = 32
S5_STATE = 64
SGU_WIDTH = 512
SGU_GROUPS = 8
CHUNK = 128
FFN_HIDDEN = 2816
GRID_W = 64
N_MOD = 6
EPS = 1e-6
N_STATE = S5_GROUPS * S5_STATE
OCTETS = 4
SCAN_T = 128
N_CHIPS = 4
N_DEV = 8
LANES = 128
VMEM_LIMIT_BYTES = 56 * 1024 * 1024
CONV_PAD = 72
CONV_ROWS = 256

ADAM_LR, ADAM_B1, ADAM_B2, ADAM_EPS, ADAM_WD, ADAM_STEP = 0.001, 0.9, 0.999, 1e-08, 0.01, 10


def _call(body, name, out_shape, grid=None, in_specs=None, out_specs=None, scratch=(), sem=None, **kw):
    params = pltpu.CompilerParams(dimension_semantics=sem, vmem_limit_bytes=VMEM_LIMIT_BYTES)
    extra = {} if grid is None else {"grid": grid}
    return pl.pallas_call(body, name=name, out_shape=out_shape, in_specs=in_specs, out_specs=out_specs,
                          scratch_shapes=list(scratch), compiler_params=params, **extra, **kw)


def _tile(n, target, mult=LANES):
    best = None
    t = mult
    while t <= min(n, target):
        if n % t == 0:
            best = t
        t += mult
    return best or n


@jax.custom_vjp
def mmul(a, b):
    return jnp.dot(a.astype(BF16), b.astype(BF16), preferred_element_type=F32)


def _mmul_fwd(a, b):
    return mmul(a, b), (a, b)


def _mmul_bwd(res, ct):
    a, b = res
    ctb = ct.astype(BF16)
    da = lax.dot_general(ctb, b.astype(BF16), (((1,), (1,)), ((), ())), preferred_element_type=F32)
    db = lax.dot_general(a.astype(BF16), ctb, (((0,), (0,)), ((), ())), preferred_element_type=F32)
    return da.astype(a.dtype), db.astype(b.dtype)


mmul.defvjp(_mmul_fwd, _mmul_bwd)

_DOT_DIMS = {"nn": ((1,), (0,)), "nt": ((1,), (1,)), "tn": ((0,), (0,))}


MM_TILE = 1408
MM_FULL_K = 2048


def matmul(a, b, mode, out_dtype, name, init=None, shards=False, halves=False):
    if mode == "nn":
        (M, K), N = a.shape, (b.shape[2] * N_CHIPS if shards else b.shape[1])
    elif mode == "nt":
        M, N, K = a.shape[-2], b.shape[-2], a.shape[-1] * (2 if halves else 1)
    else:
        (K, M), N = a.shape, b.shape[-1] * (2 if halves else 1)
    ns = (K if mode == "nt" else N) // N_CHIPS
    tm = _tile(M, MM_TILE, 8 if M < LANES else LANES)
    tn = _tile(ns if shards and mode != "nt" else N, MM_TILE)
    if shards and mode == "nt":
        tk = _tile(ns, MM_TILE)
    else:
        tk = K if K <= MM_FULL_K else _tile(K, MM_TILE)
    nk = K // tk
    per = ns // (tk if mode == "nt" else tn)
    dims = (_DOT_DIMS[mode], ((), ()))
    has_init = init is not None
    use_acc = nk > 1 and out_dtype != F32

    def body(*refs):
        a_ref, b_ref = refs[:2]
        i_ref = refs[2] if has_init else None
        o_ref = refs[3] if has_init else refs[2]
        acc = refs[-1] if use_acc else o_ref
        k = pl.program_id(2)
        part = lax.dot_general(a_ref[...].astype(BF16), b_ref[...].astype(BF16), dims, preferred_element_type=F32)

        @pl.when(k == 0)
        def _():
            first = part + i_ref[...].astype(F32) if has_init else part
            acc[...] = first.astype(acc.dtype)

        if nk > 1:
            @pl.when(k > 0)
            def _():
                acc[...] += part

        if use_acc:
            @pl.when(k == nk - 1)
            def _():
                o_ref[...] = acc[...].astype(o_ref.dtype)

    if mode == "tn":
        a_spec = pl.BlockSpec((tk, tm), lambda i, j, k: (k, i))
    elif halves:
        ph = K // 2 // tk
        a_spec = pl.BlockSpec((None, tm, tk), lambda i, j, k: (k // ph, i, k % ph))
    else:
        a_spec = pl.BlockSpec((tm, tk), lambda i, j, k: (i, k))
    if mode == "nt":
        b_spec = (pl.BlockSpec((None, tn, tk), lambda i, j, k: (k // per, j, k % per)) if shards
                  else pl.BlockSpec((tn, tk), lambda i, j, k: (j, k)))
    elif mode == "tn" and halves:
        ph = N // 2 // tn
        b_spec = pl.BlockSpec((None, tk, tn), lambda i, j, k: (j // ph, k, j % ph))
    else:
        b_spec = (pl.BlockSpec((None, tk, tn), lambda i, j, k: (j // per, k, j % per)) if shards and mode == "nn"
                  else pl.BlockSpec((tk, tn), lambda i, j, k: (k, j)))
    if shards and mode == "tn":
        o_spec = pl.BlockSpec((None, tm, tn), lambda i, j, k: (j // per, i, j % per))
        out_shape = jax.ShapeDtypeStruct((N_CHIPS, M, ns), out_dtype)
    else:
        o_spec = pl.BlockSpec((tm, tn), lambda i, j, k: (i, j))
        out_shape = jax.ShapeDtypeStruct((M, N), out_dtype)
    in_specs = [a_spec, b_spec] + ([o_spec] if has_init else [])
    args = (a, b) + ((init,) if has_init else ())
    return _call(body, name, out_shape, grid=(M // tm, N // tn, nk),
                 in_specs=in_specs, out_specs=o_spec, scratch=[pltpu.VMEM((tm, tn), F32)] if use_acc else [],
                 sem=("parallel", "parallel", "arbitrary"))(*args)


def rowcall(fn, name, nrows, tm, rins, vins, routs, aouts, ainit=None):
    n_r, n_v, n_ro = len(rins), len(vins), len(routs)
    n_i = len(aouts) if ainit is not None else 0

    def body(*refs):
        r_in, v_in, i_in = refs[:n_r], refs[n_r:n_r + n_v], refs[n_r + n_v:n_r + n_v + n_i]
        r_out, a_out = refs[n_r + n_v + n_i:n_r + n_v + n_i + n_ro], refs[n_r + n_v + n_i + n_ro:]
        outs = fn(*[r[...].astype(F32) for r in r_in], *[v[...] for v in v_in])
        if not isinstance(outs, (tuple, list)):
            outs = (outs,)
        for ref, val in zip(r_out, outs[:n_ro]):
            ref[...] = val.astype(ref.dtype)
        if a_out:
            @pl.when(pl.program_id(0) == 0)
            def _():
                for k, ref in enumerate(a_out):
                    ref[...] = i_in[k][...] if n_i else jnp.zeros_like(ref)

            for ref, val in zip(a_out, outs[n_ro:]):
                ref[...] += val.astype(F32)

    def rspec(width, cblk, roff):
        return pl.BlockSpec((tm, width), lambda i: (i + roff, cblk))

    def whole(shape):
        nd = len(shape)
        return pl.BlockSpec(tuple(shape), lambda i: (0,) * nd)

    inits = list(ainit) if n_i else []
    in_specs = [rspec(w, cb, ro) for (_, w, cb, ro) in rins] + [whole(v.shape) for v in vins + inits]
    out_specs = [rspec(w, 0, 0) for (w, _) in routs] + [whole(s) for s in aouts]
    out_shape = [jax.ShapeDtypeStruct((nrows, w), dt) for (w, dt) in routs] + \
                [jax.ShapeDtypeStruct(tuple(s), F32) for s in aouts]
    res = _call(body, name, out_shape, grid=(nrows // tm,), in_specs=in_specs, out_specs=out_specs,
                sem=("arbitrary",))(*[r[0] for r in rins], *vins, *inits)
    return res


def add_into_columns(buf, parts, width, name, tm):
    n = len(parts)

    def body(*refs):
        acc = refs[0][...].astype(F32)
        for p_ref in refs[1:1 + n]:
            acc = acc + p_ref[...].astype(F32)
        refs[1 + n][...] = acc.astype(refs[1 + n].dtype)

    def rows_at(roff):
        return pl.BlockSpec((tm, width), lambda i: (i + roff, 0))

    return _call(body, name, jax.ShapeDtypeStruct(buf.shape, buf.dtype), grid=(buf.shape[0] // tm,),
                 in_specs=[rows_at(0)] + [rows_at(ro) for _, ro in parts], out_specs=rows_at(0), sem=("arbitrary",),
                 input_output_aliases={0: 0})(buf, *[p for p, _ in parts])


def _rms(x):
    return lax.rsqrt(jnp.mean(x * x, axis=-1, keepdims=True) + EPS)


def f_modulate(x, g, sc, sh):
    return (x * _rms(x)) * g * (1.0 + sc) + sh


def f_resid_mod(x, o, ga, g, sc, sh):
    x1 = x + ga * o
    return x1, f_modulate(x1, g, sc, sh)


def f_final_loss(x1, dn, tgt, ga2, gf):
    x2 = x1 + ga2 * dn
    y = (x2 * _rms(x2)) * gf
    err = jnp.square(y - tgt)
    return 0.5 * jnp.sum(jnp.mean(err, axis=-1))


def _sgu_spatial(vn, w, bt):
    lo = lax.broadcasted_iota(jnp.int32, (1, LANES), 1) < (SGU_WIDTH // SGU_GROUPS)
    row_blocks = []
    for r in range(vn.shape[0] // CHUNK):
        rows = vn[r * CHUNK:(r + 1) * CHUNK]
        cols = []
        for j in range(SGU_WIDTH // LANES):
            blk = rows[:, j * LANES:(j + 1) * LANES]
            v_lo = jnp.where(lo, blk, 0.0)
            v_hi = jnp.where(lo, 0.0, blk)
            s = mmul(w[2 * j], v_lo) + mmul(w[2 * j + 1], v_hi)
            bias = jnp.where(lo, bt[:, 2 * j:2 * j + 1], bt[:, 2 * j + 1:2 * j + 2])
            cols.append(s + bias)
        row_blocks.append(jnp.concatenate(cols, axis=1))
    return jnp.concatenate(row_blocks, axis=0) if len(row_blocks) > 1 else row_blocks[0]


def f_mixer(u_a, y0, y1, zu, zv, ga0, ga1, gb0, gb1, d_skip, w_glu, b_glu, ln_g, ln_b, sgu_w, sgu_bt,
            w_pa, w_pb, b_gate):
    ys = u_a * d_skip + y0 + y1
    ge = jax.nn.gelu(ys)
    y_a = ge * jax.nn.sigmoid(mmul(ge, w_glu) + b_glu)
    u_sg = jax.nn.gelu(zu)
    v = jax.nn.gelu(zv)
    vc = v - jnp.mean(v, axis=-1, keepdims=True)
    vn = (vc * lax.rsqrt(jnp.mean(vc * vc, axis=-1, keepdims=True) + EPS)) * ln_g + ln_b
    y_b = u_sg * _sgu_spatial(vn, sgu_w, sgu_bt)
    gl_a = jnp.concatenate([ga0, ga1], axis=1) + b_gate[:, :D_MODEL]
    gl_b = jnp.concatenate([gb0, gb1], axis=1) + b_gate[:, D_MODEL:]
    return jax.nn.sigmoid(gl_a) * mmul(y_a, w_pa) + jax.nn.sigmoid(gl_b) * mmul(y_b, w_pb)


def _cmul(ar, ai, xr, xi):
    return ar * xr - ai * xi, ar * xi + ai * xr


SUB = 8
STRAND = SCAN_T // SUB


def to_strand_order(v):
    n = v.shape[0] // SCAN_T
    return jnp.transpose(v.reshape(n, SUB, STRAND, v.shape[1]), (0, 2, 1, 3)).reshape(v.shape)


def _to_token_order(v):
    i = lax.broadcasted_iota(jnp.int32, (SCAN_T, SCAN_T), 0)
    j = lax.broadcasted_iota(jnp.int32, (SCAN_T, SCAN_T), 1)
    perm = jnp.where(i == STRAND * (j % SUB) + j // SUB, 1.0, 0.0).astype(BF16)
    hi = v.astype(BF16)
    lo = (v - hi.astype(F32)).astype(BF16)
    return jnp.dot(perm, hi, preferred_element_type=F32) + jnp.dot(perm, lo, preferred_element_type=F32)


def _scan_strands(xr, xi, pw_ref, q_ref, col, rev, conj, cr, ci):
    def tab(ref, lo):
        t_r = ref[lo:lo + SUB, col:col + LANES]
        t_i = ref[lo:lo + SUB, col + 512:col + 512 + LANES]
        return t_r, (-t_i if conj else t_i)

    a_r, a_i = tab(pw_ref, (STRAND - 1) * SUB if rev else 0)
    order = list(range(STRAND - 1, -1, -1) if rev else range(STRAND))
    lr, li = [None] * STRAND, [None] * STRAND
    for n, k in enumerate(order):
        lr[k], li[k] = xr[k * SUB:(k + 1) * SUB], xi[k * SUB:(k + 1) * SUB]
        if n:
            m_r, m_i = _cmul(a_r, a_i, lr[order[n - 1]], li[order[n - 1]])
            lr[k], li[k] = lr[k] + m_r, li[k] + m_i
    f_r, f_i = lr[order[-1]], li[order[-1]]
    q_r, q_i = tab(q_ref, 0)
    sub = lax.broadcasted_iota(jnp.int32, (SUB, 1), 0)
    s = 1
    while s < SUB:
        row = (SUB - s) if rev else (s - 1)
        shift = (SUB - s) if rev else s
        m = (sub < SUB - s) if rev else (sub >= s)
        p_r, p_i = _cmul(q_r[row:row + 1], q_i[row:row + 1], pltpu.roll(f_r, shift, 0), pltpu.roll(f_i, shift, 0))
        f_r, f_i = f_r + jnp.where(m, p_r, 0.0), f_i + jnp.where(m, p_i, 0.0)
        s *= 2
    c_r, c_i = jnp.broadcast_to(cr, (SUB, LANES)), jnp.broadcast_to(ci, (SUB, LANES))
    p_r, p_i = _cmul(q_r, q_i, c_r, c_i)
    s_r, s_i = f_r + p_r, f_i + p_i
    edge = 0 if rev else SUB - 1
    first = sub == (SUB - 1 if rev else 0)
    e_r = jnp.where(first, c_r, pltpu.roll(s_r, SUB - 1 if rev else 1, 0))
    e_i = jnp.where(first, c_i, pltpu.roll(s_i, SUB - 1 if rev else 1, 0))
    for k in range(STRAND):
        t_r, t_i = tab(pw_ref, k * SUB)
        p_r, p_i = _cmul(t_r, t_i, e_r, e_i)
        lr[k], li[k] = lr[k] + p_r, li[k] + p_i
    return lr, li, (s_r[edge:edge + 1], s_i[edge:edge + 1]), (e_r, e_i)


SCAN_GROUP = 11


def _lane_cols(o, j):
    col = o * 1024 + j * LANES
    return col, slice(col, col + LANES), slice(col + 512, col + 512 + LANES)


def s5_forward(u, bm, cm, pw, q, rev, name):
    T, G = SCAN_T, SCAN_GROUP
    n_chunks = u.shape[0] // T
    n_steps = n_chunks // G
    order = list(range(G - 1, -1, -1) if rev else range(G))

    def pos(i):
        return (n_steps - 1 - i) if rev else i

    def body(u_ref, bm_ref, cm_ref, pw_ref, q_ref, y_ref, cin_ref, carry):
        @pl.when(pl.program_id(0) == 0)
        def _():
            carry[...] = jnp.zeros_like(carry)

        uv = u_ref[...]
        for o in range(OCTETS):
            bu = jnp.dot(uv[:, o * LANES:(o + 1) * LANES], bm_ref[o], preferred_element_type=F32)
            hr = [[None] * 4 for _ in range(G)]
            hi = [[None] * 4 for _ in range(G)]
            for j in range(4):
                col, sl_r, sl_i = _lane_cols(o, j)
                cr, ci = carry[0:1, sl_r], carry[0:1, sl_i]
                for g in order:
                    rows = slice(g * T, (g + 1) * T)
                    cin_ref[g, 0:1, sl_r] = cr
                    cin_ref[g, 0:1, sl_i] = ci
                    xr, xi, (cr, ci), _ = _scan_strands(
                        bu[rows, j * LANES:(j + 1) * LANES], bu[rows, 512 + j * LANES:512 + (j + 1) * LANES],
                        pw_ref, q_ref, col, rev, False, cr, ci)
                    hr[g][j] = jnp.concatenate(xr, axis=0)
                    hi[g][j] = jnp.concatenate(xi, axis=0)
                carry[0:1, sl_r] = cr
                carry[0:1, sl_i] = ci
            h = jnp.concatenate([jnp.concatenate(hr[g] + hi[g], axis=1) for g in range(G)], axis=0).astype(BF16)
            y = jnp.dot(h, cm_ref[o], preferred_element_type=F32)
            for g in range(G):
                y_ref[g * T:(g + 1) * T, o * LANES:(o + 1) * LANES] = _to_token_order(y[g * T:(g + 1) * T])

    whole3 = lambda s: pl.BlockSpec(s, lambda i: (0, 0, 0))
    whole2 = lambda s: pl.BlockSpec(s, lambda i: (0, 0))
    rows_spec = pl.BlockSpec((G * T, S5_WIDTH), lambda i: (pos(i), 0))
    return _call(
        body, name,
        [jax.ShapeDtypeStruct((n_chunks * T, S5_WIDTH), F32), jax.ShapeDtypeStruct((n_chunks, 1, 2 * N_STATE), F32)],
        grid=(n_steps,),
        in_specs=[rows_spec, whole3(bm.shape), whole3(cm.shape), whole2(pw.shape), whole2(q.shape)],
        out_specs=[rows_spec, pl.BlockSpec((G, 1, 2 * N_STATE), lambda i: (pos(i), 0, 0))],
        scratch=[pltpu.VMEM((1, 2 * N_STATE), F32)], sem=("arbitrary",))(u, bm, cm, pw, q)


def s5_backward(u, dy, cin, bm, cm, pw_h, q_h, pw_l, q_l, rev, name):
    T, G = SCAN_T, SCAN_GROUP
    n_chunks = u.shape[0] // T
    n_steps = n_chunks // G
    adjoint_order = list(range(G) if rev else range(G - 1, -1, -1))

    def pos(i):
        return i if rev else (n_steps - 1 - i)

    def body(u_ref, dy_ref, cin_ref, bm_ref, cm_ref, pwh_ref, qh_ref, pwl_ref, ql_ref, du_ref, dbm_ref, dcm_ref,
             da_ref, lcarry):
        @pl.when(pl.program_id(0) == 0)
        def _():
            lcarry[...] = jnp.zeros_like(lcarry)
            dbm_ref[...] = jnp.zeros_like(dbm_ref)
            dcm_ref[...] = jnp.zeros_like(dcm_ref)
            da_ref[...] = jnp.zeros_like(da_ref)

        uv = u_ref[...]
        dyv = dy_ref[...]
        for o in range(OCTETS):
            u_o = uv[:, o * LANES:(o + 1) * LANES]
            dy_o = dyv[:, o * LANES:(o + 1) * LANES]
            bu = jnp.dot(u_o, bm_ref[o], preferred_element_type=F32)
            gy = lax.dot_general(dy_o, cm_ref[o], (((1,), (1,)), ((), ())), preferred_element_type=F32)
            hs = [[None] * 8 for _ in range(G)]
            ls = [[None] * 8 for _ in range(G)]
            for j in range(4):
                col, sl_r, sl_i = _lane_cols(o, j)
                b_r = slice(j * LANES, (j + 1) * LANES)
                b_i = slice(512 + j * LANES, 512 + (j + 1) * LANES)
                l_r, l_i = lcarry[0:1, sl_r], lcarry[0:1, sl_i]
                acc_r = acc_i = None
                for g in adjoint_order:
                    rows = slice(g * T, (g + 1) * T)
                    xr, xi, _, (e_r, e_i) = _scan_strands(bu[rows, b_r], bu[rows, b_i], pwh_ref, qh_ref, col, rev, False,
                                                          cin_ref[g, 0:1, sl_r], cin_ref[g, 0:1, sl_i])
                    ar_, ai_, (l_r, l_i), _ = _scan_strands(gy[rows, b_r], gy[rows, b_i], pwl_ref, ql_ref, col, not rev,
                                                            True, l_r, l_i)
                    for k in range(STRAND):
                        kp = k + 1 if rev else k - 1
                        p_r, p_i = (e_r, e_i) if not 0 <= kp < STRAND else (xr[kp], xi[kp])
                        t_r = ar_[k] * p_r + ai_[k] * p_i
                        t_i = ai_[k] * p_r - ar_[k] * p_i
                        acc_r, acc_i = (t_r, t_i) if acc_r is None else (acc_r + t_r, acc_i + t_i)
                    hs[g][j], hs[g][4 + j] = jnp.concatenate(xr, axis=0), jnp.concatenate(xi, axis=0)
                    ls[g][j], ls[g][4 + j] = jnp.concatenate(ar_, axis=0), jnp.concatenate(ai_, axis=0)
                lcarry[0:1, sl_r] = l_r
                lcarry[0:1, sl_i] = l_i
                da_ref[0:1, sl_r] += jnp.sum(acc_r, axis=0, keepdims=True)
                da_ref[0:1, sl_i] += jnp.sum(acc_i, axis=0, keepdims=True)
            h = jnp.concatenate([jnp.concatenate(hs[g], axis=1) for g in range(G)], axis=0).astype(BF16)
            lam = jnp.concatenate([jnp.concatenate(ls[g], axis=1) for g in range(G)], axis=0).astype(BF16)
            du = lax.dot_general(lam, bm_ref[o], (((1,), (1,)), ((), ())), preferred_element_type=F32)
            for g in range(G):
                du_ref[g * T:(g + 1) * T, o * LANES:(o + 1) * LANES] = _to_token_order(du[g * T:(g + 1) * T])
            dbm_ref[o] += lax.dot_general(u_o, lam, (((0,), (0,)), ((), ())), preferred_element_type=F32)
            dcm_ref[o] += lax.dot_general(h, dy_o, (((0,), (0,)), ((), ())), preferred_element_type=F32)

    whole3 = lambda s: pl.BlockSpec(s, lambda i: (0, 0, 0))
    whole2 = lambda s: pl.BlockSpec(s, lambda i: (0, 0))
    rows_spec = pl.BlockSpec((G * T, S5_WIDTH), lambda i: (pos(i), 0))
    return _call(
        body, name,
        [jax.ShapeDtypeStruct((n_chunks * T, S5_WIDTH), F32), jax.ShapeDtypeStruct(bm.shape, F32),
         jax.ShapeDtypeStruct(cm.shape, F32), jax.ShapeDtypeStruct((1, 2 * N_STATE), F32)],
        grid=(n_steps,),
        in_specs=[rows_spec, rows_spec, pl.BlockSpec((G, 1, 2 * N_STATE), lambda i: (pos(i), 0, 0)),
                  whole3(bm.shape), whole3(cm.shape), whole2(pw_h.shape), whole2(q_h.shape), whole2(pw_l.shape),
                  whole2(q_l.shape)],
        out_specs=[rows_spec, whole3(bm.shape), whole3(cm.shape), whole2((1, 2 * N_STATE))],
        scratch=[pltpu.VMEM((1, 2 * N_STATE), F32)], sem=("arbitrary",))(u, dy, cin, bm, cm, pw_h, q_h, pw_l, q_l)


def s5_tables(ar, ai, ls):
    def body(ar_ref, ai_ref, ls_ref, *outs):
        dt = jnp.exp(ls_ref[...])
        k = 0
        for n_rows, step in ((STRAND, 1.0), (SUB, float(STRAND))):
            row = lax.broadcasted_iota(jnp.int32, (n_rows, 1), 0)
            for m_int in (row + 1, n_rows - row):
                m = m_int.astype(F32) * step
                mag = jnp.exp(m * (ar_ref[...] * dt))
                ang = m * (ai_ref[...] * dt)
                outs[k][...] = mag * jnp.cos(ang)
                outs[k + 1][...] = mag * jnp.sin(ang)
                k += 2

    vec = pl.BlockSpec((None, 1, N_STATE), lambda d: (d, 0, 0))
    tab = lambda n: pl.BlockSpec((None, n, N_STATE), lambda d: (d, 0, 0))
    shp = lambda n: jax.ShapeDtypeStruct((2, n, N_STATE), F32)
    sizes = [STRAND] * 4 + [SUB] * 4
    return _call(body, "s5_tables", [shp(n) for n in sizes], grid=(2,), in_specs=[vec, vec, vec],
                 out_specs=[tab(n) for n in sizes], sem=("arbitrary",))(ar, ai, ls)


def f_discretize(a_re, a_im, ls, b_re, b_im):
    dt = jnp.exp(ls)
    mag = jnp.exp(a_re * dt)
    ab_re = mag * jnp.cos(a_im * dt)
    ab_im = mag * jnp.sin(a_im * dt)
    p = ab_re - 1.0
    q = ab_im
    den = a_re * a_re + a_im * a_im
    k_re = ((p * a_re + q * a_im) / den)[None]
    k_im = ((q * a_re - p * a_im) / den)[None]
    return ab_re, ab_im, k_re * b_re - k_im * b_im, k_re * b_im + k_im * b_re


def _disc_specs():
    a = pl.BlockSpec((None, S5_GROUPS, S5_STATE), lambda d: (d, 0, 0))
    s = pl.BlockSpec((None, S5_GROUPS, 1), lambda d: (d, 0, 0))
    b = pl.BlockSpec((None, S5_GROUP, S5_GROUPS, S5_STATE), lambda d: (d, 0, 0, 0))
    return a, s, b


def s5_discretize(a_re, a_im, ls, b_re, b_im):
    def body(ar, ai, l, br, bi, obr, obi):
        _, _, r, i = f_discretize(ar[...], ai[...], l[...], br[...], bi[...])
        obr[...] = r
        obi[...] = i

    a, s, b = _disc_specs()
    return _call(body, "s5_discretize", [jax.ShapeDtypeStruct(b_re.shape, F32)] * 2, grid=(2,),
                 in_specs=[a, a, s, b, b], out_specs=[b, b], sem=("arbitrary",))(a_re, a_im, ls, b_re, b_im)


def s5_discretize_bwd(a_re, a_im, ls, b_re, b_im, dab_re, dab_im, dbb_re, dbb_im):
    def body(ar, ai, l, br, bi, c0, c1, c2, c3, o0, o1, o2, o3, o4):
        _, vjp = jax.vjp(f_discretize, ar[...], ai[...], l[...], br[...], bi[...])
        outs = vjp((c0[...], c1[...], c2[...], c3[...]))
        for ref, val in zip((o0, o1, o2, o3, o4), outs):
            ref[...] = val

    a, s, b = _disc_specs()
    shapes = [jax.ShapeDtypeStruct(t.shape, F32) for t in (a_re, a_im, ls, b_re, b_im)]
    return _call(body, "s5_discretize_bwd", shapes, grid=(2,), in_specs=[a, a, s, b, b, a, a, b, b],
                 out_specs=[a, a, s, b, b], sem=("arbitrary",))(a_re, a_im, ls, b_re, b_im, dab_re, dab_im,
                                                                 dbb_re, dbb_im)


def _conv_taps(s_ref, base, n_rows):
    n = n_rows + 2 * CONV_PAD
    ext = s_ref[pl.ds(base, n), :]
    col = (lax.broadcasted_iota(jnp.int32, (n, 1), 0) + (2 * GRID_W - CONV_PAD)) % GRID_W
    left = jnp.where(col == 0, 0.0, pltpu.roll(ext, 1, 0))
    right = jnp.where(col == GRID_W - 1, 0.0, pltpu.roll(ext, n - 1, 0))
    return left, ext, right


def _conv_apply(taps, w_ref, n_rows, flip):
    out = None
    for i in range(3):
        wi = 2 - i if flip else i
        start = CONV_PAD + (i - 1) * GRID_W
        for j in range(3):
            wj = 2 - j if flip else j
            term = w_ref[wi * 3 + wj:wi * 3 + wj + 1, :] * taps[j][start:start + n_rows]
            out = term if out is None else out + term
    return out


def _conv_fill(dst_ref, src_ref, n_tok):
    zeros = jnp.zeros((CONV_PAD, LANES), F32)
    dst_ref[0:CONV_PAD, :] = zeros
    dst_ref[CONV_PAD + n_tok:2 * CONV_PAD + n_tok, :] = zeros

    def step(r, carry):
        base = pl.multiple_of(r * CONV_ROWS, CONV_ROWS)
        dst_ref[pl.ds(base + CONV_PAD, CONV_ROWS), :] = src_ref[pl.ds(base, CONV_ROWS), :].astype(F32)
        return carry

    lax.fori_loop(0, n_tok // CONV_ROWS, step, 0)


def conv_forward(up, wg, wv, bias):
    n_tok = up.shape[0]
    nb = FFN_HIDDEN // LANES

    def body(ug_ref, uv_ref, wg_ref, wv_ref, bg_ref, bv_ref, act_ref, gate_ref, val_ref, sg, sv):
        _conv_fill(sg, ug_ref, n_tok)
        _conv_fill(sv, uv_ref, n_tok)

        def step(r, carry):
            base = pl.multiple_of(r * CONV_ROWS, CONV_ROWS)
            gate = _conv_apply(_conv_taps(sg, base, CONV_ROWS), wg_ref, CONV_ROWS, False) + bg_ref[...]
            val = _conv_apply(_conv_taps(sv, base, CONV_ROWS), wv_ref, CONV_ROWS, False) + bv_ref[...]
            act_ref[pl.ds(base, CONV_ROWS), :] = (gate * jax.nn.sigmoid(gate) * val).astype(BF16)
            gate_ref[pl.ds(base, CONV_ROWS), :] = gate.astype(BF16)
            val_ref[pl.ds(base, CONV_ROWS), :] = val.astype(BF16)
            return carry

        lax.fori_loop(0, n_tok // CONV_ROWS, step, 0)

    col = lambda off: pl.BlockSpec((n_tok, LANES), lambda k: (0, k + off))
    wsp = lambda off: pl.BlockSpec((16, LANES), lambda k: (0, k + off))
    bsp = lambda off: pl.BlockSpec((1, LANES), lambda k: (0, k + off))
    pad = pltpu.VMEM((n_tok + 2 * CONV_PAD, LANES), F32)
    half = jax.ShapeDtypeStruct((n_tok, FFN_HIDDEN), BF16)
    return _call(body, "conv_forward", [half, half, half], grid=(nb,),
                 in_specs=[col(0), col(nb), wsp(0), wsp(0), bsp(0), bsp(nb)], out_specs=[col(0), col(0), col(0)],
                 scratch=[pad, pad], sem=("arbitrary",))(up, up, wg, wv, bias, bias)


def conv_backward(up, gate, val, dact, wg, wv):
    n_tok = up.shape[0]
    nb = FFN_HIDDEN // LANES
    n_steps = n_tok // CONV_ROWS

    def body(ug_ref, uv_ref, gate_ref, val_ref, da_ref, wg_ref, wv_ref, dup_ref, dwg_ref, dwv_ref, sg, sv, sdg, sdv):
        _conv_fill(sg, ug_ref, n_tok)
        _conv_fill(sv, uv_ref, n_tok)
        zeros = jnp.zeros((CONV_PAD, LANES), F32)
        for s_ref in (sdg, sdv):
            s_ref[0:CONV_PAD, :] = zeros
            s_ref[CONV_PAD + n_tok:2 * CONV_PAD + n_tok, :] = zeros
        dwg_ref[...] = jnp.zeros_like(dwg_ref)
        dwv_ref[...] = jnp.zeros_like(dwv_ref)

        def grads(r, carry):
            base = pl.multiple_of(r * CONV_ROWS, CONV_ROWS)
            taps_g = _conv_taps(sg, base, CONV_ROWS)
            taps_v = _conv_taps(sv, base, CONV_ROWS)
            gate = gate_ref[pl.ds(base, CONV_ROWS), :].astype(F32)
            val = val_ref[pl.ds(base, CONV_ROWS), :].astype(F32)
            d_act = da_ref[pl.ds(base, CONV_ROWS), :].astype(F32)
            sig = jax.nn.sigmoid(gate)
            d_gate = d_act * val * (sig * (1.0 + gate * (1.0 - sig)))
            d_val = d_act * (gate * sig)
            sdg[pl.ds(base + CONV_PAD, CONV_ROWS), :] = d_gate
            sdv[pl.ds(base + CONV_PAD, CONV_ROWS), :] = d_val
            for d_out, taps, dw_ref in ((d_gate, taps_g, dwg_ref), (d_val, taps_v, dwv_ref)):
                for i in range(3):
                    start = CONV_PAD + (i - 1) * GRID_W
                    for j in range(3):
                        k = i * 3 + j
                        dw_ref[k:k + 1, :] += jnp.sum(d_out * taps[j][start:start + CONV_ROWS], axis=0, keepdims=True)
                dw_ref[9:10, :] += jnp.sum(d_out, axis=0, keepdims=True)
            return carry

        lax.fori_loop(0, n_steps, grads, 0)

        def spread(r, carry):
            base = pl.multiple_of(r * CONV_ROWS, CONV_ROWS)
            dup_ref[0, pl.ds(base, CONV_ROWS), :] = _conv_apply(
                _conv_taps(sdg, base, CONV_ROWS), wg_ref, CONV_ROWS, True).astype(BF16)
            dup_ref[1, pl.ds(base, CONV_ROWS), :] = _conv_apply(
                _conv_taps(sdv, base, CONV_ROWS), wv_ref, CONV_ROWS, True).astype(BF16)
            return carry

        lax.fori_loop(0, n_steps, spread, 0)

    col = lambda off: pl.BlockSpec((n_tok, LANES), lambda k: (0, k + off))
    wsp = pl.BlockSpec((16, LANES), lambda k: (0, k))
    pad = pltpu.VMEM((n_tok + 2 * CONV_PAD, LANES), F32)
    both = jax.ShapeDtypeStruct((2, n_tok, FFN_HIDDEN), BF16)
    dw = jax.ShapeDtypeStruct((16, FFN_HIDDEN), F32)
    return _call(body, "conv_backward", [both, dw, dw], grid=(nb,),
                 in_specs=[col(0), col(nb), col(0), col(0), col(0), wsp, wsp],
                 out_specs=[pl.BlockSpec((2, n_tok, LANES), lambda k: (0, 0, k)), wsp, wsp],
                 scratch=[pad, pad, pad, pad], sem=("arbitrary",))(up, up, gate, val, dact, wg, wv)


def f_adamw(w, g, m, v):
    m = ADAM_B1 * m + (1.0 - ADAM_B1) * g
    v = ADAM_B2 * v + (1.0 - ADAM_B2) * jnp.square(g)
    m_hat = m / (1.0 - ADAM_B1 ** ADAM_STEP)
    v_hat = v / (1.0 - ADAM_B2 ** ADAM_STEP)
    delta = -ADAM_LR * (m_hat / (jnp.sqrt(v_hat) + ADAM_EPS) + ADAM_WD * w)
    return delta, m, v


def adamw(w, g, m, v, name):
    shape = w.shape
    cols = shape[-1]
    rows = w.size // cols
    two_d = [t.reshape(rows, cols) for t in (w, g, m, v)]
    tm = _tile(rows, 256, 8) if rows % 8 == 0 else rows
    outs = rowcall(f_adamw, name, rows, tm, [(t, cols, 0, 0) for t in two_d], [], [(cols, F32)] * 3, [])
    return tuple(o.reshape(shape) for o in outs)


def _place():
    return lax.axis_index("x"), lax.axis_index("y"), lax.axis_index("c")


_ANY = pl.BlockSpec(memory_space=pl.ANY)


def _fill_own(gathered, own, index):
    return lax.dynamic_update_slice(gathered, own[None], (index,) + (0,) * own.ndim)


def allgather_devices(v, name, copy_own=True):
    def body(v_ref, out_ref, send_sems, recv_sems, local_sem):
        x, y, c = _place()
        me, sibling = (x, y, c), (x, y, 1 - c)
        chips = [(1 - x, y), (x, 1 - y), (1 - x, 1 - y)]

        def slot(p):
            return out_ref.at[4 * p[0] + 2 * p[1] + p[2]]

        def copy(k, block, to, src=None):
            return pltpu.make_async_remote_copy(
                src_ref=slot(block) if src is None else src, dst_ref=slot(block),
                send_sem=send_sems.at[k], recv_sem=recv_sems.at[k], device_id=to, device_id_type=MESH)

        mine = pltpu.make_async_copy(v_ref, slot(me), local_sem)
        if copy_own:
            mine.start()
        first = [copy(0, me, sibling, src=v_ref)]
        first += [copy(1 + j, me, (*chip, c), src=v_ref) for j, chip in enumerate(chips)]
        for cp in first:
            cp.start()
        passed = [copy(4 + j, (*chip, c), sibling) for j, chip in enumerate(chips)]
        for j, chip in enumerate(chips):
            copy(1 + j, (*chip, c), me).wait_recv()
            passed[j].start()
        copy(0, sibling, me).wait_recv()
        for j, chip in enumerate(chips):
            copy(4 + j, (*chip, 1 - c), me).wait_recv()
        for cp in first + passed:
            cp.wait_send()
        if copy_own:
            mine.wait()

    return _call(body, name, jax.ShapeDtypeStruct((N_DEV,) + v.shape, v.dtype), in_specs=[_ANY], out_specs=_ANY,
                 scratch=[pltpu.SemaphoreType.DMA((7,)), pltpu.SemaphoreType.DMA((7,)), pltpu.SemaphoreType.DMA])(v)


def _other_chips(x, y):
    return [(1 - x, y), (x, 1 - y), (1 - x, 1 - y)]


def allgather_chips(vs, name):
    n = len(vs)
    shapes = [v.shape for v in vs]
    vs = [v.reshape((2, v.shape[0] // 2) + v.shape[1:]) for v in vs]

    def body(*refs):
        v_refs, o_refs = refs[:n], refs[n:2 * n]
        send_sems, recv_sems = refs[2 * n:]
        x, y, c = _place()
        sibling = (x, y, 1 - c)
        chips = _other_chips(x, y)

        def rows(a, chip, h):
            return o_refs[a].at[2 * chip[0] + chip[1], h]

        def copy(a, k, chip, h, to, src=None):
            return pltpu.make_async_remote_copy(
                src_ref=rows(a, chip, h) if src is None else src, dst_ref=rows(a, chip, h),
                send_sem=send_sems.at[6 * a + k], recv_sem=recv_sems.at[6 * a + k], device_id=to, device_id_type=MESH)

        first = [copy(a, j, (x, y), c, (*chip, c), src=v_refs[a].at[c])
                 for a in range(n) for j, chip in enumerate(chips)]
        for cp in first:
            cp.start()
        passed = []
        for j, chip in enumerate(chips):
            for a in range(n):
                copy(a, j, chip, c, (x, y, c)).wait_recv()
                passed.append(copy(a, 3 + j, chip, c, sibling))
                passed[-1].start()
        for j, chip in enumerate(chips):
            for a in range(n):
                copy(a, 3 + j, chip, 1 - c, (x, y, c)).wait_recv()
        for cp in first + passed:
            cp.wait_send()

    outs = _call(body, name, [jax.ShapeDtypeStruct((N_CHIPS,) + v.shape, v.dtype) for v in vs], in_specs=[_ANY] * n,
                 out_specs=[_ANY] * n,
                 scratch=[pltpu.SemaphoreType.DMA((6 * n,)), pltpu.SemaphoreType.DMA((6 * n,))])(*vs)
    return [o.reshape((N_CHIPS,) + s) for o, s in zip(outs, shapes)]


def grad_pair_swap(gs, name):
    n = len(gs)

    def body(*refs):
        g_refs, o_refs, send_sems, recv_sems = refs[:n], refs[n:2 * n], refs[2 * n], refs[2 * n + 1]
        x, y, c = _place()
        cps = [pltpu.make_async_remote_copy(
            src_ref=g_refs[a].at[:, 1 - c], dst_ref=o_refs[a], send_sem=send_sems.at[a], recv_sem=recv_sems.at[a],
            device_id=(x, y, 1 - c), device_id_type=MESH) for a in range(n)]
        for cp in cps:
            cp.start()
        for cp in cps:
            cp.wait()

    return _call(body, name, [jax.ShapeDtypeStruct((N_CHIPS,) + g.shape[2:], g.dtype) for g in gs],
                 in_specs=[_ANY] * n, out_specs=[_ANY] * n,
                 scratch=[pltpu.SemaphoreType.DMA((n,)), pltpu.SemaphoreType.DMA((n,))])(*gs)


def pair_sum(g, recv, core, name):
    r2, cols = recv.shape[1], recv.shape[2]
    tr = _tile(r2, 256, 8)
    nt = r2 // tr

    def body(c_ref, g_ref, r_ref, o_ref):
        o_ref[...] = (g_ref[...] + r_ref[...]).astype(o_ref.dtype)

    spec = pl.BlockSpec((None, tr, cols), lambda q, i, c_ref: (q, i, 0))
    grid_spec = pltpu.PrefetchScalarGridSpec(
        num_scalar_prefetch=1, grid=(N_CHIPS, nt),
        in_specs=[pl.BlockSpec((None, None, tr, cols), lambda q, i, c_ref: (q, c_ref[0], i, 0)), spec], out_specs=spec)
    return pl.pallas_call(body, name=name, out_shape=jax.ShapeDtypeStruct(recv.shape, BF16), grid_spec=grid_spec,
                          compiler_params=pltpu.CompilerParams(dimension_semantics=("arbitrary", "arbitrary"),
                                                               vmem_limit_bytes=VMEM_LIMIT_BYTES))(core, g, recv)


def grad_chip_exchange(ps, name):
    n = len(ps)

    def body(*refs):
        p_refs, o_refs = refs[:n], refs[n:2 * n]
        send_sems, recv_sems, local_sems = refs[2 * n:]
        x, y, c = _place()
        chips = _other_chips(x, y)
        me = 2 * x + y
        mine = [pltpu.make_async_copy(p_refs[a].at[me], o_refs[a].at[me], local_sems.at[a]) for a in range(n)]
        sends = [pltpu.make_async_remote_copy(
            src_ref=p_refs[a].at[2 * chip[0] + chip[1]], dst_ref=o_refs[a].at[me], send_sem=send_sems.at[3 * a + j],
            recv_sem=recv_sems.at[3 * a + j], device_id=(*chip, c), device_id_type=MESH)
            for a in range(n) for j, chip in enumerate(chips)]
        for cp in mine + sends:
            cp.start()
        for a in range(n):
            for j, chip in enumerate(chips):
                pltpu.make_async_remote_copy(
                    src_ref=p_refs[a].at[me], dst_ref=o_refs[a].at[2 * chip[0] + chip[1]],
                    send_sem=send_sems.at[3 * a + j], recv_sem=recv_sems.at[3 * a + j], device_id=(*chip, c),
                    device_id_type=MESH).wait_recv()
        for cp in sends:
            cp.wait_send()
        for cp in mine:
            cp.wait()

    return _call(body, name, [jax.ShapeDtypeStruct(p.shape, p.dtype) for p in ps], in_specs=[_ANY] * n,
                 out_specs=[_ANY] * n, scratch=[pltpu.SemaphoreType.DMA((3 * n,)), pltpu.SemaphoreType.DMA((3 * n,)),
                                                pltpu.SemaphoreType.DMA((n,))])(*ps)


def grad_half_swap(ss, name):
    n = len(ss)

    def body(*refs):
        s_refs, o_refs, send_sems, recv_sems = refs[:n], refs[n:2 * n], refs[2 * n], refs[2 * n + 1]
        x, y, c = _place()
        cps = [pltpu.make_async_remote_copy(
            src_ref=s_refs[a], dst_ref=o_refs[a], send_sem=send_sems.at[a], recv_sem=recv_sems.at[a],
            device_id=(x, y, 1 - c), device_id_type=MESH) for a in range(n)]
        for cp in cps:
            cp.start()
        for cp in cps:
            cp.wait()

    return _call(body, name, [jax.ShapeDtypeStruct(s.shape, s.dtype) for s in ss], in_specs=[_ANY] * n,
                 out_specs=[_ANY] * n, scratch=[pltpu.SemaphoreType.DMA((n,)), pltpu.SemaphoreType.DMA((n,))])(*ss)


ROW_TILE = 256

BIG_SHARDS = (("w_in", (D_MODEL, 896), 1), ("s5_w_glu", (128, S5_WIDTH), 0), ("w_proj_a", (S5_WIDTH, 256), 1),
              ("w_proj_b", (SGU_WIDTH, 256), 1), ("w_out", (256, D_MODEL), 0), ("w_up", (D_MODEL, 1408), 1),
              ("w_down", (704, D_MODEL), 0))
SMALL_PARAMS = ("g_mix", "s5_a_re", "s5_a_im", "s5_log_step", "s5_b_re", "s5_b_im", "s5_c_re", "s5_c_im", "s5_d",
                "s5_b_glu", "sgu_ln_g", "sgu_ln_b", "sgu_w", "sgu_b", "b_gate", "g_ffn", "conv_b", "g_final")
PACK_COLS = 1024


def _rows_of(n):
    return -(-n // PACK_COLS)


def _pack_rows(arrays, total_rows, dtype):
    parts = []
    used = 0
    for a in arrays:
        r = _rows_of(a.size)
        parts.append(jnp.pad(a.reshape(-1).astype(dtype), (0, r * PACK_COLS - a.size)).reshape(r, PACK_COLS))
        used += r
    if total_rows > used:
        parts.append(jnp.zeros((total_rows - used, PACK_COLS), dtype))
    return jnp.concatenate(parts, axis=0)


def _unpack_rows(packed, shapes, row0=0):
    out = []
    for s in shapes:
        n = 1
        for d in s:
            n *= d
        r = _rows_of(n)
        out.append(packed[row0:row0 + r].reshape(-1)[:n].reshape(s))
        row0 += r
    return out


def _octet_major(re, im):
    parts = []
    for o in range(OCTETS):
        parts += [re[:, o * 512:(o + 1) * 512], im[:, o * 512:(o + 1) * 512]]
    return jnp.concatenate(parts, axis=1)


def _octet_split(v):
    v = v.reshape(OCTETS, 2, 512)
    return v[:, 0].reshape(N_STATE), v[:, 1].reshape(N_STATE)


def _s5_bmat(bb):
    t = bb.reshape(S5_GROUP, OCTETS, 8, 1, S5_STATE) * jnp.eye(8, dtype=F32)[None, None, :, :, None]
    return jnp.transpose(t, (1, 2, 0, 3, 4)).reshape(OCTETS, LANES, 512)


def _s5_bmat_t(dm):
    t = dm.reshape(OCTETS, 8, S5_GROUP, 8, S5_STATE) * jnp.eye(8, dtype=F32)[None, :, None, :, None]
    return jnp.transpose(t.sum(axis=3), (2, 0, 1, 3)).reshape(S5_GROUP, S5_GROUPS, S5_STATE)


def _s5_cmat(c):
    t = c.reshape(OCTETS, 8, 1, S5_GROUP, S5_STATE) * jnp.eye(8, dtype=F32)[None, :, :, None, None]
    return jnp.transpose(t, (0, 2, 4, 1, 3)).reshape(OCTETS, 512, LANES)


def _s5_cmat_t(dm):
    t = dm.reshape(OCTETS, 8, S5_STATE, 8, S5_GROUP) * jnp.eye(8, dtype=F32)[None, :, None, :, None]
    return jnp.transpose(t.sum(axis=1), (0, 2, 3, 1)).reshape(S5_GROUPS, S5_GROUP, S5_STATE)


def local_step(x, ctx, tgt, mod, modc, W):
    n_tok, n_ctx = x.shape[0], ctx.shape[0]
    tm = ROW_TILE
    D = D_MODEL
    sh1, sc1, ga1, sh2, sc2, ga2 = [mod[:, k * D:(k + 1) * D] for k in range(N_MOD)]
    sh1c, sc1c = modc[:, :D], modc[:, D:2 * D]
    g_mix, g_ffn, g_final = W["g_mix"], W["g_ffn"], W["g_final"]
    w_in = W["w_in"]
    w_in_u = w_in[0][:, :S5_WIDTH]

    h = rowcall(f_modulate, "mod1", n_tok, tm, [(x, D, 0, 0)], [g_mix, sc1, sh1], [(D, BF16)], [])[0]
    hc = rowcall(f_modulate, "mod1_ctx", n_ctx, tm, [(ctx, D, 0, 0)], [g_mix, sc1c, sh1c], [(D, BF16)], [])[0]
    proj = matmul(h, w_in, "nn", BF16, "proj_in", shards=True)
    uc = matmul(hc, w_in_u, "nn", BF16, "proj_in_ctx")
    u_lat = proj[:, :S5_WIDTH]
    u_s5 = (to_strand_order(jnp.concatenate([uc, u_lat], axis=0)), to_strand_order(jnp.concatenate([u_lat, uc], axis=0)))

    a_re, a_im, ls = W["s5_a_re"], W["s5_a_im"], W["s5_log_step"][..., None]
    b_re_t = jnp.transpose(W["s5_b_re"], (0, 3, 1, 2))
    b_im_t = jnp.transpose(W["s5_b_im"], (0, 3, 1, 2))
    bb_re, bb_im = s5_discretize(a_re, a_im, ls, b_re_t, b_im_t)
    ls_rep = jnp.repeat(W["s5_log_step"], S5_STATE, axis=1).reshape(2, 1, N_STATE)
    tabs = s5_tables(a_re.reshape(2, 1, N_STATE), a_im.reshape(2, 1, N_STATE), ls_rep)
    pw, qt, bm, cm = [], [], [], []
    for d in range(2):
        pw.append(tuple(jnp.repeat(_octet_major(tabs[k][d], tabs[k + 1][d]), SUB, axis=0) for k in (0, 2)))
        qt.append(tuple(_octet_major(tabs[k][d], tabs[k + 1][d]) for k in (4, 6)))
        bm.append(jnp.concatenate([_s5_bmat(bb_re[d]), _s5_bmat(bb_im[d])], axis=2).astype(BF16))
        cm.append(jnp.concatenate([_s5_cmat(W["s5_c_re"][d]), -_s5_cmat(W["s5_c_im"][d])], axis=1).astype(BF16))
    y0, cin0 = s5_forward(u_s5[0], bm[0], cm[0], pw[0][0], qt[0][0], False, "s5_fwd0")
    y1, cin1 = s5_forward(u_s5[1], bm[1], cm[1], pw[1][1], qt[1][1], True, "s5_fwd1")

    mix_rows = [(proj, 512, 0, 0), (y0, 512, 0, n_ctx // tm), (y1, 512, 0, 0)] + \
               [(proj, 512, k, 0) for k in range(1, 7)]
    mix_vecs = [W["s5_d"], W["s5_w_glu"], W["s5_b_glu"], W["sgu_ln_g"], W["sgu_ln_b"], W["sgu_w"],
                jnp.transpose(W["sgu_b"]), W["w_proj_a"], W["w_proj_b"], W["b_gate"]]
    mrg = rowcall(f_mixer, "mixer", n_tok, tm, mix_rows, mix_vecs, [(D, BF16)], [])[0]
    o = matmul(mrg, W["w_out"], "nn", F32, "proj_out")
    x1, h2 = rowcall(f_resid_mod, "resid_mod2", n_tok, tm, [(x, D, 0, 0), (o, D, 0, 0)], [ga1, g_ffn, sc2, sh2],
                     [(D, F32), (D, BF16)], [])
    up = matmul(h2, W["w_up"], "nn", BF16, "ffn_up", shards=True)
    conv_w = W["conv_w"].reshape(9, 2 * FFN_HIDDEN)
    wg = jnp.pad(conv_w[:, :FFN_HIDDEN], ((0, 7), (0, 0)))
    wv = jnp.pad(conv_w[:, FFN_HIDDEN:], ((0, 7), (0, 0)))
    act, gate, val = conv_forward(up, wg, wv, W["conv_b"])
    dn = matmul(act, W["w_down"], "nn", F32, "ffn_down")

    def final_fn(x1_, dn_, tgt_, ga2_, gf_):
        loss, (dx1_, ddn_, dga2_, dgf_) = jax.value_and_grad(f_final_loss, argnums=(0, 1, 3, 4))(
            x1_, dn_, tgt_, ga2_, gf_)
        return dx1_, ddn_, loss.reshape(1, 1), dga2_, dgf_

    dx2, ddn, loss, d_ga2, d_gfinal = rowcall(
        final_fn, "final_loss", n_tok, tm, [(x1, D, 0, 0), (dn, D, 0, 0), (tgt, D, 0, 0)], [ga2, g_final],
        [(D, F32), (D, BF16)], [(1, 1), (1, D), (1, D)])

    dact = matmul(ddn, W["w_down"], "nt", BF16, "ffn_down_dx")
    d_w_down = matmul(act, ddn, "tn", F32, "ffn_down_dw")
    dup, dwg, dwv = conv_backward(up, gate, val, dact, wg, wv)
    d_conv_w = jnp.concatenate([dwg[:9], dwv[:9]], axis=1).reshape(3, 3, 2 * FFN_HIDDEN)
    d_conv_b = jnp.concatenate([dwg[9:10], dwv[9:10]], axis=1)
    dh2 = matmul(dup, W["w_up"], "nt", BF16, "ffn_up_dx", shards=True, halves=True)
    d_w_up = matmul(h2, dup, "tn", F32, "ffn_up_dw", shards=True, halves=True)

    def resid_bwd(x_, o_, dx1_, dh2_, ga_, g_, sc_, sh_):
        _, vjp = jax.vjp(f_resid_mod, x_, o_, ga_, g_, sc_, sh_)
        return vjp((dx1_, dh2_))

    dxa, do, d_ga1, d_gffn, d_sc2, d_sh2 = rowcall(
        resid_bwd, "resid_mod2_bwd", n_tok, tm, [(x, D, 0, 0), (o, D, 0, 0), (dx2, D, 0, 0), (dh2, D, 0, 0)],
        [ga1, g_ffn, sc2, sh2], [(D, F32), (D, BF16)], [(1, D)] * 4)

    dmrg = matmul(do, W["w_out"], "nt", BF16, "proj_out_dx")
    d_w_out = matmul(mrg, do, "tn", F32, "proj_out_dw")

    def mixer_bwd(*args):
        rows, dm, vecs = args[:9], args[9], [v.astype(F32) for v in args[10:]]
        _, vjp = jax.vjp(f_mixer, *rows, *vecs)
        g = vjp(dm)
        return (jnp.concatenate([g[0]] + list(g[3:9]), axis=1), g[1]) + tuple(g[9:])

    mb = rowcall(mixer_bwd, "mixer_bwd", n_tok, tm, mix_rows + [(dmrg, D, 0, 0)], mix_vecs,
                 [(w_in.shape[2] * N_CHIPS, BF16), (512, BF16)], [v.shape for v in mix_vecs])
    dproj, dys = mb[:2]
    d_s5d, d_w_glu, d_b_glu, d_ln_g, d_ln_b, d_sgu_w, d_sgu_bt, d_w_pa, d_w_pb, d_b_gate = mb[2:]

    zc = jnp.zeros((n_ctx, S5_WIDTH), BF16)
    dy_s5 = (to_strand_order(jnp.concatenate([zc, dys], axis=0)), to_strand_order(jnp.concatenate([dys, zc], axis=0)))
    du0, dbm0, dcm0, da0 = s5_backward(u_s5[0], dy_s5[0], cin0, bm[0], cm[0], pw[0][0], qt[0][0], pw[0][1], qt[0][1],
                                       False, "s5_bwd0")
    du1, dbm1, dcm1, da1 = s5_backward(u_s5[1], dy_s5[1], cin1, bm[1], cm[1], pw[1][1], qt[1][1], pw[1][0], qt[1][0],
                                       True, "s5_bwd1")
    dproj = add_into_columns(dproj, [(du0, n_ctx // tm), (du1, 0)], S5_WIDTH, "du_sum", tm)
    du_c = rowcall(lambda a, b: a + b, "du_sum_ctx", n_ctx, tm, [(du0, 512, 0, 0), (du1, 512, 0, n_tok // tm)],
                   [], [(512, BF16)], [])[0]

    dab_re, dab_im, dbb_re, dbb_im, d_c_re, d_c_im = [], [], [], [], [], []
    for dbm, dcm, da in ((dbm0, dcm0, da0), (dbm1, dcm1, da1)):
        r, i = _octet_split(da)
        dab_re.append(r.reshape(S5_GROUPS, S5_STATE))
        dab_im.append(i.reshape(S5_GROUPS, S5_STATE))
        dbb_re.append(_s5_bmat_t(dbm[:, :, :512]))
        dbb_im.append(_s5_bmat_t(dbm[:, :, 512:]))
        d_c_re.append(_s5_cmat_t(dcm[:, :512]))
        d_c_im.append(-_s5_cmat_t(dcm[:, 512:]))
    d_a_re, d_a_im, d_ls, d_b_re_t, d_b_im_t = s5_discretize_bwd(
        a_re, a_im, ls, b_re_t, b_im_t, jnp.stack(dab_re), jnp.stack(dab_im), jnp.stack(dbb_re), jnp.stack(dbb_im))

    dh = matmul(dproj, w_in, "nt", BF16, "proj_in_dx", shards=True)
    dhc = matmul(du_c, w_in_u, "nt", BF16, "proj_in_ctx_dx")
    du_c_wide = jnp.pad(du_c, ((0, 0), (0, dproj.shape[1] - S5_WIDTH)))
    d_w_in_c = matmul(hc, du_c_wide, "tn", F32, "proj_in_ctx_dw", shards=True)
    d_w_in = matmul(h, dproj, "tn", F32, "proj_in_dw", shards=True, init=d_w_in_c)

    def mod_bwd_ctx(x_, dh_, g_, sc_, sh_):
        _, vjp = jax.vjp(f_modulate, x_, g_, sc_, sh_)
        return vjp(dh_)[1:]

    d_gmix_c, d_sc1c, d_sh1c = rowcall(mod_bwd_ctx, "mod1_ctx_bwd", n_ctx, tm, [(ctx, D, 0, 0), (dhc, D, 0, 0)],
                                       [g_mix, sc1c, sh1c], [], [(1, D)] * 3)

    def mod_bwd(x_, dh_, dxa_, g_, sc_, sh_):
        _, vjp = jax.vjp(f_modulate, x_, g_, sc_, sh_)
        dx_, dg_, dsc_, dsh_ = vjp(dh_)
        return dx_ + dxa_, dg_, dsc_, dsh_

    zero_d = jnp.zeros((1, D), F32)
    grad_x, d_gmix, d_sc1, d_sh1 = rowcall(
        mod_bwd, "mod1_bwd", n_tok, tm, [(x, D, 0, 0), (dh, D, 0, 0), (dxa, D, 0, 0)], [g_mix, sc1, sh1],
        [(D, F32)], [(1, D)] * 3, ainit=[d_gmix_c, zero_d, zero_d])

    grads = {
        "dmod": jnp.concatenate([d_sh1, d_sc1, d_ga1, d_sh2, d_sc2, d_ga2], axis=1),
        "dmodc": jnp.concatenate([d_sh1c, d_sc1c], axis=1),
        "g_mix": d_gmix,
        "s5_a_re": d_a_re, "s5_a_im": d_a_im, "s5_log_step": d_ls[..., 0],
        "s5_b_re": jnp.transpose(d_b_re_t, (0, 2, 3, 1)), "s5_b_im": jnp.transpose(d_b_im_t, (0, 2, 3, 1)),
        "s5_c_re": jnp.stack(d_c_re), "s5_c_im": jnp.stack(d_c_im), "s5_d": d_s5d, "s5_b_glu": d_b_glu,
        "sgu_ln_g": d_ln_g, "sgu_ln_b": d_ln_b, "sgu_w": d_sgu_w, "sgu_b": jnp.transpose(d_sgu_bt),
        "b_gate": d_b_gate, "g_ffn": d_gffn, "conv_b": d_conv_b, "g_final": d_gfinal, "conv_w": d_conv_w,
        "w_in": d_w_in, "s5_w_glu": d_w_glu, "w_proj_a": d_w_pa, "w_proj_b": d_w_pb, "w_out": d_w_out,
        "w_up": d_w_up, "w_down": d_w_down,
    }
    return loss, grad_x, grads


ADA_COLS = N_MOD * D_MODEL // N_CHIPS
MOD_ROWS = 16


def mod_forward(c16, w, b):
    n = w.shape[1]
    tn = 512

    def body(c_ref, w_ref, b_ref, o_ref):
        cv = c_ref[...]
        cs = cv * jax.nn.sigmoid(cv)
        o_ref[...] = jnp.dot(cs.astype(BF16), w_ref[...].astype(BF16), preferred_element_type=F32) + b_ref[...]

    return _call(body, "mod_forward", jax.ShapeDtypeStruct((MOD_ROWS, n), F32), grid=(n // tn,),
                 in_specs=[pl.BlockSpec((MOD_ROWS, D_MODEL), lambda j: (0, 0)),
                           pl.BlockSpec((D_MODEL, tn), lambda j: (0, j)), pl.BlockSpec((1, tn), lambda j: (0, j))],
                 out_specs=pl.BlockSpec((MOD_ROWS, tn), lambda j: (0, j)), sem=("arbitrary",))(c16, w, b)


def f_ada_outer(ct, dm):
    cs = ct * jax.nn.sigmoid(ct)
    acc = cs[:, 0:1] * dm[0:1]
    for k in range(1, 9):
        acc = acc + cs[:, k:k + 1] * dm[k:k + 1]
    return acc


def f_cctx_grad(z, p4):
    s = jax.nn.sigmoid(z)
    return (p4[0:1] + p4[1:2] + p4[2:3] + p4[3:4]) * (s + z * s * (1.0 - s))


WEIGHT_NAMES = ("c_ctx", "w_ada", "b_ada", "g_mix", "w_in", "s5_a_re", "s5_a_im", "s5_log_step", "s5_b_re",
                "s5_b_im", "s5_c_re", "s5_c_im", "s5_d", "s5_w_glu", "s5_b_glu", "sgu_ln_g", "sgu_ln_b", "sgu_w",
                "sgu_b", "w_proj_a", "w_proj_b", "b_gate", "w_out", "g_ffn", "w_up", "conv_w", "conv_b", "w_down",
                "g_final")
CONV_SHARD = 2 * FFN_HIDDEN // N_CHIPS
SMALL_PACK_ROWS = 512
SMALL_ROW0 = 58


def kernel(x, c, ctx, c_ctx, w_ada, b_ada, g_mix, w_in, s5_a_re, s5_a_im, s5_log_step, s5_b_re, s5_b_im, s5_c_re, s5_c_im, s5_d, s5_w_glu, s5_b_glu, sgu_ln_g, sgu_ln_b, sgu_w, sgu_b, w_proj_a, w_proj_b, b_gate, w_out, g_ffn, w_up, conv_w, conv_b, w_down, g_final, loss_target, m_c_ctx, m_w_ada, m_b_ada, m_g_mix, m_w_in, m_s5_a_re, m_s5_a_im, m_s5_log_step, m_s5_b_re, m_s5_b_im, m_s5_c_re, m_s5_c_im, m_s5_d, m_s5_w_glu, m_s5_b_glu, m_sgu_ln_g, m_sgu_ln_b, m_sgu_w, m_sgu_b, m_w_proj_a, m_w_proj_b, m_b_gate, m_w_out, m_g_ffn, m_w_up, m_conv_w, m_conv_b, m_w_down, m_g_final, v_c_ctx, v_w_ada, v_b_ada, v_g_mix, v_w_in, v_s5_a_re, v_s5_a_im, v_s5_log_step, v_s5_b_re, v_s5_b_im, v_s5_c_re, v_s5_c_im, v_s5_d, v_s5_w_glu, v_s5_b_glu, v_sgu_ln_g, v_sgu_ln_b, v_sgu_w, v_sgu_b, v_w_proj_a, v_w_proj_b, v_b_gate, v_w_out, v_g_ffn, v_w_up, v_conv_w, v_conv_b, v_w_down, v_g_final):
    given = dict(locals())
    wts = {n: given[n] for n in WEIGHT_NAMES}
    ms = {n: given["m_" + n] for n in WEIGHT_NAMES}
    vs = {n: given["v_" + n] for n in WEIGHT_NAMES}
    xi, yi, ci = _place()
    chip = 2 * xi + yi
    dev = 2 * chip + ci
    D = D_MODEL

    c8 = allgather_devices(jnp.pad(c, ((0, 7), (0, 0))), "gather_c")[:, 0, :]
    c16 = jnp.concatenate([c8, c_ctx[None], jnp.zeros((MOD_ROWS - 9, D), F32)], axis=0)
    b_shard = lax.dynamic_slice(b_ada, (0, chip * ADA_COLS), (1, ADA_COLS))
    mod_shard = mod_forward(c16, w_ada[0], b_shard)
    mod_all = allgather_devices(mod_shard, "gather_mod")
    mod_full = jnp.concatenate([mod_all[2 * q] for q in range(N_CHIPS)], axis=1)
    mod = lax.dynamic_slice(mod_full, (dev, 0), (1, N_MOD * D))
    modc = mod_full[8:9]

    big_names = [n for n, _, _ in BIG_SHARDS]
    conv_rows = jnp.pad(conv_w[0].reshape(9, CONV_SHARD), ((0, 7), (0, 0)))
    shards = [wts[n][0].astype(BF16) for n in big_names] + [conv_rows]
    gathered = allgather_chips(shards, "gather_weights")
    gathered = [_fill_own(t, s, chip) for t, s in zip(gathered, shards)]
    W = dict(zip(big_names, gathered[:-1]))
    for n, shape, axis in BIG_SHARDS:
        if axis == 0:
            W[n] = W[n].reshape(N_CHIPS * shape[0], shape[1])
    for n in ("w_proj_a", "w_proj_b"):
        W[n] = jnp.transpose(W[n], (1, 0, 2)).reshape(W[n].shape[1], -1)
    W["conv_w"] = jnp.transpose(gathered[-1][:, :9], (1, 0, 2)).reshape(3, 3, 2 * FFN_HIDDEN)
    for n in ("g_mix", "g_ffn", "s5_d", "s5_b_glu", "sgu_ln_g", "sgu_ln_b", "b_gate", "conv_b"):
        W[n] = wts[n]
    W["g_final"] = g_final[None]
    for n in ("s5_a_re", "s5_a_im", "s5_log_step", "s5_b_re", "s5_b_im", "s5_c_re", "s5_c_im", "sgu_w", "sgu_b"):
        W[n] = wts[n][0]

    loss_part, grad_x, g = local_step(x[0], ctx[0], loss_target[0], mod, modc, W)
    loss = lax.psum(loss_part[0, 0], ("x", "y", "c"))

    g_slots = []
    for n, shape, axis in BIG_SHARDS:
        if n in ("w_proj_a", "w_proj_b"):
            g_slots.append(jnp.transpose(g[n].reshape(shape[0], N_CHIPS, shape[1]), (1, 0, 2)))
        else:
            g_slots.append(g[n].reshape((N_CHIPS,) + shape))
    g_slots = [t.reshape(N_CHIPS, 2, t.shape[1] // 2, t.shape[2]) for t in g_slots]
    core = ci.astype(jnp.int32).reshape(1)
    from_sibling = grad_pair_swap(g_slots, "grad_pair_swap")
    pair = [pair_sum(gs, rv, core, "grad_pair_sum_" + n) for gs, rv, n in zip(g_slots, from_sibling, big_names)]
    from_chips = grad_chip_exchange(pair, "grad_chip_exchange")
    add2 = lambda a, b: a + b
    add4 = lambda a, b, c_, d: ((a + b) + c_) + d
    halves = []
    for fc, n in zip(from_chips, big_names):
        r2, cols = fc.shape[1], fc.shape[2]
        tr = _tile(r2, 256, 8)
        halves.append(rowcall(add4, "grad_chip_sum_" + n, r2, tr,
                              [(fc.reshape(N_CHIPS * r2, cols), cols, 0, q * r2 // tr) for q in range(N_CHIPS)], [],
                              [(cols, F32)], [])[0])
    others = grad_half_swap(halves, "grad_half_swap")
    big_grads = {n: jnp.where(ci == 0, jnp.concatenate([mine, other], axis=0), jnp.concatenate([other, mine], axis=0))
                 for n, mine, other in zip(big_names, halves, others)}

    small_pack = _pack_rows([g["dmod"], g["dmodc"], g["conv_w"]] + [g[n] for n in SMALL_PARAMS], SMALL_PACK_ROWS, F32)
    small_all = _fill_own(allgather_devices(small_pack, "gather_small_grads", copy_own=False), small_pack, dev)
    small_2d = small_all.reshape(N_DEV * SMALL_PACK_ROWS, PACK_COLS)

    def add8(*a):
        s = a[0]
        for t in a[1:]:
            s = s + t
        return s

    small_sum = rowcall(add8, "small_grad_sum", SMALL_PACK_ROWS, 256,
                        [(small_2d, PACK_COLS, 0, k * SMALL_PACK_ROWS // 256) for k in range(N_DEV)], [],
                        [(PACK_COLS, F32)], [])[0]
    dmod_all = small_all[:, 0:N_MOD].reshape(N_DEV, N_MOD * D)
    dmod_sum = small_sum[0:N_MOD].reshape(1, N_MOD * D)
    dmodc_sum = jnp.pad(small_sum[N_MOD:N_MOD + 2].reshape(1, 2 * D), ((0, 0), (0, (N_MOD - 2) * D)))
    conv_grad = _unpack_rows(small_sum, [(3, 3, 2 * FFN_HIDDEN)], row0=N_MOD + 2)[0]
    small_grads = dict(zip(SMALL_PARAMS, _unpack_rows(small_sum, [wts[n].shape for n in SMALL_PARAMS], row0=SMALL_ROW0)))

    dm16 = jnp.concatenate([dmod_all, dmodc_sum, jnp.zeros((MOD_ROWS - 9, N_MOD * D), F32)], axis=0)
    dm_shard = lax.dynamic_slice(dm16, (0, chip * ADA_COLS), (MOD_ROWS, ADA_COLS))
    g_w_ada = rowcall(f_ada_outer, "w_ada_grad", D, 256, [(jnp.transpose(c16), MOD_ROWS, 0, 0)], [dm_shard],
                      [(ADA_COLS, F32)], [])[0]
    g_b_ada = rowcall(add2, "b_ada_grad", 1, 1, [(dmod_sum, N_MOD * D, 0, 0), (dmodc_sum, N_MOD * D, 0, 0)], [],
                      [(N_MOD * D, F32)], [])[0]
    dmc_rows = jnp.pad(dm_shard[8:9], ((0, 7), (0, 0)))
    cctx_part = matmul(dmc_rows, w_ada[0], "nt", F32, "c_ctx_partial")
    cctx_all = allgather_devices(cctx_part, "gather_c_ctx")
    cctx_4 = jnp.stack([cctx_all[2 * q, 0] for q in range(N_CHIPS)])
    g_c_ctx = rowcall(f_cctx_grad, "c_ctx_grad", 1, 1, [(c_ctx[None], D, 0, 0)], [cctx_4], [(D, F32)], [])[0]

    grads = dict(small_grads)
    grads.update(big_grads)
    grads["w_ada"] = g_w_ada
    grads["b_ada"] = g_b_ada
    grads["c_ctx"] = g_c_ctx
    grads["conv_w"] = lax.dynamic_slice(conv_grad, (0, 0, chip * CONV_SHARD), (3, 3, CONV_SHARD))
    grads = {n: grads[n].reshape(wts[n].shape) for n in WEIGHT_NAMES}

    delta, new_m, new_v = {}, {}, {}
    for n in WEIGHT_NAMES:
        shape2d = (-1, wts[n].shape[-1])
        d_, m_, v_ = adamw(wts[n].reshape(shape2d), grads[n].reshape(shape2d), ms[n].reshape(shape2d),
                           vs[n].reshape(shape2d), "adamw_" + n)
        delta[n], new_m[n], new_v[n] = [t.reshape(wts[n].shape) for t in (d_, m_, v_)]

    return (loss, grad_x[None], *[grads[n] for n in WEIGHT_NAMES], *[delta[n] for n in WEIGHT_NAMES],
            *[new_m[n] for n in WEIGHT_NAMES], *[new_v[n] for n in WEIGHT_NAMES])
```

```python
import functools

import jax
import jax.numpy as jnp
from jax import lax
from jax.experimental import pallas as pl
from jax.experimental.pallas import tpu as pltpu

F32, BF16 = jnp.float32, jnp.bfloat16
MESH = pl.DeviceIdType.MESH

D_MODEL = 1024
S5_WIDTH = 512
S5_GROUP = 16
S5_GROUPS = 32
S5_STATE = 64
SGU_WIDTH = 512
SGU_GROUPS = 8
CHUNK = 128
FFN_HIDDEN = 2816
GRID_W = 64
N_MOD = 6
EPS = 1e-6
N_STATE = S5_GROUPS * S5_STATE
OCTETS = 4
SCAN_T = 128
N_CHIPS = 4
N_DEV = 8
LANES = 128
VMEM_LIMIT_BYTES = 56 * 1024 * 1024
CONV_PAD = 72
CONV_ROWS = 256

ADAM_LR, ADAM_B1, ADAM_B2, ADAM_EPS, ADAM_WD, ADAM_STEP = 0.001, 0.9, 0.999, 1e-08, 0.01, 10


def _call(body, name, out_shape, grid=None, in_specs=None, out_specs=None, scratch=(), sem=None, **kw):
    params = pltpu.CompilerParams(dimension_semantics=sem, vmem_limit_bytes=VMEM_LIMIT_BYTES)
    extra = {} if grid is None else {"grid": grid}
    return pl.pallas_call(body, name=name, out_shape=out_shape, in_specs=in_specs, out_specs=out_specs,
                          scratch_shapes=list(scratch), compiler_params=params, **extra, **kw)


def _tile(n, target, mult=LANES):
    best = None
    t = mult
    while t <= min(n, target):
        if n % t == 0:
            best = t
        t += mult
    return best or n


@jax.custom_vjp
def mmul(a, b):
    return jnp.dot(a.astype(BF16), b.astype(BF16), preferred_element_type=F32)


def _mmul_fwd(a, b):
    return mmul(a, b), (a, b)


def _mmul_bwd(res, ct):
    a, b = res
    ctb = ct.astype(BF16)
    da = lax.dot_general(ctb, b.astype(BF16), (((1,), (1,)), ((), ())), preferred_element_type=F32)
    db = lax.dot_general(a.astype(BF16), ctb, (((0,), (0,)), ((), ())), preferred_element_type=F32)
    return da.astype(a.dtype), db.astype(b.dtype)


mmul.defvjp(_mmul_fwd, _mmul_bwd)

_DOT_DIMS = {"nn": ((1,), (0,)), "nt": ((1,), (1,)), "tn": ((0,), (0,))}


MM_TILE = 1408
MM_FULL_K = 2048


def matmul(a, b, mode, out_dtype, name, init=None, shards=False, halves=False):
    if mode == "nn":
        (M, K), N = a.shape, (b.shape[2] * N_CHIPS if shards else b.shape[1])
    elif mode == "nt":
        M, N, K = a.shape[-2], b.shape[-2], a.shape[-1] * (2 if halves else 1)
    else:
        (K, M), N = a.shape, b.shape[-1] * (2 if halves else 1)
    ns = (K if mode == "nt" else N) // N_CHIPS
    tm = _tile(M, MM_TILE, 8 if M < LANES else LANES)
    tn = _tile(ns if shards and mode != "nt" else N, MM_TILE)
    if shards and mode == "nt":
        tk = _tile(ns, MM_TILE)
    else:
        tk = K if K <= MM_FULL_K else _tile(K, MM_TILE)
    nk = K // tk
    per = ns // (tk if mode == "nt" else tn)
    dims = (_DOT_DIMS[mode], ((), ()))
    has_init = init is not None
    use_acc = nk > 1 and out_dtype != F32

    def body(*refs):
        a_ref, b_ref = refs[:2]
        i_ref = refs[2] if has_init else None
        o_ref = refs[3] if has_init else refs[2]
        acc = refs[-1] if use_acc else o_ref
        k = pl.program_id(2)
        part = lax.dot_general(a_ref[...].astype(BF16), b_ref[...].astype(BF16), dims, preferred_element_type=F32)

        @pl.when(k == 0)
        def _():
            first = part + i_ref[...].astype(F32) if has_init else part
            acc[...] = first.astype(acc.dtype)

        if nk > 1:
            @pl.when(k > 0)
            def _():
                acc[...] += part

        if use_acc:
            @pl.when(k == nk - 1)
            def _():
                o_ref[...] = acc[...].astype(o_ref.dtype)

    if mode == "tn":
        a_spec = pl.BlockSpec((tk, tm), lambda i, j, k: (k, i))
    elif halves:
        ph = K // 2 // tk
        a_spec = pl.BlockSpec((None, tm, tk), lambda i, j, k: (k // ph, i, k % ph))
    else:
        a_spec = pl.BlockSpec((tm, tk), lambda i, j, k: (i, k))
    if mode == "nt":
        b_spec = (pl.BlockSpec((None, tn, tk), lambda i, j, k: (k // per, j, k % per)) if shards
                  else pl.BlockSpec((tn, tk), lambda i, j, k: (j, k)))
    elif mode == "tn" and halves:
        ph = N // 2 // tn
        b_spec = pl.BlockSpec((None, tk, tn), lambda i, j, k: (j // ph, k, j % ph))
    else:
        b_spec = (pl.BlockSpec((None, tk, tn), lambda i, j, k: (j // per, k, j % per)) if shards and mode == "nn"
                  else pl.BlockSpec((tk, tn), lambda i, j, k: (k, j)))
    if shards and mode == "tn":
        o_spec = pl.BlockSpec((None, tm, tn), lambda i, j, k: (j // per, i, j % per))
        out_shape = jax.ShapeDtypeStruct((N_CHIPS, M, ns), out_dtype)
    else:
        o_spec = pl.BlockSpec((tm, tn), lambda i, j, k: (i, j))
        out_shape = jax.ShapeDtypeStruct((M, N), out_dtype)
    in_specs = [a_spec, b_spec] + ([o_spec] if has_init else [])
    args = (a, b) + ((init,) if has_init else ())
    return _call(body, name, out_shape, grid=(M // tm, N // tn, nk),
                 in_specs=in_specs, out_specs=o_spec, scratch=[pltpu.VMEM((tm, tn), F32)] if use_acc else [],
                 sem=("parallel", "parallel", "arbitrary"))(*args)


def rowcall(fn, name, nrows, tm, rins, vins, routs, aouts, ainit=None):
    n_r, n_v, n_ro = len(rins), len(vins), len(routs)
    n_i = len(aouts) if ainit is not None else 0

    def body(*refs):
        r_in, v_in, i_in = refs[:n_r], refs[n_r:n_r + n_v], refs[n_r + n_v:n_r + n_v + n_i]
        r_out, a_out = refs[n_r + n_v + n_i:n_r + n_v + n_i + n_ro], refs[n_r + n_v + n_i + n_ro:]
        outs = fn(*[r[...].astype(F32) for r in r_in], *[v[...] for v in v_in])
        if not isinstance(outs, (tuple, list)):
            outs = (outs,)
        for ref, val in zip(r_out, outs[:n_ro]):
            ref[...] = val.astype(ref.dtype)
        if a_out:
            @pl.when(pl.program_id(0) == 0)
            def _():
                for k, ref in enumerate(a_out):
                    ref[...] = i_in[k][...] if n_i else jnp.zeros_like(ref)

            for ref, val in zip(a_out, outs[n_ro:]):
                ref[...] += val.astype(F32)

    def rspec(width, cblk, roff):
        return pl.BlockSpec((tm, width), lambda i: (i + roff, cblk))

    def whole(shape):
        nd = len(shape)
        return pl.BlockSpec(tuple(shape), lambda i: (0,) * nd)

    inits = list(ainit) if n_i else []
    in_specs = [rspec(w, cb, ro) for (_, w, cb, ro) in rins] + [whole(v.shape) for v in vins + inits]
    out_specs = [rspec(w, 0, 0) for (w, _) in routs] + [whole(s) for s in aouts]
    out_shape = [jax.ShapeDtypeStruct((nrows, w), dt) for (w, dt) in routs] + \
                [jax.ShapeDtypeStruct(tuple(s), F32) for s in aouts]
    res = _call(body, name, out_shape, grid=(nrows // tm,), in_specs=in_specs, out_specs=out_specs,
                sem=("arbitrary",))(*[r[0] for r in rins], *vins, *inits)
    return res


def add_into_columns(buf, parts, width, name, tm):
    n = len(parts)

    def body(*refs):
        acc = refs[0][...].astype(F32)
        for p_ref in refs[1:1 + n]:
            acc = acc + p_ref[...].astype(F32)
        refs[1 + n][...] = acc.astype(refs[1 + n].dtype)

    def rows_at(roff):
        return pl.BlockSpec((tm, width), lambda i: (i + roff, 0))

    return _call(body, name, jax.ShapeDtypeStruct(buf.shape, buf.dtype), grid=(buf.shape[0] // tm,),
                 in_specs=[rows_at(0)] + [rows_at(ro) for _, ro in parts], out_specs=rows_at(0), sem=("arbitrary",),
                 input_output_aliases={0: 0})(buf, *[p for p, _ in parts])


def _rms(x):
    return lax.rsqrt(jnp.mean(x * x, axis=-1, keepdims=True) + EPS)


def f_modulate(x, g, sc, sh):
    return (x * _rms(x)) * g * (1.0 + sc) + sh


def f_resid_mod(x, o, ga, g, sc, sh):
    x1 = x + ga * o
    return x1, f_modulate(x1, g, sc, sh)


def f_final_loss(x1, dn, tgt, ga2, gf):
    x2 = x1 + ga2 * dn
    y = (x2 * _rms(x2)) * gf
    err = jnp.square(y - tgt)
    return 0.5 * jnp.sum(jnp.mean(err, axis=-1))


def _sgu_spatial(vn, w, bt):
    lo = lax.broadcasted_iota(jnp.int32, (1, LANES), 1) < (SGU_WIDTH // SGU_GROUPS)
    row_blocks = []
    for r in range(vn.shape[0] // CHUNK):
        rows = vn[r * CHUNK:(r + 1) * CHUNK]
        cols = []
        for j in range(SGU_WIDTH // LANES):
            blk = rows[:, j * LANES:(j + 1) * LANES]
            v_lo = jnp.where(lo, blk, 0.0)
            v_hi = jnp.where(lo, 0.0, blk)
            s = mmul(w[2 * j], v_lo) + mmul(w[2 * j + 1], v_hi)
            bias = jnp.where(lo, bt[:, 2 * j:2 * j + 1], bt[:, 2 * j + 1:2 * j + 2])
            cols.append(s + bias)
        row_blocks.append(jnp.concatenate(cols, axis=1))
    return jnp.concatenate(row_blocks, axis=0) if len(row_blocks) > 1 else row_blocks[0]


def f_mixer(u_a, y0, y1, zu, zv, ga0, ga1, gb0, gb1, d_skip, w_glu, b_glu, ln_g, ln_b, sgu_w, sgu_bt,
            w_pa, w_pb, b_gate):
    ys = u_a * d_skip + y0 + y1
    ge = jax.nn.gelu(ys)
    y_a = ge * jax.nn.sigmoid(mmul(ge, w_glu) + b_glu)
    u_sg = jax.nn.gelu(zu)
    v = jax.nn.gelu(zv)
    vc = v - jnp.mean(v, axis=-1, keepdims=True)
    vn = (vc * lax.rsqrt(jnp.mean(vc * vc, axis=-1, keepdims=True) + EPS)) * ln_g + ln_b
    y_b = u_sg * _sgu_spatial(vn, sgu_w, sgu_bt)
    gl_a = jnp.concatenate([ga0, ga1], axis=1) + b_gate[:, :D_MODEL]
    gl_b = jnp.concatenate([gb0, gb1], axis=1) + b_gate[:, D_MODEL:]
    return jax.nn.sigmoid(gl_a) * mmul(y_a, w_pa) + jax.nn.sigmoid(gl_b) * mmul(y_b, w_pb)


def _cmul(ar, ai, xr, xi):
    return ar * xr - ai * xi, ar * xi + ai * xr


SUB = 8
STRAND = SCAN_T // SUB


def to_strand_order(v):
    n = v.shape[0] // SCAN_T
    return jnp.transpose(v.reshape(n, SUB, STRAND, v.shape[1]), (0, 2, 1, 3)).reshape(v.shape)


def _to_token_order(v):
    i = lax.broadcasted_iota(jnp.int32, (SCAN_T, SCAN_T), 0)
    j = lax.broadcasted_iota(jnp.int32, (SCAN_T, SCAN_T), 1)
    perm = jnp.where(i == STRAND * (j % SUB) + j // SUB, 1.0, 0.0).astype(BF16)
    hi = v.astype(BF16)
    lo = (v - hi.astype(F32)).astype(BF16)
    return jnp.dot(perm, hi, preferred_element_type=F32) + jnp.dot(perm, lo, preferred_element_type=F32)


def _scan_strands(xr, xi, pw_ref, q_ref, col, rev, conj, cr, ci):
    def tab(ref, lo):
        t_r = ref[lo:lo + SUB, col:col + LANES]
        t_i = ref[lo:lo + SUB, col + 512:col + 512 + LANES]
        return t_r, (-t_i if conj else t_i)

    a_r, a_i = tab(pw_ref, (STRAND - 1) * SUB if rev else 0)
    order = list(range(STRAND - 1, -1, -1) if rev else range(STRAND))
    lr, li = [None] * STRAND, [None] * STRAND
    for n, k in enumerate(order):
        lr[k], li[k] = xr[k * SUB:(k + 1) * SUB], xi[k * SUB:(k + 1) * SUB]
        if n:
            m_r, m_i = _cmul(a_r, a_i, lr[order[n - 1]], li[order[n - 1]])
            lr[k], li[k] = lr[k] + m_r, li[k] + m_i
    f_r, f_i = lr[order[-1]], li[order[-1]]
    q_r, q_i = tab(q_ref, 0)
    sub = lax.broadcasted_iota(jnp.int32, (SUB, 1), 0)
    s = 1
    while s < SUB:
        row = (SUB - s) if rev else (s - 1)
        shift = (SUB - s) if rev else s
        m = (sub < SUB - s) if rev else (sub >= s)
        p_r, p_i = _cmul(q_r[row:row + 1], q_i[row:row + 1], pltpu.roll(f_r, shift, 0), pltpu.roll(f_i, shift, 0))
        f_r, f_i = f_r + jnp.where(m, p_r, 0.0), f_i + jnp.where(m, p_i, 0.0)
        s *= 2
    c_r, c_i = jnp.broadcast_to(cr, (SUB, LANES)), jnp.broadcast_to(ci, (SUB, LANES))
    p_r, p_i = _cmul(q_r, q_i, c_r, c_i)
    s_r, s_i = f_r + p_r, f_i + p_i
    edge = 0 if rev else SUB - 1
    first = sub == (SUB - 1 if rev else 0)
    e_r = jnp.where(first, c_r, pltpu.roll(s_r, SUB - 1 if rev else 1, 0))
    e_i = jnp.where(first, c_i, pltpu.roll(s_i, SUB - 1 if rev else 1, 0))
    for k in range(STRAND):
        t_r, t_i = tab(pw_ref, k * SUB)
        p_r, p_i = _cmul(t_r, t_i, e_r, e_i)
        lr[k], li[k] = lr[k] + p_r, li[k] + p_i
    return lr, li, (s_r[edge:edge + 1], s_i[edge:edge + 1]), (e_r, e_i)


SCAN_GROUP = 6


def _lane_cols(o, j):
    col = o * 1024 + j * LANES
    return col, slice(col, col + LANES), slice(col + 512, col + 512 + LANES)


def s5_forward(u, bm, cm, pw, q, rev, name):
    T, G = SCAN_T, SCAN_GROUP
    n_chunks = u.shape[0] // T
    n_steps = n_chunks // G
    order = list(range(G - 1, -1, -1) if rev else range(G))

    def pos(i):
        return (n_steps - 1 - i) if rev else i

    def body(u_ref, bm_ref, cm_ref, pw_ref, q_ref, y_ref, cin_ref, carry):
        @pl.when(pl.program_id(0) == 0)
        def _():
            carry[...] = jnp.zeros_like(carry)

        uv = u_ref[...]
        for o in range(OCTETS):
            bu = jnp.dot(uv[:, o * LANES:(o + 1) * LANES], bm_ref[o], preferred_element_type=F32)
            hr = [[None] * 4 for _ in range(G)]
            hi = [[None] * 4 for _ in range(G)]
            for j in range(4):
                col, sl_r, sl_i = _lane_cols(o, j)
                cr, ci = carry[0:1, sl_r], carry[0:1, sl_i]
                for g in order:
                    rows = slice(g * T, (g + 1) * T)
                    cin_ref[g, 0:1, sl_r] = cr
                    cin_ref[g, 0:1, sl_i] = ci
                    xr, xi, (cr, ci), _ = _scan_strands(
                        bu[rows, j * LANES:(j + 1) * LANES], bu[rows, 512 + j * LANES:512 + (j + 1) * LANES],
                        pw_ref, q_ref, col, rev, False, cr, ci)
                    hr[g][j] = jnp.concatenate(xr, axis=0)
                    hi[g][j] = jnp.concatenate(xi, axis=0)
                carry[0:1, sl_r] = cr
                carry[0:1, sl_i] = ci
            h = jnp.concatenate([jnp.concatenate(hr[g] + hi[g], axis=1) for g in range(G)], axis=0).astype(BF16)
            y = jnp.dot(h, cm_ref[o], preferred_element_type=F32)
            for g in range(G):
                y_ref[g * T:(g + 1) * T, o * LANES:(o + 1) * LANES] = _to_token_order(y[g * T:(g + 1) * T])

    whole3 = lambda s: pl.BlockSpec(s, lambda i: (0, 0, 0))
    whole2 = lambda s: pl.BlockSpec(s, lambda i: (0, 0))
    rows_spec = pl.BlockSpec((G * T, S5_WIDTH), lambda i: (pos(i), 0))
    return _call(
        body, name,
        [jax.ShapeDtypeStruct((n_chunks * T, S5_WIDTH), F32), jax.ShapeDtypeStruct((n_chunks, 1, 2 * N_STATE), F32)],
        grid=(n_steps,),
        in_specs=[rows_spec, whole3(bm.shape), whole3(cm.shape), whole2(pw.shape), whole2(q.shape)],
        out_specs=[rows_spec, pl.BlockSpec((G, 1, 2 * N_STATE), lambda i: (pos(i), 0, 0))],
        scratch=[pltpu.VMEM((1, 2 * N_STATE), F32)], sem=("arbitrary",))(u, bm, cm, pw, q)


def s5_backward(u, dy, cin, bm, cm, pw_h, q_h, pw_l, q_l, rev, name):
    T, G = SCAN_T, SCAN_GROUP
    n_chunks = u.shape[0] // T
    n_steps = n_chunks // G
    adjoint_order = list(range(G) if rev else range(G - 1, -1, -1))

    def pos(i):
        return i if rev else (n_steps - 1 - i)

    def body(u_ref, dy_ref, cin_ref, bm_ref, cm_ref, pwh_ref, qh_ref, pwl_ref, ql_ref, du_ref, dbm_ref, dcm_ref,
             da_ref, lcarry):
        @pl.when(pl.program_id(0) == 0)
        def _():
            lcarry[...] = jnp.zeros_like(lcarry)
            dbm_ref[...] = jnp.zeros_like(dbm_ref)
            dcm_ref[...] = jnp.zeros_like(dcm_ref)
            da_ref[...] = jnp.zeros_like(da_ref)

        uv = u_ref[...]
        dyv = dy_ref[...]
        for o in range(OCTETS):
            u_o = uv[:, o * LANES:(o + 1) * LANES]
            dy_o = dyv[:, o * LANES:(o + 1) * LANES]
            bu = jnp.dot(u_o, bm_ref[o], preferred_element_type=F32)
            gy = lax.dot_general(dy_o, cm_ref[o], (((1,), (1,)), ((), ())), preferred_element_type=F32)
            hs = [[None] * 8 for _ in range(G)]
            ls = [[None] * 8 for _ in range(G)]
            for j in range(4):
                col, sl_r, sl_i = _lane_cols(o, j)
                b_r = slice(j * LANES, (j + 1) * LANES)
                b_i = slice(512 + j * LANES, 512 + (j + 1) * LANES)
                l_r, l_i = lcarry[0:1, sl_r], lcarry[0:1, sl_i]
                acc_r = acc_i = None
                for g in adjoint_order:
                    rows = slice(g * T, (g + 1) * T)
                    xr, xi, _, (e_r, e_i) = _scan_strands(bu[rows, b_r], bu[rows, b_i], pwh_ref, qh_ref, col, rev, False,
                                                          cin_ref[g, 0:1, sl_r], cin_ref[g, 0:1, sl_i])
                    ar_, ai_, (l_r, l_i), _ = _scan_strands(gy[rows, b_r], gy[rows, b_i], pwl_ref, ql_ref, col, not rev,
                                                            True, l_r, l_i)
                    for k in range(STRAND):
                        kp = k + 1 if rev else k - 1
                        p_r, p_i = (e_r, e_i) if not 0 <= kp < STRAND else (xr[kp], xi[kp])
                        t_r = ar_[k] * p_r + ai_[k] * p_i
                        t_i = ai_[k] * p_r - ar_[k] * p_i
                        acc_r, acc_i = (t_r, t_i) if acc_r is None else (acc_r + t_r, acc_i + t_i)
                    hs[g][j], hs[g][4 + j] = jnp.concatenate(xr, axis=0), jnp.concatenate(xi, axis=0)
                    ls[g][j], ls[g][4 + j] = jnp.concatenate(ar_, axis=0), jnp.concatenate(ai_, axis=0)
                lcarry[0:1, sl_r] = l_r
                lcarry[0:1, sl_i] = l_i
                da_ref[0:1, sl_r] += jnp.sum(acc_r, axis=0, keepdims=True)
                da_ref[0:1, sl_i] += jnp.sum(acc_i, axis=0, keepdims=True)
            h = jnp.concatenate([jnp.concatenate(hs[g], axis=1) for g in range(G)], axis=0).astype(BF16)
            lam = jnp.concatenate([jnp.concatenate(ls[g], axis=1) for g in range(G)], axis=0).astype(BF16)
            du = lax.dot_general(lam, bm_ref[o], (((1,), (1,)), ((), ())), preferred_element_type=F32)
            for g in range(G):
                du_ref[g * T:(g + 1) * T, o * LANES:(o + 1) * LANES] = _to_token_order(du[g * T:(g + 1) * T])
            dbm_ref[o] += lax.dot_general(u_o, lam, (((0,), (0,)), ((), ())), preferred_element_type=F32)
            dcm_ref[o] += lax.dot_general(h, dy_o, (((0,), (0,)), ((), ())), preferred_element_type=F32)

    whole3 = lambda s: pl.BlockSpec(s, lambda i: (0, 0, 0))
    whole2 = lambda s: pl.BlockSpec(s, lambda i: (0, 0))
    rows_spec = pl.BlockSpec((G * T, S5_WIDTH), lambda i: (pos(i), 0))
    return _call(
        body, name,
        [jax.ShapeDtypeStruct((n_chunks * T, S5_WIDTH), F32), jax.ShapeDtypeStruct(bm.shape, F32),
         jax.ShapeDtypeStruct(cm.shape, F32), jax.ShapeDtypeStruct((1, 2 * N_STATE), F32)],
        grid=(n_steps,),
        in_specs=[rows_spec, rows_spec, pl.BlockSpec((G, 1, 2 * N_STATE), lambda i: (pos(i), 0, 0)),
                  whole3(bm.shape), whole3(cm.shape), whole2(pw_h.shape), whole2(q_h.shape), whole2(pw_l.shape),
                  whole2(q_l.shape)],
        out_specs=[rows_spec, whole3(bm.shape), whole3(cm.shape), whole2((1, 2 * N_STATE))],
        scratch=[pltpu.VMEM((1, 2 * N_STATE), F32)], sem=("arbitrary",))(u, dy, cin, bm, cm, pw_h, q_h, pw_l, q_l)


def s5_tables(ar, ai, ls):
    def body(ar_ref, ai_ref, ls_ref, *outs):
        dt = jnp.exp(ls_ref[...])
        k = 0
        for n_rows, step in ((STRAND, 1.0), (SUB, float(STRAND))):
            row = lax.broadcasted_iota(jnp.int32, (n_rows, 1), 0)
            for m_int in (row + 1, n_rows - row):
                m = m_int.astype(F32) * step
                mag = jnp.exp(m * (ar_ref[...] * dt))
                ang = m * (ai_ref[...] * dt)
                outs[k][...] = mag * jnp.cos(ang)
                outs[k + 1][...] = mag * jnp.sin(ang)
                k += 2

    vec = pl.BlockSpec((None, 1, N_STATE), lambda d: (d, 0, 0))
    tab = lambda n: pl.BlockSpec((None, n, N_STATE), lambda d: (d, 0, 0))
    shp = lambda n: jax.ShapeDtypeStruct((2, n, N_STATE), F32)
    sizes = [STRAND] * 4 + [SUB] * 4
    return _call(body, "s5_tables", [shp(n) for n in sizes], grid=(2,), in_specs=[vec, vec, vec],
                 out_specs=[tab(n) for n in sizes], sem=("arbitrary",))(ar, ai, ls)


def f_discretize(a_re, a_im, ls, b_re, b_im):
    dt = jnp.exp(ls)
    mag = jnp.exp(a_re * dt)
    ab_re = mag * jnp.cos(a_im * dt)
    ab_im = mag * jnp.sin(a_im * dt)
    p = ab_re - 1.0
    q = ab_im
    den = a_re * a_re + a_im * a_im
    k_re = ((p * a_re + q * a_im) / den)[None]
    k_im = ((q * a_re - p * a_im) / den)[None]
    return ab_re, ab_im, k_re * b_re - k_im * b_im, k_re * b_im + k_im * b_re


def _disc_specs():
    a = pl.BlockSpec((None, S5_GROUPS, S5_STATE), lambda d: (d, 0, 0))
    s = pl.BlockSpec((None, S5_GROUPS, 1), lambda d: (d, 0, 0))
    b = pl.BlockSpec((None, S5_GROUP, S5_GROUPS, S5_STATE), lambda d: (d, 0, 0, 0))
    return a, s, b


def s5_discretize(a_re, a_im, ls, b_re, b_im):
    def body(ar, ai, l, br, bi, obr, obi):
        _, _, r, i = f_discretize(ar[...], ai[...], l[...], br[...], bi[...])
        obr[...] = r
        obi[...] = i

    a, s, b = _disc_specs()
    return _call(body, "s5_discretize", [jax.ShapeDtypeStruct(b_re.shape, F32)] * 2, grid=(2,),
                 in_specs=[a, a, s, b, b], out_specs=[b, b], sem=("arbitrary",))(a_re, a_im, ls, b_re, b_im)


def s5_discretize_bwd(a_re, a_im, ls, b_re, b_im, dab_re, dab_im, dbb_re, dbb_im):
    def body(ar, ai, l, br, bi, c0, c1, c2, c3, o0, o1, o2, o3, o4):
        _, vjp = jax.vjp(f_discretize, ar[...], ai[...], l[...], br[...], bi[...])
        outs = vjp((c0[...], c1[...], c2[...], c3[...]))
        for ref, val in zip((o0, o1, o2, o3, o4), outs):
            ref[...] = val

    a, s, b = _disc_specs()
    shapes = [jax.ShapeDtypeStruct(t.shape, F32) for t in (a_re, a_im, ls, b_re, b_im)]
    return _call(body, "s5_discretize_bwd", shapes, grid=(2,), in_specs=[a, a, s, b, b, a, a, b, b],
                 out_specs=[a, a, s, b, b], sem=("arbitrary",))(a_re, a_im, ls, b_re, b_im, dab_re, dab_im,
                                                                 dbb_re, dbb_im)


def _conv_taps(s_ref, base, n_rows):
    n = n_rows + 2 * CONV_PAD
    ext = s_ref[pl.ds(base, n), :]
    col = (lax.broadcasted_iota(jnp.int32, (n, 1), 0) + (2 * GRID_W - CONV_PAD)) % GRID_W
    left = jnp.where(col == 0, 0.0, pltpu.roll(ext, 1, 0))
    right = jnp.where(col == GRID_W - 1, 0.0, pltpu.roll(ext, n - 1, 0))
    return left, ext, right


def _conv_apply(taps, w_ref, n_rows, flip):
    out = None
    for i in range(3):
        wi = 2 - i if flip else i
        start = CONV_PAD + (i - 1) * GRID_W
        for j in range(3):
            wj = 2 - j if flip else j
            term = w_ref[wi * 3 + wj:wi * 3 + wj + 1, :] * taps[j][start:start + n_rows]
            out = term if out is None else out + term
    return out


def _conv_fill(dst_ref, src_ref, n_tok):
    zeros = jnp.zeros((CONV_PAD, LANES), F32)
    dst_ref[0:CONV_PAD, :] = zeros
    dst_ref[CONV_PAD + n_tok:2 * CONV_PAD + n_tok, :] = zeros

    def step(r, carry):
        base = pl.multiple_of(r * CONV_ROWS, CONV_ROWS)
        dst_ref[pl.ds(base + CONV_PAD, CONV_ROWS), :] = src_ref[pl.ds(base, CONV_ROWS), :].astype(F32)
        return carry

    lax.fori_loop(0, n_tok // CONV_ROWS, step, 0)


def conv_forward(up, wg, wv, bias):
    n_tok = up.shape[0]
    nb = FFN_HIDDEN // LANES

    def body(ug_ref, uv_ref, wg_ref, wv_ref, bg_ref, bv_ref, act_ref, gate_ref, val_ref, sg, sv):
        _conv_fill(sg, ug_ref, n_tok)
        _conv_fill(sv, uv_ref, n_tok)

        def step(r, carry):
            base = pl.multiple_of(r * CONV_ROWS, CONV_ROWS)
            gate = _conv_apply(_conv_taps(sg, base, CONV_ROWS), wg_ref, CONV_ROWS, False) + bg_ref[...]
            val = _conv_apply(_conv_taps(sv, base, CONV_ROWS), wv_ref, CONV_ROWS, False) + bv_ref[...]
            act_ref[pl.ds(base, CONV_ROWS), :] = (gate * jax.nn.sigmoid(gate) * val).astype(BF16)
            gate_ref[pl.ds(base, CONV_ROWS), :] = gate.astype(BF16)
            val_ref[pl.ds(base, CONV_ROWS), :] = val.astype(BF16)
            return carry

        lax.fori_loop(0, n_tok // CONV_ROWS, step, 0)

    col = lambda off: pl.BlockSpec((n_tok, LANES), lambda k: (0, k + off))
    wsp = lambda off: pl.BlockSpec((16, LANES), lambda k: (0, k + off))
    bsp = lambda off: pl.BlockSpec((1, LANES), lambda k: (0, k + off))
    pad = pltpu.VMEM((n_tok + 2 * CONV_PAD, LANES), F32)
    half = jax.ShapeDtypeStruct((n_tok, FFN_HIDDEN), BF16)
    return _call(body, "conv_forward", [half, half, half], grid=(nb,),
                 in_specs=[col(0), col(nb), wsp(0), wsp(0), bsp(0), bsp(nb)], out_specs=[col(0), col(0), col(0)],
                 scratch=[pad, pad], sem=("arbitrary",))(up, up, wg, wv, bias, bias)


def conv_backward(up, gate, val, dact, wg, wv):
    n_tok = up.shape[0]
    nb = FFN_HIDDEN // LANES
    n_steps = n_tok // CONV_ROWS

    def body(ug_ref, uv_ref, gate_ref, val_ref, da_ref, wg_ref, wv_ref, dup_ref, dwg_ref, dwv_ref, sg, sv, sdg, sdv):
        _conv_fill(sg, ug_ref, n_tok)
        _conv_fill(sv, uv_ref, n_tok)
        zeros = jnp.zeros((CONV_PAD, LANES), F32)
        for s_ref in (sdg, sdv):
            s_ref[0:CONV_PAD, :] = zeros
            s_ref[CONV_PAD + n_tok:2 * CONV_PAD + n_tok, :] = zeros
        dwg_ref[...] = jnp.zeros_like(dwg_ref)
        dwv_ref[...] = jnp.zeros_like(dwv_ref)

        def grads(r, carry):
            base = pl.multiple_of(r * CONV_ROWS, CONV_ROWS)
            taps_g = _conv_taps(sg, base, CONV_ROWS)
            taps_v = _conv_taps(sv, base, CONV_ROWS)
            gate = gate_ref[pl.ds(base, CONV_ROWS), :].astype(F32)
            val = val_ref[pl.ds(base, CONV_ROWS), :].astype(F32)
            d_act = da_ref[pl.ds(base, CONV_ROWS), :].astype(F32)
            sig = jax.nn.sigmoid(gate)
            d_gate = d_act * val * (sig * (1.0 + gate * (1.0 - sig)))
            d_val = d_act * (gate * sig)
            sdg[pl.ds(base + CONV_PAD, CONV_ROWS), :] = d_gate
            sdv[pl.ds(base + CONV_PAD, CONV_ROWS), :] = d_val
            for d_out, taps, dw_ref in ((d_gate, taps_g, dwg_ref), (d_val, taps_v, dwv_ref)):
                for i in range(3):
                    start = CONV_PAD + (i - 1) * GRID_W
                    for j in range(3):
                        k = i * 3 + j
                        dw_ref[k:k + 1, :] += jnp.sum(d_out * taps[j][start:start + CONV_ROWS], axis=0, keepdims=True)
                dw_ref[9:10, :] += jnp.sum(d_out, axis=0, keepdims=True)
            return carry

        lax.fori_loop(0, n_steps, grads, 0)

        def spread(r, carry):
            base = pl.multiple_of(r * CONV_ROWS, CONV_ROWS)
            dup_ref[0, pl.ds(base, CONV_ROWS), :] = _conv_apply(
                _conv_taps(sdg, base, CONV_ROWS), wg_ref, CONV_ROWS, True).astype(BF16)
            dup_ref[1, pl.ds(base, CONV_ROWS), :] = _conv_apply(
                _conv_taps(sdv, base, CONV_ROWS), wv_ref, CONV_ROWS, True).astype(BF16)
            return carry

        lax.fori_loop(0, n_steps, spread, 0)

    col = lambda off: pl.BlockSpec((n_tok, LANES), lambda k: (0, k + off))
    wsp = pl.BlockSpec((16, LANES), lambda k: (0, k))
    pad = pltpu.VMEM((n_tok + 2 * CONV_PAD, LANES), F32)
    both = jax.ShapeDtypeStruct((2, n_tok, FFN_HIDDEN), BF16)
    dw = jax.ShapeDtypeStruct((16, FFN_HIDDEN), F32)
    return _call(body, "conv_backward", [both, dw, dw], grid=(nb,),
                 in_specs=[col(0), col(nb), col(0), col(0), col(0), wsp, wsp],
                 out_specs=[pl.BlockSpec((2, n_tok, LANES), lambda k: (0, 0, k)), wsp, wsp],
                 scratch=[pad, pad, pad, pad], sem=("arbitrary",))(up, up, gate, val, dact, wg, wv)


def f_adamw(w, g, m, v):
    m = ADAM_B1 * m + (1.0 - ADAM_B1) * g
    v = ADAM_B2 * v + (1.0 - ADAM_B2) * jnp.square(g)
    m_hat = m / (1.0 - ADAM_B1 ** ADAM_STEP)
    v_hat = v / (1.0 - ADAM_B2 ** ADAM_STEP)
    delta = -ADAM_LR * (m_hat / (jnp.sqrt(v_hat) + ADAM_EPS) + ADAM_WD * w)
    return delta, m, v


def adamw(w, g, m, v, name):
    shape = w.shape
    cols = shape[-1]
    rows = w.size // cols
    two_d = [t.reshape(rows, cols) for t in (w, g, m, v)]
    tm = _tile(rows, 256, 8) if rows % 8 == 0 else rows
    outs = rowcall(f_adamw, name, rows, tm, [(t, cols, 0, 0) for t in two_d], [], [(cols, F32)] * 3, [])
    return tuple(o.reshape(shape) for o in outs)


def _place():
    return lax.axis_index("x"), lax.axis_index("y"), lax.axis_index("c")


_ANY = pl.BlockSpec(memory_space=pl.ANY)


def _fill_own(gathered, own, index):
    return lax.dynamic_update_slice(gathered, own[None], (index,) + (0,) * own.ndim)


def allgather_devices(v, name, copy_own=True):
    def body(v_ref, out_ref, send_sems, recv_sems, local_sem):
        x, y, c = _place()
        me, sibling = (x, y, c), (x, y, 1 - c)
        chips = [(1 - x, y), (x, 1 - y), (1 - x, 1 - y)]

        def slot(p):
            return out_ref.at[4 * p[0] + 2 * p[1] + p[2]]

        def copy(k, block, to, src=None):
            return pltpu.make_async_remote_copy(
                src_ref=slot(block) if src is None else src, dst_ref=slot(block),
                send_sem=send_sems.at[k], recv_sem=recv_sems.at[k], device_id=to, device_id_type=MESH)

        mine = pltpu.make_async_copy(v_ref, slot(me), local_sem)
        if copy_own:
            mine.start()
        first = [copy(0, me, sibling, src=v_ref)]
        first += [copy(1 + j, me, (*chip, c), src=v_ref) for j, chip in enumerate(chips)]
        for cp in first:
            cp.start()
        passed = [copy(4 + j, (*chip, c), sibling) for j, chip in enumerate(chips)]
        for j, chip in enumerate(chips):
            copy(1 + j, (*chip, c), me).wait_recv()
            passed[j].start()
        copy(0, sibling, me).wait_recv()
        for j, chip in enumerate(chips):
            copy(4 + j, (*chip, 1 - c), me).wait_recv()
        for cp in first + passed:
            cp.wait_send()
        if copy_own:
            mine.wait()

    return _call(body, name, jax.ShapeDtypeStruct((N_DEV,) + v.shape, v.dtype), in_specs=[_ANY], out_specs=_ANY,
                 scratch=[pltpu.SemaphoreType.DMA((7,)), pltpu.SemaphoreType.DMA((7,)), pltpu.SemaphoreType.DMA])(v)


def _other_chips(x, y):
    return [(1 - x, y), (x, 1 - y), (1 - x, 1 - y)]


def allgather_chips(vs, name):
    n = len(vs)
    shapes = [v.shape for v in vs]
    vs = [v.reshape((2, v.shape[0] // 2) + v.shape[1:]) for v in vs]

    def body(*refs):
        v_refs, o_refs = refs[:n], refs[n:2 * n]
        send_sems, recv_sems = refs[2 * n:]
        x, y, c = _place()
        sibling = (x, y, 1 - c)
        chips = _other_chips(x, y)

        def rows(a, chip, h):
            return o_refs[a].at[2 * chip[0] + chip[1], h]

        def copy(a, k, chip, h, to, src=None):
            return pltpu.make_async_remote_copy(
                src_ref=rows(a, chip, h) if src is None else src, dst_ref=rows(a, chip, h),
                send_sem=send_sems.at[6 * a + k], recv_sem=recv_sems.at[6 * a + k], device_id=to, device_id_type=MESH)

        first = [copy(a, j, (x, y), c, (*chip, c), src=v_refs[a].at[c])
                 for a in range(n) for j, chip in enumerate(chips)]
        for cp in first:
            cp.start()
        passed = []
        for j, chip in enumerate(chips):
            for a in range(n):
                copy(a, j, chip, c, (x, y, c)).wait_recv()
                passed.append(copy(a, 3 + j, chip, c, sibling))
                passed[-1].start()
        for j, chip in enumerate(chips):
            for a in range(n):
                copy(a, 3 + j, chip, 1 - c, (x, y, c)).wait_recv()
        for cp in first + passed:
            cp.wait_send()

    outs = _call(body, name, [jax.ShapeDtypeStruct((N_CHIPS,) + v.shape, v.dtype) for v in vs], in_specs=[_ANY] * n,
                 out_specs=[_ANY] * n,
                 scratch=[pltpu.SemaphoreType.DMA((6 * n,)), pltpu.SemaphoreType.DMA((6 * n,))])(*vs)
    return [o.reshape((N_CHIPS,) + s) for o, s in zip(outs, shapes)]


def grad_pair_swap(gs, name):
    n = len(gs)

    def body(*refs):
        g_refs, o_refs, send_sems, recv_sems = refs[:n], refs[n:2 * n], refs[2 * n], refs[2 * n + 1]
        x, y, c = _place()
        cps = [pltpu.make_async_remote_copy(
            src_ref=g_refs[a].at[:, 1 - c], dst_ref=o_refs[a], send_sem=send_sems.at[a], recv_sem=recv_sems.at[a],
            device_id=(x, y, 1 - c), device_id_type=MESH) for a in range(n)]
        for cp in cps:
            cp.start()
        for cp in cps:
            cp.wait()

    return _call(body, name, [jax.ShapeDtypeStruct((N_CHIPS,) + g.shape[2:], g.dtype) for g in gs],
                 in_specs=[_ANY] * n, out_specs=[_ANY] * n,
                 scratch=[pltpu.SemaphoreType.DMA((n,)), pltpu.SemaphoreType.DMA((n,))])(*gs)


def pair_sum(g, recv, core, name):
    r2, cols = recv.shape[1], recv.shape[2]
    tr = _tile(r2, 256, 8)
    nt = r2 // tr

    def body(c_ref, g_ref, r_ref, o_ref):
        o_ref[...] = (g_ref[...] + r_ref[...]).astype(o_ref.dtype)

    spec = pl.BlockSpec((None, tr, cols), lambda q, i, c_ref: (q, i, 0))
    grid_spec = pltpu.PrefetchScalarGridSpec(
        num_scalar_prefetch=1, grid=(N_CHIPS, nt),
        in_specs=[pl.BlockSpec((None, None, tr, cols), lambda q, i, c_ref: (q, c_ref[0], i, 0)), spec], out_specs=spec)
    return pl.pallas_call(body, name=name, out_shape=jax.ShapeDtypeStruct(recv.shape, BF16), grid_spec=grid_spec,
                          compiler_params=pltpu.CompilerParams(dimension_semantics=("arbitrary", "arbitrary"),
                                                               vmem_limit_bytes=VMEM_LIMIT_BYTES))(core, g, recv)


def grad_chip_exchange(ps, name):
    n = len(ps)

    def body(*refs):
        p_refs, o_refs = refs[:n], refs[n:2 * n]
        send_sems, recv_sems, local_sems = refs[2 * n:]
        x, y, c = _place()
        chips = _other_chips(x, y)
        me = 2 * x + y
        mine = [pltpu.make_async_copy(p_refs[a].at[me], o_refs[a].at[me], local_sems.at[a]) for a in range(n)]
        sends = [pltpu.make_async_remote_copy(
            src_ref=p_refs[a].at[2 * chip[0] + chip[1]], dst_ref=o_refs[a].at[me], send_sem=send_sems.at[3 * a + j],
            recv_sem=recv_sems.at[3 * a + j], device_id=(*chip, c), device_id_type=MESH)
            for a in range(n) for j, chip in enumerate(chips)]
        for cp in mine + sends:
            cp.start()
        for a in range(n):
            for j, chip in enumerate(chips):
                pltpu.make_async_remote_copy(
                    src_ref=p_refs[a].at[me], dst_ref=o_refs[a].at[2 * chip[0] + chip[1]],
                    send_sem=send_sems.at[3 * a + j], recv_sem=recv_sems.at[3 * a + j], device_id=(*chip, c),
                    device_id_type=MESH).wait_recv()
        for cp in sends:
            cp.wait_send()
        for cp in mine:
            cp.wait()

    return _call(body, name, [jax.ShapeDtypeStruct(p.shape, p.dtype) for p in ps], in_specs=[_ANY] * n,
                 out_specs=[_ANY] * n, scratch=[pltpu.SemaphoreType.DMA((3 * n,)), pltpu.SemaphoreType.DMA((3 * n,)),
                                                pltpu.SemaphoreType.DMA((n,))])(*ps)


def grad_half_swap(ss, name):
    n = len(ss)

    def body(*refs):
        s_refs, o_refs, send_sems, recv_sems = refs[:n], refs[n:2 * n], refs[2 * n], refs[2 * n + 1]
        x, y, c = _place()
        cps = [pltpu.make_async_remote_copy(
            src_ref=s_refs[a], dst_ref=o_refs[a], send_sem=send_sems.at[a], recv_sem=recv_sems.at[a],
            device_id=(x, y, 1 - c), device_id_type=MESH) for a in range(n)]
        for cp in cps:
            cp.start()
        for cp in cps:
            cp.wait()

    return _call(body, name, [jax.ShapeDtypeStruct(s.shape, s.dtype) for s in ss], in_specs=[_ANY] * n,
                 out_specs=[_ANY] * n, scratch=[pltpu.SemaphoreType.DMA((n,)), pltpu.SemaphoreType.DMA((n,))])(*ss)


ROW_TILE = 256

BIG_SHARDS = (("w_in", (D_MODEL, 896), 1), ("s5_w_glu", (128, S5_WIDTH), 0), ("w_proj_a", (S5_WIDTH, 256), 1),
              ("w_proj_b", (SGU_WIDTH, 256), 1), ("w_out", (256, D_MODEL), 0), ("w_up", (D_MODEL, 1408), 1),
              ("w_down", (704, D_MODEL), 0))
SMALL_PARAMS = ("g_mix", "s5_a_re", "s5_a_im", "s5_log_step", "s5_b_re", "s5_b_im", "s5_c_re", "s5_c_im", "s5_d",
                "s5_b_glu", "sgu_ln_g", "sgu_ln_b", "sgu_w", "sgu_b", "b_gate", "g_ffn", "conv_b", "g_final")
PACK_COLS = 1024


def _rows_of(n):
    return -(-n // PACK_COLS)


def _pack_rows(arrays, total_rows, dtype):
    parts = []
    used = 0
    for a in arrays:
        r = _rows_of(a.size)
        parts.append(jnp.pad(a.reshape(-1).astype(dtype), (0, r * PACK_COLS - a.size)).reshape(r, PACK_COLS))
        used += r
    if total_rows > used:
        parts.append(jnp.zeros((total_rows - used, PACK_COLS), dtype))
    return jnp.concatenate(parts, axis=0)


def _unpack_rows(packed, shapes, row0=0):
    out = []
    for s in shapes:
        n = 1
        for d in s:
            n *= d
        r = _rows_of(n)
        out.append(packed[row0:row0 + r].reshape(-1)[:n].reshape(s))
        row0 += r
    return out


def _octet_major(re, im):
    parts = []
    for o in range(OCTETS):
        parts += [re[:, o * 512:(o + 1) * 512], im[:, o * 512:(o + 1) * 512]]
    return jnp.concatenate(parts, axis=1)


def _octet_split(v):
    v = v.reshape(OCTETS, 2, 512)
    return v[:, 0].reshape(N_STATE), v[:, 1].reshape(N_STATE)


def _s5_bmat(bb):
    t = bb.reshape(S5_GROUP, OCTETS, 8, 1, S5_STATE) * jnp.eye(8, dtype=F32)[None, None, :, :, None]
    return jnp.transpose(t, (1, 2, 0, 3, 4)).reshape(OCTETS, LANES, 512)


def _s5_bmat_t(dm):
    t = dm.reshape(OCTETS, 8, S5_GROUP, 8, S5_STATE) * jnp.eye(8, dtype=F32)[None, :, None, :, None]
    return jnp.transpose(t.sum(axis=3), (2, 0, 1, 3)).reshape(S5_GROUP, S5_GROUPS, S5_STATE)


def _s5_cmat(c):
    t = c.reshape(OCTETS, 8, 1, S5_GROUP, S5_STATE) * jnp.eye(8, dtype=F32)[None, :, :, None, None]
    return jnp.transpose(t, (0, 2, 4, 1, 3)).reshape(OCTETS, 512, LANES)


def _s5_cmat_t(dm):
    t = dm.reshape(OCTETS, 8, S5_STATE, 8, S5_GROUP) * jnp.eye(8, dtype=F32)[None, :, None, :, None]
    return jnp.transpose(t.sum(axis=1), (0, 2, 3, 1)).reshape(S5_GROUPS, S5_GROUP, S5_STATE)


def local_step(x, ctx, tgt, mod, modc, W):
    n_tok, n_ctx = x.shape[0], ctx.shape[0]
    tm = ROW_TILE
    D = D_MODEL
    sh1, sc1, ga1, sh2, sc2, ga2 = [mod[:, k * D:(k + 1) * D] for k in range(N_MOD)]
    sh1c, sc1c = modc[:, :D], modc[:, D:2 * D]
    g_mix, g_ffn, g_final = W["g_mix"], W["g_ffn"], W["g_final"]
    w_in = W["w_in"]
    w_in_u = w_in[0][:, :S5_WIDTH]

    h = rowcall(f_modulate, "mod1", n_tok, tm, [(x, D, 0, 0)], [g_mix, sc1, sh1], [(D, BF16)], [])[0]
    hc = rowcall(f_modulate, "mod1_ctx", n_ctx, tm, [(ctx, D, 0, 0)], [g_mix, sc1c, sh1c], [(D, BF16)], [])[0]
    proj = matmul(h, w_in, "nn", BF16, "proj_in", shards=True)
    uc = matmul(hc, w_in_u, "nn", BF16, "proj_in_ctx")
    u_lat = proj[:, :S5_WIDTH]
    u_s5 = (to_strand_order(jnp.concatenate([uc, u_lat], axis=0)), to_strand_order(jnp.concatenate([u_lat, uc], axis=0)))

    a_re, a_im, ls = W["s5_a_re"], W["s5_a_im"], W["s5_log_step"][..., None]
    b_re_t = jnp.transpose(W["s5_b_re"], (0, 3, 1, 2))
    b_im_t = jnp.transpose(W["s5_b_im"], (0, 3, 1, 2))
    bb_re, bb_im = s5_discretize(a_re, a_im, ls, b_re_t, b_im_t)
    ls_rep = jnp.repeat(W["s5_log_step"], S5_STATE, axis=1).reshape(2, 1, N_STATE)
    tabs = s5_tables(a_re.reshape(2, 1, N_STATE), a_im.reshape(2, 1, N_STATE), ls_rep)
    pw, qt, bm, cm = [], [], [], []
    for d in range(2):
        pw.append(tuple(jnp.repeat(_octet_major(tabs[k][d], tabs[k + 1][d]), SUB, axis=0) for k in (0, 2)))
        qt.append(tuple(_octet_major(tabs[k][d], tabs[k + 1][d]) for k in (4, 6)))
        bm.append(jnp.concatenate([_s5_bmat(bb_re[d]), _s5_bmat(bb_im[d])], axis=2).astype(BF16))
        cm.append(jnp.concatenate([_s5_cmat(W["s5_c_re"][d]), -_s5_cmat(W["s5_c_im"][d])], axis=1).astype(BF16))
    y0, cin0 = s5_forward(u_s5[0], bm[0], cm[0], pw[0][0], qt[0][0], False, "s5_fwd0")
    y1, cin1 = s5_forward(u_s5[1], bm[1], cm[1], pw[1][1], qt[1][1], True, "s5_fwd1")

    mix_rows = [(proj, 512, 0, 0), (y0, 512, 0, n_ctx // tm), (y1, 512, 0, 0)] + \
               [(proj, 512, k, 0) for k in range(1, 7)]
    mix_vecs = [W["s5_d"], W["s5_w_glu"], W["s5_b_glu"], W["sgu_ln_g"], W["sgu_ln_b"], W["sgu_w"],
                jnp.transpose(W["sgu_b"]), W["w_proj_a"], W["w_proj_b"], W["b_gate"]]
    mrg = rowcall(f_mixer, "mixer", n_tok, tm, mix_rows, mix_vecs, [(D, BF16)], [])[0]
    o = matmul(mrg, W["w_out"], "nn", F32, "proj_out")
    x1, h2 = rowcall(f_resid_mod, "resid_mod2", n_tok, tm, [(x, D, 0, 0), (o, D, 0, 0)], [ga1, g_ffn, sc2, sh2],
                     [(D, F32), (D, BF16)], [])
    up = matmul(h2, W["w_up"], "nn", BF16, "ffn_up", shards=True)
    conv_w = W["conv_w"].reshape(9, 2 * FFN_HIDDEN)
    wg = jnp.pad(conv_w[:, :FFN_HIDDEN], ((0, 7), (0, 0)))
    wv = jnp.pad(conv_w[:, FFN_HIDDEN:], ((0, 7), (0, 0)))
    act, gate, val = conv_forward(up, wg, wv, W["conv_b"])
    dn = matmul(act, W["w_down"], "nn", F32, "ffn_down")

    def final_fn(x1_, dn_, tgt_, ga2_, gf_):
        loss, (dx1_, ddn_, dga2_, dgf_) = jax.value_and_grad(f_final_loss, argnums=(0, 1, 3, 4))(
            x1_, dn_, tgt_, ga2_, gf_)
        return dx1_, ddn_, loss.reshape(1, 1), dga2_, dgf_

    dx2, ddn, loss, d_ga2, d_gfinal = rowcall(
        final_fn, "final_loss", n_tok, tm, [(x1, D, 0, 0), (dn, D, 0, 0), (tgt, D, 0, 0)], [ga2, g_final],
        [(D, F32), (D, BF16)], [(1, 1), (1, D), (1, D)])

    dact = matmul(ddn, W["w_down"], "nt", BF16, "ffn_down_dx")
    d_w_down = matmul(act, ddn, "tn", F32, "ffn_down_dw")
    dup, dwg, dwv = conv_backward(up, gate, val, dact, wg, wv)
    d_conv_w = jnp.concatenate([dwg[:9], dwv[:9]], axis=1).reshape(3, 3, 2 * FFN_HIDDEN)
    d_conv_b = jnp.concatenate([dwg[9:10], dwv[9:10]], axis=1)
    dh2 = matmul(dup, W["w_up"], "nt", BF16, "ffn_up_dx", shards=True, halves=True)
    d_w_up = matmul(h2, dup, "tn", F32, "ffn_up_dw", shards=True, halves=True)

    def resid_bwd(x_, o_, dx1_, dh2_, ga_, g_, sc_, sh_):
        _, vjp = jax.vjp(f_resid_mod, x_, o_, ga_, g_, sc_, sh_)
        return vjp((dx1_, dh2_))

    dxa, do, d_ga1, d_gffn, d_sc2, d_sh2 = rowcall(
        resid_bwd, "resid_mod2_bwd", n_tok, tm, [(x, D, 0, 0), (o, D, 0, 0), (dx2, D, 0, 0), (dh2, D, 0, 0)],
        [ga1, g_ffn, sc2, sh2], [(D, F32), (D, BF16)], [(1, D)] * 4)

    dmrg = matmul(do, W["w_out"], "nt", BF16, "proj_out_dx")
    d_w_out = matmul(mrg, do, "tn", F32, "proj_out_dw")

    def mixer_bwd(*args):
        rows, dm, vecs = args[:9], args[9], [v.astype(F32) for v in args[10:]]
        _, vjp = jax.vjp(f_mixer, *rows, *vecs)
        g = vjp(dm)
        return (jnp.concatenate([g[0]] + list(g[3:9]), axis=1), g[1]) + tuple(g[9:])

    mb = rowcall(mixer_bwd, "mixer_bwd", n_tok, tm, mix_rows + [(dmrg, D, 0, 0)], mix_vecs,
                 [(w_in.shape[2] * N_CHIPS, BF16), (512, BF16)], [v.shape for v in mix_vecs])
    dproj, dys = mb[:2]
    d_s5d, d_w_glu, d_b_glu, d_ln_g, d_ln_b, d_sgu_w, d_sgu_bt, d_w_pa, d_w_pb, d_b_gate = mb[2:]

    zc = jnp.zeros((n_ctx, S5_WIDTH), BF16)
    dy_s5 = (to_strand_order(jnp.concatenate([zc, dys], axis=0)), to_strand_order(jnp.concatenate([dys, zc], axis=0)))
    du0, dbm0, dcm0, da0 = s5_backward(u_s5[0], dy_s5[0], cin0, bm[0], cm[0], pw[0][0], qt[0][0], pw[0][1], qt[0][1],
                                       False, "s5_bwd0")
    du1, dbm1, dcm1, da1 = s5_backward(u_s5[1], dy_s5[1], cin1, bm[1], cm[1], pw[1][1], qt[1][1], pw[1][0], qt[1][0],
                                       True, "s5_bwd1")
    dproj = add_into_columns(dproj, [(du0, n_ctx // tm), (du1, 0)], S5_WIDTH, "du_sum", tm)
    du_c = rowcall(lambda a, b: a + b, "du_sum_ctx", n_ctx, tm, [(du0, 512, 0, 0), (du1, 512, 0, n_tok // tm)],
                   [], [(512, BF16)], [])[0]

    dab_re, dab_im, dbb_re, dbb_im, d_c_re, d_c_im = [], [], [], [], [], []
    for dbm, dcm, da in ((dbm0, dcm0, da0), (dbm1, dcm1, da1)):
        r, i = _octet_split(da)
        dab_re.append(r.reshape(S5_GROUPS, S5_STATE))
        dab_im.append(i.reshape(S5_GROUPS, S5_STATE))
        dbb_re.append(_s5_bmat_t(dbm[:, :, :512]))
        dbb_im.append(_s5_bmat_t(dbm[:, :, 512:]))
        d_c_re.append(_s5_cmat_t(dcm[:, :512]))
        d_c_im.append(-_s5_cmat_t(dcm[:, 512:]))
    d_a_re, d_a_im, d_ls, d_b_re_t, d_b_im_t = s5_discretize_bwd(
        a_re, a_im, ls, b_re_t, b_im_t, jnp.stack(dab_re), jnp.stack(dab_im), jnp.stack(dbb_re), jnp.stack(dbb_im))

    dh = matmul(dproj, w_in, "nt", BF16, "proj_in_dx", shards=True)
    dhc = matmul(du_c, w_in_u, "nt", BF16, "proj_in_ctx_dx")
    du_c_wide = jnp.pad(du_c, ((0, 0), (0, dproj.shape[1] - S5_WIDTH)))
    d_w_in_c = matmul(hc, du_c_wide, "tn", F32, "proj_in_ctx_dw", shards=True)
    d_w_in = matmul(h, dproj, "tn", F32, "proj_in_dw", shards=True, init=d_w_in_c)

    def mod_bwd_ctx(x_, dh_, g_, sc_, sh_):
        _, vjp = jax.vjp(f_modulate, x_, g_, sc_, sh_)
        return vjp(dh_)[1:]

    d_gmix_c, d_sc1c, d_sh1c = rowcall(mod_bwd_ctx, "mod1_ctx_bwd", n_ctx, tm, [(ctx, D, 0, 0), (dhc, D, 0, 0)],
                                       [g_mix, sc1c, sh1c], [], [(1, D)] * 3)

    def mod_bwd(x_, dh_, dxa_, g_, sc_, sh_):
        _, vjp = jax.vjp(f_modulate, x_, g_, sc_, sh_)
        dx_, dg_, dsc_, dsh_ = vjp(dh_)
        return dx_ + dxa_, dg_, dsc_, dsh_

    zero_d = jnp.zeros((1, D), F32)
    grad_x, d_gmix, d_sc1, d_sh1 = rowcall(
        mod_bwd, "mod1_bwd", n_tok, tm, [(x, D, 0, 0), (dh, D, 0, 0), (dxa, D, 0, 0)], [g_mix, sc1, sh1],
        [(D, F32)], [(1, D)] * 3, ainit=[d_gmix_c, zero_d, zero_d])

    grads = {
        "dmod": jnp.concatenate([d_sh1, d_sc1, d_ga1, d_sh2, d_sc2, d_ga2], axis=1),
        "dmodc": jnp.concatenate([d_sh1c, d_sc1c], axis=1),
        "g_mix": d_gmix,
        "s5_a_re": d_a_re, "s5_a_im": d_a_im, "s5_log_step": d_ls[..., 0],
        "s5_b_re": jnp.transpose(d_b_re_t, (0, 2, 3, 1)), "s5_b_im": jnp.transpose(d_b_im_t, (0, 2, 3, 1)),
        "s5_c_re": jnp.stack(d_c_re), "s5_c_im": jnp.stack(d_c_im), "s5_d": d_s5d, "s5_b_glu": d_b_glu,
        "sgu_ln_g": d_ln_g, "sgu_ln_b": d_ln_b, "sgu_w": d_sgu_w, "sgu_b": jnp.transpose(d_sgu_bt),
        "b_gate": d_b_gate, "g_ffn": d_gffn, "conv_b": d_conv_b, "g_final": d_gfinal, "conv_w": d_conv_w,
        "w_in": d_w_in, "s5_w_glu": d_w_glu, "w_proj_a": d_w_pa, "w_proj_b": d_w_pb, "w_out": d_w_out,
        "w_up": d_w_up, "w_down": d_w_down,
    }
    return loss, grad_x, grads


ADA_COLS = N_MOD * D_MODEL // N_CHIPS
MOD_ROWS = 16


def mod_forward(c16, w, b):
    n = w.shape[1]
    tn = 512

    def body(c_ref, w_ref, b_ref, o_ref):
        cv = c_ref[...]
        cs = cv * jax.nn.sigmoid(cv)
        o_ref[...] = jnp.dot(cs.astype(BF16), w_ref[...].astype(BF16), preferred_element_type=F32) + b_ref[...]

    return _call(body, "mod_forward", jax.ShapeDtypeStruct((MOD_ROWS, n), F32), grid=(n // tn,),
                 in_specs=[pl.BlockSpec((MOD_ROWS, D_MODEL), lambda j: (0, 0)),
                           pl.BlockSpec((D_MODEL, tn), lambda j: (0, j)), pl.BlockSpec((1, tn), lambda j: (0, j))],
                 out_specs=pl.BlockSpec((MOD_ROWS, tn), lambda j: (0, j)), sem=("arbitrary",))(c16, w, b)


def f_ada_outer(ct, dm):
    cs = ct * jax.nn.sigmoid(ct)
    acc = cs[:, 0:1] * dm[0:1]
    for k in range(1, 9):
        acc = acc + cs[:, k:k + 1] * dm[k:k + 1]
    return acc


def f_cctx_grad(z, p4):
    s = jax.nn.sigmoid(z)
    return (p4[0:1] + p4[1:2] + p4[2:3] + p4[3:4]) * (s + z * s * (1.0 - s))


WEIGHT_NAMES = ("c_ctx", "w_ada", "b_ada", "g_mix", "w_in", "s5_a_re", "s5_a_im", "s5_log_step", "s5_b_re",
                "s5_b_im", "s5_c_re", "s5_c_im", "s5_d", "s5_w_glu", "s5_b_glu", "sgu_ln_g", "sgu_ln_b", "sgu_w",
                "sgu_b", "w_proj_a", "w_proj_b", "b_gate", "w_out", "g_ffn", "w_up", "conv_w", "conv_b", "w_down",
                "g_final")
CONV_SHARD = 2 * FFN_HIDDEN // N_CHIPS
SMALL_PACK_ROWS = 512
SMALL_ROW0 = 58


def kernel(x, c, ctx, c_ctx, w_ada, b_ada, g_mix, w_in, s5_a_re, s5_a_im, s5_log_step, s5_b_re, s5_b_im, s5_c_re, s5_c_im, s5_d, s5_w_glu, s5_b_glu, sgu_ln_g, sgu_ln_b, sgu_w, sgu_b, w_proj_a, w_proj_b, b_gate, w_out, g_ffn, w_up, conv_w, conv_b, w_down, g_final, loss_target, m_c_ctx, m_w_ada, m_b_ada, m_g_mix, m_w_in, m_s5_a_re, m_s5_a_im, m_s5_log_step, m_s5_b_re, m_s5_b_im, m_s5_c_re, m_s5_c_im, m_s5_d, m_s5_w_glu, m_s5_b_glu, m_sgu_ln_g, m_sgu_ln_b, m_sgu_w, m_sgu_b, m_w_proj_a, m_w_proj_b, m_b_gate, m_w_out, m_g_ffn, m_w_up, m_conv_w, m_conv_b, m_w_down, m_g_final, v_c_ctx, v_w_ada, v_b_ada, v_g_mix, v_w_in, v_s5_a_re, v_s5_a_im, v_s5_log_step, v_s5_b_re, v_s5_b_im, v_s5_c_re, v_s5_c_im, v_s5_d, v_s5_w_glu, v_s5_b_glu, v_sgu_ln_g, v_sgu_ln_b, v_sgu_w, v_sgu_b, v_w_proj_a, v_w_proj_b, v_b_gate, v_w_out, v_g_ffn, v_w_up, v_conv_w, v_conv_b, v_w_down, v_g_final):
    given = dict(locals())
    wts = {n: given[n] for n in WEIGHT_NAMES}
    ms = {n: given["m_" + n] for n in WEIGHT_NAMES}
    vs = {n: given["v_" + n] for n in WEIGHT_NAMES}
    xi, yi, ci = _place()
    chip = 2 * xi + yi
    dev = 2 * chip + ci
    D = D_MODEL

    c8 = allgather_devices(jnp.pad(c, ((0, 7), (0, 0))), "gather_c")[:, 0, :]
    c16 = jnp.concatenate([c8, c_ctx[None], jnp.zeros((MOD_ROWS - 9, D), F32)], axis=0)
    b_shard = lax.dynamic_slice(b_ada, (0, chip * ADA_COLS), (1, ADA_COLS))
    mod_shard = mod_forward(c16, w_ada[0], b_shard)
    mod_all = allgather_devices(mod_shard, "gather_mod")
    mod_full = jnp.concatenate([mod_all[2 * q] for q in range(N_CHIPS)], axis=1)
    mod = lax.dynamic_slice(mod_full, (dev, 0), (1, N_MOD * D))
    modc = mod_full[8:9]

    big_names = [n for n, _, _ in BIG_SHARDS]
    conv_rows = jnp.pad(conv_w[0].reshape(9, CONV_SHARD), ((0, 7), (0, 0)))
    shards = [wts[n][0].astype(BF16) for n in big_names] + [conv_rows]
    gathered = allgather_chips(shards, "gather_weights")
    gathered = [_fill_own(t, s, chip) for t, s in zip(gathered, shards)]
    W = dict(zip(big_names, gathered[:-1]))
    for n, shape, axis in BIG_SHARDS:
        if axis == 0:
            W[n] = W[n].reshape(N_CHIPS * shape[0], shape[1])
    for n in ("w_proj_a", "w_proj_b"):
        W[n] = jnp.transpose(W[n], (1, 0, 2)).reshape(W[n].shape[1], -1)
    W["conv_w"] = jnp.transpose(gathered[-1][:, :9], (1, 0, 2)).reshape(3, 3, 2 * FFN_HIDDEN)
    for n in ("g_mix", "g_ffn", "s5_d", "s5_b_glu", "sgu_ln_g", "sgu_ln_b", "b_gate", "conv_b"):
        W[n] = wts[n]
    W["g_final"] = g_final[None]
    for n in ("s5_a_re", "s5_a_im", "s5_log_step", "s5_b_re", "s5_b_im", "s5_c_re", "s5_c_im", "sgu_w", "sgu_b"):
        W[n] = wts[n][0]

    loss_part, grad_x, g = local_step(x[0], ctx[0], loss_target[0], mod, modc, W)
    loss = lax.psum(loss_part[0, 0], ("x", "y", "c"))

    g_slots = []
    for n, shape, axis in BIG_SHARDS:
        if n in ("w_proj_a", "w_proj_b"):
            g_slots.append(jnp.transpose(g[n].reshape(shape[0], N_CHIPS, shape[1]), (1, 0, 2)))
        else:
            g_slots.append(g[n].reshape((N_CHIPS,) + shape))
    g_slots = [t.reshape(N_CHIPS, 2, t.shape[1] // 2, t.shape[2]) for t in g_slots]
    core = ci.astype(jnp.int32).reshape(1)
    from_sibling = grad_pair_swap(g_slots, "grad_pair_swap")
    pair = [pair_sum(gs, rv, core, "grad_pair_sum_" + n) for gs, rv, n in zip(g_slots, from_sibling, big_names)]
    from_chips = grad_chip_exchange(pair, "grad_chip_exchange")
    add2 = lambda a, b: a + b
    add4 = lambda a, b, c_, d: ((a + b) + c_) + d
    halves = []
    for fc, n in zip(from_chips, big_names):
        r2, cols = fc.shape[1], fc.shape[2]
        tr = _tile(r2, 256, 8)
        halves.append(rowcall(add4, "grad_chip_sum_" + n, r2, tr,
                              [(fc.reshape(N_CHIPS * r2, cols), cols, 0, q * r2 // tr) for q in range(N_CHIPS)], [],
                              [(cols, F32)], [])[0])
    others = grad_half_swap(halves, "grad_half_swap")
    big_grads = {n: jnp.where(ci == 0, jnp.concatenate([mine, other], axis=0), jnp.concatenate([other, mine], axis=0))
                 for n, mine, other in zip(big_names, halves, others)}

    small_pack = _pack_rows([g["dmod"], g["dmodc"], g["conv_w"]] + [g[n] for n in SMALL_PARAMS], SMALL_PACK_ROWS, F32)
    from_core = grad_half_swap([small_pack], "small_pair_swap")[0]
    small_pair = rowcall(add2, "small_pair_sum", SMALL_PACK_ROWS, 256,
                         [(small_pack, PACK_COLS, 0, 0), (from_core, PACK_COLS, 0, 0)], [], [(PACK_COLS, F32)], [])[0]
    small_chips = _fill_own(allgather_chips([small_pair], "gather_small_grads")[0], small_pair, chip)
    small_2d = small_chips.reshape(N_CHIPS * SMALL_PACK_ROWS, PACK_COLS)
    small_sum = rowcall(add4, "small_grad_sum", SMALL_PACK_ROWS, 256,
                        [(small_2d, PACK_COLS, 0, q * SMALL_PACK_ROWS // 256) for q in range(N_CHIPS)], [],
                        [(PACK_COLS, F32)], [])[0]
    dmod_all = allgather_devices(small_pack[0:8], "gather_dmod")[:, 0:N_MOD].reshape(N_DEV, N_MOD * D)
    dmod_sum = small_sum[0:N_MOD].reshape(1, N_MOD * D)
    dmodc_sum = jnp.pad(small_sum[N_MOD:N_MOD + 2].reshape(1, 2 * D), ((0, 0), (0, (N_MOD - 2) * D)))
    conv_grad = _unpack_rows(small_sum, [(3, 3, 2 * FFN_HIDDEN)], row0=N_MOD + 2)[0]
    small_grads = dict(zip(SMALL_PARAMS, _unpack_rows(small_sum, [wts[n].shape for n in SMALL_PARAMS], row0=SMALL_ROW0)))

    dm16 = jnp.concatenate([dmod_all, dmodc_sum, jnp.zeros((MOD_ROWS - 9, N_MOD * D), F32)], axis=0)
    dm_shard = lax.dynamic_slice(dm16, (0, chip * ADA_COLS), (MOD_ROWS, ADA_COLS))
    g_w_ada = rowcall(f_ada_outer, "w_ada_grad", D, 256, [(jnp.transpose(c16), MOD_ROWS, 0, 0)], [dm_shard],
                      [(ADA_COLS, F32)], [])[0]
    g_b_ada = rowcall(add2, "b_ada_grad", 1, 1, [(dmod_sum, N_MOD * D, 0, 0), (dmodc_sum, N_MOD * D, 0, 0)], [],
                      [(N_MOD * D, F32)], [])[0]
    dmc_rows = jnp.pad(dm_shard[8:9], ((0, 7), (0, 0)))
    cctx_part = matmul(dmc_rows, w_ada[0], "nt", F32, "c_ctx_partial")
    cctx_all = allgather_devices(cctx_part, "gather_c_ctx")
    cctx_4 = jnp.stack([cctx_all[2 * q, 0] for q in range(N_CHIPS)])
    g_c_ctx = rowcall(f_cctx_grad, "c_ctx_grad", 1, 1, [(c_ctx[None], D, 0, 0)], [cctx_4], [(D, F32)], [])[0]

    grads = dict(small_grads)
    grads.update(big_grads)
    grads["w_ada"] = g_w_ada
    grads["b_ada"] = g_b_ada
    grads["c_ctx"] = g_c_ctx
    grads["conv_w"] = lax.dynamic_slice(conv_grad, (0, 0, chip * CONV_SHARD), (3, 3, CONV_SHARD))
    grads = {n: grads[n].reshape(wts[n].shape) for n in WEIGHT_NAMES}

    delta, new_m, new_v = {}, {}, {}
    for n in WEIGHT_NAMES:
        shape2d = (-1, wts[n].shape[-1])
        d_, m_, v_ = adamw(wts[n].reshape(shape2d), grads[n].reshape(shape2d), ms[n].reshape(shape2d),
                           vs[n].reshape(shape2d), "adamw_" + n)
        delta[n], new_m[n], new_v[n] = [t.reshape(wts[n].shape) for t in (d_, m_, v_)]

    return (loss, grad_x[None], *[grads[n] for n in WEIGHT_NAMES], *[delta[n] for n in WEIGHT_NAMES],
            *[new_m[n] for n in WEIGHT_NAMES], *[new_v[n] for n in WEIGHT_NAMES])
```

```python
import functools

import jax
import jax.numpy as jnp
from jax import lax
from jax.experimental import pallas as pl
from jax.experimental.pallas import tpu as pltpu

F32, BF16 = jnp.float32, jnp.bfloat16
MESH = pl.DeviceIdType.MESH

D_MODEL = 1024
S5_WIDTH = 512
S5_GROUP = 16
S5_GROUPS = 32
S5_STATE = 64
SGU_WIDTH = 512
SGU_GROUPS = 8
CHUNK = 128
FFN_HIDDEN = 2816
GRID_W = 64
N_MOD = 6
EPS = 1e-6
N_STATE = S5_GROUPS * S5_STATE
OCTETS = 4
SCAN_T = 128
N_CHIPS = 4
N_DEV = 8
LANES = 128
VMEM_LIMIT_BYTES = 56 * 1024 * 1024
CONV_PAD = 72
CONV_ROWS = 512

ADAM_LR, ADAM_B1, ADAM_B2, ADAM_EPS, ADAM_WD, ADAM_STEP = 0.001, 0.9, 0.999, 1e-08, 0.01, 10


def _call(body, name, out_shape, grid=None, in_specs=None, out_specs=None, scratch=(), sem=None, **kw):
    params = pltpu.CompilerParams(dimension_semantics=sem, vmem_limit_bytes=VMEM_LIMIT_BYTES)
    extra = {} if grid is None else {"grid": grid}
    return pl.pallas_call(body, name=name, out_shape=out_shape, in_specs=in_specs, out_specs=out_specs,
                          scratch_shapes=list(scratch), compiler_params=params, **extra, **kw)


def _tile(n, target, mult=LANES):
    best = None
    t = mult
    while t <= min(n, target):
        if n % t == 0:
            best = t
        t += mult
    return best or n


@jax.custom_vjp
def mmul(a, b):
    return jnp.dot(a.astype(BF16), b.astype(BF16), preferred_element_type=F32)


def _mmul_fwd(a, b):
    return mmul(a, b), (a, b)


def _mmul_bwd(res, ct):
    a, b = res
    ctb = ct.astype(BF16)
    da = lax.dot_general(ctb, b.astype(BF16), (((1,), (1,)), ((), ())), preferred_element_type=F32)
    db = lax.dot_general(a.astype(BF16), ctb, (((0,), (0,)), ((), ())), preferred_element_type=F32)
    return da.astype(a.dtype), db.astype(b.dtype)


mmul.defvjp(_mmul_fwd, _mmul_bwd)

_DOT_DIMS = {"nn": ((1,), (0,)), "nt": ((1,), (1,)), "tn": ((0,), (0,))}


MM_TILE = 1408
MM_FULL_K = 2048


def matmul(a, b, mode, out_dtype, name, init=None, shards=False, halves=False):
    if mode == "nn":
        (M, K), N = a.shape, (b.shape[2] * N_CHIPS if shards else b.shape[1])
    elif mode == "nt":
        M, N, K = a.shape[-2], b.shape[-2], a.shape[-1] * (2 if halves else 1)
    else:
        (K, M), N = a.shape, b.shape[-1] * (2 if halves else 1)
    ns = (K if mode == "nt" else N) // N_CHIPS
    tm = _tile(M, MM_TILE, 8 if M < LANES else LANES)
    tn = _tile(ns if shards and mode != "nt" else N, MM_TILE)
    if shards and mode == "nt":
        tk = _tile(ns, MM_TILE)
    else:
        tk = K if K <= MM_FULL_K else _tile(K, MM_TILE)
    nk = K // tk
    per = ns // (tk if mode == "nt" else tn)
    dims = (_DOT_DIMS[mode], ((), ()))
    has_init = init is not None
    use_acc = nk > 1 and out_dtype != F32

    def body(*refs):
        a_ref, b_ref = refs[:2]
        i_ref = refs[2] if has_init else None
        o_ref = refs[3] if has_init else refs[2]
        acc = refs[-1] if use_acc else o_ref
        k = pl.program_id(2)
        part = lax.dot_general(a_ref[...].astype(BF16), b_ref[...].astype(BF16), dims, preferred_element_type=F32)

        @pl.when(k == 0)
        def _():
            first = part + i_ref[...].astype(F32) if has_init else part
            acc[...] = first.astype(acc.dtype)

        if nk > 1:
            @pl.when(k > 0)
            def _():
                acc[...] += part

        if use_acc:
            @pl.when(k == nk - 1)
            def _():
                o_ref[...] = acc[...].astype(o_ref.dtype)

    if mode == "tn":
        a_spec = pl.BlockSpec((tk, tm), lambda i, j, k: (k, i))
    elif halves:
        ph = K // 2 // tk
        a_spec = pl.BlockSpec((None, tm, tk), lambda i, j, k: (k // ph, i, k % ph))
    else:
        a_spec = pl.BlockSpec((tm, tk), lambda i, j, k: (i, k))
    if mode == "nt":
        b_spec = (pl.BlockSpec((None, tn, tk), lambda i, j, k: (k // per, j, k % per)) if shards
                  else pl.BlockSpec((tn, tk), lambda i, j, k: (j, k)))
    elif mode == "tn" and halves:
        ph = N // 2 // tn
        b_spec = pl.BlockSpec((None, tk, tn), lambda i, j, k: (j // ph, k, j % ph))
    else:
        b_spec = (pl.BlockSpec((None, tk, tn), lambda i, j, k: (j // per, k, j % per)) if shards and mode == "nn"
                  else pl.BlockSpec((tk, tn), lambda i, j, k: (k, j)))
    if shards and mode == "tn":
        o_spec = pl.BlockSpec((None, tm, tn), lambda i, j, k: (j // per, i, j % per))
        out_shape = jax.ShapeDtypeStruct((N_CHIPS, M, ns), out_dtype)
    else:
        o_spec = pl.BlockSpec((tm, tn), lambda i, j, k: (i, j))
        out_shape = jax.ShapeDtypeStruct((M, N), out_dtype)
    in_specs = [a_spec, b_spec] + ([o_spec] if has_init else [])
    args = (a, b) + ((init,) if has_init else ())
    return _call(body, name, out_shape, grid=(M // tm, N // tn, nk),
                 in_specs=in_specs, out_specs=o_spec, scratch=[pltpu.VMEM((tm, tn), F32)] if use_acc else [],
                 sem=("parallel", "parallel", "arbitrary"))(*args)


def rowcall(fn, name, nrows, tm, rins, vins, routs, aouts, ainit=None):
    n_r, n_v, n_ro = len(rins), len(vins), len(routs)
    n_i = len(aouts) if ainit is not None else 0

    def body(*refs):
        r_in, v_in, i_in = refs[:n_r], refs[n_r:n_r + n_v], refs[n_r + n_v:n_r + n_v + n_i]
        r_out, a_out = refs[n_r + n_v + n_i:n_r + n_v + n_i + n_ro], refs[n_r + n_v + n_i + n_ro:]
        outs = fn(*[r[...].astype(F32) for r in r_in], *[v[...] for v in v_in])
        if not isinstance(outs, (tuple, list)):
            outs = (outs,)
        for ref, val in zip(r_out, outs[:n_ro]):
            ref[...] = val.astype(ref.dtype)
        if a_out:
            @pl.when(pl.program_id(0) == 0)
            def _():
                for k, ref in enumerate(a_out):
                    ref[...] = i_in[k][...] if n_i else jnp.zeros_like(ref)

            for ref, val in zip(a_out, outs[n_ro:]):
                ref[...] += val.astype(F32)

    def rspec(width, cblk, roff):
        return pl.BlockSpec((tm, width), lambda i: (i + roff, cblk))

    def whole(shape):
        nd = len(shape)
        return pl.BlockSpec(tuple(shape), lambda i: (0,) * nd)

    inits = list(ainit) if n_i else []
    in_specs = [rspec(w, cb, ro) for (_, w, cb, ro) in rins] + [whole(v.shape) for v in vins + inits]
    out_specs = [rspec(w, 0, 0) for (w, _) in routs] + [whole(s) for s in aouts]
    out_shape = [jax.ShapeDtypeStruct((nrows, w), dt) for (w, dt) in routs] + \
                [jax.ShapeDtypeStruct(tuple(s), F32) for s in aouts]
    res = _call(body, name, out_shape, grid=(nrows // tm,), in_specs=in_specs, out_specs=out_specs,
                sem=("arbitrary",))(*[r[0] for r in rins], *vins, *inits)
    return res


def add_into_columns(buf, parts, width, name, tm):
    n = len(parts)

    def body(*refs):
        acc = refs[0][...].astype(F32)
        for p_ref in refs[1:1 + n]:
            acc = acc + p_ref[...].astype(F32)
        refs[1 + n][...] = acc.astype(refs[1 + n].dtype)

    def rows_at(roff):
        return pl.BlockSpec((tm, width), lambda i: (i + roff, 0))

    return _call(body, name, jax.ShapeDtypeStruct(buf.shape, buf.dtype), grid=(buf.shape[0] // tm,),
                 in_specs=[rows_at(0)] + [rows_at(ro) for _, ro in parts], out_specs=rows_at(0), sem=("arbitrary",),
                 input_output_aliases={0: 0})(buf, *[p for p, _ in parts])


def _rms(x):
    return lax.rsqrt(jnp.mean(x * x, axis=-1, keepdims=True) + EPS)


def f_modulate(x, g, sc, sh):
    return (x * _rms(x)) * g * (1.0 + sc) + sh


def f_resid_mod(x, o, ga, g, sc, sh):
    x1 = x + ga * o
    return x1, f_modulate(x1, g, sc, sh)


def f_final_loss(x1, dn, tgt, ga2, gf):
    x2 = x1 + ga2 * dn
    y = (x2 * _rms(x2)) * gf
    err = jnp.square(y - tgt)
    return 0.5 * jnp.sum(jnp.mean(err, axis=-1))


def _sgu_spatial(vn, w, bt):
    lo = lax.broadcasted_iota(jnp.int32, (1, LANES), 1) < (SGU_WIDTH // SGU_GROUPS)
    row_blocks = []
    for r in range(vn.shape[0] // CHUNK):
        rows = vn[r * CHUNK:(r + 1) * CHUNK]
        cols = []
        for j in range(SGU_WIDTH // LANES):
            blk = rows[:, j * LANES:(j + 1) * LANES]
            v_lo = jnp.where(lo, blk, 0.0)
            v_hi = jnp.where(lo, 0.0, blk)
            s = mmul(w[2 * j], v_lo) + mmul(w[2 * j + 1], v_hi)
            bias = jnp.where(lo, bt[:, 2 * j:2 * j + 1], bt[:, 2 * j + 1:2 * j + 2])
            cols.append(s + bias)
        row_blocks.append(jnp.concatenate(cols, axis=1))
    return jnp.concatenate(row_blocks, axis=0) if len(row_blocks) > 1 else row_blocks[0]


def f_mixer(u_a, y0, y1, zu, zv, ga0, ga1, gb0, gb1, d_skip, w_glu, b_glu, ln_g, ln_b, sgu_w, sgu_bt,
            w_pa, w_pb, b_gate):
    ys = u_a * d_skip + y0 + y1
    ge = jax.nn.gelu(ys)
    y_a = ge * jax.nn.sigmoid(mmul(ge, w_glu) + b_glu)
    u_sg = jax.nn.gelu(zu)
    v = jax.nn.gelu(zv)
    vc = v - jnp.mean(v, axis=-1, keepdims=True)
    vn = (vc * lax.rsqrt(jnp.mean(vc * vc, axis=-1, keepdims=True) + EPS)) * ln_g + ln_b
    y_b = u_sg * _sgu_spatial(vn, sgu_w, sgu_bt)
    gl_a = jnp.concatenate([ga0, ga1], axis=1) + b_gate[:, :D_MODEL]
    gl_b = jnp.concatenate([gb0, gb1], axis=1) + b_gate[:, D_MODEL:]
    return jax.nn.sigmoid(gl_a) * mmul(y_a, w_pa) + jax.nn.sigmoid(gl_b) * mmul(y_b, w_pb)


def _cmul(ar, ai, xr, xi):
    return ar * xr - ai * xi, ar * xi + ai * xr


SUB = 8
STRAND = SCAN_T // SUB


def to_strand_order(v):
    n = v.shape[0] // SCAN_T
    return jnp.transpose(v.reshape(n, SUB, STRAND, v.shape[1]), (0, 2, 1, 3)).reshape(v.shape)


def _to_token_order(v):
    i = lax.broadcasted_iota(jnp.int32, (SCAN_T, SCAN_T), 0)
    j = lax.broadcasted_iota(jnp.int32, (SCAN_T, SCAN_T), 1)
    perm = jnp.where(i == STRAND * (j % SUB) + j // SUB, 1.0, 0.0).astype(BF16)
    hi = v.astype(BF16)
    lo = (v - hi.astype(F32)).astype(BF16)
    return jnp.dot(perm, hi, preferred_element_type=F32) + jnp.dot(perm, lo, preferred_element_type=F32)


def _scan_strands(xr, xi, pw_ref, q_ref, col, rev, conj, cr, ci):
    def tab(ref, lo):
        t_r = ref[lo:lo + SUB, col:col + LANES]
        t_i = ref[lo:lo + SUB, col + 512:col + 512 + LANES]
        return t_r, (-t_i if conj else t_i)

    a_r, a_i = tab(pw_ref, (STRAND - 1) * SUB if rev else 0)
    order = list(range(STRAND - 1, -1, -1) if rev else range(STRAND))
    lr, li = [None] * STRAND, [None] * STRAND
    for n, k in enumerate(order):
        lr[k], li[k] = xr[k * SUB:(k + 1) * SUB], xi[k * SUB:(k + 1) * SUB]
        if n:
            m_r, m_i = _cmul(a_r, a_i, lr[order[n - 1]], li[order[n - 1]])
            lr[k], li[k] = lr[k] + m_r, li[k] + m_i
    f_r, f_i = lr[order[-1]], li[order[-1]]
    q_r, q_i = tab(q_ref, 0)
    sub = lax.broadcasted_iota(jnp.int32, (SUB, 1), 0)
    s = 1
    while s < SUB:
        row = (SUB - s) if rev else (s - 1)
        shift = (SUB - s) if rev else s
        m = (sub < SUB - s) if rev else (sub >= s)
        p_r, p_i = _cmul(q_r[row:row + 1], q_i[row:row + 1], pltpu.roll(f_r, shift, 0), pltpu.roll(f_i, shift, 0))
        f_r, f_i = f_r + jnp.where(m, p_r, 0.0), f_i + jnp.where(m, p_i, 0.0)
        s *= 2
    c_r, c_i = jnp.broadcast_to(cr, (SUB, LANES)), jnp.broadcast_to(ci, (SUB, LANES))
    p_r, p_i = _cmul(q_r, q_i, c_r, c_i)
    s_r, s_i = f_r + p_r, f_i + p_i
    edge = 0 if rev else SUB - 1
    first = sub == (SUB - 1 if rev else 0)
    e_r = jnp.where(first, c_r, pltpu.roll(s_r, SUB - 1 if rev else 1, 0))
    e_i = jnp.where(first, c_i, pltpu.roll(s_i, SUB - 1 if rev else 1, 0))
    for k in range(STRAND):
        t_r, t_i = tab(pw_ref, k * SUB)
        p_r, p_i = _cmul(t_r, t_i, e_r, e_i)
        lr[k], li[k] = lr[k] + p_r, li[k] + p_i
    return lr, li, (s_r[edge:edge + 1], s_i[edge:edge + 1]), (e_r, e_i)


SCAN_GROUP = 6


def _lane_cols(o, j):
    col = o * 1024 + j * LANES
    return col, slice(col, col + LANES), slice(col + 512, col + 512 + LANES)


def s5_forward(u, bm, cm, pw, q, rev, name):
    T, G = SCAN_T, SCAN_GROUP
    n_chunks = u.shape[0] // T
    n_steps = n_chunks // G
    order = list(range(G - 1, -1, -1) if rev else range(G))

    def pos(i):
        return (n_steps - 1 - i) if rev else i

    def body(u_ref, bm_ref, cm_ref, pw_ref, q_ref, y_ref, cin_ref, carry):
        @pl.when(pl.program_id(0) == 0)
        def _():
            carry[...] = jnp.zeros_like(carry)

        uv = u_ref[...]
        for o in range(OCTETS):
            bu = jnp.dot(uv[:, o * LANES:(o + 1) * LANES], bm_ref[o], preferred_element_type=F32)
            hr = [[None] * 4 for _ in range(G)]
            hi = [[None] * 4 for _ in range(G)]
            for j in range(4):
                col, sl_r, sl_i = _lane_cols(o, j)
                cr, ci = carry[0:1, sl_r], carry[0:1, sl_i]
                for g in order:
                    rows = slice(g * T, (g + 1) * T)
                    cin_ref[g, 0:1, sl_r] = cr
                    cin_ref[g, 0:1, sl_i] = ci
                    xr, xi, (cr, ci), _ = _scan_strands(
                        bu[rows, j * LANES:(j + 1) * LANES], bu[rows, 512 + j * LANES:512 + (j + 1) * LANES],
                        pw_ref, q_ref, col, rev, False, cr, ci)
                    hr[g][j] = jnp.concatenate(xr, axis=0)
                    hi[g][j] = jnp.concatenate(xi, axis=0)
                carry[0:1, sl_r] = cr
                carry[0:1, sl_i] = ci
            h = jnp.concatenate([jnp.concatenate(hr[g] + hi[g], axis=1) for g in range(G)], axis=0).astype(BF16)
            y = jnp.dot(h, cm_ref[o], preferred_element_type=F32)
            for g in range(G):
                y_ref[g * T:(g + 1) * T, o * LANES:(o + 1) * LANES] = _to_token_order(y[g * T:(g + 1) * T])

    whole3 = lambda s: pl.BlockSpec(s, lambda i: (0, 0, 0))
    whole2 = lambda s: pl.BlockSpec(s, lambda i: (0, 0))
    rows_spec = pl.BlockSpec((G * T, S5_WIDTH), lambda i: (pos(i), 0))
    return _call(
        body, name,
        [jax.ShapeDtypeStruct((n_chunks * T, S5_WIDTH), F32), jax.ShapeDtypeStruct((n_chunks, 1, 2 * N_STATE), F32)],
        grid=(n_steps,),
        in_specs=[rows_spec, whole3(bm.shape), whole3(cm.shape), whole2(pw.shape), whole2(q.shape)],
        out_specs=[rows_spec, pl.BlockSpec((G, 1, 2 * N_STATE), lambda i: (pos(i), 0, 0))],
        scratch=[pltpu.VMEM((1, 2 * N_STATE), F32)], sem=("arbitrary",))(u, bm, cm, pw, q)


def s5_backward(u, dy, cin, bm, cm, pw_h, q_h, pw_l, q_l, rev, name):
    T, G = SCAN_T, SCAN_GROUP
    n_chunks = u.shape[0] // T
    n_steps = n_chunks // G
    adjoint_order = list(range(G) if rev else range(G - 1, -1, -1))

    def pos(i):
        return i if rev else (n_steps - 1 - i)

    def body(u_ref, dy_ref, cin_ref, bm_ref, cm_ref, pwh_ref, qh_ref, pwl_ref, ql_ref, du_ref, dbm_ref, dcm_ref,
             da_ref, lcarry):
        @pl.when(pl.program_id(0) == 0)
        def _():
            lcarry[...] = jnp.zeros_like(lcarry)
            dbm_ref[...] = jnp.zeros_like(dbm_ref)
            dcm_ref[...] = jnp.zeros_like(dcm_ref)
            da_ref[...] = jnp.zeros_like(da_ref)

        uv = u_ref[...]
        dyv = dy_ref[...]
        for o in range(OCTETS):
            u_o = uv[:, o * LANES:(o + 1) * LANES]
            dy_o = dyv[:, o * LANES:(o + 1) * LANES]
            bu = jnp.dot(u_o, bm_ref[o], preferred_element_type=F32)
            gy = lax.dot_general(dy_o, cm_ref[o], (((1,), (1,)), ((), ())), preferred_element_type=F32)
            hs = [[None] * 8 for _ in range(G)]
            ls = [[None] * 8 for _ in range(G)]
            for j in range(4):
                col, sl_r, sl_i = _lane_cols(o, j)
                b_r = slice(j * LANES, (j + 1) * LANES)
                b_i = slice(512 + j * LANES, 512 + (j + 1) * LANES)
                l_r, l_i = lcarry[0:1, sl_r], lcarry[0:1, sl_i]
                acc_r = acc_i = None
                for g in adjoint_order:
                    rows = slice(g * T, (g + 1) * T)
                    xr, xi, _, (e_r, e_i) = _scan_strands(bu[rows, b_r], bu[rows, b_i], pwh_ref, qh_ref, col, rev, False,
                                                          cin_ref[g, 0:1, sl_r], cin_ref[g, 0:1, sl_i])
                    ar_, ai_, (l_r, l_i), _ = _scan_strands(gy[rows, b_r], gy[rows, b_i], pwl_ref, ql_ref, col, not rev,
                                                            True, l_r, l_i)
                    for k in range(STRAND):
                        kp = k + 1 if rev else k - 1
                        p_r, p_i = (e_r, e_i) if not 0 <= kp < STRAND else (xr[kp], xi[kp])
                        t_r = ar_[k] * p_r + ai_[k] * p_i
                        t_i = ai_[k] * p_r - ar_[k] * p_i
                        acc_r, acc_i = (t_r, t_i) if acc_r is None else (acc_r + t_r, acc_i + t_i)
                    hs[g][j], hs[g][4 + j] = jnp.concatenate(xr, axis=0), jnp.concatenate(xi, axis=0)
                    ls[g][j], ls[g][4 + j] = jnp.concatenate(ar_, axis=0), jnp.concatenate(ai_, axis=0)
                lcarry[0:1, sl_r] = l_r
                lcarry[0:1, sl_i] = l_i
                da_ref[0:1, sl_r] += jnp.sum(acc_r, axis=0, keepdims=True)
                da_ref[0:1, sl_i] += jnp.sum(acc_i, axis=0, keepdims=True)
            h = jnp.concatenate([jnp.concatenate(hs[g], axis=1) for g in range(G)], axis=0).astype(BF16)
            lam = jnp.concatenate([jnp.concatenate(ls[g], axis=1) for g in range(G)], axis=0).astype(BF16)
            du = lax.dot_general(lam, bm_ref[o], (((1,), (1,)), ((), ())), preferred_element_type=F32)
            for g in range(G):
                du_ref[g * T:(g + 1) * T, o * LANES:(o + 1) * LANES] = _to_token_order(du[g * T:(g + 1) * T])
            dbm_ref[o] += lax.dot_general(u_o, lam, (((0,), (0,)), ((), ())), preferred_element_type=F32)
            dcm_ref[o] += lax.dot_general(h, dy_o, (((0,), (0,)), ((), ())), preferred_element_type=F32)

    whole3 = lambda s: pl.BlockSpec(s, lambda i: (0, 0, 0))
    whole2 = lambda s: pl.BlockSpec(s, lambda i: (0, 0))
    rows_spec = pl.BlockSpec((G * T, S5_WIDTH), lambda i: (pos(i), 0))
    return _call(
        body, name,
        [jax.ShapeDtypeStruct((n_chunks * T, S5_WIDTH), F32), jax.ShapeDtypeStruct(bm.shape, F32),
         jax.ShapeDtypeStruct(cm.shape, F32), jax.ShapeDtypeStruct((1, 2 * N_STATE), F32)],
        grid=(n_steps,),
        in_specs=[rows_spec, rows_spec, pl.BlockSpec((G, 1, 2 * N_STATE), lambda i: (pos(i), 0, 0)),
                  whole3(bm.shape), whole3(cm.shape), whole2(pw_h.shape), whole2(q_h.shape), whole2(pw_l.shape),
                  whole2(q_l.shape)],
        out_specs=[rows_spec, whole3(bm.shape), whole3(cm.shape), whole2((1, 2 * N_STATE))],
        scratch=[pltpu.VMEM((1, 2 * N_STATE), F32)], sem=("arbitrary",))(u, dy, cin, bm, cm, pw_h, q_h, pw_l, q_l)


def s5_tables(ar, ai, ls):
    def body(ar_ref, ai_ref, ls_ref, *outs):
        dt = jnp.exp(ls_ref[...])
        k = 0
        for n_rows, step in ((STRAND, 1.0), (SUB, float(STRAND))):
            row = lax.broadcasted_iota(jnp.int32, (n_rows, 1), 0)
            for m_int in (row + 1, n_rows - row):
                m = m_int.astype(F32) * step
                mag = jnp.exp(m * (ar_ref[...] * dt))
                ang = m * (ai_ref[...] * dt)
                outs[k][...] = mag * jnp.cos(ang)
                outs[k + 1][...] = mag * jnp.sin(ang)
                k += 2

    vec = pl.BlockSpec((None, 1, N_STATE), lambda d: (d, 0, 0))
    tab = lambda n: pl.BlockSpec((None, n, N_STATE), lambda d: (d, 0, 0))
    shp = lambda n: jax.ShapeDtypeStruct((2, n, N_STATE), F32)
    sizes = [STRAND] * 4 + [SUB] * 4
    return _call(body, "s5_tables", [shp(n) for n in sizes], grid=(2,), in_specs=[vec, vec, vec],
                 out_specs=[tab(n) for n in sizes], sem=("arbitrary",))(ar, ai, ls)


def f_discretize(a_re, a_im, ls, b_re, b_im):
    dt = jnp.exp(ls)
    mag = jnp.exp(a_re * dt)
    ab_re = mag * jnp.cos(a_im * dt)
    ab_im = mag * jnp.sin(a_im * dt)
    p = ab_re - 1.0
    q = ab_im
    den = a_re * a_re + a_im * a_im
    k_re = ((p * a_re + q * a_im) / den)[None]
    k_im = ((q * a_re - p * a_im) / den)[None]
    return ab_re, ab_im, k_re * b_re - k_im * b_im, k_re * b_im + k_im * b_re


def _disc_specs():
    a = pl.BlockSpec((None, S5_GROUPS, S5_STATE), lambda d: (d, 0, 0))
    s = pl.BlockSpec((None, S5_GROUPS, 1), lambda d: (d, 0, 0))
    b = pl.BlockSpec((None, S5_GROUP, S5_GROUPS, S5_STATE), lambda d: (d, 0, 0, 0))
    return a, s, b


def s5_discretize(a_re, a_im, ls, b_re, b_im):
    def body(ar, ai, l, br, bi, obr, obi):
        _, _, r, i = f_discretize(ar[...], ai[...], l[...], br[...], bi[...])
        obr[...] = r
        obi[...] = i

    a, s, b = _disc_specs()
    return _call(body, "s5_discretize", [jax.ShapeDtypeStruct(b_re.shape, F32)] * 2, grid=(2,),
                 in_specs=[a, a, s, b, b], out_specs=[b, b], sem=("arbitrary",))(a_re, a_im, ls, b_re, b_im)


def s5_discretize_bwd(a_re, a_im, ls, b_re, b_im, dab_re, dab_im, dbb_re, dbb_im):
    def body(ar, ai, l, br, bi, c0, c1, c2, c3, o0, o1, o2, o3, o4):
        _, vjp = jax.vjp(f_discretize, ar[...], ai[...], l[...], br[...], bi[...])
        outs = vjp((c0[...], c1[...], c2[...], c3[...]))
        for ref, val in zip((o0, o1, o2, o3, o4), outs):
            ref[...] = val

    a, s, b = _disc_specs()
    shapes = [jax.ShapeDtypeStruct(t.shape, F32) for t in (a_re, a_im, ls, b_re, b_im)]
    return _call(body, "s5_discretize_bwd", shapes, grid=(2,), in_specs=[a, a, s, b, b, a, a, b, b],
                 out_specs=[a, a, s, b, b], sem=("arbitrary",))(a_re, a_im, ls, b_re, b_im, dab_re, dab_im,
                                                                 dbb_re, dbb_im)


def _conv_taps(s_ref, base, n_rows):
    n = n_rows + 2 * CONV_PAD
    ext = s_ref[pl.ds(base, n), :]
    col = (lax.broadcasted_iota(jnp.int32, (n, 1), 0) + (2 * GRID_W - CONV_PAD)) % GRID_W
    left = jnp.where(col == 0, 0.0, pltpu.roll(ext, 1, 0))
    right = jnp.where(col == GRID_W - 1, 0.0, pltpu.roll(ext, n - 1, 0))
    return left, ext, right


def _conv_apply(taps, w_ref, n_rows, flip):
    out = None
    for i in range(3):
        wi = 2 - i if flip else i
        start = CONV_PAD + (i - 1) * GRID_W
        for j in range(3):
            wj = 2 - j if flip else j
            term = w_ref[wi * 3 + wj:wi * 3 + wj + 1, :] * taps[j][start:start + n_rows]
            out = term if out is None else out + term
    return out


def _conv_fill(dst_ref, src_ref, n_tok):
    zeros = jnp.zeros((CONV_PAD, LANES), F32)
    dst_ref[0:CONV_PAD, :] = zeros
    dst_ref[CONV_PAD + n_tok:2 * CONV_PAD + n_tok, :] = zeros

    def step(r, carry):
        base = pl.multiple_of(r * CONV_ROWS, CONV_ROWS)
        dst_ref[pl.ds(base + CONV_PAD, CONV_ROWS), :] = src_ref[pl.ds(base, CONV_ROWS), :].astype(F32)
        return carry

    lax.fori_loop(0, n_tok // CONV_ROWS, step, 0)


def conv_forward(up, wg, wv, bias):
    n_tok = up.shape[0]
    nb = FFN_HIDDEN // LANES

    def body(ug_ref, uv_ref, wg_ref, wv_ref, bg_ref, bv_ref, act_ref, gate_ref, val_ref, sg, sv):
        _conv_fill(sg, ug_ref, n_tok)
        _conv_fill(sv, uv_ref, n_tok)

        def step(r, carry):
            base = pl.multiple_of(r * CONV_ROWS, CONV_ROWS)
            gate = _conv_apply(_conv_taps(sg, base, CONV_ROWS), wg_ref, CONV_ROWS, False) + bg_ref[...]
            val = _conv_apply(_conv_taps(sv, base, CONV_ROWS), wv_ref, CONV_ROWS, False) + bv_ref[...]
            act_ref[pl.ds(base, CONV_ROWS), :] = (gate * jax.nn.sigmoid(gate) * val).astype(BF16)
            gate_ref[pl.ds(base, CONV_ROWS), :] = gate.astype(BF16)
            val_ref[pl.ds(base, CONV_ROWS), :] = val.astype(BF16)
            return carry

        lax.fori_loop(0, n_tok // CONV_ROWS, step, 0)

    col = lambda off: pl.BlockSpec((n_tok, LANES), lambda k: (0, k + off))
    wsp = lambda off: pl.BlockSpec((16, LANES), lambda k: (0, k + off))
    bsp = lambda off: pl.BlockSpec((1, LANES), lambda k: (0, k + off))
    pad = pltpu.VMEM((n_tok + 2 * CONV_PAD, LANES), F32)
    half = jax.ShapeDtypeStruct((n_tok, FFN_HIDDEN), BF16)
    return _call(body, "conv_forward", [half, half, half], grid=(nb,),
                 in_specs=[col(0), col(nb), wsp(0), wsp(0), bsp(0), bsp(nb)], out_specs=[col(0), col(0), col(0)],
                 scratch=[pad, pad], sem=("arbitrary",))(up, up, wg, wv, bias, bias)


def conv_backward(up, gate, val, dact, wg, wv):
    n_tok = up.shape[0]
    nb = FFN_HIDDEN // LANES
    n_steps = n_tok // CONV_ROWS

    def body(ug_ref, uv_ref, gate_ref, val_ref, da_ref, wg_ref, wv_ref, dup_ref, dwg_ref, dwv_ref, sg, sv, sdg, sdv):
        _conv_fill(sg, ug_ref, n_tok)
        _conv_fill(sv, uv_ref, n_tok)
        zeros = jnp.zeros((CONV_PAD, LANES), F32)
        for s_ref in (sdg, sdv):
            s_ref[0:CONV_PAD, :] = zeros
            s_ref[CONV_PAD + n_tok:2 * CONV_PAD + n_tok, :] = zeros
        dwg_ref[...] = jnp.zeros_like(dwg_ref)
        dwv_ref[...] = jnp.zeros_like(dwv_ref)

        def grads(r, carry):
            base = pl.multiple_of(r * CONV_ROWS, CONV_ROWS)
            taps_g = _conv_taps(sg, base, CONV_ROWS)
            taps_v = _conv_taps(sv, base, CONV_ROWS)
            gate = gate_ref[pl.ds(base, CONV_ROWS), :].astype(F32)
            val = val_ref[pl.ds(base, CONV_ROWS), :].astype(F32)
            d_act = da_ref[pl.ds(base, CONV_ROWS), :].astype(F32)
            sig = jax.nn.sigmoid(gate)
            d_gate = d_act * val * (sig * (1.0 + gate * (1.0 - sig)))
            d_val = d_act * (gate * sig)
            sdg[pl.ds(base + CONV_PAD, CONV_ROWS), :] = d_gate
            sdv[pl.ds(base + CONV_PAD, CONV_ROWS), :] = d_val
            for d_out, taps, dw_ref in ((d_gate, taps_g, dwg_ref), (d_val, taps_v, dwv_ref)):
                for i in range(3):
                    start = CONV_PAD + (i - 1) * GRID_W
                    for j in range(3):
                        k = i * 3 + j
                        dw_ref[k:k + 1, :] += jnp.sum(d_out * taps[j][start:start + CONV_ROWS], axis=0, keepdims=True)
                dw_ref[9:10, :] += jnp.sum(d_out, axis=0, keepdims=True)
            return carry

        lax.fori_loop(0, n_steps, grads, 0)

        def spread(r, carry):
            base = pl.multiple_of(r * CONV_ROWS, CONV_ROWS)
            dup_ref[0, pl.ds(base, CONV_ROWS), :] = _conv_apply(
                _conv_taps(sdg, base, CONV_ROWS), wg_ref, CONV_ROWS, True).astype(BF16)
            dup_ref[1, pl.ds(base, CONV_ROWS), :] = _conv_apply(
                _conv_taps(sdv, base, CONV_ROWS), wv_ref, CONV_ROWS, True).astype(BF16)
            return carry

        lax.fori_loop(0, n_steps, spread, 0)

    col = lambda off: pl.BlockSpec((n_tok, LANES), lambda k: (0, k + off))
    wsp = pl.BlockSpec((16, LANES), lambda k: (0, k))
    pad = pltpu.VMEM((n_tok + 2 * CONV_PAD, LANES), F32)
    both = jax.ShapeDtypeStruct((2, n_tok, FFN_HIDDEN), BF16)
    dw = jax.ShapeDtypeStruct((16, FFN_HIDDEN), F32)
    return _call(body, "conv_backward", [both, dw, dw], grid=(nb,),
                 in_specs=[col(0), col(nb), col(0), col(0), col(0), wsp, wsp],
                 out_specs=[pl.BlockSpec((2, n_tok, LANES), lambda k: (0, 0, k)), wsp, wsp],
                 scratch=[pad, pad, pad, pad], sem=("arbitrary",))(up, up, gate, val, dact, wg, wv)


def f_adamw(w, g, m, v):
    m = ADAM_B1 * m + (1.0 - ADAM_B1) * g
    v = ADAM_B2 * v + (1.0 - ADAM_B2) * jnp.square(g)
    m_hat = m / (1.0 - ADAM_B1 ** ADAM_STEP)
    v_hat = v / (1.0 - ADAM_B2 ** ADAM_STEP)
    delta = -ADAM_LR * (m_hat / (jnp.sqrt(v_hat) + ADAM_EPS) + ADAM_WD * w)
    return delta, m, v


def adamw(w, g, m, v, name):
    shape = w.shape
    cols = shape[-1]
    rows = w.size // cols
    two_d = [t.reshape(rows, cols) for t in (w, g, m, v)]
    tm = _tile(rows, 256, 8) if rows % 8 == 0 else rows
    outs = rowcall(f_adamw, name, rows, tm, [(t, cols, 0, 0) for t in two_d], [], [(cols, F32)] * 3, [])
    return tuple(o.reshape(shape) for o in outs)


def _place():
    return lax.axis_index("x"), lax.axis_index("y"), lax.axis_index("c")


_ANY = pl.BlockSpec(memory_space=pl.ANY)


def _fill_own(gathered, own, index):
    return lax.dynamic_update_slice(gathered, own[None], (index,) + (0,) * own.ndim)


def allgather_devices(v, name, copy_own=True):
    def body(v_ref, out_ref, send_sems, recv_sems, local_sem):
        x, y, c = _place()
        me, sibling = (x, y, c), (x, y, 1 - c)
        chips = [(1 - x, y), (x, 1 - y), (1 - x, 1 - y)]

        def slot(p):
            return out_ref.at[4 * p[0] + 2 * p[1] + p[2]]

        def copy(k, block, to, src=None):
            return pltpu.make_async_remote_copy(
                src_ref=slot(block) if src is None else src, dst_ref=slot(block),
                send_sem=send_sems.at[k], recv_sem=recv_sems.at[k], device_id=to, device_id_type=MESH)

        mine = pltpu.make_async_copy(v_ref, slot(me), local_sem)
        if copy_own:
            mine.start()
        first = [copy(0, me, sibling, src=v_ref)]
        first += [copy(1 + j, me, (*chip, c), src=v_ref) for j, chip in enumerate(chips)]
        for cp in first:
            cp.start()
        passed = [copy(4 + j, (*chip, c), sibling) for j, chip in enumerate(chips)]
        for j, chip in enumerate(chips):
            copy(1 + j, (*chip, c), me).wait_recv()
            passed[j].start()
        copy(0, sibling, me).wait_recv()
        for j, chip in enumerate(chips):
            copy(4 + j, (*chip, 1 - c), me).wait_recv()
        for cp in first + passed:
            cp.wait_send()
        if copy_own:
            mine.wait()

    return _call(body, name, jax.ShapeDtypeStruct((N_DEV,) + v.shape, v.dtype), in_specs=[_ANY], out_specs=_ANY,
                 scratch=[pltpu.SemaphoreType.DMA((7,)), pltpu.SemaphoreType.DMA((7,)), pltpu.SemaphoreType.DMA])(v)


def _other_chips(x, y):
    return [(1 - x, y), (x, 1 - y), (1 - x, 1 - y)]


def allgather_chips(vs, name):
    n = len(vs)
    shapes = [v.shape for v in vs]
    vs = [v.reshape((2, v.shape[0] // 2) + v.shape[1:]) for v in vs]

    def body(*refs):
        v_refs, o_refs = refs[:n], refs[n:2 * n]
        send_sems, recv_sems = refs[2 * n:]
        x, y, c = _place()
        sibling = (x, y, 1 - c)
        chips = _other_chips(x, y)

        def rows(a, chip, h):
            return o_refs[a].at[2 * chip[0] + chip[1], h]

        def copy(a, k, chip, h, to, src=None):
            return pltpu.make_async_remote_copy(
                src_ref=rows(a, chip, h) if src is None else src, dst_ref=rows(a, chip, h),
                send_sem=send_sems.at[6 * a + k], recv_sem=recv_sems.at[6 * a + k], device_id=to, device_id_type=MESH)

        first = [copy(a, j, (x, y), c, (*chip, c), src=v_refs[a].at[c])
                 for a in range(n) for j, chip in enumerate(chips)]
        for cp in first:
            cp.start()
        passed = []
        for j, chip in enumerate(chips):
            for a in range(n):
                copy(a, j, chip, c, (x, y, c)).wait_recv()
                passed.append(copy(a, 3 + j, chip, c, sibling))
                passed[-1].start()
        for j, chip in enumerate(chips):
            for a in range(n):
                copy(a, 3 + j, chip, 1 - c, (x, y, c)).wait_recv()
        for cp in first + passed:
            cp.wait_send()

    outs = _call(body, name, [jax.ShapeDtypeStruct((N_CHIPS,) + v.shape, v.dtype) for v in vs], in_specs=[_ANY] * n,
                 out_specs=[_ANY] * n,
                 scratch=[pltpu.SemaphoreType.DMA((6 * n,)), pltpu.SemaphoreType.DMA((6 * n,))])(*vs)
    return [o.reshape((N_CHIPS,) + s) for o, s in zip(outs, shapes)]


def grad_pair_swap(gs, name):
    n = len(gs)

    def body(*refs):
        g_refs, o_refs, send_sems, recv_sems = refs[:n], refs[n:2 * n], refs[2 * n], refs[2 * n + 1]
        x, y, c = _place()
        cps = [pltpu.make_async_remote_copy(
            src_ref=g_refs[a].at[:, 1 - c], dst_ref=o_refs[a], send_sem=send_sems.at[a], recv_sem=recv_sems.at[a],
            device_id=(x, y, 1 - c), device_id_type=MESH) for a in range(n)]
        for cp in cps:
            cp.start()
        for cp in cps:
            cp.wait()

    return _call(body, name, [jax.ShapeDtypeStruct((N_CHIPS,) + g.shape[2:], g.dtype) for g in gs],
                 in_specs=[_ANY] * n, out_specs=[_ANY] * n,
                 scratch=[pltpu.SemaphoreType.DMA((n,)), pltpu.SemaphoreType.DMA((n,))])(*gs)


def pair_sum(g, recv, core, name):
    r2, cols = recv.shape[1], recv.shape[2]
    tr = _tile(r2, 256, 8)
    nt = r2 // tr

    def body(c_ref, g_ref, r_ref, o_ref):
        o_ref[...] = (g_ref[...] + r_ref[...]).astype(o_ref.dtype)

    spec = pl.BlockSpec((None, tr, cols), lambda q, i, c_ref: (q, i, 0))
    grid_spec = pltpu.PrefetchScalarGridSpec(
        num_scalar_prefetch=1, grid=(N_CHIPS, nt),
        in_specs=[pl.BlockSpec((None, None, tr, cols), lambda q, i, c_ref: (q, c_ref[0], i, 0)), spec], out_specs=spec)
    return pl.pallas_call(body, name=name, out_shape=jax.ShapeDtypeStruct(recv.shape, BF16), grid_spec=grid_spec,
                          compiler_params=pltpu.CompilerParams(dimension_semantics=("arbitrary", "arbitrary"),
                                                               vmem_limit_bytes=VMEM_LIMIT_BYTES))(core, g, recv)


def grad_chip_exchange(ps, name):
    n = len(ps)

    def body(*refs):
        p_refs, o_refs = refs[:n], refs[n:2 * n]
        send_sems, recv_sems, local_sems = refs[2 * n:]
        x, y, c = _place()
        chips = _other_chips(x, y)
        me = 2 * x + y
        mine = [pltpu.make_async_copy(p_refs[a].at[me], o_refs[a].at[me], local_sems.at[a]) for a in range(n)]
        sends = [pltpu.make_async_remote_copy(
            src_ref=p_refs[a].at[2 * chip[0] + chip[1]], dst_ref=o_refs[a].at[me], send_sem=send_sems.at[3 * a + j],
            recv_sem=recv_sems.at[3 * a + j], device_id=(*chip, c), device_id_type=MESH)
            for a in range(n) for j, chip in enumerate(chips)]
        for cp in mine + sends:
            cp.start()
        for a in range(n):
            for j, chip in enumerate(chips):
                pltpu.make_async_remote_copy(
                    src_ref=p_refs[a].at[me], dst_ref=o_refs[a].at[2 * chip[0] + chip[1]],
                    send_sem=send_sems.at[3 * a + j], recv_sem=recv_sems.at[3 * a + j], device_id=(*chip, c),
                    device_id_type=MESH).wait_recv()
        for cp in sends:
            cp.wait_send()
        for cp in mine:
            cp.wait()

    return _call(body, name, [jax.ShapeDtypeStruct(p.shape, p.dtype) for p in ps], in_specs=[_ANY] * n,
                 out_specs=[_ANY] * n, scratch=[pltpu.SemaphoreType.DMA((3 * n,)), pltpu.SemaphoreType.DMA((3 * n,)),
                                                pltpu.SemaphoreType.DMA((n,))])(*ps)


def grad_half_swap(ss, name):
    n = len(ss)

    def body(*refs):
        s_refs, o_refs, send_sems, recv_sems = refs[:n], refs[n:2 * n], refs[2 * n], refs[2 * n + 1]
        x, y, c = _place()
        cps = [pltpu.make_async_remote_copy(
            src_ref=s_refs[a], dst_ref=o_refs[a], send_sem=send_sems.at[a], recv_sem=recv_sems.at[a],
            device_id=(x, y, 1 - c), device_id_type=MESH) for a in range(n)]
        for cp in cps:
            cp.start()
        for cp in cps:
            cp.wait()

    return _call(body, name, [jax.ShapeDtypeStruct(s.shape, s.dtype) for s in ss], in_specs=[_ANY] * n,
                 out_specs=[_ANY] * n, scratch=[pltpu.SemaphoreType.DMA((n,)), pltpu.SemaphoreType.DMA((n,))])(*ss)


ROW_TILE = 256

BIG_SHARDS = (("w_in", (D_MODEL, 896), 1), ("s5_w_glu", (128, S5_WIDTH), 0), ("w_proj_a", (S5_WIDTH, 256), 1),
              ("w_proj_b", (SGU_WIDTH, 256), 1), ("w_out", (256, D_MODEL), 0), ("w_up", (D_MODEL, 1408), 1),
              ("w_down", (704, D_MODEL), 0))
SMALL_PARAMS = ("g_mix", "s5_a_re", "s5_a_im", "s5_log_step", "s5_b_re", "s5_b_im", "s5_c_re", "s5_c_im", "s5_d",
                "s5_b_glu", "sgu_ln_g", "sgu_ln_b", "sgu_w", "sgu_b", "b_gate", "g_ffn", "conv_b", "g_final")
PACK_COLS = 1024


def _rows_of(n):
    return -(-n // PACK_COLS)


def _pack_rows(arrays, total_rows, dtype):
    parts = []
    used = 0
    for a in arrays:
        r = _rows_of(a.size)
        parts.append(jnp.pad(a.reshape(-1).astype(dtype), (0, r * PACK_COLS - a.size)).reshape(r, PACK_COLS))
        used += r
    if total_rows > used:
        parts.append(jnp.zeros((total_rows - used, PACK_COLS), dtype))
    return jnp.concatenate(parts, axis=0)


def _unpack_rows(packed, shapes, row0=0):
    out = []
    for s in shapes:
        n = 1
        for d in s:
            n *= d
        r = _rows_of(n)
        out.append(packed[row0:row0 + r].reshape(-1)[:n].reshape(s))
        row0 += r
    return out


def _octet_major(re, im):
    parts = []
    for o in range(OCTETS):
        parts += [re[:, o * 512:(o + 1) * 512], im[:, o * 512:(o + 1) * 512]]
    return jnp.concatenate(parts, axis=1)


def _octet_split(v):
    v = v.reshape(OCTETS, 2, 512)
    return v[:, 0].reshape(N_STATE), v[:, 1].reshape(N_STATE)


def _s5_bmat(bb):
    t = bb.reshape(S5_GROUP, OCTETS, 8, 1, S5_STATE) * jnp.eye(8, dtype=F32)[None, None, :, :, None]
    return jnp.transpose(t, (1, 2, 0, 3, 4)).reshape(OCTETS, LANES, 512)


def _s5_bmat_t(dm):
    t = dm.reshape(OCTETS, 8, S5_GROUP, 8, S5_STATE) * jnp.eye(8, dtype=F32)[None, :, None, :, None]
    return jnp.transpose(t.sum(axis=3), (2, 0, 1, 3)).reshape(S5_GROUP, S5_GROUPS, S5_STATE)


def _s5_cmat(c):
    t = c.reshape(OCTETS, 8, 1, S5_GROUP, S5_STATE) * jnp.eye(8, dtype=F32)[None, :, :, None, None]
    return jnp.transpose(t, (0, 2, 4, 1, 3)).reshape(OCTETS, 512, LANES)


def _s5_cmat_t(dm):
    t = dm.reshape(OCTETS, 8, S5_STATE, 8, S5_GROUP) * jnp.eye(8, dtype=F32)[None, :, None, :, None]
    return jnp.transpose(t.sum(axis=1), (0, 2, 3, 1)).reshape(S5_GROUPS, S5_GROUP, S5_STATE)


def local_step(x, ctx, tgt, mod, modc, W):
    n_tok, n_ctx = x.shape[0], ctx.shape[0]
    tm = ROW_TILE
    D = D_MODEL
    sh1, sc1, ga1, sh2, sc2, ga2 = [mod[:, k * D:(k + 1) * D] for k in range(N_MOD)]
    sh1c, sc1c = modc[:, :D], modc[:, D:2 * D]
    g_mix, g_ffn, g_final = W["g_mix"], W["g_ffn"], W["g_final"]
    w_in = W["w_in"]
    w_in_u = w_in[0][:, :S5_WIDTH]

    h = rowcall(f_modulate, "mod1", n_tok, tm, [(x, D, 0, 0)], [g_mix, sc1, sh1], [(D, BF16)], [])[0]
    hc = rowcall(f_modulate, "mod1_ctx", n_ctx, tm, [(ctx, D, 0, 0)], [g_mix, sc1c, sh1c], [(D, BF16)], [])[0]
    proj = matmul(h, w_in, "nn", BF16, "proj_in", shards=True)
    uc = matmul(hc, w_in_u, "nn", BF16, "proj_in_ctx")
    u_lat = proj[:, :S5_WIDTH]
    u_s5 = (to_strand_order(jnp.concatenate([uc, u_lat], axis=0)), to_strand_order(jnp.concatenate([u_lat, uc], axis=0)))

    a_re, a_im, ls = W["s5_a_re"], W["s5_a_im"], W["s5_log_step"][..., None]
    b_re_t = jnp.transpose(W["s5_b_re"], (0, 3, 1, 2))
    b_im_t = jnp.transpose(W["s5_b_im"], (0, 3, 1, 2))
    bb_re, bb_im = s5_discretize(a_re, a_im, ls, b_re_t, b_im_t)
    ls_rep = jnp.repeat(W["s5_log_step"], S5_STATE, axis=1).reshape(2, 1, N_STATE)
    tabs = s5_tables(a_re.reshape(2, 1, N_STATE), a_im.reshape(2, 1, N_STATE), ls_rep)
    pw, qt, bm, cm = [], [], [], []
    for d in range(2):
        pw.append(tuple(jnp.repeat(_octet_major(tabs[k][d], tabs[k + 1][d]), SUB, axis=0) for k in (0, 2)))
        qt.append(tuple(_octet_major(tabs[k][d], tabs[k + 1][d]) for k in (4, 6)))
        bm.append(jnp.concatenate([_s5_bmat(bb_re[d]), _s5_bmat(bb_im[d])], axis=2).astype(BF16))
        cm.append(jnp.concatenate([_s5_cmat(W["s5_c_re"][d]), -_s5_cmat(W["s5_c_im"][d])], axis=1).astype(BF16))
    y0, cin0 = s5_forward(u_s5[0], bm[0], cm[0], pw[0][0], qt[0][0], False, "s5_fwd0")
    y1, cin1 = s5_forward(u_s5[1], bm[1], cm[1], pw[1][1], qt[1][1], True, "s5_fwd1")

    mix_rows = [(proj, 512, 0, 0), (y0, 512, 0, n_ctx // tm), (y1, 512, 0, 0)] + \
               [(proj, 512, k, 0) for k in range(1, 7)]
    mix_vecs = [W["s5_d"], W["s5_w_glu"], W["s5_b_glu"], W["sgu_ln_g"], W["sgu_ln_b"], W["sgu_w"],
                jnp.transpose(W["sgu_b"]), W["w_proj_a"], W["w_proj_b"], W["b_gate"]]
    mrg = rowcall(f_mixer, "mixer", n_tok, tm, mix_rows, mix_vecs, [(D, BF16)], [])[0]
    o = matmul(mrg, W["w_out"], "nn", F32, "proj_out")
    x1, h2 = rowcall(f_resid_mod, "resid_mod2", n_tok, tm, [(x, D, 0, 0), (o, D, 0, 0)], [ga1, g_ffn, sc2, sh2],
                     [(D, F32), (D, BF16)], [])
    up = matmul(h2, W["w_up"], "nn", BF16, "ffn_up", shards=True)
    conv_w = W["conv_w"].reshape(9, 2 * FFN_HIDDEN)
    wg = jnp.pad(conv_w[:, :FFN_HIDDEN], ((0, 7), (0, 0)))
    wv = jnp.pad(conv_w[:, FFN_HIDDEN:], ((0, 7), (0, 0)))
    act, gate, val = conv_forward(up, wg, wv, W["conv_b"])
    dn = matmul(act, W["w_down"], "nn", F32, "ffn_down")

    def final_fn(x1_, dn_, tgt_, ga2_, gf_):
        loss, (dx1_, ddn_, dga2_, dgf_) = jax.value_and_grad(f_final_loss, argnums=(0, 1, 3, 4))(
            x1_, dn_, tgt_, ga2_, gf_)
        return dx1_, ddn_, loss.reshape(1, 1), dga2_, dgf_

    dx2, ddn, loss, d_ga2, d_gfinal = rowcall(
        final_fn, "final_loss", n_tok, tm, [(x1, D, 0, 0), (dn, D, 0, 0), (tgt, D, 0, 0)], [ga2, g_final],
        [(D, F32), (D, BF16)], [(1, 1), (1, D), (1, D)])

    dact = matmul(ddn, W["w_down"], "nt", BF16, "ffn_down_dx")
    d_w_down = matmul(act, ddn, "tn", F32, "ffn_down_dw")
    dup, dwg, dwv = conv_backward(up, gate, val, dact, wg, wv)
    d_conv_w = jnp.concatenate([dwg[:9], dwv[:9]], axis=1).reshape(3, 3, 2 * FFN_HIDDEN)
    d_conv_b = jnp.concatenate([dwg[9:10], dwv[9:10]], axis=1)
    dh2 = matmul(dup, W["w_up"], "nt", BF16, "ffn_up_dx", shards=True, halves=True)
    d_w_up = matmul(h2, dup, "tn", F32, "ffn_up_dw", shards=True, halves=True)

    def resid_bwd(x_, o_, dx1_, dh2_, ga_, g_, sc_, sh_):
        _, vjp = jax.vjp(f_resid_mod, x_, o_, ga_, g_, sc_, sh_)
        return vjp((dx1_, dh2_))

    dxa, do, d_ga1, d_gffn, d_sc2, d_sh2 = rowcall(
        resid_bwd, "resid_mod2_bwd", n_tok, tm, [(x, D, 0, 0), (o, D, 0, 0), (dx2, D, 0, 0), (dh2, D, 0, 0)],
        [ga1, g_ffn, sc2, sh2], [(D, F32), (D, BF16)], [(1, D)] * 4)

    dmrg = matmul(do, W["w_out"], "nt", BF16, "proj_out_dx")
    d_w_out = matmul(mrg, do, "tn", F32, "proj_out_dw")

    def mixer_bwd(*args):
        rows, dm, vecs = args[:9], args[9], [v.astype(F32) for v in args[10:]]
        _, vjp = jax.vjp(f_mixer, *rows, *vecs)
        g = vjp(dm)
        return (jnp.concatenate([g[0]] + list(g[3:9]), axis=1), g[1]) + tuple(g[9:])

    mb = rowcall(mixer_bwd, "mixer_bwd", n_tok, tm, mix_rows + [(dmrg, D, 0, 0)], mix_vecs,
                 [(w_in.shape[2] * N_CHIPS, BF16), (512, BF16)], [v.shape for v in mix_vecs])
    dproj, dys = mb[:2]
    d_s5d, d_w_glu, d_b_glu, d_ln_g, d_ln_b, d_sgu_w, d_sgu_bt, d_w_pa, d_w_pb, d_b_gate = mb[2:]

    zc = jnp.zeros((n_ctx, S5_WIDTH), BF16)
    dy_s5 = (to_strand_order(jnp.concatenate([zc, dys], axis=0)), to_strand_order(jnp.concatenate([dys, zc], axis=0)))
    du0, dbm0, dcm0, da0 = s5_backward(u_s5[0], dy_s5[0], cin0, bm[0], cm[0], pw[0][0], qt[0][0], pw[0][1], qt[0][1],
                                       False, "s5_bwd0")
    du1, dbm1, dcm1, da1 = s5_backward(u_s5[1], dy_s5[1], cin1, bm[1], cm[1], pw[1][1], qt[1][1], pw[1][0], qt[1][0],
                                       True, "s5_bwd1")
    dproj = add_into_columns(dproj, [(du0, n_ctx // tm), (du1, 0)], S5_WIDTH, "du_sum", tm)
    du_c = rowcall(lambda a, b: a + b, "du_sum_ctx", n_ctx, tm, [(du0, 512, 0, 0), (du1, 512, 0, n_tok // tm)],
                   [], [(512, BF16)], [])[0]

    dab_re, dab_im, dbb_re, dbb_im, d_c_re, d_c_im = [], [], [], [], [], []
    for dbm, dcm, da in ((dbm0, dcm0, da0), (dbm1, dcm1, da1)):
        r, i = _octet_split(da)
        dab_re.append(r.reshape(S5_GROUPS, S5_STATE))
        dab_im.append(i.reshape(S5_GROUPS, S5_STATE))
        dbb_re.append(_s5_bmat_t(dbm[:, :, :512]))
        dbb_im.append(_s5_bmat_t(dbm[:, :, 512:]))
        d_c_re.append(_s5_cmat_t(dcm[:, :512]))
        d_c_im.append(-_s5_cmat_t(dcm[:, 512:]))
    d_a_re, d_a_im, d_ls, d_b_re_t, d_b_im_t = s5_discretize_bwd(
        a_re, a_im, ls, b_re_t, b_im_t, jnp.stack(dab_re), jnp.stack(dab_im), jnp.stack(dbb_re), jnp.stack(dbb_im))

    dh = matmul(dproj, w_in, "nt", BF16, "proj_in_dx", shards=True)
    dhc = matmul(du_c, w_in_u, "nt", BF16, "proj_in_ctx_dx")
    du_c_wide = jnp.pad(du_c, ((0, 0), (0, dproj.shape[1] - S5_WIDTH)))
    d_w_in_c = matmul(hc, du_c_wide, "tn", F32, "proj_in_ctx_dw", shards=True)
    d_w_in = matmul(h, dproj, "tn", F32, "proj_in_dw", shards=True, init=d_w_in_c)

    def mod_bwd_ctx(x_, dh_, g_, sc_, sh_):
        _, vjp = jax.vjp(f_modulate, x_, g_, sc_, sh_)
        return vjp(dh_)[1:]

    d_gmix_c, d_sc1c, d_sh1c = rowcall(mod_bwd_ctx, "mod1_ctx_bwd", n_ctx, tm, [(ctx, D, 0, 0), (dhc, D, 0, 0)],
                                       [g_mix, sc1c, sh1c], [], [(1, D)] * 3)

    def mod_bwd(x_, dh_, dxa_, g_, sc_, sh_):
        _, vjp = jax.vjp(f_modulate, x_, g_, sc_, sh_)
        dx_, dg_, dsc_, dsh_ = vjp(dh_)
        return dx_ + dxa_, dg_, dsc_, dsh_

    zero_d = jnp.zeros((1, D), F32)
    grad_x, d_gmix, d_sc1, d_sh1 = rowcall(
        mod_bwd, "mod1_bwd", n_tok, tm, [(x, D, 0, 0), (dh, D, 0, 0), (dxa, D, 0, 0)], [g_mix, sc1, sh1],
        [(D, F32)], [(1, D)] * 3, ainit=[d_gmix_c, zero_d, zero_d])

    grads = {
        "dmod": jnp.concatenate([d_sh1, d_sc1, d_ga1, d_sh2, d_sc2, d_ga2], axis=1),
        "dmodc": jnp.concatenate([d_sh1c, d_sc1c], axis=1),
        "g_mix": d_gmix,
        "s5_a_re": d_a_re, "s5_a_im": d_a_im, "s5_log_step": d_ls[..., 0],
        "s5_b_re": jnp.transpose(d_b_re_t, (0, 2, 3, 1)), "s5_b_im": jnp.transpose(d_b_im_t, (0, 2, 3, 1)),
        "s5_c_re": jnp.stack(d_c_re), "s5_c_im": jnp.stack(d_c_im), "s5_d": d_s5d, "s5_b_glu": d_b_glu,
        "sgu_ln_g": d_ln_g, "sgu_ln_b": d_ln_b, "sgu_w": d_sgu_w, "sgu_b": jnp.transpose(d_sgu_bt),
        "b_gate": d_b_gate, "g_ffn": d_gffn, "conv_b": d_conv_b, "g_final": d_gfinal, "conv_w": d_conv_w,
        "w_in": d_w_in, "s5_w_glu": d_w_glu, "w_proj_a": d_w_pa, "w_proj_b": d_w_pb, "w_out": d_w_out,
        "w_up": d_w_up, "w_down": d_w_down,
    }
    return loss, grad_x, grads


ADA_COLS = N_MOD * D_MODEL // N_CHIPS
MOD_ROWS = 16


def mod_forward(c16, w, b):
    n = w.shape[1]
    tn = 512

    def body(c_ref, w_ref, b_ref, o_ref):
        cv = c_ref[...]
        cs = cv * jax.nn.sigmoid(cv)
        o_ref[...] = jnp.dot(cs.astype(BF16), w_ref[...].astype(BF16), preferred_element_type=F32) + b_ref[...]

    return _call(body, "mod_forward", jax.ShapeDtypeStruct((MOD_ROWS, n), F32), grid=(n // tn,),
                 in_specs=[pl.BlockSpec((MOD_ROWS, D_MODEL), lambda j: (0, 0)),
                           pl.BlockSpec((D_MODEL, tn), lambda j: (0, j)), pl.BlockSpec((1, tn), lambda j: (0, j))],
                 out_specs=pl.BlockSpec((MOD_ROWS, tn), lambda j: (0, j)), sem=("arbitrary",))(c16, w, b)


def f_ada_outer(ct, dm):
    cs = ct * jax.nn.sigmoid(ct)
    acc = cs[:, 0:1] * dm[0:1]
    for k in range(1, 9):
        acc = acc + cs[:, k:k + 1] * dm[k:k + 1]
    return acc


def f_cctx_grad(z, p4):
    s = jax.nn.sigmoid(z)
    return (p4[0:1] + p4[1:2] + p4[2:3] + p4[3:4]) * (s + z * s * (1.0 - s))


WEIGHT_NAMES = ("c_ctx", "w_ada", "b_ada", "g_mix", "w_in", "s5_a_re", "s5_a_im", "s5_log_step", "s5_b_re",
                "s5_b_im", "s5_c_re", "s5_c_im", "s5_d", "s5_w_glu", "s5_b_glu", "sgu_ln_g", "sgu_ln_b", "sgu_w",
                "sgu_b", "w_proj_a", "w_proj_b", "b_gate", "w_out", "g_ffn", "w_up", "conv_w", "conv_b", "w_down",
                "g_final")
CONV_SHARD = 2 * FFN_HIDDEN // N_CHIPS
SMALL_PACK_ROWS = 512
SMALL_ROW0 = 58


def kernel(x, c, ctx, c_ctx, w_ada, b_ada, g_mix, w_in, s5_a_re, s5_a_im, s5_log_step, s5_b_re, s5_b_im, s5_c_re, s5_c_im, s5_d, s5_w_glu, s5_b_glu, sgu_ln_g, sgu_ln_b, sgu_w, sgu_b, w_proj_a, w_proj_b, b_gate, w_out, g_ffn, w_up, conv_w, conv_b, w_down, g_final, loss_target, m_c_ctx, m_w_ada, m_b_ada, m_g_mix, m_w_in, m_s5_a_re, m_s5_a_im, m_s5_log_step, m_s5_b_re, m_s5_b_im, m_s5_c_re, m_s5_c_im, m_s5_d, m_s5_w_glu, m_s5_b_glu, m_sgu_ln_g, m_sgu_ln_b, m_sgu_w, m_sgu_b, m_w_proj_a, m_w_proj_b, m_b_gate, m_w_out, m_g_ffn, m_w_up, m_conv_w, m_conv_b, m_w_down, m_g_final, v_c_ctx, v_w_ada, v_b_ada, v_g_mix, v_w_in, v_s5_a_re, v_s5_a_im, v_s5_log_step, v_s5_b_re, v_s5_b_im, v_s5_c_re, v_s5_c_im, v_s5_d, v_s5_w_glu, v_s5_b_glu, v_sgu_ln_g, v_sgu_ln_b, v_sgu_w, v_sgu_b, v_w_proj_a, v_w_proj_b, v_b_gate, v_w_out, v_g_ffn, v_w_up, v_conv_w, v_conv_b, v_w_down, v_g_final):
    given = dict(locals())
    wts = {n: given[n] for n in WEIGHT_NAMES}
    ms = {n: given["m_" + n] for n in WEIGHT_NAMES}
    vs = {n: given["v_" + n] for n in WEIGHT_NAMES}
    xi, yi, ci = _place()
    chip = 2 * xi + yi
    dev = 2 * chip + ci
    D = D_MODEL

    c8 = allgather_devices(jnp.pad(c, ((0, 7), (0, 0))), "gather_c")[:, 0, :]
    c16 = jnp.concatenate([c8, c_ctx[None], jnp.zeros((MOD_ROWS - 9, D), F32)], axis=0)
    b_shard = lax.dynamic_slice(b_ada, (0, chip * ADA_COLS), (1, ADA_COLS))
    mod_shard = mod_forward(c16, w_ada[0], b_shard)
    mod_all = allgather_devices(mod_shard, "gather_mod")
    mod_full = jnp.concatenate([mod_all[2 * q] for q in range(N_CHIPS)], axis=1)
    mod = lax.dynamic_slice(mod_full, (dev, 0), (1, N_MOD * D))
    modc = mod_full[8:9]

    big_names = [n for n, _, _ in BIG_SHARDS]
    conv_rows = jnp.pad(conv_w[0].reshape(9, CONV_SHARD), ((0, 7), (0, 0)))
    shards = [wts[n][0].astype(BF16) for n in big_names] + [conv_rows]
    gathered = allgather_chips(shards, "gather_weights")
    gathered = [_fill_own(t, s, chip) for t, s in zip(gathered, shards)]
    W = dict(zip(big_names, gathered[:-1]))
    for n, shape, axis in BIG_SHARDS:
        if axis == 0:
            W[n] = W[n].reshape(N_CHIPS * shape[0], shape[1])
    for n in ("w_proj_a", "w_proj_b"):
        W[n] = jnp.transpose(W[n], (1, 0, 2)).reshape(W[n].shape[1], -1)
    W["conv_w"] = jnp.transpose(gathered[-1][:, :9], (1, 0, 2)).reshape(3, 3, 2 * FFN_HIDDEN)
    for n in ("g_mix", "g_ffn", "s5_d", "s5_b_glu", "sgu_ln_g", "sgu_ln_b", "b_gate", "conv_b"):
        W[n] = wts[n]
    W["g_final"] = g_final[None]
    for n in ("s5_a_re", "s5_a_im", "s5_log_step", "s5_b_re", "s5_b_im", "s5_c_re", "s5_c_im", "sgu_w", "sgu_b"):
        W[n] = wts[n][0]

    loss_part, grad_x, g = local_step(x[0], ctx[0], loss_target[0], mod, modc, W)
    loss = lax.psum(loss_part[0, 0], ("x", "y", "c"))

    g_slots = []
    for n, shape, axis in BIG_SHARDS:
        if n in ("w_proj_a", "w_proj_b"):
            g_slots.append(jnp.transpose(g[n].reshape(shape[0], N_CHIPS, shape[1]), (1, 0, 2)))
        else:
            g_slots.append(g[n].reshape((N_CHIPS,) + shape))
    g_slots = [t.reshape(N_CHIPS, 2, t.shape[1] // 2, t.shape[2]) for t in g_slots]
    core = ci.astype(jnp.int32).reshape(1)
    from_sibling = grad_pair_swap(g_slots, "grad_pair_swap")
    pair = [pair_sum(gs, rv, core, "grad_pair_sum_" + n) for gs, rv, n in zip(g_slots, from_sibling, big_names)]
    from_chips = grad_chip_exchange(pair, "grad_chip_exchange")
    add2 = lambda a, b: a + b
    add4 = lambda a, b, c_, d: ((a + b) + c_) + d
    halves = []
    for fc, n in zip(from_chips, big_names):
        r2, cols = fc.shape[1], fc.shape[2]
        tr = _tile(r2, 256, 8)
        halves.append(rowcall(add4, "grad_chip_sum_" + n, r2, tr,
                              [(fc.reshape(N_CHIPS * r2, cols), cols, 0, q * r2 // tr) for q in range(N_CHIPS)], [],
                              [(cols, F32)], [])[0])
    others = grad_half_swap(halves, "grad_half_swap")
    big_grads = {n: jnp.where(ci == 0, jnp.concatenate([mine, other], axis=0), jnp.concatenate([other, mine], axis=0))
                 for n, mine, other in zip(big_names, halves, others)}

    small_pack = _pack_rows([g["dmod"], g["dmodc"], g["conv_w"]] + [g[n] for n in SMALL_PARAMS], SMALL_PACK_ROWS, F32)
    from_core = grad_half_swap([small_pack], "small_pair_swap")[0]
    small_pair = rowcall(add2, "small_pair_sum", SMALL_PACK_ROWS, 256,
                         [(small_pack, PACK_COLS, 0, 0), (from_core, PACK_COLS, 0, 0)], [], [(PACK_COLS, F32)], [])[0]
    small_chips = _fill_own(allgather_chips([small_pair], "gather_small_grads")[0], small_pair, chip)
    small_2d = small_chips.reshape(N_CHIPS * SMALL_PACK_ROWS, PACK_COLS)
    small_sum = rowcall(add4, "small_grad_sum", SMALL_PACK_ROWS, 256,
                        [(small_2d, PACK_COLS, 0, q * SMALL_PACK_ROWS // 256) for q in range(N_CHIPS)], [],
                        [(PACK_COLS, F32)], [])[0]
    dmod_all = allgather_devices(small_pack[0:8], "gather_dmod")[:, 0:N_MOD].reshape(N_DEV, N_MOD * D)
    dmod_sum = small_sum[0:N_MOD].reshape(1, N_MOD * D)
    dmodc_sum = jnp.pad(small_sum[N_MOD:N_MOD + 2].reshape(1, 2 * D), ((0, 0), (0, (N_MOD - 2) * D)))
    conv_grad = _unpack_rows(small_sum, [(3, 3, 2 * FFN_HIDDEN)], row0=N_MOD + 2)[0]
    small_grads = dict(zip(SMALL_PARAMS, _unpack_rows(small_sum, [wts[n].shape for n in SMALL_PARAMS], row0=SMALL_ROW0)))

    dm16 = jnp.concatenate([dmod_all, dmodc_sum, jnp.zeros((MOD_ROWS - 9, N_MOD * D), F32)], axis=0)
    dm_shard = lax.dynamic_slice(dm16, (0, chip * ADA_COLS), (MOD_ROWS, ADA_COLS))
    g_w_ada = rowcall(f_ada_outer, "w_ada_grad", D, 256, [(jnp.transpose(c16), MOD_ROWS, 0, 0)], [dm_shard],
                      [(ADA_COLS, F32)], [])[0]
    g_b_ada = rowcall(add2, "b_ada_grad", 1, 1, [(dmod_sum, N_MOD * D, 0, 0), (dmodc_sum, N_MOD * D, 0, 0)], [],
                      [(N_MOD * D, F32)], [])[0]
    dmc_rows = jnp.pad(dm_shard[8:9], ((0, 7), (0, 0)))
    cctx_part = matmul(dmc_rows, w_ada[0], "nt", F32, "c_ctx_partial")
    cctx_all = allgather_devices(cctx_part, "gather_c_ctx")
    cctx_4 = jnp.stack([cctx_all[2 * q, 0] for q in range(N_CHIPS)])
    g_c_ctx = rowcall(f_cctx_grad, "c_ctx_grad", 1, 1, [(c_ctx[None], D, 0, 0)], [cctx_4], [(D, F32)], [])[0]

    grads = dict(small_grads)
    grads.update(big_grads)
    grads["w_ada"] = g_w_ada
    grads["b_ada"] = g_b_ada
    grads["c_ctx"] = g_c_ctx
    grads["conv_w"] = lax.dynamic_slice(conv_grad, (0, 0, chip * CONV_SHARD), (3, 3, CONV_SHARD))
    grads = {n: grads[n].reshape(wts[n].shape) for n in WEIGHT_NAMES}

    delta, new_m, new_v = {}, {}, {}
    for n in WEIGHT_NAMES:
        shape2d = (-1, wts[n].shape[-1])
        d_, m_, v_ = adamw(wts[n].reshape(shape2d), grads[n].reshape(shape2d), ms[n].reshape(shape2d),
                           vs[n].reshape(shape2d), "adamw_" + n)
        delta[n], new_m[n], new_v[n] = [t.reshape(wts[n].shape) for t in (d_, m_, v_)]

    return (loss, grad_x[None], *[grads[n] for n in WEIGHT_NAMES], *[delta[n] for n in WEIGHT_NAMES],
            *[new_m[n] for n in WEIGHT_NAMES], *[new_v[n] for n in WEIGHT_NAMES])
```
